```python
import jax
import jax.numpy as jnp
from jax import lax
import numpy as np

D_MODEL = 1024
BATCH = 8
SEQ = 2048
DEPTH = 1
DEC_BATCH = 32
DEC_SEQ = 8
PAST_LEN = 8192
PAGE_SIZE = 128

N_HEADS = 8
HEAD_DIM = 64
N_KV_HEADS = 2
Q_PER_KV = N_HEADS // N_KV_HEADS
ATTN_WIDTH = N_HEADS * HEAD_DIM
CMP_BLOCK = 64
TOP_K_BLOCKS = 16
WINDOW = 512
WIN_QBLK = 128
SLC_QBLK = 32
FORCE_SCORE = 1.0e4
POOL_WIDTH = D_MODEL - ATTN_WIDTH
POOL_WINDOWS = (2, 4, 8, 16)
N_POOL_GROUPS = len(POOL_WINDOWS)
POOL_GROUP_WIDTH = POOL_WIDTH // N_POOL_GROUPS
POOL_BUF = max(POOL_WINDOWS) - 1
KV_COLS = 2 * N_KV_HEADS * HEAD_DIM
GATE_COLS = 3 * N_HEADS
IN_COLS = ATTN_WIDTH + 3 * KV_COLS + GATE_COLS + POOL_WIDTH
N_EXPERT_GROUPS = 4
EXPERTS_PER_GROUP = 4
N_EXPERTS = N_EXPERT_GROUPS * EXPERTS_PER_GROUP
D_EXPERT = 512
TOP_K_EXPERTS = 2
EPS = 1e-6
NEG = -1e30

kernel_name = 'hymba_nsa_pool_hmoe_step'


def rmsnorm(x, g):
    xf = x.astype(jnp.float32)
    y = xf * lax.rsqrt(jnp.mean(xf * xf, axis=-1, keepdims=True) + EPS)
    return (y * g.astype(jnp.float32)).astype(x.dtype)


def alibi_slopes():
    s = 2.0 ** (-8.0 * np.arange(1, N_HEADS + 1) / N_HEADS)
    return jnp.asarray(s, dtype=jnp.float32).reshape(N_KV_HEADS, Q_PER_KV)


def masked_softmax(logits, mask):
    z = jnp.where(mask, logits.astype(jnp.float32), NEG)
    m = jnp.max(z, axis=-1, keepdims=True)
    e = jnp.where(mask, jnp.exp(z - m), 0.0)
    den = jnp.sum(e, axis=-1, keepdims=True)
    return e / jnp.where(den > 0, den, 1.0)


def norm_keys(kv, g):
    return jnp.concatenate([rmsnorm(kv[:, :, :1], g), kv[:, :, 1:]], axis=2)


def in_projection(x, norm1_g, w_in, q_norm_g, k_norm_slc_g, k_norm_win_g):
    b, t, _ = x.shape
    p = jnp.einsum('btd,dc->btc', rmsnorm(x, norm1_g), w_in)
    o = [0]
    for w in (ATTN_WIDTH, KV_COLS, KV_COLS, KV_COLS, GATE_COLS, POOL_WIDTH):
        o.append(o[-1] + w)
    kv_shape = (b, t, 2, N_KV_HEADS, HEAD_DIM)
    q = rmsnorm(p[..., o[0]:o[1]].reshape(b, t, N_HEADS, HEAD_DIM), q_norm_g)
    kv_cmp = p[..., o[1]:o[2]].reshape(kv_shape)
    kv_slc = norm_keys(p[..., o[2]:o[3]].reshape(kv_shape), k_norm_slc_g)
    kv_win = norm_keys(p[..., o[3]:o[4]].reshape(kv_shape), k_norm_win_g)
    gates = jax.nn.sigmoid(p[..., o[4]:o[5]].astype(jnp.float32)).reshape(b, t, N_HEADS, 3)
    u_pool = p[..., o[5]:o[6]]
    return q, kv_cmp, kv_slc, kv_win, gates, u_pool


def pad_to_blocks(kv):
    b, t = kv.shape[:2]
    nb = -(-t // CMP_BLOCK)
    kv = jnp.pad(kv, ((0, 0), (0, nb * CMP_BLOCK - t), (0, 0), (0, 0), (0, 0)))
    return kv.reshape(b, nb, CMP_BLOCK, 2, N_KV_HEADS, HEAD_DIM)


def compress_blocks(kv_cmp, cmp_pos_emb, w_cmp_k, w_cmp_v, k_norm_cmp_g):
    blk = pad_to_blocks(kv_cmp) + cmp_pos_emb[None, None, :, None, None, :]
    kc = jnp.einsum('bnlkd,lde->bnke', blk[:, :, :, 0], w_cmp_k)
    vc = jnp.einsum('bnlkd,lde->bnke', blk[:, :, :, 1], w_cmp_v)
    return rmsnorm(kc, k_norm_cmp_g), vc


def compressed_attention(q, q_pos, kc, vc, slopes):
    b, tq = q.shape[:2]
    nb = kc.shape[1]
    qg = q.reshape(b, tq, N_KV_HEADS, Q_PER_KV, HEAD_DIM)
    s = jnp.einsum('bqkgd,bnkd->bkgqn', qg, kc).astype(jnp.float32) * (HEAD_DIM ** -0.5)
    block_end = jnp.arange(nb, dtype=jnp.int32) * CMP_BLOCK + (CMP_BLOCK - 1)
    dist_i = q_pos[:, None] - block_end[None, :]
    mask = dist_i >= 0
    p = masked_softmax(s - slopes[:, :, None, None] * dist_i.astype(jnp.float32), mask)
    o = jnp.einsum('bkgqn,bnkd->bqkgd', p.astype(vc.dtype), vc)
    return o.reshape(b, tq, N_HEADS, HEAD_DIM), p.sum(axis=2)


def select_blocks(p_kv, q_pos, nb):
    j = jnp.arange(nb, dtype=jnp.int32)[None, :]
    qp = q_pos[:, None]
    cand = j * CMP_BLOCK <= qp
    forced = (j == qp // CMP_BLOCK) | (j == 0)
    score = jnp.where(forced, FORCE_SCORE, jnp.where(cand, p_kv, -1.0))
    vals, idx = lax.top_k(score, min(TOP_K_BLOCKS, nb))
    return idx, vals > -0.5


def gather_rows(table, idx):
    return table[idx]


def selected_attention(q, q_pos, kv_blocks, idx, valid, slopes):
    b, tq = q.shape[:2]
    n_sel = idx.shape[-1]
    sel = jax.vmap(jax.vmap(gather_rows))(kv_blocks, idx.reshape(b, N_KV_HEADS, tq * n_sel))
    sel = sel.reshape(b, N_KV_HEADS, tq, n_sel * CMP_BLOCK, 2, HEAD_DIM)
    tok_pos = (idx[..., None] * CMP_BLOCK + jnp.arange(CMP_BLOCK, dtype=jnp.int32)).reshape(
        b, N_KV_HEADS, tq, n_sel * CMP_BLOCK)
    mask = jnp.repeat(valid, CMP_BLOCK, axis=-1) & (tok_pos <= q_pos[:, None])
    dist = (q_pos[:, None] - tok_pos).astype(jnp.float32)
    qg = q.reshape(b, tq, N_KV_HEADS, Q_PER_KV, HEAD_DIM)
    s = jnp.einsum('bqkgd,bkqmd->bkgqm', qg, sel[..., 0, :]).astype(jnp.float32) * (HEAD_DIM ** -0.5)
    p = masked_softmax(s - slopes[None, :, :, None, None] * dist[:, :, None], mask[:, :, None])
    o = jnp.einsum('bkgqm,bkqmd->bqkgd', p.astype(sel.dtype), sel[..., 1, :])
    return o.reshape(b, tq, N_HEADS, HEAD_DIM)


def selected_branch(q, q_pos, kv_slc, idx, valid, slopes):
    b, tq = q.shape[:2]
    blk = pad_to_blocks(kv_slc)
    nb = blk.shape[1]
    kv_blocks = blk.transpose(0, 4, 1, 2, 3, 5).reshape(b, N_KV_HEADS, nb, CMP_BLOCK * 2 * HEAD_DIM)
    if tq > SLC_QBLK and tq % SLC_QBLK == 0:
        n_q = tq // SLC_QBLK
        n_sel = idx.shape[-1]
        qb = q.reshape(b, n_q, SLC_QBLK, N_HEADS, HEAD_DIM).swapaxes(0, 1)
        pb = q_pos.reshape(n_q, SLC_QBLK)
        ib = idx.reshape(b, N_KV_HEADS, n_q, SLC_QBLK, n_sel).transpose(2, 0, 1, 3, 4)
        vb = valid.reshape(b, N_KV_HEADS, n_q, SLC_QBLK, n_sel).transpose(2, 0, 1, 3, 4)
        o = lax.map(lambda a: selected_attention(a[0], a[1], kv_blocks, a[2], a[3], slopes), (qb, pb, ib, vb))
        return o.swapaxes(0, 1).reshape(b, tq, N_HEADS, HEAD_DIM)
    return selected_attention(q, q_pos, kv_blocks, idx, valid, slopes)


def compressed_and_selected(q, q_pos, kv_cmp, kv_slc, cmp_pos_emb, w_cmp_k, w_cmp_v, k_norm_cmp_g, slopes):
    kc, vc = compress_blocks(kv_cmp, cmp_pos_emb, w_cmp_k, w_cmp_v, k_norm_cmp_g)
    o_cmp, p_kv = compressed_attention(q, q_pos, kc, vc, slopes)
    idx, valid = select_blocks(p_kv, q_pos, kc.shape[1])
    o_slc = selected_branch(q, q_pos, kv_slc, idx, valid, slopes)
    return o_cmp, o_slc


def window_attention(q, q_pos, kv, k_pos, slopes):
    b, tq = q.shape[:2]
    qg = q.reshape(b, tq, N_KV_HEADS, Q_PER_KV, HEAD_DIM)
    s = jnp.einsum('bqkgd,bskd->bkgqs', qg, kv[:, :, 0]).astype(jnp.float32) * (HEAD_DIM ** -0.5)
    dist_i = q_pos[:, None] - k_pos[None, :]
    mask = (dist_i >= 0) & (dist_i < WINDOW) & (k_pos[None, :] >= 0)
    p = masked_softmax(s - slopes[:, :, None, None] * dist_i.astype(jnp.float32), mask)
    o = jnp.einsum('bkgqs,bskd->bqkgd', p.astype(kv.dtype), kv[:, :, 1])
    return o.reshape(b, tq, N_HEADS, HEAD_DIM)


def window_attention_prompt(q, kv, slopes):
    b, s = q.shape[:2]
    n_q = s // WIN_QBLK
    kv_pad = jnp.pad(kv, ((0, 0), (WINDOW, 0), (0, 0), (0, 0), (0, 0)))
    qb = q.reshape(b, n_q, WIN_QBLK, N_HEADS, HEAD_DIM).swapaxes(0, 1)

    def one_block(args):
        i, qi = args
        start = i * WIN_QBLK
        kvi = lax.dynamic_slice_in_dim(kv_pad, start, WIN_QBLK + WINDOW, axis=1)
        q_pos = start + jnp.arange(WIN_QBLK, dtype=jnp.int32)
        k_pos = start - WINDOW + jnp.arange(WIN_QBLK + WINDOW, dtype=jnp.int32)
        return window_attention(qi, q_pos, kvi, k_pos, slopes)

    o = lax.map(one_block, (jnp.arange(n_q, dtype=jnp.int32), qb))
    return o.swapaxes(0, 1).reshape(b, s, N_HEADS, HEAD_DIM)


def gate_merge(o_cmp, o_slc, o_win, gates):
    o = gates[..., 0:1] * o_cmp + gates[..., 1:2] * o_slc + gates[..., 2:3] * o_win
    b, t = o.shape[:2]
    return o.reshape(b, t, ATTN_WIDTH).astype(o_cmp.dtype)


def pool_mix(u_ext, n_new, pool_w, pool_scale):
    b, n, _ = u_ext.shape
    uf = u_ext.astype(jnp.float32)
    cs = jnp.concatenate([jnp.zeros((b, 1, POOL_WIDTH), jnp.float32), jnp.cumsum(uf, axis=1)], axis=1)
    rows = np.arange(n - n_new, n)
    outs = []
    for gi, w in enumerate(POOL_WINDOWS):
        lo = np.maximum(rows + 1 - w, 0)
        cnt = jnp.asarray((rows + 1 - lo).astype(np.float32))
        sl = slice(gi * POOL_GROUP_WIDTH, (gi + 1) * POOL_GROUP_WIDTH)
        win_sum = cs[:, rows + 1, sl] - cs[:, lo, sl]
        outs.append(win_sum / cnt[None, :, None] - uf[:, n - n_new:, sl])
    d = jnp.stack(outs, axis=2)
    y = jnp.einsum('bngc,gce->bnge', d, pool_w.astype(jnp.float32)).reshape(b, n_new, POOL_WIDTH)
    return (y * pool_scale.astype(jnp.float32)).astype(u_ext.dtype)


def hier_moe(h, w_router_group, w_router_expert, w_gate, w_up, w_down):
    b, t, d = h.shape
    x = h.reshape(b * t, d)
    pg = jax.nn.softmax(jnp.einsum('nd,dg->ng', x, w_router_group).astype(jnp.float32), axis=-1)
    g_val, g_idx = lax.top_k(pg, 1)
    le = jnp.einsum('nd,de->ne', x, w_router_expert).astype(jnp.float32).reshape(
        -1, N_EXPERT_GROUPS, EXPERTS_PER_GROUP)
    le = jnp.einsum('nge,ng->ne', le, jax.nn.one_hot(g_idx[:, 0], N_EXPERT_GROUPS, dtype=jnp.float32))
    e_val, e_idx = lax.top_k(jax.nn.softmax(le, axis=-1), TOP_K_EXPERTS)
    gate = g_val * (e_val / jnp.sum(e_val, axis=-1, keepdims=True))
    expert_id = g_idx * EXPERTS_PER_GROUP + e_idx
    dense_gate = jnp.sum(jax.nn.one_hot(expert_id, N_EXPERTS, dtype=jnp.float32) * gate[..., None], axis=1)
    y = jnp.zeros((b * t, d), jnp.float32)
    for e in range(N_EXPERTS):
        he = jax.nn.silu(x @ w_gate[e]) * (x @ w_up[e])
        y = y + dense_gate[:, e:e + 1] * (he @ w_down[e]).astype(jnp.float32)
    return y.reshape(b, t, d).astype(h.dtype)


def merge_and_ffn(x, a, m, w_out, norm2_g, w_router_group, w_router_expert, w_gate, w_up, w_down):
    h = x + jnp.einsum('btc,cd->btd', jnp.concatenate([a, m], axis=-1), w_out)
    return h + hier_moe(rmsnorm(h, norm2_g), w_router_group, w_router_expert, w_gate, w_up, w_down)


def gather_pages(cache, page_table):
    db, n_pages = page_table.shape
    rows = cache[page_table]
    return rows.reshape(db, n_pages * cache.shape[1], *cache.shape[2:])


def setup_inputs(seed: int = 0) -> dict:
    key = jax.random.key(seed)
    ks = jax.random.split(key, 24)
    n_pages = PAST_LEN // PAGE_SIZE
    n_used = DEC_BATCH * n_pages
    n_pool = n_used + max(1, n_used // 4)
    win_buf = min(WINDOW, PAST_LEN)

    def nrm(k, shape, s):
        return s * jax.random.normal(k, shape, jnp.float32)

    page_table = jax.random.permutation(ks[0], n_pool)[:n_used].reshape(DEC_BATCH, n_pages).astype(jnp.int32)
    kv_shape = (n_pool, PAGE_SIZE, 2, N_KV_HEADS, HEAD_DIM)
    return {
        'x_prompt': nrm(ks[1], (BATCH, SEQ, D_MODEL), 1.0),
        'x_sample': nrm(ks[2], (DEC_BATCH, DEC_SEQ, D_MODEL), 1.0),
        'cache_cmp_kv': nrm(ks[3], kv_shape, 1.0),
        'cache_slc_kv': nrm(ks[4], kv_shape, 1.0),
        'state_win_kv': nrm(ks[5], (DEC_BATCH, win_buf, 2, N_KV_HEADS, HEAD_DIM), 1.0),
        'state_pool': nrm(ks[6], (DEC_BATCH, POOL_BUF, POOL_WIDTH), 1.0),
        'page_table': page_table,
        'norm1_g': 1.0 + nrm(ks[7], (D_MODEL,), 0.02),
        'w_in': nrm(ks[8], (D_MODEL, IN_COLS), D_MODEL ** -0.5),
        'q_norm_g': 1.0 + nrm(ks[9], (HEAD_DIM,), 0.02),
        'k_norm_cmp_g': 1.0 + nrm(ks[10], (HEAD_DIM,), 0.02),
        'k_norm_slc_g': 1.0 + nrm(ks[11], (HEAD_DIM,), 0.02),
        'k_norm_win_g': 1.0 + nrm(ks[12], (HEAD_DIM,), 0.02),
        'cmp_pos_emb': nrm(ks[13], (CMP_BLOCK, HEAD_DIM), 0.1),
        'w_cmp_k': nrm(ks[14], (CMP_BLOCK, HEAD_DIM, HEAD_DIM), (CMP_BLOCK * HEAD_DIM) ** -0.5),
        'w_cmp_v': nrm(ks[15], (CMP_BLOCK, HEAD_DIM, HEAD_DIM), (CMP_BLOCK * HEAD_DIM) ** -0.5),
        'pool_w': nrm(ks[16], (N_POOL_GROUPS, POOL_GROUP_WIDTH, POOL_GROUP_WIDTH), POOL_GROUP_WIDTH ** -0.5),
        'pool_scale': 1.0 + nrm(ks[17], (POOL_WIDTH,), 0.1),
        'w_out': nrm(ks[18], (D_MODEL, D_MODEL), D_MODEL ** -0.5),
        'norm2_g': 1.0 + nrm(ks[19], (D_MODEL,), 0.02),
        'w_router_group': nrm(ks[20], (D_MODEL, N_EXPERT_GROUPS), D_MODEL ** -0.5),
        'w_router_expert': nrm(ks[21], (D_MODEL, N_EXPERTS), D_MODEL ** -0.5),
        'w_gate': nrm(ks[22], (N_EXPERTS, D_MODEL, D_EXPERT), D_MODEL ** -0.5),
        'w_up': nrm(jax.random.fold_in(ks[23], 1), (N_EXPERTS, D_MODEL, D_EXPERT), D_MODEL ** -0.5),
        'w_down': nrm(jax.random.fold_in(ks[23], 2), (N_EXPERTS, D_EXPERT, D_MODEL), D_EXPERT ** -0.5),
    }


def reference(x_prompt, x_sample, cache_cmp_kv, cache_slc_kv, state_win_kv, state_pool, page_table,
              norm1_g, w_in, q_norm_g, k_norm_cmp_g, k_norm_slc_g, k_norm_win_g, cmp_pos_emb, w_cmp_k,
              w_cmp_v, pool_w, pool_scale, w_out, norm2_g, w_router_group, w_router_expert, w_gate,
              w_up, w_down):
    slopes = alibi_slopes()

    s_len = x_prompt.shape[1]
    q_p, kvc_p, kvs_p, kvw_p, g_p, u_p = in_projection(
        x_prompt, norm1_g, w_in, q_norm_g, k_norm_slc_g, k_norm_win_g)
    pos_p = jnp.arange(s_len, dtype=jnp.int32)
    o_cmp_p, o_slc_p = compressed_and_selected(
        q_p, pos_p, kvc_p, kvs_p, cmp_pos_emb, w_cmp_k, w_cmp_v, k_norm_cmp_g, slopes)
    o_win_p = window_attention_prompt(q_p, kvw_p, slopes)
    a_p = gate_merge(o_cmp_p, o_slc_p, o_win_p, g_p)
    m_p = pool_mix(u_p, s_len, pool_w, pool_scale)
    y_prompt = merge_and_ffn(x_prompt, a_p, m_p, w_out, norm2_g, w_router_group, w_router_expert,
                             w_gate, w_up, w_down)

    ds = x_sample.shape[1]
    past = page_table.shape[1] * cache_cmp_kv.shape[1]
    q_s, kvc_s, kvs_s, kvw_s, g_s, u_s = in_projection(
        x_sample, norm1_g, w_in, q_norm_g, k_norm_slc_g, k_norm_win_g)
    pos_s = past + jnp.arange(ds, dtype=jnp.int32)
    full_cmp = jnp.concatenate([gather_pages(cache_cmp_kv, page_table), kvc_s], axis=1)
    full_slc = jnp.concatenate([gather_pages(cache_slc_kv, page_table), kvs_s], axis=1)
    o_cmp_s, o_slc_s = compressed_and_selected(
        q_s, pos_s, full_cmp, full_slc, cmp_pos_emb, w_cmp_k, w_cmp_v, k_norm_cmp_g, slopes)
    win_buf = state_win_kv.shape[1]
    win_ctx = jnp.concatenate([state_win_kv, kvw_s], axis=1)
    k_pos_s = past - win_buf + jnp.arange(win_buf + ds, dtype=jnp.int32)
    o_win_s = window_attention(q_s, pos_s, win_ctx, k_pos_s, slopes)
    a_s = gate_merge(o_cmp_s, o_slc_s, o_win_s, g_s)
    pool_ext = jnp.concatenate([state_pool, u_s], axis=1)
    m_s = pool_mix(pool_ext, ds, pool_w, pool_scale)
    y_sample = merge_and_ffn(x_sample, a_s, m_s, w_out, norm2_g, w_router_group, w_router_expert,
                             w_gate, w_up, w_down)

    win_keep = min(WINDOW, s_len)
    return (y_prompt, y_sample, kvc_p, kvc_s, kvs_p, kvs_s, kvw_p[:, s_len - win_keep:], win_ctx[:, ds:],
            u_p[:, s_len - POOL_BUF:], pool_ext[:, ds:])
```

```python
import functools

import jax
import jax.numpy as jnp
import numpy as np
from jax import lax
from jax.experimental import pallas as pl
from jax.experimental.pallas import tpu as pltpu

F32 = jnp.float32
BF16 = jnp.bfloat16

D_MODEL = 1024
N_HEADS = 8
HEAD_DIM = 64
N_KV_HEADS = 2
Q_PER_KV = N_HEADS // N_KV_HEADS
ATTN_WIDTH = N_HEADS * HEAD_DIM
KV_COLS = 2 * N_KV_HEADS * HEAD_DIM
GATE_COLS = 3 * N_HEADS
POOL_WIDTH = D_MODEL - ATTN_WIDTH
POOL_WINDOWS = (2, 4, 8, 16)
POOL_GROUP_WIDTH = POOL_WIDTH // len(POOL_WINDOWS)
POOL_BUF = max(POOL_WINDOWS) - 1
CMP_BLOCK = 64
TOP_K_BLOCKS = 16
WINDOW = 512
FORCE_SCORE = 1.0e4
N_EXPERT_GROUPS = 4
EXPERTS_PER_GROUP = 4
N_EXPERTS = N_EXPERT_GROUPS * EXPERTS_PER_GROUP
D_EXPERT = 512
EPS = 1e-6
NEG = -1e30
SCALE = HEAD_DIM ** -0.5

LANES = 128
GATE_PAD = LANES
PACKED_COLS = ATTN_WIDTH + 3 * KV_COLS + GATE_PAD + POOL_WIDTH
OFF_KVC = ATTN_WIDTH
OFF_KVS = OFF_KVC + KV_COLS
OFF_KVW = OFF_KVS + KV_COLS
OFF_GATE = OFF_KVW + KV_COLS
OFF_U = OFF_GATE + GATE_PAD
ROUTER_LANE0 = N_EXPERT_GROUPS
Q_TILE = 128
K_CHUNK = 256
VMEM_LIMIT = 56 * 1024 * 1024


def _slope(h):
    return float(2.0 ** (-8.0 * (h + 1) / N_HEADS))


def _half_group_norm(v, g):
    lane = lax.broadcasted_iota(jnp.int32, v.shape, 1)
    lo = lane < HEAD_DIM
    v2 = v * v
    s_lo = jnp.sum(jnp.where(lo, v2, 0.0), axis=-1, keepdims=True)
    s_hi = jnp.sum(jnp.where(lo, 0.0, v2), axis=-1, keepdims=True)
    r_lo = lax.rsqrt(s_lo * (1.0 / HEAD_DIM) + EPS)
    r_hi = lax.rsqrt(s_hi * (1.0 / HEAD_DIM) + EPS)
    return (v * jnp.where(lo, r_lo, r_hi)) * g


def _proj_kernel(x_ref, g1_ref, w_ref, qg_ref, ksg_ref, kwg_ref,
                 q_ref, kvc_ref, kvs_ref, kvw_ref, kvs16_ref, kvw16_ref, gates_ref, u_ref):
    x = x_ref[...]
    ms = jnp.mean(x * x, axis=-1, keepdims=True)
    n = (x * lax.rsqrt(ms + EPS)) * g1_ref[...]
    p = jnp.dot(n.astype(BF16), w_ref[...], preferred_element_type=F32)
    for t in range(ATTN_WIDTH // LANES):
        sl = slice(t * LANES, (t + 1) * LANES)
        q_ref[:, sl] = _half_group_norm(p[:, sl], qg_ref[...]).astype(BF16)
    kvc_ref[...] = p[:, OFF_KVC:OFF_KVC + KV_COLS]
    ks = _half_group_norm(p[:, OFF_KVS:OFF_KVS + LANES], ksg_ref[...])
    vs = p[:, OFF_KVS + LANES:OFF_KVS + KV_COLS]
    kvs_ref[:, :LANES] = ks
    kvs_ref[:, LANES:] = vs
    kvs16_ref[:, :LANES] = ks.astype(BF16)
    kvs16_ref[:, LANES:] = vs.astype(BF16)
    kw = _half_group_norm(p[:, OFF_KVW:OFF_KVW + LANES], kwg_ref[...])
    vw = p[:, OFF_KVW + LANES:OFF_KVW + KV_COLS]
    kvw_ref[:, :LANES] = kw
    kvw_ref[:, LANES:] = vw
    kvw16_ref[:, :LANES] = kw.astype(BF16)
    kvw16_ref[:, LANES:] = vw.astype(BF16)
    gl = p[:, OFF_GATE:OFF_GATE + GATE_PAD]
    sig = 1.0 / (1.0 + jnp.exp(-gl))
    gates_ref[...] = sig[:, :GATE_COLS]
    u_ref[...] = p[:, OFF_U:OFF_U + POOL_WIDTH]


def _proj(x2d, g1, w_packed, qg, ksg, kwg, tm):
    t = x2d.shape[0]
    row = lambda i: (i, 0)
    const = lambda i: (0, 0)
    out_shape = (
        jax.ShapeDtypeStruct((t, ATTN_WIDTH), BF16),
        jax.ShapeDtypeStruct((t, KV_COLS), F32),
        jax.ShapeDtypeStruct((t, KV_COLS), F32),
        jax.ShapeDtypeStruct((t, KV_COLS), F32),
        jax.ShapeDtypeStruct((t, KV_COLS), BF16),
        jax.ShapeDtypeStruct((t, KV_COLS), BF16),
        jax.ShapeDtypeStruct((t, GATE_COLS), F32),
        jax.ShapeDtypeStruct((t, POOL_WIDTH), F32),
    )
    out_specs = (
        pl.BlockSpec((tm, ATTN_WIDTH), row),
        pl.BlockSpec((tm, KV_COLS), row),
        pl.BlockSpec((tm, KV_COLS), row),
        pl.BlockSpec((tm, KV_COLS), row),
        pl.BlockSpec((tm, KV_COLS), row),
        pl.BlockSpec((tm, KV_COLS), row),
        pl.BlockSpec((tm, GATE_COLS), row),
        pl.BlockSpec((tm, POOL_WIDTH), row),
    )
    return pl.pallas_call(
        _proj_kernel,
        grid=(t // tm,),
        in_specs=[
            pl.BlockSpec((tm, D_MODEL), row),
            pl.BlockSpec((1, D_MODEL), const),
            pl.BlockSpec((D_MODEL, PACKED_COLS), const),
            pl.BlockSpec((1, LANES), const),
            pl.BlockSpec((1, LANES), const),
            pl.BlockSpec((1, LANES), const),
        ],
        out_specs=out_specs,
        out_shape=out_shape,
        compiler_params=pltpu.CompilerParams(
            dimension_semantics=("arbitrary",), vmem_limit_bytes=VMEM_LIMIT),
        name="proj",
    )(x2d, g1, w_packed, qg, ksg, kwg)


def _pool_kernel(u_ref, pw_ref, ps_ref, m_ref):
    u = u_ref[0]
    n = u.shape[0]

    def shift(v, k):
        rolled = pltpu.roll(v, k, axis=0)
        r = lax.broadcasted_iota(jnp.int32, v.shape, 0)
        return jnp.where(r >= k, rolled, 0.0)

    sums = []
    s = u
    k = 1
    for gi in range(len(POOL_WINDOWS)):
        s = s + shift(s, k)
        k *= 2
        sums.append(s[:, :POOL_GROUP_WIDTH])
        s = s[:, POOL_GROUP_WIDTH:]
    row = lax.broadcasted_iota(jnp.int32, (n, 1), 0)
    for gi, w in enumerate(POOL_WINDOWS):
        sl = slice(gi * POOL_GROUP_WIDTH, (gi + 1) * POOL_GROUP_WIDTH)
        cnt = jnp.minimum(row + 1, w).astype(F32)
        d = sums[gi] / cnt - u[:, sl]
        y = jnp.dot(d.astype(BF16), pw_ref[gi], preferred_element_type=F32)
        m_ref[0, :, sl] = (y * ps_ref[:, sl]).astype(BF16)


def _pool(u3d, pool_w16, pool_scale):
    b, n, _ = u3d.shape
    return pl.pallas_call(
        _pool_kernel,
        grid=(b,),
        in_specs=[
            pl.BlockSpec((1, n, POOL_WIDTH), lambda i: (i, 0, 0)),
            pl.BlockSpec((len(POOL_WINDOWS), POOL_GROUP_WIDTH, POOL_GROUP_WIDTH), lambda i: (0, 0, 0)),
            pl.BlockSpec((1, POOL_WIDTH), lambda i: (0, 0)),
        ],
        out_specs=pl.BlockSpec((1, n, POOL_WIDTH), lambda i: (i, 0, 0)),
        out_shape=jax.ShapeDtypeStruct((b, n, POOL_WIDTH), BF16),
        compiler_params=pltpu.CompilerParams(
            dimension_semantics=("arbitrary",), vmem_limit_bytes=VMEM_LIMIT),
        name="pool",
    )(u3d, pool_w16, pool_scale)


def _compress_kernel(pt_ref, src_ref, tail_ref, pos_ref, wbd_ref, g_ref, out_ref, buf, sem,
                     *, n_pages, page_rows, tail_rows, nblk_pad):
    b = pl.program_id(0)
    nb = pl.num_programs(0)
    blk_per_page = page_rows // CMP_BLOCK
    n_data = n_pages * blk_per_page

    def block_copy(row, n, slot):
        page = pt_ref[row, n // blk_per_page]
        off = (n % blk_per_page) * CMP_BLOCK
        return pltpu.make_async_copy(
            src_ref.at[page, pl.ds(off, CMP_BLOCK), :], buf.at[slot, :, n, :], sem.at[slot])

    def start_row(row, slot):
        def body(n, c):
            block_copy(row, n, slot).start()
            return c
        lax.fori_loop(0, n_data, body, 0)

    def wait_row(row, slot):
        def body(n, c):
            block_copy(row, n, slot).wait()
            return c
        lax.fori_loop(0, n_data, body, 0)

    @pl.when(b == 0)
    def _():
        if nblk_pad > n_data:
            buf[:, :, n_data:, :] = jnp.zeros((2, CMP_BLOCK, nblk_pad - n_data, KV_COLS), F32)
        start_row(0, 0)

    slot = b % 2

    @pl.when(b + 1 < nb)
    def _():
        start_row(b + 1, 1 - slot)

    for l in range(tail_rows):
        buf[slot, l, pl.ds(n_data, 1), :] = tail_ref[0, pl.ds(l, 1), :]

    wait_row(b, slot)

    def body(l, acc):
        xl = buf[slot, l] + pos_ref[pl.ds(l, 1), :]
        return acc + jnp.dot(xl.astype(BF16), wbd_ref[l], preferred_element_type=F32)

    acc = lax.fori_loop(0, CMP_BLOCK, body, jnp.zeros((nblk_pad, KV_COLS), F32))
    out_ref[0, :, :LANES] = _half_group_norm(acc[:, :LANES], g_ref[...])
    out_ref[0, :, LANES:] = acc[:, LANES:]


def _compress(page_table, src_pages, tail, pos4, wbd, g2, *, nblk_pad, tail_rows):
    b, n_pages = page_table.shape
    page_rows = src_pages.shape[1]
    kern = functools.partial(_compress_kernel, n_pages=n_pages, page_rows=page_rows,
                             tail_rows=tail_rows, nblk_pad=nblk_pad)
    grid_spec = pltpu.PrefetchScalarGridSpec(
        num_scalar_prefetch=1,
        grid=(b,),
        in_specs=[
            pl.BlockSpec(memory_space=pl.ANY),
            pl.BlockSpec((1, tail.shape[1], KV_COLS), lambda i, pt: (i, 0, 0)),
            pl.BlockSpec((CMP_BLOCK, KV_COLS), lambda i, pt: (0, 0)),
            pl.BlockSpec((CMP_BLOCK, KV_COLS, KV_COLS), lambda i, pt: (0, 0, 0)),
            pl.BlockSpec((1, LANES), lambda i, pt: (0, 0)),
        ],
        out_specs=pl.BlockSpec((1, nblk_pad, KV_COLS), lambda i, pt: (i, 0, 0)),
        scratch_shapes=[
            pltpu.VMEM((2, CMP_BLOCK, nblk_pad, KV_COLS), F32),
            pltpu.SemaphoreType.DMA((2,)),
        ],
    )
    return pl.pallas_call(
        kern,
        grid_spec=grid_spec,
        out_shape=jax.ShapeDtypeStruct((b, nblk_pad, KV_COLS), F32),
        compiler_params=pltpu.CompilerParams(
            dimension_semantics=("arbitrary",), vmem_limit_bytes=VMEM_LIMIT),
        name="compress",
    )(page_table, src_pages, tail, pos4, wbd, g2)


def _place(tile, kvh, odd):
    lane = lax.broadcasted_iota(jnp.int32, tile.shape, 1)
    src = tile if odd == kvh else pltpu.roll(tile, HEAD_DIM, axis=1)
    keep = (lane < HEAD_DIM) if kvh == 0 else (lane >= HEAD_DIM)
    return jnp.where(keep, src, 0.0)


def _unplace(o_even, o_odd, kvh):
    lane = lax.broadcasted_iota(jnp.int32, o_even.shape, 1)
    if kvh == 0:
        return jnp.where(lane < HEAD_DIM, o_even, pltpu.roll(o_odd, HEAD_DIM, axis=1))
    return jnp.where(lane < HEAD_DIM, pltpu.roll(o_even, HEAD_DIM, axis=1), o_odd)


def _dot_t(a, b):
    return lax.dot_general(a, b, (((1,), (1,)), ((), ())), preferred_element_type=F32)


def _rank_select(score_t, n_rows):
    ridx = lax.broadcasted_iota(jnp.int32, score_t.shape, 0)
    rank = jnp.zeros(score_t.shape, jnp.int32)
    for jp in range(n_rows):
        row = score_t[jp:jp + 1, :]
        ahead = (row > score_t) | ((row == score_t) & (ridx > jp))
        rank = rank + ahead.astype(jnp.int32)
    sel = (rank < TOP_K_BLOCKS) & (score_t > -0.5)
    return sel.astype(F32)


def _attn_prompt_kernel(q_ref, gates_ref, kcvc_ref, kvs_ref, kvw_ref, a_ref, m_sc, l_sc, acc_sc, *, nb_real):
    i = pl.program_id(1)
    q0 = i * Q_TILE
    qpos = q0 + lax.broadcasted_iota(jnp.int32, (Q_TILE, 1), 0)
    gates = gates_ref[0]
    rows = Q_PER_KV * Q_TILE

    def flash(qk, kv_ref, c_lo, c_hi, mask_fn, slopes):
        m_sc[...] = jnp.full((rows, 1), NEG, F32)
        l_sc[...] = jnp.zeros((rows, 1), F32)
        acc_sc[...] = jnp.zeros((rows, LANES), F32)

        def body(c, carry):
            k0 = pl.multiple_of(c * K_CHUNK, K_CHUNK)
            k = kv_ref[0, pl.ds(k0, K_CHUNK), :LANES]
            v = kv_ref[0, pl.ds(k0, K_CHUNK), LANES:]
            s = _dot_t(qk, k)
            kpos = k0 + lax.broadcasted_iota(jnp.int32, (Q_TILE, K_CHUNK), 1)
            dist = qpos - kpos
            mask = mask_fn(c, dist)
            distf = dist.astype(F32)
            ps, alphas = [], []
            for g in range(Q_PER_KV):
                sl = slice(g * Q_TILE, (g + 1) * Q_TILE)
                z = jnp.where(mask, s[sl] * SCALE - slopes[g] * distf, NEG)
                m_old = m_sc[sl]
                m_new = jnp.maximum(m_old, jnp.max(z, axis=-1, keepdims=True))
                alpha = jnp.exp(m_old - m_new)
                p = jnp.where(mask, jnp.exp(z - m_new), 0.0)
                l_sc[sl] = alpha * l_sc[sl] + jnp.sum(p, axis=-1, keepdims=True)
                m_sc[sl] = m_new
                ps.append(p.astype(BF16))
                alphas.append(alpha)
            pv = jnp.dot(jnp.concatenate(ps, axis=0), v, preferred_element_type=F32)
            acc_sc[...] = jnp.concatenate(alphas, axis=0) * acc_sc[...] + pv
            return carry

        lax.fori_loop(c_lo, c_hi, body, 0)
        l = l_sc[...]
        return acc_sc[...] / jnp.where(l > 0.0, l, 1.0)

    for kvh in range(N_KV_HEADS):
        slopes = [_slope(kvh * Q_PER_KV + g) for g in range(Q_PER_KV)]
        pieces = []
        for g in range(Q_PER_KV):
            h = kvh * Q_PER_KV + g
            tile = q_ref[0, :, (h // 2) * LANES:(h // 2 + 1) * LANES].astype(F32)
            pieces.append(_place(tile, kvh, h % 2))
        qk = jnp.concatenate(pieces, axis=0).astype(BF16)

        kc = kcvc_ref[0, :, :LANES].astype(BF16)
        vc = kcvc_ref[0, :, LANES:].astype(BF16)
        s = _dot_t(qk, kc)
        blk = lax.broadcasted_iota(jnp.int32, (Q_TILE, LANES), 1)
        dist_c = qpos - (blk * CMP_BLOCK + (CMP_BLOCK - 1))
        mask_c = dist_c >= 0
        dist_cf = dist_c.astype(F32)
        p_list = []
        p_kv = jnp.zeros((Q_TILE, LANES), F32)
        for g in range(Q_PER_KV):
            z = jnp.where(mask_c, s[g * Q_TILE:(g + 1) * Q_TILE] * SCALE - slopes[g] * dist_cf, NEG)
            mx = jnp.max(z, axis=-1, keepdims=True)
            e = jnp.where(mask_c, jnp.exp(z - mx), 0.0)
            den = jnp.sum(e, axis=-1, keepdims=True)
            p = e / jnp.where(den > 0.0, den, 1.0)
            p_kv = p_kv + p
            p_list.append(p.astype(BF16))
        o_cmp = jnp.dot(jnp.concatenate(p_list, axis=0), vc, preferred_element_type=F32)

        cand = blk * CMP_BLOCK <= qpos
        forced = (blk == qpos // CMP_BLOCK) | (blk == 0)
        score = jnp.where(forced, FORCE_SCORE, jnp.where(cand, p_kv, -1.0))
        score_t = score.T
        sel_t = _rank_select(score_t[:nb_real], nb_real)
        if nb_real < LANES:
            sel_t = jnp.concatenate([sel_t, jnp.zeros((LANES - nb_real, Q_TILE), F32)], axis=0)
        sel = sel_t.T.astype(BF16)

        def slc_mask(c, dist):
            j = lax.broadcasted_iota(jnp.int32, (LANES, K_CHUNK), 0)
            t = c * K_CHUNK + lax.broadcasted_iota(jnp.int32, (LANES, K_CHUNK), 1)
            expand = (j == t // CMP_BLOCK).astype(BF16)
            chosen = jnp.dot(sel, expand, preferred_element_type=F32) > 0.5
            return chosen & (dist >= 0)

        def win_mask(c, dist):
            return (dist >= 0) & (dist < WINDOW)

        n_slc = (q0 + Q_TILE + K_CHUNK - 1) // K_CHUNK
        o_slc = flash(qk, kvs_ref, 0, n_slc, slc_mask, slopes)
        w_lo = jnp.maximum(q0 - WINDOW, 0) // K_CHUNK
        o_win = flash(qk, kvw_ref, w_lo, n_slc, win_mask, slopes)

        merged = []
        for g in range(Q_PER_KV):
            h = kvh * Q_PER_KV + g
            sl = slice(g * Q_TILE, (g + 1) * Q_TILE)
            merged.append(gates[:, 3 * h:3 * h + 1] * o_cmp[sl]
                          + gates[:, 3 * h + 1:3 * h + 2] * o_slc[sl]
                          + gates[:, 3 * h + 2:3 * h + 3] * o_win[sl])
        for pr in range(Q_PER_KV // 2):
            t = (kvh * Q_PER_KV) // 2 + pr
            a_ref[0, :, t * LANES:(t + 1) * LANES] = _unplace(merged[2 * pr], merged[2 * pr + 1], kvh).astype(BF16)


def _attn_prompt(q, gates, kcvc, kvs16, kvw16, nb_real):
    b, s, _ = q.shape
    rows = Q_PER_KV * Q_TILE
    kern = functools.partial(_attn_prompt_kernel, nb_real=nb_real)
    return pl.pallas_call(
        kern,
        grid=(b, s // Q_TILE),
        in_specs=[
            pl.BlockSpec((1, Q_TILE, ATTN_WIDTH), lambda bi, i: (bi, i, 0)),
            pl.BlockSpec((1, Q_TILE, GATE_COLS), lambda bi, i: (bi, i, 0)),
            pl.BlockSpec((1, LANES, KV_COLS), lambda bi, i: (bi, 0, 0)),
            pl.BlockSpec((1, s, KV_COLS), lambda bi, i: (bi, 0, 0)),
            pl.BlockSpec((1, s, KV_COLS), lambda bi, i: (bi, 0, 0)),
        ],
        out_specs=pl.BlockSpec((1, Q_TILE, ATTN_WIDTH), lambda bi, i: (bi, i, 0)),
        out_shape=jax.ShapeDtypeStruct((b, s, ATTN_WIDTH), BF16),
        scratch_shapes=[
            pltpu.VMEM((rows, 1), F32),
            pltpu.VMEM((rows, 1), F32),
            pltpu.VMEM((rows, LANES), F32),
        ],
        compiler_params=pltpu.CompilerParams(
            dimension_semantics=("arbitrary", "arbitrary"), vmem_limit_bytes=VMEM_LIMIT),
        name="attn_prompt",
    )(q, gates, kcvc, kvs16, kvw16)


S_ROWS = LANES
S_CHUNK = 1024


def _attn_sample_kernel(pt_ref, q_ref, gates_ref, kcvc_ref, kvs_new_ref, kvw_new_ref, win_ref, cache_ref,
                        a_ref, kbuf, zbuf, sem, *, n_pages, page_rows, past, ds, nb_real):
    b = pl.program_id(0)
    nb = pl.num_programs(0)
    n_keys = n_pages * page_rows
    n_real = N_HEADS * ds
    new_pad = 16

    def page_copy(row, p, slot):
        return pltpu.make_async_copy(
            cache_ref.at[pt_ref[row, p]], kbuf.at[slot, pl.ds(p * page_rows, page_rows), :], sem.at[slot])

    def start_row(row, slot):
        def body(p, c):
            page_copy(row, p, slot).start()
            return c
        lax.fori_loop(0, n_pages, body, 0)

    def wait_row(row, slot):
        def body(p, c):
            page_copy(row, p, slot).wait()
            return c
        lax.fori_loop(0, n_pages, body, 0)

    @pl.when(b == 0)
    def _():
        start_row(0, 0)

    slot = b % 2

    @pl.when(b + 1 < nb)
    def _():
        start_row(b + 1, 1 - slot)

    pieces = []
    for h in range(N_HEADS):
        tile = q_ref[0, :, (h // 2) * LANES:(h // 2 + 1) * LANES].astype(F32)
        pieces.append(_place(tile, h // Q_PER_KV, h % 2))
    pieces.append(jnp.zeros((S_ROWS - n_real, LANES), F32))
    qrows = jnp.concatenate(pieces, axis=0).astype(BF16)

    def per_row(shape, dim):
        r = lax.broadcasted_iota(jnp.int32, shape, dim)
        qp = past + r % ds
        hh = r // ds
        sl = jnp.zeros(shape, F32)
        for h in range(N_HEADS):
            sl = jnp.where(hh == h, _slope(h), sl)
        return qp, sl

    nblk = kcvc_ref.shape[1]
    kc = kcvc_ref[0, :, :LANES].astype(BF16)
    vc = kcvc_ref[0, :, LANES:].astype(BF16)
    s = _dot_t(qrows, kc)
    blk = lax.broadcasted_iota(jnp.int32, (S_ROWS, nblk), 1)
    qp_c, sl_c = per_row((S_ROWS, 1), 0)
    dist_c = qp_c - (blk * CMP_BLOCK + (CMP_BLOCK - 1))
    mask_c = dist_c >= 0
    z = jnp.where(mask_c, s * SCALE - sl_c * dist_c.astype(F32), NEG)
    mx = jnp.max(z, axis=-1, keepdims=True)
    e = jnp.where(mask_c, jnp.exp(z - mx), 0.0)
    den = jnp.sum(e, axis=-1, keepdims=True)
    p = e / jnp.where(den > 0.0, den, 1.0)
    o_cmp = jnp.dot(p.astype(BF16), vc, preferred_element_type=F32)

    kq = N_KV_HEADS * ds
    p_kv = []
    for kvh in range(N_KV_HEADS):
        acc = jnp.zeros((ds, nblk), F32)
        for g in range(Q_PER_KV):
            r0 = (kvh * Q_PER_KV + g) * ds
            acc = acc + p[r0:r0 + ds]
        p_kv.append(acc)
    p_kv = jnp.concatenate(p_kv, axis=0)
    blk2 = lax.broadcasted_iota(jnp.int32, (kq, nblk), 1)
    qp2 = past + lax.broadcasted_iota(jnp.int32, (kq, 1), 0) % ds
    cand = blk2 * CMP_BLOCK <= qp2
    forced = (blk2 == qp2 // CMP_BLOCK) | (blk2 == 0)
    score = jnp.where(forced, FORCE_SCORE, jnp.where(cand, p_kv, -1.0))
    rank = jnp.zeros((kq, nblk), jnp.int32)
    for jp in range(nb_real):
        col = score[:, jp:jp + 1]
        ahead = (col > score) | ((col == score) & (blk2 > jp))
        rank = rank + ahead.astype(jnp.int32)
    sel2 = ((rank < TOP_K_BLOCKS) & (score > -0.5)).astype(F32)
    sel_rows = []
    for h in range(N_HEADS):
        kvh = h // Q_PER_KV
        sel_rows.append(sel2[kvh * ds:(kvh + 1) * ds])
    sel_rows.append(jnp.zeros((S_ROWS - n_real, nblk), F32))
    sel_t = jnp.concatenate(sel_rows, axis=0).T.astype(BF16)

    qp_l, sl_l = per_row((1, S_ROWS), 1)
    q_l = lax.broadcasted_iota(jnp.int32, (1, S_ROWS), 1) % ds

    def new_rows(ref):
        kv = jnp.concatenate([ref[0].astype(F32), jnp.zeros((new_pad - ds, KV_COLS), F32)], axis=0).astype(BF16)
        return kv[:, :LANES], kv[:, LANES:]

    def new_scores(k_new, extra_mask):
        i = lax.broadcasted_iota(jnp.int32, (new_pad, 1), 0)
        s_new = _dot_t(k_new, qrows)
        dist = q_l - i
        mask = (i < ds) & (dist >= 0) & extra_mask
        return jnp.where(mask, s_new * SCALE - sl_l * dist.astype(F32), NEG)

    def finish(acc, den_l):
        den_t = jnp.broadcast_to(den_l, (S_ROWS, S_ROWS)).T
        return acc / jnp.where(den_t > 0.0, den_t, 1.0)

    def dot_c0(a, bmat):
        return lax.dot_general(a, bmat, (((0,), (0,)), ((), ())), preferred_element_type=F32)

    wait_row(b, slot)
    n_chunks = n_keys // S_CHUNK
    blk_per_chunk = S_CHUNK // CMP_BLOCK

    def pass1(c, mcol):
        k0 = pl.multiple_of(c * S_CHUNK, S_CHUNK)
        k = kbuf[slot, pl.ds(k0, S_CHUNK), :LANES].astype(BF16)
        st = _dot_t(k, qrows)
        j = lax.broadcasted_iota(jnp.int32, (S_CHUNK, nblk), 1)
        t = k0 + lax.broadcasted_iota(jnp.int32, (S_CHUNK, nblk), 0)
        expand = (j == t // CMP_BLOCK).astype(BF16)
        chosen = jnp.dot(expand, sel_t, preferred_element_type=F32) > 0.5
        tpos = k0 + lax.broadcasted_iota(jnp.int32, (S_CHUNK, 1), 0)
        dist = qp_l - tpos
        zc = jnp.where(chosen & (dist >= 0), st * SCALE - sl_l * dist.astype(F32), NEG)
        zbuf[pl.ds(k0, S_CHUNK), :] = zc
        return jnp.maximum(mcol, jnp.max(zc, axis=0, keepdims=True))

    mcol = lax.fori_loop(0, n_chunks, pass1, jnp.full((1, S_ROWS), NEG, F32))
    k_new, v_new = new_rows(kvs_new_ref)
    last_sel = sel_t[past // CMP_BLOCK:past // CMP_BLOCK + 1, :].astype(F32) > 0.5
    z_new = new_scores(k_new, last_sel)
    m_l = jnp.maximum(mcol, jnp.max(z_new, axis=0, keepdims=True))

    def pass2(c, carry):
        acc, den_l = carry
        k0 = pl.multiple_of(c * S_CHUNK, S_CHUNK)
        zc = zbuf[pl.ds(k0, S_CHUNK), :]
        ec = jnp.where(zc > 0.5 * NEG, jnp.exp(zc - m_l), 0.0)
        v = kbuf[slot, pl.ds(k0, S_CHUNK), LANES:].astype(BF16)
        return acc + dot_c0(ec.astype(BF16), v), den_l + jnp.sum(ec, axis=0, keepdims=True)

    e_new = jnp.where(z_new > 0.5 * NEG, jnp.exp(z_new - m_l), 0.0)
    acc0 = dot_c0(e_new.astype(BF16), v_new)
    den0 = jnp.sum(e_new, axis=0, keepdims=True)
    acc, den_l = lax.fori_loop(0, n_chunks, pass2, (acc0, den0))
    o_slc = finish(acc, den_l)

    win_buf = win_ref.shape[1]
    kw = win_ref[0, :, :LANES].astype(BF16)
    vw = win_ref[0, :, LANES:].astype(BF16)
    st = _dot_t(kw, qrows)
    kpos = past - win_buf + lax.broadcasted_iota(jnp.int32, (win_buf, 1), 0)
    dist = qp_l - kpos
    mask = (dist >= 0) & (dist < WINDOW) & (kpos >= 0)
    z_w = jnp.where(mask, st * SCALE - sl_l * dist.astype(F32), NEG)
    kw_new, vw_new = new_rows(kvw_new_ref)
    zw_new = new_scores(kw_new, True)
    m_w = jnp.maximum(jnp.max(z_w, axis=0, keepdims=True), jnp.max(zw_new, axis=0, keepdims=True))
    e_w = jnp.where(z_w > 0.5 * NEG, jnp.exp(z_w - m_w), 0.0)
    ew_new = jnp.where(zw_new > 0.5 * NEG, jnp.exp(zw_new - m_w), 0.0)
    acc_w = dot_c0(e_w.astype(BF16), vw) + dot_c0(ew_new.astype(BF16), vw_new)
    den_w = jnp.sum(e_w, axis=0, keepdims=True) + jnp.sum(ew_new, axis=0, keepdims=True)
    o_win = finish(acc_w, den_w)

    gates = gates_ref[0]
    merged = []
    for h in range(N_HEADS):
        sl = slice(h * ds, (h + 1) * ds)
        merged.append(gates[:, 3 * h:3 * h + 1] * o_cmp[sl]
                      + gates[:, 3 * h + 1:3 * h + 2] * o_slc[sl]
                      + gates[:, 3 * h + 2:3 * h + 3] * o_win[sl])
    for t in range(N_HEADS // 2):
        kvh = (2 * t) // Q_PER_KV
        a_ref[0, :, t * LANES:(t + 1) * LANES] = _unplace(merged[2 * t], merged[2 * t + 1], kvh).astype(BF16)


def _attn_sample(page_table, q, gates, kcvc, kvs_new16, kvw_new16, state_win, cache_pages, *, past, nb_real):
    b, n_pages = page_table.shape
    ds = q.shape[1]
    page_rows = cache_pages.shape[1]
    nblk = kcvc.shape[1]
    win_buf = state_win.shape[1]
    kern = functools.partial(_attn_sample_kernel, n_pages=n_pages, page_rows=page_rows, past=past, ds=ds,
                             nb_real=nb_real)
    grid_spec = pltpu.PrefetchScalarGridSpec(
        num_scalar_prefetch=1,
        grid=(b,),
        in_specs=[
            pl.BlockSpec((1, ds, ATTN_WIDTH), lambda i, pt: (i, 0, 0)),
            pl.BlockSpec((1, ds, GATE_COLS), lambda i, pt: (i, 0, 0)),
            pl.BlockSpec((1, nblk, KV_COLS), lambda i, pt: (i, 0, 0)),
            pl.BlockSpec((1, ds, KV_COLS), lambda i, pt: (i, 0, 0)),
            pl.BlockSpec((1, ds, KV_COLS), lambda i, pt: (i, 0, 0)),
            pl.BlockSpec((1, win_buf, KV_COLS), lambda i, pt: (i, 0, 0)),
            pl.BlockSpec(memory_space=pl.ANY),
        ],
        out_specs=pl.BlockSpec((1, ds, ATTN_WIDTH), lambda i, pt: (i, 0, 0)),
        scratch_shapes=[
            pltpu.VMEM((2, n_pages * page_rows, KV_COLS), F32),
            pltpu.VMEM((n_pages * page_rows, S_ROWS), F32),
            pltpu.SemaphoreType.DMA((2,)),
        ],
    )
    return pl.pallas_call(
        kern,
        grid_spec=grid_spec,
        out_shape=jax.ShapeDtypeStruct((b, ds, ATTN_WIDTH), BF16),
        compiler_params=pltpu.CompilerParams(
            dimension_semantics=("arbitrary",), vmem_limit_bytes=VMEM_LIMIT),
        name="attn_sample",
    )(page_table, q, gates, kcvc, kvs_new16, kvw_new16, state_win, cache_pages)


def _ffn_kernel(x_ref, a_ref, m_ref, wo_ref, g2_ref, wr_ref, h_ref, n2_ref, gate_ref):
    h = (x_ref[...]
         + jnp.dot(a_ref[...], wo_ref[:ATTN_WIDTH, :], preferred_element_type=F32)
         + jnp.dot(m_ref[...], wo_ref[ATTN_WIDTH:, :], preferred_element_type=F32))
    h_ref[...] = h
    ms = jnp.mean(h * h, axis=-1, keepdims=True)
    n2 = ((h * lax.rsqrt(ms + EPS)) * g2_ref[...]).astype(BF16)
    n2_ref[...] = n2
    logits = jnp.dot(n2, wr_ref[...], preferred_element_type=F32)
    lane = lax.broadcasted_iota(jnp.int32, logits.shape, 1)
    big = jnp.int32(LANES)

    def masked_softmax(mask):
        zz = jnp.where(mask, logits, NEG)
        mx = jnp.max(zz, axis=-1, keepdims=True)
        ee = jnp.where(mask, jnp.exp(zz - mx), 0.0)
        return ee / jnp.sum(ee, axis=-1, keepdims=True)

    def first_argmax(vals, mask):
        v = jnp.max(jnp.where(mask, vals, -1.0), axis=-1, keepdims=True)
        idx = jnp.min(jnp.where(mask & (vals == v), lane, big), axis=-1, keepdims=True)
        return v, idx

    is_g = lane < N_EXPERT_GROUPS
    pg = masked_softmax(is_g)
    g_val, g_idx = first_argmax(pg, is_g)
    e_lane = lane - ROUTER_LANE0
    in_grp = (e_lane >= 0) & (e_lane < N_EXPERTS) & (e_lane // EXPERTS_PER_GROUP == g_idx)
    pe = masked_softmax(in_grp)
    v1, i1 = first_argmax(pe, in_grp)
    rest = in_grp & (lane != i1)
    v2, i2 = first_argmax(pe, rest)
    scale = g_val / (v1 + v2)
    gate_ref[...] = jnp.where(lane == i1, v1 * scale, jnp.where(lane == i2, v2 * scale, 0.0))


def _ffn(x2d, a2d, m2d, w_out16, g2, w_router16, tm):
    t = x2d.shape[0]
    row = lambda i: (i, 0)
    const = lambda i: (0, 0)
    return pl.pallas_call(
        _ffn_kernel,
        grid=(t // tm,),
        in_specs=[
            pl.BlockSpec((tm, D_MODEL), row),
            pl.BlockSpec((tm, ATTN_WIDTH), row),
            pl.BlockSpec((tm, POOL_WIDTH), row),
            pl.BlockSpec((D_MODEL, D_MODEL), const),
            pl.BlockSpec((1, D_MODEL), const),
            pl.BlockSpec((D_MODEL, LANES), const),
        ],
        out_specs=(pl.BlockSpec((tm, D_MODEL), row), pl.BlockSpec((tm, D_MODEL), row),
                   pl.BlockSpec((tm, LANES), row)),
        out_shape=(jax.ShapeDtypeStruct((t, D_MODEL), F32), jax.ShapeDtypeStruct((t, D_MODEL), BF16),
                   jax.ShapeDtypeStruct((t, LANES), F32)),
        compiler_params=pltpu.CompilerParams(
            dimension_semantics=("arbitrary",), vmem_limit_bytes=VMEM_LIMIT),
        name="ffn",
    )(x2d, a2d, m2d, w_out16, g2, w_router16)


def _moe_kernel(n2_ref, gate_ref, h_ref, wg_ref, wu_ref, wd_ref, y_ref):
    e = pl.program_id(1)

    @pl.when(e == 0)
    def _():
        y_ref[...] = h_ref[...]

    n2 = n2_ref[...]
    gu = jnp.dot(n2, wg_ref[0], preferred_element_type=F32)
    up = jnp.dot(n2, wu_ref[0], preferred_element_type=F32)
    he = (gu * (1.0 / (1.0 + jnp.exp(-gu)))) * up
    out = jnp.dot(he.astype(BF16), wd_ref[0], preferred_element_type=F32)
    lane = lax.broadcasted_iota(jnp.int32, gate_ref.shape, 1)
    gcol = jnp.sum(jnp.where(lane == e + ROUTER_LANE0, gate_ref[...], 0.0), axis=-1, keepdims=True)
    y_ref[...] += gcol * out


def _moe(n2, gate, h, wg16, wu16, wd16, tm):
    t = n2.shape[0]
    row = lambda i, e: (i, 0)
    return pl.pallas_call(
        _moe_kernel,
        grid=(t // tm, N_EXPERTS),
        in_specs=[
            pl.BlockSpec((tm, D_MODEL), row),
            pl.BlockSpec((tm, LANES), row),
            pl.BlockSpec((tm, D_MODEL), row),
            pl.BlockSpec((1, D_MODEL, D_EXPERT), lambda i, e: (e, 0, 0)),
            pl.BlockSpec((1, D_MODEL, D_EXPERT), lambda i, e: (e, 0, 0)),
            pl.BlockSpec((1, D_EXPERT, D_MODEL), lambda i, e: (e, 0, 0)),
        ],
        out_specs=pl.BlockSpec((tm, D_MODEL), row),
        out_shape=jax.ShapeDtypeStruct((t, D_MODEL), F32),
        compiler_params=pltpu.CompilerParams(
            dimension_semantics=("arbitrary", "arbitrary"), vmem_limit_bytes=VMEM_LIMIT),
        name="moe",
    )(n2, gate, h, wg16, wu16, wd16)


PROJ_TILE = 512
MOE_TILE = 1024
SAMPLE_NBLK_PAD = 2 * LANES


def _pad_rows_to(n, mult):
    return -(-n // mult) * mult


def kernel(x_prompt, x_sample, cache_cmp_kv, cache_slc_kv, state_win_kv, state_pool, page_table, norm1_g, w_in, q_norm_g, k_norm_cmp_g, k_norm_slc_g, k_norm_win_g, cmp_pos_emb, w_cmp_k, w_cmp_v, pool_w, pool_scale, w_out, norm2_g, w_router_group, w_router_expert, w_gate, w_up, w_down):
    b, s, _ = x_prompt.shape
    db, ds, _ = x_sample.shape
    n_pool, page_rows = cache_cmp_kv.shape[:2]
    past = page_table.shape[1] * page_rows
    kv5 = (2, N_KV_HEADS, HEAD_DIM)

    n_lead = ATTN_WIDTH + 3 * KV_COLS + GATE_COLS
    w_packed = jnp.concatenate(
        [w_in[:, :n_lead], jnp.zeros((D_MODEL, GATE_PAD - GATE_COLS), w_in.dtype), w_in[:, n_lead:]],
        axis=1).astype(BF16)
    g1 = norm1_g[None, :]
    g2 = norm2_g[None, :]
    two = lambda g: jnp.tile(g, 2)[None, :]
    pos4 = jnp.tile(cmp_pos_emb, (1, 4))
    zb = jnp.zeros_like(w_cmp_k)
    wbd = jnp.concatenate([
        jnp.concatenate([w_cmp_k, zb, zb, zb], axis=2),
        jnp.concatenate([zb, w_cmp_k, zb, zb], axis=2),
        jnp.concatenate([zb, zb, w_cmp_v, zb], axis=2),
        jnp.concatenate([zb, zb, zb, w_cmp_v], axis=2)], axis=1).astype(BF16)
    w_router = jnp.concatenate(
        [w_router_group, w_router_expert,
         jnp.zeros((D_MODEL, LANES - N_EXPERT_GROUPS - N_EXPERTS), w_router_group.dtype)], axis=1).astype(BF16)
    w_out16 = w_out.astype(BF16)
    pool_w16 = pool_w.astype(BF16)
    wg16, wu16, wd16 = w_gate.astype(BF16), w_up.astype(BF16), w_down.astype(BF16)
    ps = pool_scale[None, :]

    def ffn_moe(x2d, a2d, m2d, tm_ffn, tm_moe):
        h, n2, gate = _ffn(x2d, a2d, m2d, w_out16, g2, w_router, tm_ffn)
        return _moe(n2, gate, h, wg16, wu16, wd16, tm_moe)

    x2d = x_prompt.reshape(b * s, D_MODEL)
    q, kvc, kvs, kvw, kvs16, kvw16, gates, u = _proj(
        x2d, g1, w_packed, two(q_norm_g), two(k_norm_slc_g), two(k_norm_win_g), PROJ_TILE)
    m_p = _pool(u.reshape(b, s, POOL_WIDTH), pool_w16, ps)
    pages_per_seq = s // page_rows
    pt_p = jnp.arange(b * pages_per_seq, dtype=jnp.int32).reshape(b, pages_per_seq)
    kcvc_p = _compress(pt_p, kvc.reshape(b * pages_per_seq, page_rows, KV_COLS),
                       jnp.zeros((b, 8, KV_COLS), F32), pos4, wbd, two(k_norm_cmp_g),
                       nblk_pad=LANES, tail_rows=0)
    a_p = _attn_prompt(q.reshape(b, s, ATTN_WIDTH), gates.reshape(b, s, GATE_COLS), kcvc_p,
                       kvs16.reshape(b, s, KV_COLS), kvw16.reshape(b, s, KV_COLS), s // CMP_BLOCK)
    y_p = ffn_moe(x2d, a_p.reshape(b * s, ATTN_WIDTH), m_p.reshape(b * s, POOL_WIDTH), PROJ_TILE, MOE_TILE)

    xs2d = x_sample.reshape(db * ds, D_MODEL)
    ts = db * ds
    q_s, kvc_s, kvs_s, kvw_s, kvs16_s, kvw16_s, gates_s, u_s = _proj(
        xs2d, g1, w_packed, two(q_norm_g), two(k_norm_slc_g), two(k_norm_win_g), ts)
    pool_ext = jnp.concatenate([state_pool, u_s.reshape(db, ds, POOL_WIDTH)], axis=1)
    n_ext = pool_ext.shape[1]
    lead = _pad_rows_to(n_ext, 8) - n_ext
    pool_in = jnp.concatenate([jnp.zeros((db, lead, POOL_WIDTH), F32), pool_ext], axis=1)
    m_s = _pool(pool_in, pool_w16, ps)[:, lead + n_ext - ds:]
    nb_s = (past + ds + CMP_BLOCK - 1) // CMP_BLOCK
    nblk_c = _pad_rows_to(nb_s, 8)
    kcvc_s = _compress(page_table, cache_cmp_kv.reshape(n_pool, page_rows, KV_COLS),
                       kvc_s.reshape(db, ds, KV_COLS), pos4, wbd, two(k_norm_cmp_g),
                       nblk_pad=nblk_c, tail_rows=ds)
    kcvc_s = jnp.concatenate([kcvc_s, jnp.zeros((db, SAMPLE_NBLK_PAD - nblk_c, KV_COLS), F32)], axis=1)
    a_s = _attn_sample(page_table, q_s.reshape(db, ds, ATTN_WIDTH), gates_s.reshape(db, ds, GATE_COLS), kcvc_s,
                       kvs16_s.reshape(db, ds, KV_COLS), kvw16_s.reshape(db, ds, KV_COLS),
                       state_win_kv.reshape(db, state_win_kv.shape[1], KV_COLS),
                       cache_slc_kv.reshape(n_pool, page_rows, KV_COLS), past=past, nb_real=nb_s)
    y_s = ffn_moe(xs2d, a_s.reshape(ts, ATTN_WIDTH), m_s.reshape(ts, POOL_WIDTH), ts, ts)

    win_keep = min(WINDOW, s)
    kvw_p5 = kvw.reshape(b, s, *kv5)
    kvw_s5 = kvw_s.reshape(db, ds, *kv5)
    win_ctx = jnp.concatenate([state_win_kv, kvw_s5], axis=1)
    return (y_p.reshape(b, s, D_MODEL), y_s.reshape(db, ds, D_MODEL),
            kvc.reshape(b, s, *kv5), kvc_s.reshape(db, ds, *kv5),
            kvs.reshape(b, s, *kv5), kvs_s.reshape(db, ds, *kv5),
            kvw_p5[:, s - win_keep:], win_ctx[:, ds:],
            u.reshape(b, s, POOL_WIDTH)[:, s - POOL_BUF:], pool_ext[:, ds:])
```

```python
import functools

import jax
import jax.numpy as jnp
import numpy as np
from jax import lax
from jax.experimental import pallas as pl
from jax.experimental.pallas import tpu as pltpu

F32 = jnp.float32
BF16 = jnp.bfloat16

D_MODEL = 1024
N_HEADS = 8
HEAD_DIM = 64
N_KV_HEADS = 2
Q_PER_KV = N_HEADS // N_KV_HEADS
ATTN_WIDTH = N_HEADS * HEAD_DIM
KV_COLS = 2 * N_KV_HEADS * HEAD_DIM
GATE_COLS = 3 * N_HEADS
POOL_WIDTH = D_MODEL - ATTN_WIDTH
POOL_WINDOWS = (2, 4, 8, 16)
POOL_GROUP_WIDTH = POOL_WIDTH // len(POOL_WINDOWS)
POOL_BUF = max(POOL_WINDOWS) - 1
CMP_BLOCK = 64
TOP_K_BLOCKS = 16
WINDOW = 512
FORCE_SCORE = 1.0e4
N_EXPERT_GROUPS = 4
EXPERTS_PER_GROUP = 4
N_EXPERTS = N_EXPERT_GROUPS * EXPERTS_PER_GROUP
D_EXPERT = 512
EPS = 1e-6
NEG = -1e30
SCALE = HEAD_DIM ** -0.5

LANES = 128
PAGE = 2 * CMP_BLOCK
GATE_PAD = LANES
ROW_COLS = ATTN_WIDTH + GATE_PAD + POOL_WIDTH
OFF_GATE = ATTN_WIDTH
OFF_U = OFF_GATE + GATE_PAD
KV_ROWS = 3 * KV_COLS
ROUTER_LANE0 = N_EXPERT_GROUPS
Q_TILE = 256
K_CHUNK = 256
VMEM_LIMIT = 56 * 1024 * 1024


def _slope(h):
    return float(2.0 ** (-8.0 * (h + 1) / N_HEADS))


def _half_group_norm(v, g):
    lane = lax.broadcasted_iota(jnp.int32, v.shape, 1)
    lo = lane < HEAD_DIM
    v2 = v * v
    s_lo = jnp.sum(jnp.where(lo, v2, 0.0), axis=-1, keepdims=True)
    s_hi = jnp.sum(jnp.where(lo, 0.0, v2), axis=-1, keepdims=True)
    r_lo = lax.rsqrt(s_lo * (1.0 / HEAD_DIM) + EPS)
    r_hi = lax.rsqrt(s_hi * (1.0 / HEAD_DIM) + EPS)
    return (v * jnp.where(lo, r_lo, r_hi)) * g


def _dot_t(a, b):
    return lax.dot_general(a, b, (((1,), (1,)), ((), ())), preferred_element_type=F32)


def _proj_kernel(x_ref, g1_ref, wr_ref, wkv_ref, qg_ref, kg_ref,
                 q_ref, gates_ref, u_ref, kvc_ref, kvs_ref, kvw_ref, kvs16_ref, kvw16_ref):
    x = x_ref[0]
    tm = x.shape[0]
    ms = jnp.mean(x * x, axis=-1, keepdims=True)
    n = ((x * lax.rsqrt(ms + EPS)) * g1_ref[...]).astype(BF16)
    p = jnp.dot(n, wr_ref[...], preferred_element_type=F32)
    pt = _dot_t(wkv_ref[...], n)
    for t in range(ATTN_WIDTH // LANES):
        sl = slice(t * LANES, (t + 1) * LANES)
        q_ref[0, :, sl] = _half_group_norm(p[:, sl], qg_ref[...]).astype(BF16)
    sig = 1.0 / (1.0 + jnp.exp(-p[:, OFF_GATE:OFF_GATE + GATE_PAD]))
    gates_ref[0] = sig[:, :GATE_COLS]
    u_ref[0] = p[:, OFF_U:OFF_U + POOL_WIDTH]
    kvc_ref[0] = pt[:KV_COLS]
    half = KV_COLS // 2
    for bi, (out32, out16) in enumerate(((kvs_ref, kvs16_ref), (kvw_ref, kvw16_ref))):
        off = (bi + 1) * KV_COLS
        heads = []
        for hh in range(N_KV_HEADS):
            kh = pt[off + hh * HEAD_DIM:off + (hh + 1) * HEAD_DIM]
            msk = jnp.mean(kh * kh, axis=0, keepdims=True)
            heads.append((kh * lax.rsqrt(msk + EPS)) * kg_ref[bi])
        kn = jnp.concatenate(heads, axis=0)
        v = pt[off + half:off + KV_COLS]
        out32[0, :half, :] = kn
        out32[0, half:, :] = v
        for c in range(tm // K_CHUNK):
            cs = slice(c * K_CHUNK, (c + 1) * K_CHUNK)
            out16[0, c, :half, :] = kn[:, cs].astype(BF16)
            out16[0, c, half:, :] = v[:, cs].astype(BF16)


def _proj(x3d, g1, w_row, w_kv, qg, kg, tm):
    b, s, _ = x3d.shape
    tok = lambda i, j: (i, j, 0)
    feat = lambda i, j: (i, 0, j)
    const2 = lambda i, j: (0, 0)
    nck = tm // K_CHUNK
    out_shape = (
        jax.ShapeDtypeStruct((b, s, ATTN_WIDTH), BF16),
        jax.ShapeDtypeStruct((b, s, GATE_COLS), F32),
        jax.ShapeDtypeStruct((b, s, POOL_WIDTH), F32),
        jax.ShapeDtypeStruct((b, KV_COLS, s), F32),
        jax.ShapeDtypeStruct((b, KV_COLS, s), F32),
        jax.ShapeDtypeStruct((b, KV_COLS, s), F32),
        jax.ShapeDtypeStruct((b, s // K_CHUNK, KV_COLS, K_CHUNK), BF16),
        jax.ShapeDtypeStruct((b, s // K_CHUNK, KV_COLS, K_CHUNK), BF16),
    )
    out_specs = (
        pl.BlockSpec((1, tm, ATTN_WIDTH), tok),
        pl.BlockSpec((1, tm, GATE_COLS), tok),
        pl.BlockSpec((1, tm, POOL_WIDTH), tok),
        pl.BlockSpec((1, KV_COLS, tm), feat),
        pl.BlockSpec((1, KV_COLS, tm), feat),
        pl.BlockSpec((1, KV_COLS, tm), feat),
        pl.BlockSpec((1, nck, KV_COLS, K_CHUNK), lambda i, j: (i, j, 0, 0)),
        pl.BlockSpec((1, nck, KV_COLS, K_CHUNK), lambda i, j: (i, j, 0, 0)),
    )
    return pl.pallas_call(
        _proj_kernel,
        grid=(b, s // tm),
        in_specs=[
            pl.BlockSpec((1, tm, D_MODEL), tok),
            pl.BlockSpec((1, D_MODEL), const2),
            pl.BlockSpec((D_MODEL, ROW_COLS), const2),
            pl.BlockSpec((KV_ROWS, D_MODEL), const2),
            pl.BlockSpec((1, LANES), const2),
            pl.BlockSpec((2, HEAD_DIM, 1), lambda i, j: (0, 0, 0)),
        ],
        out_specs=out_specs,
        out_shape=out_shape,
        compiler_params=pltpu.CompilerParams(
            dimension_semantics=("arbitrary", "arbitrary"), vmem_limit_bytes=VMEM_LIMIT),
        name="proj",
    )(x3d, g1, w_row, w_kv, qg, kg)


def _pool_kernel(u_ref, pw_ref, ps_ref, m_ref):
    u = u_ref[0]
    n = u.shape[0]

    def shift(v, k):
        rolled = pltpu.roll(v, k, axis=0)
        r = lax.broadcasted_iota(jnp.int32, v.shape, 0)
        return jnp.where(r >= k, rolled, 0.0)

    sums = []
    s = u
    k = 1
    for gi in range(len(POOL_WINDOWS)):
        s = s + shift(s, k)
        k *= 2
        sums.append(s[:, :POOL_GROUP_WIDTH])
        s = s[:, POOL_GROUP_WIDTH:]
    row = lax.broadcasted_iota(jnp.int32, (n, 1), 0)
    for gi, w in enumerate(POOL_WINDOWS):
        sl = slice(gi * POOL_GROUP_WIDTH, (gi + 1) * POOL_GROUP_WIDTH)
        cnt = jnp.minimum(row + 1, w).astype(F32)
        d = sums[gi] / cnt - u[:, sl]
        y = jnp.dot(d.astype(BF16), pw_ref[gi], preferred_element_type=F32)
        m_ref[0, :, sl] = (y * ps_ref[:, sl]).astype(BF16)


def _pool(u3d, pool_w16, pool_scale):
    b, n, _ = u3d.shape
    return pl.pallas_call(
        _pool_kernel,
        grid=(b,),
        in_specs=[
            pl.BlockSpec((1, n, POOL_WIDTH), lambda i: (i, 0, 0)),
            pl.BlockSpec((len(POOL_WINDOWS), POOL_GROUP_WIDTH, POOL_GROUP_WIDTH), lambda i: (0, 0, 0)),
            pl.BlockSpec((1, POOL_WIDTH), lambda i: (0, 0)),
        ],
        out_specs=pl.BlockSpec((1, n, POOL_WIDTH), lambda i: (i, 0, 0)),
        out_shape=jax.ShapeDtypeStruct((b, n, POOL_WIDTH), BF16),
        compiler_params=pltpu.CompilerParams(
            dimension_semantics=("arbitrary",), vmem_limit_bytes=VMEM_LIMIT),
        name="pool",
    )(u3d, pool_w16, pool_scale)


def _compress_kernel(pt_ref, src_ref, tail_ref, pos_ref, bd_ref, g_ref, kc_ref, vc_ref, buf, sem,
                     *, n_pages, npp, has_tail, paged):
    b = pl.program_id(0)
    nb = pl.num_programs(0)
    n_slabs = 2 * N_KV_HEADS

    def slab_copy(row, p, c, slot):
        kv, kvh = divmod(c, N_KV_HEADS)
        if paged:
            src = src_ref.at[pt_ref[row, p], pl.ds(c * HEAD_DIM, HEAD_DIM), :]
        else:
            src = src_ref.at[row, pl.ds(c * HEAD_DIM, HEAD_DIM), pl.ds(p * PAGE, PAGE)]
        return pltpu.make_async_copy(src, buf.at[slot, kv, :, kvh * npp + p, :], sem.at[slot])

    def tail_copy(c, slot):
        kv, kvh = divmod(c, N_KV_HEADS)
        return pltpu.make_async_copy(tail_ref.at[0, pl.ds(c * HEAD_DIM, HEAD_DIM), :],
                                     buf.at[slot, kv, :, kvh * npp + n_pages, :], sem.at[slot])

    def row_copies(row, slot, fn):
        def body(p, carry):
            for c in range(n_slabs):
                fn(slab_copy(row, p, c, slot))
            return carry
        lax.fori_loop(0, n_pages, body, 0)

    n_real = n_pages + (1 if has_tail else 0)

    @pl.when(b == 0)
    def _():
        if npp > n_real:
            for kvh in range(N_KV_HEADS):
                buf[:, :, :, kvh * npp + n_real:(kvh + 1) * npp, :] = jnp.zeros(
                    (2, 2, HEAD_DIM, npp - n_real, LANES), F32)
        row_copies(0, 0, lambda cp: cp.start())

    slot = b % 2

    @pl.when(b + 1 < nb)
    def _():
        row_copies(b + 1, 1 - slot, lambda cp: cp.start())

    if has_tail:
        for c in range(n_slabs):
            tail_copy(c, slot).start()
        for c in range(n_slabs):
            tail_copy(c, slot).wait()
    row_copies(b, slot, lambda cp: cp.wait())

    rows = N_KV_HEADS * npp

    def body(d, carry):
        acc_k, acc_v = carry
        pos = pos_ref[pl.ds(d, 1), :]
        xk = (buf[slot, 0, d] + pos).astype(BF16)
        xv = (buf[slot, 1, d] + pos).astype(BF16)
        return (acc_k + jnp.dot(xk, bd_ref[0, d], preferred_element_type=F32),
                acc_v + jnp.dot(xv, bd_ref[1, d], preferred_element_type=F32))

    zero = jnp.zeros((rows, LANES), F32)
    acc_k, acc_v = lax.fori_loop(0, HEAD_DIM, body, (zero, zero), unroll=8)
    kc_ref[0] = _half_group_norm(acc_k, g_ref[...])
    vc_ref[0] = acc_v


def _compress(page_table, src, tail, pos_t, bd, g2, *, npp, has_tail, paged):
    b, n_pages = page_table.shape
    kern = functools.partial(_compress_kernel, n_pages=n_pages, npp=npp, has_tail=has_tail, paged=paged)
    rows = N_KV_HEADS * npp
    grid_spec = pltpu.PrefetchScalarGridSpec(
        num_scalar_prefetch=1,
        grid=(b,),
        in_specs=[
            pl.BlockSpec(memory_space=pl.ANY),
            pl.BlockSpec((1, KV_COLS, LANES), lambda i, pt: (i, 0, 0)),
            pl.BlockSpec((HEAD_DIM, LANES), lambda i, pt: (0, 0)),
            pl.BlockSpec((2, HEAD_DIM, LANES, LANES), lambda i, pt: (0, 0, 0, 0)),
            pl.BlockSpec((1, LANES), lambda i, pt: (0, 0)),
        ],
        out_specs=(pl.BlockSpec((1, rows, LANES), lambda i, pt: (i, 0, 0)),
                   pl.BlockSpec((1, rows, LANES), lambda i, pt: (i, 0, 0))),
        scratch_shapes=[
            pltpu.VMEM((2, 2, HEAD_DIM, rows, LANES), F32),
            pltpu.SemaphoreType.DMA((2,)),
        ],
    )
    return pl.pallas_call(
        kern,
        grid_spec=grid_spec,
        out_shape=(jax.ShapeDtypeStruct((b, rows, LANES), F32), jax.ShapeDtypeStruct((b, rows, LANES), F32)),
        compiler_params=pltpu.CompilerParams(
            dimension_semantics=("arbitrary",), vmem_limit_bytes=VMEM_LIMIT),
        name="compress",
    )(page_table, src, tail, pos_t, bd, g2)


def _to_half(tile, src_half, dst_half):
    lane = lax.broadcasted_iota(jnp.int32, tile.shape, 1)
    src = tile if src_half == dst_half else pltpu.roll(tile, HEAD_DIM, axis=1)
    keep = (lane < HEAD_DIM) if dst_half == 0 else (lane >= HEAD_DIM)
    return jnp.where(keep, src, 0.0)


def _pair_tile(o_even, o_odd, half):
    lane = lax.broadcasted_iota(jnp.int32, o_even.shape, 1)
    if half == 0:
        return jnp.where(lane < HEAD_DIM, o_even, pltpu.roll(o_odd, HEAD_DIM, axis=1))
    return jnp.where(lane < HEAD_DIM, pltpu.roll(o_even, HEAD_DIM, axis=1), o_odd)


def _gate_tile(gates, pair, c, shape):
    lane = lax.broadcasted_iota(jnp.int32, shape, 1)
    he, ho = 2 * pair, 2 * pair + 1
    return jnp.where(lane < HEAD_DIM, gates[:, 3 * he + c:3 * he + c + 1], gates[:, 3 * ho + c:3 * ho + c + 1])


def _block_of_col(col, npp):
    return 2 * (col % npp) + col // npp


def _cmp_operand(x, npp):
    return jnp.concatenate([_to_half(x, 0, 0), _to_half(x, 1, 0)], axis=0).astype(BF16)


def _rank_select(score_t, blk_t, cols):
    rank = jnp.zeros(score_t.shape, jnp.int32)
    for r, n in cols:
        row = score_t[r:r + 1, :]
        ahead = (row > score_t) | ((row == score_t) & (blk_t > n))
        rank = rank + ahead.astype(jnp.int32)
    return ((rank < TOP_K_BLOCKS) & (score_t > -0.5)).astype(F32)


def _attn_prompt_kernel(q_ref, gates_ref, kc_ref, vc_ref, kvs_ref, kvw_ref, exp_ref, a_ref,
                        zbuf, m_sc, l_sc, acc_sc, *, n_pages, npp):
    i = pl.program_id(1)
    q0 = i * Q_TILE
    rows = Q_PER_KV * Q_TILE
    ncols = 2 * npp
    qpos = q0 + lax.broadcasted_iota(jnp.int32, (Q_TILE, 1), 0)
    gates = gates_ref[0]
    blk = _block_of_col(lax.broadcasted_iota(jnp.int32, (Q_TILE, ncols), 1), npp)
    real_cols = [(half * n_pages + p, 2 * p + half) for half in range(2) for p in range(n_pages)]
    half_cols = KV_COLS // 2

    def branch(qk, kv_ref, kvh, slopes, c_lo, c_hi, mask_bias):
        m_sc[...] = jnp.full((rows, LANES), NEG, F32)

        def pass1(c, carry):
            kt = kv_ref[0, c, kvh * HEAD_DIM:kvh * HEAD_DIM + LANES, :]
            s = jnp.dot(qk, kt, preferred_element_type=F32)
            mb = mask_bias(c)
            krel = (c * K_CHUNK - q0 + lax.broadcasted_iota(jnp.int32, (1, K_CHUNK), 1)).astype(F32)
            for g in range(Q_PER_KV):
                sl = slice(g * Q_TILE, (g + 1) * Q_TILE)
                z = s[sl] + (mb + slopes[g] * krel)
                zbuf[c, sl, :] = z
                m_sc[sl] = jnp.maximum(m_sc[sl], jnp.maximum(z[:, :LANES], z[:, LANES:]))
            return carry

        lax.fori_loop(c_lo, c_hi, pass1, 0)
        m = jnp.broadcast_to(jnp.max(m_sc[...], axis=-1, keepdims=True), (rows, LANES))
        l_sc[...] = jnp.zeros((rows, LANES), F32)
        acc_sc[...] = jnp.zeros((rows, LANES), F32)

        def pass2(c, carry):
            z = zbuf[c]
            e0 = jnp.exp(z[:, :LANES] - m)
            e1 = jnp.exp(z[:, LANES:] - m)
            e = jnp.concatenate([e0, e1], axis=1).astype(BF16)
            vt = kv_ref[0, c, half_cols:, :]
            acc_sc[...] += _dot_t(e, vt)
            l_sc[...] += e0 + e1
            return carry

        lax.fori_loop(c_lo, c_hi, pass2, 0)
        l = jnp.sum(l_sc[...], axis=-1, keepdims=True)
        return acc_sc[...] / jnp.where(l > 0.0, l, 1.0)

    for kvh in range(N_KV_HEADS):
        slopes = [_slope(kvh * Q_PER_KV + g) for g in range(Q_PER_KV)]
        pieces = []
        for g in range(Q_PER_KV):
            h = kvh * Q_PER_KV + g
            tile = q_ref[0, :, (h // 2) * LANES:(h // 2 + 1) * LANES].astype(F32) * SCALE
            pieces.append(_to_half(tile, h % 2, 0))
        qk = jnp.concatenate(pieces, axis=0).astype(BF16)

        kc = _cmp_operand(kc_ref[0, kvh], npp)
        vc = _cmp_operand(vc_ref[0, kvh], npp)
        s = _dot_t(qk, kc)
        dist_c = qpos - (blk * CMP_BLOCK + (CMP_BLOCK - 1))
        mask_c = dist_c >= 0
        dist_cf = dist_c.astype(F32)
        p_list = []
        p_kv = jnp.zeros((Q_TILE, ncols), F32)
        for g in range(Q_PER_KV):
            z = jnp.where(mask_c, s[g * Q_TILE:(g + 1) * Q_TILE] - slopes[g] * dist_cf, NEG)
            mx = jnp.max(z, axis=-1, keepdims=True)
            e = jnp.where(mask_c, jnp.exp(z - mx), 0.0)
            den = jnp.sum(e, axis=-1, keepdims=True)
            p = e / jnp.where(den > 0.0, den, 1.0)
            p_kv = p_kv + p
            p_list.append(p.astype(BF16))
        o_cmp = jnp.dot(jnp.concatenate(p_list, axis=0), vc, preferred_element_type=F32)

        cand = blk * CMP_BLOCK <= qpos
        forced = (blk == qpos // CMP_BLOCK) | (blk == 0)
        score = jnp.where(forced, FORCE_SCORE, jnp.where(cand, p_kv, -1.0))
        score_t = score.T
        st = jnp.concatenate([score_t[:n_pages], score_t[npp:npp + n_pages]], axis=0)
        rr = lax.broadcasted_iota(jnp.int32, (2 * n_pages, Q_TILE), 0)
        bt = jnp.where(rr < n_pages, 2 * rr, 2 * (rr - n_pages) + 1)
        sel_s = _rank_select(st, bt, [(r, n) for r, (_, n) in enumerate(real_cols)])
        pad = jnp.zeros((npp - n_pages, Q_TILE), F32)
        sel_t = jnp.concatenate([sel_s[:n_pages], pad, sel_s[n_pages:], pad], axis=0)
        sel = sel_t.T.astype(BF16)

        def slc_bias(c):
            chosen = jnp.dot(sel, exp_ref[c], preferred_element_type=F32) > 0.5
            kpos = c * K_CHUNK + lax.broadcasted_iota(jnp.int32, (Q_TILE, K_CHUNK), 1)
            return jnp.where(chosen & (kpos <= qpos), 0.0, NEG)

        def win_bias(c):
            kpos = c * K_CHUNK + lax.broadcasted_iota(jnp.int32, (Q_TILE, K_CHUNK), 1)
            dist = qpos - kpos
            return jnp.where((dist >= 0) & (dist < WINDOW), 0.0, NEG)

        c_hi = (q0 + Q_TILE + K_CHUNK - 1) // K_CHUNK
        o_slc = branch(qk, kvs_ref, kvh, slopes, 0, c_hi, slc_bias)
        w_lo = jnp.maximum(q0 - WINDOW, 0) // K_CHUNK
        o_win = branch(qk, kvw_ref, kvh, slopes, w_lo, c_hi, win_bias)

        for pr in range(Q_PER_KV // 2):
            pair = (kvh * Q_PER_KV) // 2 + pr
            e_sl = slice(2 * pr * Q_TILE, (2 * pr + 1) * Q_TILE)
            o_sl = slice((2 * pr + 1) * Q_TILE, (2 * pr + 2) * Q_TILE)
            shape = (Q_TILE, LANES)
            tile = (_gate_tile(gates, pair, 0, shape) * _pair_tile(o_cmp[e_sl], o_cmp[o_sl], 0)
                    + _gate_tile(gates, pair, 1, shape) * _pair_tile(o_slc[e_sl], o_slc[o_sl], kvh)
                    + _gate_tile(gates, pair, 2, shape) * _pair_tile(o_win[e_sl], o_win[o_sl], kvh))
            a_ref[0, :, pair * LANES:(pair + 1) * LANES] = tile.astype(BF16)


def _attn_prompt(q, gates, kc, vc, kvs16, kvw16, expand, n_pages):
    b, s, _ = q.shape
    npp = kc.shape[2]
    n_chunks = s // K_CHUNK
    rows = Q_PER_KV * Q_TILE
    kern = functools.partial(_attn_prompt_kernel, n_pages=n_pages, npp=npp)
    return pl.pallas_call(
        kern,
        grid=(b, s // Q_TILE),
        in_specs=[
            pl.BlockSpec((1, Q_TILE, ATTN_WIDTH), lambda bi, i: (bi, i, 0)),
            pl.BlockSpec((1, Q_TILE, GATE_COLS), lambda bi, i: (bi, i, 0)),
            pl.BlockSpec((1, N_KV_HEADS, npp, LANES), lambda bi, i: (bi, 0, 0, 0)),
            pl.BlockSpec((1, N_KV_HEADS, npp, LANES), lambda bi, i: (bi, 0, 0, 0)),
            pl.BlockSpec((1, n_chunks, KV_COLS, K_CHUNK), lambda bi, i: (bi, 0, 0, 0)),
            pl.BlockSpec((1, n_chunks, KV_COLS, K_CHUNK), lambda bi, i: (bi, 0, 0, 0)),
            pl.BlockSpec((n_chunks, 2 * npp, K_CHUNK), lambda bi, i: (0, 0, 0)),
        ],
        out_specs=pl.BlockSpec((1, Q_TILE, ATTN_WIDTH), lambda bi, i: (bi, i, 0)),
        out_shape=jax.ShapeDtypeStruct((b, s, ATTN_WIDTH), BF16),
        scratch_shapes=[
            pltpu.VMEM((n_chunks, rows, K_CHUNK), F32),
            pltpu.VMEM((rows, LANES), F32),
            pltpu.VMEM((rows, LANES), F32),
            pltpu.VMEM((rows, LANES), F32),
        ],
        compiler_params=pltpu.CompilerParams(
            dimension_semantics=("arbitrary", "arbitrary"), vmem_limit_bytes=VMEM_LIMIT),
        name="attn_prompt",
    )(q, gates, kc, vc, kvs16, kvw16, expand)


S_ROWS = LANES
S_CHUNK = 1024


def _attn_sample_kernel(pt_ref, q_ref, gates_ref, kc_ref, vc_ref, kvs_new_ref, kvw_new_ref, win_ref, cache_ref,
                        a_ref, kbuf, zbuf, sem, *, n_pages, past, ds, npp):
    b = pl.program_id(0)
    nb = pl.num_programs(0)
    n_keys = n_pages * PAGE
    n_real = N_HEADS * ds
    ncols = 2 * npp
    half_cols = KV_COLS // 2
    n_blocks = (past + ds + CMP_BLOCK - 1) // CMP_BLOCK

    def page_copy(row, p, slot):
        return pltpu.make_async_copy(
            cache_ref.at[pt_ref[row, p]], kbuf.at[slot, :, pl.ds(p * PAGE, PAGE)], sem.at[slot])

    def row_copies(row, slot, fn):
        def body(p, carry):
            fn(page_copy(row, p, slot))
            return carry
        lax.fori_loop(0, n_pages, body, 0)

    @pl.when(b == 0)
    def _():
        row_copies(0, 0, lambda cp: cp.start())

    slot = b % 2

    @pl.when(b + 1 < nb)
    def _():
        row_copies(b + 1, 1 - slot, lambda cp: cp.start())

    by_kvh, low = [], []
    for h in range(N_HEADS):
        tile = q_ref[0, :, (h // 2) * LANES:(h // 2 + 1) * LANES].astype(F32) * SCALE
        by_kvh.append(_to_half(tile, h % 2, h // Q_PER_KV))
        low.append(_to_half(tile, h % 2, 0))
    zpad = jnp.zeros((S_ROWS - n_real, LANES), F32)
    qrows = jnp.concatenate(by_kvh + [zpad], axis=0).astype(BF16)
    qlow = jnp.concatenate(low + [zpad], axis=0).astype(BF16)

    r_col = lax.broadcasted_iota(jnp.int32, (S_ROWS, 1), 0)
    q_of_r = r_col % ds
    h_of_r = r_col // ds
    qp_r = past + q_of_r
    sl_r = jnp.zeros((S_ROWS, 1), F32)
    for h in range(N_HEADS):
        sl_r = jnp.where(h_of_r == h, _slope(h), sl_r)
    kvh_r = h_of_r // Q_PER_KV

    blk = _block_of_col(lax.broadcasted_iota(jnp.int32, (S_ROWS, ncols), 1), npp)
    s = jnp.where(kvh_r == 0, _dot_t(qlow, _cmp_operand(kc_ref[0, 0], npp)),
                  _dot_t(qlow, _cmp_operand(kc_ref[0, 1], npp)))
    dist_c = qp_r - (blk * CMP_BLOCK + (CMP_BLOCK - 1))
    mask_c = dist_c >= 0
    z = jnp.where(mask_c, s - sl_r * dist_c.astype(F32), NEG)
    mx = jnp.max(z, axis=-1, keepdims=True)
    e = jnp.where(mask_c, jnp.exp(z - mx), 0.0)
    den = jnp.sum(e, axis=-1, keepdims=True)
    p = e / jnp.where(den > 0.0, den, 1.0)
    pb = p.astype(BF16)
    o_cmp = jnp.where(kvh_r == 0, jnp.dot(pb, _cmp_operand(vc_ref[0, 0], npp), preferred_element_type=F32),
                      jnp.dot(pb, _cmp_operand(vc_ref[0, 1], npp), preferred_element_type=F32))

    kq = N_KV_HEADS * ds
    p_kv = []
    for kvh in range(N_KV_HEADS):
        acc = jnp.zeros((ds, ncols), F32)
        for g in range(Q_PER_KV):
            r0 = (kvh * Q_PER_KV + g) * ds
            acc = acc + p[r0:r0 + ds]
        p_kv.append(acc)
    p_kv = jnp.concatenate(p_kv, axis=0)
    blk2 = _block_of_col(lax.broadcasted_iota(jnp.int32, (kq, ncols), 1), npp)
    qp2 = past + lax.broadcasted_iota(jnp.int32, (kq, 1), 0) % ds
    cand = blk2 * CMP_BLOCK <= qp2
    forced = (blk2 == qp2 // CMP_BLOCK) | (blk2 == 0)
    score = jnp.where(forced, FORCE_SCORE, jnp.where(cand, p_kv, -1.0))
    rank = jnp.zeros((kq, ncols), jnp.int32)
    for n in range(n_blocks):
        c = (n % 2) * npp + n // 2
        col = score[:, c:c + 1]
        ahead = (col > score) | ((col == score) & (blk2 > n))
        rank = rank + ahead.astype(jnp.int32)
    sel2 = ((rank < TOP_K_BLOCKS) & (score > -0.5)).astype(F32)
    sel_rows = []
    for h in range(N_HEADS):
        kvh = h // Q_PER_KV
        sel_rows.append(sel2[kvh * ds:(kvh + 1) * ds])
    sel_rows.append(jnp.zeros((S_ROWS - n_real, ncols), F32))
    sel_rows = jnp.concatenate(sel_rows, axis=0)
    sel16 = sel_rows.astype(BF16)

    new_lane = lax.broadcasted_iota(jnp.int32, (1, LANES), 1)

    def new_scores(ref, extra_mask):
        k_new = ref[0, :half_cols, :].astype(BF16)
        s_new = jnp.dot(qrows, k_new, preferred_element_type=F32)
        dist = q_of_r - new_lane
        mask = (new_lane < ds) & (dist >= 0) & extra_mask
        return jnp.where(mask, s_new - sl_r * dist.astype(F32), NEG)

    def weighted_new(ref, e_new):
        return _dot_t(e_new.astype(BF16), ref[0, half_cols:, :].astype(BF16))

    row_copies(b, slot, lambda cp: cp.wait())
    n_chunks = n_keys // S_CHUNK

    def pass1(c, mrow):
        k0 = pl.multiple_of(c * S_CHUNK, S_CHUNK)
        kt = kbuf[slot, :half_cols, pl.ds(k0, S_CHUNK)].astype(BF16)
        st = jnp.dot(qrows, kt, preferred_element_type=F32)
        j = _block_of_col(lax.broadcasted_iota(jnp.int32, (ncols, S_CHUNK), 0), npp)
        t = k0 + lax.broadcasted_iota(jnp.int32, (ncols, S_CHUNK), 1)
        expand = (j == t // CMP_BLOCK).astype(BF16)
        chosen = jnp.dot(sel16, expand, preferred_element_type=F32) > 0.5
        tpos = k0 + lax.broadcasted_iota(jnp.int32, (1, S_CHUNK), 1)
        dist = qp_r - tpos
        zc = jnp.where(chosen & (dist >= 0), st - sl_r * dist.astype(F32), NEG)
        zbuf[c] = zc
        return jnp.maximum(mrow, jnp.max(zc, axis=-1, keepdims=True))

    mrow = lax.fori_loop(0, n_chunks, pass1, jnp.full((S_ROWS, 1), NEG, F32))
    c_last = ((past // CMP_BLOCK) % 2) * npp + (past // CMP_BLOCK) // 2
    z_new = new_scores(kvs_new_ref, sel_rows[:, c_last:c_last + 1] > 0.5)
    m_r = jnp.maximum(mrow, jnp.max(z_new, axis=-1, keepdims=True))

    def pass2(c, carry):
        acc, den_r = carry
        k0 = pl.multiple_of(c * S_CHUNK, S_CHUNK)
        zc = zbuf[c]
        ec = jnp.where(zc > 0.5 * NEG, jnp.exp(zc - m_r), 0.0)
        vt = kbuf[slot, half_cols:, pl.ds(k0, S_CHUNK)].astype(BF16)
        return acc + _dot_t(ec.astype(BF16), vt), den_r + jnp.sum(ec, axis=-1, keepdims=True)

    e_new = jnp.where(z_new > 0.5 * NEG, jnp.exp(z_new - m_r), 0.0)
    acc, den_r = lax.fori_loop(0, n_chunks, pass2,
                               (weighted_new(kvs_new_ref, e_new), jnp.sum(e_new, axis=-1, keepdims=True)))
    o_slc = acc / jnp.where(den_r > 0.0, den_r, 1.0)

    win_buf = win_ref.shape[2]
    kw = win_ref[0, :half_cols, :].astype(BF16)
    st = jnp.dot(qrows, kw, preferred_element_type=F32)
    kpos = past - win_buf + lax.broadcasted_iota(jnp.int32, (1, win_buf), 1)
    dist = qp_r - kpos
    mask = (dist >= 0) & (dist < WINDOW) & (kpos >= 0)
    z_w = jnp.where(mask, st - sl_r * dist.astype(F32), NEG)
    zw_new = new_scores(kvw_new_ref, True)
    m_w = jnp.maximum(jnp.max(z_w, axis=-1, keepdims=True), jnp.max(zw_new, axis=-1, keepdims=True))
    e_w = jnp.where(z_w > 0.5 * NEG, jnp.exp(z_w - m_w), 0.0)
    ew_new = jnp.where(zw_new > 0.5 * NEG, jnp.exp(zw_new - m_w), 0.0)
    acc_w = _dot_t(e_w.astype(BF16), win_ref[0, half_cols:, :].astype(BF16)) + weighted_new(kvw_new_ref, ew_new)
    den_w = jnp.sum(e_w, axis=-1, keepdims=True) + jnp.sum(ew_new, axis=-1, keepdims=True)
    o_win = acc_w / jnp.where(den_w > 0.0, den_w, 1.0)

    gates = gates_ref[0]
    for pair in range(N_HEADS // 2):
        kvh = (2 * pair) // Q_PER_KV
        e_sl = slice(2 * pair * ds, (2 * pair + 1) * ds)
        o_sl = slice((2 * pair + 1) * ds, (2 * pair + 2) * ds)
        shape = (ds, LANES)
        tile = (_gate_tile(gates, pair, 0, shape) * _pair_tile(o_cmp[e_sl], o_cmp[o_sl], 0)
                + _gate_tile(gates, pair, 1, shape) * _pair_tile(o_slc[e_sl], o_slc[o_sl], kvh)
                + _gate_tile(gates, pair, 2, shape) * _pair_tile(o_win[e_sl], o_win[o_sl], kvh))
        a_ref[0, :, pair * LANES:(pair + 1) * LANES] = tile.astype(BF16)


def _attn_sample(page_table, q, gates, kc, vc, kvs_new, kvw_new, state_win_t, cache_pages, *, past):
    b, n_pages = page_table.shape
    ds = q.shape[1]
    npp = kc.shape[2]
    win_buf = state_win_t.shape[2]
    n_keys = n_pages * PAGE
    kern = functools.partial(_attn_sample_kernel, n_pages=n_pages, past=past, ds=ds, npp=npp)
    grid_spec = pltpu.PrefetchScalarGridSpec(
        num_scalar_prefetch=1,
        grid=(b,),
        in_specs=[
            pl.BlockSpec((1, ds, ATTN_WIDTH), lambda i, pt: (i, 0, 0)),
            pl.BlockSpec((1, ds, GATE_COLS), lambda i, pt: (i, 0, 0)),
            pl.BlockSpec((1, N_KV_HEADS, npp, LANES), lambda i, pt: (i, 0, 0, 0)),
            pl.BlockSpec((1, N_KV_HEADS, npp, LANES), lambda i, pt: (i, 0, 0, 0)),
            pl.BlockSpec((1, KV_COLS, LANES), lambda i, pt: (i, 0, 0)),
            pl.BlockSpec((1, KV_COLS, LANES), lambda i, pt: (i, 0, 0)),
            pl.BlockSpec((1, KV_COLS, win_buf), lambda i, pt: (i, 0, 0)),
            pl.BlockSpec(memory_space=pl.ANY),
        ],
        out_specs=pl.BlockSpec((1, ds, ATTN_WIDTH), lambda i, pt: (i, 0, 0)),
        scratch_shapes=[
            pltpu.VMEM((2, KV_COLS, n_keys), F32),
            pltpu.VMEM((n_keys // S_CHUNK, S_ROWS, S_CHUNK), F32),
            pltpu.SemaphoreType.DMA((2,)),
        ],
    )
    return pl.pallas_call(
        kern,
        grid_spec=grid_spec,
        out_shape=jax.ShapeDtypeStruct((b, ds, ATTN_WIDTH), BF16),
        compiler_params=pltpu.CompilerParams(
            dimension_semantics=("arbitrary",), vmem_limit_bytes=VMEM_LIMIT),
        name="attn_sample",
    )(page_table, q, gates, kc, vc, kvs_new, kvw_new, state_win_t, cache_pages)


def _ffn_kernel(x_ref, a_ref, m_ref, wo_ref, g2_ref, wr_ref, h_ref, n2_ref, gate_ref):
    h = (x_ref[...]
         + jnp.dot(a_ref[...], wo_ref[:ATTN_WIDTH, :], preferred_element_type=F32)
         + jnp.dot(m_ref[...], wo_ref[ATTN_WIDTH:, :], preferred_element_type=F32))
    h_ref[...] = h
    ms = jnp.mean(h * h, axis=-1, keepdims=True)
    n2 = ((h * lax.rsqrt(ms + EPS)) * g2_ref[...]).astype(BF16)
    n2_ref[...] = n2
    logits = jnp.dot(n2, wr_ref[...], preferred_element_type=F32)
    lane = lax.broadcasted_iota(jnp.int32, logits.shape, 1)
    big = jnp.int32(LANES)

    def masked_softmax(mask):
        zz = jnp.where(mask, logits, NEG)
        mx = jnp.max(zz, axis=-1, keepdims=True)
        ee = jnp.where(mask, jnp.exp(zz - mx), 0.0)
        return ee / jnp.sum(ee, axis=-1, keepdims=True)

    def first_argmax(vals, mask):
        v = jnp.max(jnp.where(mask, vals, -1.0), axis=-1, keepdims=True)
        idx = jnp.min(jnp.where(mask & (vals == v), lane, big), axis=-1, keepdims=True)
        return v, idx

    is_g = lane < N_EXPERT_GROUPS
    pg = masked_softmax(is_g)
    g_val, g_idx = first_argmax(pg, is_g)
    e_lane = lane - ROUTER_LANE0
    in_grp = (e_lane >= 0) & (e_lane < N_EXPERTS) & (e_lane // EXPERTS_PER_GROUP == g_idx)
    pe = masked_softmax(in_grp)
    v1, i1 = first_argmax(pe, in_grp)
    rest = in_grp & (lane != i1)
    v2, i2 = first_argmax(pe, rest)
    scale = g_val / (v1 + v2)
    gate_ref[...] = jnp.where(lane == i1, v1 * scale, jnp.where(lane == i2, v2 * scale, 0.0))


def _ffn(x2d, a2d, m2d, w_out16, g2, w_router16, tm):
    t = x2d.shape[0]
    row = lambda i: (i, 0)
    const = lambda i: (0, 0)
    return pl.pallas_call(
        _ffn_kernel,
        grid=(t // tm,),
        in_specs=[
            pl.BlockSpec((tm, D_MODEL), row),
            pl.BlockSpec((tm, ATTN_WIDTH), row),
            pl.BlockSpec((tm, POOL_WIDTH), row),
            pl.BlockSpec((D_MODEL, D_MODEL), const),
            pl.BlockSpec((1, D_MODEL), const),
            pl.BlockSpec((D_MODEL, LANES), const),
        ],
        out_specs=(pl.BlockSpec((tm, D_MODEL), row), pl.BlockSpec((tm, D_MODEL), row),
                   pl.BlockSpec((tm, LANES), row)),
        out_shape=(jax.ShapeDtypeStruct((t, D_MODEL), F32), jax.ShapeDtypeStruct((t, D_MODEL), BF16),
                   jax.ShapeDtypeStruct((t, LANES), F32)),
        compiler_params=pltpu.CompilerParams(
            dimension_semantics=("arbitrary",), vmem_limit_bytes=VMEM_LIMIT),
        name="ffn",
    )(x2d, a2d, m2d, w_out16, g2, w_router16)


def _moe_kernel(n2_ref, gate_ref, h_ref, wg_ref, wu_ref, wd_ref, y_ref):
    e = pl.program_id(1)

    @pl.when(e == 0)
    def _():
        y_ref[...] = h_ref[...]

    n2 = n2_ref[...]
    gu = jnp.dot(n2, wg_ref[0], preferred_element_type=F32)
    up = jnp.dot(n2, wu_ref[0], preferred_element_type=F32)
    he = (gu * (1.0 / (1.0 + jnp.exp(-gu)))) * up
    out = jnp.dot(he.astype(BF16), wd_ref[0], preferred_element_type=F32)
    lane = lax.broadcasted_iota(jnp.int32, gate_ref.shape, 1)
    gcol = jnp.sum(jnp.where(lane == e + ROUTER_LANE0, gate_ref[...], 0.0), axis=-1, keepdims=True)
    y_ref[...] += gcol * out


def _moe(n2, gate, h, wg16, wu16, wd16, tm):
    t = n2.shape[0]
    row = lambda i, e: (i, 0)
    return pl.pallas_call(
        _moe_kernel,
        grid=(t // tm, N_EXPERTS),
        in_specs=[
            pl.BlockSpec((tm, D_MODEL), row),
            pl.BlockSpec((tm, LANES), row),
            pl.BlockSpec((tm, D_MODEL), row),
            pl.BlockSpec((1, D_MODEL, D_EXPERT), lambda i, e: (e, 0, 0)),
            pl.BlockSpec((1, D_MODEL, D_EXPERT), lambda i, e: (e, 0, 0)),
            pl.BlockSpec((1, D_EXPERT, D_MODEL), lambda i, e: (e, 0, 0)),
        ],
        out_specs=pl.BlockSpec((tm, D_MODEL), row),
        out_shape=jax.ShapeDtypeStruct((t, D_MODEL), F32),
        compiler_params=pltpu.CompilerParams(
            dimension_semantics=("arbitrary", "arbitrary"), vmem_limit_bytes=VMEM_LIMIT),
        name="moe",
    )(n2, gate, h, wg16, wu16, wd16)


PROJ_TILE = 512
MOE_TILE = 1024
PROMPT_NPP = LANES // 2
SAMPLE_NPP = LANES


def _round_up(n, mult):
    return -(-n // mult) * mult


def _feature_major(x5):
    b, t = x5.shape[:2]
    return jnp.transpose(x5, (0, 2, 3, 4, 1)).reshape(b, KV_COLS, t)


def _token_major(xt):
    b, _, t = xt.shape
    return jnp.transpose(xt.reshape(b, 2, N_KV_HEADS, HEAD_DIM, t), (0, 4, 1, 2, 3))


def _pad_pages(x, npp_from, npp_to):
    b = x.shape[0]
    x = x.reshape(b, N_KV_HEADS, npp_from, LANES)
    return jnp.concatenate([x, jnp.zeros((b, N_KV_HEADS, npp_to - npp_from, LANES), x.dtype)], axis=2)


def _expand_const(n_chunks, npp):
    col = np.arange(2 * npp)
    blk = 2 * (col % npp) + col // npp
    tok_blk = np.arange(n_chunks * K_CHUNK) // CMP_BLOCK
    e = (blk[None, :, None] == tok_blk.reshape(n_chunks, 1, K_CHUNK)).astype(np.float32)
    return jnp.asarray(e, dtype=BF16)


def kernel(x_prompt, x_sample, cache_cmp_kv, cache_slc_kv, state_win_kv, state_pool, page_table, norm1_g, w_in, q_norm_g, k_norm_cmp_g, k_norm_slc_g, k_norm_win_g, cmp_pos_emb, w_cmp_k, w_cmp_v, pool_w, pool_scale, w_out, norm2_g, w_router_group, w_router_expert, w_gate, w_up, w_down):
    b, s, _ = x_prompt.shape
    db, ds, _ = x_sample.shape
    n_pool, page_rows = cache_cmp_kv.shape[:2]
    assert page_rows == PAGE and s % PAGE == 0
    past = page_table.shape[1] * page_rows

    kv0 = ATTN_WIDTH
    kv1 = kv0 + KV_ROWS
    w_row = jnp.concatenate(
        [w_in[:, :kv0], w_in[:, kv1:kv1 + GATE_COLS], jnp.zeros((D_MODEL, GATE_PAD - GATE_COLS), w_in.dtype),
         w_in[:, kv1 + GATE_COLS:]], axis=1).astype(BF16)
    w_kv = w_in[:, kv0:kv1].T.astype(BF16)
    g1 = norm1_g[None, :]
    g2 = norm2_g[None, :]
    two = lambda g: jnp.tile(g, 2)[None, :]
    kg = jnp.stack([k_norm_slc_g, k_norm_win_g])[:, :, None]
    pos_t = jnp.tile(cmp_pos_emb.T, (1, 2))
    zb = jnp.zeros((HEAD_DIM, CMP_BLOCK, HEAD_DIM), F32)

    def blockdiag(w):
        wt = jnp.transpose(w, (1, 0, 2))
        return jnp.concatenate([jnp.concatenate([wt, zb], axis=2), jnp.concatenate([zb, wt], axis=2)], axis=1)

    bd = jnp.stack([blockdiag(w_cmp_k), blockdiag(w_cmp_v)]).astype(BF16)
    w_router = jnp.concatenate(
        [w_router_group, w_router_expert,
         jnp.zeros((D_MODEL, LANES - N_EXPERT_GROUPS - N_EXPERTS), w_router_group.dtype)], axis=1).astype(BF16)
    w_out16 = w_out.astype(BF16)
    pool_w16 = pool_w.astype(BF16)
    wg16, wu16, wd16 = w_gate.astype(BF16), w_up.astype(BF16), w_down.astype(BF16)
    ps = pool_scale[None, :]

    def ffn_moe(x2d, a2d, m2d, tm_ffn, tm_moe):
        h, n2, gate = _ffn(x2d, a2d, m2d, w_out16, g2, w_router, tm_ffn)
        return _moe(n2, gate, h, wg16, wu16, wd16, tm_moe)

    q, gates, u, kvc_t, kvs_t, kvw_t, kvs16, kvw16 = _proj(
        x_prompt, g1, w_row, w_kv, two(q_norm_g), kg, PROJ_TILE)
    m_p = _pool(u, pool_w16, ps)
    pages_p = s // PAGE
    pt_p = jnp.zeros((b, pages_p), jnp.int32)
    dummy_tail = jnp.zeros((b, KV_COLS, LANES), F32)
    kc_p, vc_p = _compress(pt_p, kvc_t, dummy_tail, pos_t, bd, two(k_norm_cmp_g),
                           npp=pages_p, has_tail=False, paged=False)
    kc_p = _pad_pages(kc_p, pages_p, PROMPT_NPP)
    vc_p = _pad_pages(vc_p, pages_p, PROMPT_NPP)
    a_p = _attn_prompt(q, gates, kc_p, vc_p, kvs16, kvw16, _expand_const(s // K_CHUNK, PROMPT_NPP), pages_p)
    y_p = ffn_moe(x_prompt.reshape(b * s, D_MODEL), a_p.reshape(b * s, ATTN_WIDTH),
                  m_p.reshape(b * s, POOL_WIDTH), PROJ_TILE, MOE_TILE)

    ts = db * ds
    q_s, gates_s, u_s, kvc_st, kvs_st, kvw_st, _, _ = _proj(
        x_sample.reshape(1, ts, D_MODEL), g1, w_row, w_kv, two(q_norm_g), kg, ts)
    u_s = u_s.reshape(db, ds, POOL_WIDTH)
    pool_ext = jnp.concatenate([state_pool, u_s], axis=1)
    n_ext = pool_ext.shape[1]
    lead = _round_up(n_ext, 8) - n_ext
    pool_in = jnp.concatenate([jnp.zeros((db, lead, POOL_WIDTH), F32), pool_ext], axis=1)
    m_s = _pool(pool_in, pool_w16, ps)[:, lead + n_ext - ds:]

    def new_rows_t(xt):
        x = jnp.transpose(xt.reshape(KV_COLS, db, ds), (1, 0, 2))
        return jnp.concatenate([x, jnp.zeros((db, KV_COLS, LANES - ds), F32)], axis=2)

    kvc_new, kvs_new, kvw_new = new_rows_t(kvc_st), new_rows_t(kvs_st), new_rows_t(kvw_st)
    n_pages = page_table.shape[1]
    npp_c = _round_up(n_pages + 1, 8)
    cmp_pages = _feature_major(cache_cmp_kv)
    slc_pages = _feature_major(cache_slc_kv)
    kc_s, vc_s = _compress(page_table, cmp_pages, kvc_new, pos_t, bd, two(k_norm_cmp_g),
                           npp=npp_c, has_tail=True, paged=True)
    kc_s = _pad_pages(kc_s, npp_c, SAMPLE_NPP)
    vc_s = _pad_pages(vc_s, npp_c, SAMPLE_NPP)
    win_t = _feature_major(state_win_kv)
    a_s = _attn_sample(page_table, q_s.reshape(db, ds, ATTN_WIDTH), gates_s.reshape(db, ds, GATE_COLS),
                       kc_s, vc_s, kvs_new, kvw_new, win_t, slc_pages, past=past)
    y_s = ffn_moe(x_sample.reshape(ts, D_MODEL), a_s.reshape(ts, ATTN_WIDTH), m_s.reshape(ts, POOL_WIDTH), ts, ts)

    win_keep = min(WINDOW, s)
    sample5 = lambda xt: jnp.transpose(xt.reshape(2, N_KV_HEADS, HEAD_DIM, db, ds), (3, 4, 0, 1, 2))
    win_ctx_t = jnp.concatenate([win_t, kvw_new[:, :, :ds]], axis=2)
    return (y_p.reshape(b, s, D_MODEL), y_s.reshape(db, ds, D_MODEL),
            _token_major(kvc_t), sample5(kvc_st),
            _token_major(kvs_t), sample5(kvs_st),
            _token_major(kvw_t[:, :, s - win_keep:]), _token_major(win_ctx_t[:, :, ds:]),
            u[:, s - POOL_BUF:], pool_ext[:, ds:])
```

```python
import functools

import jax
import jax.numpy as jnp
import numpy as np
from jax import lax
from jax.experimental import pallas as pl
from jax.experimental.pallas import tpu as pltpu

F32 = jnp.float32
BF16 = jnp.bfloat16

D_MODEL = 1024
N_HEADS = 8
HEAD_DIM = 64
N_KV_HEADS = 2
Q_PER_KV = N_HEADS // N_KV_HEADS
ATTN_WIDTH = N_HEADS * HEAD_DIM
KV_COLS = 2 * N_KV_HEADS * HEAD_DIM
GATE_COLS = 3 * N_HEADS
POOL_WIDTH = D_MODEL - ATTN_WIDTH
POOL_WINDOWS = (2, 4, 8, 16)
POOL_GROUP_WIDTH = POOL_WIDTH // len(POOL_WINDOWS)
POOL_BUF = max(POOL_WINDOWS) - 1
CMP_BLOCK = 64
TOP_K_BLOCKS = 16
WINDOW = 512
FORCE_SCORE = 1.0e4
N_EXPERT_GROUPS = 4
EXPERTS_PER_GROUP = 4
N_EXPERTS = N_EXPERT_GROUPS * EXPERTS_PER_GROUP
D_EXPERT = 512
EPS = 1e-6
NEG = -1e30
SCALE = HEAD_DIM ** -0.5

LANES = 128
PAGE = 2 * CMP_BLOCK
GATE_PAD = LANES
ROW_COLS = ATTN_WIDTH + GATE_PAD + POOL_WIDTH
OFF_GATE = ATTN_WIDTH
OFF_U = OFF_GATE + GATE_PAD
KV_ROWS = 3 * KV_COLS
ROUTER_LANE0 = N_EXPERT_GROUPS
Q_TILE = 256
K_CHUNK = 256
VMEM_LIMIT = 56 * 1024 * 1024


def _slope(h):
    return float(2.0 ** (-8.0 * (h + 1) / N_HEADS))


def _half_group_norm(v, g):
    lane = lax.broadcasted_iota(jnp.int32, v.shape, 1)
    lo = lane < HEAD_DIM
    v2 = v * v
    s_lo = jnp.sum(jnp.where(lo, v2, 0.0), axis=-1, keepdims=True)
    s_hi = jnp.sum(jnp.where(lo, 0.0, v2), axis=-1, keepdims=True)
    r_lo = lax.rsqrt(s_lo * (1.0 / HEAD_DIM) + EPS)
    r_hi = lax.rsqrt(s_hi * (1.0 / HEAD_DIM) + EPS)
    return (v * jnp.where(lo, r_lo, r_hi)) * g


def _dot_t(a, b):
    return lax.dot_general(a, b, (((1,), (1,)), ((), ())), preferred_element_type=F32)


def _proj_kernel(x_ref, g1_ref, wr_ref, wkv_ref, qg_ref, kg_ref,
                 q_ref, gates_ref, u_ref, kvc_ref, kvs_ref, kvw_ref, ks16_ref, vsx_ref, kw16_ref, vwx_ref):
    x = x_ref[0]
    tm = x.shape[0]
    ms = jnp.mean(x * x, axis=-1, keepdims=True)
    n = ((x * lax.rsqrt(ms + EPS)) * g1_ref[...]).astype(BF16)
    p = jnp.dot(n, wr_ref[...], preferred_element_type=F32)
    pt = _dot_t(wkv_ref[...], n)
    for t in range(ATTN_WIDTH // LANES):
        sl = slice(t * LANES, (t + 1) * LANES)
        q_ref[0, :, sl] = _half_group_norm(p[:, sl], qg_ref[...]).astype(BF16)
    gates_ref[0] = 1.0 / (1.0 + jnp.exp(-p[:, OFF_GATE:OFF_GATE + GATE_PAD]))
    u_ref[0] = p[:, OFF_U:OFF_U + POOL_WIDTH]
    kvc_ref[0] = pt[:KV_COLS]
    half = KV_COLS // 2
    lane = lax.broadcasted_iota(jnp.int32, (tm, LANES), 1)
    for bi, (out32, k16, vx16) in enumerate(((kvs_ref, ks16_ref, vsx_ref), (kvw_ref, kw16_ref, vwx_ref))):
        off = (bi + 1) * KV_COLS
        heads = []
        for hh in range(N_KV_HEADS):
            kh = pt[off + hh * HEAD_DIM:off + (hh + 1) * HEAD_DIM]
            msk = jnp.mean(kh * kh, axis=0, keepdims=True)
            heads.append((kh * lax.rsqrt(msk + EPS)) * kg_ref[bi])
        kn = jnp.concatenate(heads, axis=0)
        v = pt[off + half:off + KV_COLS]
        out32[0, :half, :] = kn
        out32[0, half:, :] = v
        vt = v.T
        vx = (jnp.where(lane < HEAD_DIM, vt, 1.0), jnp.where(lane < HEAD_DIM, pltpu.roll(vt, HEAD_DIM, axis=1), 1.0))
        for c in range(tm // K_CHUNK):
            cs = slice(c * K_CHUNK, (c + 1) * K_CHUNK)
            k16[0, c] = kn[:, cs].astype(BF16)
            for hh in range(N_KV_HEADS):
                vx16[0, c, hh] = vx[hh][cs].astype(BF16)


def _proj(x3d, g1, w_row, w_kv, qg, kg, tm):
    b, s, _ = x3d.shape
    tok = lambda i, j: (i, j, 0)
    feat = lambda i, j: (i, 0, j)
    const2 = lambda i, j: (0, 0)
    nck = tm // K_CHUNK
    out_shape = (
        jax.ShapeDtypeStruct((b, s, ATTN_WIDTH), BF16),
        jax.ShapeDtypeStruct((b, s, GATE_PAD), F32),
        jax.ShapeDtypeStruct((b, s, POOL_WIDTH), F32),
        jax.ShapeDtypeStruct((b, KV_COLS, s), F32),
        jax.ShapeDtypeStruct((b, KV_COLS, s), F32),
        jax.ShapeDtypeStruct((b, KV_COLS, s), F32),
        jax.ShapeDtypeStruct((b, s // K_CHUNK, LANES, K_CHUNK), BF16),
        jax.ShapeDtypeStruct((b, s // K_CHUNK, N_KV_HEADS, K_CHUNK, LANES), BF16),
        jax.ShapeDtypeStruct((b, s // K_CHUNK, LANES, K_CHUNK), BF16),
        jax.ShapeDtypeStruct((b, s // K_CHUNK, N_KV_HEADS, K_CHUNK, LANES), BF16),
    )
    k_spec = pl.BlockSpec((1, nck, LANES, K_CHUNK), lambda i, j: (i, j, 0, 0))
    v_spec = pl.BlockSpec((1, nck, N_KV_HEADS, K_CHUNK, LANES), lambda i, j: (i, j, 0, 0, 0))
    out_specs = (
        pl.BlockSpec((1, tm, ATTN_WIDTH), tok),
        pl.BlockSpec((1, tm, GATE_PAD), tok),
        pl.BlockSpec((1, tm, POOL_WIDTH), tok),
        pl.BlockSpec((1, KV_COLS, tm), feat),
        pl.BlockSpec((1, KV_COLS, tm), feat),
        pl.BlockSpec((1, KV_COLS, tm), feat),
        k_spec, v_spec, k_spec, v_spec,
    )
    return pl.pallas_call(
        _proj_kernel,
        grid=(b, s // tm),
        in_specs=[
            pl.BlockSpec((1, tm, D_MODEL), tok),
            pl.BlockSpec((1, D_MODEL), const2),
            pl.BlockSpec((D_MODEL, ROW_COLS), const2),
            pl.BlockSpec((KV_ROWS, D_MODEL), const2),
            pl.BlockSpec((1, LANES), const2),
            pl.BlockSpec((2, HEAD_DIM, 1), lambda i, j: (0, 0, 0)),
        ],
        out_specs=out_specs,
        out_shape=out_shape,
        compiler_params=pltpu.CompilerParams(
            dimension_semantics=("arbitrary", "arbitrary"), vmem_limit_bytes=VMEM_LIMIT),
        name="proj",
    )(x3d, g1, w_row, w_kv, qg, kg)


def _pool_kernel(u_ref, pw_ref, ps_ref, m_ref):
    u = u_ref[0]
    n = u.shape[0]

    def shift(v, k):
        rolled = pltpu.roll(v, k, axis=0)
        r = lax.broadcasted_iota(jnp.int32, v.shape, 0)
        return jnp.where(r >= k, rolled, 0.0)

    sums = []
    s = u
    k = 1
    for gi in range(len(POOL_WINDOWS)):
        s = s + shift(s, k)
        k *= 2
        sums.append(s[:, :POOL_GROUP_WIDTH])
        s = s[:, POOL_GROUP_WIDTH:]
    row = lax.broadcasted_iota(jnp.int32, (n, 1), 0)
    for gi, w in enumerate(POOL_WINDOWS):
        sl = slice(gi * POOL_GROUP_WIDTH, (gi + 1) * POOL_GROUP_WIDTH)
        cnt = jnp.minimum(row + 1, w).astype(F32)
        d = sums[gi] / cnt - u[:, sl]
        y = jnp.dot(d.astype(BF16), pw_ref[gi], preferred_element_type=F32)
        m_ref[0, :, sl] = (y * ps_ref[:, sl]).astype(BF16)


def _pool(u3d, pool_w16, pool_scale):
    b, n, _ = u3d.shape
    return pl.pallas_call(
        _pool_kernel,
        grid=(b,),
        in_specs=[
            pl.BlockSpec((1, n, POOL_WIDTH), lambda i: (i, 0, 0)),
            pl.BlockSpec((len(POOL_WINDOWS), POOL_GROUP_WIDTH, POOL_GROUP_WIDTH), lambda i: (0, 0, 0)),
            pl.BlockSpec((1, POOL_WIDTH), lambda i: (0, 0)),
        ],
        out_specs=pl.BlockSpec((1, n, POOL_WIDTH), lambda i: (i, 0, 0)),
        out_shape=jax.ShapeDtypeStruct((b, n, POOL_WIDTH), BF16),
        compiler_params=pltpu.CompilerParams(
            dimension_semantics=("arbitrary",), vmem_limit_bytes=VMEM_LIMIT),
        name="pool",
    )(u3d, pool_w16, pool_scale)


def _compress_kernel(pt_ref, src_ref, tail_ref, pos_ref, bd_ref, g_ref, kc_ref, vc_ref, buf, sem,
                     *, n_pages, npp, has_tail, paged):
    b = pl.program_id(0)
    nb = pl.num_programs(0)
    n_slabs = 2 * N_KV_HEADS

    def slab_copy(row, p, c, slot):
        kv, kvh = divmod(c, N_KV_HEADS)
        if paged:
            src = src_ref.at[pt_ref[row, p], pl.ds(c * HEAD_DIM, HEAD_DIM), :]
        else:
            src = src_ref.at[row, pl.ds(c * HEAD_DIM, HEAD_DIM), pl.ds(p * PAGE, PAGE)]
        return pltpu.make_async_copy(src, buf.at[slot, kv, :, kvh * npp + p, :], sem.at[slot])

    def tail_copy(c, slot):
        kv, kvh = divmod(c, N_KV_HEADS)
        return pltpu.make_async_copy(tail_ref.at[0, pl.ds(c * HEAD_DIM, HEAD_DIM), :],
                                     buf.at[slot, kv, :, kvh * npp + n_pages, :], sem.at[slot])

    def row_copies(row, slot, fn):
        def body(p, carry):
            for c in range(n_slabs):
                fn(slab_copy(row, p, c, slot))
            return carry
        lax.fori_loop(0, n_pages, body, 0)

    n_real = n_pages + (1 if has_tail else 0)

    @pl.when(b == 0)
    def _():
        if npp > n_real:
            for kvh in range(N_KV_HEADS):
                buf[:, :, :, kvh * npp + n_real:(kvh + 1) * npp, :] = jnp.zeros(
                    (2, 2, HEAD_DIM, npp - n_real, LANES), F32)
        row_copies(0, 0, lambda cp: cp.start())

    slot = b % 2

    @pl.when(b + 1 < nb)
    def _():
        row_copies(b + 1, 1 - slot, lambda cp: cp.start())

    if has_tail:
        for c in range(n_slabs):
            tail_copy(c, slot).start()
        for c in range(n_slabs):
            tail_copy(c, slot).wait()
    row_copies(b, slot, lambda cp: cp.wait())

    rows = N_KV_HEADS * npp

    def body(d, carry):
        acc_k, acc_v = carry
        pos = pos_ref[pl.ds(d, 1), :]
        xk = (buf[slot, 0, d] + pos).astype(BF16)
        xv = (buf[slot, 1, d] + pos).astype(BF16)
        return (acc_k + jnp.dot(xk, bd_ref[0, d], preferred_element_type=F32),
                acc_v + jnp.dot(xv, bd_ref[1, d], preferred_element_type=F32))

    zero = jnp.zeros((rows, LANES), F32)
    acc_k, acc_v = lax.fori_loop(0, HEAD_DIM, body, (zero, zero), unroll=8)
    kc_ref[0] = _half_group_norm(acc_k, g_ref[...])
    vc_ref[0] = acc_v


def _compress(page_table, src, tail, pos_t, bd, g2, *, npp, has_tail, paged):
    b, n_pages = page_table.shape
    kern = functools.partial(_compress_kernel, n_pages=n_pages, npp=npp, has_tail=has_tail, paged=paged)
    rows = N_KV_HEADS * npp
    grid_spec = pltpu.PrefetchScalarGridSpec(
        num_scalar_prefetch=1,
        grid=(b,),
        in_specs=[
            pl.BlockSpec(memory_space=pl.ANY),
            pl.BlockSpec((1, KV_COLS, LANES), lambda i, pt: (i, 0, 0)),
            pl.BlockSpec((HEAD_DIM, LANES), lambda i, pt: (0, 0)),
            pl.BlockSpec((2, HEAD_DIM, LANES, LANES), lambda i, pt: (0, 0, 0, 0)),
            pl.BlockSpec((1, LANES), lambda i, pt: (0, 0)),
        ],
        out_specs=(pl.BlockSpec((1, rows, LANES), lambda i, pt: (i, 0, 0)),
                   pl.BlockSpec((1, rows, LANES), lambda i, pt: (i, 0, 0))),
        scratch_shapes=[
            pltpu.VMEM((2, 2, HEAD_DIM, rows, LANES), F32),
            pltpu.SemaphoreType.DMA((2,)),
        ],
    )
    return pl.pallas_call(
        kern,
        grid_spec=grid_spec,
        out_shape=(jax.ShapeDtypeStruct((b, rows, LANES), F32), jax.ShapeDtypeStruct((b, rows, LANES), F32)),
        compiler_params=pltpu.CompilerParams(
            dimension_semantics=("arbitrary",), vmem_limit_bytes=VMEM_LIMIT),
        name="compress",
    )(page_table, src, tail, pos_t, bd, g2)


def _to_half(tile, src_half, dst_half):
    lane = lax.broadcasted_iota(jnp.int32, tile.shape, 1)
    src = tile if src_half == dst_half else pltpu.roll(tile, HEAD_DIM, axis=1)
    keep = (lane < HEAD_DIM) if dst_half == 0 else (lane >= HEAD_DIM)
    return jnp.where(keep, src, 0.0)


def _pair_tile(o_even, o_odd, half):
    lane = lax.broadcasted_iota(jnp.int32, o_even.shape, 1)
    if half == 0:
        return jnp.where(lane < HEAD_DIM, o_even, pltpu.roll(o_odd, HEAD_DIM, axis=1))
    return jnp.where(lane < HEAD_DIM, pltpu.roll(o_even, HEAD_DIM, axis=1), o_odd)


def _gate_tile(gates, pair, c, shape):
    lane = lax.broadcasted_iota(jnp.int32, shape, 1)
    he, ho = 2 * pair, 2 * pair + 1
    return jnp.where(lane < HEAD_DIM, gates[:, 3 * he + c:3 * he + c + 1], gates[:, 3 * ho + c:3 * ho + c + 1])


def _block_of_col(col, npp):
    return 2 * (col % npp) + col // npp


def _cmp_operand(x, npp):
    return jnp.concatenate([_to_half(x, 0, 0), _to_half(x, 1, 0)], axis=0).astype(BF16)


def _rank_select(score_t, blk_t, cols):
    rank = jnp.zeros(score_t.shape, jnp.int32)
    for r, n in cols:
        row = score_t[r:r + 1, :]
        ahead = (row > score_t) | ((row == score_t) & (blk_t > n))
        rank = rank + ahead.astype(jnp.int32)
    return ((rank < TOP_K_BLOCKS) & (score_t > -0.5)).astype(F32)


POS_HI, POS_LO = HEAD_DIM, HEAD_DIM + 1
MASK_BIG = 1e30


def _attn_prompt_kernel(q_ref, gates_ref, kc_ref, vc_ref, ks_ref, vsx_ref, kw_ref, vwx_ref, exp_ref, gp_ref, a_ref,
                        zbuf, m_sc, acc_sc, *, n_pages, npp):
    i = pl.program_id(1)
    q0 = i * Q_TILE
    rows = Q_PER_KV * Q_TILE
    ncols = 2 * npp
    qpos = q0 + lax.broadcasted_iota(jnp.int32, (Q_TILE, 1), 0)
    blk = _block_of_col(lax.broadcasted_iota(jnp.int32, (Q_TILE, ncols), 1), npp)
    real_cols = [(half * n_pages + p, 2 * p + half) for half in range(2) for p in range(n_pages)]

    g = gates_ref[0]
    g_hi = g.astype(BF16)
    g_lo = (g - g_hi.astype(F32)).astype(BF16)
    gexp = (jnp.dot(g_hi, gp_ref[...], preferred_element_type=F32)
            + jnp.dot(g_lo, gp_ref[...], preferred_element_type=F32))

    r_key = lax.broadcasted_iota(jnp.int32, (Q_TILE, K_CHUNK), 1)
    r_qry = lax.broadcasted_iota(jnp.int32, (Q_TILE, K_CHUNK), 0)
    keep = {"causal": r_key <= r_qry, "lower": r_key > r_qry}

    all_rows = N_HEADS * Q_TILE

    def branch(lhs, k_ref, vx_ref, slc, chunks, c_lo, c_hi):
        m_sc[...] = jnp.full((all_rows, LANES), NEG, F32)

        def scores(c, kind):
            rr = lax.broadcasted_iota(jnp.int32, (HEAD_DIM, K_CHUNK), 0)
            tt = lax.broadcasted_iota(jnp.int32, (HEAD_DIM, K_CHUNK), 1).astype(F32)
            hi = ((c - i) * K_CHUNK).astype(F32)
            pos = jnp.where(rr == 0, hi, jnp.where(rr == 1, tt, 0.0)).astype(BF16)
            for kvh in range(N_KV_HEADS):
                parts = [k_ref[0, c, kvh * HEAD_DIM:(kvh + 1) * HEAD_DIM, :], pos]
                if slc:
                    parts.append(exp_ref[c])
                kx = jnp.concatenate(parts, axis=0)
                for gq in range(Q_PER_KV):
                    sl = slice((kvh * Q_PER_KV + gq) * Q_TILE, (kvh * Q_PER_KV + gq + 1) * Q_TILE)
                    zg = jnp.dot(lhs[sl], kx, preferred_element_type=F32)
                    if kind is not None:
                        zg = jnp.where(keep[kind], zg, NEG)
                    zbuf[c, sl, :] = zg
                    m_sc[sl] = jnp.maximum(m_sc[sl], jnp.maximum(zg[:, :LANES], zg[:, LANES:]))

        for lo, hi_, kind, cond in chunks:
            if kind is None and cond is None:
                def body(c, carry):
                    scores(c, None)
                    return carry
                lax.fori_loop(lo, hi_, body, 0)
            elif cond is None:
                scores(lo, kind)
            else:
                pl.when(cond)(functools.partial(scores, lo, kind))

        m = jnp.broadcast_to(jnp.max(m_sc[...], axis=-1, keepdims=True), (all_rows, LANES))
        acc_sc[...] = jnp.zeros((all_rows, LANES), F32)

        def pass2(c, carry):
            for h in range(N_HEADS):
                sl = slice(h * Q_TILE, (h + 1) * Q_TILE)
                z = zbuf[c, sl, :]
                e = jnp.concatenate([jnp.exp(z[:, :LANES] - m[sl]), jnp.exp(z[:, LANES:] - m[sl])], axis=1)
                acc_sc[sl] += jnp.dot(e.astype(BF16), vx_ref[0, c, h // Q_PER_KV], preferred_element_type=F32)
            return carry

        lax.fori_loop(c_lo, c_hi, pass2, 0)
        acc = acc_sc[...]
        den = pltpu.roll(acc, HEAD_DIM, axis=1)
        return acc / jnp.where(den > 0.0, den, 1.0)

    o_cmp_all, lhs_slc_all, lhs_win_all = [], [], []
    for kvh in range(N_KV_HEADS):
        slopes = [_slope(kvh * Q_PER_KV + g) for g in range(Q_PER_KV)]
        lane = lax.broadcasted_iota(jnp.int32, (Q_TILE, LANES), 1)
        pieces, pieces_x = [], []
        for gq in range(Q_PER_KV):
            h = kvh * Q_PER_KV + gq
            tile = q_ref[0, :, (h // 2) * LANES:(h // 2 + 1) * LANES].astype(F32) * SCALE
            low = _to_half(tile, h % 2, 0)
            pieces.append(low)
            pieces_x.append(jnp.where((lane == POS_HI) | (lane == POS_LO), slopes[gq], low))
        qk = jnp.concatenate(pieces, axis=0).astype(BF16)
        qk_x = jnp.concatenate(pieces_x, axis=0).astype(BF16)

        kc = _cmp_operand(kc_ref[0, kvh], npp)
        vc = _cmp_operand(vc_ref[0, kvh], npp)
        s = _dot_t(qk, kc)
        dist_c = qpos - (blk * CMP_BLOCK + (CMP_BLOCK - 1))
        mask_c = dist_c >= 0
        dist_cf = dist_c.astype(F32)
        p_list = []
        p_kv = jnp.zeros((Q_TILE, ncols), F32)
        for g in range(Q_PER_KV):
            z = jnp.where(mask_c, s[g * Q_TILE:(g + 1) * Q_TILE] - slopes[g] * dist_cf, NEG)
            mx = jnp.max(z, axis=-1, keepdims=True)
            e = jnp.where(mask_c, jnp.exp(z - mx), 0.0)
            den = jnp.sum(e, axis=-1, keepdims=True)
            p = e / jnp.where(den > 0.0, den, 1.0)
            p_kv = p_kv + p
            p_list.append(p.astype(BF16))
        o_cmp = jnp.dot(jnp.concatenate(p_list, axis=0), vc, preferred_element_type=F32)

        cand = blk * CMP_BLOCK <= qpos
        forced = (blk == qpos // CMP_BLOCK) | (blk == 0)
        score = jnp.where(forced, FORCE_SCORE, jnp.where(cand, p_kv, -1.0))
        score_t = score.T
        st = jnp.concatenate([score_t[:n_pages], score_t[npp:npp + n_pages]], axis=0)
        rr = lax.broadcasted_iota(jnp.int32, (2 * n_pages, Q_TILE), 0)
        bt = jnp.where(rr < n_pages, 2 * rr, 2 * (rr - n_pages) + 1)
        sel_s = _rank_select(st, bt, [(r, n) for r, (_, n) in enumerate(real_cols)])
        pad = jnp.zeros((npp - n_pages, Q_TILE), F32)
        sel_t = jnp.concatenate([sel_s[:n_pages], pad, sel_s[n_pages:], pad], axis=0)
        sel = sel_t.T
        sel_bias = jnp.where(sel > 0.5, 0.0, -MASK_BIG).astype(BF16)
        lhs_slc_all.append(jnp.concatenate([qk_x, jnp.concatenate([sel_bias] * Q_PER_KV, axis=0)], axis=1))
        lhs_win_all.append(qk_x)
        o_cmp_all.append(o_cmp)

    o_slc = branch(jnp.concatenate(lhs_slc_all, axis=0), ks_ref, vsx_ref, True,
                   [(0, i, None, None), (i, None, "causal", None)], 0, i + 1)
    o_win = branch(jnp.concatenate(lhs_win_all, axis=0), kw_ref, vwx_ref, False,
                   [(i - 2, None, "lower", i >= 2), (i - 1, None, None, i >= 1), (i, None, "causal", None)],
                   jnp.maximum(i - 2, 0), i + 1)
    o_cmp = jnp.concatenate(o_cmp_all, axis=0)

    for pair in range(N_HEADS // 2):
        e_sl = slice(2 * pair * Q_TILE, (2 * pair + 1) * Q_TILE)
        o_sl = slice((2 * pair + 1) * Q_TILE, (2 * pair + 2) * Q_TILE)
        tile = jnp.zeros((Q_TILE, LANES), F32)
        for br, o in enumerate((o_cmp, o_slc, o_win)):
            col = br * ATTN_WIDTH + pair * LANES
            tile = tile + gexp[:, col:col + LANES] * _pair_tile(o[e_sl], o[o_sl], 0)
        a_ref[0, :, pair * LANES:(pair + 1) * LANES] = tile.astype(BF16)


def _attn_prompt(q, gates, kc, vc, ks16, vsx, kw16, vwx, expand, gate_place, n_pages):
    b, s, _ = q.shape
    npp = kc.shape[2]
    n_chunks = s // K_CHUNK
    rows = Q_PER_KV * Q_TILE
    assert WINDOW == 2 * K_CHUNK and Q_TILE == K_CHUNK
    kern = functools.partial(_attn_prompt_kernel, n_pages=n_pages, npp=npp)
    per_b4 = lambda bi, i: (bi, 0, 0, 0)
    per_b5 = lambda bi, i: (bi, 0, 0, 0, 0)
    return pl.pallas_call(
        kern,
        grid=(b, s // Q_TILE),
        in_specs=[
            pl.BlockSpec((1, Q_TILE, ATTN_WIDTH), lambda bi, i: (bi, i, 0)),
            pl.BlockSpec((1, Q_TILE, GATE_PAD), lambda bi, i: (bi, i, 0)),
            pl.BlockSpec((1, N_KV_HEADS, npp, LANES), per_b4),
            pl.BlockSpec((1, N_KV_HEADS, npp, LANES), per_b4),
            pl.BlockSpec((1, n_chunks, LANES, K_CHUNK), per_b4),
            pl.BlockSpec((1, n_chunks, N_KV_HEADS, K_CHUNK, LANES), per_b5),
            pl.BlockSpec((1, n_chunks, LANES, K_CHUNK), per_b4),
            pl.BlockSpec((1, n_chunks, N_KV_HEADS, K_CHUNK, LANES), per_b5),
            pl.BlockSpec((n_chunks, 2 * npp, K_CHUNK), lambda bi, i: (0, 0, 0)),
            pl.BlockSpec((GATE_PAD, 3 * ATTN_WIDTH), lambda bi, i: (0, 0)),
        ],
        out_specs=pl.BlockSpec((1, Q_TILE, ATTN_WIDTH), lambda bi, i: (bi, i, 0)),
        out_shape=jax.ShapeDtypeStruct((b, s, ATTN_WIDTH), BF16),
        scratch_shapes=[
            pltpu.VMEM((n_chunks, N_KV_HEADS * rows, K_CHUNK), F32),
            pltpu.VMEM((N_KV_HEADS * rows, LANES), F32),
            pltpu.VMEM((N_KV_HEADS * rows, LANES), F32),
        ],
        compiler_params=pltpu.CompilerParams(
            dimension_semantics=("arbitrary", "arbitrary"), vmem_limit_bytes=VMEM_LIMIT),
        name="attn_prompt",
    )(q, gates, kc, vc, ks16, vsx, kw16, vwx, expand, gate_place)


S_ROWS = LANES
S_CHUNK = 1024


def _attn_sample_kernel(pt_ref, q_ref, gates_ref, kc_ref, vc_ref, kvs_new_ref, kvw_new_ref, win_ref, cache_ref,
                        a_ref, kbuf, zbuf, sem, *, n_pages, past, ds, npp):
    b = pl.program_id(0)
    nb = pl.num_programs(0)
    n_keys = n_pages * PAGE
    n_real = N_HEADS * ds
    ncols = 2 * npp
    half_cols = KV_COLS // 2
    n_blocks = (past + ds + CMP_BLOCK - 1) // CMP_BLOCK

    def page_copy(row, p, slot):
        return pltpu.make_async_copy(
            cache_ref.at[pt_ref[row, p]], kbuf.at[slot, :, pl.ds(p * PAGE, PAGE)], sem.at[slot])

    def row_copies(row, slot, fn):
        def body(p, carry):
            fn(page_copy(row, p, slot))
            return carry
        lax.fori_loop(0, n_pages, body, 0)

    @pl.when(b == 0)
    def _():
        row_copies(0, 0, lambda cp: cp.start())

    slot = b % 2

    @pl.when(b + 1 < nb)
    def _():
        row_copies(b + 1, 1 - slot, lambda cp: cp.start())

    by_kvh, low = [], []
    for h in range(N_HEADS):
        tile = q_ref[0, :, (h // 2) * LANES:(h // 2 + 1) * LANES].astype(F32) * SCALE
        by_kvh.append(_to_half(tile, h % 2, h // Q_PER_KV))
        low.append(_to_half(tile, h % 2, 0))
    zpad = jnp.zeros((S_ROWS - n_real, LANES), F32)
    qrows = jnp.concatenate(by_kvh + [zpad], axis=0).astype(BF16)
    qlow = jnp.concatenate(low + [zpad], axis=0).astype(BF16)

    r_col = lax.broadcasted_iota(jnp.int32, (S_ROWS, 1), 0)
    q_of_r = r_col % ds
    h_of_r = r_col // ds
    qp_r = past + q_of_r
    sl_r = jnp.zeros((S_ROWS, 1), F32)
    for h in range(N_HEADS):
        sl_r = jnp.where(h_of_r == h, _slope(h), sl_r)
    kvh_r = h_of_r // Q_PER_KV

    blk = _block_of_col(lax.broadcasted_iota(jnp.int32, (S_ROWS, ncols), 1), npp)
    s = jnp.where(kvh_r == 0, _dot_t(qlow, _cmp_operand(kc_ref[0, 0], npp)),
                  _dot_t(qlow, _cmp_operand(kc_ref[0, 1], npp)))
    dist_c = qp_r - (blk * CMP_BLOCK + (CMP_BLOCK - 1))
    mask_c = dist_c >= 0
    z = jnp.where(mask_c, s - sl_r * dist_c.astype(F32), NEG)
    mx = jnp.max(z, axis=-1, keepdims=True)
    e = jnp.where(mask_c, jnp.exp(z - mx), 0.0)
    den = jnp.sum(e, axis=-1, keepdims=True)
    p = e / jnp.where(den > 0.0, den, 1.0)
    pb = p.astype(BF16)
    o_cmp = jnp.where(kvh_r == 0, jnp.dot(pb, _cmp_operand(vc_ref[0, 0], npp), preferred_element_type=F32),
                      jnp.dot(pb, _cmp_operand(vc_ref[0, 1], npp), preferred_element_type=F32))

    kq = N_KV_HEADS * ds
    p_kv = []
    for kvh in range(N_KV_HEADS):
        acc = jnp.zeros((ds, ncols), F32)
        for g in range(Q_PER_KV):
            r0 = (kvh * Q_PER_KV + g) * ds
            acc = acc + p[r0:r0 + ds]
        p_kv.append(acc)
    p_kv = jnp.concatenate(p_kv, axis=0)
    blk2 = _block_of_col(lax.broadcasted_iota(jnp.int32, (kq, ncols), 1), npp)
    qp2 = past + lax.broadcasted_iota(jnp.int32, (kq, 1), 0) % ds
    cand = blk2 * CMP_BLOCK <= qp2
    forced = (blk2 == qp2 // CMP_BLOCK) | (blk2 == 0)
    score = jnp.where(forced, FORCE_SCORE, jnp.where(cand, p_kv, -1.0))
    rank = jnp.zeros((kq, ncols), jnp.int32)
    for n in range(n_blocks):
        c = (n % 2) * npp + n // 2
        col = score[:, c:c + 1]
        ahead = (col > score) | ((col == score) & (blk2 > n))
        rank = rank + ahead.astype(jnp.int32)
    sel2 = ((rank < TOP_K_BLOCKS) & (score > -0.5)).astype(F32)
    sel_rows = []
    for h in range(N_HEADS):
        kvh = h // Q_PER_KV
        sel_rows.append(sel2[kvh * ds:(kvh + 1) * ds])
    sel_rows.append(jnp.zeros((S_ROWS - n_real, ncols), F32))
    sel_rows = jnp.concatenate(sel_rows, axis=0)
    sel16 = sel_rows.astype(BF16)

    new_lane = lax.broadcasted_iota(jnp.int32, (1, LANES), 1)

    def new_scores(ref, extra_mask):
        k_new = ref[0, :half_cols, :].astype(BF16)
        s_new = jnp.dot(qrows, k_new, preferred_element_type=F32)
        dist = q_of_r - new_lane
        mask = (new_lane < ds) & (dist >= 0) & extra_mask
        return jnp.where(mask, s_new - sl_r * dist.astype(F32), NEG)

    def weighted_new(ref, e_new):
        return _dot_t(e_new.astype(BF16), ref[0, half_cols:, :].astype(BF16))

    row_copies(b, slot, lambda cp: cp.wait())
    n_chunks = n_keys // S_CHUNK

    def pass1(c, mrow):
        k0 = pl.multiple_of(c * S_CHUNK, S_CHUNK)
        kt = kbuf[slot, :half_cols, pl.ds(k0, S_CHUNK)].astype(BF16)
        st = jnp.dot(qrows, kt, preferred_element_type=F32)
        j = _block_of_col(lax.broadcasted_iota(jnp.int32, (ncols, S_CHUNK), 0), npp)
        t = k0 + lax.broadcasted_iota(jnp.int32, (ncols, S_CHUNK), 1)
        expand = (j == t // CMP_BLOCK).astype(BF16)
        chosen = jnp.dot(sel16, expand, preferred_element_type=F32) > 0.5
        tpos = k0 + lax.broadcasted_iota(jnp.int32, (1, S_CHUNK), 1)
        dist = qp_r - tpos
        zc = jnp.where(chosen & (dist >= 0), st - sl_r * dist.astype(F32), NEG)
        zbuf[c] = zc
        return jnp.maximum(mrow, jnp.max(zc, axis=-1, keepdims=True))

    mrow = lax.fori_loop(0, n_chunks, pass1, jnp.full((S_ROWS, 1), NEG, F32))
    c_last = ((past // CMP_BLOCK) % 2) * npp + (past // CMP_BLOCK) // 2
    z_new = new_scores(kvs_new_ref, sel_rows[:, c_last:c_last + 1] > 0.5)
    m_r = jnp.maximum(mrow, jnp.max(z_new, axis=-1, keepdims=True))

    def pass2(c, carry):
        acc, den_r = carry
        k0 = pl.multiple_of(c * S_CHUNK, S_CHUNK)
        zc = zbuf[c]
        ec = jnp.where(zc > 0.5 * NEG, jnp.exp(zc - m_r), 0.0)
        vt = kbuf[slot, half_cols:, pl.ds(k0, S_CHUNK)].astype(BF16)
        return acc + _dot_t(ec.astype(BF16), vt), den_r + jnp.sum(ec, axis=-1, keepdims=True)

    e_new = jnp.where(z_new > 0.5 * NEG, jnp.exp(z_new - m_r), 0.0)
    acc, den_r = lax.fori_loop(0, n_chunks, pass2,
                               (weighted_new(kvs_new_ref, e_new), jnp.sum(e_new, axis=-1, keepdims=True)))
    o_slc = acc / jnp.where(den_r > 0.0, den_r, 1.0)

    win_buf = win_ref.shape[2]
    kw = win_ref[0, :half_cols, :].astype(BF16)
    st = jnp.dot(qrows, kw, preferred_element_type=F32)
    kpos = past - win_buf + lax.broadcasted_iota(jnp.int32, (1, win_buf), 1)
    dist = qp_r - kpos
    mask = (dist >= 0) & (dist < WINDOW) & (kpos >= 0)
    z_w = jnp.where(mask, st - sl_r * dist.astype(F32), NEG)
    zw_new = new_scores(kvw_new_ref, True)
    m_w = jnp.maximum(jnp.max(z_w, axis=-1, keepdims=True), jnp.max(zw_new, axis=-1, keepdims=True))
    e_w = jnp.where(z_w > 0.5 * NEG, jnp.exp(z_w - m_w), 0.0)
    ew_new = jnp.where(zw_new > 0.5 * NEG, jnp.exp(zw_new - m_w), 0.0)
    acc_w = _dot_t(e_w.astype(BF16), win_ref[0, half_cols:, :].astype(BF16)) + weighted_new(kvw_new_ref, ew_new)
    den_w = jnp.sum(e_w, axis=-1, keepdims=True) + jnp.sum(ew_new, axis=-1, keepdims=True)
    o_win = acc_w / jnp.where(den_w > 0.0, den_w, 1.0)

    gates = gates_ref[0]
    for pair in range(N_HEADS // 2):
        kvh = (2 * pair) // Q_PER_KV
        e_sl = slice(2 * pair * ds, (2 * pair + 1) * ds)
        o_sl = slice((2 * pair + 1) * ds, (2 * pair + 2) * ds)
        shape = (ds, LANES)
        tile = (_gate_tile(gates, pair, 0, shape) * _pair_tile(o_cmp[e_sl], o_cmp[o_sl], 0)
                + _gate_tile(gates, pair, 1, shape) * _pair_tile(o_slc[e_sl], o_slc[o_sl], kvh)
                + _gate_tile(gates, pair, 2, shape) * _pair_tile(o_win[e_sl], o_win[o_sl], kvh))
        a_ref[0, :, pair * LANES:(pair + 1) * LANES] = tile.astype(BF16)


def _attn_sample(page_table, q, gates, kc, vc, kvs_new, kvw_new, state_win_t, cache_pages, *, past):
    b, n_pages = page_table.shape
    ds = q.shape[1]
    npp = kc.shape[2]
    win_buf = state_win_t.shape[2]
    n_keys = n_pages * PAGE
    kern = functools.partial(_attn_sample_kernel, n_pages=n_pages, past=past, ds=ds, npp=npp)
    grid_spec = pltpu.PrefetchScalarGridSpec(
        num_scalar_prefetch=1,
        grid=(b,),
        in_specs=[
            pl.BlockSpec((1, ds, ATTN_WIDTH), lambda i, pt: (i, 0, 0)),
            pl.BlockSpec((1, ds, GATE_PAD), lambda i, pt: (i, 0, 0)),
            pl.BlockSpec((1, N_KV_HEADS, npp, LANES), lambda i, pt: (i, 0, 0, 0)),
            pl.BlockSpec((1, N_KV_HEADS, npp, LANES), lambda i, pt: (i, 0, 0, 0)),
            pl.BlockSpec((1, KV_COLS, LANES), lambda i, pt: (i, 0, 0)),
            pl.BlockSpec((1, KV_COLS, LANES), lambda i, pt: (i, 0, 0)),
            pl.BlockSpec((1, KV_COLS, win_buf), lambda i, pt: (i, 0, 0)),
            pl.BlockSpec(memory_space=pl.ANY),
        ],
        out_specs=pl.BlockSpec((1, ds, ATTN_WIDTH), lambda i, pt: (i, 0, 0)),
        scratch_shapes=[
            pltpu.VMEM((2, KV_COLS, n_keys), F32),
            pltpu.VMEM((n_keys // S_CHUNK, S_ROWS, S_CHUNK), F32),
            pltpu.SemaphoreType.DMA((2,)),
        ],
    )
    return pl.pallas_call(
        kern,
        grid_spec=grid_spec,
        out_shape=jax.ShapeDtypeStruct((b, ds, ATTN_WIDTH), BF16),
        compiler_params=pltpu.CompilerParams(
            dimension_semantics=("arbitrary",), vmem_limit_bytes=VMEM_LIMIT),
        name="attn_sample",
    )(page_table, q, gates, kc, vc, kvs_new, kvw_new, state_win_t, cache_pages)


def _ffn_kernel(x_ref, a_ref, m_ref, wo_ref, g2_ref, wr_ref, h_ref, n2_ref, gate_ref):
    h = (x_ref[...]
         + jnp.dot(a_ref[...], wo_ref[:ATTN_WIDTH, :], preferred_element_type=F32)
         + jnp.dot(m_ref[...], wo_ref[ATTN_WIDTH:, :], preferred_element_type=F32))
    h_ref[...] = h
    ms = jnp.mean(h * h, axis=-1, keepdims=True)
    n2 = ((h * lax.rsqrt(ms + EPS)) * g2_ref[...]).astype(BF16)
    n2_ref[...] = n2
    logits = jnp.dot(n2, wr_ref[...], preferred_element_type=F32)
    lane = lax.broadcasted_iota(jnp.int32, logits.shape, 1)
    big = jnp.int32(LANES)

    def masked_softmax(mask):
        zz = jnp.where(mask, logits, NEG)
        mx = jnp.max(zz, axis=-1, keepdims=True)
        ee = jnp.where(mask, jnp.exp(zz - mx), 0.0)
        return ee / jnp.sum(ee, axis=-1, keepdims=True)

    def first_argmax(vals, mask):
        v = jnp.max(jnp.where(mask, vals, -1.0), axis=-1, keepdims=True)
        idx = jnp.min(jnp.where(mask & (vals == v), lane, big), axis=-1, keepdims=True)
        return v, idx

    is_g = lane < N_EXPERT_GROUPS
    pg = masked_softmax(is_g)
    g_val, g_idx = first_argmax(pg, is_g)
    e_lane = lane - ROUTER_LANE0
    in_grp = (e_lane >= 0) & (e_lane < N_EXPERTS) & (e_lane // EXPERTS_PER_GROUP == g_idx)
    pe = masked_softmax(in_grp)
    v1, i1 = first_argmax(pe, in_grp)
    rest = in_grp & (lane != i1)
    v2, i2 = first_argmax(pe, rest)
    scale = g_val / (v1 + v2)
    gate_ref[...] = jnp.where(lane == i1, v1 * scale, jnp.where(lane == i2, v2 * scale, 0.0))


def _ffn(x2d, a2d, m2d, w_out16, g2, w_router16, tm):
    t = x2d.shape[0]
    row = lambda i: (i, 0)
    const = lambda i: (0, 0)
    return pl.pallas_call(
        _ffn_kernel,
        grid=(t // tm,),
        in_specs=[
            pl.BlockSpec((tm, D_MODEL), row),
            pl.BlockSpec((tm, ATTN_WIDTH), row),
            pl.BlockSpec((tm, POOL_WIDTH), row),
            pl.BlockSpec((D_MODEL, D_MODEL), const),
            pl.BlockSpec((1, D_MODEL), const),
            pl.BlockSpec((D_MODEL, LANES), const),
        ],
        out_specs=(pl.BlockSpec((tm, D_MODEL), row), pl.BlockSpec((tm, D_MODEL), row),
                   pl.BlockSpec((tm, LANES), row)),
        out_shape=(jax.ShapeDtypeStruct((t, D_MODEL), F32), jax.ShapeDtypeStruct((t, D_MODEL), BF16),
                   jax.ShapeDtypeStruct((t, LANES), F32)),
        compiler_params=pltpu.CompilerParams(
            dimension_semantics=("arbitrary",), vmem_limit_bytes=VMEM_LIMIT),
        name="ffn",
    )(x2d, a2d, m2d, w_out16, g2, w_router16)


def _moe_kernel(n2_ref, gate_ref, h_ref, wg_ref, wu_ref, wd_ref, y_ref):
    e = pl.program_id(1)

    @pl.when(e == 0)
    def _():
        y_ref[...] = h_ref[...]

    n2 = n2_ref[...]
    gu = jnp.dot(n2, wg_ref[0], preferred_element_type=F32)
    up = jnp.dot(n2, wu_ref[0], preferred_element_type=F32)
    he = (gu * (1.0 / (1.0 + jnp.exp(-gu)))) * up
    out = jnp.dot(he.astype(BF16), wd_ref[0], preferred_element_type=F32)
    lane = lax.broadcasted_iota(jnp.int32, gate_ref.shape, 1)
    gcol = jnp.sum(jnp.where(lane == e + ROUTER_LANE0, gate_ref[...], 0.0), axis=-1, keepdims=True)
    y_ref[...] += gcol * out


def _moe(n2, gate, h, wg16, wu16, wd16, tm):
    t = n2.shape[0]
    row = lambda i, e: (i, 0)
    return pl.pallas_call(
        _moe_kernel,
        grid=(t // tm, N_EXPERTS),
        in_specs=[
            pl.BlockSpec((tm, D_MODEL), row),
            pl.BlockSpec((tm, LANES), row),
            pl.BlockSpec((tm, D_MODEL), row),
            pl.BlockSpec((1, D_MODEL, D_EXPERT), lambda i, e: (e, 0, 0)),
            pl.BlockSpec((1, D_MODEL, D_EXPERT), lambda i, e: (e, 0, 0)),
            pl.BlockSpec((1, D_EXPERT, D_MODEL), lambda i, e: (e, 0, 0)),
        ],
        out_specs=pl.BlockSpec((tm, D_MODEL), row),
        out_shape=jax.ShapeDtypeStruct((t, D_MODEL), F32),
        compiler_params=pltpu.CompilerParams(
            dimension_semantics=("arbitrary", "arbitrary"), vmem_limit_bytes=VMEM_LIMIT),
        name="moe",
    )(n2, gate, h, wg16, wu16, wd16)


PROJ_TILE = 512
MOE_TILE = 1024
PROMPT_NPP = LANES // 2
SAMPLE_NPP = LANES


def _round_up(n, mult):
    return -(-n // mult) * mult


def _feature_major(x5):
    b, t = x5.shape[:2]
    return jnp.transpose(x5, (0, 2, 3, 4, 1)).reshape(b, KV_COLS, t)


def _token_major(xt):
    b, _, t = xt.shape
    return jnp.transpose(xt.reshape(b, 2, N_KV_HEADS, HEAD_DIM, t), (0, 4, 1, 2, 3))


def _pad_pages(x, npp_from, npp_to):
    b = x.shape[0]
    x = x.reshape(b, N_KV_HEADS, npp_from, LANES)
    return jnp.concatenate([x, jnp.zeros((b, N_KV_HEADS, npp_to - npp_from, LANES), x.dtype)], axis=2)


def _expand_const(n_chunks, npp):
    col = np.arange(2 * npp)
    blk = 2 * (col % npp) + col // npp
    tok_blk = np.arange(n_chunks * K_CHUNK) // CMP_BLOCK
    e = (blk[None, :, None] == tok_blk.reshape(n_chunks, 1, K_CHUNK)).astype(np.float32)
    return jnp.asarray(e, dtype=BF16)


def _gate_place_const():
    gp = np.zeros((GATE_PAD, 3 * ATTN_WIDTH), np.float32)
    for h in range(N_HEADS):
        for c in range(3):
            gp[3 * h + c, c * ATTN_WIDTH + h * HEAD_DIM:c * ATTN_WIDTH + (h + 1) * HEAD_DIM] = 1.0
    return jnp.asarray(gp, dtype=BF16)


def kernel(x_prompt, x_sample, cache_cmp_kv, cache_slc_kv, state_win_kv, state_pool, page_table, norm1_g, w_in, q_norm_g, k_norm_cmp_g, k_norm_slc_g, k_norm_win_g, cmp_pos_emb, w_cmp_k, w_cmp_v, pool_w, pool_scale, w_out, norm2_g, w_router_group, w_router_expert, w_gate, w_up, w_down):
    b, s, _ = x_prompt.shape
    db, ds, _ = x_sample.shape
    n_pool, page_rows = cache_cmp_kv.shape[:2]
    assert page_rows == PAGE and s % PAGE == 0
    past = page_table.shape[1] * page_rows

    kv0 = ATTN_WIDTH
    kv1 = kv0 + KV_ROWS
    w_row = jnp.concatenate(
        [w_in[:, :kv0], w_in[:, kv1:kv1 + GATE_COLS], jnp.zeros((D_MODEL, GATE_PAD - GATE_COLS), w_in.dtype),
         w_in[:, kv1 + GATE_COLS:]], axis=1).astype(BF16)
    w_kv = w_in[:, kv0:kv1].T.astype(BF16)
    g1 = norm1_g[None, :]
    g2 = norm2_g[None, :]
    two = lambda g: jnp.tile(g, 2)[None, :]
    kg = jnp.stack([k_norm_slc_g, k_norm_win_g])[:, :, None]
    pos_t = jnp.tile(cmp_pos_emb.T, (1, 2))
    zb = jnp.zeros((HEAD_DIM, CMP_BLOCK, HEAD_DIM), F32)

    def blockdiag(w):
        wt = jnp.transpose(w, (1, 0, 2))
        return jnp.concatenate([jnp.concatenate([wt, zb], axis=2), jnp.concatenate([zb, wt], axis=2)], axis=1)

    bd = jnp.stack([blockdiag(w_cmp_k), blockdiag(w_cmp_v)]).astype(BF16)
    w_router = jnp.concatenate(
        [w_router_group, w_router_expert,
         jnp.zeros((D_MODEL, LANES - N_EXPERT_GROUPS - N_EXPERTS), w_router_group.dtype)], axis=1).astype(BF16)
    w_out16 = w_out.astype(BF16)
    pool_w16 = pool_w.astype(BF16)
    wg16, wu16, wd16 = w_gate.astype(BF16), w_up.astype(BF16), w_down.astype(BF16)
    ps = pool_scale[None, :]

    def ffn_moe(x2d, a2d, m2d, tm_ffn, tm_moe):
        h, n2, gate = _ffn(x2d, a2d, m2d, w_out16, g2, w_router, tm_ffn)
        return _moe(n2, gate, h, wg16, wu16, wd16, tm_moe)

    q, gates, u, kvc_t, kvs_t, kvw_t, ks16, vsx, kw16, vwx = _proj(
        x_prompt, g1, w_row, w_kv, two(q_norm_g), kg, PROJ_TILE)
    m_p = _pool(u, pool_w16, ps)
    pages_p = s // PAGE
    pt_p = jnp.zeros((b, pages_p), jnp.int32)
    dummy_tail = jnp.zeros((b, KV_COLS, LANES), F32)
    kc_p, vc_p = _compress(pt_p, kvc_t, dummy_tail, pos_t, bd, two(k_norm_cmp_g),
                           npp=pages_p, has_tail=False, paged=False)
    kc_p = _pad_pages(kc_p, pages_p, PROMPT_NPP)
    vc_p = _pad_pages(vc_p, pages_p, PROMPT_NPP)
    a_p = _attn_prompt(q, gates, kc_p, vc_p, ks16, vsx, kw16, vwx, _expand_const(s // K_CHUNK, PROMPT_NPP),
                       _gate_place_const(), pages_p)
    y_p = ffn_moe(x_prompt.reshape(b * s, D_MODEL), a_p.reshape(b * s, ATTN_WIDTH),
                  m_p.reshape(b * s, POOL_WIDTH), PROJ_TILE, MOE_TILE)

    ts = db * ds
    q_s, gates_s, u_s, kvc_st, kvs_st, kvw_st, _, _, _, _ = _proj(
        x_sample.reshape(1, ts, D_MODEL), g1, w_row, w_kv, two(q_norm_g), kg, ts)
    u_s = u_s.reshape(db, ds, POOL_WIDTH)
    pool_ext = jnp.concatenate([state_pool, u_s], axis=1)
    n_ext = pool_ext.shape[1]
    lead = _round_up(n_ext, 8) - n_ext
    pool_in = jnp.concatenate([jnp.zeros((db, lead, POOL_WIDTH), F32), pool_ext], axis=1)
    m_s = _pool(pool_in, pool_w16, ps)[:, lead + n_ext - ds:]

    def new_rows_t(xt):
        x = jnp.transpose(xt.reshape(KV_COLS, db, ds), (1, 0, 2))
        return jnp.concatenate([x, jnp.zeros((db, KV_COLS, LANES - ds), F32)], axis=2)

    kvc_new, kvs_new, kvw_new = new_rows_t(kvc_st), new_rows_t(kvs_st), new_rows_t(kvw_st)
    n_pages = page_table.shape[1]
    npp_c = _round_up(n_pages + 1, 8)
    cmp_pages = _feature_major(cache_cmp_kv)
    slc_pages = _feature_major(cache_slc_kv)
    kc_s, vc_s = _compress(page_table, cmp_pages, kvc_new, pos_t, bd, two(k_norm_cmp_g),
                           npp=npp_c, has_tail=True, paged=True)
    kc_s = _pad_pages(kc_s, npp_c, SAMPLE_NPP)
    vc_s = _pad_pages(vc_s, npp_c, SAMPLE_NPP)
    win_t = _feature_major(state_win_kv)
    a_s = _attn_sample(page_table, q_s.reshape(db, ds, ATTN_WIDTH), gates_s.reshape(db, ds, GATE_PAD),
                       kc_s, vc_s, kvs_new, kvw_new, win_t, slc_pages, past=past)
    y_s = ffn_moe(x_sample.reshape(ts, D_MODEL), a_s.reshape(ts, ATTN_WIDTH), m_s.reshape(ts, POOL_WIDTH), ts, ts)

    win_keep = min(WINDOW, s)
    sample5 = lambda xt: jnp.transpose(xt.reshape(2, N_KV_HEADS, HEAD_DIM, db, ds), (3, 4, 0, 1, 2))
    win_ctx_t = jnp.concatenate([win_t, kvw_new[:, :, :ds]], axis=2)
    return (y_p.reshape(b, s, D_MODEL), y_s.reshape(db, ds, D_MODEL),
            _token_major(kvc_t), sample5(kvc_st),
            _token_major(kvs_t), sample5(kvs_st),
            _token_major(kvw_t[:, :, s - win_keep:]), _token_major(win_ctx_t[:, :, ds:]),
            u[:, s - POOL_BUF:], pool_ext[:, ds:])
```

```python
import functools

import jax
import jax.numpy as jnp
import numpy as np
from jax import lax
from jax.experimental import pallas as pl
from jax.experimental.pallas import tpu as pltpu

F32 = jnp.float32
BF16 = jnp.bfloat16

D_MODEL = 1024
N_HEADS = 8
HEAD_DIM = 64
N_KV_HEADS = 2
Q_PER_KV = N_HEADS // N_KV_HEADS
ATTN_WIDTH = N_HEADS * HEAD_DIM
KV_COLS = 2 * N_KV_HEADS * HEAD_DIM
GATE_COLS = 3 * N_HEADS
POOL_WIDTH = D_MODEL - ATTN_WIDTH
POOL_WINDOWS = (2, 4, 8, 16)
POOL_GROUP_WIDTH = POOL_WIDTH // len(POOL_WINDOWS)
POOL_BUF = max(POOL_WINDOWS) - 1
CMP_BLOCK = 64
TOP_K_BLOCKS = 16
WINDOW = 512
FORCE_SCORE = 1.0e4
N_EXPERT_GROUPS = 4
EXPERTS_PER_GROUP = 4
N_EXPERTS = N_EXPERT_GROUPS * EXPERTS_PER_GROUP
D_EXPERT = 512
EPS = 1e-6
NEG = -1e30
SCALE = HEAD_DIM ** -0.5

LANES = 128
PAGE = 2 * CMP_BLOCK
GATE_PAD = LANES
ROW_COLS = ATTN_WIDTH + GATE_PAD + POOL_WIDTH
OFF_GATE = ATTN_WIDTH
OFF_U = OFF_GATE + GATE_PAD
KV_ROWS = 3 * KV_COLS
ROUTER_LANE0 = N_EXPERT_GROUPS
Q_TILE = 256
K_CHUNK = 256
VMEM_LIMIT = 56 * 1024 * 1024


def _slope(h):
    return float(2.0 ** (-8.0 * (h + 1) / N_HEADS))


def _half_group_norm(v, g):
    lane = lax.broadcasted_iota(jnp.int32, v.shape, 1)
    lo = lane < HEAD_DIM
    v2 = v * v
    s_lo = jnp.sum(jnp.where(lo, v2, 0.0), axis=-1, keepdims=True)
    s_hi = jnp.sum(jnp.where(lo, 0.0, v2), axis=-1, keepdims=True)
    r_lo = lax.rsqrt(s_lo * (1.0 / HEAD_DIM) + EPS)
    r_hi = lax.rsqrt(s_hi * (1.0 / HEAD_DIM) + EPS)
    return (v * jnp.where(lo, r_lo, r_hi)) * g


def _dot_t(a, b):
    return lax.dot_general(a, b, (((1,), (1,)), ((), ())), preferred_element_type=F32)


def _proj_kernel(x_ref, g1_ref, wr_ref, wkv_ref, qg_ref, kg_ref,
                 q_ref, gates_ref, u_ref, kvc_ref, kvs_ref, kvw_ref, ks16_ref, vsx_ref, kw16_ref, vwx_ref):
    x = x_ref[0]
    tm = x.shape[0]
    ms = jnp.mean(x * x, axis=-1, keepdims=True)
    n = ((x * lax.rsqrt(ms + EPS)) * g1_ref[...]).astype(BF16)
    p = jnp.dot(n, wr_ref[...], preferred_element_type=F32)
    pt = _dot_t(wkv_ref[...], n)
    for t in range(ATTN_WIDTH // LANES):
        sl = slice(t * LANES, (t + 1) * LANES)
        q_ref[0, :, sl] = _half_group_norm(p[:, sl], qg_ref[...]).astype(BF16)
    gates_ref[0] = 1.0 / (1.0 + jnp.exp(-p[:, OFF_GATE:OFF_GATE + GATE_PAD]))
    u_ref[0] = p[:, OFF_U:OFF_U + POOL_WIDTH]
    kvc_ref[0] = pt[:KV_COLS]
    half = KV_COLS // 2
    lane = lax.broadcasted_iota(jnp.int32, (tm, LANES), 1)
    for bi, (out32, k16, vx16) in enumerate(((kvs_ref, ks16_ref, vsx_ref), (kvw_ref, kw16_ref, vwx_ref))):
        off = (bi + 1) * KV_COLS
        heads = []
        for hh in range(N_KV_HEADS):
            kh = pt[off + hh * HEAD_DIM:off + (hh + 1) * HEAD_DIM]
            msk = jnp.mean(kh * kh, axis=0, keepdims=True)
            heads.append((kh * lax.rsqrt(msk + EPS)) * kg_ref[bi])
        kn = jnp.concatenate(heads, axis=0)
        v = pt[off + half:off + KV_COLS]
        out32[0, :half, :] = kn
        out32[0, half:, :] = v
        vt = v.T
        vx = (jnp.where(lane < HEAD_DIM, vt, 1.0), jnp.where(lane < HEAD_DIM, pltpu.roll(vt, HEAD_DIM, axis=1), 1.0))
        for c in range(tm // K_CHUNK):
            cs = slice(c * K_CHUNK, (c + 1) * K_CHUNK)
            k16[0, c] = kn[:, cs].astype(BF16)
            for hh in range(N_KV_HEADS):
                vx16[0, c, hh] = vx[hh][cs].astype(BF16)


def _proj(x3d, g1, w_row, w_kv, qg, kg, tm):
    b, s, _ = x3d.shape
    tok = lambda i, j: (i, j, 0)
    feat = lambda i, j: (i, 0, j)
    const2 = lambda i, j: (0, 0)
    nck = tm // K_CHUNK
    out_shape = (
        jax.ShapeDtypeStruct((b, s, ATTN_WIDTH), BF16),
        jax.ShapeDtypeStruct((b, s, GATE_PAD), F32),
        jax.ShapeDtypeStruct((b, s, POOL_WIDTH), F32),
        jax.ShapeDtypeStruct((b, KV_COLS, s), F32),
        jax.ShapeDtypeStruct((b, KV_COLS, s), F32),
        jax.ShapeDtypeStruct((b, KV_COLS, s), F32),
        jax.ShapeDtypeStruct((b, s // K_CHUNK, LANES, K_CHUNK), BF16),
        jax.ShapeDtypeStruct((b, s // K_CHUNK, N_KV_HEADS, K_CHUNK, LANES), BF16),
        jax.ShapeDtypeStruct((b, s // K_CHUNK, LANES, K_CHUNK), BF16),
        jax.ShapeDtypeStruct((b, s // K_CHUNK, N_KV_HEADS, K_CHUNK, LANES), BF16),
    )
    k_spec = pl.BlockSpec((1, nck, LANES, K_CHUNK), lambda i, j: (i, j, 0, 0))
    v_spec = pl.BlockSpec((1, nck, N_KV_HEADS, K_CHUNK, LANES), lambda i, j: (i, j, 0, 0, 0))
    out_specs = (
        pl.BlockSpec((1, tm, ATTN_WIDTH), tok),
        pl.BlockSpec((1, tm, GATE_PAD), tok),
        pl.BlockSpec((1, tm, POOL_WIDTH), tok),
        pl.BlockSpec((1, KV_COLS, tm), feat),
        pl.BlockSpec((1, KV_COLS, tm), feat),
        pl.BlockSpec((1, KV_COLS, tm), feat),
        k_spec, v_spec, k_spec, v_spec,
    )
    return pl.pallas_call(
        _proj_kernel,
        grid=(b, s // tm),
        in_specs=[
            pl.BlockSpec((1, tm, D_MODEL), tok),
            pl.BlockSpec((1, D_MODEL), const2),
            pl.BlockSpec((D_MODEL, ROW_COLS), const2),
            pl.BlockSpec((KV_ROWS, D_MODEL), const2),
            pl.BlockSpec((1, LANES), const2),
            pl.BlockSpec((2, HEAD_DIM, 1), lambda i, j: (0, 0, 0)),
        ],
        out_specs=out_specs,
        out_shape=out_shape,
        compiler_params=pltpu.CompilerParams(
            dimension_semantics=("arbitrary", "arbitrary"), vmem_limit_bytes=VMEM_LIMIT),
        name="proj",
    )(x3d, g1, w_row, w_kv, qg, kg)


def _pool_kernel(u_ref, pw_ref, ps_ref, m_ref):
    u = u_ref[0]
    n = u.shape[0]

    def shift(v, k):
        rolled = pltpu.roll(v, k, axis=0)
        r = lax.broadcasted_iota(jnp.int32, v.shape, 0)
        return jnp.where(r >= k, rolled, 0.0)

    sums = []
    s = u
    k = 1
    for gi in range(len(POOL_WINDOWS)):
        s = s + shift(s, k)
        k *= 2
        sums.append(s[:, :POOL_GROUP_WIDTH])
        s = s[:, POOL_GROUP_WIDTH:]
    row = lax.broadcasted_iota(jnp.int32, (n, 1), 0)
    for gi, w in enumerate(POOL_WINDOWS):
        sl = slice(gi * POOL_GROUP_WIDTH, (gi + 1) * POOL_GROUP_WIDTH)
        cnt = jnp.minimum(row + 1, w).astype(F32)
        d = sums[gi] / cnt - u[:, sl]
        y = jnp.dot(d.astype(BF16), pw_ref[gi], preferred_element_type=F32)
        m_ref[0, :, sl] = (y * ps_ref[:, sl]).astype(BF16)


def _pool(u3d, pool_w16, pool_scale):
    b, n, _ = u3d.shape
    return pl.pallas_call(
        _pool_kernel,
        grid=(b,),
        in_specs=[
            pl.BlockSpec((1, n, POOL_WIDTH), lambda i: (i, 0, 0)),
            pl.BlockSpec((len(POOL_WINDOWS), POOL_GROUP_WIDTH, POOL_GROUP_WIDTH), lambda i: (0, 0, 0)),
            pl.BlockSpec((1, POOL_WIDTH), lambda i: (0, 0)),
        ],
        out_specs=pl.BlockSpec((1, n, POOL_WIDTH), lambda i: (i, 0, 0)),
        out_shape=jax.ShapeDtypeStruct((b, n, POOL_WIDTH), BF16),
        compiler_params=pltpu.CompilerParams(
            dimension_semantics=("arbitrary",), vmem_limit_bytes=VMEM_LIMIT),
        name="pool",
    )(u3d, pool_w16, pool_scale)


def _compress_kernel(pt_ref, src_ref, tail_ref, pos_ref, bd_ref, g_ref, kc_ref, vc_ref, buf, sem,
                     *, n_pages, npp, has_tail, paged):
    b = pl.program_id(0)
    nb = pl.num_programs(0)
    n_slabs = 2 * N_KV_HEADS

    def slab_copy(row, p, c, slot):
        kv, kvh = divmod(c, N_KV_HEADS)
        if paged:
            src = src_ref.at[pt_ref[row, p], pl.ds(c * HEAD_DIM, HEAD_DIM), :]
        else:
            src = src_ref.at[row, pl.ds(c * HEAD_DIM, HEAD_DIM), pl.ds(p * PAGE, PAGE)]
        return pltpu.make_async_copy(src, buf.at[slot, kv, :, kvh * npp + p, :], sem.at[slot])

    def tail_copy(c, slot):
        kv, kvh = divmod(c, N_KV_HEADS)
        return pltpu.make_async_copy(tail_ref.at[0, pl.ds(c * HEAD_DIM, HEAD_DIM), :],
                                     buf.at[slot, kv, :, kvh * npp + n_pages, :], sem.at[slot])

    def row_copies(row, slot, fn):
        def body(p, carry):
            for c in range(n_slabs):
                fn(slab_copy(row, p, c, slot))
            return carry
        lax.fori_loop(0, n_pages, body, 0)

    n_real = n_pages + (1 if has_tail else 0)

    @pl.when(b == 0)
    def _():
        if npp > n_real:
            for kvh in range(N_KV_HEADS):
                buf[:, :, :, kvh * npp + n_real:(kvh + 1) * npp, :] = jnp.zeros(
                    (2, 2, HEAD_DIM, npp - n_real, LANES), F32)
        row_copies(0, 0, lambda cp: cp.start())

    slot = b % 2

    @pl.when(b + 1 < nb)
    def _():
        row_copies(b + 1, 1 - slot, lambda cp: cp.start())

    if has_tail:
        for c in range(n_slabs):
            tail_copy(c, slot).start()
        for c in range(n_slabs):
            tail_copy(c, slot).wait()
    row_copies(b, slot, lambda cp: cp.wait())

    rows = N_KV_HEADS * npp

    def body(d, carry):
        acc_k, acc_v = carry
        pos = pos_ref[pl.ds(d, 1), :]
        xk = (buf[slot, 0, d] + pos).astype(BF16)
        xv = (buf[slot, 1, d] + pos).astype(BF16)
        return (acc_k + jnp.dot(xk, bd_ref[0, d], preferred_element_type=F32),
                acc_v + jnp.dot(xv, bd_ref[1, d], preferred_element_type=F32))

    zero = jnp.zeros((rows, LANES), F32)
    acc_k, acc_v = lax.fori_loop(0, HEAD_DIM, body, (zero, zero), unroll=8)
    kc_ref[0] = _half_group_norm(acc_k, g_ref[...])
    vc_ref[0] = acc_v


def _compress(page_table, src, tail, pos_t, bd, g2, *, npp, has_tail, paged):
    b, n_pages = page_table.shape
    kern = functools.partial(_compress_kernel, n_pages=n_pages, npp=npp, has_tail=has_tail, paged=paged)
    rows = N_KV_HEADS * npp
    grid_spec = pltpu.PrefetchScalarGridSpec(
        num_scalar_prefetch=1,
        grid=(b,),
        in_specs=[
            pl.BlockSpec(memory_space=pl.ANY),
            pl.BlockSpec((1, KV_COLS, LANES), lambda i, pt: (i, 0, 0)),
            pl.BlockSpec((HEAD_DIM, LANES), lambda i, pt: (0, 0)),
            pl.BlockSpec((2, HEAD_DIM, LANES, LANES), lambda i, pt: (0, 0, 0, 0)),
            pl.BlockSpec((1, LANES), lambda i, pt: (0, 0)),
        ],
        out_specs=(pl.BlockSpec((1, rows, LANES), lambda i, pt: (i, 0, 0)),
                   pl.BlockSpec((1, rows, LANES), lambda i, pt: (i, 0, 0))),
        scratch_shapes=[
            pltpu.VMEM((2, 2, HEAD_DIM, rows, LANES), F32),
            pltpu.SemaphoreType.DMA((2,)),
        ],
    )
    return pl.pallas_call(
        kern,
        grid_spec=grid_spec,
        out_shape=(jax.ShapeDtypeStruct((b, rows, LANES), F32), jax.ShapeDtypeStruct((b, rows, LANES), F32)),
        compiler_params=pltpu.CompilerParams(
            dimension_semantics=("arbitrary",), vmem_limit_bytes=VMEM_LIMIT),
        name="compress",
    )(page_table, src, tail, pos_t, bd, g2)


def _to_half(tile, src_half, dst_half):
    lane = lax.broadcasted_iota(jnp.int32, tile.shape, 1)
    src = tile if src_half == dst_half else pltpu.roll(tile, HEAD_DIM, axis=1)
    keep = (lane < HEAD_DIM) if dst_half == 0 else (lane >= HEAD_DIM)
    return jnp.where(keep, src, 0.0)


def _pair_tile(o_even, o_odd, half):
    lane = lax.broadcasted_iota(jnp.int32, o_even.shape, 1)
    if half == 0:
        return jnp.where(lane < HEAD_DIM, o_even, pltpu.roll(o_odd, HEAD_DIM, axis=1))
    return jnp.where(lane < HEAD_DIM, pltpu.roll(o_even, HEAD_DIM, axis=1), o_odd)


def _gate_tile(gates, pair, c, shape):
    lane = lax.broadcasted_iota(jnp.int32, shape, 1)
    he, ho = 2 * pair, 2 * pair + 1
    return jnp.where(lane < HEAD_DIM, gates[:, 3 * he + c:3 * he + c + 1], gates[:, 3 * ho + c:3 * ho + c + 1])


def _block_of_col(col, npp):
    return 2 * (col % npp) + col // npp


def _cmp_operand(x, npp):
    return jnp.concatenate([_to_half(x, 0, 0), _to_half(x, 1, 0)], axis=0).astype(BF16)


def _rank_select(score_t, blk_t, cols):
    rank = jnp.zeros(score_t.shape, jnp.int32)
    for r, n in cols:
        row = score_t[r:r + 1, :]
        ahead = (row > score_t) | ((row == score_t) & (blk_t > n))
        rank = rank + ahead.astype(jnp.int32)
    return ((rank < TOP_K_BLOCKS) & (score_t > -0.5)).astype(F32)


POS_HI, POS_LO = HEAD_DIM, HEAD_DIM + 1
MASK_BIG = 1e30


def _attn_prompt_kernel(q_ref, gates_ref, kc_ref, vc_ref, ks_ref, vsx_ref, kw_ref, vwx_ref, exp_ref, gp_ref, a_ref,
                        zbuf, m_sc, acc_sc, *, n_pages, npp):
    i = pl.program_id(1)
    q0 = i * Q_TILE
    rows = Q_PER_KV * Q_TILE
    ncols = 2 * npp
    qpos = q0 + lax.broadcasted_iota(jnp.int32, (Q_TILE, 1), 0)
    blk = _block_of_col(lax.broadcasted_iota(jnp.int32, (Q_TILE, ncols), 1), npp)
    real_cols = [(half * n_pages + p, 2 * p + half) for half in range(2) for p in range(n_pages)]

    g = gates_ref[0]
    g_hi = g.astype(BF16)
    g_lo = (g - g_hi.astype(F32)).astype(BF16)
    gexp = (jnp.dot(g_hi, gp_ref[...], preferred_element_type=F32)
            + jnp.dot(g_lo, gp_ref[...], preferred_element_type=F32))

    r_key = lax.broadcasted_iota(jnp.int32, (Q_TILE, K_CHUNK), 1)
    r_qry = lax.broadcasted_iota(jnp.int32, (Q_TILE, K_CHUNK), 0)
    keep = {"causal": r_key <= r_qry, "lower": r_key > r_qry}

    all_rows = N_HEADS * Q_TILE

    def branch(lhs, k_ref, vx_ref, slc, chunks, c_lo, c_hi):
        m_sc[...] = jnp.full((all_rows, LANES), NEG, F32)

        def scores(c, kind):
            rr = lax.broadcasted_iota(jnp.int32, (HEAD_DIM, K_CHUNK), 0)
            tt = lax.broadcasted_iota(jnp.int32, (HEAD_DIM, K_CHUNK), 1).astype(F32)
            hi = ((c - i) * K_CHUNK).astype(F32)
            pos = jnp.where(rr == 0, hi, jnp.where(rr == 1, tt, 0.0)).astype(BF16)
            for kvh in range(N_KV_HEADS):
                parts = [k_ref[0, c, kvh * HEAD_DIM:(kvh + 1) * HEAD_DIM, :], pos]
                if slc:
                    parts.append(exp_ref[c])
                kx = jnp.concatenate(parts, axis=0)
                for gq in range(Q_PER_KV):
                    sl = slice((kvh * Q_PER_KV + gq) * Q_TILE, (kvh * Q_PER_KV + gq + 1) * Q_TILE)
                    zg = jnp.dot(lhs[sl], kx, preferred_element_type=F32)
                    if kind is not None:
                        zg = jnp.where(keep[kind], zg, NEG)
                    zbuf[c, sl, :] = zg
                    m_sc[sl] = jnp.maximum(m_sc[sl], jnp.maximum(zg[:, :LANES], zg[:, LANES:]))

        for lo, hi_, kind, cond in chunks:
            if kind is None and cond is None:
                def body(c, carry):
                    scores(c, None)
                    return carry
                lax.fori_loop(lo, hi_, body, 0)
            elif cond is None:
                scores(lo, kind)
            else:
                pl.when(cond)(functools.partial(scores, lo, kind))

        m = jnp.broadcast_to(jnp.max(m_sc[...], axis=-1, keepdims=True), (all_rows, LANES))
        acc_sc[...] = jnp.zeros((all_rows, LANES), F32)

        def pass2(c, carry):
            for h in range(N_HEADS):
                sl = slice(h * Q_TILE, (h + 1) * Q_TILE)
                z = zbuf[c, sl, :]
                e = jnp.concatenate([jnp.exp(z[:, :LANES] - m[sl]), jnp.exp(z[:, LANES:] - m[sl])], axis=1)
                acc_sc[sl] += jnp.dot(e.astype(BF16), vx_ref[0, c, h // Q_PER_KV], preferred_element_type=F32)
            return carry

        lax.fori_loop(c_lo, c_hi, pass2, 0)
        acc = acc_sc[...]
        den = pltpu.roll(acc, HEAD_DIM, axis=1)
        return acc / jnp.where(den > 0.0, den, 1.0)

    o_cmp_all, lhs_slc_all, lhs_win_all = [], [], []
    for kvh in range(N_KV_HEADS):
        slopes = [_slope(kvh * Q_PER_KV + g) for g in range(Q_PER_KV)]
        lane = lax.broadcasted_iota(jnp.int32, (Q_TILE, LANES), 1)
        pieces, pieces_x = [], []
        for gq in range(Q_PER_KV):
            h = kvh * Q_PER_KV + gq
            tile = q_ref[0, :, (h // 2) * LANES:(h // 2 + 1) * LANES].astype(F32) * SCALE
            low = _to_half(tile, h % 2, 0)
            pieces.append(low)
            pieces_x.append(jnp.where((lane == POS_HI) | (lane == POS_LO), slopes[gq], low))
        qk = jnp.concatenate(pieces, axis=0).astype(BF16)
        qk_x = jnp.concatenate(pieces_x, axis=0).astype(BF16)

        kc = _cmp_operand(kc_ref[0, kvh], npp)
        vc = _cmp_operand(vc_ref[0, kvh], npp)
        s = _dot_t(qk, kc)
        dist_c = qpos - (blk * CMP_BLOCK + (CMP_BLOCK - 1))
        mask_c = dist_c >= 0
        dist_cf = dist_c.astype(F32)
        p_list = []
        p_kv = jnp.zeros((Q_TILE, ncols), F32)
        for g in range(Q_PER_KV):
            z = jnp.where(mask_c, s[g * Q_TILE:(g + 1) * Q_TILE] - slopes[g] * dist_cf, NEG)
            mx = jnp.max(z, axis=-1, keepdims=True)
            e = jnp.where(mask_c, jnp.exp(z - mx), 0.0)
            den = jnp.sum(e, axis=-1, keepdims=True)
            p = e / jnp.where(den > 0.0, den, 1.0)
            p_kv = p_kv + p
            p_list.append(p.astype(BF16))
        o_cmp = jnp.dot(jnp.concatenate(p_list, axis=0), vc, preferred_element_type=F32)

        cand = blk * CMP_BLOCK <= qpos
        forced = (blk == qpos // CMP_BLOCK) | (blk == 0)
        score = jnp.where(forced, FORCE_SCORE, jnp.where(cand, p_kv, -1.0))
        score_t = score.T
        st = jnp.concatenate([score_t[:n_pages], score_t[npp:npp + n_pages]], axis=0)
        rr = lax.broadcasted_iota(jnp.int32, (2 * n_pages, Q_TILE), 0)
        bt = jnp.where(rr < n_pages, 2 * rr, 2 * (rr - n_pages) + 1)
        sel_s = _rank_select(st, bt, [(r, n) for r, (_, n) in enumerate(real_cols)])
        pad = jnp.zeros((npp - n_pages, Q_TILE), F32)
        sel_t = jnp.concatenate([sel_s[:n_pages], pad, sel_s[n_pages:], pad], axis=0)
        sel = sel_t.T
        sel_bias = jnp.where(sel > 0.5, 0.0, -MASK_BIG).astype(BF16)
        lhs_slc_all.append(jnp.concatenate([qk_x, jnp.concatenate([sel_bias] * Q_PER_KV, axis=0)], axis=1))
        lhs_win_all.append(qk_x)
        o_cmp_all.append(o_cmp)

    o_slc = branch(jnp.concatenate(lhs_slc_all, axis=0), ks_ref, vsx_ref, True,
                   [(0, i, None, None), (i, None, "causal", None)], 0, i + 1)
    o_win = branch(jnp.concatenate(lhs_win_all, axis=0), kw_ref, vwx_ref, False,
                   [(i - 2, None, "lower", i >= 2), (i - 1, None, None, i >= 1), (i, None, "causal", None)],
                   jnp.maximum(i - 2, 0), i + 1)
    o_cmp = jnp.concatenate(o_cmp_all, axis=0)

    for pair in range(N_HEADS // 2):
        e_sl = slice(2 * pair * Q_TILE, (2 * pair + 1) * Q_TILE)
        o_sl = slice((2 * pair + 1) * Q_TILE, (2 * pair + 2) * Q_TILE)
        tile = jnp.zeros((Q_TILE, LANES), F32)
        for br, o in enumerate((o_cmp, o_slc, o_win)):
            col = br * ATTN_WIDTH + pair * LANES
            tile = tile + gexp[:, col:col + LANES] * _pair_tile(o[e_sl], o[o_sl], 0)
        a_ref[0, :, pair * LANES:(pair + 1) * LANES] = tile.astype(BF16)


def _attn_prompt(q, gates, kc, vc, ks16, vsx, kw16, vwx, expand, gate_place, n_pages):
    b, s, _ = q.shape
    npp = kc.shape[2]
    n_chunks = s // K_CHUNK
    rows = Q_PER_KV * Q_TILE
    assert WINDOW == 2 * K_CHUNK and Q_TILE == K_CHUNK
    kern = functools.partial(_attn_prompt_kernel, n_pages=n_pages, npp=npp)
    per_b4 = lambda bi, i: (bi, 0, 0, 0)
    per_b5 = lambda bi, i: (bi, 0, 0, 0, 0)
    return pl.pallas_call(
        kern,
        grid=(b, s // Q_TILE),
        in_specs=[
            pl.BlockSpec((1, Q_TILE, ATTN_WIDTH), lambda bi, i: (bi, i, 0)),
            pl.BlockSpec((1, Q_TILE, GATE_PAD), lambda bi, i: (bi, i, 0)),
            pl.BlockSpec((1, N_KV_HEADS, npp, LANES), per_b4),
            pl.BlockSpec((1, N_KV_HEADS, npp, LANES), per_b4),
            pl.BlockSpec((1, n_chunks, LANES, K_CHUNK), per_b4),
            pl.BlockSpec((1, n_chunks, N_KV_HEADS, K_CHUNK, LANES), per_b5),
            pl.BlockSpec((1, n_chunks, LANES, K_CHUNK), per_b4),
            pl.BlockSpec((1, n_chunks, N_KV_HEADS, K_CHUNK, LANES), per_b5),
            pl.BlockSpec((n_chunks, 2 * npp, K_CHUNK), lambda bi, i: (0, 0, 0)),
            pl.BlockSpec((GATE_PAD, 3 * ATTN_WIDTH), lambda bi, i: (0, 0)),
        ],
        out_specs=pl.BlockSpec((1, Q_TILE, ATTN_WIDTH), lambda bi, i: (bi, i, 0)),
        out_shape=jax.ShapeDtypeStruct((b, s, ATTN_WIDTH), BF16),
        scratch_shapes=[
            pltpu.VMEM((n_chunks, N_KV_HEADS * rows, K_CHUNK), F32),
            pltpu.VMEM((N_KV_HEADS * rows, LANES), F32),
            pltpu.VMEM((N_KV_HEADS * rows, LANES), F32),
        ],
        compiler_params=pltpu.CompilerParams(
            dimension_semantics=("arbitrary", "arbitrary"), vmem_limit_bytes=VMEM_LIMIT),
        name="attn_prompt",
    )(q, gates, kc, vc, ks16, vsx, kw16, vwx, expand, gate_place)


S_ROWS = LANES
S_CHUNK = 1024


def _attn_sample_kernel(pt_ref, q_ref, gates_ref, kc_ref, vc_ref, kvs_new_ref, kvw_new_ref, win_ref, cache_ref,
                        a_ref, kbuf, zbuf, sem, *, n_pages, past, ds, npp):
    b = pl.program_id(0)
    nb = pl.num_programs(0)
    n_keys = n_pages * PAGE
    n_real = N_HEADS * ds
    ncols = 2 * npp
    half_cols = KV_COLS // 2
    n_blocks = (past + ds + CMP_BLOCK - 1) // CMP_BLOCK

    def page_copy(row, p, slot):
        return pltpu.make_async_copy(
            cache_ref.at[pt_ref[row, p]], kbuf.at[slot, :, pl.ds(p * PAGE, PAGE)], sem.at[slot])

    def row_copies(row, slot, fn):
        def body(p, carry):
            fn(page_copy(row, p, slot))
            return carry
        lax.fori_loop(0, n_pages, body, 0)

    @pl.when(b == 0)
    def _():
        row_copies(0, 0, lambda cp: cp.start())

    slot = b % 2

    @pl.when(b + 1 < nb)
    def _():
        row_copies(b + 1, 1 - slot, lambda cp: cp.start())

    by_kvh, low = [], []
    for h in range(N_HEADS):
        tile = q_ref[0, :, (h // 2) * LANES:(h // 2 + 1) * LANES].astype(F32) * SCALE
        by_kvh.append(_to_half(tile, h % 2, h // Q_PER_KV))
        low.append(_to_half(tile, h % 2, 0))
    zpad = jnp.zeros((S_ROWS - n_real, LANES), F32)
    qrows = jnp.concatenate(by_kvh + [zpad], axis=0).astype(BF16)
    qlow = jnp.concatenate(low + [zpad], axis=0).astype(BF16)

    r_col = lax.broadcasted_iota(jnp.int32, (S_ROWS, 1), 0)
    q_of_r = r_col % ds
    h_of_r = r_col // ds
    qp_r = past + q_of_r
    sl_r = jnp.zeros((S_ROWS, 1), F32)
    for h in range(N_HEADS):
        sl_r = jnp.where(h_of_r == h, _slope(h), sl_r)
    kvh_r = h_of_r // Q_PER_KV

    blk = _block_of_col(lax.broadcasted_iota(jnp.int32, (S_ROWS, ncols), 1), npp)
    s = jnp.where(kvh_r == 0, _dot_t(qlow, _cmp_operand(kc_ref[0, 0], npp)),
                  _dot_t(qlow, _cmp_operand(kc_ref[0, 1], npp)))
    dist_c = qp_r - (blk * CMP_BLOCK + (CMP_BLOCK - 1))
    mask_c = dist_c >= 0
    z = jnp.where(mask_c, s - sl_r * dist_c.astype(F32), NEG)
    mx = jnp.max(z, axis=-1, keepdims=True)
    e = jnp.where(mask_c, jnp.exp(z - mx), 0.0)
    den = jnp.sum(e, axis=-1, keepdims=True)
    p = e / jnp.where(den > 0.0, den, 1.0)
    pb = p.astype(BF16)
    o_cmp = jnp.where(kvh_r == 0, jnp.dot(pb, _cmp_operand(vc_ref[0, 0], npp), preferred_element_type=F32),
                      jnp.dot(pb, _cmp_operand(vc_ref[0, 1], npp), preferred_element_type=F32))

    kq = N_KV_HEADS * ds
    p_kv = []
    for kvh in range(N_KV_HEADS):
        acc = jnp.zeros((ds, ncols), F32)
        for g in range(Q_PER_KV):
            r0 = (kvh * Q_PER_KV + g) * ds
            acc = acc + p[r0:r0 + ds]
        p_kv.append(acc)
    p_kv = jnp.concatenate(p_kv, axis=0)
    blk2 = _block_of_col(lax.broadcasted_iota(jnp.int32, (kq, ncols), 1), npp)
    qp2 = past + lax.broadcasted_iota(jnp.int32, (kq, 1), 0) % ds
    cand = blk2 * CMP_BLOCK <= qp2
    forced = (blk2 == qp2 // CMP_BLOCK) | (blk2 == 0)
    score = jnp.where(forced, FORCE_SCORE, jnp.where(cand, p_kv, -1.0))
    rank = jnp.zeros((kq, ncols), jnp.int32)
    for n in range(n_blocks):
        c = (n % 2) * npp + n // 2
        col = score[:, c:c + 1]
        ahead = (col > score) | ((col == score) & (blk2 > n))
        rank = rank + ahead.astype(jnp.int32)
    sel2 = ((rank < TOP_K_BLOCKS) & (score > -0.5)).astype(F32)
    sel_rows = []
    for h in range(N_HEADS):
        kvh = h // Q_PER_KV
        sel_rows.append(sel2[kvh * ds:(kvh + 1) * ds])
    sel_rows.append(jnp.zeros((S_ROWS - n_real, ncols), F32))
    sel_rows = jnp.concatenate(sel_rows, axis=0)
    sel16 = sel_rows.astype(BF16)

    new_lane = lax.broadcasted_iota(jnp.int32, (1, LANES), 1)

    def new_scores(ref, extra_mask):
        k_new = ref[0, :half_cols, :].astype(BF16)
        s_new = jnp.dot(qrows, k_new, preferred_element_type=F32)
        dist = q_of_r - new_lane
        mask = (new_lane < ds) & (dist >= 0) & extra_mask
        return jnp.where(mask, s_new - sl_r * dist.astype(F32), NEG)

    def weighted_new(ref, e_new):
        return _dot_t(e_new.astype(BF16), ref[0, half_cols:, :].astype(BF16))

    row_copies(b, slot, lambda cp: cp.wait())
    n_chunks = n_keys // S_CHUNK

    def pass1(c, mrow):
        k0 = pl.multiple_of(c * S_CHUNK, S_CHUNK)
        kt = kbuf[slot, :half_cols, pl.ds(k0, S_CHUNK)].astype(BF16)
        st = jnp.dot(qrows, kt, preferred_element_type=F32)
        j = _block_of_col(lax.broadcasted_iota(jnp.int32, (ncols, S_CHUNK), 0), npp)
        t = k0 + lax.broadcasted_iota(jnp.int32, (ncols, S_CHUNK), 1)
        expand = (j == t // CMP_BLOCK).astype(BF16)
        chosen = jnp.dot(sel16, expand, preferred_element_type=F32) > 0.5
        tpos = k0 + lax.broadcasted_iota(jnp.int32, (1, S_CHUNK), 1)
        dist = qp_r - tpos
        zc = jnp.where(chosen & (dist >= 0), st - sl_r * dist.astype(F32), NEG)
        zbuf[c] = zc
        return jnp.maximum(mrow, jnp.max(zc, axis=-1, keepdims=True))

    mrow = lax.fori_loop(0, n_chunks, pass1, jnp.full((S_ROWS, 1), NEG, F32))
    c_last = ((past // CMP_BLOCK) % 2) * npp + (past // CMP_BLOCK) // 2
    z_new = new_scores(kvs_new_ref, sel_rows[:, c_last:c_last + 1] > 0.5)
    m_r = jnp.maximum(mrow, jnp.max(z_new, axis=-1, keepdims=True))

    def pass2(c, carry):
        acc, den_r = carry
        k0 = pl.multiple_of(c * S_CHUNK, S_CHUNK)
        zc = zbuf[c]
        ec = jnp.where(zc > 0.5 * NEG, jnp.exp(zc - m_r), 0.0)
        vt = kbuf[slot, half_cols:, pl.ds(k0, S_CHUNK)].astype(BF16)
        return acc + _dot_t(ec.astype(BF16), vt), den_r + jnp.sum(ec, axis=-1, keepdims=True)

    e_new = jnp.where(z_new > 0.5 * NEG, jnp.exp(z_new - m_r), 0.0)
    acc, den_r = lax.fori_loop(0, n_chunks, pass2,
                               (weighted_new(kvs_new_ref, e_new), jnp.sum(e_new, axis=-1, keepdims=True)))
    o_slc = acc / jnp.where(den_r > 0.0, den_r, 1.0)

    win_buf = win_ref.shape[2]
    kw = win_ref[0, :half_cols, :].astype(BF16)
    st = jnp.dot(qrows, kw, preferred_element_type=F32)
    kpos = past - win_buf + lax.broadcasted_iota(jnp.int32, (1, win_buf), 1)
    dist = qp_r - kpos
    mask = (dist >= 0) & (dist < WINDOW) & (kpos >= 0)
    z_w = jnp.where(mask, st - sl_r * dist.astype(F32), NEG)
    zw_new = new_scores(kvw_new_ref, True)
    m_w = jnp.maximum(jnp.max(z_w, axis=-1, keepdims=True), jnp.max(zw_new, axis=-1, keepdims=True))
    e_w = jnp.where(z_w > 0.5 * NEG, jnp.exp(z_w - m_w), 0.0)
    ew_new = jnp.where(zw_new > 0.5 * NEG, jnp.exp(zw_new - m_w), 0.0)
    acc_w = _dot_t(e_w.astype(BF16), win_ref[0, half_cols:, :].astype(BF16)) + weighted_new(kvw_new_ref, ew_new)
    den_w = jnp.sum(e_w, axis=-1, keepdims=True) + jnp.sum(ew_new, axis=-1, keepdims=True)
    o_win = acc_w / jnp.where(den_w > 0.0, den_w, 1.0)

    gates = gates_ref[0]
    for pair in range(N_HEADS // 2):
        kvh = (2 * pair) // Q_PER_KV
        e_sl = slice(2 * pair * ds, (2 * pair + 1) * ds)
        o_sl = slice((2 * pair + 1) * ds, (2 * pair + 2) * ds)
        shape = (ds, LANES)
        tile = (_gate_tile(gates, pair, 0, shape) * _pair_tile(o_cmp[e_sl], o_cmp[o_sl], 0)
                + _gate_tile(gates, pair, 1, shape) * _pair_tile(o_slc[e_sl], o_slc[o_sl], kvh)
                + _gate_tile(gates, pair, 2, shape) * _pair_tile(o_win[e_sl], o_win[o_sl], kvh))
        a_ref[0, :, pair * LANES:(pair + 1) * LANES] = tile.astype(BF16)


def _attn_sample(page_table, q, gates, kc, vc, kvs_new, kvw_new, state_win_t, cache_pages, *, past):
    b, n_pages = page_table.shape
    ds = q.shape[1]
    npp = kc.shape[2]
    win_buf = state_win_t.shape[2]
    n_keys = n_pages * PAGE
    kern = functools.partial(_attn_sample_kernel, n_pages=n_pages, past=past, ds=ds, npp=npp)
    grid_spec = pltpu.PrefetchScalarGridSpec(
        num_scalar_prefetch=1,
        grid=(b,),
        in_specs=[
            pl.BlockSpec((1, ds, ATTN_WIDTH), lambda i, pt: (i, 0, 0)),
            pl.BlockSpec((1, ds, GATE_PAD), lambda i, pt: (i, 0, 0)),
            pl.BlockSpec((1, N_KV_HEADS, npp, LANES), lambda i, pt: (i, 0, 0, 0)),
            pl.BlockSpec((1, N_KV_HEADS, npp, LANES), lambda i, pt: (i, 0, 0, 0)),
            pl.BlockSpec((1, KV_COLS, LANES), lambda i, pt: (i, 0, 0)),
            pl.BlockSpec((1, KV_COLS, LANES), lambda i, pt: (i, 0, 0)),
            pl.BlockSpec((1, KV_COLS, win_buf), lambda i, pt: (i, 0, 0)),
            pl.BlockSpec(memory_space=pl.ANY),
        ],
        out_specs=pl.BlockSpec((1, ds, ATTN_WIDTH), lambda i, pt: (i, 0, 0)),
        scratch_shapes=[
            pltpu.VMEM((2, KV_COLS, n_keys), F32),
            pltpu.VMEM((n_keys // S_CHUNK, S_ROWS, S_CHUNK), F32),
            pltpu.SemaphoreType.DMA((2,)),
        ],
    )
    return pl.pallas_call(
        kern,
        grid_spec=grid_spec,
        out_shape=jax.ShapeDtypeStruct((b, ds, ATTN_WIDTH), BF16),
        compiler_params=pltpu.CompilerParams(
            dimension_semantics=("arbitrary",), vmem_limit_bytes=VMEM_LIMIT),
        name="attn_sample",
    )(page_table, q, gates, kc, vc, kvs_new, kvw_new, state_win_t, cache_pages)


PAY_H = D_MODEL
PAY_GATE = 2 * D_MODEL
PAY_W = 2 * D_MODEL + LANES
GROUP_LANE = 0


def _ffn_kernel(x_ref, a_ref, m_ref, wo_ref, g2_ref, wr_ref, *out_refs, packed):
    h = (x_ref[...]
         + jnp.dot(a_ref[...], wo_ref[:ATTN_WIDTH, :], preferred_element_type=F32)
         + jnp.dot(m_ref[...], wo_ref[ATTN_WIDTH:, :], preferred_element_type=F32))
    ms = jnp.mean(h * h, axis=-1, keepdims=True)
    n2f = (h * lax.rsqrt(ms + EPS)) * g2_ref[...]
    n2 = n2f.astype(BF16)
    logits = jnp.dot(n2, wr_ref[...], preferred_element_type=F32)
    lane = lax.broadcasted_iota(jnp.int32, logits.shape, 1)
    big = jnp.int32(LANES)

    def masked_softmax(mask):
        zz = jnp.where(mask, logits, NEG)
        mx = jnp.max(zz, axis=-1, keepdims=True)
        ee = jnp.where(mask, jnp.exp(zz - mx), 0.0)
        return ee / jnp.sum(ee, axis=-1, keepdims=True)

    def first_argmax(vals, mask):
        v = jnp.max(jnp.where(mask, vals, -1.0), axis=-1, keepdims=True)
        idx = jnp.min(jnp.where(mask & (vals == v), lane, big), axis=-1, keepdims=True)
        return v, idx

    is_g = lane < N_EXPERT_GROUPS
    pg = masked_softmax(is_g)
    g_val, g_idx = first_argmax(pg, is_g)
    e_lane = lane - ROUTER_LANE0
    in_grp = (e_lane >= 0) & (e_lane < N_EXPERTS) & (e_lane // EXPERTS_PER_GROUP == g_idx)
    pe = masked_softmax(in_grp)
    v1, i1 = first_argmax(pe, in_grp)
    rest = in_grp & (lane != i1)
    v2, i2 = first_argmax(pe, rest)
    scale = g_val / (v1 + v2)
    gate = jnp.where(lane == i1, v1 * scale, jnp.where(lane == i2, v2 * scale, 0.0))
    gate = jnp.where(lane == GROUP_LANE, g_idx.astype(F32), gate)
    if packed:
        (pay_ref,) = out_refs
        pay_ref[:, :PAY_H] = n2f
        pay_ref[:, PAY_H:PAY_GATE] = h
        pay_ref[:, PAY_GATE:] = gate
    else:
        h_ref, n2_ref, gate_ref = out_refs
        h_ref[...] = h
        n2_ref[...] = n2
        gate_ref[...] = gate


def _ffn(x2d, a2d, m2d, w_out16, g2, w_router16, tm, packed):
    t = x2d.shape[0]
    row = lambda i: (i, 0)
    const = lambda i: (0, 0)
    if packed:
        out_specs = pl.BlockSpec((tm, PAY_W), row)
        out_shape = jax.ShapeDtypeStruct((t, PAY_W), F32)
    else:
        out_specs = (pl.BlockSpec((tm, D_MODEL), row), pl.BlockSpec((tm, D_MODEL), row),
                     pl.BlockSpec((tm, LANES), row))
        out_shape = (jax.ShapeDtypeStruct((t, D_MODEL), F32), jax.ShapeDtypeStruct((t, D_MODEL), BF16),
                     jax.ShapeDtypeStruct((t, LANES), F32))
    return pl.pallas_call(
        functools.partial(_ffn_kernel, packed=packed),
        grid=(t // tm,),
        in_specs=[
            pl.BlockSpec((tm, D_MODEL), row),
            pl.BlockSpec((tm, ATTN_WIDTH), row),
            pl.BlockSpec((tm, POOL_WIDTH), row),
            pl.BlockSpec((D_MODEL, D_MODEL), const),
            pl.BlockSpec((1, D_MODEL), const),
            pl.BlockSpec((D_MODEL, LANES), const),
        ],
        out_specs=out_specs,
        out_shape=out_shape,
        compiler_params=pltpu.CompilerParams(
            dimension_semantics=("arbitrary",), vmem_limit_bytes=VMEM_LIMIT),
        name="ffn",
    )(x2d, a2d, m2d, w_out16, g2, w_router16)


MOE_ROWS = 256
ROW_CHUNK = 512
PLAN_TILE = 1024


def _plan_kernel(gate_ref, tri_ref, out_ref, cnt_ref, carry):
    @pl.when(pl.program_id(0) == 0)
    def _():
        carry[...] = jnp.zeros(carry.shape, F32)

    gl = gate_ref[...]
    lane = lax.broadcasted_iota(jnp.int32, gl.shape, 1)
    gid = gl[:, GROUP_LANE:GROUP_LANE + 1]
    onehot = jnp.where((lane < N_EXPERT_GROUPS) & (lane.astype(F32) == gid), 1.0, 0.0)
    before = jnp.dot(tri_ref[...], onehot.astype(BF16), preferred_element_type=F32) + carry[0:1, :]
    rank = jnp.sum(onehot * before, axis=-1, keepdims=True)
    out_ref[...] = jnp.where(lane == 0, rank, jnp.where(lane == 1, gid, 0.0))
    carry[0:1, :] = carry[0:1, :] + jnp.sum(onehot, axis=0, keepdims=True)
    cnt_ref[...] = carry[...]


def _plan(pay):
    t = pay.shape[0]
    tri = jnp.asarray(np.tril(np.ones((PLAN_TILE, PLAN_TILE), np.float32), -1), dtype=BF16)
    return pl.pallas_call(
        _plan_kernel,
        grid=(t // PLAN_TILE,),
        in_specs=[pl.BlockSpec((PLAN_TILE, LANES), lambda i: (i, PAY_GATE // LANES)),
                  pl.BlockSpec((PLAN_TILE, PLAN_TILE), lambda i: (0, 0))],
        out_specs=(pl.BlockSpec((PLAN_TILE, LANES), lambda i: (i, 0)), pl.BlockSpec((8, LANES), lambda i: (0, 0))),
        out_shape=(jax.ShapeDtypeStruct((t, LANES), F32), jax.ShapeDtypeStruct((8, LANES), F32)),
        scratch_shapes=[pltpu.VMEM((8, LANES), F32)],
        compiler_params=pltpu.CompilerParams(
            dimension_semantics=("arbitrary",), vmem_limit_bytes=VMEM_LIMIT),
        name="moe_plan",
    )(pay, tri)


def _row_copy_steps(c, nc, start_one, wait_chunk, sem):
    def issue(u, carry):
        start_one(u, sem.at[c % 2])
        return carry
    lax.fori_loop(0, ROW_CHUNK, issue, 0, unroll=8)

    @pl.when(c > 0)
    def _():
        wait_chunk(sem.at[(c + 1) % 2])

    @pl.when(c == nc - 1)
    def _():
        wait_chunk(sem.at[c % 2])


def _scatter_kernel(meta_ref, slot_ref, pay_ref, xs_ref, zrow, sem, zsem):
    c = pl.program_id(0)
    nc = pl.num_programs(0)

    @pl.when(c == 0)
    def _():
        zrow[...] = jnp.zeros(zrow.shape, F32)
        for g in range(N_EXPERT_GROUPS + 1):
            lo = meta_ref[g]
            hi = meta_ref[N_EXPERT_GROUPS + 1 + g]

            def zero_start(r, carry):
                pltpu.make_async_copy(zrow.at[pl.ds(0, 1)], xs_ref.at[pl.ds(r, 1)], zsem.at[0]).start()
                return carry

            def zero_wait(r, carry):
                pltpu.make_async_copy(zrow.at[pl.ds(0, 1)], xs_ref.at[pl.ds(0, 1)], zsem.at[0]).wait()
                return carry

            lax.fori_loop(lo, hi, zero_start, 0)
            lax.fori_loop(lo, hi, zero_wait, 0)

    def start_one(u, s):
        pltpu.make_async_copy(pay_ref.at[pl.ds(c * ROW_CHUNK + u, 1)], xs_ref.at[pl.ds(slot_ref[0, 0, u], 1)], s).start()

    def wait_chunk(s):
        pltpu.make_async_copy(pay_ref.at[pl.ds(0, ROW_CHUNK)], xs_ref.at[pl.ds(0, ROW_CHUNK)], s).wait()

    _row_copy_steps(c, nc, start_one, wait_chunk, sem)


def _scatter(meta, slot3, pay, n_slots):
    t = pay.shape[0]
    return pl.pallas_call(
        _scatter_kernel,
        grid_spec=pltpu.PrefetchScalarGridSpec(
            num_scalar_prefetch=1,
            grid=(t // ROW_CHUNK,),
            in_specs=[pl.BlockSpec((1, 1, ROW_CHUNK), lambda c, meta: (c, 0, 0), memory_space=pltpu.SMEM),
                      pl.BlockSpec(memory_space=pl.ANY)],
            out_specs=pl.BlockSpec(memory_space=pl.ANY),
            scratch_shapes=[pltpu.VMEM((8, PAY_W), F32), pltpu.SemaphoreType.DMA((2,)),
                            pltpu.SemaphoreType.DMA((1,))],
        ),
        out_shape=jax.ShapeDtypeStruct((n_slots, PAY_W), F32),
        compiler_params=pltpu.CompilerParams(dimension_semantics=("arbitrary",), vmem_limit_bytes=VMEM_LIMIT),
        name="moe_scatter",
    )(meta, slot3, pay)


def _gather_kernel(slot_ref, ys_ref, y_ref, sem):
    c = pl.program_id(0)

    def start_one(u, s):
        pltpu.make_async_copy(ys_ref.at[pl.ds(slot_ref[0, 0, u], 1)], y_ref.at[pl.ds(c * ROW_CHUNK + u, 1)], s).start()

    def wait_chunk(s):
        pltpu.make_async_copy(ys_ref.at[pl.ds(0, ROW_CHUNK)], y_ref.at[pl.ds(0, ROW_CHUNK)], s).wait()

    _row_copy_steps(c, pl.num_programs(0), start_one, wait_chunk, sem)


def _gather(slot3, ys, t):
    return pl.pallas_call(
        _gather_kernel,
        grid=(t // ROW_CHUNK,),
        in_specs=[pl.BlockSpec((1, 1, ROW_CHUNK), lambda c: (c, 0, 0), memory_space=pltpu.SMEM),
                  pl.BlockSpec(memory_space=pl.ANY)],
        out_specs=pl.BlockSpec(memory_space=pl.ANY),
        out_shape=jax.ShapeDtypeStruct((t, D_MODEL), F32),
        scratch_shapes=[pltpu.SemaphoreType.DMA((2,))],
        compiler_params=pltpu.CompilerParams(dimension_semantics=("arbitrary",), vmem_limit_bytes=VMEM_LIMIT),
        name="moe_gather",
    )(slot3, ys)


def _group_moe_kernel(tg_ref, nu_ref, xs_ref, wg_ref, wu_ref, wd_ref, ys_ref):
    j = pl.program_id(0)

    @pl.when(j < nu_ref[0])
    def _():
        g = tg_ref[j]
        x = xs_ref[:, :PAY_H].astype(BF16)
        y = xs_ref[:, PAY_H:PAY_GATE]
        gate = xs_ref[:, PAY_GATE:]
        lane = lax.broadcasted_iota(jnp.int32, gate.shape, 1)
        for e in range(EXPERTS_PER_GROUP):
            gu = jnp.dot(x, wg_ref[e], preferred_element_type=F32)
            up = jnp.dot(x, wu_ref[e], preferred_element_type=F32)
            he = (gu * (1.0 / (1.0 + jnp.exp(-gu)))) * up
            out = jnp.dot(he.astype(BF16), wd_ref[e], preferred_element_type=F32)
            col = ROUTER_LANE0 + g * EXPERTS_PER_GROUP + e
            y = y + jnp.sum(jnp.where(lane == col, gate, 0.0), axis=-1, keepdims=True) * out
        ys_ref[...] = y

    @pl.when(j >= nu_ref[0])
    def _():
        ys_ref[...] = jnp.zeros(ys_ref.shape, F32)


def _group_moe(tile_group, n_used, xs, wg16, wu16, wd16):
    n_slots = xs.shape[0]
    wspec = lambda shape: pl.BlockSpec((EXPERTS_PER_GROUP,) + shape, lambda j, tg, nu: (tg[j], 0, 0))
    return pl.pallas_call(
        _group_moe_kernel,
        grid_spec=pltpu.PrefetchScalarGridSpec(
            num_scalar_prefetch=2,
            grid=(n_slots // MOE_ROWS,),
            in_specs=[pl.BlockSpec((MOE_ROWS, PAY_W), lambda j, tg, nu: (j, 0)),
                      wspec((D_MODEL, D_EXPERT)), wspec((D_MODEL, D_EXPERT)), wspec((D_EXPERT, D_MODEL))],
            out_specs=pl.BlockSpec((MOE_ROWS, D_MODEL), lambda j, tg, nu: (j, 0)),
        ),
        out_shape=jax.ShapeDtypeStruct((n_slots, D_MODEL), F32),
        compiler_params=pltpu.CompilerParams(dimension_semantics=("arbitrary",), vmem_limit_bytes=VMEM_LIMIT),
        name="moe_group",
    )(tile_group, n_used, xs, wg16, wu16, wd16)


def _routed_moe(pay, wg16, wu16, wd16):
    t = pay.shape[0]
    n_slots = t + N_EXPERT_GROUPS * MOE_ROWS
    n_tiles = n_slots // MOE_ROWS
    plan, counts = _plan(pay)
    rank = plan[:, 0].astype(jnp.int32)
    gid = plan[:, 1].astype(jnp.int32)
    cnt = counts[0, :N_EXPERT_GROUPS].astype(jnp.int32)
    padded = -(-cnt // MOE_ROWS) * MOE_ROWS
    ends = jnp.cumsum(padded)
    off = ends - padded
    slot = (off[gid] + rank).reshape(t // ROW_CHUNK, 1, ROW_CHUNK)
    tile_start = jnp.arange(n_tiles, dtype=jnp.int32) * MOE_ROWS
    tile_group = jnp.minimum(jnp.sum(tile_start[:, None] >= ends[None, :], axis=1), N_EXPERT_GROUPS - 1)
    n_used = (ends[-1] // MOE_ROWS).reshape(1)
    meta = jnp.concatenate([off + cnt, ends[-1:], ends, jnp.full((1,), n_slots, jnp.int32)]).astype(jnp.int32)
    xs = _scatter(meta, slot, pay, n_slots)
    ys = _group_moe(tile_group.astype(jnp.int32), n_used.astype(jnp.int32), xs, wg16, wu16, wd16)
    return _gather(slot, ys, t)


def _moe_kernel(n2_ref, gate_ref, h_ref, wg_ref, wu_ref, wd_ref, y_ref):
    e = pl.program_id(1)

    @pl.when(e == 0)
    def _():
        y_ref[...] = h_ref[...]

    n2 = n2_ref[...]
    gu = jnp.dot(n2, wg_ref[0], preferred_element_type=F32)
    up = jnp.dot(n2, wu_ref[0], preferred_element_type=F32)
    he = (gu * (1.0 / (1.0 + jnp.exp(-gu)))) * up
    out = jnp.dot(he.astype(BF16), wd_ref[0], preferred_element_type=F32)
    lane = lax.broadcasted_iota(jnp.int32, gate_ref.shape, 1)
    gcol = jnp.sum(jnp.where(lane == e + ROUTER_LANE0, gate_ref[...], 0.0), axis=-1, keepdims=True)
    y_ref[...] += gcol * out


def _moe(n2, gate, h, wg16, wu16, wd16, tm):
    t = n2.shape[0]
    row = lambda i, e: (i, 0)
    return pl.pallas_call(
        _moe_kernel,
        grid=(t // tm, N_EXPERTS),
        in_specs=[
            pl.BlockSpec((tm, D_MODEL), row),
            pl.BlockSpec((tm, LANES), row),
            pl.BlockSpec((tm, D_MODEL), row),
            pl.BlockSpec((1, D_MODEL, D_EXPERT), lambda i, e: (e, 0, 0)),
            pl.BlockSpec((1, D_MODEL, D_EXPERT), lambda i, e: (e, 0, 0)),
            pl.BlockSpec((1, D_EXPERT, D_MODEL), lambda i, e: (e, 0, 0)),
        ],
        out_specs=pl.BlockSpec((tm, D_MODEL), row),
        out_shape=jax.ShapeDtypeStruct((t, D_MODEL), F32),
        compiler_params=pltpu.CompilerParams(
            dimension_semantics=("arbitrary", "arbitrary"), vmem_limit_bytes=VMEM_LIMIT),
        name="moe",
    )(n2, gate, h, wg16, wu16, wd16)


PROJ_TILE = 512
MOE_TILE = 1024
PROMPT_NPP = LANES // 2
SAMPLE_NPP = LANES


def _round_up(n, mult):
    return -(-n // mult) * mult


def _feature_major(x5):
    b, t = x5.shape[:2]
    return jnp.transpose(x5, (0, 2, 3, 4, 1)).reshape(b, KV_COLS, t)


def _token_major(xt):
    b, _, t = xt.shape
    return jnp.transpose(xt.reshape(b, 2, N_KV_HEADS, HEAD_DIM, t), (0, 4, 1, 2, 3))


def _pad_pages(x, npp_from, npp_to):
    b = x.shape[0]
    x = x.reshape(b, N_KV_HEADS, npp_from, LANES)
    return jnp.concatenate([x, jnp.zeros((b, N_KV_HEADS, npp_to - npp_from, LANES), x.dtype)], axis=2)


def _expand_const(n_chunks, npp):
    col = np.arange(2 * npp)
    blk = 2 * (col % npp) + col // npp
    tok_blk = np.arange(n_chunks * K_CHUNK) // CMP_BLOCK
    e = (blk[None, :, None] == tok_blk.reshape(n_chunks, 1, K_CHUNK)).astype(np.float32)
    return jnp.asarray(e, dtype=BF16)


def _gate_place_const():
    gp = np.zeros((GATE_PAD, 3 * ATTN_WIDTH), np.float32)
    for h in range(N_HEADS):
        for c in range(3):
            gp[3 * h + c, c * ATTN_WIDTH + h * HEAD_DIM:c * ATTN_WIDTH + (h + 1) * HEAD_DIM] = 1.0
    return jnp.asarray(gp, dtype=BF16)


def kernel(x_prompt, x_sample, cache_cmp_kv, cache_slc_kv, state_win_kv, state_pool, page_table, norm1_g, w_in, q_norm_g, k_norm_cmp_g, k_norm_slc_g, k_norm_win_g, cmp_pos_emb, w_cmp_k, w_cmp_v, pool_w, pool_scale, w_out, norm2_g, w_router_group, w_router_expert, w_gate, w_up, w_down):
    b, s, _ = x_prompt.shape
    db, ds, _ = x_sample.shape
    n_pool, page_rows = cache_cmp_kv.shape[:2]
    assert page_rows == PAGE and s % PAGE == 0
    past = page_table.shape[1] * page_rows

    kv0 = ATTN_WIDTH
    kv1 = kv0 + KV_ROWS
    w_row = jnp.concatenate(
        [w_in[:, :kv0], w_in[:, kv1:kv1 + GATE_COLS], jnp.zeros((D_MODEL, GATE_PAD - GATE_COLS), w_in.dtype),
         w_in[:, kv1 + GATE_COLS:]], axis=1).astype(BF16)
    w_kv = w_in[:, kv0:kv1].T.astype(BF16)
    g1 = norm1_g[None, :]
    g2 = norm2_g[None, :]
    two = lambda g: jnp.tile(g, 2)[None, :]
    kg = jnp.stack([k_norm_slc_g, k_norm_win_g])[:, :, None]
    pos_t = jnp.tile(cmp_pos_emb.T, (1, 2))
    zb = jnp.zeros((HEAD_DIM, CMP_BLOCK, HEAD_DIM), F32)

    def blockdiag(w):
        wt = jnp.transpose(w, (1, 0, 2))
        return jnp.concatenate([jnp.concatenate([wt, zb], axis=2), jnp.concatenate([zb, wt], axis=2)], axis=1)

    bd = jnp.stack([blockdiag(w_cmp_k), blockdiag(w_cmp_v)]).astype(BF16)
    w_router = jnp.concatenate(
        [w_router_group, w_router_expert,
         jnp.zeros((D_MODEL, LANES - N_EXPERT_GROUPS - N_EXPERTS), w_router_group.dtype)], axis=1).astype(BF16)
    w_out16 = w_out.astype(BF16)
    pool_w16 = pool_w.astype(BF16)
    wg16, wu16, wd16 = w_gate.astype(BF16), w_up.astype(BF16), w_down.astype(BF16)
    ps = pool_scale[None, :]

    def ffn_moe(x2d, a2d, m2d, tm_ffn, tm_moe, routed):
        if routed:
            return _routed_moe(_ffn(x2d, a2d, m2d, w_out16, g2, w_router, tm_ffn, True), wg16, wu16, wd16)
        h, n2, gate = _ffn(x2d, a2d, m2d, w_out16, g2, w_router, tm_ffn, False)
        return _moe(n2, gate, h, wg16, wu16, wd16, tm_moe)

    q, gates, u, kvc_t, kvs_t, kvw_t, ks16, vsx, kw16, vwx = _proj(
        x_prompt, g1, w_row, w_kv, two(q_norm_g), kg, PROJ_TILE)
    m_p = _pool(u, pool_w16, ps)
    pages_p = s // PAGE
    pt_p = jnp.zeros((b, pages_p), jnp.int32)
    dummy_tail = jnp.zeros((b, KV_COLS, LANES), F32)
    kc_p, vc_p = _compress(pt_p, kvc_t, dummy_tail, pos_t, bd, two(k_norm_cmp_g),
                           npp=pages_p, has_tail=False, paged=False)
    kc_p = _pad_pages(kc_p, pages_p, PROMPT_NPP)
    vc_p = _pad_pages(vc_p, pages_p, PROMPT_NPP)
    a_p = _attn_prompt(q, gates, kc_p, vc_p, ks16, vsx, kw16, vwx, _expand_const(s // K_CHUNK, PROMPT_NPP),
                       _gate_place_const(), pages_p)
    y_p = ffn_moe(x_prompt.reshape(b * s, D_MODEL), a_p.reshape(b * s, ATTN_WIDTH),
                  m_p.reshape(b * s, POOL_WIDTH), PROJ_TILE, MOE_TILE, True)

    ts = db * ds
    q_s, gates_s, u_s, kvc_st, kvs_st, kvw_st, _, _, _, _ = _proj(
        x_sample.reshape(1, ts, D_MODEL), g1, w_row, w_kv, two(q_norm_g), kg, ts)
    u_s = u_s.reshape(db, ds, POOL_WIDTH)
    pool_ext = jnp.concatenate([state_pool, u_s], axis=1)
    n_ext = pool_ext.shape[1]
    lead = _round_up(n_ext, 8) - n_ext
    pool_in = jnp.concatenate([jnp.zeros((db, lead, POOL_WIDTH), F32), pool_ext], axis=1)
    m_s = _pool(pool_in, pool_w16, ps)[:, lead + n_ext - ds:]

    def new_rows_t(xt):
        x = jnp.transpose(xt.reshape(KV_COLS, db, ds), (1, 0, 2))
        return jnp.concatenate([x, jnp.zeros((db, KV_COLS, LANES - ds), F32)], axis=2)

    kvc_new, kvs_new, kvw_new = new_rows_t(kvc_st), new_rows_t(kvs_st), new_rows_t(kvw_st)
    n_pages = page_table.shape[1]
    npp_c = _round_up(n_pages + 1, 8)
    cmp_pages = _feature_major(cache_cmp_kv)
    slc_pages = _feature_major(cache_slc_kv)
    kc_s, vc_s = _compress(page_table, cmp_pages, kvc_new, pos_t, bd, two(k_norm_cmp_g),
                           npp=npp_c, has_tail=True, paged=True)
    kc_s = _pad_pages(kc_s, npp_c, SAMPLE_NPP)
    vc_s = _pad_pages(vc_s, npp_c, SAMPLE_NPP)
    win_t = _feature_major(state_win_kv)
    a_s = _attn_sample(page_table, q_s.reshape(db, ds, ATTN_WIDTH), gates_s.reshape(db, ds, GATE_PAD),
                       kc_s, vc_s, kvs_new, kvw_new, win_t, slc_pages, past=past)
    y_s = ffn_moe(x_sample.reshape(ts, D_MODEL), a_s.reshape(ts, ATTN_WIDTH), m_s.reshape(ts, POOL_WIDTH), ts, ts, False)

    win_keep = min(WINDOW, s)
    sample5 = lambda xt: jnp.transpose(xt.reshape(2, N_KV_HEADS, HEAD_DIM, db, ds), (3, 4, 0, 1, 2))
    win_ctx_t = jnp.concatenate([win_t, kvw_new[:, :, :ds]], axis=2)
    return (y_p.reshape(b, s, D_MODEL), y_s.reshape(db, ds, D_MODEL),
            _token_major(kvc_t), sample5(kvc_st),
            _token_major(kvs_t), sample5(kvs_st),
            _token_major(kvw_t[:, :, s - win_keep:]), _token_major(win_ctx_t[:, :, ds:]),
            u[:, s - POOL_BUF:], pool_ext[:, ds:])
```

```python
import functools

import jax
import jax.numpy as jnp
import numpy as np
from jax import lax
from jax.experimental import pallas as pl
from jax.experimental.pallas import tpu as pltpu

F32 = jnp.float32
BF16 = jnp.bfloat16

D_MODEL = 1024
N_HEADS = 8
HEAD_DIM = 64
N_KV_HEADS = 2
Q_PER_KV = N_HEADS // N_KV_HEADS
ATTN_WIDTH = N_HEADS * HEAD_DIM
KV_COLS = 2 * N_KV_HEADS * HEAD_DIM
GATE_COLS = 3 * N_HEADS
POOL_WIDTH = D_MODEL - ATTN_WIDTH
POOL_WINDOWS = (2, 4, 8, 16)
POOL_GROUP_WIDTH = POOL_WIDTH // len(POOL_WINDOWS)
POOL_BUF = max(POOL_WINDOWS) - 1
CMP_BLOCK = 64
TOP_K_BLOCKS = 16
WINDOW = 512
FORCE_SCORE = 1.0e4
N_EXPERT_GROUPS = 4
EXPERTS_PER_GROUP = 4
N_EXPERTS = N_EXPERT_GROUPS * EXPERTS_PER_GROUP
D_EXPERT = 512
EPS = 1e-6
NEG = -1e30
SCALE = HEAD_DIM ** -0.5

LANES = 128
PAGE = 2 * CMP_BLOCK
GATE_PAD = LANES
ROW_COLS = ATTN_WIDTH + GATE_PAD + POOL_WIDTH
OFF_GATE = ATTN_WIDTH
OFF_U = OFF_GATE + GATE_PAD
KV_ROWS = 3 * KV_COLS
ROUTER_LANE0 = N_EXPERT_GROUPS
Q_TILE = 256
K_CHUNK = 256
VMEM_LIMIT = 56 * 1024 * 1024


def _slope(h):
    return float(2.0 ** (-8.0 * (h + 1) / N_HEADS))


def _half_group_norm(v, g):
    lane = lax.broadcasted_iota(jnp.int32, v.shape, 1)
    lo = lane < HEAD_DIM
    v2 = v * v
    s_lo = jnp.sum(jnp.where(lo, v2, 0.0), axis=-1, keepdims=True)
    s_hi = jnp.sum(jnp.where(lo, 0.0, v2), axis=-1, keepdims=True)
    r_lo = lax.rsqrt(s_lo * (1.0 / HEAD_DIM) + EPS)
    r_hi = lax.rsqrt(s_hi * (1.0 / HEAD_DIM) + EPS)
    return (v * jnp.where(lo, r_lo, r_hi)) * g


def _dot_t(a, b):
    return lax.dot_general(a, b, (((1,), (1,)), ((), ())), preferred_element_type=F32)


def _proj_kernel(x_ref, g1_ref, wr_ref, wkv_ref, qg_ref, kg_ref,
                 q_ref, gates_ref, u_ref, kvc_ref, kvs_ref, kvw_ref, ks16_ref, vsx_ref, kw16_ref, vwx_ref):
    x = x_ref[0]
    tm = x.shape[0]
    ms = jnp.mean(x * x, axis=-1, keepdims=True)
    n = ((x * lax.rsqrt(ms + EPS)) * g1_ref[...]).astype(BF16)
    p = jnp.dot(n, wr_ref[...], preferred_element_type=F32)
    pt = _dot_t(wkv_ref[...], n)
    for t in range(ATTN_WIDTH // LANES):
        sl = slice(t * LANES, (t + 1) * LANES)
        q_ref[0, :, sl] = _half_group_norm(p[:, sl], qg_ref[...]).astype(BF16)
    gates_ref[0] = 1.0 / (1.0 + jnp.exp(-p[:, OFF_GATE:OFF_GATE + GATE_PAD]))
    u_ref[0] = p[:, OFF_U:OFF_U + POOL_WIDTH]
    kvc_ref[0] = pt[:KV_COLS]
    half = KV_COLS // 2
    lane = lax.broadcasted_iota(jnp.int32, (tm, LANES), 1)
    for bi, (out32, k16, vx16) in enumerate(((kvs_ref, ks16_ref, vsx_ref), (kvw_ref, kw16_ref, vwx_ref))):
        off = (bi + 1) * KV_COLS
        heads = []
        for hh in range(N_KV_HEADS):
            kh = pt[off + hh * HEAD_DIM:off + (hh + 1) * HEAD_DIM]
            msk = jnp.mean(kh * kh, axis=0, keepdims=True)
            heads.append((kh * lax.rsqrt(msk + EPS)) * kg_ref[bi])
        kn = jnp.concatenate(heads, axis=0)
        v = pt[off + half:off + KV_COLS]
        out32[0, :half, :] = kn
        out32[0, half:, :] = v
        vt = v.T
        vx = (jnp.where(lane < HEAD_DIM, vt, 1.0), jnp.where(lane < HEAD_DIM, pltpu.roll(vt, HEAD_DIM, axis=1), 1.0))
        for c in range(tm // K_CHUNK):
            cs = slice(c * K_CHUNK, (c + 1) * K_CHUNK)
            k16[0, c] = kn[:, cs].astype(BF16)
            for hh in range(N_KV_HEADS):
                vx16[0, c, hh] = vx[hh][cs].astype(BF16)


def _proj(x3d, g1, w_row, w_kv, qg, kg, tm):
    b, s, _ = x3d.shape
    tok = lambda i, j: (i, j, 0)
    feat = lambda i, j: (i, 0, j)
    const2 = lambda i, j: (0, 0)
    nck = tm // K_CHUNK
    out_shape = (
        jax.ShapeDtypeStruct((b, s, ATTN_WIDTH), BF16),
        jax.ShapeDtypeStruct((b, s, GATE_PAD), F32),
        jax.ShapeDtypeStruct((b, s, POOL_WIDTH), F32),
        jax.ShapeDtypeStruct((b, KV_COLS, s), F32),
        jax.ShapeDtypeStruct((b, KV_COLS, s), F32),
        jax.ShapeDtypeStruct((b, KV_COLS, s), F32),
        jax.ShapeDtypeStruct((b, s // K_CHUNK, LANES, K_CHUNK), BF16),
        jax.ShapeDtypeStruct((b, s // K_CHUNK, N_KV_HEADS, K_CHUNK, LANES), BF16),
        jax.ShapeDtypeStruct((b, s // K_CHUNK, LANES, K_CHUNK), BF16),
        jax.ShapeDtypeStruct((b, s // K_CHUNK, N_KV_HEADS, K_CHUNK, LANES), BF16),
    )
    k_spec = pl.BlockSpec((1, nck, LANES, K_CHUNK), lambda i, j: (i, j, 0, 0))
    v_spec = pl.BlockSpec((1, nck, N_KV_HEADS, K_CHUNK, LANES), lambda i, j: (i, j, 0, 0, 0))
    out_specs = (
        pl.BlockSpec((1, tm, ATTN_WIDTH), tok),
        pl.BlockSpec((1, tm, GATE_PAD), tok),
        pl.BlockSpec((1, tm, POOL_WIDTH), tok),
        pl.BlockSpec((1, KV_COLS, tm), feat),
        pl.BlockSpec((1, KV_COLS, tm), feat),
        pl.BlockSpec((1, KV_COLS, tm), feat),
        k_spec, v_spec, k_spec, v_spec,
    )
    return pl.pallas_call(
        _proj_kernel,
        grid=(b, s // tm),
        in_specs=[
            pl.BlockSpec((1, tm, D_MODEL), tok),
            pl.BlockSpec((1, D_MODEL), const2),
            pl.BlockSpec((D_MODEL, ROW_COLS), const2),
            pl.BlockSpec((KV_ROWS, D_MODEL), const2),
            pl.BlockSpec((1, LANES), const2),
            pl.BlockSpec((2, HEAD_DIM, 1), lambda i, j: (0, 0, 0)),
        ],
        out_specs=out_specs,
        out_shape=out_shape,
        compiler_params=pltpu.CompilerParams(
            dimension_semantics=("arbitrary", "arbitrary"), vmem_limit_bytes=VMEM_LIMIT),
        name="proj",
    )(x3d, g1, w_row, w_kv, qg, kg)


def _pool_kernel(u_ref, pw_ref, ps_ref, m_ref):
    u = u_ref[0]
    n = u.shape[0]

    def shift(v, k):
        rolled = pltpu.roll(v, k, axis=0)
        r = lax.broadcasted_iota(jnp.int32, v.shape, 0)
        return jnp.where(r >= k, rolled, 0.0)

    sums = []
    s = u
    k = 1
    for gi in range(len(POOL_WINDOWS)):
        s = s + shift(s, k)
        k *= 2
        sums.append(s[:, :POOL_GROUP_WIDTH])
        s = s[:, POOL_GROUP_WIDTH:]
    row = lax.broadcasted_iota(jnp.int32, (n, 1), 0)
    for gi, w in enumerate(POOL_WINDOWS):
        sl = slice(gi * POOL_GROUP_WIDTH, (gi + 1) * POOL_GROUP_WIDTH)
        cnt = jnp.minimum(row + 1, w).astype(F32)
        d = sums[gi] / cnt - u[:, sl]
        y = jnp.dot(d.astype(BF16), pw_ref[gi], preferred_element_type=F32)
        m_ref[0, :, sl] = (y * ps_ref[:, sl]).astype(BF16)


def _pool(u3d, pool_w16, pool_scale):
    b, n, _ = u3d.shape
    return pl.pallas_call(
        _pool_kernel,
        grid=(b,),
        in_specs=[
            pl.BlockSpec((1, n, POOL_WIDTH), lambda i: (i, 0, 0)),
            pl.BlockSpec((len(POOL_WINDOWS), POOL_GROUP_WIDTH, POOL_GROUP_WIDTH), lambda i: (0, 0, 0)),
            pl.BlockSpec((1, POOL_WIDTH), lambda i: (0, 0)),
        ],
        out_specs=pl.BlockSpec((1, n, POOL_WIDTH), lambda i: (i, 0, 0)),
        out_shape=jax.ShapeDtypeStruct((b, n, POOL_WIDTH), BF16),
        compiler_params=pltpu.CompilerParams(
            dimension_semantics=("arbitrary",), vmem_limit_bytes=VMEM_LIMIT),
        name="pool",
    )(u3d, pool_w16, pool_scale)


def _compress_kernel(pt_ref, src_ref, tail_ref, pos_ref, bd_ref, g_ref, kc_ref, vc_ref, buf, sem,
                     *, n_pages, npp, has_tail, paged):
    b = pl.program_id(0)
    nb = pl.num_programs(0)
    n_slabs = 2 * N_KV_HEADS

    def slab_copy(row, p, c, slot):
        kv, kvh = divmod(c, N_KV_HEADS)
        if paged:
            src = src_ref.at[pt_ref[row, p], pl.ds(c * HEAD_DIM, HEAD_DIM), :]
        else:
            src = src_ref.at[row, pl.ds(c * HEAD_DIM, HEAD_DIM), pl.ds(p * PAGE, PAGE)]
        return pltpu.make_async_copy(src, buf.at[slot, kv, :, kvh * npp + p, :], sem.at[slot])

    def tail_copy(c, slot):
        kv, kvh = divmod(c, N_KV_HEADS)
        return pltpu.make_async_copy(tail_ref.at[0, pl.ds(c * HEAD_DIM, HEAD_DIM), :],
                                     buf.at[slot, kv, :, kvh * npp + n_pages, :], sem.at[slot])

    def row_copies(row, slot, fn):
        def body(p, carry):
            for c in range(n_slabs):
                fn(slab_copy(row, p, c, slot))
            return carry
        lax.fori_loop(0, n_pages, body, 0)

    n_real = n_pages + (1 if has_tail else 0)

    @pl.when(b == 0)
    def _():
        if npp > n_real:
            for kvh in range(N_KV_HEADS):
                buf[:, :, :, kvh * npp + n_real:(kvh + 1) * npp, :] = jnp.zeros(
                    (2, 2, HEAD_DIM, npp - n_real, LANES), F32)
        row_copies(0, 0, lambda cp: cp.start())

    slot = b % 2

    @pl.when(b + 1 < nb)
    def _():
        row_copies(b + 1, 1 - slot, lambda cp: cp.start())

    if has_tail:
        for c in range(n_slabs):
            tail_copy(c, slot).start()
        for c in range(n_slabs):
            tail_copy(c, slot).wait()
    row_copies(b, slot, lambda cp: cp.wait())

    rows = N_KV_HEADS * npp

    def body(d, carry):
        acc_k, acc_v = carry
        pos = pos_ref[pl.ds(d, 1), :]
        xk = (buf[slot, 0, d] + pos).astype(BF16)
        xv = (buf[slot, 1, d] + pos).astype(BF16)
        return (acc_k + jnp.dot(xk, bd_ref[0, d], preferred_element_type=F32),
                acc_v + jnp.dot(xv, bd_ref[1, d], preferred_element_type=F32))

    zero = jnp.zeros((rows, LANES), F32)
    acc_k, acc_v = lax.fori_loop(0, HEAD_DIM, body, (zero, zero), unroll=8)
    kc_ref[0] = _half_group_norm(acc_k, g_ref[...])
    vc_ref[0] = acc_v


def _compress(page_table, src, tail, pos_t, bd, g2, *, npp, has_tail, paged):
    b, n_pages = page_table.shape
    kern = functools.partial(_compress_kernel, n_pages=n_pages, npp=npp, has_tail=has_tail, paged=paged)
    rows = N_KV_HEADS * npp
    grid_spec = pltpu.PrefetchScalarGridSpec(
        num_scalar_prefetch=1,
        grid=(b,),
        in_specs=[
            pl.BlockSpec(memory_space=pl.ANY),
            pl.BlockSpec((1, KV_COLS, LANES), lambda i, pt: (i, 0, 0)),
            pl.BlockSpec((HEAD_DIM, LANES), lambda i, pt: (0, 0)),
            pl.BlockSpec((2, HEAD_DIM, LANES, LANES), lambda i, pt: (0, 0, 0, 0)),
            pl.BlockSpec((1, LANES), lambda i, pt: (0, 0)),
        ],
        out_specs=(pl.BlockSpec((1, rows, LANES), lambda i, pt: (i, 0, 0)),
                   pl.BlockSpec((1, rows, LANES), lambda i, pt: (i, 0, 0))),
        scratch_shapes=[
            pltpu.VMEM((2, 2, HEAD_DIM, rows, LANES), F32),
            pltpu.SemaphoreType.DMA((2,)),
        ],
    )
    return pl.pallas_call(
        kern,
        grid_spec=grid_spec,
        out_shape=(jax.ShapeDtypeStruct((b, rows, LANES), F32), jax.ShapeDtypeStruct((b, rows, LANES), F32)),
        compiler_params=pltpu.CompilerParams(
            dimension_semantics=("arbitrary",), vmem_limit_bytes=VMEM_LIMIT),
        name="compress",
    )(page_table, src, tail, pos_t, bd, g2)


def _to_half(tile, src_half, dst_half):
    lane = lax.broadcasted_iota(jnp.int32, tile.shape, 1)
    src = tile if src_half == dst_half else pltpu.roll(tile, HEAD_DIM, axis=1)
    keep = (lane < HEAD_DIM) if dst_half == 0 else (lane >= HEAD_DIM)
    return jnp.where(keep, src, 0.0)


def _pair_tile(o_even, o_odd, half):
    lane = lax.broadcasted_iota(jnp.int32, o_even.shape, 1)
    if half == 0:
        return jnp.where(lane < HEAD_DIM, o_even, pltpu.roll(o_odd, HEAD_DIM, axis=1))
    return jnp.where(lane < HEAD_DIM, pltpu.roll(o_even, HEAD_DIM, axis=1), o_odd)


def _gate_tile(gates, pair, c, shape):
    lane = lax.broadcasted_iota(jnp.int32, shape, 1)
    he, ho = 2 * pair, 2 * pair + 1
    return jnp.where(lane < HEAD_DIM, gates[:, 3 * he + c:3 * he + c + 1], gates[:, 3 * ho + c:3 * ho + c + 1])


def _block_of_col(col, npp):
    return 2 * (col % npp) + col // npp


def _cmp_operand(x, npp):
    return jnp.concatenate([_to_half(x, 0, 0), _to_half(x, 1, 0)], axis=0).astype(BF16)


def _rank_select(score_t, blk_t, cols):
    rank = jnp.zeros(score_t.shape, jnp.int32)
    for r, n in cols:
        row = score_t[r:r + 1, :]
        ahead = (row > score_t) | ((row == score_t) & (blk_t > n))
        rank = rank + ahead.astype(jnp.int32)
    return ((rank < TOP_K_BLOCKS) & (score_t > -0.5)).astype(F32)


POS_HI, POS_LO = HEAD_DIM, HEAD_DIM + 1
MASK_BIG = 1e30


def _attn_prompt_kernel(q_ref, gates_ref, kc_ref, vc_ref, ks_ref, vsx_ref, kw_ref, vwx_ref, exp_ref, gp_ref, a_ref,
                        zbuf, m_sc, acc_sc, *, n_pages, npp):
    i = pl.program_id(1)
    q0 = i * Q_TILE
    rows = Q_PER_KV * Q_TILE
    ncols = 2 * npp
    qpos = q0 + lax.broadcasted_iota(jnp.int32, (Q_TILE, 1), 0)
    blk = _block_of_col(lax.broadcasted_iota(jnp.int32, (Q_TILE, ncols), 1), npp)
    real_cols = [(half * n_pages + p, 2 * p + half) for half in range(2) for p in range(n_pages)]

    g = gates_ref[0]
    g_hi = g.astype(BF16)
    g_lo = (g - g_hi.astype(F32)).astype(BF16)
    gexp = (jnp.dot(g_hi, gp_ref[...], preferred_element_type=F32)
            + jnp.dot(g_lo, gp_ref[...], preferred_element_type=F32))

    r_key = lax.broadcasted_iota(jnp.int32, (Q_TILE, K_CHUNK), 1)
    r_qry = lax.broadcasted_iota(jnp.int32, (Q_TILE, K_CHUNK), 0)
    keep = {"causal": r_key <= r_qry, "lower": r_key > r_qry}

    all_rows = N_HEADS * Q_TILE

    def branch(lhs, k_ref, vx_ref, slc, chunks, c_lo, c_hi):
        m_sc[...] = jnp.full((all_rows, LANES), NEG, F32)

        def scores(c, kind):
            rr = lax.broadcasted_iota(jnp.int32, (HEAD_DIM, K_CHUNK), 0)
            tt = lax.broadcasted_iota(jnp.int32, (HEAD_DIM, K_CHUNK), 1).astype(F32)
            hi = ((c - i) * K_CHUNK).astype(F32)
            pos = jnp.where(rr == 0, hi, jnp.where(rr == 1, tt, 0.0)).astype(BF16)
            for kvh in range(N_KV_HEADS):
                parts = [k_ref[0, c, kvh * HEAD_DIM:(kvh + 1) * HEAD_DIM, :], pos]
                if slc:
                    parts.append(exp_ref[c])
                kx = jnp.concatenate(parts, axis=0)
                for gq in range(Q_PER_KV):
                    sl = slice((kvh * Q_PER_KV + gq) * Q_TILE, (kvh * Q_PER_KV + gq + 1) * Q_TILE)
                    zg = jnp.dot(lhs[sl], kx, preferred_element_type=F32)
                    if kind is not None:
                        zg = jnp.where(keep[kind], zg, NEG)
                    zbuf[c, sl, :] = zg
                    m_sc[sl] = jnp.maximum(m_sc[sl], jnp.maximum(zg[:, :LANES], zg[:, LANES:]))

        for lo, hi_, kind, cond in chunks:
            if kind is None and cond is None:
                def body(c, carry):
                    scores(c, None)
                    return carry
                lax.fori_loop(lo, hi_, body, 0)
            elif cond is None:
                scores(lo, kind)
            else:
                pl.when(cond)(functools.partial(scores, lo, kind))

        m = jnp.broadcast_to(jnp.max(m_sc[...], axis=-1, keepdims=True), (all_rows, LANES))
        acc_sc[...] = jnp.zeros((all_rows, LANES), F32)

        def pass2(c, carry):
            for h in range(N_HEADS):
                sl = slice(h * Q_TILE, (h + 1) * Q_TILE)
                z = zbuf[c, sl, :]
                e = jnp.concatenate([jnp.exp(z[:, :LANES] - m[sl]), jnp.exp(z[:, LANES:] - m[sl])], axis=1)
                acc_sc[sl] += jnp.dot(e.astype(BF16), vx_ref[0, c, h // Q_PER_KV], preferred_element_type=F32)
            return carry

        lax.fori_loop(c_lo, c_hi, pass2, 0)
        acc = acc_sc[...]
        den = pltpu.roll(acc, HEAD_DIM, axis=1)
        return acc / jnp.where(den > 0.0, den, 1.0)

    o_cmp_all, lhs_slc_all, lhs_win_all = [], [], []
    for kvh in range(N_KV_HEADS):
        slopes = [_slope(kvh * Q_PER_KV + g) for g in range(Q_PER_KV)]
        lane = lax.broadcasted_iota(jnp.int32, (Q_TILE, LANES), 1)
        pieces, pieces_x = [], []
        for gq in range(Q_PER_KV):
            h = kvh * Q_PER_KV + gq
            tile = q_ref[0, :, (h // 2) * LANES:(h // 2 + 1) * LANES].astype(F32) * SCALE
            low = _to_half(tile, h % 2, 0)
            pieces.append(low)
            pieces_x.append(jnp.where((lane == POS_HI) | (lane == POS_LO), slopes[gq], low))
        qk = jnp.concatenate(pieces, axis=0).astype(BF16)
        qk_x = jnp.concatenate(pieces_x, axis=0).astype(BF16)

        kc = _cmp_operand(kc_ref[0, kvh], npp)
        vc = _cmp_operand(vc_ref[0, kvh], npp)
        s = _dot_t(qk, kc)
        dist_c = qpos - (blk * CMP_BLOCK + (CMP_BLOCK - 1))
        mask_c = dist_c >= 0
        dist_cf = dist_c.astype(F32)
        p_list = []
        p_kv = jnp.zeros((Q_TILE, ncols), F32)
        for g in range(Q_PER_KV):
            z = jnp.where(mask_c, s[g * Q_TILE:(g + 1) * Q_TILE] - slopes[g] * dist_cf, NEG)
            mx = jnp.max(z, axis=-1, keepdims=True)
            e = jnp.where(mask_c, jnp.exp(z - mx), 0.0)
            den = jnp.sum(e, axis=-1, keepdims=True)
            p = e / jnp.where(den > 0.0, den, 1.0)
            p_kv = p_kv + p
            p_list.append(p.astype(BF16))
        o_cmp = jnp.dot(jnp.concatenate(p_list, axis=0), vc, preferred_element_type=F32)

        cand = blk * CMP_BLOCK <= qpos
        forced = (blk == qpos // CMP_BLOCK) | (blk == 0)
        score = jnp.where(forced, FORCE_SCORE, jnp.where(cand, p_kv, -1.0))
        score_t = score.T
        st = jnp.concatenate([score_t[:n_pages], score_t[npp:npp + n_pages]], axis=0)
        rr = lax.broadcasted_iota(jnp.int32, (2 * n_pages, Q_TILE), 0)
        bt = jnp.where(rr < n_pages, 2 * rr, 2 * (rr - n_pages) + 1)
        sel_s = _rank_select(st, bt, [(r, n) for r, (_, n) in enumerate(real_cols)])
        pad = jnp.zeros((npp - n_pages, Q_TILE), F32)
        sel_t = jnp.concatenate([sel_s[:n_pages], pad, sel_s[n_pages:], pad], axis=0)
        sel = sel_t.T
        sel_bias = jnp.where(sel > 0.5, 0.0, -MASK_BIG).astype(BF16)
        lhs_slc_all.append(jnp.concatenate([qk_x, jnp.concatenate([sel_bias] * Q_PER_KV, axis=0)], axis=1))
        lhs_win_all.append(qk_x)
        o_cmp_all.append(o_cmp)

    o_slc = branch(jnp.concatenate(lhs_slc_all, axis=0), ks_ref, vsx_ref, True,
                   [(0, i, None, None), (i, None, "causal", None)], 0, i + 1)
    o_win = branch(jnp.concatenate(lhs_win_all, axis=0), kw_ref, vwx_ref, False,
                   [(i - 2, None, "lower", i >= 2), (i - 1, None, None, i >= 1), (i, None, "causal", None)],
                   jnp.maximum(i - 2, 0), i + 1)
    o_cmp = jnp.concatenate(o_cmp_all, axis=0)

    for pair in range(N_HEADS // 2):
        e_sl = slice(2 * pair * Q_TILE, (2 * pair + 1) * Q_TILE)
        o_sl = slice((2 * pair + 1) * Q_TILE, (2 * pair + 2) * Q_TILE)
        tile = jnp.zeros((Q_TILE, LANES), F32)
        for br, o in enumerate((o_cmp, o_slc, o_win)):
            col = br * ATTN_WIDTH + pair * LANES
            tile = tile + gexp[:, col:col + LANES] * _pair_tile(o[e_sl], o[o_sl], 0)
        a_ref[0, :, pair * LANES:(pair + 1) * LANES] = tile.astype(BF16)


def _attn_prompt(q, gates, kc, vc, ks16, vsx, kw16, vwx, expand, gate_place, n_pages):
    b, s, _ = q.shape
    npp = kc.shape[2]
    n_chunks = s // K_CHUNK
    rows = Q_PER_KV * Q_TILE
    assert WINDOW == 2 * K_CHUNK and Q_TILE == K_CHUNK
    kern = functools.partial(_attn_prompt_kernel, n_pages=n_pages, npp=npp)
    per_b4 = lambda bi, i: (bi, 0, 0, 0)
    per_b5 = lambda bi, i: (bi, 0, 0, 0, 0)
    return pl.pallas_call(
        kern,
        grid=(b, s // Q_TILE),
        in_specs=[
            pl.BlockSpec((1, Q_TILE, ATTN_WIDTH), lambda bi, i: (bi, i, 0)),
            pl.BlockSpec((1, Q_TILE, GATE_PAD), lambda bi, i: (bi, i, 0)),
            pl.BlockSpec((1, N_KV_HEADS, npp, LANES), per_b4),
            pl.BlockSpec((1, N_KV_HEADS, npp, LANES), per_b4),
            pl.BlockSpec((1, n_chunks, LANES, K_CHUNK), per_b4),
            pl.BlockSpec((1, n_chunks, N_KV_HEADS, K_CHUNK, LANES), per_b5),
            pl.BlockSpec((1, n_chunks, LANES, K_CHUNK), per_b4),
            pl.BlockSpec((1, n_chunks, N_KV_HEADS, K_CHUNK, LANES), per_b5),
            pl.BlockSpec((n_chunks, 2 * npp, K_CHUNK), lambda bi, i: (0, 0, 0)),
            pl.BlockSpec((GATE_PAD, 3 * ATTN_WIDTH), lambda bi, i: (0, 0)),
        ],
        out_specs=pl.BlockSpec((1, Q_TILE, ATTN_WIDTH), lambda bi, i: (bi, i, 0)),
        out_shape=jax.ShapeDtypeStruct((b, s, ATTN_WIDTH), BF16),
        scratch_shapes=[
            pltpu.VMEM((n_chunks, N_KV_HEADS * rows, K_CHUNK), F32),
            pltpu.VMEM((N_KV_HEADS * rows, LANES), F32),
            pltpu.VMEM((N_KV_HEADS * rows, LANES), F32),
        ],
        compiler_params=pltpu.CompilerParams(
            dimension_semantics=("arbitrary", "arbitrary"), vmem_limit_bytes=VMEM_LIMIT),
        name="attn_prompt",
    )(q, gates, kc, vc, ks16, vsx, kw16, vwx, expand, gate_place)


S_ROWS = LANES
S_CHUNK = 1024


def _attn_sample_kernel(pt_ref, q_ref, gates_ref, kc_ref, vc_ref, kvs_new_ref, kvw_new_ref, win_ref, cache_ref,
                        a_ref, kbuf, zbuf, sem, *, n_pages, past, ds, npp):
    b = pl.program_id(0)
    nb = pl.num_programs(0)
    n_keys = n_pages * PAGE
    n_real = N_HEADS * ds
    ncols = 2 * npp
    half_cols = KV_COLS // 2
    n_blocks = (past + ds + CMP_BLOCK - 1) // CMP_BLOCK

    def page_copy(row, p, slot):
        return pltpu.make_async_copy(
            cache_ref.at[pt_ref[row, p]], kbuf.at[slot, :, pl.ds(p * PAGE, PAGE)], sem.at[slot])

    def row_copies(row, slot, fn):
        def body(p, carry):
            fn(page_copy(row, p, slot))
            return carry
        lax.fori_loop(0, n_pages, body, 0)

    @pl.when(b == 0)
    def _():
        row_copies(0, 0, lambda cp: cp.start())

    slot = b % 2

    @pl.when(b + 1 < nb)
    def _():
        row_copies(b + 1, 1 - slot, lambda cp: cp.start())

    by_kvh, low = [], []
    for h in range(N_HEADS):
        tile = q_ref[0, :, (h // 2) * LANES:(h // 2 + 1) * LANES].astype(F32) * SCALE
        by_kvh.append(_to_half(tile, h % 2, h // Q_PER_KV))
        low.append(_to_half(tile, h % 2, 0))
    zpad = jnp.zeros((S_ROWS - n_real, LANES), F32)
    qrows = jnp.concatenate(by_kvh + [zpad], axis=0).astype(BF16)
    qlow = jnp.concatenate(low + [zpad], axis=0).astype(BF16)

    r_col = lax.broadcasted_iota(jnp.int32, (S_ROWS, 1), 0)
    q_of_r = r_col % ds
    h_of_r = r_col // ds
    qp_r = past + q_of_r
    sl_r = jnp.zeros((S_ROWS, 1), F32)
    for h in range(N_HEADS):
        sl_r = jnp.where(h_of_r == h, _slope(h), sl_r)
    kvh_r = h_of_r // Q_PER_KV

    blk = _block_of_col(lax.broadcasted_iota(jnp.int32, (S_ROWS, ncols), 1), npp)
    s = jnp.where(kvh_r == 0, _dot_t(qlow, _cmp_operand(kc_ref[0, 0], npp)),
                  _dot_t(qlow, _cmp_operand(kc_ref[0, 1], npp)))
    dist_c = qp_r - (blk * CMP_BLOCK + (CMP_BLOCK - 1))
    mask_c = dist_c >= 0
    z = jnp.where(mask_c, s - sl_r * dist_c.astype(F32), NEG)
    mx = jnp.max(z, axis=-1, keepdims=True)
    e = jnp.where(mask_c, jnp.exp(z - mx), 0.0)
    den = jnp.sum(e, axis=-1, keepdims=True)
    p = e / jnp.where(den > 0.0, den, 1.0)
    pb = p.astype(BF16)
    o_cmp = jnp.where(kvh_r == 0, jnp.dot(pb, _cmp_operand(vc_ref[0, 0], npp), preferred_element_type=F32),
                      jnp.dot(pb, _cmp_operand(vc_ref[0, 1], npp), preferred_element_type=F32))

    kq = N_KV_HEADS * ds
    p_kv = []
    for kvh in range(N_KV_HEADS):
        acc = jnp.zeros((ds, ncols), F32)
        for g in range(Q_PER_KV):
            r0 = (kvh * Q_PER_KV + g) * ds
            acc = acc + p[r0:r0 + ds]
        p_kv.append(acc)
    p_kv = jnp.concatenate(p_kv, axis=0)
    blk2 = _block_of_col(lax.broadcasted_iota(jnp.int32, (kq, ncols), 1), npp)
    qp2 = past + lax.broadcasted_iota(jnp.int32, (kq, 1), 0) % ds
    cand = blk2 * CMP_BLOCK <= qp2
    forced = (blk2 == qp2 // CMP_BLOCK) | (blk2 == 0)
    score = jnp.where(forced, FORCE_SCORE, jnp.where(cand, p_kv, -1.0))
    rank = jnp.zeros((kq, ncols), jnp.int32)
    for n in range(n_blocks):
        c = (n % 2) * npp + n // 2
        col = score[:, c:c + 1]
        ahead = (col > score) | ((col == score) & (blk2 > n))
        rank = rank + ahead.astype(jnp.int32)
    sel2 = ((rank < TOP_K_BLOCKS) & (score > -0.5)).astype(F32)
    sel_rows = []
    for h in range(N_HEADS):
        kvh = h // Q_PER_KV
        sel_rows.append(sel2[kvh * ds:(kvh + 1) * ds])
    sel_rows.append(jnp.zeros((S_ROWS - n_real, ncols), F32))
    sel_rows = jnp.concatenate(sel_rows, axis=0)
    sel16 = sel_rows.astype(BF16)

    new_lane = lax.broadcasted_iota(jnp.int32, (1, LANES), 1)

    def new_scores(ref, extra_mask):
        k_new = ref[0, :half_cols, :].astype(BF16)
        s_new = jnp.dot(qrows, k_new, preferred_element_type=F32)
        dist = q_of_r - new_lane
        mask = (new_lane < ds) & (dist >= 0) & extra_mask
        return jnp.where(mask, s_new - sl_r * dist.astype(F32), NEG)

    def weighted_new(ref, e_new):
        return _dot_t(e_new.astype(BF16), ref[0, half_cols:, :].astype(BF16))

    row_copies(b, slot, lambda cp: cp.wait())
    n_chunks = n_keys // S_CHUNK

    def pass1(c, mrow):
        k0 = pl.multiple_of(c * S_CHUNK, S_CHUNK)
        kt = kbuf[slot, :half_cols, pl.ds(k0, S_CHUNK)].astype(BF16)
        st = jnp.dot(qrows, kt, preferred_element_type=F32)
        j = _block_of_col(lax.broadcasted_iota(jnp.int32, (ncols, S_CHUNK), 0), npp)
        t = k0 + lax.broadcasted_iota(jnp.int32, (ncols, S_CHUNK), 1)
        expand = (j == t // CMP_BLOCK).astype(BF16)
        chosen = jnp.dot(sel16, expand, preferred_element_type=F32) > 0.5
        tpos = k0 + lax.broadcasted_iota(jnp.int32, (1, S_CHUNK), 1)
        dist = qp_r - tpos
        zc = jnp.where(chosen & (dist >= 0), st - sl_r * dist.astype(F32), NEG)
        zbuf[c] = zc
        return jnp.maximum(mrow, jnp.max(zc, axis=-1, keepdims=True))

    mrow = lax.fori_loop(0, n_chunks, pass1, jnp.full((S_ROWS, 1), NEG, F32))
    c_last = ((past // CMP_BLOCK) % 2) * npp + (past // CMP_BLOCK) // 2
    z_new = new_scores(kvs_new_ref, sel_rows[:, c_last:c_last + 1] > 0.5)
    m_r = jnp.maximum(mrow, jnp.max(z_new, axis=-1, keepdims=True))

    def pass2(c, carry):
        acc, den_r = carry
        k0 = pl.multiple_of(c * S_CHUNK, S_CHUNK)
        zc = zbuf[c]
        ec = jnp.where(zc > 0.5 * NEG, jnp.exp(zc - m_r), 0.0)
        vt = kbuf[slot, half_cols:, pl.ds(k0, S_CHUNK)].astype(BF16)
        return acc + _dot_t(ec.astype(BF16), vt), den_r + jnp.sum(ec, axis=-1, keepdims=True)

    e_new = jnp.where(z_new > 0.5 * NEG, jnp.exp(z_new - m_r), 0.0)
    acc, den_r = lax.fori_loop(0, n_chunks, pass2,
                               (weighted_new(kvs_new_ref, e_new), jnp.sum(e_new, axis=-1, keepdims=True)))
    o_slc = acc / jnp.where(den_r > 0.0, den_r, 1.0)

    win_buf = win_ref.shape[2]
    kw = win_ref[0, :half_cols, :].astype(BF16)
    st = jnp.dot(qrows, kw, preferred_element_type=F32)
    kpos = past - win_buf + lax.broadcasted_iota(jnp.int32, (1, win_buf), 1)
    dist = qp_r - kpos
    mask = (dist >= 0) & (dist < WINDOW) & (kpos >= 0)
    z_w = jnp.where(mask, st - sl_r * dist.astype(F32), NEG)
    zw_new = new_scores(kvw_new_ref, True)
    m_w = jnp.maximum(jnp.max(z_w, axis=-1, keepdims=True), jnp.max(zw_new, axis=-1, keepdims=True))
    e_w = jnp.where(z_w > 0.5 * NEG, jnp.exp(z_w - m_w), 0.0)
    ew_new = jnp.where(zw_new > 0.5 * NEG, jnp.exp(zw_new - m_w), 0.0)
    acc_w = _dot_t(e_w.astype(BF16), win_ref[0, half_cols:, :].astype(BF16)) + weighted_new(kvw_new_ref, ew_new)
    den_w = jnp.sum(e_w, axis=-1, keepdims=True) + jnp.sum(ew_new, axis=-1, keepdims=True)
    o_win = acc_w / jnp.where(den_w > 0.0, den_w, 1.0)

    gates = gates_ref[0]
    for pair in range(N_HEADS // 2):
        kvh = (2 * pair) // Q_PER_KV
        e_sl = slice(2 * pair * ds, (2 * pair + 1) * ds)
        o_sl = slice((2 * pair + 1) * ds, (2 * pair + 2) * ds)
        shape = (ds, LANES)
        tile = (_gate_tile(gates, pair, 0, shape) * _pair_tile(o_cmp[e_sl], o_cmp[o_sl], 0)
                + _gate_tile(gates, pair, 1, shape) * _pair_tile(o_slc[e_sl], o_slc[o_sl], kvh)
                + _gate_tile(gates, pair, 2, shape) * _pair_tile(o_win[e_sl], o_win[o_sl], kvh))
        a_ref[0, :, pair * LANES:(pair + 1) * LANES] = tile.astype(BF16)


def _attn_sample(page_table, q, gates, kc, vc, kvs_new, kvw_new, state_win_t, cache_pages, *, past):
    b, n_pages = page_table.shape
    ds = q.shape[1]
    npp = kc.shape[2]
    win_buf = state_win_t.shape[2]
    n_keys = n_pages * PAGE
    kern = functools.partial(_attn_sample_kernel, n_pages=n_pages, past=past, ds=ds, npp=npp)
    grid_spec = pltpu.PrefetchScalarGridSpec(
        num_scalar_prefetch=1,
        grid=(b,),
        in_specs=[
            pl.BlockSpec((1, ds, ATTN_WIDTH), lambda i, pt: (i, 0, 0)),
            pl.BlockSpec((1, ds, GATE_PAD), lambda i, pt: (i, 0, 0)),
            pl.BlockSpec((1, N_KV_HEADS, npp, LANES), lambda i, pt: (i, 0, 0, 0)),
            pl.BlockSpec((1, N_KV_HEADS, npp, LANES), lambda i, pt: (i, 0, 0, 0)),
            pl.BlockSpec((1, KV_COLS, LANES), lambda i, pt: (i, 0, 0)),
            pl.BlockSpec((1, KV_COLS, LANES), lambda i, pt: (i, 0, 0)),
            pl.BlockSpec((1, KV_COLS, win_buf), lambda i, pt: (i, 0, 0)),
            pl.BlockSpec(memory_space=pl.ANY),
        ],
        out_specs=pl.BlockSpec((1, ds, ATTN_WIDTH), lambda i, pt: (i, 0, 0)),
        scratch_shapes=[
            pltpu.VMEM((2, KV_COLS, n_keys), F32),
            pltpu.VMEM((n_keys // S_CHUNK, S_ROWS, S_CHUNK), F32),
            pltpu.SemaphoreType.DMA((2,)),
        ],
    )
    return pl.pallas_call(
        kern,
        grid_spec=grid_spec,
        out_shape=jax.ShapeDtypeStruct((b, ds, ATTN_WIDTH), BF16),
        compiler_params=pltpu.CompilerParams(
            dimension_semantics=("arbitrary",), vmem_limit_bytes=VMEM_LIMIT),
        name="attn_sample",
    )(page_table, q, gates, kc, vc, kvs_new, kvw_new, state_win_t, cache_pages)


X_TILES = D_MODEL // LANES
PAY_SUB = 2 * X_TILES
GROUP_LANE = 0


def _rows_to_tiles(ref, lo, val):
    for k in range(val.shape[1] // LANES):
        ref[:, lo + k, :] = val[:, k * LANES:(k + 1) * LANES]


def _tiles_to_rows(ref, lo, n):
    return jnp.concatenate([ref[:, lo + k, :] for k in range(n)], axis=1)


def _ffn_kernel(x_ref, a_ref, m_ref, wo_ref, g2_ref, wr_ref, *out_refs, packed):
    h = (x_ref[...]
         + jnp.dot(a_ref[...], wo_ref[:ATTN_WIDTH, :], preferred_element_type=F32)
         + jnp.dot(m_ref[...], wo_ref[ATTN_WIDTH:, :], preferred_element_type=F32))
    ms = jnp.mean(h * h, axis=-1, keepdims=True)
    n2f = (h * lax.rsqrt(ms + EPS)) * g2_ref[...]
    n2 = n2f.astype(BF16)
    logits = jnp.dot(n2, wr_ref[...], preferred_element_type=F32)
    lane = lax.broadcasted_iota(jnp.int32, logits.shape, 1)
    big = jnp.int32(LANES)

    def masked_softmax(mask):
        zz = jnp.where(mask, logits, NEG)
        mx = jnp.max(zz, axis=-1, keepdims=True)
        ee = jnp.where(mask, jnp.exp(zz - mx), 0.0)
        return ee / jnp.sum(ee, axis=-1, keepdims=True)

    def first_argmax(vals, mask):
        v = jnp.max(jnp.where(mask, vals, -1.0), axis=-1, keepdims=True)
        idx = jnp.min(jnp.where(mask & (vals == v), lane, big), axis=-1, keepdims=True)
        return v, idx

    is_g = lane < N_EXPERT_GROUPS
    pg = masked_softmax(is_g)
    g_val, g_idx = first_argmax(pg, is_g)
    e_lane = lane - ROUTER_LANE0
    in_grp = (e_lane >= 0) & (e_lane < N_EXPERTS) & (e_lane // EXPERTS_PER_GROUP == g_idx)
    pe = masked_softmax(in_grp)
    v1, i1 = first_argmax(pe, in_grp)
    rest = in_grp & (lane != i1)
    v2, i2 = first_argmax(pe, rest)
    scale = g_val / (v1 + v2)
    gate = jnp.where(lane == i1, v1 * scale, jnp.where(lane == i2, v2 * scale, 0.0))
    gate = jnp.where(lane == GROUP_LANE, g_idx.astype(F32), gate)
    if packed:
        h_ref, pay_ref, gate_ref = out_refs
        h_ref[...] = h
        gate_ref[...] = gate
        _rows_to_tiles(pay_ref, 0, n2f)
        for k in range(X_TILES, PAY_SUB):
            pay_ref[:, k, :] = gate
    else:
        h_ref, n2_ref, gate_ref = out_refs
        h_ref[...] = h
        n2_ref[...] = n2
        gate_ref[...] = gate


def _ffn(x2d, a2d, m2d, w_out16, g2, w_router16, tm, packed):
    t = x2d.shape[0]
    row = lambda i: (i, 0)
    const = lambda i: (0, 0)
    if packed:
        out_specs = (pl.BlockSpec((tm, D_MODEL), row), pl.BlockSpec((tm, PAY_SUB, LANES), lambda i: (i, 0, 0)),
                     pl.BlockSpec((tm, LANES), row))
        out_shape = (jax.ShapeDtypeStruct((t, D_MODEL), F32), jax.ShapeDtypeStruct((t, PAY_SUB, LANES), F32),
                     jax.ShapeDtypeStruct((t, LANES), F32))
    else:
        out_specs = (pl.BlockSpec((tm, D_MODEL), row), pl.BlockSpec((tm, D_MODEL), row),
                     pl.BlockSpec((tm, LANES), row))
        out_shape = (jax.ShapeDtypeStruct((t, D_MODEL), F32), jax.ShapeDtypeStruct((t, D_MODEL), BF16),
                     jax.ShapeDtypeStruct((t, LANES), F32))
    return pl.pallas_call(
        functools.partial(_ffn_kernel, packed=packed),
        grid=(t // tm,),
        in_specs=[
            pl.BlockSpec((tm, D_MODEL), row),
            pl.BlockSpec((tm, ATTN_WIDTH), row),
            pl.BlockSpec((tm, POOL_WIDTH), row),
            pl.BlockSpec((D_MODEL, D_MODEL), const),
            pl.BlockSpec((1, D_MODEL), const),
            pl.BlockSpec((D_MODEL, LANES), const),
        ],
        out_specs=out_specs,
        out_shape=out_shape,
        compiler_params=pltpu.CompilerParams(
            dimension_semantics=("arbitrary",), vmem_limit_bytes=VMEM_LIMIT),
        name="ffn",
    )(x2d, a2d, m2d, w_out16, g2, w_router16)


MOE_ROWS = 256
ROW_CHUNK = 512
PLAN_TILE = 1024


def _plan_kernel(gate_ref, tri_ref, out_ref, cnt_ref, carry):
    @pl.when(pl.program_id(0) == 0)
    def _():
        carry[...] = jnp.zeros(carry.shape, F32)

    gl = gate_ref[...]
    lane = lax.broadcasted_iota(jnp.int32, gl.shape, 1)
    gid = gl[:, GROUP_LANE:GROUP_LANE + 1]
    onehot = jnp.where((lane < N_EXPERT_GROUPS) & (lane.astype(F32) == gid), 1.0, 0.0)
    before = jnp.dot(tri_ref[...], onehot.astype(BF16), preferred_element_type=F32) + carry[0:1, :]
    rank = jnp.sum(onehot * before, axis=-1, keepdims=True)
    out_ref[...] = jnp.where(lane == 0, rank, jnp.where(lane == 1, gid, 0.0))
    carry[0:1, :] = carry[0:1, :] + jnp.sum(onehot, axis=0, keepdims=True)
    cnt_ref[...] = carry[...]


def _plan(gate):
    t = gate.shape[0]
    tri = jnp.asarray(np.tril(np.ones((PLAN_TILE, PLAN_TILE), np.float32), -1), dtype=BF16)
    return pl.pallas_call(
        _plan_kernel,
        grid=(t // PLAN_TILE,),
        in_specs=[pl.BlockSpec((PLAN_TILE, LANES), lambda i: (i, 0)),
                  pl.BlockSpec((PLAN_TILE, PLAN_TILE), lambda i: (0, 0))],
        out_specs=(pl.BlockSpec((PLAN_TILE, LANES), lambda i: (i, 0)), pl.BlockSpec((8, LANES), lambda i: (0, 0))),
        out_shape=(jax.ShapeDtypeStruct((t, LANES), F32), jax.ShapeDtypeStruct((8, LANES), F32)),
        scratch_shapes=[pltpu.VMEM((8, LANES), F32)],
        compiler_params=pltpu.CompilerParams(
            dimension_semantics=("arbitrary",), vmem_limit_bytes=VMEM_LIMIT),
        name="moe_plan",
    )(gate, tri)


def _scatter_kernel(meta_ref, slot_ref, pay_ref, xs_ref, zrow, sem, zsem):
    c = pl.program_id(0)
    nc = pl.num_programs(0)

    @pl.when(c == 0)
    def _():
        zrow[...] = jnp.zeros(zrow.shape, F32)
        for g in range(N_EXPERT_GROUPS + 1):
            lo = meta_ref[g]
            hi = meta_ref[N_EXPERT_GROUPS + 1 + g]

            def zero_start(r, carry):
                pltpu.make_async_copy(zrow.at[pl.ds(0, 1)], xs_ref.at[pl.ds(r, 1)], zsem.at[0]).start()
                return carry

            def zero_wait(r, carry):
                pltpu.make_async_copy(zrow.at[pl.ds(0, 1)], xs_ref.at[pl.ds(0, 1)], zsem.at[0]).wait()
                return carry

            lax.fori_loop(lo, hi, zero_start, 0)
            lax.fori_loop(lo, hi, zero_wait, 0)

    def issue(u, carry):
        pltpu.make_async_copy(pay_ref.at[pl.ds(c * ROW_CHUNK + u, 1)], xs_ref.at[pl.ds(slot_ref[0, 0, u], 1)],
                              sem.at[c % 2]).start()
        return carry

    lax.fori_loop(0, ROW_CHUNK, issue, 0, unroll=8)

    def wait_chunk(s):
        pltpu.make_async_copy(pay_ref.at[pl.ds(0, ROW_CHUNK)], xs_ref.at[pl.ds(0, ROW_CHUNK)], s).wait()

    @pl.when(c > 0)
    def _():
        wait_chunk(sem.at[(c + 1) % 2])

    @pl.when(c == nc - 1)
    def _():
        wait_chunk(sem.at[c % 2])


def _scatter(meta, slot3, pay, n_slots):
    t = pay.shape[0]
    return pl.pallas_call(
        _scatter_kernel,
        grid_spec=pltpu.PrefetchScalarGridSpec(
            num_scalar_prefetch=1,
            grid=(t // ROW_CHUNK,),
            in_specs=[pl.BlockSpec((1, 1, ROW_CHUNK), lambda c, meta: (c, 0, 0), memory_space=pltpu.SMEM),
                      pl.BlockSpec(memory_space=pl.ANY)],
            out_specs=pl.BlockSpec(memory_space=pl.ANY),
            scratch_shapes=[pltpu.VMEM((1, PAY_SUB, LANES), F32), pltpu.SemaphoreType.DMA((2,)),
                            pltpu.SemaphoreType.DMA((1,))],
        ),
        out_shape=jax.ShapeDtypeStruct((n_slots, PAY_SUB, LANES), F32),
        compiler_params=pltpu.CompilerParams(dimension_semantics=("arbitrary",), vmem_limit_bytes=VMEM_LIMIT),
        name="moe_scatter",
    )(meta, slot3, pay)


def _gather_kernel(slot_ref, nslot_ref, h_ref, ys_ref, y_ref, buf, sem):
    c = pl.program_id(0)
    nc = pl.num_programs(0)

    def fetch(ref, slot):
        def issue(u, carry):
            pltpu.make_async_copy(ys_ref.at[pl.ds(ref[0, 0, u], 1)], buf.at[slot, pl.ds(u, 1)], sem.at[slot]).start()
            return carry
        lax.fori_loop(0, ROW_CHUNK, issue, 0, unroll=8)

    @pl.when(c == 0)
    def _():
        fetch(slot_ref, 0)

    @pl.when(c + 1 < nc)
    def _():
        fetch(nslot_ref, (c + 1) % 2)

    cur = c % 2
    pltpu.make_async_copy(ys_ref.at[pl.ds(0, ROW_CHUNK)], buf.at[cur], sem.at[cur]).wait()
    y_ref[...] = h_ref[...] + jnp.concatenate([buf[cur, :, k, :] for k in range(X_TILES)], axis=1)


def _gather(slot3, h, ys):
    t = h.shape[0]
    nc = t // ROW_CHUNK
    return pl.pallas_call(
        _gather_kernel,
        grid=(nc,),
        in_specs=[pl.BlockSpec((1, 1, ROW_CHUNK), lambda c: (c, 0, 0), memory_space=pltpu.SMEM),
                  pl.BlockSpec((1, 1, ROW_CHUNK), lambda c: (jnp.minimum(c + 1, nc - 1), 0, 0),
                               memory_space=pltpu.SMEM),
                  pl.BlockSpec((ROW_CHUNK, D_MODEL), lambda c: (c, 0)),
                  pl.BlockSpec(memory_space=pl.ANY)],
        out_specs=pl.BlockSpec((ROW_CHUNK, D_MODEL), lambda c: (c, 0)),
        out_shape=jax.ShapeDtypeStruct((t, D_MODEL), F32),
        scratch_shapes=[pltpu.VMEM((2, ROW_CHUNK, X_TILES, LANES), F32), pltpu.SemaphoreType.DMA((2,))],
        compiler_params=pltpu.CompilerParams(dimension_semantics=("arbitrary",), vmem_limit_bytes=VMEM_LIMIT),
        name="moe_gather",
    )(slot3, slot3, h, ys)


def _group_moe_kernel(tg_ref, nu_ref, xs_ref, wg_ref, wu_ref, wd_ref, ys_ref):
    j = pl.program_id(0)

    @pl.when(j < nu_ref[0])
    def _():
        g = tg_ref[j]
        x = _tiles_to_rows(xs_ref, 0, X_TILES).astype(BF16)
        gate = xs_ref[:, X_TILES, :]
        lane = lax.broadcasted_iota(jnp.int32, gate.shape, 1)
        y = jnp.zeros((MOE_ROWS, D_MODEL), F32)
        for e in range(EXPERTS_PER_GROUP):
            gu = jnp.dot(x, wg_ref[e], preferred_element_type=F32)
            up = jnp.dot(x, wu_ref[e], preferred_element_type=F32)
            he = (gu * (1.0 / (1.0 + jnp.exp(-gu)))) * up
            out = jnp.dot(he.astype(BF16), wd_ref[e], preferred_element_type=F32)
            col = ROUTER_LANE0 + g * EXPERTS_PER_GROUP + e
            y = y + jnp.sum(jnp.where(lane == col, gate, 0.0), axis=-1, keepdims=True) * out
        _rows_to_tiles(ys_ref, 0, y)

    @pl.when(j >= nu_ref[0])
    def _():
        ys_ref[...] = jnp.zeros(ys_ref.shape, F32)


def _group_moe(tile_group, n_used, xs, wg16, wu16, wd16):
    n_slots = xs.shape[0]
    wspec = lambda shape: pl.BlockSpec((EXPERTS_PER_GROUP,) + shape, lambda j, tg, nu: (tg[j], 0, 0))
    return pl.pallas_call(
        _group_moe_kernel,
        grid_spec=pltpu.PrefetchScalarGridSpec(
            num_scalar_prefetch=2,
            grid=(n_slots // MOE_ROWS,),
            in_specs=[pl.BlockSpec((MOE_ROWS, PAY_SUB, LANES), lambda j, tg, nu: (j, 0, 0)),
                      wspec((D_MODEL, D_EXPERT)), wspec((D_MODEL, D_EXPERT)), wspec((D_EXPERT, D_MODEL))],
            out_specs=pl.BlockSpec((MOE_ROWS, X_TILES, LANES), lambda j, tg, nu: (j, 0, 0)),
        ),
        out_shape=jax.ShapeDtypeStruct((n_slots, X_TILES, LANES), F32),
        compiler_params=pltpu.CompilerParams(dimension_semantics=("arbitrary",), vmem_limit_bytes=VMEM_LIMIT),
        name="moe_group",
    )(tile_group, n_used, xs, wg16, wu16, wd16)


def _routed_moe(h, pay, gate, wg16, wu16, wd16):
    t = pay.shape[0]
    n_slots = t + N_EXPERT_GROUPS * MOE_ROWS
    n_tiles = n_slots // MOE_ROWS
    plan, counts = _plan(gate)
    rank = plan[:, 0].astype(jnp.int32)
    gid = plan[:, 1].astype(jnp.int32)
    cnt = counts[0, :N_EXPERT_GROUPS].astype(jnp.int32)
    padded = -(-cnt // MOE_ROWS) * MOE_ROWS
    ends = jnp.cumsum(padded)
    off = ends - padded
    slot = (off[gid] + rank).reshape(t // ROW_CHUNK, 1, ROW_CHUNK)
    tile_start = jnp.arange(n_tiles, dtype=jnp.int32) * MOE_ROWS
    tile_group = jnp.minimum(jnp.sum(tile_start[:, None] >= ends[None, :], axis=1), N_EXPERT_GROUPS - 1)
    n_used = (ends[-1] // MOE_ROWS).reshape(1)
    meta = jnp.concatenate([off + cnt, ends[-1:], ends, jnp.full((1,), n_slots, jnp.int32)]).astype(jnp.int32)
    xs = _scatter(meta, slot, pay, n_slots)
    ys = _group_moe(tile_group.astype(jnp.int32), n_used.astype(jnp.int32), xs, wg16, wu16, wd16)
    return _gather(slot, h, ys)


def _moe_kernel(n2_ref, gate_ref, h_ref, wg_ref, wu_ref, wd_ref, y_ref):
    e = pl.program_id(1)

    @pl.when(e == 0)
    def _():
        y_ref[...] = h_ref[...]

    n2 = n2_ref[...]
    gu = jnp.dot(n2, wg_ref[0], preferred_element_type=F32)
    up = jnp.dot(n2, wu_ref[0], preferred_element_type=F32)
    he = (gu * (1.0 / (1.0 + jnp.exp(-gu)))) * up
    out = jnp.dot(he.astype(BF16), wd_ref[0], preferred_element_type=F32)
    lane = lax.broadcasted_iota(jnp.int32, gate_ref.shape, 1)
    gcol = jnp.sum(jnp.where(lane == e + ROUTER_LANE0, gate_ref[...], 0.0), axis=-1, keepdims=True)
    y_ref[...] += gcol * out


def _moe(n2, gate, h, wg16, wu16, wd16, tm):
    t = n2.shape[0]
    row = lambda i, e: (i, 0)
    return pl.pallas_call(
        _moe_kernel,
        grid=(t // tm, N_EXPERTS),
        in_specs=[
            pl.BlockSpec((tm, D_MODEL), row),
            pl.BlockSpec((tm, LANES), row),
            pl.BlockSpec((tm, D_MODEL), row),
            pl.BlockSpec((1, D_MODEL, D_EXPERT), lambda i, e: (e, 0, 0)),
            pl.BlockSpec((1, D_MODEL, D_EXPERT), lambda i, e: (e, 0, 0)),
            pl.BlockSpec((1, D_EXPERT, D_MODEL), lambda i, e: (e, 0, 0)),
        ],
        out_specs=pl.BlockSpec((tm, D_MODEL), row),
        out_shape=jax.ShapeDtypeStruct((t, D_MODEL), F32),
        compiler_params=pltpu.CompilerParams(
            dimension_semantics=("arbitrary", "arbitrary"), vmem_limit_bytes=VMEM_LIMIT),
        name="moe",
    )(n2, gate, h, wg16, wu16, wd16)


PROJ_TILE = 512
MOE_TILE = 1024
PROMPT_NPP = LANES // 2
SAMPLE_NPP = LANES


def _round_up(n, mult):
    return -(-n // mult) * mult


def _feature_major(x5):
    b, t = x5.shape[:2]
    return jnp.transpose(x5, (0, 2, 3, 4, 1)).reshape(b, KV_COLS, t)


def _token_major(xt):
    b, _, t = xt.shape
    return jnp.transpose(xt.reshape(b, 2, N_KV_HEADS, HEAD_DIM, t), (0, 4, 1, 2, 3))


def _pad_pages(x, npp_from, npp_to):
    b = x.shape[0]
    x = x.reshape(b, N_KV_HEADS, npp_from, LANES)
    return jnp.concatenate([x, jnp.zeros((b, N_KV_HEADS, npp_to - npp_from, LANES), x.dtype)], axis=2)


def _expand_const(n_chunks, npp):
    col = np.arange(2 * npp)
    blk = 2 * (col % npp) + col // npp
    tok_blk = np.arange(n_chunks * K_CHUNK) // CMP_BLOCK
    e = (blk[None, :, None] == tok_blk.reshape(n_chunks, 1, K_CHUNK)).astype(np.float32)
    return jnp.asarray(e, dtype=BF16)


def _gate_place_const():
    gp = np.zeros((GATE_PAD, 3 * ATTN_WIDTH), np.float32)
    for h in range(N_HEADS):
        for c in range(3):
            gp[3 * h + c, c * ATTN_WIDTH + h * HEAD_DIM:c * ATTN_WIDTH + (h + 1) * HEAD_DIM] = 1.0
    return jnp.asarray(gp, dtype=BF16)


def kernel(x_prompt, x_sample, cache_cmp_kv, cache_slc_kv, state_win_kv, state_pool, page_table, norm1_g, w_in, q_norm_g, k_norm_cmp_g, k_norm_slc_g, k_norm_win_g, cmp_pos_emb, w_cmp_k, w_cmp_v, pool_w, pool_scale, w_out, norm2_g, w_router_group, w_router_expert, w_gate, w_up, w_down):
    b, s, _ = x_prompt.shape
    db, ds, _ = x_sample.shape
    n_pool, page_rows = cache_cmp_kv.shape[:2]
    assert page_rows == PAGE and s % PAGE == 0
    past = page_table.shape[1] * page_rows

    kv0 = ATTN_WIDTH
    kv1 = kv0 + KV_ROWS
    w_row = jnp.concatenate(
        [w_in[:, :kv0], w_in[:, kv1:kv1 + GATE_COLS], jnp.zeros((D_MODEL, GATE_PAD - GATE_COLS), w_in.dtype),
         w_in[:, kv1 + GATE_COLS:]], axis=1).astype(BF16)
    w_kv = w_in[:, kv0:kv1].T.astype(BF16)
    g1 = norm1_g[None, :]
    g2 = norm2_g[None, :]
    two = lambda g: jnp.tile(g, 2)[None, :]
    kg = jnp.stack([k_norm_slc_g, k_norm_win_g])[:, :, None]
    pos_t = jnp.tile(cmp_pos_emb.T, (1, 2))
    zb = jnp.zeros((HEAD_DIM, CMP_BLOCK, HEAD_DIM), F32)

    def blockdiag(w):
        wt = jnp.transpose(w, (1, 0, 2))
        return jnp.concatenate([jnp.concatenate([wt, zb], axis=2), jnp.concatenate([zb, wt], axis=2)], axis=1)

    bd = jnp.stack([blockdiag(w_cmp_k), blockdiag(w_cmp_v)]).astype(BF16)
    w_router = jnp.concatenate(
        [w_router_group, w_router_expert,
         jnp.zeros((D_MODEL, LANES - N_EXPERT_GROUPS - N_EXPERTS), w_router_group.dtype)], axis=1).astype(BF16)
    w_out16 = w_out.astype(BF16)
    pool_w16 = pool_w.astype(BF16)
    wg16, wu16, wd16 = w_gate.astype(BF16), w_up.astype(BF16), w_down.astype(BF16)
    ps = pool_scale[None, :]

    def ffn_moe(x2d, a2d, m2d, tm_ffn, tm_moe, routed):
        if routed:
            return _routed_moe(*_ffn(x2d, a2d, m2d, w_out16, g2, w_router, tm_ffn, True), wg16, wu16, wd16)
        h, n2, gate = _ffn(x2d, a2d, m2d, w_out16, g2, w_router, tm_ffn, False)
        return _moe(n2, gate, h, wg16, wu16, wd16, tm_moe)

    q, gates, u, kvc_t, kvs_t, kvw_t, ks16, vsx, kw16, vwx = _proj(
        x_prompt, g1, w_row, w_kv, two(q_norm_g), kg, PROJ_TILE)
    m_p = _pool(u, pool_w16, ps)
    pages_p = s // PAGE
    pt_p = jnp.zeros((b, pages_p), jnp.int32)
    dummy_tail = jnp.zeros((b, KV_COLS, LANES), F32)
    kc_p, vc_p = _compress(pt_p, kvc_t, dummy_tail, pos_t, bd, two(k_norm_cmp_g),
                           npp=pages_p, has_tail=False, paged=False)
    kc_p = _pad_pages(kc_p, pages_p, PROMPT_NPP)
    vc_p = _pad_pages(vc_p, pages_p, PROMPT_NPP)
    a_p = _attn_prompt(q, gates, kc_p, vc_p, ks16, vsx, kw16, vwx, _expand_const(s // K_CHUNK, PROMPT_NPP),
                       _gate_place_const(), pages_p)
    y_p = ffn_moe(x_prompt.reshape(b * s, D_MODEL), a_p.reshape(b * s, ATTN_WIDTH),
                  m_p.reshape(b * s, POOL_WIDTH), PROJ_TILE, MOE_TILE, True)

    ts = db * ds
    q_s, gates_s, u_s, kvc_st, kvs_st, kvw_st, _, _, _, _ = _proj(
        x_sample.reshape(1, ts, D_MODEL), g1, w_row, w_kv, two(q_norm_g), kg, ts)
    u_s = u_s.reshape(db, ds, POOL_WIDTH)
    pool_ext = jnp.concatenate([state_pool, u_s], axis=1)
    n_ext = pool_ext.shape[1]
    lead = _round_up(n_ext, 8) - n_ext
    pool_in = jnp.concatenate([jnp.zeros((db, lead, POOL_WIDTH), F32), pool_ext], axis=1)
    m_s = _pool(pool_in, pool_w16, ps)[:, lead + n_ext - ds:]

    def new_rows_t(xt):
        x = jnp.transpose(xt.reshape(KV_COLS, db, ds), (1, 0, 2))
        return jnp.concatenate([x, jnp.zeros((db, KV_COLS, LANES - ds), F32)], axis=2)

    kvc_new, kvs_new, kvw_new = new_rows_t(kvc_st), new_rows_t(kvs_st), new_rows_t(kvw_st)
    n_pages = page_table.shape[1]
    npp_c = _round_up(n_pages + 1, 8)
    cmp_pages = _feature_major(cache_cmp_kv)
    slc_pages = _feature_major(cache_slc_kv)
    kc_s, vc_s = _compress(page_table, cmp_pages, kvc_new, pos_t, bd, two(k_norm_cmp_g),
                           npp=npp_c, has_tail=True, paged=True)
    kc_s = _pad_pages(kc_s, npp_c, SAMPLE_NPP)
    vc_s = _pad_pages(vc_s, npp_c, SAMPLE_NPP)
    win_t = _feature_major(state_win_kv)
    a_s = _attn_sample(page_table, q_s.reshape(db, ds, ATTN_WIDTH), gates_s.reshape(db, ds, GATE_PAD),
                       kc_s, vc_s, kvs_new, kvw_new, win_t, slc_pages, past=past)
    y_s = ffn_moe(x_sample.reshape(ts, D_MODEL), a_s.reshape(ts, ATTN_WIDTH), m_s.reshape(ts, POOL_WIDTH), ts, ts, False)

    win_keep = min(WINDOW, s)
    sample5 = lambda xt: jnp.transpose(xt.reshape(2, N_KV_HEADS, HEAD_DIM, db, ds), (3, 4, 0, 1, 2))
    win_ctx_t = jnp.concatenate([win_t, kvw_new[:, :, :ds]], axis=2)
    return (y_p.reshape(b, s, D_MODEL), y_s.reshape(db, ds, D_MODEL),
            _token_major(kvc_t), sample5(kvc_st),
            _token_major(kvs_t), sample5(kvs_st),
            _token_major(kvw_t[:, :, s - win_keep:]), _token_major(win_ctx_t[:, :, ds:]),
            u[:, s - POOL_BUF:], pool_ext[:, ds:])
```

```python
import functools

import jax
import jax.numpy as jnp
import numpy as np
from jax import lax
from jax.experimental import pallas as pl
from jax.experimental.pallas import tpu as pltpu

F32 = jnp.float32
BF16 = jnp.bfloat16

D_MODEL = 1024
N_HEADS = 8
HEAD_DIM = 64
N_KV_HEADS = 2
Q_PER_KV = N_HEADS // N_KV_HEADS
ATTN_WIDTH = N_HEADS * HEAD_DIM
KV_COLS = 2 * N_KV_HEADS * HEAD_DIM
GATE_COLS = 3 * N_HEADS
POOL_WIDTH = D_MODEL - ATTN_WIDTH
POOL_WINDOWS = (2, 4, 8, 16)
POOL_GROUP_WIDTH = POOL_WIDTH // len(POOL_WINDOWS)
POOL_BUF = max(POOL_WINDOWS) - 1
CMP_BLOCK = 64
TOP_K_BLOCKS = 16
WINDOW = 512
FORCE_SCORE = 1.0e4
N_EXPERT_GROUPS = 4
EXPERTS_PER_GROUP = 4
N_EXPERTS = N_EXPERT_GROUPS * EXPERTS_PER_GROUP
D_EXPERT = 512
EPS = 1e-6
NEG = -1e30
SCALE = HEAD_DIM ** -0.5

LANES = 128
PAGE = 2 * CMP_BLOCK
GATE_PAD = LANES
ROW_COLS = ATTN_WIDTH + GATE_PAD + POOL_WIDTH
OFF_GATE = ATTN_WIDTH
OFF_U = OFF_GATE + GATE_PAD
KV_ROWS = 3 * KV_COLS
ROUTER_LANE0 = N_EXPERT_GROUPS
Q_TILE = 256
K_CHUNK = 256
VMEM_LIMIT = 56 * 1024 * 1024


def _slope(h):
    return float(2.0 ** (-8.0 * (h + 1) / N_HEADS))


def _half_group_norm(v, g):
    lane = lax.broadcasted_iota(jnp.int32, v.shape, 1)
    lo = lane < HEAD_DIM
    v2 = v * v
    s_lo = jnp.sum(jnp.where(lo, v2, 0.0), axis=-1, keepdims=True)
    s_hi = jnp.sum(jnp.where(lo, 0.0, v2), axis=-1, keepdims=True)
    r_lo = lax.rsqrt(s_lo * (1.0 / HEAD_DIM) + EPS)
    r_hi = lax.rsqrt(s_hi * (1.0 / HEAD_DIM) + EPS)
    return (v * jnp.where(lo, r_lo, r_hi)) * g


def _dot_t(a, b):
    return lax.dot_general(a, b, (((1,), (1,)), ((), ())), preferred_element_type=F32)


def _proj_kernel(x_ref, g1_ref, wr_ref, wkv_ref, qg_ref, kg_ref,
                 q_ref, gates_ref, u_ref, kvc_ref, kvs_ref, kvw_ref, ks16_ref, vsx_ref, kw16_ref, vwx_ref):
    x = x_ref[0]
    tm = x.shape[0]
    ms = jnp.mean(x * x, axis=-1, keepdims=True)
    n = ((x * lax.rsqrt(ms + EPS)) * g1_ref[...]).astype(BF16)
    p = jnp.dot(n, wr_ref[...], preferred_element_type=F32)
    pt = _dot_t(wkv_ref[...], n)
    for t in range(ATTN_WIDTH // LANES):
        sl = slice(t * LANES, (t + 1) * LANES)
        q_ref[0, :, sl] = _half_group_norm(p[:, sl], qg_ref[...]).astype(BF16)
    gates_ref[0] = 1.0 / (1.0 + jnp.exp(-p[:, OFF_GATE:OFF_GATE + GATE_PAD]))
    u_ref[0] = p[:, OFF_U:OFF_U + POOL_WIDTH]
    kvc_ref[0] = pt[:KV_COLS]
    half = KV_COLS // 2
    lane = lax.broadcasted_iota(jnp.int32, (tm, LANES), 1)
    for bi, (out32, k16, vx16) in enumerate(((kvs_ref, ks16_ref, vsx_ref), (kvw_ref, kw16_ref, vwx_ref))):
        off = (bi + 1) * KV_COLS
        heads = []
        for hh in range(N_KV_HEADS):
            kh = pt[off + hh * HEAD_DIM:off + (hh + 1) * HEAD_DIM]
            msk = jnp.mean(kh * kh, axis=0, keepdims=True)
            heads.append((kh * lax.rsqrt(msk + EPS)) * kg_ref[bi])
        kn = jnp.concatenate(heads, axis=0)
        v = pt[off + half:off + KV_COLS]
        out32[0, :half, :] = kn
        out32[0, half:, :] = v
        vt = v.T
        vx = (jnp.where(lane < HEAD_DIM, vt, 1.0), jnp.where(lane < HEAD_DIM, pltpu.roll(vt, HEAD_DIM, axis=1), 1.0))
        for c in range(tm // K_CHUNK):
            cs = slice(c * K_CHUNK, (c + 1) * K_CHUNK)
            k16[0, c] = kn[:, cs].astype(BF16)
            for hh in range(N_KV_HEADS):
                vx16[0, c, hh] = vx[hh][cs].astype(BF16)


def _proj(x3d, g1, w_row, w_kv, qg, kg, tm):
    b, s, _ = x3d.shape
    tok = lambda i, j: (i, j, 0)
    feat = lambda i, j: (i, 0, j)
    const2 = lambda i, j: (0, 0)
    nck = tm // K_CHUNK
    out_shape = (
        jax.ShapeDtypeStruct((b, s, ATTN_WIDTH), BF16),
        jax.ShapeDtypeStruct((b, s, GATE_PAD), F32),
        jax.ShapeDtypeStruct((b, s, POOL_WIDTH), F32),
        jax.ShapeDtypeStruct((b, KV_COLS, s), F32),
        jax.ShapeDtypeStruct((b, KV_COLS, s), F32),
        jax.ShapeDtypeStruct((b, KV_COLS, s), F32),
        jax.ShapeDtypeStruct((b, s // K_CHUNK, LANES, K_CHUNK), BF16),
        jax.ShapeDtypeStruct((b, s // K_CHUNK, N_KV_HEADS, K_CHUNK, LANES), BF16),
        jax.ShapeDtypeStruct((b, s // K_CHUNK, LANES, K_CHUNK), BF16),
        jax.ShapeDtypeStruct((b, s // K_CHUNK, N_KV_HEADS, K_CHUNK, LANES), BF16),
    )
    k_spec = pl.BlockSpec((1, nck, LANES, K_CHUNK), lambda i, j: (i, j, 0, 0))
    v_spec = pl.BlockSpec((1, nck, N_KV_HEADS, K_CHUNK, LANES), lambda i, j: (i, j, 0, 0, 0))
    out_specs = (
        pl.BlockSpec((1, tm, ATTN_WIDTH), tok),
        pl.BlockSpec((1, tm, GATE_PAD), tok),
        pl.BlockSpec((1, tm, POOL_WIDTH), tok),
        pl.BlockSpec((1, KV_COLS, tm), feat),
        pl.BlockSpec((1, KV_COLS, tm), feat),
        pl.BlockSpec((1, KV_COLS, tm), feat),
        k_spec, v_spec, k_spec, v_spec,
    )
    return pl.pallas_call(
        _proj_kernel,
        grid=(b, s // tm),
        in_specs=[
            pl.BlockSpec((1, tm, D_MODEL), tok),
            pl.BlockSpec((1, D_MODEL), const2),
            pl.BlockSpec((D_MODEL, ROW_COLS), const2),
            pl.BlockSpec((KV_ROWS, D_MODEL), const2),
            pl.BlockSpec((1, LANES), const2),
            pl.BlockSpec((2, HEAD_DIM, 1), lambda i, j: (0, 0, 0)),
        ],
        out_specs=out_specs,
        out_shape=out_shape,
        compiler_params=pltpu.CompilerParams(
            dimension_semantics=("arbitrary", "arbitrary"), vmem_limit_bytes=VMEM_LIMIT),
        name="proj",
    )(x3d, g1, w_row, w_kv, qg, kg)


def _pool_kernel(u_ref, pw_ref, ps_ref, m_ref):
    u = u_ref[0]
    n = u.shape[0]

    def shift(v, k):
        rolled = pltpu.roll(v, k, axis=0)
        r = lax.broadcasted_iota(jnp.int32, v.shape, 0)
        return jnp.where(r >= k, rolled, 0.0)

    sums = []
    s = u
    k = 1
    for gi in range(len(POOL_WINDOWS)):
        s = s + shift(s, k)
        k *= 2
        sums.append(s[:, :POOL_GROUP_WIDTH])
        s = s[:, POOL_GROUP_WIDTH:]
    row = lax.broadcasted_iota(jnp.int32, (n, 1), 0)
    for gi, w in enumerate(POOL_WINDOWS):
        sl = slice(gi * POOL_GROUP_WIDTH, (gi + 1) * POOL_GROUP_WIDTH)
        cnt = jnp.minimum(row + 1, w).astype(F32)
        d = sums[gi] / cnt - u[:, sl]
        y = jnp.dot(d.astype(BF16), pw_ref[gi], preferred_element_type=F32)
        m_ref[0, :, sl] = (y * ps_ref[:, sl]).astype(BF16)


def _pool(u3d, pool_w16, pool_scale):
    b, n, _ = u3d.shape
    return pl.pallas_call(
        _pool_kernel,
        grid=(b,),
        in_specs=[
            pl.BlockSpec((1, n, POOL_WIDTH), lambda i: (i, 0, 0)),
            pl.BlockSpec((len(POOL_WINDOWS), POOL_GROUP_WIDTH, POOL_GROUP_WIDTH), lambda i: (0, 0, 0)),
            pl.BlockSpec((1, POOL_WIDTH), lambda i: (0, 0)),
        ],
        out_specs=pl.BlockSpec((1, n, POOL_WIDTH), lambda i: (i, 0, 0)),
        out_shape=jax.ShapeDtypeStruct((b, n, POOL_WIDTH), BF16),
        compiler_params=pltpu.CompilerParams(
            dimension_semantics=("arbitrary",), vmem_limit_bytes=VMEM_LIMIT),
        name="pool",
    )(u3d, pool_w16, pool_scale)


def _compress_kernel(pt_ref, src_ref, tail_ref, pos_ref, bd_ref, g_ref, kc_ref, vc_ref, buf, sem,
                     *, n_pages, npp, has_tail, paged):
    b = pl.program_id(0)
    nb = pl.num_programs(0)
    n_slabs = 2 * N_KV_HEADS

    def slab_copy(row, p, c, slot):
        kv, kvh = divmod(c, N_KV_HEADS)
        if paged:
            src = src_ref.at[pt_ref[row, p], pl.ds(c * HEAD_DIM, HEAD_DIM), :]
        else:
            src = src_ref.at[row, pl.ds(c * HEAD_DIM, HEAD_DIM), pl.ds(p * PAGE, PAGE)]
        return pltpu.make_async_copy(src, buf.at[slot, kv, :, kvh * npp + p, :], sem.at[slot])

    def tail_copy(c, slot):
        kv, kvh = divmod(c, N_KV_HEADS)
        return pltpu.make_async_copy(tail_ref.at[0, pl.ds(c * HEAD_DIM, HEAD_DIM), :],
                                     buf.at[slot, kv, :, kvh * npp + n_pages, :], sem.at[slot])

    def row_copies(row, slot, fn):
        def body(p, carry):
            for c in range(n_slabs):
                fn(slab_copy(row, p, c, slot))
            return carry
        lax.fori_loop(0, n_pages, body, 0)

    n_real = n_pages + (1 if has_tail else 0)

    @pl.when(b == 0)
    def _():
        if npp > n_real:
            for kvh in range(N_KV_HEADS):
                buf[:, :, :, kvh * npp + n_real:(kvh + 1) * npp, :] = jnp.zeros(
                    (2, 2, HEAD_DIM, npp - n_real, LANES), F32)
        row_copies(0, 0, lambda cp: cp.start())

    slot = b % 2

    @pl.when(b + 1 < nb)
    def _():
        row_copies(b + 1, 1 - slot, lambda cp: cp.start())

    if has_tail:
        for c in range(n_slabs):
            tail_copy(c, slot).start()
        for c in range(n_slabs):
            tail_copy(c, slot).wait()
    row_copies(b, slot, lambda cp: cp.wait())

    rows = N_KV_HEADS * npp

    def body(d, carry):
        acc_k, acc_v = carry
        pos = pos_ref[pl.ds(d, 1), :]
        xk = (buf[slot, 0, d] + pos).astype(BF16)
        xv = (buf[slot, 1, d] + pos).astype(BF16)
        return (acc_k + jnp.dot(xk, bd_ref[0, d], preferred_element_type=F32),
                acc_v + jnp.dot(xv, bd_ref[1, d], preferred_element_type=F32))

    zero = jnp.zeros((rows, LANES), F32)
    acc_k, acc_v = lax.fori_loop(0, HEAD_DIM, body, (zero, zero), unroll=8)
    kc_ref[0] = _half_group_norm(acc_k, g_ref[...])
    vc_ref[0] = acc_v


def _compress(page_table, src, tail, pos_t, bd, g2, *, npp, has_tail, paged):
    b, n_pages = page_table.shape
    kern = functools.partial(_compress_kernel, n_pages=n_pages, npp=npp, has_tail=has_tail, paged=paged)
    rows = N_KV_HEADS * npp
    grid_spec = pltpu.PrefetchScalarGridSpec(
        num_scalar_prefetch=1,
        grid=(b,),
        in_specs=[
            pl.BlockSpec(memory_space=pl.ANY),
            pl.BlockSpec((1, KV_COLS, LANES), lambda i, pt: (i, 0, 0)),
            pl.BlockSpec((HEAD_DIM, LANES), lambda i, pt: (0, 0)),
            pl.BlockSpec((2, HEAD_DIM, LANES, LANES), lambda i, pt: (0, 0, 0, 0)),
            pl.BlockSpec((1, LANES), lambda i, pt: (0, 0)),
        ],
        out_specs=(pl.BlockSpec((1, rows, LANES), lambda i, pt: (i, 0, 0)),
                   pl.BlockSpec((1, rows, LANES), lambda i, pt: (i, 0, 0))),
        scratch_shapes=[
            pltpu.VMEM((2, 2, HEAD_DIM, rows, LANES), F32),
            pltpu.SemaphoreType.DMA((2,)),
        ],
    )
    return pl.pallas_call(
        kern,
        grid_spec=grid_spec,
        out_shape=(jax.ShapeDtypeStruct((b, rows, LANES), F32), jax.ShapeDtypeStruct((b, rows, LANES), F32)),
        compiler_params=pltpu.CompilerParams(
            dimension_semantics=("arbitrary",), vmem_limit_bytes=VMEM_LIMIT),
        name="compress",
    )(page_table, src, tail, pos_t, bd, g2)


def _to_half(tile, src_half, dst_half):
    lane = lax.broadcasted_iota(jnp.int32, tile.shape, 1)
    src = tile if src_half == dst_half else pltpu.roll(tile, HEAD_DIM, axis=1)
    keep = (lane < HEAD_DIM) if dst_half == 0 else (lane >= HEAD_DIM)
    return jnp.where(keep, src, 0.0)


def _pair_tile(o_even, o_odd, half):
    lane = lax.broadcasted_iota(jnp.int32, o_even.shape, 1)
    if half == 0:
        return jnp.where(lane < HEAD_DIM, o_even, pltpu.roll(o_odd, HEAD_DIM, axis=1))
    return jnp.where(lane < HEAD_DIM, pltpu.roll(o_even, HEAD_DIM, axis=1), o_odd)


def _gate_tile(gates, pair, c, shape):
    lane = lax.broadcasted_iota(jnp.int32, shape, 1)
    he, ho = 2 * pair, 2 * pair + 1
    return jnp.where(lane < HEAD_DIM, gates[:, 3 * he + c:3 * he + c + 1], gates[:, 3 * ho + c:3 * ho + c + 1])


def _block_of_col(col, npp):
    return 2 * (col % npp) + col // npp


def _cmp_operand(x, npp):
    return jnp.concatenate([_to_half(x, 0, 0), _to_half(x, 1, 0)], axis=0).astype(BF16)


def _rank_select(score_t, blk_t, cols):
    rank = jnp.zeros(score_t.shape, jnp.int32)
    for r, n in cols:
        row = score_t[r:r + 1, :]
        ahead = (row > score_t) | ((row == score_t) & (blk_t > n))
        rank = rank + ahead.astype(jnp.int32)
    return ((rank < TOP_K_BLOCKS) & (score_t > -0.5)).astype(F32)


POS_HI, POS_LO = HEAD_DIM, HEAD_DIM + 1
MASK_BIG = 1e30


def _attn_prompt_kernel(q_ref, gates_ref, kc_ref, vc_ref, ks_ref, vsx_ref, kw_ref, vwx_ref, exp_ref, gp_ref, a_ref,
                        zbuf, m_sc, acc_sc, *, n_pages, npp):
    i = pl.program_id(1)
    q0 = i * Q_TILE
    rows = Q_PER_KV * Q_TILE
    ncols = 2 * npp
    qpos = q0 + lax.broadcasted_iota(jnp.int32, (Q_TILE, 1), 0)
    blk = _block_of_col(lax.broadcasted_iota(jnp.int32, (Q_TILE, ncols), 1), npp)
    real_cols = [(half * n_pages + p, 2 * p + half) for half in range(2) for p in range(n_pages)]

    g = gates_ref[0]
    g_hi = g.astype(BF16)
    g_lo = (g - g_hi.astype(F32)).astype(BF16)
    gexp = (jnp.dot(g_hi, gp_ref[...], preferred_element_type=F32)
            + jnp.dot(g_lo, gp_ref[...], preferred_element_type=F32))

    r_key = lax.broadcasted_iota(jnp.int32, (Q_TILE, K_CHUNK), 1)
    r_qry = lax.broadcasted_iota(jnp.int32, (Q_TILE, K_CHUNK), 0)
    keep = {"causal": r_key <= r_qry, "lower": r_key > r_qry}

    all_rows = N_HEADS * Q_TILE

    def branch(lhs, k_ref, vx_ref, slc, chunks, c_lo, c_hi):
        m_sc[...] = jnp.full((all_rows, LANES), NEG, F32)

        def scores(c, kind):
            rr = lax.broadcasted_iota(jnp.int32, (HEAD_DIM, K_CHUNK), 0)
            tt = lax.broadcasted_iota(jnp.int32, (HEAD_DIM, K_CHUNK), 1).astype(F32)
            hi = ((c - i) * K_CHUNK).astype(F32)
            pos = jnp.where(rr == 0, hi, jnp.where(rr == 1, tt, 0.0)).astype(BF16)
            for kvh in range(N_KV_HEADS):
                parts = [k_ref[0, c, kvh * HEAD_DIM:(kvh + 1) * HEAD_DIM, :], pos]
                if slc:
                    parts.append(exp_ref[c])
                kx = jnp.concatenate(parts, axis=0)
                for gq in range(Q_PER_KV):
                    sl = slice((kvh * Q_PER_KV + gq) * Q_TILE, (kvh * Q_PER_KV + gq + 1) * Q_TILE)
                    zg = jnp.dot(lhs[sl], kx, preferred_element_type=F32)
                    if kind is not None:
                        zg = jnp.where(keep[kind], zg, NEG)
                    zbuf[c, sl, :] = zg
                    m_sc[sl] = jnp.maximum(m_sc[sl], jnp.maximum(zg[:, :LANES], zg[:, LANES:]))

        for lo, hi_, kind, cond in chunks:
            if kind is None and cond is None:
                def body(c, carry):
                    scores(c, None)
                    return carry
                lax.fori_loop(lo, hi_, body, 0)
            elif cond is None:
                scores(lo, kind)
            else:
                pl.when(cond)(functools.partial(scores, lo, kind))

        m = jnp.broadcast_to(jnp.max(m_sc[...], axis=-1, keepdims=True), (all_rows, LANES))
        acc_sc[...] = jnp.zeros((all_rows, LANES), F32)

        def pass2(c, carry):
            for h in range(N_HEADS):
                sl = slice(h * Q_TILE, (h + 1) * Q_TILE)
                z = zbuf[c, sl, :]
                e = jnp.concatenate([jnp.exp(z[:, :LANES] - m[sl]), jnp.exp(z[:, LANES:] - m[sl])], axis=1)
                acc_sc[sl] += jnp.dot(e.astype(BF16), vx_ref[0, c, h // Q_PER_KV], preferred_element_type=F32)
            return carry

        lax.fori_loop(c_lo, c_hi, pass2, 0)
        acc = acc_sc[...]
        den = pltpu.roll(acc, HEAD_DIM, axis=1)
        return acc / jnp.where(den > 0.0, den, 1.0)

    o_cmp_all, lhs_slc_all, lhs_win_all = [], [], []
    for kvh in range(N_KV_HEADS):
        slopes = [_slope(kvh * Q_PER_KV + g) for g in range(Q_PER_KV)]
        lane = lax.broadcasted_iota(jnp.int32, (Q_TILE, LANES), 1)
        pieces, pieces_x = [], []
        for gq in range(Q_PER_KV):
            h = kvh * Q_PER_KV + gq
            tile = q_ref[0, :, (h // 2) * LANES:(h // 2 + 1) * LANES].astype(F32) * SCALE
            low = _to_half(tile, h % 2, 0)
            pieces.append(low)
            pieces_x.append(jnp.where((lane == POS_HI) | (lane == POS_LO), slopes[gq], low))
        qk = jnp.concatenate(pieces, axis=0).astype(BF16)
        qk_x = jnp.concatenate(pieces_x, axis=0).astype(BF16)

        kc = _cmp_operand(kc_ref[0, kvh], npp)
        vc = _cmp_operand(vc_ref[0, kvh], npp)
        s = _dot_t(qk, kc)
        dist_c = qpos - (blk * CMP_BLOCK + (CMP_BLOCK - 1))
        mask_c = dist_c >= 0
        dist_cf = dist_c.astype(F32)
        p_list = []
        p_kv = jnp.zeros((Q_TILE, ncols), F32)
        for g in range(Q_PER_KV):
            z = jnp.where(mask_c, s[g * Q_TILE:(g + 1) * Q_TILE] - slopes[g] * dist_cf, NEG)
            mx = jnp.max(z, axis=-1, keepdims=True)
            e = jnp.where(mask_c, jnp.exp(z - mx), 0.0)
            den = jnp.sum(e, axis=-1, keepdims=True)
            p = e / jnp.where(den > 0.0, den, 1.0)
            p_kv = p_kv + p
            p_list.append(p.astype(BF16))
        o_cmp = jnp.dot(jnp.concatenate(p_list, axis=0), vc, preferred_element_type=F32)

        cand = blk * CMP_BLOCK <= qpos
        forced = (blk == qpos // CMP_BLOCK) | (blk == 0)
        score = jnp.where(forced, FORCE_SCORE, jnp.where(cand, p_kv, -1.0))
        score_t = score.T
        st = jnp.concatenate([score_t[:n_pages], score_t[npp:npp + n_pages]], axis=0)
        rr = lax.broadcasted_iota(jnp.int32, (2 * n_pages, Q_TILE), 0)
        bt = jnp.where(rr < n_pages, 2 * rr, 2 * (rr - n_pages) + 1)
        sel_s = _rank_select(st, bt, [(r, n) for r, (_, n) in enumerate(real_cols)])
        pad = jnp.zeros((npp - n_pages, Q_TILE), F32)
        sel_t = jnp.concatenate([sel_s[:n_pages], pad, sel_s[n_pages:], pad], axis=0)
        sel = sel_t.T
        sel_bias = jnp.where(sel > 0.5, 0.0, -MASK_BIG).astype(BF16)
        lhs_slc_all.append(jnp.concatenate([qk_x, jnp.concatenate([sel_bias] * Q_PER_KV, axis=0)], axis=1))
        lhs_win_all.append(qk_x)
        o_cmp_all.append(o_cmp)

    o_slc = branch(jnp.concatenate(lhs_slc_all, axis=0), ks_ref, vsx_ref, True,
                   [(0, i, None, None), (i, None, "causal", None)], 0, i + 1)
    o_win = branch(jnp.concatenate(lhs_win_all, axis=0), kw_ref, vwx_ref, False,
                   [(i - 2, None, "lower", i >= 2), (i - 1, None, None, i >= 1), (i, None, "causal", None)],
                   jnp.maximum(i - 2, 0), i + 1)
    o_cmp = jnp.concatenate(o_cmp_all, axis=0)

    for pair in range(N_HEADS // 2):
        e_sl = slice(2 * pair * Q_TILE, (2 * pair + 1) * Q_TILE)
        o_sl = slice((2 * pair + 1) * Q_TILE, (2 * pair + 2) * Q_TILE)
        tile = jnp.zeros((Q_TILE, LANES), F32)
        for br, o in enumerate((o_cmp, o_slc, o_win)):
            col = br * ATTN_WIDTH + pair * LANES
            tile = tile + gexp[:, col:col + LANES] * _pair_tile(o[e_sl], o[o_sl], 0)
        a_ref[0, :, pair * LANES:(pair + 1) * LANES] = tile.astype(BF16)


def _attn_prompt(q, gates, kc, vc, ks16, vsx, kw16, vwx, expand, gate_place, n_pages):
    b, s, _ = q.shape
    npp = kc.shape[2]
    n_chunks = s // K_CHUNK
    rows = Q_PER_KV * Q_TILE
    assert WINDOW == 2 * K_CHUNK and Q_TILE == K_CHUNK
    kern = functools.partial(_attn_prompt_kernel, n_pages=n_pages, npp=npp)
    per_b4 = lambda bi, i: (bi, 0, 0, 0)
    per_b5 = lambda bi, i: (bi, 0, 0, 0, 0)
    return pl.pallas_call(
        kern,
        grid=(b, s // Q_TILE),
        in_specs=[
            pl.BlockSpec((1, Q_TILE, ATTN_WIDTH), lambda bi, i: (bi, i, 0)),
            pl.BlockSpec((1, Q_TILE, GATE_PAD), lambda bi, i: (bi, i, 0)),
            pl.BlockSpec((1, N_KV_HEADS, npp, LANES), per_b4),
            pl.BlockSpec((1, N_KV_HEADS, npp, LANES), per_b4),
            pl.BlockSpec((1, n_chunks, LANES, K_CHUNK), per_b4),
            pl.BlockSpec((1, n_chunks, N_KV_HEADS, K_CHUNK, LANES), per_b5),
            pl.BlockSpec((1, n_chunks, LANES, K_CHUNK), per_b4),
            pl.BlockSpec((1, n_chunks, N_KV_HEADS, K_CHUNK, LANES), per_b5),
            pl.BlockSpec((n_chunks, 2 * npp, K_CHUNK), lambda bi, i: (0, 0, 0)),
            pl.BlockSpec((GATE_PAD, 3 * ATTN_WIDTH), lambda bi, i: (0, 0)),
        ],
        out_specs=pl.BlockSpec((1, Q_TILE, ATTN_WIDTH), lambda bi, i: (bi, i, 0)),
        out_shape=jax.ShapeDtypeStruct((b, s, ATTN_WIDTH), BF16),
        scratch_shapes=[
            pltpu.VMEM((n_chunks, N_KV_HEADS * rows, K_CHUNK), F32),
            pltpu.VMEM((N_KV_HEADS * rows, LANES), F32),
            pltpu.VMEM((N_KV_HEADS * rows, LANES), F32),
        ],
        compiler_params=pltpu.CompilerParams(
            dimension_semantics=("arbitrary", "arbitrary"), vmem_limit_bytes=VMEM_LIMIT),
        name="attn_prompt",
    )(q, gates, kc, vc, ks16, vsx, kw16, vwx, expand, gate_place)


S_ROWS = LANES
S_CHUNK = 1024


def _attn_sample_kernel(pt_ref, q_ref, gates_ref, kc_ref, vc_ref, kvs_new_ref, kvw_new_ref, win_ref, cache_ref,
                        a_ref, kbuf, zbuf, sem, *, n_pages, past, ds, npp):
    b = pl.program_id(0)
    nb = pl.num_programs(0)
    n_keys = n_pages * PAGE
    n_real = N_HEADS * ds
    ncols = 2 * npp
    half_cols = KV_COLS // 2
    n_blocks = (past + ds + CMP_BLOCK - 1) // CMP_BLOCK

    def page_copy(row, p, slot):
        return pltpu.make_async_copy(
            cache_ref.at[pt_ref[row, p]], kbuf.at[slot, :, pl.ds(p * PAGE, PAGE)], sem.at[slot])

    def row_copies(row, slot, fn):
        def body(p, carry):
            fn(page_copy(row, p, slot))
            return carry
        lax.fori_loop(0, n_pages, body, 0)

    @pl.when(b == 0)
    def _():
        row_copies(0, 0, lambda cp: cp.start())

    slot = b % 2

    @pl.when(b + 1 < nb)
    def _():
        row_copies(b + 1, 1 - slot, lambda cp: cp.start())

    by_kvh, low = [], []
    for h in range(N_HEADS):
        tile = q_ref[0, :, (h // 2) * LANES:(h // 2 + 1) * LANES].astype(F32) * SCALE
        by_kvh.append(_to_half(tile, h % 2, h // Q_PER_KV))
        low.append(_to_half(tile, h % 2, 0))
    zpad = jnp.zeros((S_ROWS - n_real, LANES), F32)
    qrows = jnp.concatenate(by_kvh + [zpad], axis=0).astype(BF16)
    qlow = jnp.concatenate(low + [zpad], axis=0).astype(BF16)

    r_col = lax.broadcasted_iota(jnp.int32, (S_ROWS, 1), 0)
    q_of_r = r_col % ds
    h_of_r = r_col // ds
    qp_r = past + q_of_r
    sl_r = jnp.zeros((S_ROWS, 1), F32)
    for h in range(N_HEADS):
        sl_r = jnp.where(h_of_r == h, _slope(h), sl_r)
    kvh_r = h_of_r // Q_PER_KV

    blk = _block_of_col(lax.broadcasted_iota(jnp.int32, (S_ROWS, ncols), 1), npp)
    s = jnp.where(kvh_r == 0, _dot_t(qlow, _cmp_operand(kc_ref[0, 0], npp)),
                  _dot_t(qlow, _cmp_operand(kc_ref[0, 1], npp)))
    dist_c = qp_r - (blk * CMP_BLOCK + (CMP_BLOCK - 1))
    mask_c = dist_c >= 0
    z = jnp.where(mask_c, s - sl_r * dist_c.astype(F32), NEG)
    mx = jnp.max(z, axis=-1, keepdims=True)
    e = jnp.where(mask_c, jnp.exp(z - mx), 0.0)
    den = jnp.sum(e, axis=-1, keepdims=True)
    p = e / jnp.where(den > 0.0, den, 1.0)
    pb = p.astype(BF16)
    o_cmp = jnp.where(kvh_r == 0, jnp.dot(pb, _cmp_operand(vc_ref[0, 0], npp), preferred_element_type=F32),
                      jnp.dot(pb, _cmp_operand(vc_ref[0, 1], npp), preferred_element_type=F32))

    kq = N_KV_HEADS * ds
    p_kv = []
    for kvh in range(N_KV_HEADS):
        acc = jnp.zeros((ds, ncols), F32)
        for g in range(Q_PER_KV):
            r0 = (kvh * Q_PER_KV + g) * ds
            acc = acc + p[r0:r0 + ds]
        p_kv.append(acc)
    p_kv = jnp.concatenate(p_kv, axis=0)
    blk2 = _block_of_col(lax.broadcasted_iota(jnp.int32, (kq, ncols), 1), npp)
    qp2 = past + lax.broadcasted_iota(jnp.int32, (kq, 1), 0) % ds
    cand = blk2 * CMP_BLOCK <= qp2
    forced = (blk2 == qp2 // CMP_BLOCK) | (blk2 == 0)
    score = jnp.where(forced, FORCE_SCORE, jnp.where(cand, p_kv, -1.0))
    rank = jnp.zeros((kq, ncols), jnp.int32)
    for n in range(n_blocks):
        c = (n % 2) * npp + n // 2
        col = score[:, c:c + 1]
        ahead = (col > score) | ((col == score) & (blk2 > n))
        rank = rank + ahead.astype(jnp.int32)
    sel2 = ((rank < TOP_K_BLOCKS) & (score > -0.5)).astype(F32)
    sel_rows = []
    for h in range(N_HEADS):
        kvh = h // Q_PER_KV
        sel_rows.append(sel2[kvh * ds:(kvh + 1) * ds])
    sel_rows.append(jnp.zeros((S_ROWS - n_real, ncols), F32))
    sel_rows = jnp.concatenate(sel_rows, axis=0)
    sel16 = sel_rows.astype(BF16)

    new_lane = lax.broadcasted_iota(jnp.int32, (1, LANES), 1)

    def new_scores(ref, extra_mask):
        k_new = ref[0, :half_cols, :].astype(BF16)
        s_new = jnp.dot(qrows, k_new, preferred_element_type=F32)
        dist = q_of_r - new_lane
        mask = (new_lane < ds) & (dist >= 0) & extra_mask
        return jnp.where(mask, s_new - sl_r * dist.astype(F32), NEG)

    def weighted_new(ref, e_new):
        return _dot_t(e_new.astype(BF16), ref[0, half_cols:, :].astype(BF16))

    row_copies(b, slot, lambda cp: cp.wait())
    n_chunks = n_keys // S_CHUNK

    def pass1(c, mrow):
        k0 = pl.multiple_of(c * S_CHUNK, S_CHUNK)
        kt = kbuf[slot, :half_cols, pl.ds(k0, S_CHUNK)].astype(BF16)
        st = jnp.dot(qrows, kt, preferred_element_type=F32)
        j = _block_of_col(lax.broadcasted_iota(jnp.int32, (ncols, S_CHUNK), 0), npp)
        t = k0 + lax.broadcasted_iota(jnp.int32, (ncols, S_CHUNK), 1)
        expand = (j == t // CMP_BLOCK).astype(BF16)
        chosen = jnp.dot(sel16, expand, preferred_element_type=F32) > 0.5
        tpos = k0 + lax.broadcasted_iota(jnp.int32, (1, S_CHUNK), 1)
        dist = qp_r - tpos
        zc = jnp.where(chosen & (dist >= 0), st - sl_r * dist.astype(F32), NEG)
        zbuf[c] = zc
        return jnp.maximum(mrow, jnp.max(zc, axis=-1, keepdims=True))

    mrow = lax.fori_loop(0, n_chunks, pass1, jnp.full((S_ROWS, 1), NEG, F32))
    c_last = ((past // CMP_BLOCK) % 2) * npp + (past // CMP_BLOCK) // 2
    z_new = new_scores(kvs_new_ref, sel_rows[:, c_last:c_last + 1] > 0.5)
    m_r = jnp.maximum(mrow, jnp.max(z_new, axis=-1, keepdims=True))

    def pass2(c, carry):
        acc, den_r = carry
        k0 = pl.multiple_of(c * S_CHUNK, S_CHUNK)
        zc = zbuf[c]
        ec = jnp.where(zc > 0.5 * NEG, jnp.exp(zc - m_r), 0.0)
        vt = kbuf[slot, half_cols:, pl.ds(k0, S_CHUNK)].astype(BF16)
        return acc + _dot_t(ec.astype(BF16), vt), den_r + jnp.sum(ec, axis=-1, keepdims=True)

    e_new = jnp.where(z_new > 0.5 * NEG, jnp.exp(z_new - m_r), 0.0)
    acc, den_r = lax.fori_loop(0, n_chunks, pass2,
                               (weighted_new(kvs_new_ref, e_new), jnp.sum(e_new, axis=-1, keepdims=True)))
    o_slc = acc / jnp.where(den_r > 0.0, den_r, 1.0)

    win_buf = win_ref.shape[2]
    kw = win_ref[0, :half_cols, :].astype(BF16)
    st = jnp.dot(qrows, kw, preferred_element_type=F32)
    kpos = past - win_buf + lax.broadcasted_iota(jnp.int32, (1, win_buf), 1)
    dist = qp_r - kpos
    mask = (dist >= 0) & (dist < WINDOW) & (kpos >= 0)
    z_w = jnp.where(mask, st - sl_r * dist.astype(F32), NEG)
    zw_new = new_scores(kvw_new_ref, True)
    m_w = jnp.maximum(jnp.max(z_w, axis=-1, keepdims=True), jnp.max(zw_new, axis=-1, keepdims=True))
    e_w = jnp.where(z_w > 0.5 * NEG, jnp.exp(z_w - m_w), 0.0)
    ew_new = jnp.where(zw_new > 0.5 * NEG, jnp.exp(zw_new - m_w), 0.0)
    acc_w = _dot_t(e_w.astype(BF16), win_ref[0, half_cols:, :].astype(BF16)) + weighted_new(kvw_new_ref, ew_new)
    den_w = jnp.sum(e_w, axis=-1, keepdims=True) + jnp.sum(ew_new, axis=-1, keepdims=True)
    o_win = acc_w / jnp.where(den_w > 0.0, den_w, 1.0)

    gates = gates_ref[0]
    for pair in range(N_HEADS // 2):
        kvh = (2 * pair) // Q_PER_KV
        e_sl = slice(2 * pair * ds, (2 * pair + 1) * ds)
        o_sl = slice((2 * pair + 1) * ds, (2 * pair + 2) * ds)
        shape = (ds, LANES)
        tile = (_gate_tile(gates, pair, 0, shape) * _pair_tile(o_cmp[e_sl], o_cmp[o_sl], 0)
                + _gate_tile(gates, pair, 1, shape) * _pair_tile(o_slc[e_sl], o_slc[o_sl], kvh)
                + _gate_tile(gates, pair, 2, shape) * _pair_tile(o_win[e_sl], o_win[o_sl], kvh))
        a_ref[0, :, pair * LANES:(pair + 1) * LANES] = tile.astype(BF16)


def _attn_sample(page_table, q, gates, kc, vc, kvs_new, kvw_new, state_win_t, cache_pages, *, past):
    b, n_pages = page_table.shape
    ds = q.shape[1]
    npp = kc.shape[2]
    win_buf = state_win_t.shape[2]
    n_keys = n_pages * PAGE
    kern = functools.partial(_attn_sample_kernel, n_pages=n_pages, past=past, ds=ds, npp=npp)
    grid_spec = pltpu.PrefetchScalarGridSpec(
        num_scalar_prefetch=1,
        grid=(b,),
        in_specs=[
            pl.BlockSpec((1, ds, ATTN_WIDTH), lambda i, pt: (i, 0, 0)),
            pl.BlockSpec((1, ds, GATE_PAD), lambda i, pt: (i, 0, 0)),
            pl.BlockSpec((1, N_KV_HEADS, npp, LANES), lambda i, pt: (i, 0, 0, 0)),
            pl.BlockSpec((1, N_KV_HEADS, npp, LANES), lambda i, pt: (i, 0, 0, 0)),
            pl.BlockSpec((1, KV_COLS, LANES), lambda i, pt: (i, 0, 0)),
            pl.BlockSpec((1, KV_COLS, LANES), lambda i, pt: (i, 0, 0)),
            pl.BlockSpec((1, KV_COLS, win_buf), lambda i, pt: (i, 0, 0)),
            pl.BlockSpec(memory_space=pl.ANY),
        ],
        out_specs=pl.BlockSpec((1, ds, ATTN_WIDTH), lambda i, pt: (i, 0, 0)),
        scratch_shapes=[
            pltpu.VMEM((2, KV_COLS, n_keys), F32),
            pltpu.VMEM((n_keys // S_CHUNK, S_ROWS, S_CHUNK), F32),
            pltpu.SemaphoreType.DMA((2,)),
        ],
    )
    return pl.pallas_call(
        kern,
        grid_spec=grid_spec,
        out_shape=jax.ShapeDtypeStruct((b, ds, ATTN_WIDTH), BF16),
        compiler_params=pltpu.CompilerParams(
            dimension_semantics=("arbitrary",), vmem_limit_bytes=VMEM_LIMIT),
        name="attn_sample",
    )(page_table, q, gates, kc, vc, kvs_new, kvw_new, state_win_t, cache_pages)


X_TILES = D_MODEL // LANES
PAY_SUB = 2 * X_TILES
GROUP_LANE = 0


def _rows_to_tiles(ref, lo, val):
    for k in range(val.shape[1] // LANES):
        ref[:, lo + k, :] = val[:, k * LANES:(k + 1) * LANES]


def _tiles_to_rows(ref, lo, n):
    return jnp.concatenate([ref[:, lo + k, :] for k in range(n)], axis=1)


def _ffn_kernel(x_ref, a_ref, m_ref, wo_ref, g2_ref, wr_ref, *out_refs, packed):
    h = (x_ref[...]
         + jnp.dot(a_ref[...], wo_ref[:ATTN_WIDTH, :], preferred_element_type=F32)
         + jnp.dot(m_ref[...], wo_ref[ATTN_WIDTH:, :], preferred_element_type=F32))
    ms = jnp.mean(h * h, axis=-1, keepdims=True)
    n2f = (h * lax.rsqrt(ms + EPS)) * g2_ref[...]
    n2 = n2f.astype(BF16)
    logits = jnp.dot(n2, wr_ref[...], preferred_element_type=F32)
    lane = lax.broadcasted_iota(jnp.int32, logits.shape, 1)
    big = jnp.int32(LANES)

    def masked_softmax(mask):
        zz = jnp.where(mask, logits, NEG)
        mx = jnp.max(zz, axis=-1, keepdims=True)
        ee = jnp.where(mask, jnp.exp(zz - mx), 0.0)
        return ee / jnp.sum(ee, axis=-1, keepdims=True)

    def first_argmax(vals, mask):
        v = jnp.max(jnp.where(mask, vals, -1.0), axis=-1, keepdims=True)
        idx = jnp.min(jnp.where(mask & (vals == v), lane, big), axis=-1, keepdims=True)
        return v, idx

    is_g = lane < N_EXPERT_GROUPS
    pg = masked_softmax(is_g)
    g_val, g_idx = first_argmax(pg, is_g)
    e_lane = lane - ROUTER_LANE0
    in_grp = (e_lane >= 0) & (e_lane < N_EXPERTS) & (e_lane // EXPERTS_PER_GROUP == g_idx)
    pe = masked_softmax(in_grp)
    v1, i1 = first_argmax(pe, in_grp)
    rest = in_grp & (lane != i1)
    v2, i2 = first_argmax(pe, rest)
    scale = g_val / (v1 + v2)
    gate = jnp.where(lane == i1, v1 * scale, jnp.where(lane == i2, v2 * scale, 0.0))
    gate = jnp.where(lane == GROUP_LANE, g_idx.astype(F32), gate)
    if packed:
        h_ref, pay_ref, gate_ref = out_refs
        h_ref[...] = h
        gate_ref[...] = gate
        _rows_to_tiles(pay_ref, 0, n2f)
        for k in range(X_TILES, PAY_SUB):
            pay_ref[:, k, :] = gate
    else:
        h_ref, n2_ref, gate_ref = out_refs
        h_ref[...] = h
        n2_ref[...] = n2
        gate_ref[...] = gate


def _ffn(x2d, a2d, m2d, w_out16, g2, w_router16, tm, packed):
    t = x2d.shape[0]
    row = lambda i: (i, 0)
    const = lambda i: (0, 0)
    if packed:
        out_specs = (pl.BlockSpec((tm, D_MODEL), row), pl.BlockSpec((tm, PAY_SUB, LANES), lambda i: (i, 0, 0)),
                     pl.BlockSpec((tm, LANES), row))
        out_shape = (jax.ShapeDtypeStruct((t, D_MODEL), F32), jax.ShapeDtypeStruct((t, PAY_SUB, LANES), F32),
                     jax.ShapeDtypeStruct((t, LANES), F32))
    else:
        out_specs = (pl.BlockSpec((tm, D_MODEL), row), pl.BlockSpec((tm, D_MODEL), row),
                     pl.BlockSpec((tm, LANES), row))
        out_shape = (jax.ShapeDtypeStruct((t, D_MODEL), F32), jax.ShapeDtypeStruct((t, D_MODEL), BF16),
                     jax.ShapeDtypeStruct((t, LANES), F32))
    return pl.pallas_call(
        functools.partial(_ffn_kernel, packed=packed),
        grid=(t // tm,),
        in_specs=[
            pl.BlockSpec((tm, D_MODEL), row),
            pl.BlockSpec((tm, ATTN_WIDTH), row),
            pl.BlockSpec((tm, POOL_WIDTH), row),
            pl.BlockSpec((D_MODEL, D_MODEL), const),
            pl.BlockSpec((1, D_MODEL), const),
            pl.BlockSpec((D_MODEL, LANES), const),
        ],
        out_specs=out_specs,
        out_shape=out_shape,
        compiler_params=pltpu.CompilerParams(
            dimension_semantics=("arbitrary",), vmem_limit_bytes=VMEM_LIMIT),
        name="ffn",
    )(x2d, a2d, m2d, w_out16, g2, w_router16)


MOE_ROWS = 256
ROW_CHUNK = 512
PLAN_TILE = 1024


def _plan_kernel(gate_ref, tri_ref, out_ref, cnt_ref, carry):
    @pl.when(pl.program_id(0) == 0)
    def _():
        carry[...] = jnp.zeros(carry.shape, F32)

    gl = gate_ref[...]
    lane = lax.broadcasted_iota(jnp.int32, gl.shape, 1)
    gid = gl[:, GROUP_LANE:GROUP_LANE + 1]
    onehot = jnp.where((lane < N_EXPERT_GROUPS) & (lane.astype(F32) == gid), 1.0, 0.0)
    before = jnp.dot(tri_ref[...], onehot.astype(BF16), preferred_element_type=F32) + carry[0:1, :]
    rank = jnp.sum(onehot * before, axis=-1, keepdims=True)
    out_ref[...] = jnp.where(lane == 0, rank, jnp.where(lane == 1, gid, 0.0))
    carry[0:1, :] = carry[0:1, :] + jnp.sum(onehot, axis=0, keepdims=True)
    cnt_ref[...] = carry[...]


def _plan(gate):
    t = gate.shape[0]
    tri = jnp.asarray(np.tril(np.ones((PLAN_TILE, PLAN_TILE), np.float32), -1), dtype=BF16)
    return pl.pallas_call(
        _plan_kernel,
        grid=(t // PLAN_TILE,),
        in_specs=[pl.BlockSpec((PLAN_TILE, LANES), lambda i: (i, 0)),
                  pl.BlockSpec((PLAN_TILE, PLAN_TILE), lambda i: (0, 0))],
        out_specs=(pl.BlockSpec((PLAN_TILE, LANES), lambda i: (i, 0)), pl.BlockSpec((8, LANES), lambda i: (0, 0))),
        out_shape=(jax.ShapeDtypeStruct((t, LANES), F32), jax.ShapeDtypeStruct((8, LANES), F32)),
        scratch_shapes=[pltpu.VMEM((8, LANES), F32)],
        compiler_params=pltpu.CompilerParams(
            dimension_semantics=("arbitrary",), vmem_limit_bytes=VMEM_LIMIT),
        name="moe_plan",
    )(gate, tri)


def _scatter_kernel(meta_ref, slot_ref, pay_ref, xs_ref, zrow, sem, zsem):
    c = pl.program_id(0)

    @pl.when(c == 0)
    def _():
        zrow[...] = jnp.zeros(zrow.shape, F32)
        for g in range(N_EXPERT_GROUPS + 1):
            lo = meta_ref[g]
            hi = meta_ref[N_EXPERT_GROUPS + 1 + g]

            def zero_start(r, carry):
                pltpu.make_async_copy(zrow.at[pl.ds(0, 1)], xs_ref.at[pl.ds(r, 1)], zsem.at[0]).start()
                return carry

            def zero_wait(r, carry):
                pltpu.make_async_copy(zrow.at[pl.ds(0, 1)], xs_ref.at[pl.ds(0, 1)], zsem.at[0]).wait()
                return carry

            lax.fori_loop(lo, hi, zero_start, 0)
            lax.fori_loop(lo, hi, zero_wait, 0)

    def issue(u, carry):
        pltpu.make_async_copy(pay_ref.at[pl.ds(u, 1)], xs_ref.at[pl.ds(slot_ref[0, 0, u], 1)], sem.at[0]).start()
        return carry

    lax.fori_loop(0, ROW_CHUNK, issue, 0, unroll=8)
    pltpu.make_async_copy(pay_ref, xs_ref.at[pl.ds(0, ROW_CHUNK)], sem.at[0]).wait()


def _scatter(meta, slot3, pay, n_slots):
    t = pay.shape[0]
    return pl.pallas_call(
        _scatter_kernel,
        grid_spec=pltpu.PrefetchScalarGridSpec(
            num_scalar_prefetch=1,
            grid=(t // ROW_CHUNK,),
            in_specs=[pl.BlockSpec((1, 1, ROW_CHUNK), lambda c, meta: (c, 0, 0), memory_space=pltpu.SMEM),
                      pl.BlockSpec((ROW_CHUNK, PAY_SUB, LANES), lambda c, meta: (c, 0, 0))],
            out_specs=pl.BlockSpec(memory_space=pl.ANY),
            scratch_shapes=[pltpu.VMEM((1, PAY_SUB, LANES), F32), pltpu.SemaphoreType.DMA((1,)),
                            pltpu.SemaphoreType.DMA((1,))],
        ),
        out_shape=jax.ShapeDtypeStruct((n_slots, PAY_SUB, LANES), F32),
        compiler_params=pltpu.CompilerParams(dimension_semantics=("arbitrary",), vmem_limit_bytes=VMEM_LIMIT),
        name="moe_scatter",
    )(meta, slot3, pay)


def _gather_kernel(slot_ref, nslot_ref, h_ref, ys_ref, y_ref, buf, sem):
    c = pl.program_id(0)
    nc = pl.num_programs(0)

    def fetch(ref, slot):
        def issue(u, carry):
            pltpu.make_async_copy(ys_ref.at[pl.ds(ref[0, 0, u], 1)], buf.at[slot, pl.ds(u, 1)], sem.at[slot]).start()
            return carry
        lax.fori_loop(0, ROW_CHUNK, issue, 0, unroll=8)

    @pl.when(c == 0)
    def _():
        fetch(slot_ref, 0)

    @pl.when(c + 1 < nc)
    def _():
        fetch(nslot_ref, (c + 1) % 2)

    cur = c % 2
    pltpu.make_async_copy(ys_ref.at[pl.ds(0, ROW_CHUNK)], buf.at[cur], sem.at[cur]).wait()
    y_ref[...] = h_ref[...] + jnp.concatenate([buf[cur, :, k, :] for k in range(X_TILES)], axis=1)


def _gather(slot3, h, ys):
    t = h.shape[0]
    nc = t // ROW_CHUNK
    return pl.pallas_call(
        _gather_kernel,
        grid=(nc,),
        in_specs=[pl.BlockSpec((1, 1, ROW_CHUNK), lambda c: (c, 0, 0), memory_space=pltpu.SMEM),
                  pl.BlockSpec((1, 1, ROW_CHUNK), lambda c: (jnp.minimum(c + 1, nc - 1), 0, 0),
                               memory_space=pltpu.SMEM),
                  pl.BlockSpec((ROW_CHUNK, D_MODEL), lambda c: (c, 0)),
                  pl.BlockSpec(memory_space=pl.ANY)],
        out_specs=pl.BlockSpec((ROW_CHUNK, D_MODEL), lambda c: (c, 0)),
        out_shape=jax.ShapeDtypeStruct((t, D_MODEL), F32),
        scratch_shapes=[pltpu.VMEM((2, ROW_CHUNK, X_TILES, LANES), F32), pltpu.SemaphoreType.DMA((2,))],
        compiler_params=pltpu.CompilerParams(dimension_semantics=("arbitrary",), vmem_limit_bytes=VMEM_LIMIT),
        name="moe_gather",
    )(slot3, slot3, h, ys)


def _group_moe_kernel(tg_ref, nu_ref, xs_ref, wg_ref, wu_ref, wd_ref, ys_ref):
    j = pl.program_id(0)

    @pl.when(j < nu_ref[0])
    def _():
        g = tg_ref[j]
        x = _tiles_to_rows(xs_ref, 0, X_TILES).astype(BF16)
        gate = xs_ref[:, X_TILES, :]
        lane = lax.broadcasted_iota(jnp.int32, gate.shape, 1)
        y = jnp.zeros((MOE_ROWS, D_MODEL), F32)
        for e in range(EXPERTS_PER_GROUP):
            gu = jnp.dot(x, wg_ref[e], preferred_element_type=F32)
            up = jnp.dot(x, wu_ref[e], preferred_element_type=F32)
            he = (gu * (1.0 / (1.0 + jnp.exp(-gu)))) * up
            out = jnp.dot(he.astype(BF16), wd_ref[e], preferred_element_type=F32)
            col = ROUTER_LANE0 + g * EXPERTS_PER_GROUP + e
            y = y + jnp.sum(jnp.where(lane == col, gate, 0.0), axis=-1, keepdims=True) * out
        _rows_to_tiles(ys_ref, 0, y)

    @pl.when(j >= nu_ref[0])
    def _():
        ys_ref[...] = jnp.zeros(ys_ref.shape, F32)


def _group_moe(tile_group, n_used, xs, wg16, wu16, wd16):
    n_slots = xs.shape[0]
    wspec = lambda shape: pl.BlockSpec((EXPERTS_PER_GROUP,) + shape, lambda j, tg, nu: (tg[j], 0, 0))
    return pl.pallas_call(
        _group_moe_kernel,
        grid_spec=pltpu.PrefetchScalarGridSpec(
            num_scalar_prefetch=2,
            grid=(n_slots // MOE_ROWS,),
            in_specs=[pl.BlockSpec((MOE_ROWS, PAY_SUB, LANES), lambda j, tg, nu: (j, 0, 0)),
                      wspec((D_MODEL, D_EXPERT)), wspec((D_MODEL, D_EXPERT)), wspec((D_EXPERT, D_MODEL))],
            out_specs=pl.BlockSpec((MOE_ROWS, X_TILES, LANES), lambda j, tg, nu: (j, 0, 0)),
        ),
        out_shape=jax.ShapeDtypeStruct((n_slots, X_TILES, LANES), F32),
        compiler_params=pltpu.CompilerParams(dimension_semantics=("arbitrary",), vmem_limit_bytes=VMEM_LIMIT),
        name="moe_group",
    )(tile_group, n_used, xs, wg16, wu16, wd16)


def _routed_moe(h, pay, gate, wg16, wu16, wd16):
    t = pay.shape[0]
    n_slots = t + N_EXPERT_GROUPS * MOE_ROWS
    n_tiles = n_slots // MOE_ROWS
    plan, counts = _plan(gate)
    rank = plan[:, 0].astype(jnp.int32)
    gid = plan[:, 1].astype(jnp.int32)
    cnt = counts[0, :N_EXPERT_GROUPS].astype(jnp.int32)
    padded = -(-cnt // MOE_ROWS) * MOE_ROWS
    ends = jnp.cumsum(padded)
    off = ends - padded
    slot = (off[gid] + rank).reshape(t // ROW_CHUNK, 1, ROW_CHUNK)
    tile_start = jnp.arange(n_tiles, dtype=jnp.int32) * MOE_ROWS
    tile_group = jnp.minimum(jnp.sum(tile_start[:, None] >= ends[None, :], axis=1), N_EXPERT_GROUPS - 1)
    n_used = (ends[-1] // MOE_ROWS).reshape(1)
    meta = jnp.concatenate([off + cnt, ends[-1:], ends, jnp.full((1,), n_slots, jnp.int32)]).astype(jnp.int32)
    xs = _scatter(meta, slot, pay, n_slots)
    ys = _group_moe(tile_group.astype(jnp.int32), n_used.astype(jnp.int32), xs, wg16, wu16, wd16)
    return _gather(slot, h, ys)


def _moe_kernel(n2_ref, gate_ref, h_ref, wg_ref, wu_ref, wd_ref, y_ref):
    e = pl.program_id(1)

    @pl.when(e == 0)
    def _():
        y_ref[...] = h_ref[...]

    n2 = n2_ref[...]
    gu = jnp.dot(n2, wg_ref[0], preferred_element_type=F32)
    up = jnp.dot(n2, wu_ref[0], preferred_element_type=F32)
    he = (gu * (1.0 / (1.0 + jnp.exp(-gu)))) * up
    out = jnp.dot(he.astype(BF16), wd_ref[0], preferred_element_type=F32)
    lane = lax.broadcasted_iota(jnp.int32, gate_ref.shape, 1)
    gcol = jnp.sum(jnp.where(lane == e + ROUTER_LANE0, gate_ref[...], 0.0), axis=-1, keepdims=True)
    y_ref[...] += gcol * out


def _moe(n2, gate, h, wg16, wu16, wd16, tm):
    t = n2.shape[0]
    row = lambda i, e: (i, 0)
    return pl.pallas_call(
        _moe_kernel,
        grid=(t // tm, N_EXPERTS),
        in_specs=[
            pl.BlockSpec((tm, D_MODEL), row),
            pl.BlockSpec((tm, LANES), row),
            pl.BlockSpec((tm, D_MODEL), row),
            pl.BlockSpec((1, D_MODEL, D_EXPERT), lambda i, e: (e, 0, 0)),
            pl.BlockSpec((1, D_MODEL, D_EXPERT), lambda i, e: (e, 0, 0)),
            pl.BlockSpec((1, D_EXPERT, D_MODEL), lambda i, e: (e, 0, 0)),
        ],
        out_specs=pl.BlockSpec((tm, D_MODEL), row),
        out_shape=jax.ShapeDtypeStruct((t, D_MODEL), F32),
        compiler_params=pltpu.CompilerParams(
            dimension_semantics=("arbitrary", "arbitrary"), vmem_limit_bytes=VMEM_LIMIT),
        name="moe",
    )(n2, gate, h, wg16, wu16, wd16)


PROJ_TILE = 512
MOE_TILE = 1024
PROMPT_NPP = LANES // 2
SAMPLE_NPP = LANES


def _round_up(n, mult):
    return -(-n // mult) * mult


def _feature_major(x5):
    b, t = x5.shape[:2]
    return jnp.transpose(x5, (0, 2, 3, 4, 1)).reshape(b, KV_COLS, t)


def _token_major(xt):
    b, _, t = xt.shape
    return jnp.transpose(xt.reshape(b, 2, N_KV_HEADS, HEAD_DIM, t), (0, 4, 1, 2, 3))


def _pad_pages(x, npp_from, npp_to):
    b = x.shape[0]
    x = x.reshape(b, N_KV_HEADS, npp_from, LANES)
    return jnp.concatenate([x, jnp.zeros((b, N_KV_HEADS, npp_to - npp_from, LANES), x.dtype)], axis=2)


def _expand_const(n_chunks, npp):
    col = np.arange(2 * npp)
    blk = 2 * (col % npp) + col // npp
    tok_blk = np.arange(n_chunks * K_CHUNK) // CMP_BLOCK
    e = (blk[None, :, None] == tok_blk.reshape(n_chunks, 1, K_CHUNK)).astype(np.float32)
    return jnp.asarray(e, dtype=BF16)


def _gate_place_const():
    gp = np.zeros((GATE_PAD, 3 * ATTN_WIDTH), np.float32)
    for h in range(N_HEADS):
        for c in range(3):
            gp[3 * h + c, c * ATTN_WIDTH + h * HEAD_DIM:c * ATTN_WIDTH + (h + 1) * HEAD_DIM] = 1.0
    return jnp.asarray(gp, dtype=BF16)


def kernel(x_prompt, x_sample, cache_cmp_kv, cache_slc_kv, state_win_kv, state_pool, page_table, norm1_g, w_in, q_norm_g, k_norm_cmp_g, k_norm_slc_g, k_norm_win_g, cmp_pos_emb, w_cmp_k, w_cmp_v, pool_w, pool_scale, w_out, norm2_g, w_router_group, w_router_expert, w_gate, w_up, w_down):
    b, s, _ = x_prompt.shape
    db, ds, _ = x_sample.shape
    n_pool, page_rows = cache_cmp_kv.shape[:2]
    assert page_rows == PAGE and s % PAGE == 0
    past = page_table.shape[1] * page_rows

    kv0 = ATTN_WIDTH
    kv1 = kv0 + KV_ROWS
    w_row = jnp.concatenate(
        [w_in[:, :kv0], w_in[:, kv1:kv1 + GATE_COLS], jnp.zeros((D_MODEL, GATE_PAD - GATE_COLS), w_in.dtype),
         w_in[:, kv1 + GATE_COLS:]], axis=1).astype(BF16)
    w_kv = w_in[:, kv0:kv1].T.astype(BF16)
    g1 = norm1_g[None, :]
    g2 = norm2_g[None, :]
    two = lambda g: jnp.tile(g, 2)[None, :]
    kg = jnp.stack([k_norm_slc_g, k_norm_win_g])[:, :, None]
    pos_t = jnp.tile(cmp_pos_emb.T, (1, 2))
    zb = jnp.zeros((HEAD_DIM, CMP_BLOCK, HEAD_DIM), F32)

    def blockdiag(w):
        wt = jnp.transpose(w, (1, 0, 2))
        return jnp.concatenate([jnp.concatenate([wt, zb], axis=2), jnp.concatenate([zb, wt], axis=2)], axis=1)

    bd = jnp.stack([blockdiag(w_cmp_k), blockdiag(w_cmp_v)]).astype(BF16)
    w_router = jnp.concatenate(
        [w_router_group, w_router_expert,
         jnp.zeros((D_MODEL, LANES - N_EXPERT_GROUPS - N_EXPERTS), w_router_group.dtype)], axis=1).astype(BF16)
    w_out16 = w_out.astype(BF16)
    pool_w16 = pool_w.astype(BF16)
    wg16, wu16, wd16 = w_gate.astype(BF16), w_up.astype(BF16), w_down.astype(BF16)
    ps = pool_scale[None, :]

    def ffn_moe(x2d, a2d, m2d, tm_ffn, tm_moe, routed):
        if routed:
            return _routed_moe(*_ffn(x2d, a2d, m2d, w_out16, g2, w_router, tm_ffn, True), wg16, wu16, wd16)
        h, n2, gate = _ffn(x2d, a2d, m2d, w_out16, g2, w_router, tm_ffn, False)
        return _moe(n2, gate, h, wg16, wu16, wd16, tm_moe)

    q, gates, u, kvc_t, kvs_t, kvw_t, ks16, vsx, kw16, vwx = _proj(
        x_prompt, g1, w_row, w_kv, two(q_norm_g), kg, PROJ_TILE)
    m_p = _pool(u, pool_w16, ps)
    pages_p = s // PAGE
    pt_p = jnp.zeros((b, pages_p), jnp.int32)
    dummy_tail = jnp.zeros((b, KV_COLS, LANES), F32)
    kc_p, vc_p = _compress(pt_p, kvc_t, dummy_tail, pos_t, bd, two(k_norm_cmp_g),
                           npp=pages_p, has_tail=False, paged=False)
    kc_p = _pad_pages(kc_p, pages_p, PROMPT_NPP)
    vc_p = _pad_pages(vc_p, pages_p, PROMPT_NPP)
    a_p = _attn_prompt(q, gates, kc_p, vc_p, ks16, vsx, kw16, vwx, _expand_const(s // K_CHUNK, PROMPT_NPP),
                       _gate_place_const(), pages_p)
    y_p = ffn_moe(x_prompt.reshape(b * s, D_MODEL), a_p.reshape(b * s, ATTN_WIDTH),
                  m_p.reshape(b * s, POOL_WIDTH), PROJ_TILE, MOE_TILE, True)

    ts = db * ds
    q_s, gates_s, u_s, kvc_st, kvs_st, kvw_st, _, _, _, _ = _proj(
        x_sample.reshape(1, ts, D_MODEL), g1, w_row, w_kv, two(q_norm_g), kg, ts)
    u_s = u_s.reshape(db, ds, POOL_WIDTH)
    pool_ext = jnp.concatenate([state_pool, u_s], axis=1)
    n_ext = pool_ext.shape[1]
    lead = _round_up(n_ext, 8) - n_ext
    pool_in = jnp.concatenate([jnp.zeros((db, lead, POOL_WIDTH), F32), pool_ext], axis=1)
    m_s = _pool(pool_in, pool_w16, ps)[:, lead + n_ext - ds:]

    def new_rows_t(xt):
        x = jnp.transpose(xt.reshape(KV_COLS, db, ds), (1, 0, 2))
        return jnp.concatenate([x, jnp.zeros((db, KV_COLS, LANES - ds), F32)], axis=2)

    kvc_new, kvs_new, kvw_new = new_rows_t(kvc_st), new_rows_t(kvs_st), new_rows_t(kvw_st)
    n_pages = page_table.shape[1]
    npp_c = _round_up(n_pages + 1, 8)
    cmp_pages = _feature_major(cache_cmp_kv)
    slc_pages = _feature_major(cache_slc_kv)
    kc_s, vc_s = _compress(page_table, cmp_pages, kvc_new, pos_t, bd, two(k_norm_cmp_g),
                           npp=npp_c, has_tail=True, paged=True)
    kc_s = _pad_pages(kc_s, npp_c, SAMPLE_NPP)
    vc_s = _pad_pages(vc_s, npp_c, SAMPLE_NPP)
    win_t = _feature_major(state_win_kv)
    a_s = _attn_sample(page_table, q_s.reshape(db, ds, ATTN_WIDTH), gates_s.reshape(db, ds, GATE_PAD),
                       kc_s, vc_s, kvs_new, kvw_new, win_t, slc_pages, past=past)
    y_s = ffn_moe(x_sample.reshape(ts, D_MODEL), a_s.reshape(ts, ATTN_WIDTH), m_s.reshape(ts, POOL_WIDTH), ts, ts, False)

    win_keep = min(WINDOW, s)
    sample5 = lambda xt: jnp.transpose(xt.reshape(2, N_KV_HEADS, HEAD_DIM, db, ds), (3, 4, 0, 1, 2))
    win_ctx_t = jnp.concatenate([win_t, kvw_new[:, :, :ds]], axis=2)
    return (y_p.reshape(b, s, D_MODEL), y_s.reshape(db, ds, D_MODEL),
            _token_major(kvc_t), sample5(kvc_st),
            _token_major(kvs_t), sample5(kvs_st),
            _token_major(kvw_t[:, :, s - win_keep:]), _token_major(win_ctx_t[:, :, ds:]),
            u[:, s - POOL_BUF:], pool_ext[:, ds:])
```

```python
import functools

import jax
import jax.numpy as jnp
import numpy as np
from jax import lax
from jax.experimental import pallas as pl
from jax.experimental.pallas import tpu as pltpu

F32 = jnp.float32
BF16 = jnp.bfloat16

D_MODEL = 1024
N_HEADS = 8
HEAD_DIM = 64
N_KV_HEADS = 2
Q_PER_KV = N_HEADS // N_KV_HEADS
ATTN_WIDTH = N_HEADS * HEAD_DIM
KV_COLS = 2 * N_KV_HEADS * HEAD_DIM
GATE_COLS = 3 * N_HEADS
POOL_WIDTH = D_MODEL - ATTN_WIDTH
POOL_WINDOWS = (2, 4, 8, 16)
POOL_GROUP_WIDTH = POOL_WIDTH // len(POOL_WINDOWS)
POOL_BUF = max(POOL_WINDOWS) - 1
CMP_BLOCK = 64
TOP_K_BLOCKS = 16
WINDOW = 512
FORCE_SCORE = 1.0e4
N_EXPERT_GROUPS = 4
EXPERTS_PER_GROUP = 4
N_EXPERTS = N_EXPERT_GROUPS * EXPERTS_PER_GROUP
D_EXPERT = 512
EPS = 1e-6
NEG = -1e30
SCALE = HEAD_DIM ** -0.5

LANES = 128
PAGE = 2 * CMP_BLOCK
GATE_PAD = LANES
ROW_COLS = ATTN_WIDTH + GATE_PAD + POOL_WIDTH
OFF_GATE = ATTN_WIDTH
OFF_U = OFF_GATE + GATE_PAD
KV_ROWS = 3 * KV_COLS
ROUTER_LANE0 = N_EXPERT_GROUPS
Q_TILE = 256
K_CHUNK = 256
VMEM_LIMIT = 56 * 1024 * 1024


def _slope(h):
    return float(2.0 ** (-8.0 * (h + 1) / N_HEADS))


def _half_group_norm(v, g):
    lane = lax.broadcasted_iota(jnp.int32, v.shape, 1)
    lo = lane < HEAD_DIM
    v2 = v * v
    s_lo = jnp.sum(jnp.where(lo, v2, 0.0), axis=-1, keepdims=True)
    s_hi = jnp.sum(jnp.where(lo, 0.0, v2), axis=-1, keepdims=True)
    r_lo = lax.rsqrt(s_lo * (1.0 / HEAD_DIM) + EPS)
    r_hi = lax.rsqrt(s_hi * (1.0 / HEAD_DIM) + EPS)
    return (v * jnp.where(lo, r_lo, r_hi)) * g


def _dot_t(a, b):
    return lax.dot_general(a, b, (((1,), (1,)), ((), ())), preferred_element_type=F32)


def _proj_kernel(x_ref, g1_ref, wr_ref, wkv_ref, qg_ref, kg_ref,
                 q_ref, gates_ref, u_ref, kvc_ref, kvs_ref, kvw_ref, ks16_ref, vsx_ref, kw16_ref, vwx_ref):
    x = x_ref[0]
    tm = x.shape[0]
    ms = jnp.mean(x * x, axis=-1, keepdims=True)
    n = ((x * lax.rsqrt(ms + EPS)) * g1_ref[...]).astype(BF16)
    p = jnp.dot(n, wr_ref[...], preferred_element_type=F32)
    pt = _dot_t(wkv_ref[...], n)
    for t in range(ATTN_WIDTH // LANES):
        sl = slice(t * LANES, (t + 1) * LANES)
        q_ref[0, :, sl] = _half_group_norm(p[:, sl], qg_ref[...]).astype(BF16)
    gates_ref[0] = 1.0 / (1.0 + jnp.exp(-p[:, OFF_GATE:OFF_GATE + GATE_PAD]))
    u_ref[0] = p[:, OFF_U:OFF_U + POOL_WIDTH]
    kvc_ref[0] = pt[:KV_COLS]
    half = KV_COLS // 2
    lane = lax.broadcasted_iota(jnp.int32, (tm, LANES), 1)
    for bi, (out32, k16, vx16) in enumerate(((kvs_ref, ks16_ref, vsx_ref), (kvw_ref, kw16_ref, vwx_ref))):
        off = (bi + 1) * KV_COLS
        heads = []
        for hh in range(N_KV_HEADS):
            kh = pt[off + hh * HEAD_DIM:off + (hh + 1) * HEAD_DIM]
            msk = jnp.mean(kh * kh, axis=0, keepdims=True)
            heads.append((kh * lax.rsqrt(msk + EPS)) * kg_ref[bi])
        kn = jnp.concatenate(heads, axis=0)
        v = pt[off + half:off + KV_COLS]
        out32[0, :half, :] = kn
        out32[0, half:, :] = v
        vt = v.T
        vx = (jnp.where(lane < HEAD_DIM, vt, 1.0), jnp.where(lane < HEAD_DIM, pltpu.roll(vt, HEAD_DIM, axis=1), 1.0))
        for c in range(tm // K_CHUNK):
            cs = slice(c * K_CHUNK, (c + 1) * K_CHUNK)
            k16[0, c] = kn[:, cs].astype(BF16)
            for hh in range(N_KV_HEADS):
                vx16[0, c, hh] = vx[hh][cs].astype(BF16)


def _proj(x3d, g1, w_row, w_kv, qg, kg, tm):
    b, s, _ = x3d.shape
    tok = lambda i, j: (i, j, 0)
    feat = lambda i, j: (i, 0, j)
    const2 = lambda i, j: (0, 0)
    nck = tm // K_CHUNK
    out_shape = (
        jax.ShapeDtypeStruct((b, s, ATTN_WIDTH), BF16),
        jax.ShapeDtypeStruct((b, s, GATE_PAD), F32),
        jax.ShapeDtypeStruct((b, s, POOL_WIDTH), F32),
        jax.ShapeDtypeStruct((b, KV_COLS, s), F32),
        jax.ShapeDtypeStruct((b, KV_COLS, s), F32),
        jax.ShapeDtypeStruct((b, KV_COLS, s), F32),
        jax.ShapeDtypeStruct((b, s // K_CHUNK, LANES, K_CHUNK), BF16),
        jax.ShapeDtypeStruct((b, s // K_CHUNK, N_KV_HEADS, K_CHUNK, LANES), BF16),
        jax.ShapeDtypeStruct((b, s // K_CHUNK, LANES, K_CHUNK), BF16),
        jax.ShapeDtypeStruct((b, s // K_CHUNK, N_KV_HEADS, K_CHUNK, LANES), BF16),
    )
    k_spec = pl.BlockSpec((1, nck, LANES, K_CHUNK), lambda i, j: (i, j, 0, 0))
    v_spec = pl.BlockSpec((1, nck, N_KV_HEADS, K_CHUNK, LANES), lambda i, j: (i, j, 0, 0, 0))
    out_specs = (
        pl.BlockSpec((1, tm, ATTN_WIDTH), tok),
        pl.BlockSpec((1, tm, GATE_PAD), tok),
        pl.BlockSpec((1, tm, POOL_WIDTH), tok),
        pl.BlockSpec((1, KV_COLS, tm), feat),
        pl.BlockSpec((1, KV_COLS, tm), feat),
        pl.BlockSpec((1, KV_COLS, tm), feat),
        k_spec, v_spec, k_spec, v_spec,
    )
    return pl.pallas_call(
        _proj_kernel,
        grid=(b, s // tm),
        in_specs=[
            pl.BlockSpec((1, tm, D_MODEL), tok),
            pl.BlockSpec((1, D_MODEL), const2),
            pl.BlockSpec((D_MODEL, ROW_COLS), const2),
            pl.BlockSpec((KV_ROWS, D_MODEL), const2),
            pl.BlockSpec((1, LANES), const2),
            pl.BlockSpec((2, HEAD_DIM, 1), lambda i, j: (0, 0, 0)),
        ],
        out_specs=out_specs,
        out_shape=out_shape,
        compiler_params=pltpu.CompilerParams(
            dimension_semantics=("arbitrary", "arbitrary"), vmem_limit_bytes=VMEM_LIMIT),
        name="proj",
    )(x3d, g1, w_row, w_kv, qg, kg)


def _pool_kernel(u_ref, pw_ref, ps_ref, m_ref):
    u = u_ref[0]
    n = u.shape[0]

    def shift(v, k):
        rolled = pltpu.roll(v, k, axis=0)
        r = lax.broadcasted_iota(jnp.int32, v.shape, 0)
        return jnp.where(r >= k, rolled, 0.0)

    sums = []
    s = u
    k = 1
    for gi in range(len(POOL_WINDOWS)):
        s = s + shift(s, k)
        k *= 2
        sums.append(s[:, :POOL_GROUP_WIDTH])
        s = s[:, POOL_GROUP_WIDTH:]
    row = lax.broadcasted_iota(jnp.int32, (n, 1), 0)
    for gi, w in enumerate(POOL_WINDOWS):
        sl = slice(gi * POOL_GROUP_WIDTH, (gi + 1) * POOL_GROUP_WIDTH)
        cnt = jnp.minimum(row + 1, w).astype(F32)
        d = sums[gi] / cnt - u[:, sl]
        y = jnp.dot(d.astype(BF16), pw_ref[gi], preferred_element_type=F32)
        m_ref[0, :, sl] = (y * ps_ref[:, sl]).astype(BF16)


def _pool(u3d, pool_w16, pool_scale):
    b, n, _ = u3d.shape
    return pl.pallas_call(
        _pool_kernel,
        grid=(b,),
        in_specs=[
            pl.BlockSpec((1, n, POOL_WIDTH), lambda i: (i, 0, 0)),
            pl.BlockSpec((len(POOL_WINDOWS), POOL_GROUP_WIDTH, POOL_GROUP_WIDTH), lambda i: (0, 0, 0)),
            pl.BlockSpec((1, POOL_WIDTH), lambda i: (0, 0)),
        ],
        out_specs=pl.BlockSpec((1, n, POOL_WIDTH), lambda i: (i, 0, 0)),
        out_shape=jax.ShapeDtypeStruct((b, n, POOL_WIDTH), BF16),
        compiler_params=pltpu.CompilerParams(
            dimension_semantics=("arbitrary",), vmem_limit_bytes=VMEM_LIMIT),
        name="pool",
    )(u3d, pool_w16, pool_scale)


def _compress_kernel(pt_ref, src_ref, tail_ref, pos_ref, bd_ref, g_ref, kc_ref, vc_ref, buf, sem,
                     *, n_pages, npp, has_tail, paged):
    b = pl.program_id(0)
    nb = pl.num_programs(0)
    n_slabs = 2 * N_KV_HEADS

    def slab_copy(row, p, c, slot):
        kv, kvh = divmod(c, N_KV_HEADS)
        if paged:
            src = src_ref.at[pt_ref[row, p], pl.ds(c * HEAD_DIM, HEAD_DIM), :]
        else:
            src = src_ref.at[row, pl.ds(c * HEAD_DIM, HEAD_DIM), pl.ds(p * PAGE, PAGE)]
        return pltpu.make_async_copy(src, buf.at[slot, kv, :, kvh * npp + p, :], sem.at[slot])

    def tail_copy(c, slot):
        kv, kvh = divmod(c, N_KV_HEADS)
        return pltpu.make_async_copy(tail_ref.at[0, pl.ds(c * HEAD_DIM, HEAD_DIM), :],
                                     buf.at[slot, kv, :, kvh * npp + n_pages, :], sem.at[slot])

    def row_copies(row, slot, fn):
        def body(p, carry):
            for c in range(n_slabs):
                fn(slab_copy(row, p, c, slot))
            return carry
        lax.fori_loop(0, n_pages, body, 0)

    n_real = n_pages + (1 if has_tail else 0)

    @pl.when(b == 0)
    def _():
        if npp > n_real:
            for kvh in range(N_KV_HEADS):
                buf[:, :, :, kvh * npp + n_real:(kvh + 1) * npp, :] = jnp.zeros(
                    (2, 2, HEAD_DIM, npp - n_real, LANES), F32)
        row_copies(0, 0, lambda cp: cp.start())

    slot = b % 2

    @pl.when(b + 1 < nb)
    def _():
        row_copies(b + 1, 1 - slot, lambda cp: cp.start())

    if has_tail:
        for c in range(n_slabs):
            tail_copy(c, slot).start()
        for c in range(n_slabs):
            tail_copy(c, slot).wait()
    row_copies(b, slot, lambda cp: cp.wait())

    rows = N_KV_HEADS * npp

    def body(d, acc):
        pos = pos_ref[pl.ds(d, 1), :]
        x = jnp.concatenate([buf[slot, 0, d] + pos, buf[slot, 1, d] + pos], axis=1).astype(BF16)
        return acc + jnp.dot(x, bd_ref[d], preferred_element_type=F32)

    acc = lax.fori_loop(0, HEAD_DIM, body, jnp.zeros((rows, 2 * LANES), F32), unroll=8)
    kc_ref[0] = _half_group_norm(acc[:, :LANES], g_ref[...])
    vc_ref[0] = acc[:, LANES:]


def _compress(page_table, src, tail, pos_t, bd, g2, *, npp, has_tail, paged):
    b, n_pages = page_table.shape
    kern = functools.partial(_compress_kernel, n_pages=n_pages, npp=npp, has_tail=has_tail, paged=paged)
    rows = N_KV_HEADS * npp
    grid_spec = pltpu.PrefetchScalarGridSpec(
        num_scalar_prefetch=1,
        grid=(b,),
        in_specs=[
            pl.BlockSpec(memory_space=pl.ANY),
            pl.BlockSpec((1, KV_COLS, LANES), lambda i, pt: (i, 0, 0)),
            pl.BlockSpec((HEAD_DIM, LANES), lambda i, pt: (0, 0)),
            pl.BlockSpec((HEAD_DIM, 2 * LANES, 2 * LANES), lambda i, pt: (0, 0, 0)),
            pl.BlockSpec((1, LANES), lambda i, pt: (0, 0)),
        ],
        out_specs=(pl.BlockSpec((1, rows, LANES), lambda i, pt: (i, 0, 0)),
                   pl.BlockSpec((1, rows, LANES), lambda i, pt: (i, 0, 0))),
        scratch_shapes=[
            pltpu.VMEM((2, 2, HEAD_DIM, rows, LANES), F32),
            pltpu.SemaphoreType.DMA((2,)),
        ],
    )
    return pl.pallas_call(
        kern,
        grid_spec=grid_spec,
        out_shape=(jax.ShapeDtypeStruct((b, rows, LANES), F32), jax.ShapeDtypeStruct((b, rows, LANES), F32)),
        compiler_params=pltpu.CompilerParams(
            dimension_semantics=("arbitrary",), vmem_limit_bytes=VMEM_LIMIT),
        name="compress",
    )(page_table, src, tail, pos_t, bd, g2)


def _to_half(tile, src_half, dst_half):
    lane = lax.broadcasted_iota(jnp.int32, tile.shape, 1)
    src = tile if src_half == dst_half else pltpu.roll(tile, HEAD_DIM, axis=1)
    keep = (lane < HEAD_DIM) if dst_half == 0 else (lane >= HEAD_DIM)
    return jnp.where(keep, src, 0.0)


def _pair_tile(o_even, o_odd, half):
    lane = lax.broadcasted_iota(jnp.int32, o_even.shape, 1)
    if half == 0:
        return jnp.where(lane < HEAD_DIM, o_even, pltpu.roll(o_odd, HEAD_DIM, axis=1))
    return jnp.where(lane < HEAD_DIM, pltpu.roll(o_even, HEAD_DIM, axis=1), o_odd)


def _gate_tile(gates, pair, c, shape):
    lane = lax.broadcasted_iota(jnp.int32, shape, 1)
    he, ho = 2 * pair, 2 * pair + 1
    return jnp.where(lane < HEAD_DIM, gates[:, 3 * he + c:3 * he + c + 1], gates[:, 3 * ho + c:3 * ho + c + 1])


def _block_of_col(col, npp):
    return 2 * (col % npp) + col // npp


def _cmp_operand(x, npp):
    return jnp.concatenate([_to_half(x, 0, 0), _to_half(x, 1, 0)], axis=0).astype(BF16)


def _rank_select(score_t, blk_t, cols):
    rank = jnp.zeros(score_t.shape, jnp.int32)
    for r, n in cols:
        row = score_t[r:r + 1, :]
        ahead = (row > score_t) | ((row == score_t) & (blk_t > n))
        rank = rank + ahead.astype(jnp.int32)
    return ((rank < TOP_K_BLOCKS) & (score_t > -0.5)).astype(F32)


POS_HI, POS_LO = HEAD_DIM, HEAD_DIM + 1
MASK_BIG = 1e30


def _attn_prompt_kernel(q_ref, gates_ref, kc_ref, vc_ref, ks_ref, vsx_ref, kw_ref, vwx_ref, exp_ref, gp_ref, a_ref,
                        zbuf, m_sc, acc_sc, *, n_pages, npp):
    i = pl.program_id(1)
    q0 = i * Q_TILE
    rows = Q_PER_KV * Q_TILE
    ncols = 2 * npp
    qpos = q0 + lax.broadcasted_iota(jnp.int32, (Q_TILE, 1), 0)
    blk = _block_of_col(lax.broadcasted_iota(jnp.int32, (Q_TILE, ncols), 1), npp)
    real_cols = [(half * n_pages + p, 2 * p + half) for half in range(2) for p in range(n_pages)]

    g = gates_ref[0]
    g_hi = g.astype(BF16)
    g_lo = (g - g_hi.astype(F32)).astype(BF16)
    gexp = (jnp.dot(g_hi, gp_ref[...], preferred_element_type=F32)
            + jnp.dot(g_lo, gp_ref[...], preferred_element_type=F32))

    r_key = lax.broadcasted_iota(jnp.int32, (Q_TILE, K_CHUNK), 1)
    r_qry = lax.broadcasted_iota(jnp.int32, (Q_TILE, K_CHUNK), 0)
    keep = {"causal": r_key <= r_qry, "lower": r_key > r_qry}

    all_rows = N_HEADS * Q_TILE

    def branch(lhs, k_ref, vx_ref, slc, chunks, c_lo, c_hi):
        m_sc[...] = jnp.full((all_rows, LANES), NEG, F32)

        def scores(c, kind):
            rr = lax.broadcasted_iota(jnp.int32, (HEAD_DIM, K_CHUNK), 0)
            tt = lax.broadcasted_iota(jnp.int32, (HEAD_DIM, K_CHUNK), 1).astype(F32)
            hi = ((c - i) * K_CHUNK).astype(F32)
            pos = jnp.where(rr == 0, hi, jnp.where(rr == 1, tt, 0.0)).astype(BF16)
            for kvh in range(N_KV_HEADS):
                parts = [k_ref[0, c, kvh * HEAD_DIM:(kvh + 1) * HEAD_DIM, :], pos]
                if slc:
                    parts.append(exp_ref[c])
                kx = jnp.concatenate(parts, axis=0)
                for gq in range(Q_PER_KV):
                    sl = slice((kvh * Q_PER_KV + gq) * Q_TILE, (kvh * Q_PER_KV + gq + 1) * Q_TILE)
                    zg = jnp.dot(lhs[sl], kx, preferred_element_type=F32)
                    if kind is not None:
                        zg = jnp.where(keep[kind], zg, NEG)
                    zbuf[c, sl, :] = zg
                    m_sc[sl] = jnp.maximum(m_sc[sl], jnp.maximum(zg[:, :LANES], zg[:, LANES:]))

        for lo, hi_, kind, cond in chunks:
            if kind is None and cond is None:
                def body(c, carry):
                    scores(c, None)
                    return carry
                lax.fori_loop(lo, hi_, body, 0)
            elif cond is None:
                scores(lo, kind)
            else:
                pl.when(cond)(functools.partial(scores, lo, kind))

        m = jnp.broadcast_to(jnp.max(m_sc[...], axis=-1, keepdims=True), (all_rows, LANES))
        acc_sc[...] = jnp.zeros((all_rows, LANES), F32)

        def pass2(c, carry):
            for h in range(N_HEADS):
                sl = slice(h * Q_TILE, (h + 1) * Q_TILE)
                z = zbuf[c, sl, :]
                e = jnp.concatenate([jnp.exp(z[:, :LANES] - m[sl]), jnp.exp(z[:, LANES:] - m[sl])], axis=1)
                acc_sc[sl] += jnp.dot(e.astype(BF16), vx_ref[0, c, h // Q_PER_KV], preferred_element_type=F32)
            return carry

        lax.fori_loop(c_lo, c_hi, pass2, 0)
        acc = acc_sc[...]
        den = pltpu.roll(acc, HEAD_DIM, axis=1)
        return acc / jnp.where(den > 0.0, den, 1.0)

    o_cmp_all, lhs_slc_all, lhs_win_all = [], [], []
    for kvh in range(N_KV_HEADS):
        slopes = [_slope(kvh * Q_PER_KV + g) for g in range(Q_PER_KV)]
        lane = lax.broadcasted_iota(jnp.int32, (Q_TILE, LANES), 1)
        pieces, pieces_x = [], []
        for gq in range(Q_PER_KV):
            h = kvh * Q_PER_KV + gq
            tile = q_ref[0, :, (h // 2) * LANES:(h // 2 + 1) * LANES].astype(F32) * SCALE
            low = _to_half(tile, h % 2, 0)
            pieces.append(low)
            pieces_x.append(jnp.where((lane == POS_HI) | (lane == POS_LO), slopes[gq], low))
        qk = jnp.concatenate(pieces, axis=0).astype(BF16)
        qk_x = jnp.concatenate(pieces_x, axis=0).astype(BF16)

        kc = _cmp_operand(kc_ref[0, kvh], npp)
        vc = _cmp_operand(vc_ref[0, kvh], npp)
        s = _dot_t(qk, kc)
        dist_c = qpos - (blk * CMP_BLOCK + (CMP_BLOCK - 1))
        mask_c = dist_c >= 0
        dist_cf = dist_c.astype(F32)
        p_list = []
        p_kv = jnp.zeros((Q_TILE, ncols), F32)
        for g in range(Q_PER_KV):
            z = jnp.where(mask_c, s[g * Q_TILE:(g + 1) * Q_TILE] - slopes[g] * dist_cf, NEG)
            mx = jnp.max(z, axis=-1, keepdims=True)
            e = jnp.where(mask_c, jnp.exp(z - mx), 0.0)
            den = jnp.sum(e, axis=-1, keepdims=True)
            p = e / jnp.where(den > 0.0, den, 1.0)
            p_kv = p_kv + p
            p_list.append(p.astype(BF16))
        o_cmp = jnp.dot(jnp.concatenate(p_list, axis=0), vc, preferred_element_type=F32)

        cand = blk * CMP_BLOCK <= qpos
        forced = (blk == qpos // CMP_BLOCK) | (blk == 0)
        score = jnp.where(forced, FORCE_SCORE, jnp.where(cand, p_kv, -1.0))
        score_t = score.T
        st = jnp.concatenate([score_t[:n_pages], score_t[npp:npp + n_pages]], axis=0)
        rr = lax.broadcasted_iota(jnp.int32, (2 * n_pages, Q_TILE), 0)
        bt = jnp.where(rr < n_pages, 2 * rr, 2 * (rr - n_pages) + 1)
        sel_s = _rank_select(st, bt, [(r, n) for r, (_, n) in enumerate(real_cols)])
        pad = jnp.zeros((npp - n_pages, Q_TILE), F32)
        sel_t = jnp.concatenate([sel_s[:n_pages], pad, sel_s[n_pages:], pad], axis=0)
        sel = sel_t.T
        sel_bias = jnp.where(sel > 0.5, 0.0, -MASK_BIG).astype(BF16)
        lhs_slc_all.append(jnp.concatenate([qk_x, jnp.concatenate([sel_bias] * Q_PER_KV, axis=0)], axis=1))
        lhs_win_all.append(qk_x)
        o_cmp_all.append(o_cmp)

    o_slc = branch(jnp.concatenate(lhs_slc_all, axis=0), ks_ref, vsx_ref, True,
                   [(0, i, None, None), (i, None, "causal", None)], 0, i + 1)
    o_win = branch(jnp.concatenate(lhs_win_all, axis=0), kw_ref, vwx_ref, False,
                   [(i - 2, None, "lower", i >= 2), (i - 1, None, None, i >= 1), (i, None, "causal", None)],
                   jnp.maximum(i - 2, 0), i + 1)
    o_cmp = jnp.concatenate(o_cmp_all, axis=0)

    for pair in range(N_HEADS // 2):
        e_sl = slice(2 * pair * Q_TILE, (2 * pair + 1) * Q_TILE)
        o_sl = slice((2 * pair + 1) * Q_TILE, (2 * pair + 2) * Q_TILE)
        tile = jnp.zeros((Q_TILE, LANES), F32)
        for br, o in enumerate((o_cmp, o_slc, o_win)):
            col = br * ATTN_WIDTH + pair * LANES
            tile = tile + gexp[:, col:col + LANES] * _pair_tile(o[e_sl], o[o_sl], 0)
        a_ref[0, :, pair * LANES:(pair + 1) * LANES] = tile.astype(BF16)


def _attn_prompt(q, gates, kc, vc, ks16, vsx, kw16, vwx, expand, gate_place, n_pages):
    b, s, _ = q.shape
    npp = kc.shape[2]
    n_chunks = s // K_CHUNK
    rows = Q_PER_KV * Q_TILE
    assert WINDOW == 2 * K_CHUNK and Q_TILE == K_CHUNK
    kern = functools.partial(_attn_prompt_kernel, n_pages=n_pages, npp=npp)
    per_b4 = lambda bi, i: (bi, 0, 0, 0)
    per_b5 = lambda bi, i: (bi, 0, 0, 0, 0)
    return pl.pallas_call(
        kern,
        grid=(b, s // Q_TILE),
        in_specs=[
            pl.BlockSpec((1, Q_TILE, ATTN_WIDTH), lambda bi, i: (bi, i, 0)),
            pl.BlockSpec((1, Q_TILE, GATE_PAD), lambda bi, i: (bi, i, 0)),
            pl.BlockSpec((1, N_KV_HEADS, npp, LANES), per_b4),
            pl.BlockSpec((1, N_KV_HEADS, npp, LANES), per_b4),
            pl.BlockSpec((1, n_chunks, LANES, K_CHUNK), per_b4),
            pl.BlockSpec((1, n_chunks, N_KV_HEADS, K_CHUNK, LANES), per_b5),
            pl.BlockSpec((1, n_chunks, LANES, K_CHUNK), per_b4),
            pl.BlockSpec((1, n_chunks, N_KV_HEADS, K_CHUNK, LANES), per_b5),
            pl.BlockSpec((n_chunks, 2 * npp, K_CHUNK), lambda bi, i: (0, 0, 0)),
            pl.BlockSpec((GATE_PAD, 3 * ATTN_WIDTH), lambda bi, i: (0, 0)),
        ],
        out_specs=pl.BlockSpec((1, Q_TILE, ATTN_WIDTH), lambda bi, i: (bi, i, 0)),
        out_shape=jax.ShapeDtypeStruct((b, s, ATTN_WIDTH), BF16),
        scratch_shapes=[
            pltpu.VMEM((n_chunks, N_KV_HEADS * rows, K_CHUNK), F32),
            pltpu.VMEM((N_KV_HEADS * rows, LANES), F32),
            pltpu.VMEM((N_KV_HEADS * rows, LANES), F32),
        ],
        compiler_params=pltpu.CompilerParams(
            dimension_semantics=("arbitrary", "arbitrary"), vmem_limit_bytes=VMEM_LIMIT),
        name="attn_prompt",
    )(q, gates, kc, vc, ks16, vsx, kw16, vwx, expand, gate_place)


S_ROWS = LANES
S_CHUNK = 1024


def _sample_rows(q_ref, ds, past):
    n_real = N_HEADS * ds
    by_kvh, low = [], []
    for h in range(N_HEADS):
        tile = q_ref[0, :, (h // 2) * LANES:(h // 2 + 1) * LANES].astype(F32) * SCALE
        by_kvh.append(_to_half(tile, h % 2, h // Q_PER_KV))
        low.append(_to_half(tile, h % 2, 0))
    zpad = jnp.zeros((S_ROWS - n_real, LANES), F32)
    qrows = jnp.concatenate(by_kvh + [zpad], axis=0).astype(BF16)
    qlow = jnp.concatenate(low + [zpad], axis=0).astype(BF16)
    r_col = lax.broadcasted_iota(jnp.int32, (S_ROWS, 1), 0)
    q_of_r = r_col % ds
    h_of_r = r_col // ds
    sl_r = jnp.zeros((S_ROWS, 1), F32)
    for h in range(N_HEADS):
        sl_r = jnp.where(h_of_r == h, _slope(h), sl_r)
    return qrows, qlow, q_of_r, past + q_of_r, sl_r, h_of_r // Q_PER_KV


def _select_sample_kernel(q_ref, kc_ref, vc_ref, ocmp_ref, sel_ref, need_ref, *, past, ds, npp):
    n_real = N_HEADS * ds
    ncols = 2 * npp
    n_blocks = (past + ds + CMP_BLOCK - 1) // CMP_BLOCK
    _, qlow, _, qp_r, sl_r, kvh_r = _sample_rows(q_ref, ds, past)

    blk = _block_of_col(lax.broadcasted_iota(jnp.int32, (S_ROWS, ncols), 1), npp)
    s = jnp.where(kvh_r == 0, _dot_t(qlow, _cmp_operand(kc_ref[0, 0], npp)),
                  _dot_t(qlow, _cmp_operand(kc_ref[0, 1], npp)))
    dist_c = qp_r - (blk * CMP_BLOCK + (CMP_BLOCK - 1))
    mask_c = dist_c >= 0
    z = jnp.where(mask_c, s - sl_r * dist_c.astype(F32), NEG)
    mx = jnp.max(z, axis=-1, keepdims=True)
    e = jnp.where(mask_c, jnp.exp(z - mx), 0.0)
    den = jnp.sum(e, axis=-1, keepdims=True)
    p = e / jnp.where(den > 0.0, den, 1.0)
    pb = p.astype(BF16)
    o_cmp = jnp.where(kvh_r == 0, jnp.dot(pb, _cmp_operand(vc_ref[0, 0], npp), preferred_element_type=F32),
                      jnp.dot(pb, _cmp_operand(vc_ref[0, 1], npp), preferred_element_type=F32))

    kq = N_KV_HEADS * ds
    p_kv = []
    for kvh in range(N_KV_HEADS):
        acc = jnp.zeros((ds, ncols), F32)
        for g in range(Q_PER_KV):
            r0 = (kvh * Q_PER_KV + g) * ds
            acc = acc + p[r0:r0 + ds]
        p_kv.append(acc)
    p_kv = jnp.concatenate(p_kv, axis=0)
    blk2 = _block_of_col(lax.broadcasted_iota(jnp.int32, (kq, ncols), 1), npp)
    qp2 = past + lax.broadcasted_iota(jnp.int32, (kq, 1), 0) % ds
    cand = blk2 * CMP_BLOCK <= qp2
    forced = (blk2 == qp2 // CMP_BLOCK) | (blk2 == 0)
    score = jnp.where(forced, FORCE_SCORE, jnp.where(cand, p_kv, -1.0))
    rank = jnp.zeros((kq, ncols), jnp.int32)
    for n in range(n_blocks):
        c = (n % 2) * npp + n // 2
        col = score[:, c:c + 1]
        ahead = (col > score) | ((col == score) & (blk2 > n))
        rank = rank + ahead.astype(jnp.int32)
    sel2 = ((rank < TOP_K_BLOCKS) & (score > -0.5)).astype(F32)
    sel_rows = []
    for h in range(N_HEADS):
        kvh = h // Q_PER_KV
        sel_rows.append(sel2[kvh * ds:(kvh + 1) * ds])
    sel_rows.append(jnp.zeros((S_ROWS - n_real, ncols), F32))
    sel_rows = jnp.concatenate(sel_rows, axis=0)
    ocmp_ref[0] = o_cmp
    sel_ref[0] = sel_rows
    any_row = jnp.max(sel_rows, axis=0, keepdims=True)
    page_need = jnp.maximum(any_row[:, :npp], any_row[:, npp:])
    need_ref[0] = jnp.broadcast_to(page_need, (8, npp))


def _select_sample(q, kc, vc, *, past):
    b, ds, _ = q.shape
    npp = kc.shape[2]
    kern = functools.partial(_select_sample_kernel, past=past, ds=ds, npp=npp)
    per_b3 = lambda i: (i, 0, 0)
    per_b4 = lambda i: (i, 0, 0, 0)
    return pl.pallas_call(
        kern,
        grid=(b,),
        in_specs=[pl.BlockSpec((1, ds, ATTN_WIDTH), per_b3),
                  pl.BlockSpec((1, N_KV_HEADS, npp, LANES), per_b4),
                  pl.BlockSpec((1, N_KV_HEADS, npp, LANES), per_b4)],
        out_specs=(pl.BlockSpec((1, S_ROWS, LANES), per_b3), pl.BlockSpec((1, S_ROWS, 2 * npp), per_b3),
                   pl.BlockSpec((1, 8, npp), per_b3)),
        out_shape=(jax.ShapeDtypeStruct((b, S_ROWS, LANES), F32), jax.ShapeDtypeStruct((b, S_ROWS, 2 * npp), F32),
                   jax.ShapeDtypeStruct((b, 8, npp), F32)),
        compiler_params=pltpu.CompilerParams(dimension_semantics=("arbitrary",), vmem_limit_bytes=VMEM_LIMIT),
        name="select_sample",
    )(q, kc, vc)


def _attn_sample_kernel(pt_ref, order_ref, nn_ref, q_ref, gates_ref, ocmp_ref, sel_ref, kvs_new_ref, kvw_new_ref,
                        win_ref, cache_ref, a_ref, kbuf, zbuf, sem, *, n_pages, past, ds, npp):
    b = pl.program_id(0)
    nb = pl.num_programs(0)
    ncols = 2 * npp
    half_cols = KV_COLS // 2
    pages_per_chunk = S_CHUNK // PAGE

    def page_copy(row, j, slot):
        return pltpu.make_async_copy(
            cache_ref.at[pt_ref[row, order_ref[row, j]]], kbuf.at[slot, :, pl.ds(j * PAGE, PAGE)], sem.at[slot])

    def row_copies(row, slot, fn):
        def body(j, carry):
            fn(page_copy(row, j, slot))
            return carry
        lax.fori_loop(0, nn_ref[row], body, 0)

    @pl.when(b == 0)
    def _():
        kbuf[...] = jnp.zeros(kbuf.shape, F32)
        row_copies(0, 0, lambda cp: cp.start())

    slot = b % 2

    @pl.when(b + 1 < nb)
    def _():
        row_copies(b + 1, 1 - slot, lambda cp: cp.start())

    qrows, _, q_of_r, qp_r, sl_r, _ = _sample_rows(q_ref, ds, past)
    o_cmp = ocmp_ref[0]
    sel_rows = sel_ref[0]
    sel16 = sel_rows.astype(BF16)

    new_lane = lax.broadcasted_iota(jnp.int32, (1, LANES), 1)

    def new_scores(ref, extra_mask):
        k_new = ref[0, :half_cols, :].astype(BF16)
        s_new = jnp.dot(qrows, k_new, preferred_element_type=F32)
        dist = q_of_r - new_lane
        mask = (new_lane < ds) & (dist >= 0) & extra_mask
        return jnp.where(mask, s_new - sl_r * dist.astype(F32), NEG)

    def weighted_new(ref, e_new):
        return _dot_t(e_new.astype(BF16), ref[0, half_cols:, :].astype(BF16))

    row_copies(b, slot, lambda cp: cp.wait())
    n_need = nn_ref[b]
    n_chunks = (n_need + pages_per_chunk - 1) // pages_per_chunk

    def pass1(c, mrow):
        k0 = pl.multiple_of(c * S_CHUNK, S_CHUNK)
        kt = kbuf[slot, :half_cols, pl.ds(k0, S_CHUNK)].astype(BF16)
        st = jnp.dot(qrows, kt, preferred_element_type=F32)
        lane = lax.broadcasted_iota(jnp.int32, (1, S_CHUNK), 1)
        tpos = jnp.full((1, S_CHUNK), -PAGE * n_pages, jnp.int32)
        for jj in range(pages_per_chunk):
            j = c * pages_per_chunk + jj
            page = order_ref[b, jnp.minimum(j, n_pages - 1)]
            base = jnp.where(j < n_need, page * PAGE, -PAGE * n_pages) - jj * PAGE
            tpos = jnp.where(lane // PAGE == jj, base + lane, tpos)
        blk_c = _block_of_col(lax.broadcasted_iota(jnp.int32, (ncols, S_CHUNK), 0), npp)
        expand = (blk_c == jnp.broadcast_to(tpos, (ncols, S_CHUNK)) // CMP_BLOCK).astype(BF16)
        chosen = jnp.dot(sel16, expand, preferred_element_type=F32) > 0.5
        dist = qp_r - tpos
        zc = jnp.where(chosen & (dist >= 0), st - sl_r * dist.astype(F32), NEG)
        zbuf[c] = zc
        return jnp.maximum(mrow, jnp.max(zc, axis=-1, keepdims=True))

    mrow = lax.fori_loop(0, n_chunks, pass1, jnp.full((S_ROWS, 1), NEG, F32))
    c_last = ((past // CMP_BLOCK) % 2) * npp + (past // CMP_BLOCK) // 2
    z_new = new_scores(kvs_new_ref, sel_rows[:, c_last:c_last + 1] > 0.5)
    m_r = jnp.maximum(mrow, jnp.max(z_new, axis=-1, keepdims=True))

    def pass2(c, carry):
        acc, den_r = carry
        k0 = pl.multiple_of(c * S_CHUNK, S_CHUNK)
        zc = zbuf[c]
        ec = jnp.where(zc > 0.5 * NEG, jnp.exp(zc - m_r), 0.0)
        vt = kbuf[slot, half_cols:, pl.ds(k0, S_CHUNK)].astype(BF16)
        return acc + _dot_t(ec.astype(BF16), vt), den_r + jnp.sum(ec, axis=-1, keepdims=True)

    e_new = jnp.where(z_new > 0.5 * NEG, jnp.exp(z_new - m_r), 0.0)
    acc, den_r = lax.fori_loop(0, n_chunks, pass2,
                               (weighted_new(kvs_new_ref, e_new), jnp.sum(e_new, axis=-1, keepdims=True)))
    o_slc = acc / jnp.where(den_r > 0.0, den_r, 1.0)

    win_buf = win_ref.shape[2]
    kw = win_ref[0, :half_cols, :].astype(BF16)
    st = jnp.dot(qrows, kw, preferred_element_type=F32)
    kpos = past - win_buf + lax.broadcasted_iota(jnp.int32, (1, win_buf), 1)
    dist = qp_r - kpos
    mask = (dist >= 0) & (dist < WINDOW) & (kpos >= 0)
    z_w = jnp.where(mask, st - sl_r * dist.astype(F32), NEG)
    zw_new = new_scores(kvw_new_ref, True)
    m_w = jnp.maximum(jnp.max(z_w, axis=-1, keepdims=True), jnp.max(zw_new, axis=-1, keepdims=True))
    e_w = jnp.where(z_w > 0.5 * NEG, jnp.exp(z_w - m_w), 0.0)
    ew_new = jnp.where(zw_new > 0.5 * NEG, jnp.exp(zw_new - m_w), 0.0)
    acc_w = _dot_t(e_w.astype(BF16), win_ref[0, half_cols:, :].astype(BF16)) + weighted_new(kvw_new_ref, ew_new)
    den_w = jnp.sum(e_w, axis=-1, keepdims=True) + jnp.sum(ew_new, axis=-1, keepdims=True)
    o_win = acc_w / jnp.where(den_w > 0.0, den_w, 1.0)

    gates = gates_ref[0]
    for pair in range(N_HEADS // 2):
        kvh = (2 * pair) // Q_PER_KV
        e_sl = slice(2 * pair * ds, (2 * pair + 1) * ds)
        o_sl = slice((2 * pair + 1) * ds, (2 * pair + 2) * ds)
        shape = (ds, LANES)
        tile = (_gate_tile(gates, pair, 0, shape) * _pair_tile(o_cmp[e_sl], o_cmp[o_sl], 0)
                + _gate_tile(gates, pair, 1, shape) * _pair_tile(o_slc[e_sl], o_slc[o_sl], kvh)
                + _gate_tile(gates, pair, 2, shape) * _pair_tile(o_win[e_sl], o_win[o_sl], kvh))
        a_ref[0, :, pair * LANES:(pair + 1) * LANES] = tile.astype(BF16)


def _attn_sample(page_table, order, n_need, q, gates, o_cmp, sel_rows, kvs_new, kvw_new, state_win_t, cache_pages,
                 *, past):
    b, n_pages = page_table.shape
    ds = q.shape[1]
    npp = sel_rows.shape[2] // 2
    win_buf = state_win_t.shape[2]
    n_keys = n_pages * PAGE
    kern = functools.partial(_attn_sample_kernel, n_pages=n_pages, past=past, ds=ds, npp=npp)
    per_b = lambda i, pt, od, nn: (i, 0, 0)
    grid_spec = pltpu.PrefetchScalarGridSpec(
        num_scalar_prefetch=3,
        grid=(b,),
        in_specs=[
            pl.BlockSpec((1, ds, ATTN_WIDTH), per_b),
            pl.BlockSpec((1, ds, GATE_PAD), per_b),
            pl.BlockSpec((1, S_ROWS, LANES), per_b),
            pl.BlockSpec((1, S_ROWS, 2 * npp), per_b),
            pl.BlockSpec((1, KV_COLS, LANES), per_b),
            pl.BlockSpec((1, KV_COLS, LANES), per_b),
            pl.BlockSpec((1, KV_COLS, win_buf), per_b),
            pl.BlockSpec(memory_space=pl.ANY),
        ],
        out_specs=pl.BlockSpec((1, ds, ATTN_WIDTH), per_b),
        scratch_shapes=[
            pltpu.VMEM((2, KV_COLS, n_keys), F32),
            pltpu.VMEM((n_keys // S_CHUNK, S_ROWS, S_CHUNK), F32),
            pltpu.SemaphoreType.DMA((2,)),
        ],
    )
    return pl.pallas_call(
        kern,
        grid_spec=grid_spec,
        out_shape=jax.ShapeDtypeStruct((b, ds, ATTN_WIDTH), BF16),
        compiler_params=pltpu.CompilerParams(
            dimension_semantics=("arbitrary",), vmem_limit_bytes=VMEM_LIMIT),
        name="attn_sample",
    )(page_table, order, n_need, q, gates, o_cmp, sel_rows, kvs_new, kvw_new, state_win_t, cache_pages)


X_TILES = D_MODEL // LANES
PAY_SUB = X_TILES + 1
GROUP_LANE = 0


def _rows_to_tiles(ref, lo, val):
    for k in range(val.shape[1] // LANES):
        ref[:, lo + k, :] = val[:, k * LANES:(k + 1) * LANES]


def _tiles_to_rows(ref, lo, n):
    return jnp.concatenate([ref[:, lo + k, :] for k in range(n)], axis=1)


def _ffn_kernel(x_ref, a_ref, m_ref, wo_ref, g2_ref, wr_ref, *out_refs, packed):
    h = (x_ref[...]
         + jnp.dot(a_ref[...], wo_ref[:ATTN_WIDTH, :], preferred_element_type=F32)
         + jnp.dot(m_ref[...], wo_ref[ATTN_WIDTH:, :], preferred_element_type=F32))
    ms = jnp.mean(h * h, axis=-1, keepdims=True)
    n2f = (h * lax.rsqrt(ms + EPS)) * g2_ref[...]
    n2 = n2f.astype(BF16)
    logits = jnp.dot(n2, wr_ref[...], preferred_element_type=F32)
    lane = lax.broadcasted_iota(jnp.int32, logits.shape, 1)
    big = jnp.int32(LANES)

    def masked_softmax(mask):
        zz = jnp.where(mask, logits, NEG)
        mx = jnp.max(zz, axis=-1, keepdims=True)
        ee = jnp.where(mask, jnp.exp(zz - mx), 0.0)
        return ee / jnp.sum(ee, axis=-1, keepdims=True)

    def first_argmax(vals, mask):
        v = jnp.max(jnp.where(mask, vals, -1.0), axis=-1, keepdims=True)
        idx = jnp.min(jnp.where(mask & (vals == v), lane, big), axis=-1, keepdims=True)
        return v, idx

    is_g = lane < N_EXPERT_GROUPS
    pg = masked_softmax(is_g)
    g_val, g_idx = first_argmax(pg, is_g)
    e_lane = lane - ROUTER_LANE0
    in_grp = (e_lane >= 0) & (e_lane < N_EXPERTS) & (e_lane // EXPERTS_PER_GROUP == g_idx)
    pe = masked_softmax(in_grp)
    v1, i1 = first_argmax(pe, in_grp)
    rest = in_grp & (lane != i1)
    v2, i2 = first_argmax(pe, rest)
    scale = g_val / (v1 + v2)
    gate = jnp.where(lane == i1, v1 * scale, jnp.where(lane == i2, v2 * scale, 0.0))
    gate = jnp.where(lane == GROUP_LANE, g_idx.astype(F32), gate)
    if packed:
        h_ref, pay_ref, gate_ref = out_refs
        h_ref[...] = h
        gate_ref[...] = gate
        _rows_to_tiles(pay_ref, 0, n2f)
        pay_ref[:, X_TILES, :] = gate
    else:
        h_ref, n2_ref, gate_ref = out_refs
        h_ref[...] = h
        n2_ref[...] = n2
        gate_ref[...] = gate


def _ffn(x2d, a2d, m2d, w_out16, g2, w_router16, tm, packed):
    t = x2d.shape[0]
    row = lambda i: (i, 0)
    const = lambda i: (0, 0)
    if packed:
        out_specs = (pl.BlockSpec((tm, D_MODEL), row), pl.BlockSpec((tm, PAY_SUB, LANES), lambda i: (i, 0, 0)),
                     pl.BlockSpec((tm, LANES), row))
        out_shape = (jax.ShapeDtypeStruct((t, D_MODEL), F32), jax.ShapeDtypeStruct((t, PAY_SUB, LANES), F32),
                     jax.ShapeDtypeStruct((t, LANES), F32))
    else:
        out_specs = (pl.BlockSpec((tm, D_MODEL), row), pl.BlockSpec((tm, D_MODEL), row),
                     pl.BlockSpec((tm, LANES), row))
        out_shape = (jax.ShapeDtypeStruct((t, D_MODEL), F32), jax.ShapeDtypeStruct((t, D_MODEL), BF16),
                     jax.ShapeDtypeStruct((t, LANES), F32))
    return pl.pallas_call(
        functools.partial(_ffn_kernel, packed=packed),
        grid=(t // tm,),
        in_specs=[
            pl.BlockSpec((tm, D_MODEL), row),
            pl.BlockSpec((tm, ATTN_WIDTH), row),
            pl.BlockSpec((tm, POOL_WIDTH), row),
            pl.BlockSpec((D_MODEL, D_MODEL), const),
            pl.BlockSpec((1, D_MODEL), const),
            pl.BlockSpec((D_MODEL, LANES), const),
        ],
        out_specs=out_specs,
        out_shape=out_shape,
        compiler_params=pltpu.CompilerParams(
            dimension_semantics=("arbitrary",), vmem_limit_bytes=VMEM_LIMIT),
        name="ffn",
    )(x2d, a2d, m2d, w_out16, g2, w_router16)


MOE_ROWS = 256
ROW_CHUNK = 512
PLAN_TILE = 1024


def _plan_kernel(gate_ref, tri_ref, out_ref, cnt_ref, carry):
    @pl.when(pl.program_id(0) == 0)
    def _():
        carry[...] = jnp.zeros(carry.shape, F32)

    gl = gate_ref[...]
    lane = lax.broadcasted_iota(jnp.int32, gl.shape, 1)
    gid = gl[:, GROUP_LANE:GROUP_LANE + 1]
    onehot = jnp.where((lane < N_EXPERT_GROUPS) & (lane.astype(F32) == gid), 1.0, 0.0)
    before = jnp.dot(tri_ref[...], onehot.astype(BF16), preferred_element_type=F32) + carry[0:1, :]
    rank = jnp.sum(onehot * before, axis=-1, keepdims=True)
    out_ref[...] = jnp.where(lane == 0, rank, jnp.where(lane == 1, gid, 0.0))
    carry[0:1, :] = carry[0:1, :] + jnp.sum(onehot, axis=0, keepdims=True)
    cnt_ref[...] = carry[...]


def _plan(gate):
    t = gate.shape[0]
    tri = jnp.asarray(np.tril(np.ones((PLAN_TILE, PLAN_TILE), np.float32), -1), dtype=BF16)
    return pl.pallas_call(
        _plan_kernel,
        grid=(t // PLAN_TILE,),
        in_specs=[pl.BlockSpec((PLAN_TILE, LANES), lambda i: (i, 0)),
                  pl.BlockSpec((PLAN_TILE, PLAN_TILE), lambda i: (0, 0))],
        out_specs=(pl.BlockSpec((PLAN_TILE, LANES), lambda i: (i, 0)), pl.BlockSpec((8, LANES), lambda i: (0, 0))),
        out_shape=(jax.ShapeDtypeStruct((t, LANES), F32), jax.ShapeDtypeStruct((8, LANES), F32)),
        scratch_shapes=[pltpu.VMEM((8, LANES), F32)],
        compiler_params=pltpu.CompilerParams(
            dimension_semantics=("arbitrary",), vmem_limit_bytes=VMEM_LIMIT),
        name="moe_plan",
    )(gate, tri)


def _scatter_kernel(meta_ref, slot_ref, pay_ref, xs_ref, zrow, sem, zsem):
    c = pl.program_id(0)

    @pl.when(c == 0)
    def _():
        zrow[...] = jnp.zeros(zrow.shape, F32)
        for g in range(N_EXPERT_GROUPS + 1):
            lo = meta_ref[g]
            hi = meta_ref[N_EXPERT_GROUPS + 1 + g]

            def zero_start(r, carry):
                pltpu.make_async_copy(zrow.at[pl.ds(0, 1)], xs_ref.at[pl.ds(r, 1)], zsem.at[0]).start()
                return carry

            def zero_wait(r, carry):
                pltpu.make_async_copy(zrow.at[pl.ds(0, 1)], xs_ref.at[pl.ds(0, 1)], zsem.at[0]).wait()
                return carry

            lax.fori_loop(lo, hi, zero_start, 0)
            lax.fori_loop(lo, hi, zero_wait, 0)

    def issue(u, carry):
        pltpu.make_async_copy(pay_ref.at[pl.ds(u, 1)], xs_ref.at[pl.ds(slot_ref[0, 0, u], 1)], sem.at[0]).start()
        return carry

    lax.fori_loop(0, ROW_CHUNK, issue, 0, unroll=8)
    pltpu.make_async_copy(pay_ref, xs_ref.at[pl.ds(0, ROW_CHUNK)], sem.at[0]).wait()


def _scatter(meta, slot3, pay, n_slots):
    t = pay.shape[0]
    return pl.pallas_call(
        _scatter_kernel,
        grid_spec=pltpu.PrefetchScalarGridSpec(
            num_scalar_prefetch=1,
            grid=(t // ROW_CHUNK,),
            in_specs=[pl.BlockSpec((1, 1, ROW_CHUNK), lambda c, meta: (c, 0, 0), memory_space=pltpu.SMEM),
                      pl.BlockSpec((ROW_CHUNK, PAY_SUB, LANES), lambda c, meta: (c, 0, 0))],
            out_specs=pl.BlockSpec(memory_space=pl.ANY),
            scratch_shapes=[pltpu.VMEM((1, PAY_SUB, LANES), F32), pltpu.SemaphoreType.DMA((1,)),
                            pltpu.SemaphoreType.DMA((1,))],
        ),
        out_shape=jax.ShapeDtypeStruct((n_slots, PAY_SUB, LANES), F32),
        compiler_params=pltpu.CompilerParams(dimension_semantics=("arbitrary",), vmem_limit_bytes=VMEM_LIMIT),
        name="moe_scatter",
    )(meta, slot3, pay)


def _gather_kernel(slot_ref, nslot_ref, h_ref, ys_ref, y_ref, buf, sem):
    c = pl.program_id(0)
    nc = pl.num_programs(0)

    def fetch(ref, slot):
        def issue(u, carry):
            pltpu.make_async_copy(ys_ref.at[pl.ds(ref[0, 0, u], 1)], buf.at[slot, pl.ds(u, 1)], sem.at[slot]).start()
            return carry
        lax.fori_loop(0, ROW_CHUNK, issue, 0, unroll=8)

    @pl.when(c == 0)
    def _():
        fetch(slot_ref, 0)

    @pl.when(c + 1 < nc)
    def _():
        fetch(nslot_ref, (c + 1) % 2)

    cur = c % 2
    pltpu.make_async_copy(ys_ref.at[pl.ds(0, ROW_CHUNK)], buf.at[cur], sem.at[cur]).wait()
    y_ref[...] = h_ref[...] + jnp.concatenate([buf[cur, :, k, :] for k in range(X_TILES)], axis=1)


def _gather(slot3, h, ys):
    t = h.shape[0]
    nc = t // ROW_CHUNK
    return pl.pallas_call(
        _gather_kernel,
        grid=(nc,),
        in_specs=[pl.BlockSpec((1, 1, ROW_CHUNK), lambda c: (c, 0, 0), memory_space=pltpu.SMEM),
                  pl.BlockSpec((1, 1, ROW_CHUNK), lambda c: (jnp.minimum(c + 1, nc - 1), 0, 0),
                               memory_space=pltpu.SMEM),
                  pl.BlockSpec((ROW_CHUNK, D_MODEL), lambda c: (c, 0)),
                  pl.BlockSpec(memory_space=pl.ANY)],
        out_specs=pl.BlockSpec((ROW_CHUNK, D_MODEL), lambda c: (c, 0)),
        out_shape=jax.ShapeDtypeStruct((t, D_MODEL), F32),
        scratch_shapes=[pltpu.VMEM((2, ROW_CHUNK, X_TILES, LANES), F32), pltpu.SemaphoreType.DMA((2,))],
        compiler_params=pltpu.CompilerParams(dimension_semantics=("arbitrary",), vmem_limit_bytes=VMEM_LIMIT),
        name="moe_gather",
    )(slot3, slot3, h, ys)


def _group_moe_kernel(tg_ref, nu_ref, xs_ref, wg_ref, wu_ref, wd_ref, ys_ref):
    j = pl.program_id(0)

    @pl.when(j < nu_ref[0])
    def _():
        g = tg_ref[j]
        x = _tiles_to_rows(xs_ref, 0, X_TILES).astype(BF16)
        gate = xs_ref[:, X_TILES, :]
        lane = lax.broadcasted_iota(jnp.int32, gate.shape, 1)
        y = jnp.zeros((MOE_ROWS, D_MODEL), F32)
        for e in range(EXPERTS_PER_GROUP):
            gu = jnp.dot(x, wg_ref[e], preferred_element_type=F32)
            up = jnp.dot(x, wu_ref[e], preferred_element_type=F32)
            he = (gu * (1.0 / (1.0 + jnp.exp(-gu)))) * up
            out = jnp.dot(he.astype(BF16), wd_ref[e], preferred_element_type=F32)
            col = ROUTER_LANE0 + g * EXPERTS_PER_GROUP + e
            y = y + jnp.sum(jnp.where(lane == col, gate, 0.0), axis=-1, keepdims=True) * out
        _rows_to_tiles(ys_ref, 0, y)

    @pl.when(j >= nu_ref[0])
    def _():
        ys_ref[...] = jnp.zeros(ys_ref.shape, F32)


def _group_moe(tile_group, n_used, xs, wg16, wu16, wd16):
    n_slots = xs.shape[0]
    wspec = lambda shape: pl.BlockSpec((EXPERTS_PER_GROUP,) + shape, lambda j, tg, nu: (tg[j], 0, 0))
    return pl.pallas_call(
        _group_moe_kernel,
        grid_spec=pltpu.PrefetchScalarGridSpec(
            num_scalar_prefetch=2,
            grid=(n_slots // MOE_ROWS,),
            in_specs=[pl.BlockSpec((MOE_ROWS, PAY_SUB, LANES), lambda j, tg, nu: (j, 0, 0)),
                      wspec((D_MODEL, D_EXPERT)), wspec((D_MODEL, D_EXPERT)), wspec((D_EXPERT, D_MODEL))],
            out_specs=pl.BlockSpec((MOE_ROWS, X_TILES, LANES), lambda j, tg, nu: (j, 0, 0)),
        ),
        out_shape=jax.ShapeDtypeStruct((n_slots, X_TILES, LANES), F32),
        compiler_params=pltpu.CompilerParams(dimension_semantics=("arbitrary",), vmem_limit_bytes=VMEM_LIMIT),
        name="moe_group",
    )(tile_group, n_used, xs, wg16, wu16, wd16)


def _routed_moe(h, pay, gate, wg16, wu16, wd16):
    t = pay.shape[0]
    n_slots = t + N_EXPERT_GROUPS * MOE_ROWS
    n_tiles = n_slots // MOE_ROWS
    plan, counts = _plan(gate)
    rank = plan[:, 0].astype(jnp.int32)
    gid = plan[:, 1].astype(jnp.int32)
    cnt = counts[0, :N_EXPERT_GROUPS].astype(jnp.int32)
    padded = -(-cnt // MOE_ROWS) * MOE_ROWS
    ends = jnp.cumsum(padded)
    off = ends - padded
    slot = (off[gid] + rank).reshape(t // ROW_CHUNK, 1, ROW_CHUNK)
    tile_start = jnp.arange(n_tiles, dtype=jnp.int32) * MOE_ROWS
    tile_group = jnp.minimum(jnp.sum(tile_start[:, None] >= ends[None, :], axis=1), N_EXPERT_GROUPS - 1)
    n_used = (ends[-1] // MOE_ROWS).reshape(1)
    meta = jnp.concatenate([off + cnt, ends[-1:], ends, jnp.full((1,), n_slots, jnp.int32)]).astype(jnp.int32)
    xs = _scatter(meta, slot, pay, n_slots)
    ys = _group_moe(tile_group.astype(jnp.int32), n_used.astype(jnp.int32), xs, wg16, wu16, wd16)
    return _gather(slot, h, ys)


def _moe_kernel(n2_ref, gate_ref, h_ref, wg_ref, wu_ref, wd_ref, y_ref):
    e = pl.program_id(1)

    @pl.when(e == 0)
    def _():
        y_ref[...] = h_ref[...]

    n2 = n2_ref[...]
    gu = jnp.dot(n2, wg_ref[0], preferred_element_type=F32)
    up = jnp.dot(n2, wu_ref[0], preferred_element_type=F32)
    he = (gu * (1.0 / (1.0 + jnp.exp(-gu)))) * up
    out = jnp.dot(he.astype(BF16), wd_ref[0], preferred_element_type=F32)
    lane = lax.broadcasted_iota(jnp.int32, gate_ref.shape, 1)
    gcol = jnp.sum(jnp.where(lane == e + ROUTER_LANE0, gate_ref[...], 0.0), axis=-1, keepdims=True)
    y_ref[...] += gcol * out


def _moe(n2, gate, h, wg16, wu16, wd16, tm):
    t = n2.shape[0]
    row = lambda i, e: (i, 0)
    return pl.pallas_call(
        _moe_kernel,
        grid=(t // tm, N_EXPERTS),
        in_specs=[
            pl.BlockSpec((tm, D_MODEL), row),
            pl.BlockSpec((tm, LANES), row),
            pl.BlockSpec((tm, D_MODEL), row),
            pl.BlockSpec((1, D_MODEL, D_EXPERT), lambda i, e: (e, 0, 0)),
            pl.BlockSpec((1, D_MODEL, D_EXPERT), lambda i, e: (e, 0, 0)),
            pl.BlockSpec((1, D_EXPERT, D_MODEL), lambda i, e: (e, 0, 0)),
        ],
        out_specs=pl.BlockSpec((tm, D_MODEL), row),
        out_shape=jax.ShapeDtypeStruct((t, D_MODEL), F32),
        compiler_params=pltpu.CompilerParams(
            dimension_semantics=("arbitrary", "arbitrary"), vmem_limit_bytes=VMEM_LIMIT),
        name="moe",
    )(n2, gate, h, wg16, wu16, wd16)


PROJ_TILE = 512
MOE_TILE = 1024
PROMPT_NPP = LANES // 2
SAMPLE_NPP = LANES


def _round_up(n, mult):
    return -(-n // mult) * mult


def _feature_major(x5):
    b, t = x5.shape[:2]
    return jnp.transpose(x5, (0, 2, 3, 4, 1)).reshape(b, KV_COLS, t)


def _token_major(xt):
    b, _, t = xt.shape
    return jnp.transpose(xt.reshape(b, 2, N_KV_HEADS, HEAD_DIM, t), (0, 4, 1, 2, 3))


def _pad_pages(x, npp_from, npp_to):
    b = x.shape[0]
    x = x.reshape(b, N_KV_HEADS, npp_from, LANES)
    return jnp.concatenate([x, jnp.zeros((b, N_KV_HEADS, npp_to - npp_from, LANES), x.dtype)], axis=2)


def _expand_const(n_chunks, npp):
    col = np.arange(2 * npp)
    blk = 2 * (col % npp) + col // npp
    tok_blk = np.arange(n_chunks * K_CHUNK) // CMP_BLOCK
    e = (blk[None, :, None] == tok_blk.reshape(n_chunks, 1, K_CHUNK)).astype(np.float32)
    return jnp.asarray(e, dtype=BF16)


def _gate_place_const():
    gp = np.zeros((GATE_PAD, 3 * ATTN_WIDTH), np.float32)
    for h in range(N_HEADS):
        for c in range(3):
            gp[3 * h + c, c * ATTN_WIDTH + h * HEAD_DIM:c * ATTN_WIDTH + (h + 1) * HEAD_DIM] = 1.0
    return jnp.asarray(gp, dtype=BF16)


def kernel(x_prompt, x_sample, cache_cmp_kv, cache_slc_kv, state_win_kv, state_pool, page_table, norm1_g, w_in, q_norm_g, k_norm_cmp_g, k_norm_slc_g, k_norm_win_g, cmp_pos_emb, w_cmp_k, w_cmp_v, pool_w, pool_scale, w_out, norm2_g, w_router_group, w_router_expert, w_gate, w_up, w_down):
    b, s, _ = x_prompt.shape
    db, ds, _ = x_sample.shape
    n_pool, page_rows = cache_cmp_kv.shape[:2]
    assert page_rows == PAGE and s % PAGE == 0
    past = page_table.shape[1] * page_rows

    kv0 = ATTN_WIDTH
    kv1 = kv0 + KV_ROWS
    w_row = jnp.concatenate(
        [w_in[:, :kv0], w_in[:, kv1:kv1 + GATE_COLS], jnp.zeros((D_MODEL, GATE_PAD - GATE_COLS), w_in.dtype),
         w_in[:, kv1 + GATE_COLS:]], axis=1).astype(BF16)
    w_kv = w_in[:, kv0:kv1].T.astype(BF16)
    g1 = norm1_g[None, :]
    g2 = norm2_g[None, :]
    two = lambda g: jnp.tile(g, 2)[None, :]
    kg = jnp.stack([k_norm_slc_g, k_norm_win_g])[:, :, None]
    pos_t = jnp.tile(cmp_pos_emb.T, (1, 2))
    zb = jnp.zeros((HEAD_DIM, CMP_BLOCK, HEAD_DIM), F32)

    def blockdiag(w):
        wt = jnp.transpose(w, (1, 0, 2))
        return jnp.concatenate([jnp.concatenate([wt, zb], axis=2), jnp.concatenate([zb, wt], axis=2)], axis=1)

    bdk, bdv = blockdiag(w_cmp_k), blockdiag(w_cmp_v)
    zq = jnp.zeros_like(bdk)
    bd = jnp.concatenate([jnp.concatenate([bdk, zq], axis=2), jnp.concatenate([zq, bdv], axis=2)],
                         axis=1).astype(BF16)
    w_router = jnp.concatenate(
        [w_router_group, w_router_expert,
         jnp.zeros((D_MODEL, LANES - N_EXPERT_GROUPS - N_EXPERTS), w_router_group.dtype)], axis=1).astype(BF16)
    w_out16 = w_out.astype(BF16)
    pool_w16 = pool_w.astype(BF16)
    wg16, wu16, wd16 = w_gate.astype(BF16), w_up.astype(BF16), w_down.astype(BF16)
    ps = pool_scale[None, :]

    def ffn_moe(x2d, a2d, m2d, tm_ffn, tm_moe, routed):
        if routed:
            return _routed_moe(*_ffn(x2d, a2d, m2d, w_out16, g2, w_router, tm_ffn, True), wg16, wu16, wd16)
        h, n2, gate = _ffn(x2d, a2d, m2d, w_out16, g2, w_router, tm_ffn, False)
        return _moe(n2, gate, h, wg16, wu16, wd16, tm_moe)

    q, gates, u, kvc_t, kvs_t, kvw_t, ks16, vsx, kw16, vwx = _proj(
        x_prompt, g1, w_row, w_kv, two(q_norm_g), kg, PROJ_TILE)
    m_p = _pool(u, pool_w16, ps)
    pages_p = s // PAGE
    pt_p = jnp.zeros((b, pages_p), jnp.int32)
    dummy_tail = jnp.zeros((b, KV_COLS, LANES), F32)
    kc_p, vc_p = _compress(pt_p, kvc_t, dummy_tail, pos_t, bd, two(k_norm_cmp_g),
                           npp=pages_p, has_tail=False, paged=False)
    kc_p = _pad_pages(kc_p, pages_p, PROMPT_NPP)
    vc_p = _pad_pages(vc_p, pages_p, PROMPT_NPP)
    a_p = _attn_prompt(q, gates, kc_p, vc_p, ks16, vsx, kw16, vwx, _expand_const(s // K_CHUNK, PROMPT_NPP),
                       _gate_place_const(), pages_p)
    y_p = ffn_moe(x_prompt.reshape(b * s, D_MODEL), a_p.reshape(b * s, ATTN_WIDTH),
                  m_p.reshape(b * s, POOL_WIDTH), PROJ_TILE, MOE_TILE, True)

    ts = db * ds
    q_s, gates_s, u_s, kvc_st, kvs_st, kvw_st, _, _, _, _ = _proj(
        x_sample.reshape(1, ts, D_MODEL), g1, w_row, w_kv, two(q_norm_g), kg, ts)
    u_s = u_s.reshape(db, ds, POOL_WIDTH)
    pool_ext = jnp.concatenate([state_pool, u_s], axis=1)
    n_ext = pool_ext.shape[1]
    lead = _round_up(n_ext, 8) - n_ext
    pool_in = jnp.concatenate([jnp.zeros((db, lead, POOL_WIDTH), F32), pool_ext], axis=1)
    m_s = _pool(pool_in, pool_w16, ps)[:, lead + n_ext - ds:]

    def new_rows_t(xt):
        x = jnp.transpose(xt.reshape(KV_COLS, db, ds), (1, 0, 2))
        return jnp.concatenate([x, jnp.zeros((db, KV_COLS, LANES - ds), F32)], axis=2)

    kvc_new, kvs_new, kvw_new = new_rows_t(kvc_st), new_rows_t(kvs_st), new_rows_t(kvw_st)
    n_pages = page_table.shape[1]
    npp_c = _round_up(n_pages + 1, 8)
    cmp_pages = _feature_major(cache_cmp_kv)
    slc_pages = _feature_major(cache_slc_kv)
    kc_s, vc_s = _compress(page_table, cmp_pages, kvc_new, pos_t, bd, two(k_norm_cmp_g),
                           npp=npp_c, has_tail=True, paged=True)
    kc_s = _pad_pages(kc_s, npp_c, SAMPLE_NPP)
    vc_s = _pad_pages(vc_s, npp_c, SAMPLE_NPP)
    win_t = _feature_major(state_win_kv)
    q_s3 = q_s.reshape(db, ds, ATTN_WIDTH)
    o_cmp_s, sel_s, need_s = _select_sample(q_s3, kc_s, vc_s, past=past)
    need = need_s[:, 0, :n_pages] > 0.5
    order = jnp.argsort(jnp.logical_not(need), axis=1, stable=True).astype(jnp.int32)
    n_need = jnp.sum(need, axis=1).astype(jnp.int32)
    a_s = _attn_sample(page_table, order, n_need, q_s3, gates_s.reshape(db, ds, GATE_PAD), o_cmp_s, sel_s,
                       kvs_new, kvw_new, win_t, slc_pages, past=past)
    y_s = ffn_moe(x_sample.reshape(ts, D_MODEL), a_s.reshape(ts, ATTN_WIDTH), m_s.reshape(ts, POOL_WIDTH), ts, ts, False)

    win_keep = min(WINDOW, s)
    sample5 = lambda xt: jnp.transpose(xt.reshape(2, N_KV_HEADS, HEAD_DIM, db, ds), (3, 4, 0, 1, 2))
    win_ctx_t = jnp.concatenate([win_t, kvw_new[:, :, :ds]], axis=2)
    return (y_p.reshape(b, s, D_MODEL), y_s.reshape(db, ds, D_MODEL),
            _token_major(kvc_t), sample5(kvc_st),
            _token_major(kvs_t), sample5(kvs_st),
            _token_major(kvw_t[:, :, s - win_keep:]), _token_major(win_ctx_t[:, :, ds:]),
            u[:, s - POOL_BUF:], pool_ext[:, ds:])
```

```python
import functools

import jax
import jax.numpy as jnp
import numpy as np
from jax import lax
from jax.experimental import pallas as pl
from jax.experimental.pallas import tpu as pltpu

F32 = jnp.float32
BF16 = jnp.bfloat16

D_MODEL = 1024
N_HEADS = 8
HEAD_DIM = 64
N_KV_HEADS = 2
Q_PER_KV = N_HEADS // N_KV_HEADS
ATTN_WIDTH = N_HEADS * HEAD_DIM
KV_COLS = 2 * N_KV_HEADS * HEAD_DIM
GATE_COLS = 3 * N_HEADS
POOL_WIDTH = D_MODEL - ATTN_WIDTH
POOL_WINDOWS = (2, 4, 8, 16)
POOL_GROUP_WIDTH = POOL_WIDTH // len(POOL_WINDOWS)
POOL_BUF = max(POOL_WINDOWS) - 1
CMP_BLOCK = 64
TOP_K_BLOCKS = 16
WINDOW = 512
FORCE_SCORE = 1.0e4
N_EXPERT_GROUPS = 4
EXPERTS_PER_GROUP = 4
N_EXPERTS = N_EXPERT_GROUPS * EXPERTS_PER_GROUP
D_EXPERT = 512
EPS = 1e-6
NEG = -1e30
SCALE = HEAD_DIM ** -0.5

LANES = 128
PAGE = 2 * CMP_BLOCK
GATE_PAD = LANES
ROW_COLS = ATTN_WIDTH + GATE_PAD + POOL_WIDTH
OFF_GATE = ATTN_WIDTH
OFF_U = OFF_GATE + GATE_PAD
KV_ROWS = 3 * KV_COLS
ROUTER_LANE0 = N_EXPERT_GROUPS
Q_TILE = 256
K_CHUNK = 256
VMEM_LIMIT = 56 * 1024 * 1024


def _slope(h):
    return float(2.0 ** (-8.0 * (h + 1) / N_HEADS))


def _half_group_norm(v, g):
    lane = lax.broadcasted_iota(jnp.int32, v.shape, 1)
    lo = lane < HEAD_DIM
    v2 = v * v
    s_lo = jnp.sum(jnp.where(lo, v2, 0.0), axis=-1, keepdims=True)
    s_hi = jnp.sum(jnp.where(lo, 0.0, v2), axis=-1, keepdims=True)
    r_lo = lax.rsqrt(s_lo * (1.0 / HEAD_DIM) + EPS)
    r_hi = lax.rsqrt(s_hi * (1.0 / HEAD_DIM) + EPS)
    return (v * jnp.where(lo, r_lo, r_hi)) * g


def _dot_t(a, b):
    return lax.dot_general(a, b, (((1,), (1,)), ((), ())), preferred_element_type=F32)


def _proj_kernel(x_ref, g1_ref, wr_ref, wkv_ref, qg_ref, kg_ref,
                 q_ref, gates_ref, u_ref, kvc_ref, kvs_ref, kvw_ref, ks16_ref, vsx_ref, kw16_ref, vwx_ref):
    x = x_ref[0]
    tm = x.shape[0]
    ms = jnp.mean(x * x, axis=-1, keepdims=True)
    n = ((x * lax.rsqrt(ms + EPS)) * g1_ref[...]).astype(BF16)
    p = jnp.dot(n, wr_ref[...], preferred_element_type=F32)
    pt = _dot_t(wkv_ref[...], n)
    for t in range(ATTN_WIDTH // LANES):
        sl = slice(t * LANES, (t + 1) * LANES)
        q_ref[0, :, sl] = _half_group_norm(p[:, sl], qg_ref[...]).astype(BF16)
    gates_ref[0] = 1.0 / (1.0 + jnp.exp(-p[:, OFF_GATE:OFF_GATE + GATE_PAD]))
    u_ref[0] = p[:, OFF_U:OFF_U + POOL_WIDTH]
    kvc_ref[0] = pt[:KV_COLS]
    half = KV_COLS // 2
    lane = lax.broadcasted_iota(jnp.int32, (tm, LANES), 1)
    for bi, (out32, k16, vx16) in enumerate(((kvs_ref, ks16_ref, vsx_ref), (kvw_ref, kw16_ref, vwx_ref))):
        off = (bi + 1) * KV_COLS
        heads = []
        for hh in range(N_KV_HEADS):
            kh = pt[off + hh * HEAD_DIM:off + (hh + 1) * HEAD_DIM]
            msk = jnp.mean(kh * kh, axis=0, keepdims=True)
            heads.append((kh * lax.rsqrt(msk + EPS)) * kg_ref[bi])
        kn = jnp.concatenate(heads, axis=0)
        v = pt[off + half:off + KV_COLS]
        out32[0, :half, :] = kn
        out32[0, half:, :] = v
        vt = v.T
        vx = (jnp.where(lane < HEAD_DIM, vt, 1.0), jnp.where(lane < HEAD_DIM, pltpu.roll(vt, HEAD_DIM, axis=1), 1.0))
        for c in range(tm // K_CHUNK):
            cs = slice(c * K_CHUNK, (c + 1) * K_CHUNK)
            k16[0, c] = kn[:, cs].astype(BF16)
            for hh in range(N_KV_HEADS):
                vx16[0, c, hh] = vx[hh][cs].astype(BF16)


def _proj(x3d, g1, w_row, w_kv, qg, kg, tm):
    b, s, _ = x3d.shape
    tok = lambda i, j: (i, j, 0)
    feat = lambda i, j: (i, 0, j)
    const2 = lambda i, j: (0, 0)
    nck = tm // K_CHUNK
    out_shape = (
        jax.ShapeDtypeStruct((b, s, ATTN_WIDTH), BF16),
        jax.ShapeDtypeStruct((b, s, GATE_PAD), F32),
        jax.ShapeDtypeStruct((b, s, POOL_WIDTH), F32),
        jax.ShapeDtypeStruct((b, KV_COLS, s), F32),
        jax.ShapeDtypeStruct((b, KV_COLS, s), F32),
        jax.ShapeDtypeStruct((b, KV_COLS, s), F32),
        jax.ShapeDtypeStruct((b, s // K_CHUNK, LANES, K_CHUNK), BF16),
        jax.ShapeDtypeStruct((b, s // K_CHUNK, N_KV_HEADS, K_CHUNK, LANES), BF16),
        jax.ShapeDtypeStruct((b, s // K_CHUNK, LANES, K_CHUNK), BF16),
        jax.ShapeDtypeStruct((b, s // K_CHUNK, N_KV_HEADS, K_CHUNK, LANES), BF16),
    )
    k_spec = pl.BlockSpec((1, nck, LANES, K_CHUNK), lambda i, j: (i, j, 0, 0))
    v_spec = pl.BlockSpec((1, nck, N_KV_HEADS, K_CHUNK, LANES), lambda i, j: (i, j, 0, 0, 0))
    out_specs = (
        pl.BlockSpec((1, tm, ATTN_WIDTH), tok),
        pl.BlockSpec((1, tm, GATE_PAD), tok),
        pl.BlockSpec((1, tm, POOL_WIDTH), tok),
        pl.BlockSpec((1, KV_COLS, tm), feat),
        pl.BlockSpec((1, KV_COLS, tm), feat),
        pl.BlockSpec((1, KV_COLS, tm), feat),
        k_spec, v_spec, k_spec, v_spec,
    )
    return pl.pallas_call(
        _proj_kernel,
        grid=(b, s // tm),
        in_specs=[
            pl.BlockSpec((1, tm, D_MODEL), tok),
            pl.BlockSpec((1, D_MODEL), const2),
            pl.BlockSpec((D_MODEL, ROW_COLS), const2),
            pl.BlockSpec((KV_ROWS, D_MODEL), const2),
            pl.BlockSpec((1, LANES), const2),
            pl.BlockSpec((2, HEAD_DIM, 1), lambda i, j: (0, 0, 0)),
        ],
        out_specs=out_specs,
        out_shape=out_shape,
        compiler_params=pltpu.CompilerParams(
            dimension_semantics=("arbitrary", "arbitrary"), vmem_limit_bytes=VMEM_LIMIT),
        name="proj",
    )(x3d, g1, w_row, w_kv, qg, kg)


def _pool_kernel(u_ref, pw_ref, ps_ref, m_ref):
    u = u_ref[0]
    n = u.shape[0]

    def shift(v, k):
        rolled = pltpu.roll(v, k, axis=0)
        r = lax.broadcasted_iota(jnp.int32, v.shape, 0)
        return jnp.where(r >= k, rolled, 0.0)

    sums = []
    s = u
    k = 1
    for gi in range(len(POOL_WINDOWS)):
        s = s + shift(s, k)
        k *= 2
        sums.append(s[:, :POOL_GROUP_WIDTH])
        s = s[:, POOL_GROUP_WIDTH:]
    row = lax.broadcasted_iota(jnp.int32, (n, 1), 0)
    for gi, w in enumerate(POOL_WINDOWS):
        sl = slice(gi * POOL_GROUP_WIDTH, (gi + 1) * POOL_GROUP_WIDTH)
        cnt = jnp.minimum(row + 1, w).astype(F32)
        d = sums[gi] / cnt - u[:, sl]
        y = jnp.dot(d.astype(BF16), pw_ref[gi], preferred_element_type=F32)
        m_ref[0, :, sl] = (y * ps_ref[:, sl]).astype(BF16)


def _pool(u3d, pool_w16, pool_scale):
    b, n, _ = u3d.shape
    return pl.pallas_call(
        _pool_kernel,
        grid=(b,),
        in_specs=[
            pl.BlockSpec((1, n, POOL_WIDTH), lambda i: (i, 0, 0)),
            pl.BlockSpec((len(POOL_WINDOWS), POOL_GROUP_WIDTH, POOL_GROUP_WIDTH), lambda i: (0, 0, 0)),
            pl.BlockSpec((1, POOL_WIDTH), lambda i: (0, 0)),
        ],
        out_specs=pl.BlockSpec((1, n, POOL_WIDTH), lambda i: (i, 0, 0)),
        out_shape=jax.ShapeDtypeStruct((b, n, POOL_WIDTH), BF16),
        compiler_params=pltpu.CompilerParams(
            dimension_semantics=("arbitrary",), vmem_limit_bytes=VMEM_LIMIT),
        name="pool",
    )(u3d, pool_w16, pool_scale)


def _compress_kernel(pt_ref, src_ref, tail_ref, pos_ref, bd_ref, g_ref, kc_ref, vc_ref, buf, sem,
                     *, n_pages, npp, has_tail, paged):
    b = pl.program_id(0)
    nb = pl.num_programs(0)
    n_slabs = 2 * N_KV_HEADS

    def slab_copy(row, p, c, slot):
        kv, kvh = divmod(c, N_KV_HEADS)
        if paged:
            src = src_ref.at[pt_ref[row, p], pl.ds(c * HEAD_DIM, HEAD_DIM), :]
        else:
            src = src_ref.at[row, pl.ds(c * HEAD_DIM, HEAD_DIM), pl.ds(p * PAGE, PAGE)]
        return pltpu.make_async_copy(src, buf.at[slot, kv, :, kvh * npp + p, :], sem.at[slot])

    def tail_copy(c, slot):
        kv, kvh = divmod(c, N_KV_HEADS)
        return pltpu.make_async_copy(tail_ref.at[0, pl.ds(c * HEAD_DIM, HEAD_DIM), :],
                                     buf.at[slot, kv, :, kvh * npp + n_pages, :], sem.at[slot])

    def row_copies(row, slot, fn):
        def body(p, carry):
            for c in range(n_slabs):
                fn(slab_copy(row, p, c, slot), c % 2)
            return carry
        lax.fori_loop(0, n_pages, body, 0)

    n_real = n_pages + (1 if has_tail else 0)

    @pl.when(b == 0)
    def _():
        if npp > n_real:
            for kvh in range(N_KV_HEADS):
                buf[:, :, :, kvh * npp + n_real:(kvh + 1) * npp, :] = jnp.zeros(
                    (2, 2, HEAD_DIM, npp - n_real, LANES), F32)
        row_copies(0, 0, lambda cp, pri: cp.start(priority=pri))

    slot = b % 2

    @pl.when(b + 1 < nb)
    def _():
        row_copies(b + 1, 1 - slot, lambda cp, pri: cp.start(priority=pri))

    if has_tail:
        for c in range(n_slabs):
            tail_copy(c, slot).start()
        for c in range(n_slabs):
            tail_copy(c, slot).wait()
    row_copies(b, slot, lambda cp, pri: cp.wait())

    rows = N_KV_HEADS * npp

    def body(d, acc):
        pos = pos_ref[pl.ds(d, 1), :]
        x = jnp.concatenate([buf[slot, 0, d] + pos, buf[slot, 1, d] + pos], axis=1).astype(BF16)
        return acc + jnp.dot(x, bd_ref[d], preferred_element_type=F32)

    acc = lax.fori_loop(0, HEAD_DIM, body, jnp.zeros((rows, 2 * LANES), F32), unroll=8)
    kc_ref[0] = _half_group_norm(acc[:, :LANES], g_ref[...])
    vc_ref[0] = acc[:, LANES:]


def _compress(page_table, src, tail, pos_t, bd, g2, *, npp, has_tail, paged):
    b, n_pages = page_table.shape
    kern = functools.partial(_compress_kernel, n_pages=n_pages, npp=npp, has_tail=has_tail, paged=paged)
    rows = N_KV_HEADS * npp
    grid_spec = pltpu.PrefetchScalarGridSpec(
        num_scalar_prefetch=1,
        grid=(b,),
        in_specs=[
            pl.BlockSpec(memory_space=pl.ANY),
            pl.BlockSpec((1, KV_COLS, LANES), lambda i, pt: (i, 0, 0)),
            pl.BlockSpec((HEAD_DIM, LANES), lambda i, pt: (0, 0)),
            pl.BlockSpec((HEAD_DIM, 2 * LANES, 2 * LANES), lambda i, pt: (0, 0, 0)),
            pl.BlockSpec((1, LANES), lambda i, pt: (0, 0)),
        ],
        out_specs=(pl.BlockSpec((1, rows, LANES), lambda i, pt: (i, 0, 0)),
                   pl.BlockSpec((1, rows, LANES), lambda i, pt: (i, 0, 0))),
        scratch_shapes=[
            pltpu.VMEM((2, 2, HEAD_DIM, rows, LANES), F32),
            pltpu.SemaphoreType.DMA((2,)),
        ],
    )
    return pl.pallas_call(
        kern,
        grid_spec=grid_spec,
        out_shape=(jax.ShapeDtypeStruct((b, rows, LANES), F32), jax.ShapeDtypeStruct((b, rows, LANES), F32)),
        compiler_params=pltpu.CompilerParams(
            dimension_semantics=("arbitrary",), vmem_limit_bytes=VMEM_LIMIT),
        name="compress",
    )(page_table, src, tail, pos_t, bd, g2)


def _to_half(tile, src_half, dst_half):
    lane = lax.broadcasted_iota(jnp.int32, tile.shape, 1)
    src = tile if src_half == dst_half else pltpu.roll(tile, HEAD_DIM, axis=1)
    keep = (lane < HEAD_DIM) if dst_half == 0 else (lane >= HEAD_DIM)
    return jnp.where(keep, src, 0.0)


def _pair_tile(o_even, o_odd, half):
    lane = lax.broadcasted_iota(jnp.int32, o_even.shape, 1)
    if half == 0:
        return jnp.where(lane < HEAD_DIM, o_even, pltpu.roll(o_odd, HEAD_DIM, axis=1))
    return jnp.where(lane < HEAD_DIM, pltpu.roll(o_even, HEAD_DIM, axis=1), o_odd)


def _gate_tile(gates, pair, c, shape):
    lane = lax.broadcasted_iota(jnp.int32, shape, 1)
    he, ho = 2 * pair, 2 * pair + 1
    return jnp.where(lane < HEAD_DIM, gates[:, 3 * he + c:3 * he + c + 1], gates[:, 3 * ho + c:3 * ho + c + 1])


def _block_of_col(col, npp):
    return 2 * (col % npp) + col // npp


def _cmp_operand(x, npp):
    return jnp.concatenate([_to_half(x, 0, 0), _to_half(x, 1, 0)], axis=0).astype(BF16)


def _rank_select(score_t, blk_t, cols):
    rank = jnp.zeros(score_t.shape, jnp.int32)
    for r, n in cols:
        row = score_t[r:r + 1, :]
        ahead = (row > score_t) | ((row == score_t) & (blk_t > n))
        rank = rank + ahead.astype(jnp.int32)
    return ((rank < TOP_K_BLOCKS) & (score_t > -0.5)).astype(F32)


POS_HI, POS_LO = HEAD_DIM, HEAD_DIM + 1
MASK_BIG = 1e30


def _attn_prompt_kernel(q_ref, gates_ref, kc_ref, vc_ref, ks_ref, vsx_ref, kw_ref, vwx_ref, exp_ref, gp_ref, a_ref,
                        zbuf, m_sc, acc_sc, *, n_pages, npp):
    i = pl.program_id(1)
    q0 = i * Q_TILE
    rows = Q_PER_KV * Q_TILE
    ncols = 2 * npp
    qpos = q0 + lax.broadcasted_iota(jnp.int32, (Q_TILE, 1), 0)
    blk = _block_of_col(lax.broadcasted_iota(jnp.int32, (Q_TILE, ncols), 1), npp)
    real_cols = [(half * n_pages + p, 2 * p + half) for half in range(2) for p in range(n_pages)]

    g = gates_ref[0]
    g_hi = g.astype(BF16)
    g_lo = (g - g_hi.astype(F32)).astype(BF16)
    gexp = (jnp.dot(g_hi, gp_ref[...], preferred_element_type=F32)
            + jnp.dot(g_lo, gp_ref[...], preferred_element_type=F32))

    r_key = lax.broadcasted_iota(jnp.int32, (Q_TILE, K_CHUNK), 1)
    r_qry = lax.broadcasted_iota(jnp.int32, (Q_TILE, K_CHUNK), 0)
    keep = {"causal": r_key <= r_qry, "lower": r_key > r_qry}

    all_rows = N_HEADS * Q_TILE

    def branch(lhs, k_ref, vx_ref, slc, chunks, c_lo, c_hi):
        m_sc[...] = jnp.full((all_rows, LANES), NEG, F32)

        def scores(c, kind):
            rr = lax.broadcasted_iota(jnp.int32, (HEAD_DIM, K_CHUNK), 0)
            tt = lax.broadcasted_iota(jnp.int32, (HEAD_DIM, K_CHUNK), 1).astype(F32)
            hi = ((c - i) * K_CHUNK).astype(F32)
            pos = jnp.where(rr == 0, hi, jnp.where(rr == 1, tt, 0.0)).astype(BF16)
            for kvh in range(N_KV_HEADS):
                parts = [k_ref[0, c, kvh * HEAD_DIM:(kvh + 1) * HEAD_DIM, :], pos]
                if slc:
                    parts.append(exp_ref[c])
                kx = jnp.concatenate(parts, axis=0)
                for gq in range(Q_PER_KV):
                    sl = slice((kvh * Q_PER_KV + gq) * Q_TILE, (kvh * Q_PER_KV + gq + 1) * Q_TILE)
                    zg = jnp.dot(lhs[sl], kx, preferred_element_type=F32)
                    if kind is not None:
                        zg = jnp.where(keep[kind], zg, NEG)
                    zbuf[c, sl, :] = zg
                    m_sc[sl] = jnp.maximum(m_sc[sl], jnp.maximum(zg[:, :LANES], zg[:, LANES:]))

        for lo, hi_, kind, cond in chunks:
            if kind is None and cond is None:
                def body(c, carry):
                    scores(c, None)
                    return carry
                lax.fori_loop(lo, hi_, body, 0)
            elif cond is None:
                scores(lo, kind)
            else:
                pl.when(cond)(functools.partial(scores, lo, kind))

        m = jnp.broadcast_to(jnp.max(m_sc[...], axis=-1, keepdims=True), (all_rows, LANES))
        acc_sc[...] = jnp.zeros((all_rows, LANES), F32)

        def pass2(c, carry):
            for h in range(N_HEADS):
                sl = slice(h * Q_TILE, (h + 1) * Q_TILE)
                z = zbuf[c, sl, :]
                e = jnp.concatenate([jnp.exp(z[:, :LANES] - m[sl]), jnp.exp(z[:, LANES:] - m[sl])], axis=1)
                acc_sc[sl] += jnp.dot(e.astype(BF16), vx_ref[0, c, h // Q_PER_KV], preferred_element_type=F32)
            return carry

        lax.fori_loop(c_lo, c_hi, pass2, 0)
        return acc_sc[...]

    def pair_norm(acc_e, acc_o):
        lane = lax.broadcasted_iota(jnp.int32, acc_e.shape, 1)
        safe = lambda d: jnp.where(d > 0.0, d, 1.0)
        sw_e = pltpu.roll(acc_e, HEAD_DIM, axis=1)
        sw_o = pltpu.roll(acc_o, HEAD_DIM, axis=1)
        return jnp.where(lane < HEAD_DIM, acc_e / safe(sw_e), sw_o / safe(acc_o))

    o_cmp_all, lhs_slc_all, lhs_win_all = [], [], []
    for kvh in range(N_KV_HEADS):
        slopes = [_slope(kvh * Q_PER_KV + g) for g in range(Q_PER_KV)]
        lane = lax.broadcasted_iota(jnp.int32, (Q_TILE, LANES), 1)
        pieces, pieces_x = [], []
        for gq in range(Q_PER_KV):
            h = kvh * Q_PER_KV + gq
            tile = q_ref[0, :, (h // 2) * LANES:(h // 2 + 1) * LANES].astype(F32) * SCALE
            low = _to_half(tile, h % 2, 0)
            pieces.append(low)
            pieces_x.append(jnp.where((lane == POS_HI) | (lane == POS_LO), slopes[gq], low))
        qk = jnp.concatenate(pieces, axis=0).astype(BF16)
        qk_x = jnp.concatenate(pieces_x, axis=0).astype(BF16)

        kc = _cmp_operand(kc_ref[0, kvh], npp)
        vc = _cmp_operand(vc_ref[0, kvh], npp)
        s = _dot_t(qk, kc)
        dist_c = qpos - (blk * CMP_BLOCK + (CMP_BLOCK - 1))
        mask_c = dist_c >= 0
        dist_cf = dist_c.astype(F32)
        p_list = []
        p_kv = jnp.zeros((Q_TILE, ncols), F32)
        for g in range(Q_PER_KV):
            z = jnp.where(mask_c, s[g * Q_TILE:(g + 1) * Q_TILE] - slopes[g] * dist_cf, NEG)
            mx = jnp.max(z, axis=-1, keepdims=True)
            e = jnp.where(mask_c, jnp.exp(z - mx), 0.0)
            den = jnp.sum(e, axis=-1, keepdims=True)
            p = e / jnp.where(den > 0.0, den, 1.0)
            p_kv = p_kv + p
            p_list.append(p.astype(BF16))
        o_cmp = jnp.dot(jnp.concatenate(p_list, axis=0), vc, preferred_element_type=F32)

        cand = blk * CMP_BLOCK <= qpos
        forced = (blk == qpos // CMP_BLOCK) | (blk == 0)
        score = jnp.where(forced, FORCE_SCORE, jnp.where(cand, p_kv, -1.0))
        score_t = score.T
        st = jnp.concatenate([score_t[:n_pages], score_t[npp:npp + n_pages]], axis=0)
        rr = lax.broadcasted_iota(jnp.int32, (2 * n_pages, Q_TILE), 0)
        bt = jnp.where(rr < n_pages, 2 * rr, 2 * (rr - n_pages) + 1)
        sel_s = _rank_select(st, bt, [(r, n) for r, (_, n) in enumerate(real_cols)])
        pad = jnp.zeros((npp - n_pages, Q_TILE), F32)
        sel_t = jnp.concatenate([sel_s[:n_pages], pad, sel_s[n_pages:], pad], axis=0)
        sel = sel_t.T
        sel_bias = jnp.where(sel > 0.5, 0.0, -MASK_BIG).astype(BF16)
        lhs_slc_all.append(jnp.concatenate([qk_x, jnp.concatenate([sel_bias] * Q_PER_KV, axis=0)], axis=1))
        lhs_win_all.append(qk_x)
        o_cmp_all.append(o_cmp)

    o_slc = branch(jnp.concatenate(lhs_slc_all, axis=0), ks_ref, vsx_ref, True,
                   [(0, i, None, None), (i, None, "causal", None)], 0, i + 1)
    o_win = branch(jnp.concatenate(lhs_win_all, axis=0), kw_ref, vwx_ref, False,
                   [(i - 2, None, "lower", i >= 2), (i - 1, None, None, i >= 1), (i, None, "causal", None)],
                   jnp.maximum(i - 2, 0), i + 1)
    o_cmp = jnp.concatenate(o_cmp_all, axis=0)

    for pair in range(N_HEADS // 2):
        e_sl = slice(2 * pair * Q_TILE, (2 * pair + 1) * Q_TILE)
        o_sl = slice((2 * pair + 1) * Q_TILE, (2 * pair + 2) * Q_TILE)
        col = pair * LANES
        tile = (gexp[:, col:col + LANES] * _pair_tile(o_cmp[e_sl], o_cmp[o_sl], 0)
                + gexp[:, ATTN_WIDTH + col:ATTN_WIDTH + col + LANES] * pair_norm(o_slc[e_sl], o_slc[o_sl])
                + gexp[:, 2 * ATTN_WIDTH + col:2 * ATTN_WIDTH + col + LANES] * pair_norm(o_win[e_sl], o_win[o_sl]))
        a_ref[0, :, pair * LANES:(pair + 1) * LANES] = tile.astype(BF16)


def _attn_prompt(q, gates, kc, vc, ks16, vsx, kw16, vwx, expand, gate_place, n_pages):
    b, s, _ = q.shape
    npp = kc.shape[2]
    n_chunks = s // K_CHUNK
    rows = Q_PER_KV * Q_TILE
    assert WINDOW == 2 * K_CHUNK and Q_TILE == K_CHUNK
    kern = functools.partial(_attn_prompt_kernel, n_pages=n_pages, npp=npp)
    per_b4 = lambda bi, i: (bi, 0, 0, 0)
    per_b5 = lambda bi, i: (bi, 0, 0, 0, 0)
    return pl.pallas_call(
        kern,
        grid=(b, s // Q_TILE),
        in_specs=[
            pl.BlockSpec((1, Q_TILE, ATTN_WIDTH), lambda bi, i: (bi, i, 0)),
            pl.BlockSpec((1, Q_TILE, GATE_PAD), lambda bi, i: (bi, i, 0)),
            pl.BlockSpec((1, N_KV_HEADS, npp, LANES), per_b4),
            pl.BlockSpec((1, N_KV_HEADS, npp, LANES), per_b4),
            pl.BlockSpec((1, n_chunks, LANES, K_CHUNK), per_b4),
            pl.BlockSpec((1, n_chunks, N_KV_HEADS, K_CHUNK, LANES), per_b5),
            pl.BlockSpec((1, n_chunks, LANES, K_CHUNK), per_b4),
            pl.BlockSpec((1, n_chunks, N_KV_HEADS, K_CHUNK, LANES), per_b5),
            pl.BlockSpec((n_chunks, 2 * npp, K_CHUNK), lambda bi, i: (0, 0, 0)),
            pl.BlockSpec((GATE_PAD, 3 * ATTN_WIDTH), lambda bi, i: (0, 0)),
        ],
        out_specs=pl.BlockSpec((1, Q_TILE, ATTN_WIDTH), lambda bi, i: (bi, i, 0)),
        out_shape=jax.ShapeDtypeStruct((b, s, ATTN_WIDTH), BF16),
        scratch_shapes=[
            pltpu.VMEM((n_chunks, N_KV_HEADS * rows, K_CHUNK), F32),
            pltpu.VMEM((N_KV_HEADS * rows, LANES), F32),
            pltpu.VMEM((N_KV_HEADS * rows, LANES), F32),
        ],
        compiler_params=pltpu.CompilerParams(
            dimension_semantics=("arbitrary", "arbitrary"), vmem_limit_bytes=VMEM_LIMIT),
        name="attn_prompt",
    )(q, gates, kc, vc, ks16, vsx, kw16, vwx, expand, gate_place)


S_ROWS = LANES
S_CHUNK = 1024


def _sample_rows(q_ref, ds, past):
    n_real = N_HEADS * ds
    by_kvh, low = [], []
    for h in range(N_HEADS):
        tile = q_ref[0, :, (h // 2) * LANES:(h // 2 + 1) * LANES].astype(F32) * SCALE
        by_kvh.append(_to_half(tile, h % 2, h // Q_PER_KV))
        low.append(_to_half(tile, h % 2, 0))
    zpad = jnp.zeros((S_ROWS - n_real, LANES), F32)
    qrows = jnp.concatenate(by_kvh + [zpad], axis=0).astype(BF16)
    qlow = jnp.concatenate(low + [zpad], axis=0).astype(BF16)
    r_col = lax.broadcasted_iota(jnp.int32, (S_ROWS, 1), 0)
    q_of_r = r_col % ds
    h_of_r = r_col // ds
    sl_r = jnp.zeros((S_ROWS, 1), F32)
    for h in range(N_HEADS):
        sl_r = jnp.where(h_of_r == h, _slope(h), sl_r)
    return qrows, qlow, q_of_r, past + q_of_r, sl_r, h_of_r // Q_PER_KV


def _select_sample_kernel(q_ref, kc_ref, vc_ref, ocmp_ref, sel_ref, need_ref, *, past, ds, npp):
    n_real = N_HEADS * ds
    ncols = 2 * npp
    n_blocks = (past + ds + CMP_BLOCK - 1) // CMP_BLOCK
    _, qlow, _, qp_r, sl_r, kvh_r = _sample_rows(q_ref, ds, past)

    blk = _block_of_col(lax.broadcasted_iota(jnp.int32, (S_ROWS, ncols), 1), npp)
    s = jnp.where(kvh_r == 0, _dot_t(qlow, _cmp_operand(kc_ref[0, 0], npp)),
                  _dot_t(qlow, _cmp_operand(kc_ref[0, 1], npp)))
    dist_c = qp_r - (blk * CMP_BLOCK + (CMP_BLOCK - 1))
    mask_c = dist_c >= 0
    z = jnp.where(mask_c, s - sl_r * dist_c.astype(F32), NEG)
    mx = jnp.max(z, axis=-1, keepdims=True)
    e = jnp.where(mask_c, jnp.exp(z - mx), 0.0)
    den = jnp.sum(e, axis=-1, keepdims=True)
    p = e / jnp.where(den > 0.0, den, 1.0)
    pb = p.astype(BF16)
    o_cmp = jnp.where(kvh_r == 0, jnp.dot(pb, _cmp_operand(vc_ref[0, 0], npp), preferred_element_type=F32),
                      jnp.dot(pb, _cmp_operand(vc_ref[0, 1], npp), preferred_element_type=F32))

    kq = N_KV_HEADS * ds
    p_kv = []
    for kvh in range(N_KV_HEADS):
        acc = jnp.zeros((ds, ncols), F32)
        for g in range(Q_PER_KV):
            r0 = (kvh * Q_PER_KV + g) * ds
            acc = acc + p[r0:r0 + ds]
        p_kv.append(acc)
    p_kv = jnp.concatenate(p_kv, axis=0)
    blk2 = _block_of_col(lax.broadcasted_iota(jnp.int32, (kq, ncols), 1), npp)
    qp2 = past + lax.broadcasted_iota(jnp.int32, (kq, 1), 0) % ds
    cand = blk2 * CMP_BLOCK <= qp2
    forced = (blk2 == qp2 // CMP_BLOCK) | (blk2 == 0)
    score = jnp.where(forced, FORCE_SCORE, jnp.where(cand, p_kv, -1.0))
    rank = jnp.zeros((kq, ncols), jnp.int32)
    for n in range(n_blocks):
        c = (n % 2) * npp + n // 2
        col = score[:, c:c + 1]
        ahead = (col > score) | ((col == score) & (blk2 > n))
        rank = rank + ahead.astype(jnp.int32)
    sel2 = ((rank < TOP_K_BLOCKS) & (score > -0.5)).astype(F32)
    sel_rows = []
    for h in range(N_HEADS):
        kvh = h // Q_PER_KV
        sel_rows.append(sel2[kvh * ds:(kvh + 1) * ds])
    sel_rows.append(jnp.zeros((S_ROWS - n_real, ncols), F32))
    sel_rows = jnp.concatenate(sel_rows, axis=0)
    ocmp_ref[0] = o_cmp
    sel_ref[0] = sel_rows
    any_row = jnp.max(sel_rows, axis=0, keepdims=True)
    page_need = jnp.maximum(any_row[:, :npp], any_row[:, npp:])
    need_ref[0] = jnp.broadcast_to(page_need, (8, npp))


def _select_sample(q, kc, vc, *, past):
    b, ds, _ = q.shape
    npp = kc.shape[2]
    kern = functools.partial(_select_sample_kernel, past=past, ds=ds, npp=npp)
    per_b3 = lambda i: (i, 0, 0)
    per_b4 = lambda i: (i, 0, 0, 0)
    return pl.pallas_call(
        kern,
        grid=(b,),
        in_specs=[pl.BlockSpec((1, ds, ATTN_WIDTH), per_b3),
                  pl.BlockSpec((1, N_KV_HEADS, npp, LANES), per_b4),
                  pl.BlockSpec((1, N_KV_HEADS, npp, LANES), per_b4)],
        out_specs=(pl.BlockSpec((1, S_ROWS, LANES), per_b3), pl.BlockSpec((1, S_ROWS, 2 * npp), per_b3),
                   pl.BlockSpec((1, 8, npp), per_b3)),
        out_shape=(jax.ShapeDtypeStruct((b, S_ROWS, LANES), F32), jax.ShapeDtypeStruct((b, S_ROWS, 2 * npp), F32),
                   jax.ShapeDtypeStruct((b, 8, npp), F32)),
        compiler_params=pltpu.CompilerParams(dimension_semantics=("arbitrary",), vmem_limit_bytes=VMEM_LIMIT),
        name="select_sample",
    )(q, kc, vc)


def _attn_sample_kernel(pt_ref, order_ref, nn_ref, q_ref, gates_ref, ocmp_ref, sel_ref, kvs_new_ref, kvw_new_ref,
                        win_ref, cache_ref, a_ref, kbuf, zbuf, sem, *, n_pages, past, ds, npp):
    b = pl.program_id(0)
    nb = pl.num_programs(0)
    ncols = 2 * npp
    half_cols = KV_COLS // 2
    pages_per_chunk = S_CHUNK // PAGE

    def page_copy(row, j, slot):
        return pltpu.make_async_copy(
            cache_ref.at[pt_ref[row, order_ref[row, j]]], kbuf.at[slot, :, pl.ds(j * PAGE, PAGE)], sem.at[slot])

    def row_copies(row, slot, fn):
        def body(j, carry):
            fn(page_copy(row, j, slot))
            return carry
        lax.fori_loop(0, nn_ref[row], body, 0)

    @pl.when(b == 0)
    def _():
        kbuf[...] = jnp.zeros(kbuf.shape, F32)
        row_copies(0, 0, lambda cp: cp.start())

    slot = b % 2

    @pl.when(b + 1 < nb)
    def _():
        row_copies(b + 1, 1 - slot, lambda cp: cp.start())

    qrows, _, q_of_r, qp_r, sl_r, _ = _sample_rows(q_ref, ds, past)
    o_cmp = ocmp_ref[0]
    sel_rows = sel_ref[0]
    sel16 = sel_rows.astype(BF16)

    new_lane = lax.broadcasted_iota(jnp.int32, (1, LANES), 1)

    def new_scores(ref, extra_mask):
        k_new = ref[0, :half_cols, :].astype(BF16)
        s_new = jnp.dot(qrows, k_new, preferred_element_type=F32)
        dist = q_of_r - new_lane
        mask = (new_lane < ds) & (dist >= 0) & extra_mask
        return jnp.where(mask, s_new - sl_r * dist.astype(F32), NEG)

    def weighted_new(ref, e_new):
        return _dot_t(e_new.astype(BF16), ref[0, half_cols:, :].astype(BF16))

    row_copies(b, slot, lambda cp: cp.wait())
    n_need = nn_ref[b]
    n_chunks = (n_need + pages_per_chunk - 1) // pages_per_chunk

    def pass1(c, mrow):
        k0 = pl.multiple_of(c * S_CHUNK, S_CHUNK)
        kt = kbuf[slot, :half_cols, pl.ds(k0, S_CHUNK)].astype(BF16)
        st = jnp.dot(qrows, kt, preferred_element_type=F32)
        lane = lax.broadcasted_iota(jnp.int32, (1, S_CHUNK), 1)
        tpos = jnp.full((1, S_CHUNK), -PAGE * n_pages, jnp.int32)
        for jj in range(pages_per_chunk):
            j = c * pages_per_chunk + jj
            page = order_ref[b, jnp.minimum(j, n_pages - 1)]
            base = jnp.where(j < n_need, page * PAGE, -PAGE * n_pages) - jj * PAGE
            tpos = jnp.where(lane // PAGE == jj, base + lane, tpos)
        blk_c = _block_of_col(lax.broadcasted_iota(jnp.int32, (ncols, S_CHUNK), 0), npp)
        expand = (blk_c == jnp.broadcast_to(tpos, (ncols, S_CHUNK)) // CMP_BLOCK).astype(BF16)
        chosen = jnp.dot(sel16, expand, preferred_element_type=F32) > 0.5
        dist = qp_r - tpos
        zc = jnp.where(chosen & (dist >= 0), st - sl_r * dist.astype(F32), NEG)
        zbuf[c] = zc
        return jnp.maximum(mrow, jnp.max(zc, axis=-1, keepdims=True))

    mrow = lax.fori_loop(0, n_chunks, pass1, jnp.full((S_ROWS, 1), NEG, F32))
    c_last = ((past // CMP_BLOCK) % 2) * npp + (past // CMP_BLOCK) // 2
    z_new = new_scores(kvs_new_ref, sel_rows[:, c_last:c_last + 1] > 0.5)
    m_r = jnp.maximum(mrow, jnp.max(z_new, axis=-1, keepdims=True))

    def pass2(c, carry):
        acc, den_r = carry
        k0 = pl.multiple_of(c * S_CHUNK, S_CHUNK)
        zc = zbuf[c]
        ec = jnp.where(zc > 0.5 * NEG, jnp.exp(zc - m_r), 0.0)
        vt = kbuf[slot, half_cols:, pl.ds(k0, S_CHUNK)].astype(BF16)
        return acc + _dot_t(ec.astype(BF16), vt), den_r + jnp.sum(ec, axis=-1, keepdims=True)

    e_new = jnp.where(z_new > 0.5 * NEG, jnp.exp(z_new - m_r), 0.0)
    acc, den_r = lax.fori_loop(0, n_chunks, pass2,
                               (weighted_new(kvs_new_ref, e_new), jnp.sum(e_new, axis=-1, keepdims=True)))
    o_slc = acc / jnp.where(den_r > 0.0, den_r, 1.0)

    win_buf = win_ref.shape[2]
    kw = win_ref[0, :half_cols, :].astype(BF16)
    st = jnp.dot(qrows, kw, preferred_element_type=F32)
    kpos = past - win_buf + lax.broadcasted_iota(jnp.int32, (1, win_buf), 1)
    dist = qp_r - kpos
    mask = (dist >= 0) & (dist < WINDOW) & (kpos >= 0)
    z_w = jnp.where(mask, st - sl_r * dist.astype(F32), NEG)
    zw_new = new_scores(kvw_new_ref, True)
    m_w = jnp.maximum(jnp.max(z_w, axis=-1, keepdims=True), jnp.max(zw_new, axis=-1, keepdims=True))
    e_w = jnp.where(z_w > 0.5 * NEG, jnp.exp(z_w - m_w), 0.0)
    ew_new = jnp.where(zw_new > 0.5 * NEG, jnp.exp(zw_new - m_w), 0.0)
    acc_w = _dot_t(e_w.astype(BF16), win_ref[0, half_cols:, :].astype(BF16)) + weighted_new(kvw_new_ref, ew_new)
    den_w = jnp.sum(e_w, axis=-1, keepdims=True) + jnp.sum(ew_new, axis=-1, keepdims=True)
    o_win = acc_w / jnp.where(den_w > 0.0, den_w, 1.0)

    gates = gates_ref[0]
    for pair in range(N_HEADS // 2):
        kvh = (2 * pair) // Q_PER_KV
        e_sl = slice(2 * pair * ds, (2 * pair + 1) * ds)
        o_sl = slice((2 * pair + 1) * ds, (2 * pair + 2) * ds)
        shape = (ds, LANES)
        tile = (_gate_tile(gates, pair, 0, shape) * _pair_tile(o_cmp[e_sl], o_cmp[o_sl], 0)
                + _gate_tile(gates, pair, 1, shape) * _pair_tile(o_slc[e_sl], o_slc[o_sl], kvh)
                + _gate_tile(gates, pair, 2, shape) * _pair_tile(o_win[e_sl], o_win[o_sl], kvh))
        a_ref[0, :, pair * LANES:(pair + 1) * LANES] = tile.astype(BF16)


def _attn_sample(page_table, order, n_need, q, gates, o_cmp, sel_rows, kvs_new, kvw_new, state_win_t, cache_pages,
                 *, past):
    b, n_pages = page_table.shape
    ds = q.shape[1]
    npp = sel_rows.shape[2] // 2
    win_buf = state_win_t.shape[2]
    n_keys = n_pages * PAGE
    kern = functools.partial(_attn_sample_kernel, n_pages=n_pages, past=past, ds=ds, npp=npp)
    per_b = lambda i, pt, od, nn: (i, 0, 0)
    grid_spec = pltpu.PrefetchScalarGridSpec(
        num_scalar_prefetch=3,
        grid=(b,),
        in_specs=[
            pl.BlockSpec((1, ds, ATTN_WIDTH), per_b),
            pl.BlockSpec((1, ds, GATE_PAD), per_b),
            pl.BlockSpec((1, S_ROWS, LANES), per_b),
            pl.BlockSpec((1, S_ROWS, 2 * npp), per_b),
            pl.BlockSpec((1, KV_COLS, LANES), per_b),
            pl.BlockSpec((1, KV_COLS, LANES), per_b),
            pl.BlockSpec((1, KV_COLS, win_buf), per_b),
            pl.BlockSpec(memory_space=pl.ANY),
        ],
        out_specs=pl.BlockSpec((1, ds, ATTN_WIDTH), per_b),
        scratch_shapes=[
            pltpu.VMEM((2, KV_COLS, n_keys), F32),
            pltpu.VMEM((n_keys // S_CHUNK, S_ROWS, S_CHUNK), F32),
            pltpu.SemaphoreType.DMA((2,)),
        ],
    )
    return pl.pallas_call(
        kern,
        grid_spec=grid_spec,
        out_shape=jax.ShapeDtypeStruct((b, ds, ATTN_WIDTH), BF16),
        compiler_params=pltpu.CompilerParams(
            dimension_semantics=("arbitrary",), vmem_limit_bytes=VMEM_LIMIT),
        name="attn_sample",
    )(page_table, order, n_need, q, gates, o_cmp, sel_rows, kvs_new, kvw_new, state_win_t, cache_pages)


X_TILES = D_MODEL // LANES
PAY_SUB = X_TILES + 1
GROUP_LANE = 0


def _rows_to_tiles(ref, lo, val):
    for k in range(val.shape[1] // LANES):
        ref[:, lo + k, :] = val[:, k * LANES:(k + 1) * LANES]


def _tiles_to_rows(ref, lo, n):
    return jnp.concatenate([ref[:, lo + k, :] for k in range(n)], axis=1)


def _ffn_kernel(x_ref, a_ref, m_ref, wo_ref, g2_ref, wr_ref, *out_refs, packed):
    h = (x_ref[...]
         + jnp.dot(a_ref[...], wo_ref[:ATTN_WIDTH, :], preferred_element_type=F32)
         + jnp.dot(m_ref[...], wo_ref[ATTN_WIDTH:, :], preferred_element_type=F32))
    ms = jnp.mean(h * h, axis=-1, keepdims=True)
    n2f = (h * lax.rsqrt(ms + EPS)) * g2_ref[...]
    n2 = n2f.astype(BF16)
    n2_lo = (n2f - n2.astype(F32)).astype(BF16)
    logits = (jnp.dot(n2, wr_ref[0], preferred_element_type=F32)
              + (jnp.dot(n2, wr_ref[1], preferred_element_type=F32)
                 + jnp.dot(n2_lo, wr_ref[0], preferred_element_type=F32)))
    lane = lax.broadcasted_iota(jnp.int32, logits.shape, 1)
    big = jnp.int32(LANES)

    def masked_softmax(mask):
        zz = jnp.where(mask, logits, NEG)
        mx = jnp.max(zz, axis=-1, keepdims=True)
        ee = jnp.where(mask, jnp.exp(zz - mx), 0.0)
        return ee / jnp.sum(ee, axis=-1, keepdims=True)

    def first_argmax(vals, mask):
        v = jnp.max(jnp.where(mask, vals, -1.0), axis=-1, keepdims=True)
        idx = jnp.min(jnp.where(mask & (vals == v), lane, big), axis=-1, keepdims=True)
        return v, idx

    is_g = lane < N_EXPERT_GROUPS
    pg = masked_softmax(is_g)
    g_val, g_idx = first_argmax(pg, is_g)
    e_lane = lane - ROUTER_LANE0
    in_grp = (e_lane >= 0) & (e_lane < N_EXPERTS) & (e_lane // EXPERTS_PER_GROUP == g_idx)
    pe = masked_softmax(in_grp)
    v1, i1 = first_argmax(pe, in_grp)
    rest = in_grp & (lane != i1)
    v2, i2 = first_argmax(pe, rest)
    scale = g_val / (v1 + v2)
    gate = jnp.where(lane == i1, v1 * scale, jnp.where(lane == i2, v2 * scale, 0.0))
    gate = jnp.where(lane == GROUP_LANE, g_idx.astype(F32), gate)
    if packed:
        h_ref, pay_ref, gate_ref = out_refs
        h_ref[...] = h
        gate_ref[...] = gate
        _rows_to_tiles(pay_ref, 0, n2f)
        pay_ref[:, X_TILES, :] = gate
    else:
        h_ref, n2_ref, gate_ref = out_refs
        h_ref[...] = h
        n2_ref[...] = n2
        gate_ref[...] = gate


def _ffn(x2d, a2d, m2d, w_out16, g2, w_router16, tm, packed):
    t = x2d.shape[0]
    row = lambda i: (i, 0)
    const = lambda i: (0, 0)
    if packed:
        out_specs = (pl.BlockSpec((tm, D_MODEL), row), pl.BlockSpec((tm, PAY_SUB, LANES), lambda i: (i, 0, 0)),
                     pl.BlockSpec((tm, LANES), row))
        out_shape = (jax.ShapeDtypeStruct((t, D_MODEL), F32), jax.ShapeDtypeStruct((t, PAY_SUB, LANES), F32),
                     jax.ShapeDtypeStruct((t, LANES), F32))
    else:
        out_specs = (pl.BlockSpec((tm, D_MODEL), row), pl.BlockSpec((tm, D_MODEL), row),
                     pl.BlockSpec((tm, LANES), row))
        out_shape = (jax.ShapeDtypeStruct((t, D_MODEL), F32), jax.ShapeDtypeStruct((t, D_MODEL), BF16),
                     jax.ShapeDtypeStruct((t, LANES), F32))
    return pl.pallas_call(
        functools.partial(_ffn_kernel, packed=packed),
        grid=(t // tm,),
        in_specs=[
            pl.BlockSpec((tm, D_MODEL), row),
            pl.BlockSpec((tm, ATTN_WIDTH), row),
            pl.BlockSpec((tm, POOL_WIDTH), row),
            pl.BlockSpec((D_MODEL, D_MODEL), const),
            pl.BlockSpec((1, D_MODEL), const),
            pl.BlockSpec((2, D_MODEL, LANES), lambda i: (0, 0, 0)),
        ],
        out_specs=out_specs,
        out_shape=out_shape,
        compiler_params=pltpu.CompilerParams(
            dimension_semantics=("arbitrary",), vmem_limit_bytes=VMEM_LIMIT),
        name="ffn",
    )(x2d, a2d, m2d, w_out16, g2, w_router16)


MOE_ROWS = 512
ROW_CHUNK = 512
PLAN_TILE = 1024


def _plan_kernel(gate_ref, tri_ref, out_ref, cnt_ref, carry):
    @pl.when(pl.program_id(0) == 0)
    def _():
        carry[...] = jnp.zeros(carry.shape, F32)

    gl = gate_ref[...]
    lane = lax.broadcasted_iota(jnp.int32, gl.shape, 1)
    gid = gl[:, GROUP_LANE:GROUP_LANE + 1]
    onehot = jnp.where((lane < N_EXPERT_GROUPS) & (lane.astype(F32) == gid), 1.0, 0.0)
    before = jnp.dot(tri_ref[...], onehot.astype(BF16), preferred_element_type=F32) + carry[0:1, :]
    rank = jnp.sum(onehot * before, axis=-1, keepdims=True)
    out_ref[...] = jnp.where(lane == 0, rank, jnp.where(lane == 1, gid, 0.0))
    carry[0:1, :] = carry[0:1, :] + jnp.sum(onehot, axis=0, keepdims=True)
    cnt_ref[...] = carry[...]


def _plan(gate):
    t = gate.shape[0]
    tri = jnp.asarray(np.tril(np.ones((PLAN_TILE, PLAN_TILE), np.float32), -1), dtype=BF16)
    return pl.pallas_call(
        _plan_kernel,
        grid=(t // PLAN_TILE,),
        in_specs=[pl.BlockSpec((PLAN_TILE, LANES), lambda i: (i, 0)),
                  pl.BlockSpec((PLAN_TILE, PLAN_TILE), lambda i: (0, 0))],
        out_specs=(pl.BlockSpec((PLAN_TILE, LANES), lambda i: (i, 0)), pl.BlockSpec((8, LANES), lambda i: (0, 0))),
        out_shape=(jax.ShapeDtypeStruct((t, LANES), F32), jax.ShapeDtypeStruct((8, LANES), F32)),
        scratch_shapes=[pltpu.VMEM((8, LANES), F32)],
        compiler_params=pltpu.CompilerParams(
            dimension_semantics=("arbitrary",), vmem_limit_bytes=VMEM_LIMIT),
        name="moe_plan",
    )(gate, tri)


def _scatter_kernel(meta_ref, slot_ref, pay_ref, xs_ref, zrow, sem, zsem):
    c = pl.program_id(0)

    @pl.when(c == 0)
    def _():
        zrow[...] = jnp.zeros(zrow.shape, F32)
        for g in range(N_EXPERT_GROUPS + 1):
            lo = meta_ref[g]
            hi = meta_ref[N_EXPERT_GROUPS + 1 + g]

            def zero_start(r, carry):
                pltpu.make_async_copy(zrow.at[pl.ds(0, 1)], xs_ref.at[pl.ds(r, 1)], zsem.at[0]).start()
                return carry

            def zero_wait(r, carry):
                pltpu.make_async_copy(zrow.at[pl.ds(0, 1)], xs_ref.at[pl.ds(0, 1)], zsem.at[0]).wait()
                return carry

            lax.fori_loop(lo, hi, zero_start, 0)
            lax.fori_loop(lo, hi, zero_wait, 0)

    def issue(v, carry):
        for k in range(2):
            u = 2 * v + k
            pltpu.make_async_copy(pay_ref.at[pl.ds(u, 1)], xs_ref.at[pl.ds(slot_ref[0, 0, u], 1)],
                                  sem.at[0]).start(priority=k)
        return carry

    lax.fori_loop(0, ROW_CHUNK // 2, issue, 0, unroll=4)
    pltpu.make_async_copy(pay_ref, xs_ref.at[pl.ds(0, ROW_CHUNK)], sem.at[0]).wait()


def _scatter(meta, slot3, pay, n_slots):
    t = pay.shape[0]
    return pl.pallas_call(
        _scatter_kernel,
        grid_spec=pltpu.PrefetchScalarGridSpec(
            num_scalar_prefetch=1,
            grid=(t // ROW_CHUNK,),
            in_specs=[pl.BlockSpec((1, 1, ROW_CHUNK), lambda c, meta: (c, 0, 0), memory_space=pltpu.SMEM),
                      pl.BlockSpec((ROW_CHUNK, PAY_SUB, LANES), lambda c, meta: (c, 0, 0))],
            out_specs=pl.BlockSpec(memory_space=pl.ANY),
            scratch_shapes=[pltpu.VMEM((1, PAY_SUB, LANES), F32), pltpu.SemaphoreType.DMA((1,)),
                            pltpu.SemaphoreType.DMA((1,))],
        ),
        out_shape=jax.ShapeDtypeStruct((n_slots, PAY_SUB, LANES), F32),
        compiler_params=pltpu.CompilerParams(dimension_semantics=("arbitrary",), vmem_limit_bytes=VMEM_LIMIT),
        name="moe_scatter",
    )(meta, slot3, pay)


def _gather_kernel(slot_ref, nslot_ref, h_ref, ys_ref, y_ref, buf, sem):
    c = pl.program_id(0)
    nc = pl.num_programs(0)

    def fetch(ref, slot):
        def issue(v, carry):
            for k in range(2):
                u = 2 * v + k
                pltpu.make_async_copy(ys_ref.at[pl.ds(ref[0, 0, u], 1)], buf.at[slot, pl.ds(u, 1)],
                                      sem.at[slot]).start(priority=k)
            return carry
        lax.fori_loop(0, ROW_CHUNK // 2, issue, 0, unroll=4)

    @pl.when(c == 0)
    def _():
        fetch(slot_ref, 0)

    @pl.when(c + 1 < nc)
    def _():
        fetch(nslot_ref, (c + 1) % 2)

    cur = c % 2
    pltpu.make_async_copy(ys_ref.at[pl.ds(0, ROW_CHUNK)], buf.at[cur], sem.at[cur]).wait()
    y_ref[...] = h_ref[...] + jnp.concatenate([buf[cur, :, k, :] for k in range(X_TILES)], axis=1)


def _gather(slot3, h, ys):
    t = h.shape[0]
    nc = t // ROW_CHUNK
    return pl.pallas_call(
        _gather_kernel,
        grid=(nc,),
        in_specs=[pl.BlockSpec((1, 1, ROW_CHUNK), lambda c: (c, 0, 0), memory_space=pltpu.SMEM),
                  pl.BlockSpec((1, 1, ROW_CHUNK), lambda c: (jnp.minimum(c + 1, nc - 1), 0, 0),
                               memory_space=pltpu.SMEM),
                  pl.BlockSpec((ROW_CHUNK, D_MODEL), lambda c: (c, 0)),
                  pl.BlockSpec(memory_space=pl.ANY)],
        out_specs=pl.BlockSpec((ROW_CHUNK, D_MODEL), lambda c: (c, 0)),
        out_shape=jax.ShapeDtypeStruct((t, D_MODEL), F32),
        scratch_shapes=[pltpu.VMEM((2, ROW_CHUNK, X_TILES, LANES), F32), pltpu.SemaphoreType.DMA((2,))],
        compiler_params=pltpu.CompilerParams(dimension_semantics=("arbitrary",), vmem_limit_bytes=VMEM_LIMIT),
        name="moe_gather",
    )(slot3, slot3, h, ys)


def _group_moe_kernel(tg_ref, nu_ref, xs_ref, wg_ref, wu_ref, wd_ref, ys_ref):
    j = pl.program_id(0)

    @pl.when(j < nu_ref[0])
    def _():
        g = tg_ref[j]
        x = _tiles_to_rows(xs_ref, 0, X_TILES).astype(BF16)
        gate = xs_ref[:, X_TILES, :]
        lane = lax.broadcasted_iota(jnp.int32, gate.shape, 1)
        y = jnp.zeros((MOE_ROWS, D_MODEL), F32)
        for e in range(EXPERTS_PER_GROUP):
            gu = jnp.dot(x, wg_ref[e], preferred_element_type=F32)
            up = jnp.dot(x, wu_ref[e], preferred_element_type=F32)
            he = (gu * (1.0 / (1.0 + jnp.exp(-gu)))) * up
            out = jnp.dot(he.astype(BF16), wd_ref[e], preferred_element_type=F32)
            col = ROUTER_LANE0 + g * EXPERTS_PER_GROUP + e
            y = y + jnp.sum(jnp.where(lane == col, gate, 0.0), axis=-1, keepdims=True) * out
        _rows_to_tiles(ys_ref, 0, y)

    @pl.when(j >= nu_ref[0])
    def _():
        ys_ref[...] = jnp.zeros(ys_ref.shape, F32)


def _group_moe(tile_group, n_used, xs, wg16, wu16, wd16):
    n_slots = xs.shape[0]
    wspec = lambda shape: pl.BlockSpec((EXPERTS_PER_GROUP,) + shape, lambda j, tg, nu: (tg[j], 0, 0))
    return pl.pallas_call(
        _group_moe_kernel,
        grid_spec=pltpu.PrefetchScalarGridSpec(
            num_scalar_prefetch=2,
            grid=(n_slots // MOE_ROWS,),
            in_specs=[pl.BlockSpec((MOE_ROWS, PAY_SUB, LANES), lambda j, tg, nu: (j, 0, 0)),
                      wspec((D_MODEL, D_EXPERT)), wspec((D_MODEL, D_EXPERT)), wspec((D_EXPERT, D_MODEL))],
            out_specs=pl.BlockSpec((MOE_ROWS, X_TILES, LANES), lambda j, tg, nu: (j, 0, 0)),
        ),
        out_shape=jax.ShapeDtypeStruct((n_slots, X_TILES, LANES), F32),
        compiler_params=pltpu.CompilerParams(dimension_semantics=("arbitrary",), vmem_limit_bytes=VMEM_LIMIT),
        name="moe_group",
    )(tile_group, n_used, xs, wg16, wu16, wd16)


def _routed_moe(h, pay, gate, wg16, wu16, wd16):
    t = pay.shape[0]
    n_slots = t + N_EXPERT_GROUPS * MOE_ROWS
    n_tiles = n_slots // MOE_ROWS
    plan, counts = _plan(gate)
    rank = plan[:, 0].astype(jnp.int32)
    gid = plan[:, 1].astype(jnp.int32)
    cnt = counts[0, :N_EXPERT_GROUPS].astype(jnp.int32)
    padded = -(-cnt // MOE_ROWS) * MOE_ROWS
    ends = jnp.cumsum(padded)
    off = ends - padded
    slot = (off[gid] + rank).reshape(t // ROW_CHUNK, 1, ROW_CHUNK)
    tile_start = jnp.arange(n_tiles, dtype=jnp.int32) * MOE_ROWS
    tile_group = jnp.minimum(jnp.sum(tile_start[:, None] >= ends[None, :], axis=1), N_EXPERT_GROUPS - 1)
    n_used = (ends[-1] // MOE_ROWS).reshape(1)
    meta = jnp.concatenate([off + cnt, ends[-1:], ends, jnp.full((1,), n_slots, jnp.int32)]).astype(jnp.int32)
    xs = _scatter(meta, slot, pay, n_slots)
    ys = _group_moe(tile_group.astype(jnp.int32), n_used.astype(jnp.int32), xs, wg16, wu16, wd16)
    return _gather(slot, h, ys)


def _moe_kernel(n2_ref, gate_ref, h_ref, wg_ref, wu_ref, wd_ref, y_ref):
    e = pl.program_id(1)

    @pl.when(e == 0)
    def _():
        y_ref[...] = h_ref[...]

    n2 = n2_ref[...]
    gu = jnp.dot(n2, wg_ref[0], preferred_element_type=F32)
    up = jnp.dot(n2, wu_ref[0], preferred_element_type=F32)
    he = (gu * (1.0 / (1.0 + jnp.exp(-gu)))) * up
    out = jnp.dot(he.astype(BF16), wd_ref[0], preferred_element_type=F32)
    lane = lax.broadcasted_iota(jnp.int32, gate_ref.shape, 1)
    gcol = jnp.sum(jnp.where(lane == e + ROUTER_LANE0, gate_ref[...], 0.0), axis=-1, keepdims=True)
    y_ref[...] += gcol * out


def _moe(n2, gate, h, wg16, wu16, wd16, tm):
    t = n2.shape[0]
    row = lambda i, e: (i, 0)
    return pl.pallas_call(
        _moe_kernel,
        grid=(t // tm, N_EXPERTS),
        in_specs=[
            pl.BlockSpec((tm, D_MODEL), row),
            pl.BlockSpec((tm, LANES), row),
            pl.BlockSpec((tm, D_MODEL), row),
            pl.BlockSpec((1, D_MODEL, D_EXPERT), lambda i, e: (e, 0, 0)),
            pl.BlockSpec((1, D_MODEL, D_EXPERT), lambda i, e: (e, 0, 0)),
            pl.BlockSpec((1, D_EXPERT, D_MODEL), lambda i, e: (e, 0, 0)),
        ],
        out_specs=pl.BlockSpec((tm, D_MODEL), row),
        out_shape=jax.ShapeDtypeStruct((t, D_MODEL), F32),
        compiler_params=pltpu.CompilerParams(
            dimension_semantics=("arbitrary", "arbitrary"), vmem_limit_bytes=VMEM_LIMIT),
        name="moe",
    )(n2, gate, h, wg16, wu16, wd16)


PROJ_TILE = 512
MOE_TILE = 1024
PROMPT_NPP = LANES // 2
SAMPLE_NPP = LANES


def _round_up(n, mult):
    return -(-n // mult) * mult


def _feature_major(x5):
    b, t = x5.shape[:2]
    return jnp.transpose(x5, (0, 2, 3, 4, 1)).reshape(b, KV_COLS, t)


def _token_major(xt):
    b, _, t = xt.shape
    return jnp.transpose(xt.reshape(b, 2, N_KV_HEADS, HEAD_DIM, t), (0, 4, 1, 2, 3))


def _pad_pages(x, npp_from, npp_to):
    b = x.shape[0]
    x = x.reshape(b, N_KV_HEADS, npp_from, LANES)
    return jnp.concatenate([x, jnp.zeros((b, N_KV_HEADS, npp_to - npp_from, LANES), x.dtype)], axis=2)


def _expand_const(n_chunks, npp):
    col = np.arange(2 * npp)
    blk = 2 * (col % npp) + col // npp
    tok_blk = np.arange(n_chunks * K_CHUNK) // CMP_BLOCK
    e = (blk[None, :, None] == tok_blk.reshape(n_chunks, 1, K_CHUNK)).astype(np.float32)
    return jnp.asarray(e, dtype=BF16)


def _gate_place_const():
    gp = np.zeros((GATE_PAD, 3 * ATTN_WIDTH), np.float32)
    for h in range(N_HEADS):
        for c in range(3):
            gp[3 * h + c, c * ATTN_WIDTH + h * HEAD_DIM:c * ATTN_WIDTH + (h + 1) * HEAD_DIM] = 1.0
    return jnp.asarray(gp, dtype=BF16)


def kernel(x_prompt, x_sample, cache_cmp_kv, cache_slc_kv, state_win_kv, state_pool, page_table, norm1_g, w_in, q_norm_g, k_norm_cmp_g, k_norm_slc_g, k_norm_win_g, cmp_pos_emb, w_cmp_k, w_cmp_v, pool_w, pool_scale, w_out, norm2_g, w_router_group, w_router_expert, w_gate, w_up, w_down):
    b, s, _ = x_prompt.shape
    db, ds, _ = x_sample.shape
    n_pool, page_rows = cache_cmp_kv.shape[:2]
    assert page_rows == PAGE and s % PAGE == 0
    past = page_table.shape[1] * page_rows

    kv0 = ATTN_WIDTH
    kv1 = kv0 + KV_ROWS
    w_row = jnp.concatenate(
        [w_in[:, :kv0], w_in[:, kv1:kv1 + GATE_COLS], jnp.zeros((D_MODEL, GATE_PAD - GATE_COLS), w_in.dtype),
         w_in[:, kv1 + GATE_COLS:]], axis=1).astype(BF16)
    w_kv = w_in[:, kv0:kv1].T.astype(BF16)
    g1 = norm1_g[None, :]
    g2 = norm2_g[None, :]
    two = lambda g: jnp.tile(g, 2)[None, :]
    kg = jnp.stack([k_norm_slc_g, k_norm_win_g])[:, :, None]
    pos_t = jnp.tile(cmp_pos_emb.T, (1, 2))
    zb = jnp.zeros((HEAD_DIM, CMP_BLOCK, HEAD_DIM), F32)

    def blockdiag(w):
        wt = jnp.transpose(w, (1, 0, 2))
        return jnp.concatenate([jnp.concatenate([wt, zb], axis=2), jnp.concatenate([zb, wt], axis=2)], axis=1)

    bdk, bdv = blockdiag(w_cmp_k), blockdiag(w_cmp_v)
    zq = jnp.zeros_like(bdk)
    bd = jnp.concatenate([jnp.concatenate([bdk, zq], axis=2), jnp.concatenate([zq, bdv], axis=2)],
                         axis=1).astype(BF16)
    w_router32 = jnp.concatenate(
        [w_router_group, w_router_expert,
         jnp.zeros((D_MODEL, LANES - N_EXPERT_GROUPS - N_EXPERTS), w_router_group.dtype)], axis=1)
    w_router_hi = w_router32.astype(BF16)
    w_router = jnp.stack([w_router_hi, (w_router32 - w_router_hi.astype(F32)).astype(BF16)])
    w_out16 = w_out.astype(BF16)
    pool_w16 = pool_w.astype(BF16)
    wg16, wu16, wd16 = w_gate.astype(BF16), w_up.astype(BF16), w_down.astype(BF16)
    ps = pool_scale[None, :]

    def ffn_moe(x2d, a2d, m2d, tm_ffn, tm_moe, routed):
        if routed:
            return _routed_moe(*_ffn(x2d, a2d, m2d, w_out16, g2, w_router, tm_ffn, True), wg16, wu16, wd16)
        h, n2, gate = _ffn(x2d, a2d, m2d, w_out16, g2, w_router, tm_ffn, False)
        return _moe(n2, gate, h, wg16, wu16, wd16, tm_moe)

    q, gates, u, kvc_t, kvs_t, kvw_t, ks16, vsx, kw16, vwx = _proj(
        x_prompt, g1, w_row, w_kv, two(q_norm_g), kg, PROJ_TILE)
    m_p = _pool(u, pool_w16, ps)
    pages_p = s // PAGE
    pt_p = jnp.zeros((b, pages_p), jnp.int32)
    dummy_tail = jnp.zeros((b, KV_COLS, LANES), F32)
    kc_p, vc_p = _compress(pt_p, kvc_t, dummy_tail, pos_t, bd, two(k_norm_cmp_g),
                           npp=pages_p, has_tail=False, paged=False)
    kc_p = _pad_pages(kc_p, pages_p, PROMPT_NPP)
    vc_p = _pad_pages(vc_p, pages_p, PROMPT_NPP)
    a_p = _attn_prompt(q, gates, kc_p, vc_p, ks16, vsx, kw16, vwx, _expand_const(s // K_CHUNK, PROMPT_NPP),
                       _gate_place_const(), pages_p)
    y_p = ffn_moe(x_prompt.reshape(b * s, D_MODEL), a_p.reshape(b * s, ATTN_WIDTH),
                  m_p.reshape(b * s, POOL_WIDTH), PROJ_TILE, MOE_TILE, True)

    ts = db * ds
    q_s, gates_s, u_s, kvc_st, kvs_st, kvw_st, _, _, _, _ = _proj(
        x_sample.reshape(1, ts, D_MODEL), g1, w_row, w_kv, two(q_norm_g), kg, ts)
    u_s = u_s.reshape(db, ds, POOL_WIDTH)
    pool_ext = jnp.concatenate([state_pool, u_s], axis=1)
    n_ext = pool_ext.shape[1]
    lead = _round_up(n_ext, 8) - n_ext
    pool_in = jnp.concatenate([jnp.zeros((db, lead, POOL_WIDTH), F32), pool_ext], axis=1)
    m_s = _pool(pool_in, pool_w16, ps)[:, lead + n_ext - ds:]

    def new_rows_t(xt):
        x = jnp.transpose(xt.reshape(KV_COLS, db, ds), (1, 0, 2))
        return jnp.concatenate([x, jnp.zeros((db, KV_COLS, LANES - ds), F32)], axis=2)

    kvc_new, kvs_new, kvw_new = new_rows_t(kvc_st), new_rows_t(kvs_st), new_rows_t(kvw_st)
    n_pages = page_table.shape[1]
    npp_c = _round_up(n_pages + 1, 8)
    cmp_pages = _feature_major(cache_cmp_kv)
    slc_pages = _feature_major(cache_slc_kv)
    kc_s, vc_s = _compress(page_table, cmp_pages, kvc_new, pos_t, bd, two(k_norm_cmp_g),
                           npp=npp_c, has_tail=True, paged=True)
    kc_s = _pad_pages(kc_s, npp_c, SAMPLE_NPP)
    vc_s = _pad_pages(vc_s, npp_c, SAMPLE_NPP)
    win_t = _feature_major(state_win_kv)
    q_s3 = q_s.reshape(db, ds, ATTN_WIDTH)
    o_cmp_s, sel_s, need_s = _select_sample(q_s3, kc_s, vc_s, past=past)
    need = need_s[:, 0, :n_pages] > 0.5
    order = jnp.argsort(jnp.logical_not(need), axis=1, stable=True).astype(jnp.int32)
    n_need = jnp.sum(need, axis=1).astype(jnp.int32)
    a_s = _attn_sample(page_table, order, n_need, q_s3, gates_s.reshape(db, ds, GATE_PAD), o_cmp_s, sel_s,
                       kvs_new, kvw_new, win_t, slc_pages, past=past)
    y_s = ffn_moe(x_sample.reshape(ts, D_MODEL), a_s.reshape(ts, ATTN_WIDTH), m_s.reshape(ts, POOL_WIDTH), ts, ts, False)

    win_keep = min(WINDOW, s)
    sample5 = lambda xt: jnp.transpose(xt.reshape(2, N_KV_HEADS, HEAD_DIM, db, ds), (3, 4, 0, 1, 2))
    win_ctx_t = jnp.concatenate([win_t, kvw_new[:, :, :ds]], axis=2)
    return (y_p.reshape(b, s, D_MODEL), y_s.reshape(db, ds, D_MODEL),
            _token_major(kvc_t), sample5(kvc_st),
            _token_major(kvs_t), sample5(kvs_st),
            _token_major(kvw_t[:, :, s - win_keep:]), _token_major(win_ctx_t[:, :, ds:]),
            u[:, s - POOL_BUF:], pool_ext[:, ds:])
```

```python
import functools

import jax
import jax.numpy as jnp
import numpy as np
from jax import lax
from jax.experimental import pallas as pl
from jax.experimental.pallas import tpu as pltpu

F32 = jnp.float32
BF16 = jnp.bfloat16

D_MODEL = 1024
N_HEADS = 8
HEAD_DIM = 64
N_KV_HEADS = 2
Q_PER_KV = N_HEADS // N_KV_HEADS
ATTN_WIDTH = N_HEADS * HEAD_DIM
KV_COLS = 2 * N_KV_HEADS * HEAD_DIM
GATE_COLS = 3 * N_HEADS
POOL_WIDTH = D_MODEL - ATTN_WIDTH
POOL_WINDOWS = (2, 4, 8, 16)
POOL_GROUP_WIDTH = POOL_WIDTH // len(POOL_WINDOWS)
POOL_BUF = max(POOL_WINDOWS) - 1
CMP_BLOCK = 64
TOP_K_BLOCKS = 16
WINDOW = 512
FORCE_SCORE = 1.0e4
N_EXPERT_GROUPS = 4
EXPERTS_PER_GROUP = 4
N_EXPERTS = N_EXPERT_GROUPS * EXPERTS_PER_GROUP
D_EXPERT = 512
EPS = 1e-6
NEG = -1e30
SCALE = HEAD_DIM ** -0.5

LANES = 128
PAGE = 2 * CMP_BLOCK
GATE_PAD = LANES
ROW_COLS = ATTN_WIDTH + GATE_PAD + POOL_WIDTH
OFF_GATE = ATTN_WIDTH
OFF_U = OFF_GATE + GATE_PAD
KV_ROWS = 3 * KV_COLS
ROUTER_LANE0 = N_EXPERT_GROUPS
Q_TILE = 256
K_CHUNK = 256
VMEM_LIMIT = 56 * 1024 * 1024


def _slope(h):
    return float(2.0 ** (-8.0 * (h + 1) / N_HEADS))


def _half_group_norm(v, g):
    lane = lax.broadcasted_iota(jnp.int32, v.shape, 1)
    lo = lane < HEAD_DIM
    v2 = v * v
    s_lo = jnp.sum(jnp.where(lo, v2, 0.0), axis=-1, keepdims=True)
    s_hi = jnp.sum(jnp.where(lo, 0.0, v2), axis=-1, keepdims=True)
    r_lo = lax.rsqrt(s_lo * (1.0 / HEAD_DIM) + EPS)
    r_hi = lax.rsqrt(s_hi * (1.0 / HEAD_DIM) + EPS)
    return (v * jnp.where(lo, r_lo, r_hi)) * g


def _dot_t(a, b):
    return lax.dot_general(a, b, (((1,), (1,)), ((), ())), preferred_element_type=F32)


def _proj_kernel(x_ref, g1_ref, wr_ref, wkv_ref, qg_ref, kg_ref,
                 q_ref, gates_ref, u_ref, kvc_ref, kvs_ref, kvw_ref, ks16_ref, vsx_ref, kw16_ref, vwx_ref):
    x = x_ref[0]
    tm = x.shape[0]
    ms = jnp.mean(x * x, axis=-1, keepdims=True)
    n = ((x * lax.rsqrt(ms + EPS)) * g1_ref[...]).astype(BF16)
    p = jnp.dot(n, wr_ref[...], preferred_element_type=F32)
    pt = _dot_t(wkv_ref[...], n)
    for t in range(ATTN_WIDTH // LANES):
        sl = slice(t * LANES, (t + 1) * LANES)
        q_ref[0, :, sl] = _half_group_norm(p[:, sl], qg_ref[...]).astype(BF16)
    gates_ref[0] = 1.0 / (1.0 + jnp.exp(-p[:, OFF_GATE:OFF_GATE + GATE_PAD]))
    u_ref[0] = p[:, OFF_U:OFF_U + POOL_WIDTH]
    kvc_ref[0] = pt[:KV_COLS]
    half = KV_COLS // 2
    lane = lax.broadcasted_iota(jnp.int32, (tm, LANES), 1)
    for bi, (out32, k16, vx16) in enumerate(((kvs_ref, ks16_ref, vsx_ref), (kvw_ref, kw16_ref, vwx_ref))):
        off = (bi + 1) * KV_COLS
        heads = []
        for hh in range(N_KV_HEADS):
            kh = pt[off + hh * HEAD_DIM:off + (hh + 1) * HEAD_DIM]
            msk = jnp.mean(kh * kh, axis=0, keepdims=True)
            heads.append((kh * lax.rsqrt(msk + EPS)) * kg_ref[bi])
        kn = jnp.concatenate(heads, axis=0)
        v = pt[off + half:off + KV_COLS]
        out32[0, :half, :] = kn
        out32[0, half:, :] = v
        vt = v.T
        vx = (jnp.where(lane < HEAD_DIM, vt, 1.0), jnp.where(lane < HEAD_DIM, pltpu.roll(vt, HEAD_DIM, axis=1), 1.0))
        for c in range(tm // K_CHUNK):
            cs = slice(c * K_CHUNK, (c + 1) * K_CHUNK)
            k16[0, c] = kn[:, cs].astype(BF16)
            for hh in range(N_KV_HEADS):
                vx16[0, c, hh] = vx[hh][cs].astype(BF16)


def _proj(x3d, g1, w_row, w_kv, qg, kg, tm):
    b, s, _ = x3d.shape
    tok = lambda i, j: (i, j, 0)
    feat = lambda i, j: (i, 0, j)
    const2 = lambda i, j: (0, 0)
    nck = tm // K_CHUNK
    out_shape = (
        jax.ShapeDtypeStruct((b, s, ATTN_WIDTH), BF16),
        jax.ShapeDtypeStruct((b, s, GATE_PAD), F32),
        jax.ShapeDtypeStruct((b, s, POOL_WIDTH), F32),
        jax.ShapeDtypeStruct((b, KV_COLS, s), F32),
        jax.ShapeDtypeStruct((b, KV_COLS, s), F32),
        jax.ShapeDtypeStruct((b, KV_COLS, s), F32),
        jax.ShapeDtypeStruct((b, s // K_CHUNK, LANES, K_CHUNK), BF16),
        jax.ShapeDtypeStruct((b, s // K_CHUNK, N_KV_HEADS, K_CHUNK, LANES), BF16),
        jax.ShapeDtypeStruct((b, s // K_CHUNK, LANES, K_CHUNK), BF16),
        jax.ShapeDtypeStruct((b, s // K_CHUNK, N_KV_HEADS, K_CHUNK, LANES), BF16),
    )
    k_spec = pl.BlockSpec((1, nck, LANES, K_CHUNK), lambda i, j: (i, j, 0, 0))
    v_spec = pl.BlockSpec((1, nck, N_KV_HEADS, K_CHUNK, LANES), lambda i, j: (i, j, 0, 0, 0))
    out_specs = (
        pl.BlockSpec((1, tm, ATTN_WIDTH), tok),
        pl.BlockSpec((1, tm, GATE_PAD), tok),
        pl.BlockSpec((1, tm, POOL_WIDTH), tok),
        pl.BlockSpec((1, KV_COLS, tm), feat),
        pl.BlockSpec((1, KV_COLS, tm), feat),
        pl.BlockSpec((1, KV_COLS, tm), feat),
        k_spec, v_spec, k_spec, v_spec,
    )
    return pl.pallas_call(
        _proj_kernel,
        grid=(b, s // tm),
        in_specs=[
            pl.BlockSpec((1, tm, D_MODEL), tok),
            pl.BlockSpec((1, D_MODEL), const2),
            pl.BlockSpec((D_MODEL, ROW_COLS), const2),
            pl.BlockSpec((KV_ROWS, D_MODEL), const2),
            pl.BlockSpec((1, LANES), const2),
            pl.BlockSpec((2, HEAD_DIM, 1), lambda i, j: (0, 0, 0)),
        ],
        out_specs=out_specs,
        out_shape=out_shape,
        compiler_params=pltpu.CompilerParams(
            dimension_semantics=("arbitrary", "arbitrary"), vmem_limit_bytes=VMEM_LIMIT),
        name="proj",
    )(x3d, g1, w_row, w_kv, qg, kg)


def _pool_kernel(u_ref, pw_ref, ps_ref, m_ref, *, seg):
    u = u_ref[0]
    n = u.shape[0]

    def shift(v, k):
        rolled = pltpu.roll(v, k, axis=0)
        r = lax.broadcasted_iota(jnp.int32, v.shape, 0) % seg
        return jnp.where(r >= k, rolled, 0.0)

    sums = []
    s = u
    k = 1
    for gi in range(len(POOL_WINDOWS)):
        s = s + shift(s, k)
        k *= 2
        sums.append(s[:, :POOL_GROUP_WIDTH])
        s = s[:, POOL_GROUP_WIDTH:]
    row = lax.broadcasted_iota(jnp.int32, (n, 1), 0) % seg
    for gi, w in enumerate(POOL_WINDOWS):
        sl = slice(gi * POOL_GROUP_WIDTH, (gi + 1) * POOL_GROUP_WIDTH)
        cnt = jnp.minimum(row + 1, w).astype(F32)
        d = sums[gi] / cnt - u[:, sl]
        y = jnp.dot(d.astype(BF16), pw_ref[gi], preferred_element_type=F32)
        m_ref[0, :, sl] = (y * ps_ref[:, sl]).astype(BF16)


def _pool(u3d, pool_w16, pool_scale, seg):
    b, n, _ = u3d.shape
    assert n % seg == 0
    return pl.pallas_call(
        functools.partial(_pool_kernel, seg=seg),
        grid=(b,),
        in_specs=[
            pl.BlockSpec((1, n, POOL_WIDTH), lambda i: (i, 0, 0)),
            pl.BlockSpec((len(POOL_WINDOWS), POOL_GROUP_WIDTH, POOL_GROUP_WIDTH), lambda i: (0, 0, 0)),
            pl.BlockSpec((1, POOL_WIDTH), lambda i: (0, 0)),
        ],
        out_specs=pl.BlockSpec((1, n, POOL_WIDTH), lambda i: (i, 0, 0)),
        out_shape=jax.ShapeDtypeStruct((b, n, POOL_WIDTH), BF16),
        compiler_params=pltpu.CompilerParams(
            dimension_semantics=("arbitrary",), vmem_limit_bytes=VMEM_LIMIT),
        name="pool",
    )(u3d, pool_w16, pool_scale)


def _compress_kernel(pt_ref, src_ref, tail_ref, pos_ref, bd_ref, g_ref, kc_ref, vc_ref, buf, sem,
                     *, n_pages, npp, has_tail, paged):
    b = pl.program_id(0)
    nb = pl.num_programs(0)
    n_slabs = 2 * N_KV_HEADS

    def slab_copy(row, p, c, slot):
        kv, kvh = divmod(c, N_KV_HEADS)
        if paged:
            src = src_ref.at[pt_ref[row, p], pl.ds(c * HEAD_DIM, HEAD_DIM), :]
        else:
            src = src_ref.at[row, pl.ds(c * HEAD_DIM, HEAD_DIM), pl.ds(p * PAGE, PAGE)]
        return pltpu.make_async_copy(src, buf.at[slot, kv, :, kvh * npp + p, :], sem.at[slot])

    def tail_copy(c, slot):
        kv, kvh = divmod(c, N_KV_HEADS)
        return pltpu.make_async_copy(tail_ref.at[0, pl.ds(c * HEAD_DIM, HEAD_DIM), :],
                                     buf.at[slot, kv, :, kvh * npp + n_pages, :], sem.at[slot])

    def row_copies(row, slot, fn):
        def body(p, carry):
            for c in range(n_slabs):
                fn(slab_copy(row, p, c, slot))
            return carry
        lax.fori_loop(0, n_pages, body, 0)

    n_real = n_pages + (1 if has_tail else 0)

    @pl.when(b == 0)
    def _():
        if npp > n_real:
            for kvh in range(N_KV_HEADS):
                buf[:, :, :, kvh * npp + n_real:(kvh + 1) * npp, :] = jnp.zeros(
                    (2, 2, HEAD_DIM, npp - n_real, LANES), F32)
        row_copies(0, 0, lambda cp: cp.start())

    slot = b % 2

    if has_tail:
        for c in range(n_slabs):
            tail_copy(c, slot).start()
        for c in range(n_slabs):
            tail_copy(c, slot).wait()
    row_copies(b, slot, lambda cp: cp.wait())

    rows = N_KV_HEADS * npp
    d_per_it = 8
    n_it = HEAD_DIM // d_per_it
    pages_per_it = n_pages // n_it
    nxt = jnp.minimum(b + 1, nb - 1)

    def body(it, acc):
        for pp in range(pages_per_it):
            for c in range(n_slabs):
                slab_copy(nxt, it * pages_per_it + pp, c, 1 - slot).start()
        for dd in range(d_per_it):
            d = it * d_per_it + dd
            pos = pos_ref[pl.ds(d, 1), :]
            x = jnp.concatenate([buf[slot, 0, d] + pos, buf[slot, 1, d] + pos], axis=1).astype(BF16)
            acc = acc + jnp.dot(x, bd_ref[d], preferred_element_type=F32)
        return acc

    acc = lax.fori_loop(0, n_it, body, jnp.zeros((rows, 2 * LANES), F32))
    kc_ref[0] = _half_group_norm(acc[:, :LANES], g_ref[...])
    vc_ref[0] = acc[:, LANES:]

    @pl.when(b == nb - 1)
    def _():
        row_copies(nxt, 1 - slot, lambda cp: cp.wait())


def _compress(page_table, src, tail, pos_t, bd, g2, *, npp, has_tail, paged):
    b, n_pages = page_table.shape
    kern = functools.partial(_compress_kernel, n_pages=n_pages, npp=npp, has_tail=has_tail, paged=paged)
    rows = N_KV_HEADS * npp
    grid_spec = pltpu.PrefetchScalarGridSpec(
        num_scalar_prefetch=1,
        grid=(b,),
        in_specs=[
            pl.BlockSpec(memory_space=pl.ANY),
            pl.BlockSpec((1, KV_COLS, LANES), lambda i, pt: (i, 0, 0)),
            pl.BlockSpec((HEAD_DIM, LANES), lambda i, pt: (0, 0)),
            pl.BlockSpec((HEAD_DIM, 2 * LANES, 2 * LANES), lambda i, pt: (0, 0, 0)),
            pl.BlockSpec((1, LANES), lambda i, pt: (0, 0)),
        ],
        out_specs=(pl.BlockSpec((1, rows, LANES), lambda i, pt: (i, 0, 0)),
                   pl.BlockSpec((1, rows, LANES), lambda i, pt: (i, 0, 0))),
        scratch_shapes=[
            pltpu.VMEM((2, 2, HEAD_DIM, rows, LANES), F32),
            pltpu.SemaphoreType.DMA((2,)),
        ],
    )
    return pl.pallas_call(
        kern,
        grid_spec=grid_spec,
        out_shape=(jax.ShapeDtypeStruct((b, rows, LANES), F32), jax.ShapeDtypeStruct((b, rows, LANES), F32)),
        compiler_params=pltpu.CompilerParams(
            dimension_semantics=("arbitrary",), vmem_limit_bytes=VMEM_LIMIT),
        name="compress",
    )(page_table, src, tail, pos_t, bd, g2)


def _to_half(tile, src_half, dst_half):
    lane = lax.broadcasted_iota(jnp.int32, tile.shape, 1)
    src = tile if src_half == dst_half else pltpu.roll(tile, HEAD_DIM, axis=1)
    keep = (lane < HEAD_DIM) if dst_half == 0 else (lane >= HEAD_DIM)
    return jnp.where(keep, src, 0.0)


def _pair_tile(o_even, o_odd, half):
    lane = lax.broadcasted_iota(jnp.int32, o_even.shape, 1)
    if half == 0:
        return jnp.where(lane < HEAD_DIM, o_even, pltpu.roll(o_odd, HEAD_DIM, axis=1))
    return jnp.where(lane < HEAD_DIM, pltpu.roll(o_even, HEAD_DIM, axis=1), o_odd)


def _gate_tile(gates, pair, c, shape):
    lane = lax.broadcasted_iota(jnp.int32, shape, 1)
    he, ho = 2 * pair, 2 * pair + 1
    return jnp.where(lane < HEAD_DIM, gates[:, 3 * he + c:3 * he + c + 1], gates[:, 3 * ho + c:3 * ho + c + 1])


def _block_of_col(col, npp):
    return 2 * (col % npp) + col // npp


def _cmp_operand(x, npp):
    return jnp.concatenate([_to_half(x, 0, 0), _to_half(x, 1, 0)], axis=0).astype(BF16)


def _rank_select(score_t, blk_t, cols):
    rank = jnp.zeros(score_t.shape, jnp.int32)
    for r, n in cols:
        row = score_t[r:r + 1, :]
        ahead = (row > score_t) | ((row == score_t) & (blk_t > n))
        rank = rank + ahead.astype(jnp.int32)
    return ((rank < TOP_K_BLOCKS) & (score_t > -0.5)).astype(F32)


POS_HI, POS_LO = HEAD_DIM, HEAD_DIM + 1
MASK_BIG = 1e30


def _attn_prompt_kernel(q_ref, gates_ref, kc_ref, vc_ref, ks_ref, vsx_ref, kw_ref, vwx_ref, exp_ref, gp_ref, a_ref,
                        zbuf, m_sc, acc_sc, *, n_pages, npp):
    i = pl.program_id(1)
    q0 = i * Q_TILE
    rows = Q_PER_KV * Q_TILE
    ncols = 2 * npp
    qpos = q0 + lax.broadcasted_iota(jnp.int32, (Q_TILE, 1), 0)
    blk = _block_of_col(lax.broadcasted_iota(jnp.int32, (Q_TILE, ncols), 1), npp)
    real_cols = [(half * n_pages + p, 2 * p + half) for half in range(2) for p in range(n_pages)]

    g = gates_ref[0]
    g_hi = g.astype(BF16)
    g_lo = (g - g_hi.astype(F32)).astype(BF16)
    gexp = (jnp.dot(g_hi, gp_ref[...], preferred_element_type=F32)
            + jnp.dot(g_lo, gp_ref[...], preferred_element_type=F32))

    r_key = lax.broadcasted_iota(jnp.int32, (Q_TILE, K_CHUNK), 1)
    r_qry = lax.broadcasted_iota(jnp.int32, (Q_TILE, K_CHUNK), 0)
    keep = {"causal": r_key <= r_qry, "lower": r_key > r_qry}

    all_rows = N_HEADS * Q_TILE

    def branch(lhs, k_ref, vx_ref, slc, chunks, c_lo, c_hi):
        m_sc[...] = jnp.full((all_rows, LANES), NEG, F32)

        def scores(c, kind):
            rr = lax.broadcasted_iota(jnp.int32, (HEAD_DIM, K_CHUNK), 0)
            tt = lax.broadcasted_iota(jnp.int32, (HEAD_DIM, K_CHUNK), 1).astype(F32)
            hi = ((c - i) * K_CHUNK).astype(F32)
            pos = jnp.where(rr == 0, hi, jnp.where(rr == 1, tt, 0.0)).astype(BF16)
            for kvh in range(N_KV_HEADS):
                parts = [k_ref[0, c, kvh * HEAD_DIM:(kvh + 1) * HEAD_DIM, :], pos]
                if slc:
                    parts.append(exp_ref[c])
                kx = jnp.concatenate(parts, axis=0)
                for gq in range(Q_PER_KV):
                    sl = slice((kvh * Q_PER_KV + gq) * Q_TILE, (kvh * Q_PER_KV + gq + 1) * Q_TILE)
                    zg = jnp.dot(lhs[sl], kx, preferred_element_type=F32)
                    if kind is not None:
                        zg = jnp.where(keep[kind], zg, NEG)
                    zbuf[c, sl, :] = zg
                    m_sc[sl] = jnp.maximum(m_sc[sl], jnp.maximum(zg[:, :LANES], zg[:, LANES:]))

        for lo, hi_, kind, cond in chunks:
            if kind is None and cond is None:
                def body(c, carry):
                    scores(c, None)
                    return carry
                lax.fori_loop(lo, hi_, body, 0)
            elif cond is None:
                scores(lo, kind)
            else:
                pl.when(cond)(functools.partial(scores, lo, kind))

        m = jnp.broadcast_to(jnp.max(m_sc[...], axis=-1, keepdims=True), (all_rows, LANES))
        acc_sc[...] = jnp.zeros((all_rows, LANES), F32)

        def pass2(c, carry):
            for h in range(N_HEADS):
                sl = slice(h * Q_TILE, (h + 1) * Q_TILE)
                z = zbuf[c, sl, :]
                e = jnp.concatenate([jnp.exp(z[:, :LANES] - m[sl]), jnp.exp(z[:, LANES:] - m[sl])], axis=1)
                acc_sc[sl] += jnp.dot(e.astype(BF16), vx_ref[0, c, h // Q_PER_KV], preferred_element_type=F32)
            return carry

        lax.fori_loop(c_lo, c_hi, pass2, 0)
        return acc_sc[...]

    def pair_norm(acc_e, acc_o):
        lane = lax.broadcasted_iota(jnp.int32, acc_e.shape, 1)
        safe = lambda d: jnp.where(d > 0.0, d, 1.0)
        sw_e = pltpu.roll(acc_e, HEAD_DIM, axis=1)
        sw_o = pltpu.roll(acc_o, HEAD_DIM, axis=1)
        return jnp.where(lane < HEAD_DIM, acc_e / safe(sw_e), sw_o / safe(acc_o))

    o_cmp_all, lhs_slc_all, lhs_win_all = [], [], []
    for kvh in range(N_KV_HEADS):
        slopes = [_slope(kvh * Q_PER_KV + g) for g in range(Q_PER_KV)]
        lane = lax.broadcasted_iota(jnp.int32, (Q_TILE, LANES), 1)
        pieces, pieces_x = [], []
        for gq in range(Q_PER_KV):
            h = kvh * Q_PER_KV + gq
            tile = q_ref[0, :, (h // 2) * LANES:(h // 2 + 1) * LANES].astype(F32) * SCALE
            low = _to_half(tile, h % 2, 0)
            pieces.append(low)
            pieces_x.append(jnp.where((lane == POS_HI) | (lane == POS_LO), slopes[gq], low))
        qk = jnp.concatenate(pieces, axis=0).astype(BF16)
        qk_x = jnp.concatenate(pieces_x, axis=0).astype(BF16)

        kc = _cmp_operand(kc_ref[0, kvh], npp)
        vc = _cmp_operand(vc_ref[0, kvh], npp)
        s = _dot_t(qk, kc)
        dist_c = qpos - (blk * CMP_BLOCK + (CMP_BLOCK - 1))
        mask_c = dist_c >= 0
        dist_cf = dist_c.astype(F32)
        p_list = []
        p_kv = jnp.zeros((Q_TILE, ncols), F32)
        for g in range(Q_PER_KV):
            z = jnp.where(mask_c, s[g * Q_TILE:(g + 1) * Q_TILE] - slopes[g] * dist_cf, NEG)
            mx = jnp.max(z, axis=-1, keepdims=True)
            e = jnp.where(mask_c, jnp.exp(z - mx), 0.0)
            den = jnp.sum(e, axis=-1, keepdims=True)
            p = e / jnp.where(den > 0.0, den, 1.0)
            p_kv = p_kv + p
            p_list.append(p.astype(BF16))
        o_cmp = jnp.dot(jnp.concatenate(p_list, axis=0), vc, preferred_element_type=F32)

        cand = blk * CMP_BLOCK <= qpos
        forced = (blk == qpos // CMP_BLOCK) | (blk == 0)
        score = jnp.where(forced, FORCE_SCORE, jnp.where(cand, p_kv, -1.0))
        score_t = score.T
        st = jnp.concatenate([score_t[:n_pages], score_t[npp:npp + n_pages]], axis=0)
        rr = lax.broadcasted_iota(jnp.int32, (2 * n_pages, Q_TILE), 0)
        bt = jnp.where(rr < n_pages, 2 * rr, 2 * (rr - n_pages) + 1)
        sel_s = _rank_select(st, bt, [(r, n) for r, (_, n) in enumerate(real_cols)])
        pad = jnp.zeros((npp - n_pages, Q_TILE), F32)
        sel_t = jnp.concatenate([sel_s[:n_pages], pad, sel_s[n_pages:], pad], axis=0)
        sel = sel_t.T
        sel_bias = jnp.where(sel > 0.5, 0.0, -MASK_BIG).astype(BF16)
        lhs_slc_all.append(jnp.concatenate([qk_x, jnp.concatenate([sel_bias] * Q_PER_KV, axis=0)], axis=1))
        lhs_win_all.append(qk_x)
        o_cmp_all.append(o_cmp)

    o_slc = branch(jnp.concatenate(lhs_slc_all, axis=0), ks_ref, vsx_ref, True,
                   [(0, i, None, None), (i, None, "causal", None)], 0, i + 1)
    o_win = branch(jnp.concatenate(lhs_win_all, axis=0), kw_ref, vwx_ref, False,
                   [(i - 2, None, "lower", i >= 2), (i - 1, None, None, i >= 1), (i, None, "causal", None)],
                   jnp.maximum(i - 2, 0), i + 1)
    o_cmp = jnp.concatenate(o_cmp_all, axis=0)

    for pair in range(N_HEADS // 2):
        e_sl = slice(2 * pair * Q_TILE, (2 * pair + 1) * Q_TILE)
        o_sl = slice((2 * pair + 1) * Q_TILE, (2 * pair + 2) * Q_TILE)
        col = pair * LANES
        tile = (gexp[:, col:col + LANES] * _pair_tile(o_cmp[e_sl], o_cmp[o_sl], 0)
                + gexp[:, ATTN_WIDTH + col:ATTN_WIDTH + col + LANES] * pair_norm(o_slc[e_sl], o_slc[o_sl])
                + gexp[:, 2 * ATTN_WIDTH + col:2 * ATTN_WIDTH + col + LANES] * pair_norm(o_win[e_sl], o_win[o_sl]))
        a_ref[0, :, pair * LANES:(pair + 1) * LANES] = tile.astype(BF16)


def _attn_prompt(q, gates, kc, vc, ks16, vsx, kw16, vwx, expand, gate_place, n_pages):
    b, s, _ = q.shape
    npp = kc.shape[2]
    n_chunks = s // K_CHUNK
    rows = Q_PER_KV * Q_TILE
    assert WINDOW == 2 * K_CHUNK and Q_TILE == K_CHUNK
    kern = functools.partial(_attn_prompt_kernel, n_pages=n_pages, npp=npp)
    per_b4 = lambda bi, i: (bi, 0, 0, 0)
    per_b5 = lambda bi, i: (bi, 0, 0, 0, 0)
    return pl.pallas_call(
        kern,
        grid=(b, s // Q_TILE),
        in_specs=[
            pl.BlockSpec((1, Q_TILE, ATTN_WIDTH), lambda bi, i: (bi, i, 0)),
            pl.BlockSpec((1, Q_TILE, GATE_PAD), lambda bi, i: (bi, i, 0)),
            pl.BlockSpec((1, N_KV_HEADS, npp, LANES), per_b4),
            pl.BlockSpec((1, N_KV_HEADS, npp, LANES), per_b4),
            pl.BlockSpec((1, n_chunks, LANES, K_CHUNK), per_b4),
            pl.BlockSpec((1, n_chunks, N_KV_HEADS, K_CHUNK, LANES), per_b5),
            pl.BlockSpec((1, n_chunks, LANES, K_CHUNK), per_b4),
            pl.BlockSpec((1, n_chunks, N_KV_HEADS, K_CHUNK, LANES), per_b5),
            pl.BlockSpec((n_chunks, 2 * npp, K_CHUNK), lambda bi, i: (0, 0, 0)),
            pl.BlockSpec((GATE_PAD, 3 * ATTN_WIDTH), lambda bi, i: (0, 0)),
        ],
        out_specs=pl.BlockSpec((1, Q_TILE, ATTN_WIDTH), lambda bi, i: (bi, i, 0)),
        out_shape=jax.ShapeDtypeStruct((b, s, ATTN_WIDTH), BF16),
        scratch_shapes=[
            pltpu.VMEM((n_chunks, N_KV_HEADS * rows, K_CHUNK), F32),
            pltpu.VMEM((N_KV_HEADS * rows, LANES), F32),
            pltpu.VMEM((N_KV_HEADS * rows, LANES), F32),
        ],
        compiler_params=pltpu.CompilerParams(
            dimension_semantics=("arbitrary", "arbitrary"), vmem_limit_bytes=VMEM_LIMIT),
        name="attn_prompt",
    )(q, gates, kc, vc, ks16, vsx, kw16, vwx, expand, gate_place)


S_ROWS = LANES
S_CHUNK = 1024


def _sample_rows(q_ref, ds, past):
    n_real = N_HEADS * ds
    by_kvh, low = [], []
    for h in range(N_HEADS):
        tile = q_ref[0, :, (h // 2) * LANES:(h // 2 + 1) * LANES].astype(F32) * SCALE
        by_kvh.append(_to_half(tile, h % 2, h // Q_PER_KV))
        low.append(_to_half(tile, h % 2, 0))
    zpad = jnp.zeros((S_ROWS - n_real, LANES), F32)
    qrows = jnp.concatenate(by_kvh + [zpad], axis=0).astype(BF16)
    qlow = jnp.concatenate(low + [zpad], axis=0).astype(BF16)
    r_col = lax.broadcasted_iota(jnp.int32, (S_ROWS, 1), 0)
    q_of_r = r_col % ds
    h_of_r = r_col // ds
    sl_r = jnp.zeros((S_ROWS, 1), F32)
    for h in range(N_HEADS):
        sl_r = jnp.where(h_of_r == h, _slope(h), sl_r)
    return qrows, qlow, q_of_r, past + q_of_r, sl_r, h_of_r // Q_PER_KV


def _select_sample_kernel(q_ref, kc_ref, vc_ref, ocmp_ref, sel_ref, need_ref, *, past, ds, npp):
    n_real = N_HEADS * ds
    ncols = 2 * npp
    n_blocks = (past + ds + CMP_BLOCK - 1) // CMP_BLOCK
    _, qlow, _, qp_r, sl_r, kvh_r = _sample_rows(q_ref, ds, past)

    blk = _block_of_col(lax.broadcasted_iota(jnp.int32, (S_ROWS, ncols), 1), npp)
    s = jnp.where(kvh_r == 0, _dot_t(qlow, _cmp_operand(kc_ref[0, 0], npp)),
                  _dot_t(qlow, _cmp_operand(kc_ref[0, 1], npp)))
    dist_c = qp_r - (blk * CMP_BLOCK + (CMP_BLOCK - 1))
    mask_c = dist_c >= 0
    z = jnp.where(mask_c, s - sl_r * dist_c.astype(F32), NEG)
    mx = jnp.max(z, axis=-1, keepdims=True)
    e = jnp.where(mask_c, jnp.exp(z - mx), 0.0)
    den = jnp.sum(e, axis=-1, keepdims=True)
    p = e / jnp.where(den > 0.0, den, 1.0)
    pb = p.astype(BF16)
    o_cmp = jnp.where(kvh_r == 0, jnp.dot(pb, _cmp_operand(vc_ref[0, 0], npp), preferred_element_type=F32),
                      jnp.dot(pb, _cmp_operand(vc_ref[0, 1], npp), preferred_element_type=F32))

    kq = N_KV_HEADS * ds
    p_kv = []
    for kvh in range(N_KV_HEADS):
        acc = jnp.zeros((ds, ncols), F32)
        for g in range(Q_PER_KV):
            r0 = (kvh * Q_PER_KV + g) * ds
            acc = acc + p[r0:r0 + ds]
        p_kv.append(acc)
    p_kv = jnp.concatenate(p_kv, axis=0)
    blk2 = _block_of_col(lax.broadcasted_iota(jnp.int32, (kq, ncols), 1), npp)
    qp2 = past + lax.broadcasted_iota(jnp.int32, (kq, 1), 0) % ds
    cand = blk2 * CMP_BLOCK <= qp2
    forced = (blk2 == qp2 // CMP_BLOCK) | (blk2 == 0)
    score = jnp.where(forced, FORCE_SCORE, jnp.where(cand, p_kv, -1.0))
    rank = jnp.zeros((kq, ncols), jnp.int32)
    for n in range(n_blocks):
        c = (n % 2) * npp + n // 2
        col = score[:, c:c + 1]
        ahead = (col > score) | ((col == score) & (blk2 > n))
        rank = rank + ahead.astype(jnp.int32)
    sel2 = ((rank < TOP_K_BLOCKS) & (score > -0.5)).astype(F32)
    sel_rows = []
    for h in range(N_HEADS):
        kvh = h // Q_PER_KV
        sel_rows.append(sel2[kvh * ds:(kvh + 1) * ds])
    sel_rows.append(jnp.zeros((S_ROWS - n_real, ncols), F32))
    sel_rows = jnp.concatenate(sel_rows, axis=0)
    ocmp_ref[0] = o_cmp
    sel_ref[0] = sel_rows
    any_row = jnp.max(sel_rows, axis=0, keepdims=True)
    page_need = jnp.maximum(any_row[:, :npp], any_row[:, npp:])
    need_ref[0] = jnp.broadcast_to(page_need, (8, npp))


def _select_sample(q, kc, vc, *, past):
    b, ds, _ = q.shape
    npp = kc.shape[2]
    kern = functools.partial(_select_sample_kernel, past=past, ds=ds, npp=npp)
    per_b3 = lambda i: (i, 0, 0)
    per_b4 = lambda i: (i, 0, 0, 0)
    return pl.pallas_call(
        kern,
        grid=(b,),
        in_specs=[pl.BlockSpec((1, ds, ATTN_WIDTH), per_b3),
                  pl.BlockSpec((1, N_KV_HEADS, npp, LANES), per_b4),
                  pl.BlockSpec((1, N_KV_HEADS, npp, LANES), per_b4)],
        out_specs=(pl.BlockSpec((1, S_ROWS, LANES), per_b3), pl.BlockSpec((1, S_ROWS, 2 * npp), per_b3),
                   pl.BlockSpec((1, 8, npp), per_b3)),
        out_shape=(jax.ShapeDtypeStruct((b, S_ROWS, LANES), F32), jax.ShapeDtypeStruct((b, S_ROWS, 2 * npp), F32),
                   jax.ShapeDtypeStruct((b, 8, npp), F32)),
        compiler_params=pltpu.CompilerParams(dimension_semantics=("arbitrary",), vmem_limit_bytes=VMEM_LIMIT),
        name="select_sample",
    )(q, kc, vc)


def _attn_sample_kernel(pt_ref, order_ref, nn_ref, q_ref, gates_ref, ocmp_ref, sel_ref, kvs_new_ref, kvw_new_ref,
                        win_ref, cache_ref, a_ref, kbuf, zbuf, sem, *, n_pages, past, ds, npp):
    b = pl.program_id(0)
    nb = pl.num_programs(0)
    ncols = 2 * npp
    half_cols = KV_COLS // 2
    pages_per_chunk = S_CHUNK // PAGE

    def page_copy(row, j, slot):
        return pltpu.make_async_copy(
            cache_ref.at[pt_ref[row, order_ref[row, j]]], kbuf.at[slot, :, pl.ds(j * PAGE, PAGE)], sem.at[slot])

    def row_copies(row, slot, fn):
        def body(j, carry):
            fn(page_copy(row, j, slot))
            return carry
        lax.fori_loop(0, nn_ref[row], body, 0)

    @pl.when(b == 0)
    def _():
        kbuf[...] = jnp.zeros(kbuf.shape, F32)
        row_copies(0, 0, lambda cp: cp.start())

    slot = b % 2

    @pl.when(b + 1 < nb)
    def _():
        row_copies(b + 1, 1 - slot, lambda cp: cp.start())

    qrows, _, q_of_r, qp_r, sl_r, _ = _sample_rows(q_ref, ds, past)
    o_cmp = ocmp_ref[0]
    sel_rows = sel_ref[0]
    sel16 = sel_rows.astype(BF16)

    new_lane = lax.broadcasted_iota(jnp.int32, (1, LANES), 1)

    def new_scores(ref, extra_mask):
        k_new = ref[0, :half_cols, :].astype(BF16)
        s_new = jnp.dot(qrows, k_new, preferred_element_type=F32)
        dist = q_of_r - new_lane
        mask = (new_lane < ds) & (dist >= 0) & extra_mask
        return jnp.where(mask, s_new - sl_r * dist.astype(F32), NEG)

    def weighted_new(ref, e_new):
        return _dot_t(e_new.astype(BF16), ref[0, half_cols:, :].astype(BF16))

    row_copies(b, slot, lambda cp: cp.wait())
    n_need = nn_ref[b]
    n_chunks = (n_need + pages_per_chunk - 1) // pages_per_chunk

    def pass1(c, mrow):
        k0 = pl.multiple_of(c * S_CHUNK, S_CHUNK)
        kt = kbuf[slot, :half_cols, pl.ds(k0, S_CHUNK)].astype(BF16)
        st = jnp.dot(qrows, kt, preferred_element_type=F32)
        lane = lax.broadcasted_iota(jnp.int32, (1, S_CHUNK), 1)
        tpos = jnp.full((1, S_CHUNK), -PAGE * n_pages, jnp.int32)
        for jj in range(pages_per_chunk):
            j = c * pages_per_chunk + jj
            page = order_ref[b, jnp.minimum(j, n_pages - 1)]
            base = jnp.where(j < n_need, page * PAGE, -PAGE * n_pages) - jj * PAGE
            tpos = jnp.where(lane // PAGE == jj, base + lane, tpos)
        blk_c = _block_of_col(lax.broadcasted_iota(jnp.int32, (ncols, S_CHUNK), 0), npp)
        expand = (blk_c == jnp.broadcast_to(tpos, (ncols, S_CHUNK)) // CMP_BLOCK).astype(BF16)
        chosen = jnp.dot(sel16, expand, preferred_element_type=F32) > 0.5
        dist = qp_r - tpos
        zc = jnp.where(chosen & (dist >= 0), st - sl_r * dist.astype(F32), NEG)
        zbuf[c] = zc
        return jnp.maximum(mrow, jnp.max(zc, axis=-1, keepdims=True))

    mrow = lax.fori_loop(0, n_chunks, pass1, jnp.full((S_ROWS, 1), NEG, F32))
    c_last = ((past // CMP_BLOCK) % 2) * npp + (past // CMP_BLOCK) // 2
    z_new = new_scores(kvs_new_ref, sel_rows[:, c_last:c_last + 1] > 0.5)
    m_r = jnp.maximum(mrow, jnp.max(z_new, axis=-1, keepdims=True))

    def pass2(c, carry):
        acc, den_r = carry
        k0 = pl.multiple_of(c * S_CHUNK, S_CHUNK)
        zc = zbuf[c]
        ec = jnp.where(zc > 0.5 * NEG, jnp.exp(zc - m_r), 0.0)
        vt = kbuf[slot, half_cols:, pl.ds(k0, S_CHUNK)].astype(BF16)
        return acc + _dot_t(ec.astype(BF16), vt), den_r + jnp.sum(ec, axis=-1, keepdims=True)

    e_new = jnp.where(z_new > 0.5 * NEG, jnp.exp(z_new - m_r), 0.0)
    acc, den_r = lax.fori_loop(0, n_chunks, pass2,
                               (weighted_new(kvs_new_ref, e_new), jnp.sum(e_new, axis=-1, keepdims=True)))
    o_slc = acc / jnp.where(den_r > 0.0, den_r, 1.0)

    win_buf = win_ref.shape[2]
    kw = win_ref[0, :half_cols, :].astype(BF16)
    st = jnp.dot(qrows, kw, preferred_element_type=F32)
    kpos = past - win_buf + lax.broadcasted_iota(jnp.int32, (1, win_buf), 1)
    dist = qp_r - kpos
    mask = (dist >= 0) & (dist < WINDOW) & (kpos >= 0)
    z_w = jnp.where(mask, st - sl_r * dist.astype(F32), NEG)
    zw_new = new_scores(kvw_new_ref, True)
    m_w = jnp.maximum(jnp.max(z_w, axis=-1, keepdims=True), jnp.max(zw_new, axis=-1, keepdims=True))
    e_w = jnp.where(z_w > 0.5 * NEG, jnp.exp(z_w - m_w), 0.0)
    ew_new = jnp.where(zw_new > 0.5 * NEG, jnp.exp(zw_new - m_w), 0.0)
    acc_w = _dot_t(e_w.astype(BF16), win_ref[0, half_cols:, :].astype(BF16)) + weighted_new(kvw_new_ref, ew_new)
    den_w = jnp.sum(e_w, axis=-1, keepdims=True) + jnp.sum(ew_new, axis=-1, keepdims=True)
    o_win = acc_w / jnp.where(den_w > 0.0, den_w, 1.0)

    gates = gates_ref[0]
    for pair in range(N_HEADS // 2):
        kvh = (2 * pair) // Q_PER_KV
        e_sl = slice(2 * pair * ds, (2 * pair + 1) * ds)
        o_sl = slice((2 * pair + 1) * ds, (2 * pair + 2) * ds)
        shape = (ds, LANES)
        tile = (_gate_tile(gates, pair, 0, shape) * _pair_tile(o_cmp[e_sl], o_cmp[o_sl], 0)
                + _gate_tile(gates, pair, 1, shape) * _pair_tile(o_slc[e_sl], o_slc[o_sl], kvh)
                + _gate_tile(gates, pair, 2, shape) * _pair_tile(o_win[e_sl], o_win[o_sl], kvh))
        a_ref[0, :, pair * LANES:(pair + 1) * LANES] = tile.astype(BF16)


def _attn_sample(page_table, order, n_need, q, gates, o_cmp, sel_rows, kvs_new, kvw_new, state_win_t, cache_pages,
                 *, past):
    b, n_pages = page_table.shape
    ds = q.shape[1]
    npp = sel_rows.shape[2] // 2
    win_buf = state_win_t.shape[2]
    n_keys = n_pages * PAGE
    kern = functools.partial(_attn_sample_kernel, n_pages=n_pages, past=past, ds=ds, npp=npp)
    per_b = lambda i, pt, od, nn: (i, 0, 0)
    grid_spec = pltpu.PrefetchScalarGridSpec(
        num_scalar_prefetch=3,
        grid=(b,),
        in_specs=[
            pl.BlockSpec((1, ds, ATTN_WIDTH), per_b),
            pl.BlockSpec((1, ds, GATE_PAD), per_b),
            pl.BlockSpec((1, S_ROWS, LANES), per_b),
            pl.BlockSpec((1, S_ROWS, 2 * npp), per_b),
            pl.BlockSpec((1, KV_COLS, LANES), per_b),
            pl.BlockSpec((1, KV_COLS, LANES), per_b),
            pl.BlockSpec((1, KV_COLS, win_buf), per_b),
            pl.BlockSpec(memory_space=pl.ANY),
        ],
        out_specs=pl.BlockSpec((1, ds, ATTN_WIDTH), per_b),
        scratch_shapes=[
            pltpu.VMEM((2, KV_COLS, n_keys), F32),
            pltpu.VMEM((n_keys // S_CHUNK, S_ROWS, S_CHUNK), F32),
            pltpu.SemaphoreType.DMA((2,)),
        ],
    )
    return pl.pallas_call(
        kern,
        grid_spec=grid_spec,
        out_shape=jax.ShapeDtypeStruct((b, ds, ATTN_WIDTH), BF16),
        compiler_params=pltpu.CompilerParams(
            dimension_semantics=("arbitrary",), vmem_limit_bytes=VMEM_LIMIT),
        name="attn_sample",
    )(page_table, order, n_need, q, gates, o_cmp, sel_rows, kvs_new, kvw_new, state_win_t, cache_pages)


X_TILES = D_MODEL // LANES
PAY_SUB = X_TILES + 1
GROUP_LANE = 0


def _rows_to_tiles(ref, lo, val):
    for k in range(val.shape[1] // LANES):
        ref[:, lo + k, :] = val[:, k * LANES:(k + 1) * LANES]


def _tiles_to_rows(ref, lo, n):
    return jnp.concatenate([ref[:, lo + k, :] for k in range(n)], axis=1)


def _ffn_kernel(x_ref, a_ref, m_ref, wo_ref, g2_ref, wr_ref, *out_refs, packed):
    h = (x_ref[...]
         + jnp.dot(a_ref[...], wo_ref[:ATTN_WIDTH, :], preferred_element_type=F32)
         + jnp.dot(m_ref[...], wo_ref[ATTN_WIDTH:, :], preferred_element_type=F32))
    ms = jnp.mean(h * h, axis=-1, keepdims=True)
    n2f = (h * lax.rsqrt(ms + EPS)) * g2_ref[...]
    n2 = n2f.astype(BF16)
    logits = jnp.dot(n2, wr_ref[...], preferred_element_type=F32)
    lane = lax.broadcasted_iota(jnp.int32, logits.shape, 1)
    big = jnp.int32(LANES)

    def masked_softmax(mask):
        zz = jnp.where(mask, logits, NEG)
        mx = jnp.max(zz, axis=-1, keepdims=True)
        ee = jnp.where(mask, jnp.exp(zz - mx), 0.0)
        return ee / jnp.sum(ee, axis=-1, keepdims=True)

    def first_argmax(vals, mask):
        v = jnp.max(jnp.where(mask, vals, -1.0), axis=-1, keepdims=True)
        idx = jnp.min(jnp.where(mask & (vals == v), lane, big), axis=-1, keepdims=True)
        return v, idx

    is_g = lane < N_EXPERT_GROUPS
    pg = masked_softmax(is_g)
    g_val, g_idx = first_argmax(pg, is_g)
    e_lane = lane - ROUTER_LANE0
    in_grp = (e_lane >= 0) & (e_lane < N_EXPERTS) & (e_lane // EXPERTS_PER_GROUP == g_idx)
    pe = masked_softmax(in_grp)
    v1, i1 = first_argmax(pe, in_grp)
    rest = in_grp & (lane != i1)
    v2, i2 = first_argmax(pe, rest)
    scale = g_val / (v1 + v2)
    gate = jnp.where(lane == i1, v1 * scale, jnp.where(lane == i2, v2 * scale, 0.0))
    gate = jnp.where(lane == GROUP_LANE, g_idx.astype(F32), gate)
    if packed:
        h_ref, pay_ref, gate_ref = out_refs
        h_ref[...] = h
        gate_ref[...] = gate
        _rows_to_tiles(pay_ref, 0, n2f)
        pay_ref[:, X_TILES, :] = gate
    else:
        h_ref, n2_ref, gate_ref = out_refs
        h_ref[...] = h
        n2_ref[...] = n2
        gate_ref[...] = gate


def _ffn(x2d, a2d, m2d, w_out16, g2, w_router16, tm, packed):
    t = x2d.shape[0]
    row = lambda i: (i, 0)
    const = lambda i: (0, 0)
    if packed:
        out_specs = (pl.BlockSpec((tm, D_MODEL), row), pl.BlockSpec((tm, PAY_SUB, LANES), lambda i: (i, 0, 0)),
                     pl.BlockSpec((tm, LANES), row))
        out_shape = (jax.ShapeDtypeStruct((t, D_MODEL), F32), jax.ShapeDtypeStruct((t, PAY_SUB, LANES), F32),
                     jax.ShapeDtypeStruct((t, LANES), F32))
    else:
        out_specs = (pl.BlockSpec((tm, D_MODEL), row), pl.BlockSpec((tm, D_MODEL), row),
                     pl.BlockSpec((tm, LANES), row))
        out_shape = (jax.ShapeDtypeStruct((t, D_MODEL), F32), jax.ShapeDtypeStruct((t, D_MODEL), BF16),
                     jax.ShapeDtypeStruct((t, LANES), F32))
    return pl.pallas_call(
        functools.partial(_ffn_kernel, packed=packed),
        grid=(t // tm,),
        in_specs=[
            pl.BlockSpec((tm, D_MODEL), row),
            pl.BlockSpec((tm, ATTN_WIDTH), row),
            pl.BlockSpec((tm, POOL_WIDTH), row),
            pl.BlockSpec((D_MODEL, D_MODEL), const),
            pl.BlockSpec((1, D_MODEL), const),
            pl.BlockSpec((D_MODEL, LANES), const),
        ],
        out_specs=out_specs,
        out_shape=out_shape,
        compiler_params=pltpu.CompilerParams(
            dimension_semantics=("arbitrary",), vmem_limit_bytes=VMEM_LIMIT),
        name="ffn",
    )(x2d, a2d, m2d, w_out16, g2, w_router16)


MOE_ROWS = 512
ROW_CHUNK = 512
PLAN_TILE = 1024
PLAN_COLS = 8


def _plan_kernel(gate_ref, tri_ref, out_ref, cnt_ref, carry):
    @pl.when(pl.program_id(0) == 0)
    def _():
        carry[...] = jnp.zeros(carry.shape, F32)

    gl = gate_ref[...]
    lane = lax.broadcasted_iota(jnp.int32, gl.shape, 1)
    gid = gl[:, GROUP_LANE:GROUP_LANE + 1]
    onehot = jnp.where((lane < N_EXPERT_GROUPS) & (lane.astype(F32) == gid), 1.0, 0.0)
    before = jnp.dot(tri_ref[...], onehot.astype(BF16), preferred_element_type=F32) + carry[0:1, :]
    rank = jnp.sum(onehot * before, axis=-1, keepdims=True)
    out_ref[...] = jnp.where(lane == 0, rank, jnp.where(lane == 1, gid, 0.0))[:, :PLAN_COLS]
    carry[0:1, :] = carry[0:1, :] + jnp.sum(onehot, axis=0, keepdims=True)
    cnt_ref[...] = carry[...]


def _plan(gate):
    t = gate.shape[0]
    tri = jnp.asarray(np.tril(np.ones((PLAN_TILE, PLAN_TILE), np.float32), -1), dtype=BF16)
    return pl.pallas_call(
        _plan_kernel,
        grid=(t // PLAN_TILE,),
        in_specs=[pl.BlockSpec((PLAN_TILE, LANES), lambda i: (i, 0)),
                  pl.BlockSpec((PLAN_TILE, PLAN_TILE), lambda i: (0, 0))],
        out_specs=(pl.BlockSpec((PLAN_TILE, PLAN_COLS), lambda i: (i, 0)), pl.BlockSpec((8, LANES), lambda i: (0, 0))),
        out_shape=(jax.ShapeDtypeStruct((t, PLAN_COLS), F32), jax.ShapeDtypeStruct((8, LANES), F32)),
        scratch_shapes=[pltpu.VMEM((8, LANES), F32)],
        compiler_params=pltpu.CompilerParams(
            dimension_semantics=("arbitrary",), vmem_limit_bytes=VMEM_LIMIT),
        name="moe_plan",
    )(gate, tri)


def _scatter_kernel(meta_ref, slot_ref, pay_ref, xs_ref, zrow, sem, zsem):
    c = pl.program_id(0)

    @pl.when(c == 0)
    def _():
        zrow[...] = jnp.zeros(zrow.shape, F32)
        for g in range(N_EXPERT_GROUPS + 1):
            lo = meta_ref[g]
            hi = meta_ref[N_EXPERT_GROUPS + 1 + g]

            def zero_start(r, carry):
                pltpu.make_async_copy(zrow.at[pl.ds(0, 1)], xs_ref.at[pl.ds(r, 1)], zsem.at[0]).start()
                return carry

            def zero_wait(r, carry):
                pltpu.make_async_copy(zrow.at[pl.ds(0, 1)], xs_ref.at[pl.ds(0, 1)], zsem.at[0]).wait()
                return carry

            lax.fori_loop(lo, hi, zero_start, 0)
            lax.fori_loop(lo, hi, zero_wait, 0)

    def issue(u, carry):
        pltpu.make_async_copy(pay_ref.at[pl.ds(u, 1)], xs_ref.at[pl.ds(slot_ref[0, 0, u], 1)], sem.at[0]).start()
        return carry

    lax.fori_loop(0, ROW_CHUNK, issue, 0, unroll=8)
    pltpu.make_async_copy(pay_ref, xs_ref.at[pl.ds(0, ROW_CHUNK)], sem.at[0]).wait()


def _scatter(meta, slot3, pay, n_slots):
    t = pay.shape[0]
    return pl.pallas_call(
        _scatter_kernel,
        grid_spec=pltpu.PrefetchScalarGridSpec(
            num_scalar_prefetch=1,
            grid=(t // ROW_CHUNK,),
            in_specs=[pl.BlockSpec((1, 1, ROW_CHUNK), lambda c, meta: (c, 0, 0), memory_space=pltpu.SMEM),
                      pl.BlockSpec((ROW_CHUNK, PAY_SUB, LANES), lambda c, meta: (c, 0, 0))],
            out_specs=pl.BlockSpec(memory_space=pl.ANY),
            scratch_shapes=[pltpu.VMEM((1, PAY_SUB, LANES), F32), pltpu.SemaphoreType.DMA((1,)),
                            pltpu.SemaphoreType.DMA((1,))],
        ),
        out_shape=jax.ShapeDtypeStruct((n_slots, PAY_SUB, LANES), F32),
        compiler_params=pltpu.CompilerParams(dimension_semantics=("arbitrary",), vmem_limit_bytes=VMEM_LIMIT),
        name="moe_scatter",
    )(meta, slot3, pay)


def _gather_kernel(slot_ref, nslot_ref, h_ref, ys_ref, y_ref, buf, sem):
    c = pl.program_id(0)
    nc = pl.num_programs(0)

    def fetch(ref, slot):
        def issue(u, carry):
            pltpu.make_async_copy(ys_ref.at[pl.ds(ref[0, 0, u], 1)], buf.at[slot, pl.ds(u, 1)], sem.at[slot]).start()
            return carry
        lax.fori_loop(0, ROW_CHUNK, issue, 0, unroll=8)

    @pl.when(c == 0)
    def _():
        fetch(slot_ref, 0)

    @pl.when(c + 1 < nc)
    def _():
        fetch(nslot_ref, (c + 1) % 2)

    cur = c % 2
    pltpu.make_async_copy(ys_ref.at[pl.ds(0, ROW_CHUNK)], buf.at[cur], sem.at[cur]).wait()
    y_ref[...] = h_ref[...] + jnp.concatenate([buf[cur, :, k, :] for k in range(X_TILES)], axis=1)


def _gather(slot3, h, ys):
    t = h.shape[0]
    nc = t // ROW_CHUNK
    return pl.pallas_call(
        _gather_kernel,
        grid=(nc,),
        in_specs=[pl.BlockSpec((1, 1, ROW_CHUNK), lambda c: (c, 0, 0), memory_space=pltpu.SMEM),
                  pl.BlockSpec((1, 1, ROW_CHUNK), lambda c: (jnp.minimum(c + 1, nc - 1), 0, 0),
                               memory_space=pltpu.SMEM),
                  pl.BlockSpec((ROW_CHUNK, D_MODEL), lambda c: (c, 0)),
                  pl.BlockSpec(memory_space=pl.ANY)],
        out_specs=pl.BlockSpec((ROW_CHUNK, D_MODEL), lambda c: (c, 0)),
        out_shape=jax.ShapeDtypeStruct((t, D_MODEL), F32),
        scratch_shapes=[pltpu.VMEM((2, ROW_CHUNK, X_TILES, LANES), F32), pltpu.SemaphoreType.DMA((2,))],
        compiler_params=pltpu.CompilerParams(dimension_semantics=("arbitrary",), vmem_limit_bytes=VMEM_LIMIT),
        name="moe_gather",
    )(slot3, slot3, h, ys)


def _group_moe_kernel(tg_ref, nu_ref, xs_ref, wg_ref, wu_ref, wd_ref, ys_ref):
    j = pl.program_id(0)

    @pl.when(j < nu_ref[0])
    def _():
        g = tg_ref[j]
        x = _tiles_to_rows(xs_ref, 0, X_TILES).astype(BF16)
        gate = xs_ref[:, X_TILES, :]
        lane = lax.broadcasted_iota(jnp.int32, gate.shape, 1)
        y = jnp.zeros((MOE_ROWS, D_MODEL), F32)
        for e in range(EXPERTS_PER_GROUP):
            gu = jnp.dot(x, wg_ref[e], preferred_element_type=F32)
            up = jnp.dot(x, wu_ref[e], preferred_element_type=F32)
            he = (gu * (1.0 / (1.0 + jnp.exp(-gu)))) * up
            out = jnp.dot(he.astype(BF16), wd_ref[e], preferred_element_type=F32)
            col = ROUTER_LANE0 + g * EXPERTS_PER_GROUP + e
            y = y + jnp.sum(jnp.where(lane == col, gate, 0.0), axis=-1, keepdims=True) * out
        _rows_to_tiles(ys_ref, 0, y)

    @pl.when(j >= nu_ref[0])
    def _():
        ys_ref[...] = jnp.zeros(ys_ref.shape, F32)


def _group_moe(tile_group, n_used, xs, wg16, wu16, wd16):
    n_slots = xs.shape[0]
    wspec = lambda shape: pl.BlockSpec((EXPERTS_PER_GROUP,) + shape, lambda j, tg, nu: (tg[j], 0, 0))
    return pl.pallas_call(
        _group_moe_kernel,
        grid_spec=pltpu.PrefetchScalarGridSpec(
            num_scalar_prefetch=2,
            grid=(n_slots // MOE_ROWS,),
            in_specs=[pl.BlockSpec((MOE_ROWS, PAY_SUB, LANES), lambda j, tg, nu: (j, 0, 0)),
                      wspec((D_MODEL, D_EXPERT)), wspec((D_MODEL, D_EXPERT)), wspec((D_EXPERT, D_MODEL))],
            out_specs=pl.BlockSpec((MOE_ROWS, X_TILES, LANES), lambda j, tg, nu: (j, 0, 0)),
        ),
        out_shape=jax.ShapeDtypeStruct((n_slots, X_TILES, LANES), F32),
        compiler_params=pltpu.CompilerParams(dimension_semantics=("arbitrary",), vmem_limit_bytes=VMEM_LIMIT),
        name="moe_group",
    )(tile_group, n_used, xs, wg16, wu16, wd16)


def _routed_moe(h, pay, gate, wg16, wu16, wd16):
    t = pay.shape[0]
    n_slots = t + N_EXPERT_GROUPS * MOE_ROWS
    n_tiles = n_slots // MOE_ROWS
    plan, counts = _plan(gate)
    rank = plan[:, 0].astype(jnp.int32)
    gid = plan[:, 1].astype(jnp.int32)
    cnt = counts[0, :N_EXPERT_GROUPS].astype(jnp.int32)
    padded = -(-cnt // MOE_ROWS) * MOE_ROWS
    ends = jnp.cumsum(padded)
    off = ends - padded
    slot = (off[gid] + rank).reshape(t // ROW_CHUNK, 1, ROW_CHUNK)
    tile_start = jnp.arange(n_tiles, dtype=jnp.int32) * MOE_ROWS
    tile_group = jnp.minimum(jnp.sum(tile_start[:, None] >= ends[None, :], axis=1), N_EXPERT_GROUPS - 1)
    n_used = (ends[-1] // MOE_ROWS).reshape(1)
    meta = jnp.concatenate([off + cnt, ends[-1:], ends, jnp.full((1,), n_slots, jnp.int32)]).astype(jnp.int32)
    xs = _scatter(meta, slot, pay, n_slots)
    ys = _group_moe(tile_group.astype(jnp.int32), n_used.astype(jnp.int32), xs, wg16, wu16, wd16)
    return _gather(slot, h, ys)


def _moe_kernel(n2_ref, gate_ref, h_ref, wg_ref, wu_ref, wd_ref, y_ref):
    e = pl.program_id(1)

    @pl.when(e == 0)
    def _():
        y_ref[...] = h_ref[...]

    n2 = n2_ref[...]
    gu = jnp.dot(n2, wg_ref[0], preferred_element_type=F32)
    up = jnp.dot(n2, wu_ref[0], preferred_element_type=F32)
    he = (gu * (1.0 / (1.0 + jnp.exp(-gu)))) * up
    out = jnp.dot(he.astype(BF16), wd_ref[0], preferred_element_type=F32)
    lane = lax.broadcasted_iota(jnp.int32, gate_ref.shape, 1)
    gcol = jnp.sum(jnp.where(lane == e + ROUTER_LANE0, gate_ref[...], 0.0), axis=-1, keepdims=True)
    y_ref[...] += gcol * out


def _moe(n2, gate, h, wg16, wu16, wd16, tm):
    t = n2.shape[0]
    row = lambda i, e: (i, 0)
    return pl.pallas_call(
        _moe_kernel,
        grid=(t // tm, N_EXPERTS),
        in_specs=[
            pl.BlockSpec((tm, D_MODEL), row),
            pl.BlockSpec((tm, LANES), row),
            pl.BlockSpec((tm, D_MODEL), row),
            pl.BlockSpec((1, D_MODEL, D_EXPERT), lambda i, e: (e, 0, 0)),
            pl.BlockSpec((1, D_MODEL, D_EXPERT), lambda i, e: (e, 0, 0)),
            pl.BlockSpec((1, D_EXPERT, D_MODEL), lambda i, e: (e, 0, 0)),
        ],
        out_specs=pl.BlockSpec((tm, D_MODEL), row),
        out_shape=jax.ShapeDtypeStruct((t, D_MODEL), F32),
        compiler_params=pltpu.CompilerParams(
            dimension_semantics=("arbitrary", "arbitrary"), vmem_limit_bytes=VMEM_LIMIT),
        name="moe",
    )(n2, gate, h, wg16, wu16, wd16)


PROJ_TILE = 512
MOE_TILE = 1024
PROMPT_NPP = LANES // 2
SAMPLE_NPP = LANES


def _round_up(n, mult):
    return -(-n // mult) * mult


def _feature_major(x5):
    b, t = x5.shape[:2]
    return jnp.transpose(x5, (0, 2, 3, 4, 1)).reshape(b, KV_COLS, t)


def _token_major(xt):
    b, _, t = xt.shape
    return jnp.transpose(xt.reshape(b, 2, N_KV_HEADS, HEAD_DIM, t), (0, 4, 1, 2, 3))


def _pad_pages(x, npp_from, npp_to):
    b = x.shape[0]
    x = x.reshape(b, N_KV_HEADS, npp_from, LANES)
    return jnp.concatenate([x, jnp.zeros((b, N_KV_HEADS, npp_to - npp_from, LANES), x.dtype)], axis=2)


def _expand_const(n_chunks, npp):
    col = np.arange(2 * npp)
    blk = 2 * (col % npp) + col // npp
    tok_blk = np.arange(n_chunks * K_CHUNK) // CMP_BLOCK
    e = (blk[None, :, None] == tok_blk.reshape(n_chunks, 1, K_CHUNK)).astype(np.float32)
    return jnp.asarray(e, dtype=BF16)


def _gate_place_const():
    gp = np.zeros((GATE_PAD, 3 * ATTN_WIDTH), np.float32)
    for h in range(N_HEADS):
        for c in range(3):
            gp[3 * h + c, c * ATTN_WIDTH + h * HEAD_DIM:c * ATTN_WIDTH + (h + 1) * HEAD_DIM] = 1.0
    return jnp.asarray(gp, dtype=BF16)


def kernel(x_prompt, x_sample, cache_cmp_kv, cache_slc_kv, state_win_kv, state_pool, page_table, norm1_g, w_in, q_norm_g, k_norm_cmp_g, k_norm_slc_g, k_norm_win_g, cmp_pos_emb, w_cmp_k, w_cmp_v, pool_w, pool_scale, w_out, norm2_g, w_router_group, w_router_expert, w_gate, w_up, w_down):
    b, s, _ = x_prompt.shape
    db, ds, _ = x_sample.shape
    n_pool, page_rows = cache_cmp_kv.shape[:2]
    assert page_rows == PAGE and s % PAGE == 0
    past = page_table.shape[1] * page_rows

    kv0 = ATTN_WIDTH
    kv1 = kv0 + KV_ROWS
    w_row = jnp.concatenate(
        [w_in[:, :kv0], w_in[:, kv1:kv1 + GATE_COLS], jnp.zeros((D_MODEL, GATE_PAD - GATE_COLS), w_in.dtype),
         w_in[:, kv1 + GATE_COLS:]], axis=1).astype(BF16)
    w_kv = w_in[:, kv0:kv1].T.astype(BF16)
    g1 = norm1_g[None, :]
    g2 = norm2_g[None, :]
    two = lambda g: jnp.tile(g, 2)[None, :]
    kg = jnp.stack([k_norm_slc_g, k_norm_win_g])[:, :, None]
    pos_t = jnp.tile(cmp_pos_emb.T, (1, 2))
    zb = jnp.zeros((HEAD_DIM, CMP_BLOCK, HEAD_DIM), F32)

    def blockdiag(w):
        wt = jnp.transpose(w, (1, 0, 2))
        return jnp.concatenate([jnp.concatenate([wt, zb], axis=2), jnp.concatenate([zb, wt], axis=2)], axis=1)

    bdk, bdv = blockdiag(w_cmp_k), blockdiag(w_cmp_v)
    zq = jnp.zeros_like(bdk)
    bd = jnp.concatenate([jnp.concatenate([bdk, zq], axis=2), jnp.concatenate([zq, bdv], axis=2)],
                         axis=1).astype(BF16)
    w_router = jnp.concatenate(
        [w_router_group, w_router_expert,
         jnp.zeros((D_MODEL, LANES - N_EXPERT_GROUPS - N_EXPERTS), w_router_group.dtype)], axis=1).astype(BF16)
    w_out16 = w_out.astype(BF16)
    pool_w16 = pool_w.astype(BF16)
    wg16, wu16, wd16 = w_gate.astype(BF16), w_up.astype(BF16), w_down.astype(BF16)
    ps = pool_scale[None, :]

    def ffn_moe(x2d, a2d, m2d, tm_ffn, tm_moe, routed):
        if routed:
            return _routed_moe(*_ffn(x2d, a2d, m2d, w_out16, g2, w_router, tm_ffn, True), wg16, wu16, wd16)
        h, n2, gate = _ffn(x2d, a2d, m2d, w_out16, g2, w_router, tm_ffn, False)
        return _moe(n2, gate, h, wg16, wu16, wd16, tm_moe)

    q, gates, u, kvc_t, kvs_t, kvw_t, ks16, vsx, kw16, vwx = _proj(
        x_prompt, g1, w_row, w_kv, two(q_norm_g), kg, PROJ_TILE)
    m_p = _pool(u, pool_w16, ps, s)
    pages_p = s // PAGE
    pt_p = jnp.zeros((b, pages_p), jnp.int32)
    dummy_tail = jnp.zeros((b, KV_COLS, LANES), F32)
    kc_p, vc_p = _compress(pt_p, kvc_t, dummy_tail, pos_t, bd, two(k_norm_cmp_g),
                           npp=pages_p, has_tail=False, paged=False)
    kc_p = _pad_pages(kc_p, pages_p, PROMPT_NPP)
    vc_p = _pad_pages(vc_p, pages_p, PROMPT_NPP)
    a_p = _attn_prompt(q, gates, kc_p, vc_p, ks16, vsx, kw16, vwx, _expand_const(s // K_CHUNK, PROMPT_NPP),
                       _gate_place_const(), pages_p)
    y_p = ffn_moe(x_prompt.reshape(b * s, D_MODEL), a_p.reshape(b * s, ATTN_WIDTH),
                  m_p.reshape(b * s, POOL_WIDTH), PROJ_TILE, MOE_TILE, True)

    ts = db * ds
    q_s, gates_s, u_s, kvc_st, kvs_st, kvw_st, _, _, _, _ = _proj(
        x_sample.reshape(1, ts, D_MODEL), g1, w_row, w_kv, two(q_norm_g), kg, ts)
    u_s = u_s.reshape(db, ds, POOL_WIDTH)
    pool_ext = jnp.concatenate([state_pool, u_s], axis=1)
    n_ext = pool_ext.shape[1]
    lead = _round_up(n_ext, 8) - n_ext
    pool_in = jnp.concatenate([jnp.zeros((db, lead, POOL_WIDTH), F32), pool_ext], axis=1)
    n_in = lead + n_ext
    m_s = _pool(pool_in.reshape(1, db * n_in, POOL_WIDTH), pool_w16, ps, n_in).reshape(db, n_in, POOL_WIDTH)[:, n_in - ds:]

    def new_rows_t(xt):
        x = jnp.transpose(xt.reshape(KV_COLS, db, ds), (1, 0, 2))
        return jnp.concatenate([x, jnp.zeros((db, KV_COLS, LANES - ds), F32)], axis=2)

    kvc_new, kvs_new, kvw_new = new_rows_t(kvc_st), new_rows_t(kvs_st), new_rows_t(kvw_st)
    n_pages = page_table.shape[1]
    npp_c = _round_up(n_pages + 1, 8)
    cmp_pages = _feature_major(cache_cmp_kv)
    slc_pages = _feature_major(cache_slc_kv)
    kc_s, vc_s = _compress(page_table, cmp_pages, kvc_new, pos_t, bd, two(k_norm_cmp_g),
                           npp=npp_c, has_tail=True, paged=True)
    kc_s = _pad_pages(kc_s, npp_c, SAMPLE_NPP)
    vc_s = _pad_pages(vc_s, npp_c, SAMPLE_NPP)
    win_t = _feature_major(state_win_kv)
    q_s3 = q_s.reshape(db, ds, ATTN_WIDTH)
    o_cmp_s, sel_s, need_s = _select_sample(q_s3, kc_s, vc_s, past=past)
    need = need_s[:, 0, :n_pages] > 0.5
    order = jnp.argsort(jnp.logical_not(need), axis=1, stable=True).astype(jnp.int32)
    n_need = jnp.sum(need, axis=1).astype(jnp.int32)
    a_s = _attn_sample(page_table, order, n_need, q_s3, gates_s.reshape(db, ds, GATE_PAD), o_cmp_s, sel_s,
                       kvs_new, kvw_new, win_t, slc_pages, past=past)
    y_s = ffn_moe(x_sample.reshape(ts, D_MODEL), a_s.reshape(ts, ATTN_WIDTH), m_s.reshape(ts, POOL_WIDTH), ts, ts, False)

    win_keep = min(WINDOW, s)
    sample5 = lambda xt: jnp.transpose(xt.reshape(2, N_KV_HEADS, HEAD_DIM, db, ds), (3, 4, 0, 1, 2))
    win_ctx_t = jnp.concatenate([win_t, kvw_new[:, :, :ds]], axis=2)
    return (y_p.reshape(b, s, D_MODEL), y_s.reshape(db, ds, D_MODEL),
            _token_major(kvc_t), sample5(kvc_st),
            _token_major(kvs_t), sample5(kvs_st),
            _token_major(kvw_t[:, :, s - win_keep:]), _token_major(win_ctx_t[:, :, ds:]),
            u[:, s - POOL_BUF:], pool_ext[:, ds:])
```

```python
import functools

import jax
import jax.numpy as jnp
import numpy as np
from jax import lax
from jax.experimental import pallas as pl
from jax.experimental.pallas import tpu as pltpu

F32 = jnp.float32
BF16 = jnp.bfloat16

D_MODEL = 1024
N_HEADS = 8
HEAD_DIM = 64
N_KV_HEADS = 2
Q_PER_KV = N_HEADS // N_KV_HEADS
ATTN_WIDTH = N_HEADS * HEAD_DIM
KV_COLS = 2 * N_KV_HEADS * HEAD_DIM
GATE_COLS = 3 * N_HEADS
POOL_WIDTH = D_MODEL - ATTN_WIDTH
POOL_WINDOWS = (2, 4, 8, 16)
POOL_GROUP_WIDTH = POOL_WIDTH // len(POOL_WINDOWS)
POOL_BUF = max(POOL_WINDOWS) - 1
CMP_BLOCK = 64
TOP_K_BLOCKS = 16
WINDOW = 512
FORCE_SCORE = 1.0e4
N_EXPERT_GROUPS = 4
EXPERTS_PER_GROUP = 4
N_EXPERTS = N_EXPERT_GROUPS * EXPERTS_PER_GROUP
D_EXPERT = 512
EPS = 1e-6
NEG = -1e30
SCALE = HEAD_DIM ** -0.5

LANES = 128
PAGE = 2 * CMP_BLOCK
GATE_PAD = LANES
ROW_COLS = ATTN_WIDTH + GATE_PAD + POOL_WIDTH
OFF_GATE = ATTN_WIDTH
OFF_U = OFF_GATE + GATE_PAD
KV_ROWS = 3 * KV_COLS
ROUTER_LANE0 = N_EXPERT_GROUPS
Q_TILE = 256
K_CHUNK = 256
VMEM_LIMIT = 56 * 1024 * 1024


def _slope(h):
    return float(2.0 ** (-8.0 * (h + 1) / N_HEADS))


def _half_group_norm(v, g):
    lane = lax.broadcasted_iota(jnp.int32, v.shape, 1)
    lo = lane < HEAD_DIM
    v2 = v * v
    s_lo = jnp.sum(jnp.where(lo, v2, 0.0), axis=-1, keepdims=True)
    s_hi = jnp.sum(jnp.where(lo, 0.0, v2), axis=-1, keepdims=True)
    r_lo = lax.rsqrt(s_lo * (1.0 / HEAD_DIM) + EPS)
    r_hi = lax.rsqrt(s_hi * (1.0 / HEAD_DIM) + EPS)
    return (v * jnp.where(lo, r_lo, r_hi)) * g


def _dot_t(a, b):
    return lax.dot_general(a, b, (((1,), (1,)), ((), ())), preferred_element_type=F32)


def _proj_kernel(x_ref, g1_ref, wr_ref, wkv_ref, qg_ref, kg_ref,
                 q_ref, gates_ref, u_ref, kvc_ref, kvs_ref, kvw_ref, ks16_ref, vsx_ref, kw16_ref, vwx_ref):
    x = x_ref[0]
    tm = x.shape[0]
    ms = jnp.mean(x * x, axis=-1, keepdims=True)
    n = ((x * lax.rsqrt(ms + EPS)) * g1_ref[...]).astype(BF16)
    p = jnp.dot(n, wr_ref[...], preferred_element_type=F32)
    pt = _dot_t(wkv_ref[...], n)
    for t in range(ATTN_WIDTH // LANES):
        sl = slice(t * LANES, (t + 1) * LANES)
        q_ref[0, :, sl] = _half_group_norm(p[:, sl], qg_ref[...]).astype(BF16)
    gates_ref[0] = 1.0 / (1.0 + jnp.exp(-p[:, OFF_GATE:OFF_GATE + GATE_PAD]))
    u_ref[0] = p[:, OFF_U:OFF_U + POOL_WIDTH]
    kvc_ref[0] = pt[:KV_COLS]
    half = KV_COLS // 2
    lane = lax.broadcasted_iota(jnp.int32, (tm, LANES), 1)
    for bi, (out32, k16, vx16) in enumerate(((kvs_ref, ks16_ref, vsx_ref), (kvw_ref, kw16_ref, vwx_ref))):
        off = (bi + 1) * KV_COLS
        heads = []
        for hh in range(N_KV_HEADS):
            kh = pt[off + hh * HEAD_DIM:off + (hh + 1) * HEAD_DIM]
            msk = jnp.mean(kh * kh, axis=0, keepdims=True)
            heads.append((kh * lax.rsqrt(msk + EPS)) * kg_ref[bi])
        kn = jnp.concatenate(heads, axis=0)
        v = pt[off + half:off + KV_COLS]
        out32[0, :half, :] = kn
        out32[0, half:, :] = v
        vt = v.T
        vx = (jnp.where(lane < HEAD_DIM, vt, 1.0), jnp.where(lane < HEAD_DIM, pltpu.roll(vt, HEAD_DIM, axis=1), 1.0))
        for c in range(tm // K_CHUNK):
            cs = slice(c * K_CHUNK, (c + 1) * K_CHUNK)
            k16[0, c] = kn[:, cs].astype(BF16)
            for hh in range(N_KV_HEADS):
                vx16[0, c, hh] = vx[hh][cs].astype(BF16)


def _proj(x3d, g1, w_row, w_kv, qg, kg, tm):
    b, s, _ = x3d.shape
    tok = lambda i, j: (i, j, 0)
    feat = lambda i, j: (i, 0, j)
    const2 = lambda i, j: (0, 0)
    nck = tm // K_CHUNK
    out_shape = (
        jax.ShapeDtypeStruct((b, s, ATTN_WIDTH), BF16),
        jax.ShapeDtypeStruct((b, s, GATE_PAD), F32),
        jax.ShapeDtypeStruct((b, s, POOL_WIDTH), F32),
        jax.ShapeDtypeStruct((b, KV_COLS, s), F32),
        jax.ShapeDtypeStruct((b, KV_COLS, s), F32),
        jax.ShapeDtypeStruct((b, KV_COLS, s), F32),
        jax.ShapeDtypeStruct((b, s // K_CHUNK, LANES, K_CHUNK), BF16),
        jax.ShapeDtypeStruct((b, s // K_CHUNK, N_KV_HEADS, K_CHUNK, LANES), BF16),
        jax.ShapeDtypeStruct((b, s // K_CHUNK, LANES, K_CHUNK), BF16),
        jax.ShapeDtypeStruct((b, s // K_CHUNK, N_KV_HEADS, K_CHUNK, LANES), BF16),
    )
    k_spec = pl.BlockSpec((1, nck, LANES, K_CHUNK), lambda i, j: (i, j, 0, 0))
    v_spec = pl.BlockSpec((1, nck, N_KV_HEADS, K_CHUNK, LANES), lambda i, j: (i, j, 0, 0, 0))
    out_specs = (
        pl.BlockSpec((1, tm, ATTN_WIDTH), tok),
        pl.BlockSpec((1, tm, GATE_PAD), tok),
        pl.BlockSpec((1, tm, POOL_WIDTH), tok),
        pl.BlockSpec((1, KV_COLS, tm), feat),
        pl.BlockSpec((1, KV_COLS, tm), feat),
        pl.BlockSpec((1, KV_COLS, tm), feat),
        k_spec, v_spec, k_spec, v_spec,
    )
    return pl.pallas_call(
        _proj_kernel,
        grid=(b, s // tm),
        in_specs=[
            pl.BlockSpec((1, tm, D_MODEL), tok),
            pl.BlockSpec((1, D_MODEL), const2),
            pl.BlockSpec((D_MODEL, ROW_COLS), const2),
            pl.BlockSpec((KV_ROWS, D_MODEL), const2),
            pl.BlockSpec((1, LANES), const2),
            pl.BlockSpec((2, HEAD_DIM, 1), lambda i, j: (0, 0, 0)),
        ],
        out_specs=out_specs,
        out_shape=out_shape,
        compiler_params=pltpu.CompilerParams(
            dimension_semantics=("arbitrary", "arbitrary"), vmem_limit_bytes=VMEM_LIMIT),
        name="proj",
    )(x3d, g1, w_row, w_kv, qg, kg)


def _pool_kernel(u_ref, pw_ref, ps_ref, m_ref, *, seg):
    u = u_ref[0]
    n = u.shape[0]

    def shift(v, k):
        rolled = pltpu.roll(v, k, axis=0)
        r = lax.broadcasted_iota(jnp.int32, v.shape, 0) % seg
        return jnp.where(r >= k, rolled, 0.0)

    sums = []
    s = u
    k = 1
    for gi in range(len(POOL_WINDOWS)):
        s = s + shift(s, k)
        k *= 2
        sums.append(s[:, :POOL_GROUP_WIDTH])
        s = s[:, POOL_GROUP_WIDTH:]
    row = lax.broadcasted_iota(jnp.int32, (n, 1), 0) % seg
    for gi, w in enumerate(POOL_WINDOWS):
        sl = slice(gi * POOL_GROUP_WIDTH, (gi + 1) * POOL_GROUP_WIDTH)
        cnt = jnp.minimum(row + 1, w).astype(F32)
        d = sums[gi] / cnt - u[:, sl]
        y = jnp.dot(d.astype(BF16), pw_ref[gi], preferred_element_type=F32)
        m_ref[0, :, sl] = (y * ps_ref[:, sl]).astype(BF16)


def _pool(u3d, pool_w16, pool_scale, seg):
    b, n, _ = u3d.shape
    assert n % seg == 0
    return pl.pallas_call(
        functools.partial(_pool_kernel, seg=seg),
        grid=(b,),
        in_specs=[
            pl.BlockSpec((1, n, POOL_WIDTH), lambda i: (i, 0, 0)),
            pl.BlockSpec((len(POOL_WINDOWS), POOL_GROUP_WIDTH, POOL_GROUP_WIDTH), lambda i: (0, 0, 0)),
            pl.BlockSpec((1, POOL_WIDTH), lambda i: (0, 0)),
        ],
        out_specs=pl.BlockSpec((1, n, POOL_WIDTH), lambda i: (i, 0, 0)),
        out_shape=jax.ShapeDtypeStruct((b, n, POOL_WIDTH), BF16),
        compiler_params=pltpu.CompilerParams(
            dimension_semantics=("arbitrary",), vmem_limit_bytes=VMEM_LIMIT),
        name="pool",
    )(u3d, pool_w16, pool_scale)


def _compress_kernel(pt_ref, src_ref, tail_ref, pos_ref, bd_ref, g_ref, kc_ref, vc_ref, buf, sem,
                     *, n_pages, npp, has_tail, paged):
    b = pl.program_id(0)
    nb = pl.num_programs(0)
    n_slabs = 2 * N_KV_HEADS

    def slab_copy(row, p, c, slot):
        kv, kvh = divmod(c, N_KV_HEADS)
        if paged:
            src = src_ref.at[pt_ref[row, p], pl.ds(c * HEAD_DIM, HEAD_DIM), :]
        else:
            src = src_ref.at[row, pl.ds(c * HEAD_DIM, HEAD_DIM), pl.ds(p * PAGE, PAGE)]
        return pltpu.make_async_copy(src, buf.at[slot, kv, :, kvh * npp + p, :], sem.at[slot])

    def tail_copy(c, slot):
        kv, kvh = divmod(c, N_KV_HEADS)
        return pltpu.make_async_copy(tail_ref.at[0, pl.ds(c * HEAD_DIM, HEAD_DIM), :],
                                     buf.at[slot, kv, :, kvh * npp + n_pages, :], sem.at[slot])

    def row_copies(row, slot, fn):
        def body(p, carry):
            for c in range(n_slabs):
                fn(slab_copy(row, p, c, slot))
            return carry
        lax.fori_loop(0, n_pages, body, 0)

    n_real = n_pages + (1 if has_tail else 0)

    @pl.when(b == 0)
    def _():
        if npp > n_real:
            for kvh in range(N_KV_HEADS):
                buf[:, :, :, kvh * npp + n_real:(kvh + 1) * npp, :] = jnp.zeros(
                    (2, 2, HEAD_DIM, npp - n_real, LANES), F32)
        row_copies(0, 0, lambda cp: cp.start())

    slot = b % 2

    if has_tail:
        for c in range(n_slabs):
            tail_copy(c, slot).start()
        for c in range(n_slabs):
            tail_copy(c, slot).wait()
    row_copies(b, slot, lambda cp: cp.wait())

    rows = N_KV_HEADS * npp
    d_per_it = 8
    n_it = HEAD_DIM // d_per_it
    pages_per_it = n_pages // n_it
    nxt = jnp.minimum(b + 1, nb - 1)

    def body(it, acc):
        for pp in range(pages_per_it):
            for c in range(n_slabs):
                slab_copy(nxt, it * pages_per_it + pp, c, 1 - slot).start()
        for dd in range(d_per_it):
            d = it * d_per_it + dd
            pos = pos_ref[pl.ds(d, 1), :]
            x = jnp.concatenate([buf[slot, 0, d] + pos, buf[slot, 1, d] + pos], axis=1).astype(BF16)
            acc = acc + jnp.dot(x, bd_ref[d], preferred_element_type=F32)
        return acc

    acc = lax.fori_loop(0, n_it, body, jnp.zeros((rows, 2 * LANES), F32))
    kc_ref[0] = _half_group_norm(acc[:, :LANES], g_ref[...])
    vc_ref[0] = acc[:, LANES:]

    @pl.when(b == nb - 1)
    def _():
        row_copies(nxt, 1 - slot, lambda cp: cp.wait())


def _compress(page_table, src, tail, pos_t, bd, g2, *, npp, has_tail, paged):
    b, n_pages = page_table.shape
    kern = functools.partial(_compress_kernel, n_pages=n_pages, npp=npp, has_tail=has_tail, paged=paged)
    rows = N_KV_HEADS * npp
    grid_spec = pltpu.PrefetchScalarGridSpec(
        num_scalar_prefetch=1,
        grid=(b,),
        in_specs=[
            pl.BlockSpec(memory_space=pl.ANY),
            pl.BlockSpec((1, KV_COLS, LANES), lambda i, pt: (i, 0, 0)),
            pl.BlockSpec((HEAD_DIM, LANES), lambda i, pt: (0, 0)),
            pl.BlockSpec((HEAD_DIM, 2 * LANES, 2 * LANES), lambda i, pt: (0, 0, 0)),
            pl.BlockSpec((1, LANES), lambda i, pt: (0, 0)),
        ],
        out_specs=(pl.BlockSpec((1, rows, LANES), lambda i, pt: (i, 0, 0)),
                   pl.BlockSpec((1, rows, LANES), lambda i, pt: (i, 0, 0))),
        scratch_shapes=[
            pltpu.VMEM((2, 2, HEAD_DIM, rows, LANES), F32),
            pltpu.SemaphoreType.DMA((2,)),
        ],
    )
    return pl.pallas_call(
        kern,
        grid_spec=grid_spec,
        out_shape=(jax.ShapeDtypeStruct((b, rows, LANES), F32), jax.ShapeDtypeStruct((b, rows, LANES), F32)),
        compiler_params=pltpu.CompilerParams(
            dimension_semantics=("arbitrary",), vmem_limit_bytes=VMEM_LIMIT),
        name="compress",
    )(page_table, src, tail, pos_t, bd, g2)


def _to_half(tile, src_half, dst_half):
    lane = lax.broadcasted_iota(jnp.int32, tile.shape, 1)
    src = tile if src_half == dst_half else pltpu.roll(tile, HEAD_DIM, axis=1)
    keep = (lane < HEAD_DIM) if dst_half == 0 else (lane >= HEAD_DIM)
    return jnp.where(keep, src, 0.0)


def _pair_tile(o_even, o_odd, half):
    lane = lax.broadcasted_iota(jnp.int32, o_even.shape, 1)
    if half == 0:
        return jnp.where(lane < HEAD_DIM, o_even, pltpu.roll(o_odd, HEAD_DIM, axis=1))
    return jnp.where(lane < HEAD_DIM, pltpu.roll(o_even, HEAD_DIM, axis=1), o_odd)


def _gate_tile(gates, pair, c, shape):
    lane = lax.broadcasted_iota(jnp.int32, shape, 1)
    he, ho = 2 * pair, 2 * pair + 1
    return jnp.where(lane < HEAD_DIM, gates[:, 3 * he + c:3 * he + c + 1], gates[:, 3 * ho + c:3 * ho + c + 1])


def _block_of_col(col, npp):
    return 2 * (col % npp) + col // npp


def _cmp_operand(x, npp):
    return jnp.concatenate([_to_half(x, 0, 0), _to_half(x, 1, 0)], axis=0).astype(BF16)


def _rank_select(score_t, blk_t, cols):
    rank = jnp.zeros(score_t.shape, jnp.int32)
    for r, n in cols:
        row = score_t[r:r + 1, :]
        ahead = (row > score_t) | ((row == score_t) & (blk_t > n))
        rank = rank + ahead.astype(jnp.int32)
    return ((rank < TOP_K_BLOCKS) & (score_t > -0.5)).astype(F32)


POS_HI, POS_LO = HEAD_DIM, HEAD_DIM + 1
MASK_BIG = 1e30


def _attn_prompt_kernel(q_ref, gates_ref, kc_ref, vc_ref, ks_ref, vsx_ref, kw_ref, vwx_ref, exp_ref, gp_ref, a_ref,
                        zbuf, m_sc, acc_sc, *, n_pages, npp):
    i = pl.program_id(1)
    q0 = i * Q_TILE
    rows = Q_PER_KV * Q_TILE
    ncols = 2 * npp
    qpos = q0 + lax.broadcasted_iota(jnp.int32, (Q_TILE, 1), 0)
    blk = _block_of_col(lax.broadcasted_iota(jnp.int32, (Q_TILE, ncols), 1), npp)
    real_cols = [(half * n_pages + p, 2 * p + half) for half in range(2) for p in range(n_pages)]

    g = gates_ref[0]
    g_hi = g.astype(BF16)
    g_lo = (g - g_hi.astype(F32)).astype(BF16)
    gexp = (jnp.dot(g_hi, gp_ref[...], preferred_element_type=F32)
            + jnp.dot(g_lo, gp_ref[...], preferred_element_type=F32))

    r_key = lax.broadcasted_iota(jnp.int32, (Q_TILE, K_CHUNK), 1)
    r_qry = lax.broadcasted_iota(jnp.int32, (Q_TILE, K_CHUNK), 0)
    keep = {"causal": r_key <= r_qry, "lower": r_key > r_qry}

    all_rows = N_HEADS * Q_TILE

    def branch(lhs, k_ref, vx_ref, slc, chunks, c_lo, c_hi):
        m_sc[...] = jnp.full((all_rows, LANES), NEG, F32)

        def scores(c, kind):
            rr = lax.broadcasted_iota(jnp.int32, (HEAD_DIM, K_CHUNK), 0)
            tt = lax.broadcasted_iota(jnp.int32, (HEAD_DIM, K_CHUNK), 1).astype(F32)
            hi = ((c - i) * K_CHUNK).astype(F32)
            pos = jnp.where(rr == 0, hi, jnp.where(rr == 1, tt, 0.0)).astype(BF16)
            for kvh in range(N_KV_HEADS):
                parts = [k_ref[0, c, kvh * HEAD_DIM:(kvh + 1) * HEAD_DIM, :], pos]
                if slc:
                    parts.append(exp_ref[c])
                kx = jnp.concatenate(parts, axis=0)
                for gq in range(Q_PER_KV):
                    sl = slice((kvh * Q_PER_KV + gq) * Q_TILE, (kvh * Q_PER_KV + gq + 1) * Q_TILE)
                    zg = jnp.dot(lhs[sl], kx, preferred_element_type=F32)
                    if kind is not None:
                        zg = jnp.where(keep[kind], zg, NEG)
                    zbuf[c, sl, :] = zg
                    m_sc[sl] = jnp.maximum(m_sc[sl], jnp.maximum(zg[:, :LANES], zg[:, LANES:]))

        def in_pairs(lo, hi, one):
            n = hi - lo

            def body(k, carry):
                one(lo + 2 * k)
                one(lo + 2 * k + 1)
                return carry
            lax.fori_loop(0, n // 2, body, 0)
            pl.when(n % 2 == 1)(lambda: one(hi - 1))

        for group in chunks:
            if not isinstance(group[1], list):
                in_pairs(group[0], group[1], lambda c: scores(c, None))
            else:
                cond, items = group[0], group[1]

                def run(items=items):
                    for c, kind in items:
                        scores(c, kind)
                run() if cond is None else pl.when(cond)(run)

        m = jnp.broadcast_to(jnp.max(m_sc[...], axis=-1, keepdims=True), (all_rows, LANES))
        acc_sc[...] = jnp.zeros((all_rows, LANES), F32)

        def weigh(c):
            for h in range(N_HEADS):
                sl = slice(h * Q_TILE, (h + 1) * Q_TILE)
                z = zbuf[c, sl, :]
                e = jnp.concatenate([jnp.exp(z[:, :LANES] - m[sl]), jnp.exp(z[:, LANES:] - m[sl])], axis=1)
                acc_sc[sl] += jnp.dot(e.astype(BF16), vx_ref[0, c, h // Q_PER_KV], preferred_element_type=F32)

        in_pairs(c_lo, c_hi, weigh)
        return acc_sc[...]

    def pair_norm(acc_e, acc_o):
        lane = lax.broadcasted_iota(jnp.int32, acc_e.shape, 1)
        safe = lambda d: jnp.where(d > 0.0, d, 1.0)
        sw_e = pltpu.roll(acc_e, HEAD_DIM, axis=1)
        sw_o = pltpu.roll(acc_o, HEAD_DIM, axis=1)
        return jnp.where(lane < HEAD_DIM, acc_e / safe(sw_e), sw_o / safe(acc_o))

    o_cmp_all, lhs_slc_all, lhs_win_all = [], [], []
    for kvh in range(N_KV_HEADS):
        slopes = [_slope(kvh * Q_PER_KV + g) for g in range(Q_PER_KV)]
        lane = lax.broadcasted_iota(jnp.int32, (Q_TILE, LANES), 1)
        pieces, pieces_x = [], []
        for gq in range(Q_PER_KV):
            h = kvh * Q_PER_KV + gq
            tile = q_ref[0, :, (h // 2) * LANES:(h // 2 + 1) * LANES].astype(F32) * SCALE
            low = _to_half(tile, h % 2, 0)
            pieces.append(low)
            pieces_x.append(jnp.where((lane == POS_HI) | (lane == POS_LO), slopes[gq], low))
        qk = jnp.concatenate(pieces, axis=0).astype(BF16)
        qk_x = jnp.concatenate(pieces_x, axis=0).astype(BF16)

        kc = _cmp_operand(kc_ref[0, kvh], npp)
        vc = _cmp_operand(vc_ref[0, kvh], npp)
        s = _dot_t(qk, kc)
        dist_c = qpos - (blk * CMP_BLOCK + (CMP_BLOCK - 1))
        mask_c = dist_c >= 0
        dist_cf = dist_c.astype(F32)
        p_list = []
        p_kv = jnp.zeros((Q_TILE, ncols), F32)
        for g in range(Q_PER_KV):
            z = jnp.where(mask_c, s[g * Q_TILE:(g + 1) * Q_TILE] - slopes[g] * dist_cf, NEG)
            mx = jnp.max(z, axis=-1, keepdims=True)
            e = jnp.where(mask_c, jnp.exp(z - mx), 0.0)
            den = jnp.sum(e, axis=-1, keepdims=True)
            p = e / jnp.where(den > 0.0, den, 1.0)
            p_kv = p_kv + p
            p_list.append(p.astype(BF16))
        o_cmp = jnp.dot(jnp.concatenate(p_list, axis=0), vc, preferred_element_type=F32)

        cand = blk * CMP_BLOCK <= qpos
        forced = (blk == qpos // CMP_BLOCK) | (blk == 0)
        score = jnp.where(forced, FORCE_SCORE, jnp.where(cand, p_kv, -1.0))
        score_t = score.T
        st = jnp.concatenate([score_t[:n_pages], score_t[npp:npp + n_pages]], axis=0)
        rr = lax.broadcasted_iota(jnp.int32, (2 * n_pages, Q_TILE), 0)
        bt = jnp.where(rr < n_pages, 2 * rr, 2 * (rr - n_pages) + 1)
        sel_s = _rank_select(st, bt, [(r, n) for r, (_, n) in enumerate(real_cols)])
        pad = jnp.zeros((npp - n_pages, Q_TILE), F32)
        sel_t = jnp.concatenate([sel_s[:n_pages], pad, sel_s[n_pages:], pad], axis=0)
        sel = sel_t.T
        sel_bias = jnp.where(sel > 0.5, 0.0, -MASK_BIG).astype(BF16)
        lhs_slc_all.append(jnp.concatenate([qk_x, jnp.concatenate([sel_bias] * Q_PER_KV, axis=0)], axis=1))
        lhs_win_all.append(qk_x)
        o_cmp_all.append(o_cmp)

    o_slc = branch(jnp.concatenate(lhs_slc_all, axis=0), ks_ref, vsx_ref, True,
                   [(0, i), (None, [(i, "causal")])], 0, i + 1)
    o_win = branch(jnp.concatenate(lhs_win_all, axis=0), kw_ref, vwx_ref, False,
                   [(i >= 2, [(i - 2, "lower"), (i - 1, None), (i, "causal")]),
                    (i == 1, [(0, None), (1, "causal")]),
                    (i == 0, [(0, "causal")])],
                   jnp.maximum(i - 2, 0), i + 1)
    o_cmp = jnp.concatenate(o_cmp_all, axis=0)

    for pair in range(N_HEADS // 2):
        e_sl = slice(2 * pair * Q_TILE, (2 * pair + 1) * Q_TILE)
        o_sl = slice((2 * pair + 1) * Q_TILE, (2 * pair + 2) * Q_TILE)
        col = pair * LANES
        tile = (gexp[:, col:col + LANES] * _pair_tile(o_cmp[e_sl], o_cmp[o_sl], 0)
                + gexp[:, ATTN_WIDTH + col:ATTN_WIDTH + col + LANES] * pair_norm(o_slc[e_sl], o_slc[o_sl])
                + gexp[:, 2 * ATTN_WIDTH + col:2 * ATTN_WIDTH + col + LANES] * pair_norm(o_win[e_sl], o_win[o_sl]))
        a_ref[0, :, pair * LANES:(pair + 1) * LANES] = tile.astype(BF16)


def _attn_prompt(q, gates, kc, vc, ks16, vsx, kw16, vwx, expand, gate_place, n_pages):
    b, s, _ = q.shape
    npp = kc.shape[2]
    n_chunks = s // K_CHUNK
    rows = Q_PER_KV * Q_TILE
    assert WINDOW == 2 * K_CHUNK and Q_TILE == K_CHUNK
    kern = functools.partial(_attn_prompt_kernel, n_pages=n_pages, npp=npp)
    per_b4 = lambda bi, i: (bi, 0, 0, 0)
    per_b5 = lambda bi, i: (bi, 0, 0, 0, 0)
    return pl.pallas_call(
        kern,
        grid=(b, s // Q_TILE),
        in_specs=[
            pl.BlockSpec((1, Q_TILE, ATTN_WIDTH), lambda bi, i: (bi, i, 0)),
            pl.BlockSpec((1, Q_TILE, GATE_PAD), lambda bi, i: (bi, i, 0)),
            pl.BlockSpec((1, N_KV_HEADS, npp, LANES), per_b4),
            pl.BlockSpec((1, N_KV_HEADS, npp, LANES), per_b4),
            pl.BlockSpec((1, n_chunks, LANES, K_CHUNK), per_b4),
            pl.BlockSpec((1, n_chunks, N_KV_HEADS, K_CHUNK, LANES), per_b5),
            pl.BlockSpec((1, n_chunks, LANES, K_CHUNK), per_b4),
            pl.BlockSpec((1, n_chunks, N_KV_HEADS, K_CHUNK, LANES), per_b5),
            pl.BlockSpec((n_chunks, 2 * npp, K_CHUNK), lambda bi, i: (0, 0, 0)),
            pl.BlockSpec((GATE_PAD, 3 * ATTN_WIDTH), lambda bi, i: (0, 0)),
        ],
        out_specs=pl.BlockSpec((1, Q_TILE, ATTN_WIDTH), lambda bi, i: (bi, i, 0)),
        out_shape=jax.ShapeDtypeStruct((b, s, ATTN_WIDTH), BF16),
        scratch_shapes=[
            pltpu.VMEM((n_chunks, N_KV_HEADS * rows, K_CHUNK), F32),
            pltpu.VMEM((N_KV_HEADS * rows, LANES), F32),
            pltpu.VMEM((N_KV_HEADS * rows, LANES), F32),
        ],
        compiler_params=pltpu.CompilerParams(
            dimension_semantics=("arbitrary", "arbitrary"), vmem_limit_bytes=VMEM_LIMIT),
        name="attn_prompt",
    )(q, gates, kc, vc, ks16, vsx, kw16, vwx, expand, gate_place)


S_ROWS = LANES
S_CHUNK = 1024


def _sample_rows(q_ref, ds, past):
    n_real = N_HEADS * ds
    by_kvh, low = [], []
    for h in range(N_HEADS):
        tile = q_ref[0, :, (h // 2) * LANES:(h // 2 + 1) * LANES].astype(F32) * SCALE
        by_kvh.append(_to_half(tile, h % 2, h // Q_PER_KV))
        low.append(_to_half(tile, h % 2, 0))
    zpad = jnp.zeros((S_ROWS - n_real, LANES), F32)
    qrows = jnp.concatenate(by_kvh + [zpad], axis=0).astype(BF16)
    qlow = jnp.concatenate(low + [zpad], axis=0).astype(BF16)
    r_col = lax.broadcasted_iota(jnp.int32, (S_ROWS, 1), 0)
    q_of_r = r_col % ds
    h_of_r = r_col // ds
    sl_r = jnp.zeros((S_ROWS, 1), F32)
    for h in range(N_HEADS):
        sl_r = jnp.where(h_of_r == h, _slope(h), sl_r)
    return qrows, qlow, q_of_r, past + q_of_r, sl_r, h_of_r // Q_PER_KV


def _select_sample_kernel(q_ref, kc_ref, vc_ref, ocmp_ref, sel_ref, need_ref, *, past, ds, npp):
    n_real = N_HEADS * ds
    ncols = 2 * npp
    n_blocks = (past + ds + CMP_BLOCK - 1) // CMP_BLOCK
    _, qlow, _, qp_r, sl_r, kvh_r = _sample_rows(q_ref, ds, past)

    blk = _block_of_col(lax.broadcasted_iota(jnp.int32, (S_ROWS, ncols), 1), npp)
    s = jnp.where(kvh_r == 0, _dot_t(qlow, _cmp_operand(kc_ref[0, 0], npp)),
                  _dot_t(qlow, _cmp_operand(kc_ref[0, 1], npp)))
    dist_c = qp_r - (blk * CMP_BLOCK + (CMP_BLOCK - 1))
    mask_c = dist_c >= 0
    z = jnp.where(mask_c, s - sl_r * dist_c.astype(F32), NEG)
    mx = jnp.max(z, axis=-1, keepdims=True)
    e = jnp.where(mask_c, jnp.exp(z - mx), 0.0)
    den = jnp.sum(e, axis=-1, keepdims=True)
    p = e / jnp.where(den > 0.0, den, 1.0)
    pb = p.astype(BF16)
    o_cmp = jnp.where(kvh_r == 0, jnp.dot(pb, _cmp_operand(vc_ref[0, 0], npp), preferred_element_type=F32),
                      jnp.dot(pb, _cmp_operand(vc_ref[0, 1], npp), preferred_element_type=F32))

    kq = N_KV_HEADS * ds
    p_kv = []
    for kvh in range(N_KV_HEADS):
        acc = jnp.zeros((ds, ncols), F32)
        for g in range(Q_PER_KV):
            r0 = (kvh * Q_PER_KV + g) * ds
            acc = acc + p[r0:r0 + ds]
        p_kv.append(acc)
    p_kv = jnp.concatenate(p_kv, axis=0)
    blk2 = _block_of_col(lax.broadcasted_iota(jnp.int32, (kq, ncols), 1), npp)
    qp2 = past + lax.broadcasted_iota(jnp.int32, (kq, 1), 0) % ds
    cand = blk2 * CMP_BLOCK <= qp2
    forced = (blk2 == qp2 // CMP_BLOCK) | (blk2 == 0)
    score = jnp.where(forced, FORCE_SCORE, jnp.where(cand, p_kv, -1.0))
    rank = jnp.zeros((kq, ncols), jnp.int32)
    for n in range(n_blocks):
        c = (n % 2) * npp + n // 2
        col = score[:, c:c + 1]
        ahead = (col > score) | ((col == score) & (blk2 > n))
        rank = rank + ahead.astype(jnp.int32)
    sel2 = ((rank < TOP_K_BLOCKS) & (score > -0.5)).astype(F32)
    sel_rows = []
    for h in range(N_HEADS):
        kvh = h // Q_PER_KV
        sel_rows.append(sel2[kvh * ds:(kvh + 1) * ds])
    sel_rows.append(jnp.zeros((S_ROWS - n_real, ncols), F32))
    sel_rows = jnp.concatenate(sel_rows, axis=0)
    ocmp_ref[0] = o_cmp
    sel_ref[0] = sel_rows
    any_row = jnp.max(sel_rows, axis=0, keepdims=True)
    page_need = jnp.maximum(any_row[:, :npp], any_row[:, npp:])
    need_ref[0] = jnp.broadcast_to(page_need, (8, npp))


def _select_sample(q, kc, vc, *, past):
    b, ds, _ = q.shape
    npp = kc.shape[2]
    kern = functools.partial(_select_sample_kernel, past=past, ds=ds, npp=npp)
    per_b3 = lambda i: (i, 0, 0)
    per_b4 = lambda i: (i, 0, 0, 0)
    return pl.pallas_call(
        kern,
        grid=(b,),
        in_specs=[pl.BlockSpec((1, ds, ATTN_WIDTH), per_b3),
                  pl.BlockSpec((1, N_KV_HEADS, npp, LANES), per_b4),
                  pl.BlockSpec((1, N_KV_HEADS, npp, LANES), per_b4)],
        out_specs=(pl.BlockSpec((1, S_ROWS, LANES), per_b3), pl.BlockSpec((1, S_ROWS, 2 * npp), per_b3),
                   pl.BlockSpec((1, 8, npp), per_b3)),
        out_shape=(jax.ShapeDtypeStruct((b, S_ROWS, LANES), F32), jax.ShapeDtypeStruct((b, S_ROWS, 2 * npp), F32),
                   jax.ShapeDtypeStruct((b, 8, npp), F32)),
        compiler_params=pltpu.CompilerParams(dimension_semantics=("arbitrary",), vmem_limit_bytes=VMEM_LIMIT),
        name="select_sample",
    )(q, kc, vc)


def _attn_sample_kernel(pt_ref, order_ref, nn_ref, q_ref, gates_ref, ocmp_ref, sel_ref, kvs_new_ref, kvw_new_ref,
                        win_ref, cache_ref, a_ref, kbuf, zbuf, sem, *, n_pages, past, ds, npp):
    b = pl.program_id(0)
    nb = pl.num_programs(0)
    ncols = 2 * npp
    half_cols = KV_COLS // 2
    pages_per_chunk = S_CHUNK // PAGE

    def page_copy(row, j, slot):
        return pltpu.make_async_copy(
            cache_ref.at[pt_ref[row, order_ref[row, j]]], kbuf.at[slot, :, pl.ds(j * PAGE, PAGE)], sem.at[slot])

    def row_copies(row, slot, fn):
        def body(j, carry):
            fn(page_copy(row, j, slot))
            return carry
        lax.fori_loop(0, nn_ref[row], body, 0)

    @pl.when(b == 0)
    def _():
        kbuf[...] = jnp.zeros(kbuf.shape, F32)
        row_copies(0, 0, lambda cp: cp.start())

    slot = b % 2

    @pl.when(b + 1 < nb)
    def _():
        row_copies(b + 1, 1 - slot, lambda cp: cp.start())

    qrows, _, q_of_r, qp_r, sl_r, _ = _sample_rows(q_ref, ds, past)
    o_cmp = ocmp_ref[0]
    sel_rows = sel_ref[0]
    sel16 = sel_rows.astype(BF16)

    new_lane = lax.broadcasted_iota(jnp.int32, (1, LANES), 1)

    def new_scores(ref, extra_mask):
        k_new = ref[0, :half_cols, :].astype(BF16)
        s_new = jnp.dot(qrows, k_new, preferred_element_type=F32)
        dist = q_of_r - new_lane
        mask = (new_lane < ds) & (dist >= 0) & extra_mask
        return jnp.where(mask, s_new - sl_r * dist.astype(F32), NEG)

    def weighted_new(ref, e_new):
        return _dot_t(e_new.astype(BF16), ref[0, half_cols:, :].astype(BF16))

    row_copies(b, slot, lambda cp: cp.wait())
    n_need = nn_ref[b]
    n_chunks = (n_need + pages_per_chunk - 1) // pages_per_chunk

    def pass1(c, mrow):
        k0 = pl.multiple_of(c * S_CHUNK, S_CHUNK)
        kt = kbuf[slot, :half_cols, pl.ds(k0, S_CHUNK)].astype(BF16)
        st = jnp.dot(qrows, kt, preferred_element_type=F32)
        lane = lax.broadcasted_iota(jnp.int32, (1, S_CHUNK), 1)
        tpos = jnp.full((1, S_CHUNK), -PAGE * n_pages, jnp.int32)
        for jj in range(pages_per_chunk):
            j = c * pages_per_chunk + jj
            page = order_ref[b, jnp.minimum(j, n_pages - 1)]
            base = jnp.where(j < n_need, page * PAGE, -PAGE * n_pages) - jj * PAGE
            tpos = jnp.where(lane // PAGE == jj, base + lane, tpos)
        blk_c = _block_of_col(lax.broadcasted_iota(jnp.int32, (ncols, S_CHUNK), 0), npp)
        expand = (blk_c == jnp.broadcast_to(tpos, (ncols, S_CHUNK)) // CMP_BLOCK).astype(BF16)
        chosen = jnp.dot(sel16, expand, preferred_element_type=F32) > 0.5
        dist = qp_r - tpos
        zc = jnp.where(chosen & (dist >= 0), st - sl_r * dist.astype(F32), NEG)
        zbuf[c] = zc
        return jnp.maximum(mrow, jnp.max(zc, axis=-1, keepdims=True))

    mrow = lax.fori_loop(0, n_chunks, pass1, jnp.full((S_ROWS, 1), NEG, F32))
    c_last = ((past // CMP_BLOCK) % 2) * npp + (past // CMP_BLOCK) // 2
    z_new = new_scores(kvs_new_ref, sel_rows[:, c_last:c_last + 1] > 0.5)
    m_r = jnp.maximum(mrow, jnp.max(z_new, axis=-1, keepdims=True))

    def pass2(c, carry):
        acc, den_r = carry
        k0 = pl.multiple_of(c * S_CHUNK, S_CHUNK)
        zc = zbuf[c]
        ec = jnp.where(zc > 0.5 * NEG, jnp.exp(zc - m_r), 0.0)
        vt = kbuf[slot, half_cols:, pl.ds(k0, S_CHUNK)].astype(BF16)
        return acc + _dot_t(ec.astype(BF16), vt), den_r + jnp.sum(ec, axis=-1, keepdims=True)

    e_new = jnp.where(z_new > 0.5 * NEG, jnp.exp(z_new - m_r), 0.0)
    acc, den_r = lax.fori_loop(0, n_chunks, pass2,
                               (weighted_new(kvs_new_ref, e_new), jnp.sum(e_new, axis=-1, keepdims=True)))
    o_slc = acc / jnp.where(den_r > 0.0, den_r, 1.0)

    win_buf = win_ref.shape[2]
    kw = win_ref[0, :half_cols, :].astype(BF16)
    st = jnp.dot(qrows, kw, preferred_element_type=F32)
    kpos = past - win_buf + lax.broadcasted_iota(jnp.int32, (1, win_buf), 1)
    dist = qp_r - kpos
    mask = (dist >= 0) & (dist < WINDOW) & (kpos >= 0)
    z_w = jnp.where(mask, st - sl_r * dist.astype(F32), NEG)
    zw_new = new_scores(kvw_new_ref, True)
    m_w = jnp.maximum(jnp.max(z_w, axis=-1, keepdims=True), jnp.max(zw_new, axis=-1, keepdims=True))
    e_w = jnp.where(z_w > 0.5 * NEG, jnp.exp(z_w - m_w), 0.0)
    ew_new = jnp.where(zw_new > 0.5 * NEG, jnp.exp(zw_new - m_w), 0.0)
    acc_w = _dot_t(e_w.astype(BF16), win_ref[0, half_cols:, :].astype(BF16)) + weighted_new(kvw_new_ref, ew_new)
    den_w = jnp.sum(e_w, axis=-1, keepdims=True) + jnp.sum(ew_new, axis=-1, keepdims=True)
    o_win = acc_w / jnp.where(den_w > 0.0, den_w, 1.0)

    gates = gates_ref[0]
    for pair in range(N_HEADS // 2):
        kvh = (2 * pair) // Q_PER_KV
        e_sl = slice(2 * pair * ds, (2 * pair + 1) * ds)
        o_sl = slice((2 * pair + 1) * ds, (2 * pair + 2) * ds)
        shape = (ds, LANES)
        tile = (_gate_tile(gates, pair, 0, shape) * _pair_tile(o_cmp[e_sl], o_cmp[o_sl], 0)
                + _gate_tile(gates, pair, 1, shape) * _pair_tile(o_slc[e_sl], o_slc[o_sl], kvh)
                + _gate_tile(gates, pair, 2, shape) * _pair_tile(o_win[e_sl], o_win[o_sl], kvh))
        a_ref[0, :, pair * LANES:(pair + 1) * LANES] = tile.astype(BF16)


def _attn_sample(page_table, order, n_need, q, gates, o_cmp, sel_rows, kvs_new, kvw_new, state_win_t, cache_pages,
                 *, past):
    b, n_pages = page_table.shape
    ds = q.shape[1]
    npp = sel_rows.shape[2] // 2
    win_buf = state_win_t.shape[2]
    n_keys = n_pages * PAGE
    kern = functools.partial(_attn_sample_kernel, n_pages=n_pages, past=past, ds=ds, npp=npp)
    per_b = lambda i, pt, od, nn: (i, 0, 0)
    grid_spec = pltpu.PrefetchScalarGridSpec(
        num_scalar_prefetch=3,
        grid=(b,),
        in_specs=[
            pl.BlockSpec((1, ds, ATTN_WIDTH), per_b),
            pl.BlockSpec((1, ds, GATE_PAD), per_b),
            pl.BlockSpec((1, S_ROWS, LANES), per_b),
            pl.BlockSpec((1, S_ROWS, 2 * npp), per_b),
            pl.BlockSpec((1, KV_COLS, LANES), per_b),
            pl.BlockSpec((1, KV_COLS, LANES), per_b),
            pl.BlockSpec((1, KV_COLS, win_buf), per_b),
            pl.BlockSpec(memory_space=pl.ANY),
        ],
        out_specs=pl.BlockSpec((1, ds, ATTN_WIDTH), per_b),
        scratch_shapes=[
            pltpu.VMEM((2, KV_COLS, n_keys), F32),
            pltpu.VMEM((n_keys // S_CHUNK, S_ROWS, S_CHUNK), F32),
            pltpu.SemaphoreType.DMA((2,)),
        ],
    )
    return pl.pallas_call(
        kern,
        grid_spec=grid_spec,
        out_shape=jax.ShapeDtypeStruct((b, ds, ATTN_WIDTH), BF16),
        compiler_params=pltpu.CompilerParams(
            dimension_semantics=("arbitrary",), vmem_limit_bytes=VMEM_LIMIT),
        name="attn_sample",
    )(page_table, order, n_need, q, gates, o_cmp, sel_rows, kvs_new, kvw_new, state_win_t, cache_pages)


X_TILES = D_MODEL // LANES
PAY_SUB = X_TILES + 1
GROUP_LANE = 0


def _rows_to_tiles(ref, lo, val):
    for k in range(val.shape[1] // LANES):
        ref[:, lo + k, :] = val[:, k * LANES:(k + 1) * LANES]


def _tiles_to_rows(ref, lo, n):
    return jnp.concatenate([ref[:, lo + k, :] for k in range(n)], axis=1)


def _ffn_kernel(x_ref, a_ref, m_ref, wo_ref, g2_ref, wr_ref, *out_refs, packed):
    h = (x_ref[...]
         + jnp.dot(a_ref[...], wo_ref[:ATTN_WIDTH, :], preferred_element_type=F32)
         + jnp.dot(m_ref[...], wo_ref[ATTN_WIDTH:, :], preferred_element_type=F32))
    ms = jnp.mean(h * h, axis=-1, keepdims=True)
    n2f = (h * lax.rsqrt(ms + EPS)) * g2_ref[...]
    n2 = n2f.astype(BF16)
    logits = jnp.dot(n2, wr_ref[...], preferred_element_type=F32)
    lane = lax.broadcasted_iota(jnp.int32, logits.shape, 1)
    big = jnp.int32(LANES)

    def masked_softmax(mask):
        zz = jnp.where(mask, logits, NEG)
        mx = jnp.max(zz, axis=-1, keepdims=True)
        ee = jnp.where(mask, jnp.exp(zz - mx), 0.0)
        return ee / jnp.sum(ee, axis=-1, keepdims=True)

    def first_argmax(vals, mask):
        v = jnp.max(jnp.where(mask, vals, -1.0), axis=-1, keepdims=True)
        idx = jnp.min(jnp.where(mask & (vals == v), lane, big), axis=-1, keepdims=True)
        return v, idx

    is_g = lane < N_EXPERT_GROUPS
    pg = masked_softmax(is_g)
    g_val, g_idx = first_argmax(pg, is_g)
    e_lane = lane - ROUTER_LANE0
    in_grp = (e_lane >= 0) & (e_lane < N_EXPERTS) & (e_lane // EXPERTS_PER_GROUP == g_idx)
    pe = masked_softmax(in_grp)
    v1, i1 = first_argmax(pe, in_grp)
    rest = in_grp & (lane != i1)
    v2, i2 = first_argmax(pe, rest)
    scale = g_val / (v1 + v2)
    gate = jnp.where(lane == i1, v1 * scale, jnp.where(lane == i2, v2 * scale, 0.0))
    gate = jnp.where(lane == GROUP_LANE, g_idx.astype(F32), gate)
    if packed:
        h_ref, pay_ref, gate_ref = out_refs
        h_ref[...] = h
        gate_ref[...] = gate
        _rows_to_tiles(pay_ref, 0, n2f)
        pay_ref[:, X_TILES, :] = gate
    else:
        h_ref, n2_ref, gate_ref = out_refs
        h_ref[...] = h
        n2_ref[...] = n2
        gate_ref[...] = gate


def _ffn(x2d, a2d, m2d, w_out16, g2, w_router16, tm, packed):
    t = x2d.shape[0]
    row = lambda i: (i, 0)
    const = lambda i: (0, 0)
    if packed:
        out_specs = (pl.BlockSpec((tm, D_MODEL), row), pl.BlockSpec((tm, PAY_SUB, LANES), lambda i: (i, 0, 0)),
                     pl.BlockSpec((tm, LANES), row))
        out_shape = (jax.ShapeDtypeStruct((t, D_MODEL), F32), jax.ShapeDtypeStruct((t, PAY_SUB, LANES), F32),
                     jax.ShapeDtypeStruct((t, LANES), F32))
    else:
        out_specs = (pl.BlockSpec((tm, D_MODEL), row), pl.BlockSpec((tm, D_MODEL), row),
                     pl.BlockSpec((tm, LANES), row))
        out_shape = (jax.ShapeDtypeStruct((t, D_MODEL), F32), jax.ShapeDtypeStruct((t, D_MODEL), BF16),
                     jax.ShapeDtypeStruct((t, LANES), F32))
    return pl.pallas_call(
        functools.partial(_ffn_kernel, packed=packed),
        grid=(t // tm,),
        in_specs=[
            pl.BlockSpec((tm, D_MODEL), row),
            pl.BlockSpec((tm, ATTN_WIDTH), row),
            pl.BlockSpec((tm, POOL_WIDTH), row),
            pl.BlockSpec((D_MODEL, D_MODEL), const),
            pl.BlockSpec((1, D_MODEL), const),
            pl.BlockSpec((D_MODEL, LANES), const),
        ],
        out_specs=out_specs,
        out_shape=out_shape,
        compiler_params=pltpu.CompilerParams(
            dimension_semantics=("arbitrary",), vmem_limit_bytes=VMEM_LIMIT),
        name="ffn",
    )(x2d, a2d, m2d, w_out16, g2, w_router16)


MOE_ROWS = 512
ROW_CHUNK = 512
PLAN_TILE = 1024
PLAN_COLS = 8


def _plan_kernel(gate_ref, tri_ref, out_ref, cnt_ref, carry):
    @pl.when(pl.program_id(0) == 0)
    def _():
        carry[...] = jnp.zeros(carry.shape, F32)

    gl = gate_ref[...]
    lane = lax.broadcasted_iota(jnp.int32, gl.shape, 1)
    gid = gl[:, GROUP_LANE:GROUP_LANE + 1]
    onehot = jnp.where((lane < N_EXPERT_GROUPS) & (lane.astype(F32) == gid), 1.0, 0.0)
    before = jnp.dot(tri_ref[...], onehot.astype(BF16), preferred_element_type=F32) + carry[0:1, :]
    rank = jnp.sum(onehot * before, axis=-1, keepdims=True)
    out_ref[...] = jnp.where(lane == 0, rank, jnp.where(lane == 1, gid, 0.0))[:, :PLAN_COLS]
    carry[0:1, :] = carry[0:1, :] + jnp.sum(onehot, axis=0, keepdims=True)
    cnt_ref[...] = carry[...]


def _plan(gate):
    t = gate.shape[0]
    tri = jnp.asarray(np.tril(np.ones((PLAN_TILE, PLAN_TILE), np.float32), -1), dtype=BF16)
    return pl.pallas_call(
        _plan_kernel,
        grid=(t // PLAN_TILE,),
        in_specs=[pl.BlockSpec((PLAN_TILE, LANES), lambda i: (i, 0)),
                  pl.BlockSpec((PLAN_TILE, PLAN_TILE), lambda i: (0, 0))],
        out_specs=(pl.BlockSpec((PLAN_TILE, PLAN_COLS), lambda i: (i, 0)), pl.BlockSpec((8, LANES), lambda i: (0, 0))),
        out_shape=(jax.ShapeDtypeStruct((t, PLAN_COLS), F32), jax.ShapeDtypeStruct((8, LANES), F32)),
        scratch_shapes=[pltpu.VMEM((8, LANES), F32)],
        compiler_params=pltpu.CompilerParams(
            dimension_semantics=("arbitrary",), vmem_limit_bytes=VMEM_LIMIT),
        name="moe_plan",
    )(gate, tri)


def _scatter_kernel(meta_ref, slot_ref, pay_ref, xs_ref, zrow, sem, zsem):
    c = pl.program_id(0)

    @pl.when(c == 0)
    def _():
        zrow[...] = jnp.zeros(zrow.shape, F32)
        for g in range(N_EXPERT_GROUPS + 1):
            lo = meta_ref[g]
            hi = meta_ref[N_EXPERT_GROUPS + 1 + g]

            def zero_start(r, carry):
                pltpu.make_async_copy(zrow.at[pl.ds(0, 1)], xs_ref.at[pl.ds(r, 1)], zsem.at[0]).start()
                return carry

            def zero_wait(r, carry):
                pltpu.make_async_copy(zrow.at[pl.ds(0, 1)], xs_ref.at[pl.ds(0, 1)], zsem.at[0]).wait()
                return carry

            lax.fori_loop(lo, hi, zero_start, 0)
            lax.fori_loop(lo, hi, zero_wait, 0)

    def issue(u, carry):
        pltpu.make_async_copy(pay_ref.at[pl.ds(u, 1)], xs_ref.at[pl.ds(slot_ref[0, 0, u], 1)], sem.at[0]).start()
        return carry

    lax.fori_loop(0, ROW_CHUNK, issue, 0, unroll=8)
    pltpu.make_async_copy(pay_ref, xs_ref.at[pl.ds(0, ROW_CHUNK)], sem.at[0]).wait()


def _scatter(meta, slot3, pay, n_slots):
    t = pay.shape[0]
    return pl.pallas_call(
        _scatter_kernel,
        grid_spec=pltpu.PrefetchScalarGridSpec(
            num_scalar_prefetch=1,
            grid=(t // ROW_CHUNK,),
            in_specs=[pl.BlockSpec((1, 1, ROW_CHUNK), lambda c, meta: (c, 0, 0), memory_space=pltpu.SMEM),
                      pl.BlockSpec((ROW_CHUNK, PAY_SUB, LANES), lambda c, meta: (c, 0, 0))],
            out_specs=pl.BlockSpec(memory_space=pl.ANY),
            scratch_shapes=[pltpu.VMEM((1, PAY_SUB, LANES), F32), pltpu.SemaphoreType.DMA((1,)),
                            pltpu.SemaphoreType.DMA((1,))],
        ),
        out_shape=jax.ShapeDtypeStruct((n_slots, PAY_SUB, LANES), F32),
        compiler_params=pltpu.CompilerParams(dimension_semantics=("arbitrary",), vmem_limit_bytes=VMEM_LIMIT),
        name="moe_scatter",
    )(meta, slot3, pay)


def _gather_kernel(slot_ref, nslot_ref, h_ref, ys_ref, y_ref, buf, sem):
    c = pl.program_id(0)
    nc = pl.num_programs(0)

    def fetch(ref, slot):
        def issue(u, carry):
            pltpu.make_async_copy(ys_ref.at[pl.ds(ref[0, 0, u], 1)], buf.at[slot, pl.ds(u, 1)], sem.at[slot]).start()
            return carry
        lax.fori_loop(0, ROW_CHUNK, issue, 0, unroll=8)

    @pl.when(c == 0)
    def _():
        fetch(slot_ref, 0)

    @pl.when(c + 1 < nc)
    def _():
        fetch(nslot_ref, (c + 1) % 2)

    cur = c % 2
    pltpu.make_async_copy(ys_ref.at[pl.ds(0, ROW_CHUNK)], buf.at[cur], sem.at[cur]).wait()
    y_ref[...] = h_ref[...] + jnp.concatenate([buf[cur, :, k, :] for k in range(X_TILES)], axis=1)


def _gather(slot3, h, ys):
    t = h.shape[0]
    nc = t // ROW_CHUNK
    return pl.pallas_call(
        _gather_kernel,
        grid=(nc,),
        in_specs=[pl.BlockSpec((1, 1, ROW_CHUNK), lambda c: (c, 0, 0), memory_space=pltpu.SMEM),
                  pl.BlockSpec((1, 1, ROW_CHUNK), lambda c: (jnp.minimum(c + 1, nc - 1), 0, 0),
                               memory_space=pltpu.SMEM),
                  pl.BlockSpec((ROW_CHUNK, D_MODEL), lambda c: (c, 0)),
                  pl.BlockSpec(memory_space=pl.ANY)],
        out_specs=pl.BlockSpec((ROW_CHUNK, D_MODEL), lambda c: (c, 0)),
        out_shape=jax.ShapeDtypeStruct((t, D_MODEL), F32),
        scratch_shapes=[pltpu.VMEM((2, ROW_CHUNK, X_TILES, LANES), F32), pltpu.SemaphoreType.DMA((2,))],
        compiler_params=pltpu.CompilerParams(dimension_semantics=("arbitrary",), vmem_limit_bytes=VMEM_LIMIT),
        name="moe_gather",
    )(slot3, slot3, h, ys)


def _group_moe_kernel(tg_ref, nu_ref, xs_ref, wg_ref, wu_ref, wd_ref, ys_ref):
    j = pl.program_id(0)

    @pl.when(j < nu_ref[0])
    def _():
        g = tg_ref[j]
        x = _tiles_to_rows(xs_ref, 0, X_TILES).astype(BF16)
        gate = xs_ref[:, X_TILES, :]
        lane = lax.broadcasted_iota(jnp.int32, gate.shape, 1)
        y = jnp.zeros((MOE_ROWS, D_MODEL), F32)
        for e in range(EXPERTS_PER_GROUP):
            gu = jnp.dot(x, wg_ref[e], preferred_element_type=F32)
            up = jnp.dot(x, wu_ref[e], preferred_element_type=F32)
            he = (gu * (1.0 / (1.0 + jnp.exp(-gu)))) * up
            out = jnp.dot(he.astype(BF16), wd_ref[e], preferred_element_type=F32)
            col = ROUTER_LANE0 + g * EXPERTS_PER_GROUP + e
            y = y + jnp.sum(jnp.where(lane == col, gate, 0.0), axis=-1, keepdims=True) * out
        _rows_to_tiles(ys_ref, 0, y)

    @pl.when(j >= nu_ref[0])
    def _():
        ys_ref[...] = jnp.zeros(ys_ref.shape, F32)


def _group_moe(tile_group, n_used, xs, wg16, wu16, wd16):
    n_slots = xs.shape[0]
    wspec = lambda shape: pl.BlockSpec((EXPERTS_PER_GROUP,) + shape, lambda j, tg, nu: (tg[j], 0, 0))
    return pl.pallas_call(
        _group_moe_kernel,
        grid_spec=pltpu.PrefetchScalarGridSpec(
            num_scalar_prefetch=2,
            grid=(n_slots // MOE_ROWS,),
            in_specs=[pl.BlockSpec((MOE_ROWS, PAY_SUB, LANES), lambda j, tg, nu: (j, 0, 0)),
                      wspec((D_MODEL, D_EXPERT)), wspec((D_MODEL, D_EXPERT)), wspec((D_EXPERT, D_MODEL))],
            out_specs=pl.BlockSpec((MOE_ROWS, X_TILES, LANES), lambda j, tg, nu: (j, 0, 0)),
        ),
        out_shape=jax.ShapeDtypeStruct((n_slots, X_TILES, LANES), F32),
        compiler_params=pltpu.CompilerParams(dimension_semantics=("arbitrary",), vmem_limit_bytes=VMEM_LIMIT),
        name="moe_group",
    )(tile_group, n_used, xs, wg16, wu16, wd16)


def _routed_moe(h, pay, gate, wg16, wu16, wd16):
    t = pay.shape[0]
    n_slots = t + N_EXPERT_GROUPS * MOE_ROWS
    n_tiles = n_slots // MOE_ROWS
    plan, counts = _plan(gate)
    rank = plan[:, 0].astype(jnp.int32)
    gid = plan[:, 1].astype(jnp.int32)
    cnt = counts[0, :N_EXPERT_GROUPS].astype(jnp.int32)
    padded = -(-cnt // MOE_ROWS) * MOE_ROWS
    ends = jnp.cumsum(padded)
    off = ends - padded
    slot = (off[gid] + rank).reshape(t // ROW_CHUNK, 1, ROW_CHUNK)
    tile_start = jnp.arange(n_tiles, dtype=jnp.int32) * MOE_ROWS
    tile_group = jnp.minimum(jnp.sum(tile_start[:, None] >= ends[None, :], axis=1), N_EXPERT_GROUPS - 1)
    n_used = (ends[-1] // MOE_ROWS).reshape(1)
    meta = jnp.concatenate([off + cnt, ends[-1:], ends, jnp.full((1,), n_slots, jnp.int32)]).astype(jnp.int32)
    xs = _scatter(meta, slot, pay, n_slots)
    ys = _group_moe(tile_group.astype(jnp.int32), n_used.astype(jnp.int32), xs, wg16, wu16, wd16)
    return _gather(slot, h, ys)


def _moe_kernel(n2_ref, gate_ref, h_ref, wg_ref, wu_ref, wd_ref, y_ref):
    e = pl.program_id(1)

    @pl.when(e == 0)
    def _():
        y_ref[...] = h_ref[...]

    n2 = n2_ref[...]
    gu = jnp.dot(n2, wg_ref[0], preferred_element_type=F32)
    up = jnp.dot(n2, wu_ref[0], preferred_element_type=F32)
    he = (gu * (1.0 / (1.0 + jnp.exp(-gu)))) * up
    out = jnp.dot(he.astype(BF16), wd_ref[0], preferred_element_type=F32)
    lane = lax.broadcasted_iota(jnp.int32, gate_ref.shape, 1)
    gcol = jnp.sum(jnp.where(lane == e + ROUTER_LANE0, gate_ref[...], 0.0), axis=-1, keepdims=True)
    y_ref[...] += gcol * out


def _moe(n2, gate, h, wg16, wu16, wd16, tm):
    t = n2.shape[0]
    row = lambda i, e: (i, 0)
    return pl.pallas_call(
        _moe_kernel,
        grid=(t // tm, N_EXPERTS),
        in_specs=[
            pl.BlockSpec((tm, D_MODEL), row),
            pl.BlockSpec((tm, LANES), row),
            pl.BlockSpec((tm, D_MODEL), row),
            pl.BlockSpec((1, D_MODEL, D_EXPERT), lambda i, e: (e, 0, 0)),
            pl.BlockSpec((1, D_MODEL, D_EXPERT), lambda i, e: (e, 0, 0)),
            pl.BlockSpec((1, D_EXPERT, D_MODEL), lambda i, e: (e, 0, 0)),
        ],
        out_specs=pl.BlockSpec((tm, D_MODEL), row),
        out_shape=jax.ShapeDtypeStruct((t, D_MODEL), F32),
        compiler_params=pltpu.CompilerParams(
            dimension_semantics=("arbitrary", "arbitrary"), vmem_limit_bytes=VMEM_LIMIT),
        name="moe",
    )(n2, gate, h, wg16, wu16, wd16)


PROJ_TILE = 512
MOE_TILE = 1024
PROMPT_NPP = LANES // 2
SAMPLE_NPP = LANES


def _round_up(n, mult):
    return -(-n // mult) * mult


def _feature_major(x5):
    b, t = x5.shape[:2]
    return jnp.transpose(x5, (0, 2, 3, 4, 1)).reshape(b, KV_COLS, t)


def _token_major(xt):
    b, _, t = xt.shape
    return jnp.transpose(xt.reshape(b, 2, N_KV_HEADS, HEAD_DIM, t), (0, 4, 1, 2, 3))


def _pad_pages(x, npp_from, npp_to):
    b = x.shape[0]
    x = x.reshape(b, N_KV_HEADS, npp_from, LANES)
    return jnp.concatenate([x, jnp.zeros((b, N_KV_HEADS, npp_to - npp_from, LANES), x.dtype)], axis=2)


def _expand_const(n_chunks, npp):
    col = np.arange(2 * npp)
    blk = 2 * (col % npp) + col // npp
    tok_blk = np.arange(n_chunks * K_CHUNK) // CMP_BLOCK
    e = (blk[None, :, None] == tok_blk.reshape(n_chunks, 1, K_CHUNK)).astype(np.float32)
    return jnp.asarray(e, dtype=BF16)


def _gate_place_const():
    gp = np.zeros((GATE_PAD, 3 * ATTN_WIDTH), np.float32)
    for h in range(N_HEADS):
        for c in range(3):
            gp[3 * h + c, c * ATTN_WIDTH + h * HEAD_DIM:c * ATTN_WIDTH + (h + 1) * HEAD_DIM] = 1.0
    return jnp.asarray(gp, dtype=BF16)


def kernel(x_prompt, x_sample, cache_cmp_kv, cache_slc_kv, state_win_kv, state_pool, page_table, norm1_g, w_in, q_norm_g, k_norm_cmp_g, k_norm_slc_g, k_norm_win_g, cmp_pos_emb, w_cmp_k, w_cmp_v, pool_w, pool_scale, w_out, norm2_g, w_router_group, w_router_expert, w_gate, w_up, w_down):
    b, s, _ = x_prompt.shape
    db, ds, _ = x_sample.shape
    n_pool, page_rows = cache_cmp_kv.shape[:2]
    assert page_rows == PAGE and s % PAGE == 0
    past = page_table.shape[1] * page_rows

    kv0 = ATTN_WIDTH
    kv1 = kv0 + KV_ROWS
    w_row = jnp.concatenate(
        [w_in[:, :kv0], w_in[:, kv1:kv1 + GATE_COLS], jnp.zeros((D_MODEL, GATE_PAD - GATE_COLS), w_in.dtype),
         w_in[:, kv1 + GATE_COLS:]], axis=1).astype(BF16)
    w_kv = w_in[:, kv0:kv1].T.astype(BF16)
    g1 = norm1_g[None, :]
    g2 = norm2_g[None, :]
    two = lambda g: jnp.tile(g, 2)[None, :]
    kg = jnp.stack([k_norm_slc_g, k_norm_win_g])[:, :, None]
    pos_t = jnp.tile(cmp_pos_emb.T, (1, 2))
    zb = jnp.zeros((HEAD_DIM, CMP_BLOCK, HEAD_DIM), F32)

    def blockdiag(w):
        wt = jnp.transpose(w, (1, 0, 2))
        return jnp.concatenate([jnp.concatenate([wt, zb], axis=2), jnp.concatenate([zb, wt], axis=2)], axis=1)

    bdk, bdv = blockdiag(w_cmp_k), blockdiag(w_cmp_v)
    zq = jnp.zeros_like(bdk)
    bd = jnp.concatenate([jnp.concatenate([bdk, zq], axis=2), jnp.concatenate([zq, bdv], axis=2)],
                         axis=1).astype(BF16)
    w_router = jnp.concatenate(
        [w_router_group, w_router_expert,
         jnp.zeros((D_MODEL, LANES - N_EXPERT_GROUPS - N_EXPERTS), w_router_group.dtype)], axis=1).astype(BF16)
    w_out16 = w_out.astype(BF16)
    pool_w16 = pool_w.astype(BF16)
    wg16, wu16, wd16 = w_gate.astype(BF16), w_up.astype(BF16), w_down.astype(BF16)
    ps = pool_scale[None, :]

    def ffn_moe(x2d, a2d, m2d, tm_ffn, tm_moe, routed):
        if routed:
            return _routed_moe(*_ffn(x2d, a2d, m2d, w_out16, g2, w_router, tm_ffn, True), wg16, wu16, wd16)
        h, n2, gate = _ffn(x2d, a2d, m2d, w_out16, g2, w_router, tm_ffn, False)
        return _moe(n2, gate, h, wg16, wu16, wd16, tm_moe)

    q, gates, u, kvc_t, kvs_t, kvw_t, ks16, vsx, kw16, vwx = _proj(
        x_prompt, g1, w_row, w_kv, two(q_norm_g), kg, PROJ_TILE)
    m_p = _pool(u, pool_w16, ps, s)
    pages_p = s // PAGE
    pt_p = jnp.zeros((b, pages_p), jnp.int32)
    dummy_tail = jnp.zeros((b, KV_COLS, LANES), F32)
    kc_p, vc_p = _compress(pt_p, kvc_t, dummy_tail, pos_t, bd, two(k_norm_cmp_g),
                           npp=pages_p, has_tail=False, paged=False)
    kc_p = _pad_pages(kc_p, pages_p, PROMPT_NPP)
    vc_p = _pad_pages(vc_p, pages_p, PROMPT_NPP)
    a_p = _attn_prompt(q, gates, kc_p, vc_p, ks16, vsx, kw16, vwx, _expand_const(s // K_CHUNK, PROMPT_NPP),
                       _gate_place_const(), pages_p)
    y_p = ffn_moe(x_prompt.reshape(b * s, D_MODEL), a_p.reshape(b * s, ATTN_WIDTH),
                  m_p.reshape(b * s, POOL_WIDTH), PROJ_TILE, MOE_TILE, True)

    ts = db * ds
    q_s, gates_s, u_s, kvc_st, kvs_st, kvw_st, _, _, _, _ = _proj(
        x_sample.reshape(1, ts, D_MODEL), g1, w_row, w_kv, two(q_norm_g), kg, ts)
    u_s = u_s.reshape(db, ds, POOL_WIDTH)
    pool_ext = jnp.concatenate([state_pool, u_s], axis=1)
    n_ext = pool_ext.shape[1]
    lead = _round_up(n_ext, 8) - n_ext
    pool_in = jnp.concatenate([jnp.zeros((db, lead, POOL_WIDTH), F32), pool_ext], axis=1)
    n_in = lead + n_ext
    m_s = _pool(pool_in.reshape(1, db * n_in, POOL_WIDTH), pool_w16, ps, n_in).reshape(db, n_in, POOL_WIDTH)[:, n_in - ds:]

    def new_rows_t(xt):
        x = jnp.transpose(xt.reshape(KV_COLS, db, ds), (1, 0, 2))
        return jnp.concatenate([x, jnp.zeros((db, KV_COLS, LANES - ds), F32)], axis=2)

    kvc_new, kvs_new, kvw_new = new_rows_t(kvc_st), new_rows_t(kvs_st), new_rows_t(kvw_st)
    n_pages = page_table.shape[1]
    npp_c = _round_up(n_pages + 1, 8)
    cmp_pages = _feature_major(cache_cmp_kv)
    slc_pages = _feature_major(cache_slc_kv)
    kc_s, vc_s = _compress(page_table, cmp_pages, kvc_new, pos_t, bd, two(k_norm_cmp_g),
                           npp=npp_c, has_tail=True, paged=True)
    kc_s = _pad_pages(kc_s, npp_c, SAMPLE_NPP)
    vc_s = _pad_pages(vc_s, npp_c, SAMPLE_NPP)
    win_t = _feature_major(state_win_kv)
    q_s3 = q_s.reshape(db, ds, ATTN_WIDTH)
    o_cmp_s, sel_s, need_s = _select_sample(q_s3, kc_s, vc_s, past=past)
    need = need_s[:, 0, :n_pages] > 0.5
    order = jnp.argsort(jnp.logical_not(need), axis=1, stable=True).astype(jnp.int32)
    n_need = jnp.sum(need, axis=1).astype(jnp.int32)
    a_s = _attn_sample(page_table, order, n_need, q_s3, gates_s.reshape(db, ds, GATE_PAD), o_cmp_s, sel_s,
                       kvs_new, kvw_new, win_t, slc_pages, past=past)
    y_s = ffn_moe(x_sample.reshape(ts, D_MODEL), a_s.reshape(ts, ATTN_WIDTH), m_s.reshape(ts, POOL_WIDTH), ts, ts, False)

    win_keep = min(WINDOW, s)
    sample5 = lambda xt: jnp.transpose(xt.reshape(2, N_KV_HEADS, HEAD_DIM, db, ds), (3, 4, 0, 1, 2))
    win_ctx_t = jnp.concatenate([win_t, kvw_new[:, :, :ds]], axis=2)
    return (y_p.reshape(b, s, D_MODEL), y_s.reshape(db, ds, D_MODEL),
            _token_major(kvc_t), sample5(kvc_st),
            _token_major(kvs_t), sample5(kvs_st),
            _token_major(kvw_t[:, :, s - win_keep:]), _token_major(win_ctx_t[:, :, ds:]),
            u[:, s - POOL_BUF:], pool_ext[:, ds:])
```

```python
import functools

import jax
import jax.numpy as jnp
import numpy as np
from jax import lax
from jax.experimental import pallas as pl
from jax.experimental.pallas import tpu as pltpu

F32 = jnp.float32
BF16 = jnp.bfloat16

D_MODEL = 1024
N_HEADS = 8
HEAD_DIM = 64
N_KV_HEADS = 2
Q_PER_KV = N_HEADS // N_KV_HEADS
ATTN_WIDTH = N_HEADS * HEAD_DIM
KV_COLS = 2 * N_KV_HEADS * HEAD_DIM
GATE_COLS = 3 * N_HEADS
POOL_WIDTH = D_MODEL - ATTN_WIDTH
POOL_WINDOWS = (2, 4, 8, 16)
POOL_GROUP_WIDTH = POOL_WIDTH // len(POOL_WINDOWS)
POOL_BUF = max(POOL_WINDOWS) - 1
CMP_BLOCK = 64
TOP_K_BLOCKS = 16
WINDOW = 512
FORCE_SCORE = 1.0e4
N_EXPERT_GROUPS = 4
EXPERTS_PER_GROUP = 4
N_EXPERTS = N_EXPERT_GROUPS * EXPERTS_PER_GROUP
D_EXPERT = 512
EPS = 1e-6
NEG = -1e30
SCALE = HEAD_DIM ** -0.5

LANES = 128
PAGE = 2 * CMP_BLOCK
GATE_PAD = LANES
ROW_COLS = ATTN_WIDTH + GATE_PAD + POOL_WIDTH
OFF_GATE = ATTN_WIDTH
OFF_U = OFF_GATE + GATE_PAD
KV_ROWS = 3 * KV_COLS
ROUTER_LANE0 = N_EXPERT_GROUPS
Q_TILE = 256
K_CHUNK = 256
VMEM_LIMIT = 56 * 1024 * 1024


def _slope(h):
    return float(2.0 ** (-8.0 * (h + 1) / N_HEADS))


def _half_group_norm(v, g):
    lane = lax.broadcasted_iota(jnp.int32, v.shape, 1)
    lo = lane < HEAD_DIM
    v2 = v * v
    s_lo = jnp.sum(jnp.where(lo, v2, 0.0), axis=-1, keepdims=True)
    s_hi = jnp.sum(jnp.where(lo, 0.0, v2), axis=-1, keepdims=True)
    r_lo = lax.rsqrt(s_lo * (1.0 / HEAD_DIM) + EPS)
    r_hi = lax.rsqrt(s_hi * (1.0 / HEAD_DIM) + EPS)
    return (v * jnp.where(lo, r_lo, r_hi)) * g


def _dot_t(a, b):
    return lax.dot_general(a, b, (((1,), (1,)), ((), ())), preferred_element_type=F32)


def _proj_kernel(x_ref, g1_ref, wr_ref, wkv_ref, qg_ref, kg_ref,
                 q_ref, gates_ref, u_ref, kvc_ref, kvs_ref, kvw_ref, ks16_ref, vsx_ref, kw16_ref, vwx_ref):
    x = x_ref[0]
    tm = x.shape[0]
    ms = jnp.mean(x * x, axis=-1, keepdims=True)
    n = ((x * lax.rsqrt(ms + EPS)) * g1_ref[...]).astype(BF16)
    p = jnp.dot(n, wr_ref[...], preferred_element_type=F32)
    pt = _dot_t(wkv_ref[...], n)
    for t in range(ATTN_WIDTH // LANES):
        sl = slice(t * LANES, (t + 1) * LANES)
        q_ref[0, :, sl] = _half_group_norm(p[:, sl], qg_ref[...]).astype(BF16)
    gates_ref[0] = 1.0 / (1.0 + jnp.exp(-p[:, OFF_GATE:OFF_GATE + GATE_PAD]))
    u_ref[0] = p[:, OFF_U:OFF_U + POOL_WIDTH]
    kvc_ref[0] = pt[:KV_COLS]
    half = KV_COLS // 2
    lane = lax.broadcasted_iota(jnp.int32, (tm, LANES), 1)
    for bi, (out32, k16, vx16) in enumerate(((kvs_ref, ks16_ref, vsx_ref), (kvw_ref, kw16_ref, vwx_ref))):
        off = (bi + 1) * KV_COLS
        heads = []
        for hh in range(N_KV_HEADS):
            kh = pt[off + hh * HEAD_DIM:off + (hh + 1) * HEAD_DIM]
            msk = jnp.mean(kh * kh, axis=0, keepdims=True)
            heads.append((kh * lax.rsqrt(msk + EPS)) * kg_ref[bi])
        kn = jnp.concatenate(heads, axis=0)
        v = pt[off + half:off + KV_COLS]
        out32[0, :half, :] = kn
        out32[0, half:, :] = v
        vt = v.T
        vx = (jnp.where(lane < HEAD_DIM, vt, 1.0), jnp.where(lane < HEAD_DIM, pltpu.roll(vt, HEAD_DIM, axis=1), 1.0))
        for c in range(tm // K_CHUNK):
            cs = slice(c * K_CHUNK, (c + 1) * K_CHUNK)
            k16[0, c] = kn[:, cs].astype(BF16)
            for hh in range(N_KV_HEADS):
                vx16[0, c, hh] = vx[hh][cs].astype(BF16)


def _proj(x3d, g1, w_row, w_kv, qg, kg, tm):
    b, s, _ = x3d.shape
    tok = lambda i, j: (i, j, 0)
    feat = lambda i, j: (i, 0, j)
    const2 = lambda i, j: (0, 0)
    nck = tm // K_CHUNK
    out_shape = (
        jax.ShapeDtypeStruct((b, s, ATTN_WIDTH), BF16),
        jax.ShapeDtypeStruct((b, s, GATE_PAD), F32),
        jax.ShapeDtypeStruct((b, s, POOL_WIDTH), F32),
        jax.ShapeDtypeStruct((b, KV_COLS, s), F32),
        jax.ShapeDtypeStruct((b, KV_COLS, s), F32),
        jax.ShapeDtypeStruct((b, KV_COLS, s), F32),
        jax.ShapeDtypeStruct((b, s // K_CHUNK, LANES, K_CHUNK), BF16),
        jax.ShapeDtypeStruct((b, s // K_CHUNK, N_KV_HEADS, K_CHUNK, LANES), BF16),
        jax.ShapeDtypeStruct((b, s // K_CHUNK, LANES, K_CHUNK), BF16),
        jax.ShapeDtypeStruct((b, s // K_CHUNK, N_KV_HEADS, K_CHUNK, LANES), BF16),
    )
    k_spec = pl.BlockSpec((1, nck, LANES, K_CHUNK), lambda i, j: (i, j, 0, 0))
    v_spec = pl.BlockSpec((1, nck, N_KV_HEADS, K_CHUNK, LANES), lambda i, j: (i, j, 0, 0, 0))
    out_specs = (
        pl.BlockSpec((1, tm, ATTN_WIDTH), tok),
        pl.BlockSpec((1, tm, GATE_PAD), tok),
        pl.BlockSpec((1, tm, POOL_WIDTH), tok),
        pl.BlockSpec((1, KV_COLS, tm), feat),
        pl.BlockSpec((1, KV_COLS, tm), feat),
        pl.BlockSpec((1, KV_COLS, tm), feat),
        k_spec, v_spec, k_spec, v_spec,
    )
    return pl.pallas_call(
        _proj_kernel,
        grid=(b, s // tm),
        in_specs=[
            pl.BlockSpec((1, tm, D_MODEL), tok),
            pl.BlockSpec((1, D_MODEL), const2),
            pl.BlockSpec((D_MODEL, ROW_COLS), const2),
            pl.BlockSpec((KV_ROWS, D_MODEL), const2),
            pl.BlockSpec((1, LANES), const2),
            pl.BlockSpec((2, HEAD_DIM, 1), lambda i, j: (0, 0, 0)),
        ],
        out_specs=out_specs,
        out_shape=out_shape,
        compiler_params=pltpu.CompilerParams(
            dimension_semantics=("arbitrary", "arbitrary"), vmem_limit_bytes=VMEM_LIMIT),
        name="proj",
    )(x3d, g1, w_row, w_kv, qg, kg)


def _pool_kernel(u_ref, pw_ref, ps_ref, m_ref, *, seg):
    u = u_ref[0]
    n = u.shape[0]

    def shift(v, k):
        rolled = pltpu.roll(v, k, axis=0)
        r = lax.broadcasted_iota(jnp.int32, v.shape, 0) % seg
        return jnp.where(r >= k, rolled, 0.0)

    sums = []
    s = u
    k = 1
    for gi in range(len(POOL_WINDOWS)):
        s = s + shift(s, k)
        k *= 2
        sums.append(s[:, :POOL_GROUP_WIDTH])
        s = s[:, POOL_GROUP_WIDTH:]
    row = lax.broadcasted_iota(jnp.int32, (n, 1), 0) % seg
    for gi, w in enumerate(POOL_WINDOWS):
        sl = slice(gi * POOL_GROUP_WIDTH, (gi + 1) * POOL_GROUP_WIDTH)
        cnt = jnp.minimum(row + 1, w).astype(F32)
        d = sums[gi] / cnt - u[:, sl]
        y = jnp.dot(d.astype(BF16), pw_ref[gi], preferred_element_type=F32)
        m_ref[0, :, sl] = (y * ps_ref[:, sl]).astype(BF16)


def _pool(u3d, pool_w16, pool_scale, seg):
    b, n, _ = u3d.shape
    assert n % seg == 0
    return pl.pallas_call(
        functools.partial(_pool_kernel, seg=seg),
        grid=(b,),
        in_specs=[
            pl.BlockSpec((1, n, POOL_WIDTH), lambda i: (i, 0, 0)),
            pl.BlockSpec((len(POOL_WINDOWS), POOL_GROUP_WIDTH, POOL_GROUP_WIDTH), lambda i: (0, 0, 0)),
            pl.BlockSpec((1, POOL_WIDTH), lambda i: (0, 0)),
        ],
        out_specs=pl.BlockSpec((1, n, POOL_WIDTH), lambda i: (i, 0, 0)),
        out_shape=jax.ShapeDtypeStruct((b, n, POOL_WIDTH), BF16),
        compiler_params=pltpu.CompilerParams(
            dimension_semantics=("arbitrary",), vmem_limit_bytes=VMEM_LIMIT),
        name="pool",
    )(u3d, pool_w16, pool_scale)


def _compress_kernel(pt_ref, src_ref, tail_ref, pos_ref, bd_ref, g_ref, kc_ref, vc_ref, buf, sem,
                     *, n_pages, npp, has_tail, paged):
    b = pl.program_id(0)
    nb = pl.num_programs(0)
    n_slabs = 2 * N_KV_HEADS

    def slab_copy(row, p, c, slot):
        kv, kvh = divmod(c, N_KV_HEADS)
        if paged:
            src = src_ref.at[pt_ref[row, p], pl.ds(c * HEAD_DIM, HEAD_DIM), :]
        else:
            src = src_ref.at[row, pl.ds(c * HEAD_DIM, HEAD_DIM), pl.ds(p * PAGE, PAGE)]
        return pltpu.make_async_copy(src, buf.at[slot, kv, :, kvh * npp + p, :], sem.at[slot])

    def tail_copy(c, slot):
        kv, kvh = divmod(c, N_KV_HEADS)
        return pltpu.make_async_copy(tail_ref.at[0, pl.ds(c * HEAD_DIM, HEAD_DIM), :],
                                     buf.at[slot, kv, :, kvh * npp + n_pages, :], sem.at[slot])

    def row_copies(row, slot, fn):
        def body(p, carry):
            for c in range(n_slabs):
                fn(slab_copy(row, p, c, slot))
            return carry
        lax.fori_loop(0, n_pages, body, 0)

    n_real = n_pages + (1 if has_tail else 0)

    @pl.when(b == 0)
    def _():
        if npp > n_real:
            for kvh in range(N_KV_HEADS):
                buf[:, :, :, kvh * npp + n_real:(kvh + 1) * npp, :] = jnp.zeros(
                    (2, 2, HEAD_DIM, npp - n_real, LANES), F32)
        row_copies(0, 0, lambda cp: cp.start())

    slot = b % 2

    if has_tail:
        for c in range(n_slabs):
            tail_copy(c, slot).start()
        for c in range(n_slabs):
            tail_copy(c, slot).wait()
    row_copies(b, slot, lambda cp: cp.wait())

    rows = N_KV_HEADS * npp
    d_per_it = 8
    n_it = HEAD_DIM // d_per_it
    pages_per_it = n_pages // n_it
    nxt = jnp.minimum(b + 1, nb - 1)

    def body(it, acc):
        for pp in range(pages_per_it):
            for c in range(n_slabs):
                slab_copy(nxt, it * pages_per_it + pp, c, 1 - slot).start()
        for dd in range(d_per_it):
            d = it * d_per_it + dd
            pos = pos_ref[pl.ds(d, 1), :]
            x = jnp.concatenate([buf[slot, 0, d] + pos, buf[slot, 1, d] + pos], axis=1).astype(BF16)
            acc = acc + jnp.dot(x, bd_ref[d], preferred_element_type=F32)
        return acc

    acc = lax.fori_loop(0, n_it, body, jnp.zeros((rows, 2 * LANES), F32))
    kc_ref[0] = _half_group_norm(acc[:, :LANES], g_ref[...])
    vc_ref[0] = acc[:, LANES:]

    @pl.when(b == nb - 1)
    def _():
        row_copies(nxt, 1 - slot, lambda cp: cp.wait())


def _compress(page_table, src, tail, pos_t, bd, g2, *, npp, has_tail, paged):
    b, n_pages = page_table.shape
    kern = functools.partial(_compress_kernel, n_pages=n_pages, npp=npp, has_tail=has_tail, paged=paged)
    rows = N_KV_HEADS * npp
    grid_spec = pltpu.PrefetchScalarGridSpec(
        num_scalar_prefetch=1,
        grid=(b,),
        in_specs=[
            pl.BlockSpec(memory_space=pl.ANY),
            pl.BlockSpec((1, KV_COLS, LANES), lambda i, pt: (i, 0, 0)),
            pl.BlockSpec((HEAD_DIM, LANES), lambda i, pt: (0, 0)),
            pl.BlockSpec((HEAD_DIM, 2 * LANES, 2 * LANES), lambda i, pt: (0, 0, 0)),
            pl.BlockSpec((1, LANES), lambda i, pt: (0, 0)),
        ],
        out_specs=(pl.BlockSpec((1, rows, LANES), lambda i, pt: (i, 0, 0)),
                   pl.BlockSpec((1, rows, LANES), lambda i, pt: (i, 0, 0))),
        scratch_shapes=[
            pltpu.VMEM((2, 2, HEAD_DIM, rows, LANES), F32),
            pltpu.SemaphoreType.DMA((2,)),
        ],
    )
    return pl.pallas_call(
        kern,
        grid_spec=grid_spec,
        out_shape=(jax.ShapeDtypeStruct((b, rows, LANES), F32), jax.ShapeDtypeStruct((b, rows, LANES), F32)),
        compiler_params=pltpu.CompilerParams(
            dimension_semantics=("arbitrary",), vmem_limit_bytes=VMEM_LIMIT),
        name="compress",
    )(page_table, src, tail, pos_t, bd, g2)


def _to_half(tile, src_half, dst_half):
    lane = lax.broadcasted_iota(jnp.int32, tile.shape, 1)
    src = tile if src_half == dst_half else pltpu.roll(tile, HEAD_DIM, axis=1)
    keep = (lane < HEAD_DIM) if dst_half == 0 else (lane >= HEAD_DIM)
    return jnp.where(keep, src, 0.0)


def _pair_tile(o_even, o_odd, half):
    lane = lax.broadcasted_iota(jnp.int32, o_even.shape, 1)
    if half == 0:
        return jnp.where(lane < HEAD_DIM, o_even, pltpu.roll(o_odd, HEAD_DIM, axis=1))
    return jnp.where(lane < HEAD_DIM, pltpu.roll(o_even, HEAD_DIM, axis=1), o_odd)


def _gate_tile(gates, pair, c, shape):
    lane = lax.broadcasted_iota(jnp.int32, shape, 1)
    he, ho = 2 * pair, 2 * pair + 1
    return jnp.where(lane < HEAD_DIM, gates[:, 3 * he + c:3 * he + c + 1], gates[:, 3 * ho + c:3 * ho + c + 1])


def _block_of_col(col, npp):
    return 2 * (col % npp) + col // npp


def _cmp_operand(x, npp):
    return jnp.concatenate([_to_half(x, 0, 0), _to_half(x, 1, 0)], axis=0).astype(BF16)


def _rank_select(score_t, blk_t, cols):
    rank = jnp.zeros(score_t.shape, jnp.int32)
    for r, n in cols:
        row = score_t[r:r + 1, :]
        ahead = (row > score_t) | ((row == score_t) & (blk_t > n))
        rank = rank + ahead.astype(jnp.int32)
    return ((rank < TOP_K_BLOCKS) & (score_t > -0.5)).astype(F32)


POS_HI, POS_LO = HEAD_DIM, HEAD_DIM + 1
MASK_BIG = 1e30


def _attn_prompt_kernel(q_ref, gates_ref, kc_ref, vc_ref, ks_ref, vsx_ref, kw_ref, vwx_ref, exp_ref, gp_ref, a_ref,
                        zbuf, m_sc, acc_sc, *, n_pages, npp):
    i = pl.program_id(1)
    q0 = i * Q_TILE
    rows = Q_PER_KV * Q_TILE
    ncols = 2 * npp
    qpos = q0 + lax.broadcasted_iota(jnp.int32, (Q_TILE, 1), 0)
    blk = _block_of_col(lax.broadcasted_iota(jnp.int32, (Q_TILE, ncols), 1), npp)
    real_cols = [(half * n_pages + p, 2 * p + half) for half in range(2) for p in range(n_pages)]

    g = gates_ref[0]
    g_hi = g.astype(BF16)
    g_lo = (g - g_hi.astype(F32)).astype(BF16)
    gexp = (jnp.dot(g_hi, gp_ref[...], preferred_element_type=F32)
            + jnp.dot(g_lo, gp_ref[...], preferred_element_type=F32))

    r_key = lax.broadcasted_iota(jnp.int32, (Q_TILE, K_CHUNK), 1)
    r_qry = lax.broadcasted_iota(jnp.int32, (Q_TILE, K_CHUNK), 0)
    keep = {"causal": r_key <= r_qry, "lower": r_key > r_qry}

    all_rows = N_HEADS * Q_TILE

    def branch(lhs, k_ref, vx_ref, slc, chunks, c_lo, c_hi):
        m_sc[...] = jnp.full((all_rows, LANES), NEG, F32)

        def scores(c, kind):
            rr = lax.broadcasted_iota(jnp.int32, (HEAD_DIM, K_CHUNK), 0)
            tt = lax.broadcasted_iota(jnp.int32, (HEAD_DIM, K_CHUNK), 1).astype(F32)
            hi = ((c - i) * K_CHUNK).astype(F32)
            pos = jnp.where(rr == 0, hi, jnp.where(rr == 1, tt, 0.0)).astype(BF16)
            for kvh in range(N_KV_HEADS):
                parts = [k_ref[0, c, kvh * HEAD_DIM:(kvh + 1) * HEAD_DIM, :], pos]
                if slc:
                    parts.append(exp_ref[c])
                kx = jnp.concatenate(parts, axis=0)
                for gq in range(Q_PER_KV):
                    sl = slice((kvh * Q_PER_KV + gq) * Q_TILE, (kvh * Q_PER_KV + gq + 1) * Q_TILE)
                    zg = jnp.dot(lhs[sl], kx, preferred_element_type=F32)
                    if kind is not None:
                        zg = jnp.where(keep[kind], zg, NEG)
                    zbuf[c, sl, :] = zg
                    m_sc[sl] = jnp.maximum(m_sc[sl], jnp.maximum(zg[:, :LANES], zg[:, LANES:]))

        def in_pairs(lo, hi, one):
            n = hi - lo

            def body(k, carry):
                one(lo + 2 * k)
                one(lo + 2 * k + 1)
                return carry
            lax.fori_loop(0, n // 2, body, 0)
            pl.when(n % 2 == 1)(lambda: one(hi - 1))

        for group in chunks:
            if not isinstance(group[1], list):
                in_pairs(group[0], group[1], lambda c: scores(c, None))
            else:
                cond, items = group[0], group[1]

                def run(items=items):
                    for c, kind in items:
                        scores(c, kind)
                run() if cond is None else pl.when(cond)(run)

        m = jnp.broadcast_to(jnp.max(m_sc[...], axis=-1, keepdims=True), (all_rows, LANES))
        acc_sc[...] = jnp.zeros((all_rows, LANES), F32)

        def weigh(c):
            for h in range(N_HEADS):
                sl = slice(h * Q_TILE, (h + 1) * Q_TILE)
                z = zbuf[c, sl, :]
                e = jnp.concatenate([jnp.exp(z[:, :LANES] - m[sl]), jnp.exp(z[:, LANES:] - m[sl])], axis=1)
                acc_sc[sl] += jnp.dot(e.astype(BF16), vx_ref[0, c, h // Q_PER_KV], preferred_element_type=F32)

        in_pairs(c_lo, c_hi, weigh)
        return acc_sc[...]

    def pair_norm(acc_e, acc_o):
        lane = lax.broadcasted_iota(jnp.int32, acc_e.shape, 1)
        safe = lambda d: jnp.where(d > 0.0, d, 1.0)
        sw_e = pltpu.roll(acc_e, HEAD_DIM, axis=1)
        sw_o = pltpu.roll(acc_o, HEAD_DIM, axis=1)
        return jnp.where(lane < HEAD_DIM, acc_e / safe(sw_e), sw_o / safe(acc_o))

    o_cmp_all, lhs_slc_all, lhs_win_all = [], [], []
    for kvh in range(N_KV_HEADS):
        slopes = [_slope(kvh * Q_PER_KV + g) for g in range(Q_PER_KV)]
        lane = lax.broadcasted_iota(jnp.int32, (Q_TILE, LANES), 1)
        pieces, pieces_x = [], []
        for gq in range(Q_PER_KV):
            h = kvh * Q_PER_KV + gq
            tile = q_ref[0, :, (h // 2) * LANES:(h // 2 + 1) * LANES].astype(F32) * SCALE
            low = _to_half(tile, h % 2, 0)
            pieces.append(low)
            pieces_x.append(jnp.where((lane == POS_HI) | (lane == POS_LO), slopes[gq], low))
        qk = jnp.concatenate(pieces, axis=0).astype(BF16)
        qk_x = jnp.concatenate(pieces_x, axis=0).astype(BF16)

        kc = _cmp_operand(kc_ref[0, kvh], npp)
        vc = _cmp_operand(vc_ref[0, kvh], npp)
        s = _dot_t(qk, kc)
        dist_c = qpos - (blk * CMP_BLOCK + (CMP_BLOCK - 1))
        mask_c = dist_c >= 0
        dist_cf = dist_c.astype(F32)
        p_list = []
        p_kv = jnp.zeros((Q_TILE, ncols), F32)
        for g in range(Q_PER_KV):
            z = jnp.where(mask_c, s[g * Q_TILE:(g + 1) * Q_TILE] - slopes[g] * dist_cf, NEG)
            mx = jnp.max(z, axis=-1, keepdims=True)
            e = jnp.where(mask_c, jnp.exp(z - mx), 0.0)
            den = jnp.sum(e, axis=-1, keepdims=True)
            p = e / jnp.where(den > 0.0, den, 1.0)
            p_kv = p_kv + p
            p_list.append(p.astype(BF16))
        o_cmp = jnp.dot(jnp.concatenate(p_list, axis=0), vc, preferred_element_type=F32)

        cand = blk * CMP_BLOCK <= qpos
        forced = (blk == qpos // CMP_BLOCK) | (blk == 0)
        score = jnp.where(forced, FORCE_SCORE, jnp.where(cand, p_kv, -1.0))
        score_t = score.T
        st = jnp.concatenate([score_t[:n_pages], score_t[npp:npp + n_pages]], axis=0)
        rr = lax.broadcasted_iota(jnp.int32, (2 * n_pages, Q_TILE), 0)
        bt = jnp.where(rr < n_pages, 2 * rr, 2 * (rr - n_pages) + 1)
        sel_s = _rank_select(st, bt, [(r, n) for r, (_, n) in enumerate(real_cols)])
        pad = jnp.zeros((npp - n_pages, Q_TILE), F32)
        sel_t = jnp.concatenate([sel_s[:n_pages], pad, sel_s[n_pages:], pad], axis=0)
        sel = sel_t.T
        sel_bias = jnp.where(sel > 0.5, 0.0, -MASK_BIG).astype(BF16)
        lhs_slc_all.append(jnp.concatenate([qk_x, jnp.concatenate([sel_bias] * Q_PER_KV, axis=0)], axis=1))
        lhs_win_all.append(qk_x)
        o_cmp_all.append(o_cmp)

    o_slc = branch(jnp.concatenate(lhs_slc_all, axis=0), ks_ref, vsx_ref, True,
                   [(0, i), (None, [(i, "causal")])], 0, i + 1)
    o_win = branch(jnp.concatenate(lhs_win_all, axis=0), kw_ref, vwx_ref, False,
                   [(i >= 2, [(i - 2, "lower"), (i - 1, None), (i, "causal")]),
                    (i == 1, [(0, None), (1, "causal")]),
                    (i == 0, [(0, "causal")])],
                   jnp.maximum(i - 2, 0), i + 1)
    o_cmp = jnp.concatenate(o_cmp_all, axis=0)

    for pair in range(N_HEADS // 2):
        e_sl = slice(2 * pair * Q_TILE, (2 * pair + 1) * Q_TILE)
        o_sl = slice((2 * pair + 1) * Q_TILE, (2 * pair + 2) * Q_TILE)
        col = pair * LANES
        tile = (gexp[:, col:col + LANES] * _pair_tile(o_cmp[e_sl], o_cmp[o_sl], 0)
                + gexp[:, ATTN_WIDTH + col:ATTN_WIDTH + col + LANES] * pair_norm(o_slc[e_sl], o_slc[o_sl])
                + gexp[:, 2 * ATTN_WIDTH + col:2 * ATTN_WIDTH + col + LANES] * pair_norm(o_win[e_sl], o_win[o_sl]))
        a_ref[0, :, pair * LANES:(pair + 1) * LANES] = tile.astype(BF16)


def _attn_prompt(q, gates, kc, vc, ks16, vsx, kw16, vwx, expand, gate_place, n_pages):
    b, s, _ = q.shape
    npp = kc.shape[2]
    n_chunks = s // K_CHUNK
    rows = Q_PER_KV * Q_TILE
    assert WINDOW == 2 * K_CHUNK and Q_TILE == K_CHUNK
    kern = functools.partial(_attn_prompt_kernel, n_pages=n_pages, npp=npp)
    per_b4 = lambda bi, i: (bi, 0, 0, 0)
    per_b5 = lambda bi, i: (bi, 0, 0, 0, 0)
    return pl.pallas_call(
        kern,
        grid=(b, s // Q_TILE),
        in_specs=[
            pl.BlockSpec((1, Q_TILE, ATTN_WIDTH), lambda bi, i: (bi, i, 0)),
            pl.BlockSpec((1, Q_TILE, GATE_PAD), lambda bi, i: (bi, i, 0)),
            pl.BlockSpec((1, N_KV_HEADS, npp, LANES), per_b4),
            pl.BlockSpec((1, N_KV_HEADS, npp, LANES), per_b4),
            pl.BlockSpec((1, n_chunks, LANES, K_CHUNK), per_b4),
            pl.BlockSpec((1, n_chunks, N_KV_HEADS, K_CHUNK, LANES), per_b5),
            pl.BlockSpec((1, n_chunks, LANES, K_CHUNK), per_b4),
            pl.BlockSpec((1, n_chunks, N_KV_HEADS, K_CHUNK, LANES), per_b5),
            pl.BlockSpec((n_chunks, 2 * npp, K_CHUNK), lambda bi, i: (0, 0, 0)),
            pl.BlockSpec((GATE_PAD, 3 * ATTN_WIDTH), lambda bi, i: (0, 0)),
        ],
        out_specs=pl.BlockSpec((1, Q_TILE, ATTN_WIDTH), lambda bi, i: (bi, i, 0)),
        out_shape=jax.ShapeDtypeStruct((b, s, ATTN_WIDTH), BF16),
        scratch_shapes=[
            pltpu.VMEM((n_chunks, N_KV_HEADS * rows, K_CHUNK), F32),
            pltpu.VMEM((N_KV_HEADS * rows, LANES), F32),
            pltpu.VMEM((N_KV_HEADS * rows, LANES), F32),
        ],
        compiler_params=pltpu.CompilerParams(
            dimension_semantics=("arbitrary", "arbitrary"), vmem_limit_bytes=VMEM_LIMIT),
        name="attn_prompt",
    )(q, gates, kc, vc, ks16, vsx, kw16, vwx, expand, gate_place)


S_ROWS = LANES
S_CHUNK = 1024


def _sample_rows(q_ref, ds, past):
    n_real = N_HEADS * ds
    by_kvh, low = [], []
    for h in range(N_HEADS):
        tile = q_ref[0, :, (h // 2) * LANES:(h // 2 + 1) * LANES].astype(F32) * SCALE
        by_kvh.append(_to_half(tile, h % 2, h // Q_PER_KV))
        low.append(_to_half(tile, h % 2, 0))
    zpad = jnp.zeros((S_ROWS - n_real, LANES), F32)
    qrows = jnp.concatenate(by_kvh + [zpad], axis=0).astype(BF16)
    qlow = jnp.concatenate(low + [zpad], axis=0).astype(BF16)
    r_col = lax.broadcasted_iota(jnp.int32, (S_ROWS, 1), 0)
    q_of_r = r_col % ds
    h_of_r = r_col // ds
    sl_r = jnp.zeros((S_ROWS, 1), F32)
    for h in range(N_HEADS):
        sl_r = jnp.where(h_of_r == h, _slope(h), sl_r)
    return qrows, qlow, q_of_r, past + q_of_r, sl_r, h_of_r // Q_PER_KV


def _select_sample_kernel(q_ref, kc_ref, vc_ref, ocmp_ref, sel_ref, need_ref, *, past, ds, npp):
    n_real = N_HEADS * ds
    ncols = 2 * npp
    n_blocks = (past + ds + CMP_BLOCK - 1) // CMP_BLOCK
    _, qlow, _, qp_r, sl_r, kvh_r = _sample_rows(q_ref, ds, past)

    blk = _block_of_col(lax.broadcasted_iota(jnp.int32, (S_ROWS, ncols), 1), npp)
    s = jnp.where(kvh_r == 0, _dot_t(qlow, _cmp_operand(kc_ref[0, 0], npp)),
                  _dot_t(qlow, _cmp_operand(kc_ref[0, 1], npp)))
    dist_c = qp_r - (blk * CMP_BLOCK + (CMP_BLOCK - 1))
    mask_c = dist_c >= 0
    z = jnp.where(mask_c, s - sl_r * dist_c.astype(F32), NEG)
    mx = jnp.max(z, axis=-1, keepdims=True)
    e = jnp.where(mask_c, jnp.exp(z - mx), 0.0)
    den = jnp.sum(e, axis=-1, keepdims=True)
    p = e / jnp.where(den > 0.0, den, 1.0)
    pb = p.astype(BF16)
    o_cmp = jnp.where(kvh_r == 0, jnp.dot(pb, _cmp_operand(vc_ref[0, 0], npp), preferred_element_type=F32),
                      jnp.dot(pb, _cmp_operand(vc_ref[0, 1], npp), preferred_element_type=F32))

    kq = N_KV_HEADS * ds
    p_kv = []
    for kvh in range(N_KV_HEADS):
        acc = jnp.zeros((ds, ncols), F32)
        for g in range(Q_PER_KV):
            r0 = (kvh * Q_PER_KV + g) * ds
            acc = acc + p[r0:r0 + ds]
        p_kv.append(acc)
    p_kv = jnp.concatenate(p_kv, axis=0)
    blk2 = _block_of_col(lax.broadcasted_iota(jnp.int32, (kq, ncols), 1), npp)
    qp2 = past + lax.broadcasted_iota(jnp.int32, (kq, 1), 0) % ds
    cand = blk2 * CMP_BLOCK <= qp2
    forced = (blk2 == qp2 // CMP_BLOCK) | (blk2 == 0)
    score = jnp.where(forced, FORCE_SCORE, jnp.where(cand, p_kv, -1.0))
    rank = jnp.zeros((kq, ncols), jnp.int32)
    for n in range(n_blocks):
        c = (n % 2) * npp + n // 2
        col = score[:, c:c + 1]
        ahead = (col > score) | ((col == score) & (blk2 > n))
        rank = rank + ahead.astype(jnp.int32)
    sel2 = ((rank < TOP_K_BLOCKS) & (score > -0.5)).astype(F32)
    sel_rows = []
    for h in range(N_HEADS):
        kvh = h // Q_PER_KV
        sel_rows.append(sel2[kvh * ds:(kvh + 1) * ds])
    sel_rows.append(jnp.zeros((S_ROWS - n_real, ncols), F32))
    sel_rows = jnp.concatenate(sel_rows, axis=0)
    ocmp_ref[0] = o_cmp
    sel_ref[0] = sel_rows
    any_row = jnp.max(sel_rows, axis=0, keepdims=True)
    page_need = jnp.maximum(any_row[:, :npp], any_row[:, npp:])
    need_ref[0] = jnp.broadcast_to(page_need, (8, npp))


def _select_sample(q, kc, vc, *, past):
    b, ds, _ = q.shape
    npp = kc.shape[2]
    kern = functools.partial(_select_sample_kernel, past=past, ds=ds, npp=npp)
    per_b3 = lambda i: (i, 0, 0)
    per_b4 = lambda i: (i, 0, 0, 0)
    return pl.pallas_call(
        kern,
        grid=(b,),
        in_specs=[pl.BlockSpec((1, ds, ATTN_WIDTH), per_b3),
                  pl.BlockSpec((1, N_KV_HEADS, npp, LANES), per_b4),
                  pl.BlockSpec((1, N_KV_HEADS, npp, LANES), per_b4)],
        out_specs=(pl.BlockSpec((1, S_ROWS, LANES), per_b3), pl.BlockSpec((1, S_ROWS, 2 * npp), per_b3),
                   pl.BlockSpec((1, 8, npp), per_b3)),
        out_shape=(jax.ShapeDtypeStruct((b, S_ROWS, LANES), F32), jax.ShapeDtypeStruct((b, S_ROWS, 2 * npp), F32),
                   jax.ShapeDtypeStruct((b, 8, npp), F32)),
        compiler_params=pltpu.CompilerParams(dimension_semantics=("arbitrary",), vmem_limit_bytes=VMEM_LIMIT),
        name="select_sample",
    )(q, kc, vc)


def _attn_sample_kernel(pt_ref, order_ref, nn_ref, q_ref, gates_ref, ocmp_ref, sel_ref, kvs_new_ref, kvw_new_ref,
                        win_ref, cache_ref, a_ref, kbuf, zbuf, sem, *, n_pages, past, ds, npp):
    b = pl.program_id(0)
    nb = pl.num_programs(0)
    ncols = 2 * npp
    half_cols = KV_COLS // 2
    pages_per_chunk = S_CHUNK // PAGE

    def page_copy(row, j, slot):
        return pltpu.make_async_copy(
            cache_ref.at[pt_ref[row, order_ref[row, j]]], kbuf.at[slot, :, pl.ds(j * PAGE, PAGE)], sem.at[slot])

    def row_copies(row, slot, fn):
        def body(j, carry):
            fn(page_copy(row, j, slot))
            return carry
        lax.fori_loop(0, nn_ref[row], body, 0)

    @pl.when(b == 0)
    def _():
        kbuf[...] = jnp.zeros(kbuf.shape, F32)
        row_copies(0, 0, lambda cp: cp.start())

    slot = b % 2

    @pl.when(b + 1 < nb)
    def _():
        row_copies(b + 1, 1 - slot, lambda cp: cp.start())

    qrows, _, q_of_r, qp_r, sl_r, _ = _sample_rows(q_ref, ds, past)
    o_cmp = ocmp_ref[0]
    sel_rows = sel_ref[0]
    sel16 = sel_rows.astype(BF16)

    new_lane = lax.broadcasted_iota(jnp.int32, (1, LANES), 1)

    def new_scores(ref, extra_mask):
        k_new = ref[0, :half_cols, :].astype(BF16)
        s_new = jnp.dot(qrows, k_new, preferred_element_type=F32)
        dist = q_of_r - new_lane
        mask = (new_lane < ds) & (dist >= 0) & extra_mask
        return jnp.where(mask, s_new - sl_r * dist.astype(F32), NEG)

    def weighted_new(ref, e_new):
        return _dot_t(e_new.astype(BF16), ref[0, half_cols:, :].astype(BF16))

    row_copies(b, slot, lambda cp: cp.wait())
    n_need = nn_ref[b]
    n_chunks = (n_need + pages_per_chunk - 1) // pages_per_chunk

    def pass1(c, mrow):
        k0 = pl.multiple_of(c * S_CHUNK, S_CHUNK)
        kt = kbuf[slot, :half_cols, pl.ds(k0, S_CHUNK)].astype(BF16)
        st = jnp.dot(qrows, kt, preferred_element_type=F32)
        lane = lax.broadcasted_iota(jnp.int32, (1, S_CHUNK), 1)
        tpos = jnp.full((1, S_CHUNK), -PAGE * n_pages, jnp.int32)
        for jj in range(pages_per_chunk):
            j = c * pages_per_chunk + jj
            page = order_ref[b, jnp.minimum(j, n_pages - 1)]
            base = jnp.where(j < n_need, page * PAGE, -PAGE * n_pages) - jj * PAGE
            tpos = jnp.where(lane // PAGE == jj, base + lane, tpos)
        blk_c = _block_of_col(lax.broadcasted_iota(jnp.int32, (ncols, S_CHUNK), 0), npp)
        expand = (blk_c == jnp.broadcast_to(tpos, (ncols, S_CHUNK)) // CMP_BLOCK).astype(BF16)
        chosen = jnp.dot(sel16, expand, preferred_element_type=F32) > 0.5
        dist = qp_r - tpos
        zc = jnp.where(chosen & (dist >= 0), st - sl_r * dist.astype(F32), NEG)
        zbuf[c] = zc
        return jnp.maximum(mrow, jnp.max(zc, axis=-1, keepdims=True))

    mrow = lax.fori_loop(0, n_chunks, pass1, jnp.full((S_ROWS, 1), NEG, F32))
    c_last = ((past // CMP_BLOCK) % 2) * npp + (past // CMP_BLOCK) // 2
    z_new = new_scores(kvs_new_ref, sel_rows[:, c_last:c_last + 1] > 0.5)
    m_r = jnp.maximum(mrow, jnp.max(z_new, axis=-1, keepdims=True))

    def pass2(c, carry):
        acc, den_r = carry
        k0 = pl.multiple_of(c * S_CHUNK, S_CHUNK)
        zc = zbuf[c]
        ec = jnp.where(zc > 0.5 * NEG, jnp.exp(zc - m_r), 0.0)
        vt = kbuf[slot, half_cols:, pl.ds(k0, S_CHUNK)].astype(BF16)
        return acc + _dot_t(ec.astype(BF16), vt), den_r + jnp.sum(ec, axis=-1, keepdims=True)

    e_new = jnp.where(z_new > 0.5 * NEG, jnp.exp(z_new - m_r), 0.0)
    acc, den_r = lax.fori_loop(0, n_chunks, pass2,
                               (weighted_new(kvs_new_ref, e_new), jnp.sum(e_new, axis=-1, keepdims=True)))
    o_slc = acc / jnp.where(den_r > 0.0, den_r, 1.0)

    win_buf = win_ref.shape[2]
    kw = win_ref[0, :half_cols, :].astype(BF16)
    st = jnp.dot(qrows, kw, preferred_element_type=F32)
    kpos = past - win_buf + lax.broadcasted_iota(jnp.int32, (1, win_buf), 1)
    dist = qp_r - kpos
    mask = (dist >= 0) & (dist < WINDOW) & (kpos >= 0)
    z_w = jnp.where(mask, st - sl_r * dist.astype(F32), NEG)
    zw_new = new_scores(kvw_new_ref, True)
    m_w = jnp.maximum(jnp.max(z_w, axis=-1, keepdims=True), jnp.max(zw_new, axis=-1, keepdims=True))
    e_w = jnp.where(z_w > 0.5 * NEG, jnp.exp(z_w - m_w), 0.0)
    ew_new = jnp.where(zw_new > 0.5 * NEG, jnp.exp(zw_new - m_w), 0.0)
    acc_w = _dot_t(e_w.astype(BF16), win_ref[0, half_cols:, :].astype(BF16)) + weighted_new(kvw_new_ref, ew_new)
    den_w = jnp.sum(e_w, axis=-1, keepdims=True) + jnp.sum(ew_new, axis=-1, keepdims=True)
    o_win = acc_w / jnp.where(den_w > 0.0, den_w, 1.0)

    gates = gates_ref[0]
    for pair in range(N_HEADS // 2):
        kvh = (2 * pair) // Q_PER_KV
        e_sl = slice(2 * pair * ds, (2 * pair + 1) * ds)
        o_sl = slice((2 * pair + 1) * ds, (2 * pair + 2) * ds)
        shape = (ds, LANES)
        tile = (_gate_tile(gates, pair, 0, shape) * _pair_tile(o_cmp[e_sl], o_cmp[o_sl], 0)
                + _gate_tile(gates, pair, 1, shape) * _pair_tile(o_slc[e_sl], o_slc[o_sl], kvh)
                + _gate_tile(gates, pair, 2, shape) * _pair_tile(o_win[e_sl], o_win[o_sl], kvh))
        a_ref[0, :, pair * LANES:(pair + 1) * LANES] = tile.astype(BF16)


def _attn_sample(page_table, order, n_need, q, gates, o_cmp, sel_rows, kvs_new, kvw_new, state_win_t, cache_pages,
                 *, past):
    b, n_pages = page_table.shape
    ds = q.shape[1]
    npp = sel_rows.shape[2] // 2
    win_buf = state_win_t.shape[2]
    n_keys = n_pages * PAGE
    kern = functools.partial(_attn_sample_kernel, n_pages=n_pages, past=past, ds=ds, npp=npp)
    per_b = lambda i, pt, od, nn: (i, 0, 0)
    grid_spec = pltpu.PrefetchScalarGridSpec(
        num_scalar_prefetch=3,
        grid=(b,),
        in_specs=[
            pl.BlockSpec((1, ds, ATTN_WIDTH), per_b),
            pl.BlockSpec((1, ds, GATE_PAD), per_b),
            pl.BlockSpec((1, S_ROWS, LANES), per_b),
            pl.BlockSpec((1, S_ROWS, 2 * npp), per_b),
            pl.BlockSpec((1, KV_COLS, LANES), per_b),
            pl.BlockSpec((1, KV_COLS, LANES), per_b),
            pl.BlockSpec((1, KV_COLS, win_buf), per_b),
            pl.BlockSpec(memory_space=pl.ANY),
        ],
        out_specs=pl.BlockSpec((1, ds, ATTN_WIDTH), per_b),
        scratch_shapes=[
            pltpu.VMEM((2, KV_COLS, n_keys), F32),
            pltpu.VMEM((n_keys // S_CHUNK, S_ROWS, S_CHUNK), F32),
            pltpu.SemaphoreType.DMA((2,)),
        ],
    )
    return pl.pallas_call(
        kern,
        grid_spec=grid_spec,
        out_shape=jax.ShapeDtypeStruct((b, ds, ATTN_WIDTH), BF16),
        compiler_params=pltpu.CompilerParams(
            dimension_semantics=("arbitrary",), vmem_limit_bytes=VMEM_LIMIT),
        name="attn_sample",
    )(page_table, order, n_need, q, gates, o_cmp, sel_rows, kvs_new, kvw_new, state_win_t, cache_pages)


X_TILES = D_MODEL // LANES
PAY_SUB = X_TILES + 1
GROUP_LANE = 0
ROUTER_ROWS = 32


def _rows_to_tiles(ref, lo, val):
    for k in range(val.shape[1] // LANES):
        ref[:, lo + k, :] = val[:, k * LANES:(k + 1) * LANES]


def _tiles_to_rows(ref, lo, n):
    return jnp.concatenate([ref[:, lo + k, :] for k in range(n)], axis=1)


def _ffn_kernel(x_ref, a_ref, m_ref, wo_ref, g2_ref, wr_ref, *out_refs, packed):
    h = (x_ref[...]
         + jnp.dot(a_ref[...], wo_ref[:ATTN_WIDTH, :], preferred_element_type=F32)
         + jnp.dot(m_ref[...], wo_ref[ATTN_WIDTH:, :], preferred_element_type=F32))
    ms = jnp.mean(h * h, axis=-1, keepdims=True)
    n2f = (h * lax.rsqrt(ms + EPS)) * g2_ref[...]
    n2 = n2f.astype(BF16)
    tm = n2.shape[0]
    logits = _dot_t(wr_ref[...], n2)[:ROUTER_ROWS]
    row = lax.broadcasted_iota(jnp.int32, (ROUTER_ROWS, tm), 0)
    big = jnp.int32(LANES)

    def masked_softmax(mask):
        zz = jnp.where(mask, logits, NEG)
        mx = jnp.max(zz, axis=0, keepdims=True)
        ee = jnp.where(mask, jnp.exp(zz - mx), 0.0)
        return ee / jnp.sum(ee, axis=0, keepdims=True)

    def first_argmax(vals, mask):
        v = jnp.max(jnp.where(mask, vals, -1.0), axis=0, keepdims=True)
        idx = jnp.min(jnp.where(mask & (vals == v), row, big), axis=0, keepdims=True)
        return v, idx

    is_g = row < N_EXPERT_GROUPS
    pg = masked_softmax(is_g)
    g_val, g_idx = first_argmax(pg, is_g)
    e_row = row - ROUTER_LANE0
    in_grp = (e_row >= 0) & (e_row < N_EXPERTS) & (e_row // EXPERTS_PER_GROUP == g_idx)
    pe = masked_softmax(in_grp)
    v1, i1 = first_argmax(pe, in_grp)
    rest = in_grp & (row != i1)
    v2, i2 = first_argmax(pe, rest)
    scale = g_val / (v1 + v2)
    gate_t = jnp.where(row == i1, v1 * scale, jnp.where(row == i2, v2 * scale, 0.0))
    gate_t = jnp.where(row == GROUP_LANE, g_idx.astype(F32), gate_t)
    gate = jnp.concatenate([gate_t, jnp.zeros((LANES - ROUTER_ROWS, tm), F32)], axis=0).T
    if packed:
        h_ref, pay_ref, gate_ref = out_refs
        h_ref[...] = h
        gate_ref[...] = gate
        _rows_to_tiles(pay_ref, 0, n2f)
        pay_ref[:, X_TILES, :] = gate
    else:
        h_ref, n2_ref, gate_ref = out_refs
        h_ref[...] = h
        n2_ref[...] = n2
        gate_ref[...] = gate


def _ffn(x2d, a2d, m2d, w_out16, g2, w_router16, tm, packed):
    t = x2d.shape[0]
    row = lambda i: (i, 0)
    const = lambda i: (0, 0)
    if packed:
        out_specs = (pl.BlockSpec((tm, D_MODEL), row), pl.BlockSpec((tm, PAY_SUB, LANES), lambda i: (i, 0, 0)),
                     pl.BlockSpec((tm, LANES), row))
        out_shape = (jax.ShapeDtypeStruct((t, D_MODEL), F32), jax.ShapeDtypeStruct((t, PAY_SUB, LANES), F32),
                     jax.ShapeDtypeStruct((t, LANES), F32))
    else:
        out_specs = (pl.BlockSpec((tm, D_MODEL), row), pl.BlockSpec((tm, D_MODEL), row),
                     pl.BlockSpec((tm, LANES), row))
        out_shape = (jax.ShapeDtypeStruct((t, D_MODEL), F32), jax.ShapeDtypeStruct((t, D_MODEL), BF16),
                     jax.ShapeDtypeStruct((t, LANES), F32))
    return pl.pallas_call(
        functools.partial(_ffn_kernel, packed=packed),
        grid=(t // tm,),
        in_specs=[
            pl.BlockSpec((tm, D_MODEL), row),
            pl.BlockSpec((tm, ATTN_WIDTH), row),
            pl.BlockSpec((tm, POOL_WIDTH), row),
            pl.BlockSpec((D_MODEL, D_MODEL), const),
            pl.BlockSpec((1, D_MODEL), const),
            pl.BlockSpec((LANES, D_MODEL), const),
        ],
        out_specs=out_specs,
        out_shape=out_shape,
        compiler_params=pltpu.CompilerParams(
            dimension_semantics=("arbitrary",), vmem_limit_bytes=VMEM_LIMIT),
        name="ffn",
    )(x2d, a2d, m2d, w_out16, g2, w_router16)


MOE_ROWS = 512
ROW_CHUNK = 1024
PLAN_TILE = 1024
PLAN_COLS = 8


def _plan_kernel(gate_ref, tri_ref, out_ref, cnt_ref, carry):
    @pl.when(pl.program_id(0) == 0)
    def _():
        carry[...] = jnp.zeros(carry.shape, F32)

    gl = gate_ref[...]
    lane = lax.broadcasted_iota(jnp.int32, gl.shape, 1)
    gid = gl[:, GROUP_LANE:GROUP_LANE + 1]
    onehot = jnp.where((lane < N_EXPERT_GROUPS) & (lane.astype(F32) == gid), 1.0, 0.0)
    before = jnp.dot(tri_ref[...], onehot.astype(BF16), preferred_element_type=F32) + carry[0:1, :]
    rank = jnp.sum(onehot * before, axis=-1, keepdims=True)
    out_ref[...] = jnp.where(lane == 0, rank, jnp.where(lane == 1, gid, 0.0))[:, :PLAN_COLS]
    carry[0:1, :] = carry[0:1, :] + jnp.sum(onehot, axis=0, keepdims=True)
    cnt_ref[...] = carry[...]


def _plan(gate):
    t = gate.shape[0]
    tri = jnp.asarray(np.tril(np.ones((PLAN_TILE, PLAN_TILE), np.float32), -1), dtype=BF16)
    return pl.pallas_call(
        _plan_kernel,
        grid=(t // PLAN_TILE,),
        in_specs=[pl.BlockSpec((PLAN_TILE, LANES), lambda i: (i, 0)),
                  pl.BlockSpec((PLAN_TILE, PLAN_TILE), lambda i: (0, 0))],
        out_specs=(pl.BlockSpec((PLAN_TILE, PLAN_COLS), lambda i: (i, 0)), pl.BlockSpec((8, LANES), lambda i: (0, 0))),
        out_shape=(jax.ShapeDtypeStruct((t, PLAN_COLS), F32), jax.ShapeDtypeStruct((8, LANES), F32)),
        scratch_shapes=[pltpu.VMEM((8, LANES), F32)],
        compiler_params=pltpu.CompilerParams(
            dimension_semantics=("arbitrary",), vmem_limit_bytes=VMEM_LIMIT),
        name="moe_plan",
    )(gate, tri)


def _scatter_kernel(meta_ref, slot_ref, pay_ref, xs_ref, zrow, sem, zsem):
    c = pl.program_id(0)

    @pl.when(c == 0)
    def _():
        zrow[...] = jnp.zeros(zrow.shape, F32)
        for g in range(N_EXPERT_GROUPS + 1):
            lo = meta_ref[g]
            hi = meta_ref[N_EXPERT_GROUPS + 1 + g]

            def zero_start(r, carry):
                pltpu.make_async_copy(zrow.at[pl.ds(0, 1)], xs_ref.at[pl.ds(r, 1)], zsem.at[0]).start()
                return carry

            def zero_wait(r, carry):
                pltpu.make_async_copy(zrow.at[pl.ds(0, 1)], xs_ref.at[pl.ds(0, 1)], zsem.at[0]).wait()
                return carry

            lax.fori_loop(lo, hi, zero_start, 0)
            lax.fori_loop(lo, hi, zero_wait, 0)

    def issue(u, carry):
        pltpu.make_async_copy(pay_ref.at[pl.ds(u, 1)], xs_ref.at[pl.ds(slot_ref[0, 0, u], 1)], sem.at[0]).start()
        return carry

    lax.fori_loop(0, ROW_CHUNK, issue, 0, unroll=8)
    pltpu.make_async_copy(pay_ref, xs_ref.at[pl.ds(0, ROW_CHUNK)], sem.at[0]).wait()


def _scatter(meta, slot3, pay, n_slots):
    t = pay.shape[0]
    return pl.pallas_call(
        _scatter_kernel,
        grid_spec=pltpu.PrefetchScalarGridSpec(
            num_scalar_prefetch=1,
            grid=(t // ROW_CHUNK,),
            in_specs=[pl.BlockSpec((1, 1, ROW_CHUNK), lambda c, meta: (c, 0, 0), memory_space=pltpu.SMEM),
                      pl.BlockSpec((ROW_CHUNK, PAY_SUB, LANES), lambda c, meta: (c, 0, 0))],
            out_specs=pl.BlockSpec(memory_space=pl.ANY),
            scratch_shapes=[pltpu.VMEM((1, PAY_SUB, LANES), F32), pltpu.SemaphoreType.DMA((1,)),
                            pltpu.SemaphoreType.DMA((1,))],
        ),
        out_shape=jax.ShapeDtypeStruct((n_slots, PAY_SUB, LANES), F32),
        compiler_params=pltpu.CompilerParams(dimension_semantics=("arbitrary",), vmem_limit_bytes=VMEM_LIMIT),
        name="moe_scatter",
    )(meta, slot3, pay)


def _gather_kernel(slot_ref, nslot_ref, h_ref, ys_ref, y_ref, buf, sem):
    c = pl.program_id(0)
    nc = pl.num_programs(0)

    def fetch(ref, slot):
        def issue(u, carry):
            pltpu.make_async_copy(ys_ref.at[pl.ds(ref[0, 0, u], 1)], buf.at[slot, pl.ds(u, 1)], sem.at[slot]).start()
            return carry
        lax.fori_loop(0, ROW_CHUNK, issue, 0, unroll=8)

    @pl.when(c == 0)
    def _():
        fetch(slot_ref, 0)

    @pl.when(c + 1 < nc)
    def _():
        fetch(nslot_ref, (c + 1) % 2)

    cur = c % 2
    pltpu.make_async_copy(ys_ref.at[pl.ds(0, ROW_CHUNK)], buf.at[cur], sem.at[cur]).wait()
    y_ref[...] = h_ref[...] + jnp.concatenate([buf[cur, :, k, :] for k in range(X_TILES)], axis=1)


def _gather(slot3, h, ys):
    t = h.shape[0]
    nc = t // ROW_CHUNK
    return pl.pallas_call(
        _gather_kernel,
        grid=(nc,),
        in_specs=[pl.BlockSpec((1, 1, ROW_CHUNK), lambda c: (c, 0, 0), memory_space=pltpu.SMEM),
                  pl.BlockSpec((1, 1, ROW_CHUNK), lambda c: (jnp.minimum(c + 1, nc - 1), 0, 0),
                               memory_space=pltpu.SMEM),
                  pl.BlockSpec((ROW_CHUNK, D_MODEL), lambda c: (c, 0)),
                  pl.BlockSpec(memory_space=pl.ANY)],
        out_specs=pl.BlockSpec((ROW_CHUNK, D_MODEL), lambda c: (c, 0)),
        out_shape=jax.ShapeDtypeStruct((t, D_MODEL), F32),
        scratch_shapes=[pltpu.VMEM((2, ROW_CHUNK, X_TILES, LANES), F32), pltpu.SemaphoreType.DMA((2,))],
        compiler_params=pltpu.CompilerParams(dimension_semantics=("arbitrary",), vmem_limit_bytes=VMEM_LIMIT),
        name="moe_gather",
    )(slot3, slot3, h, ys)


def _group_moe_kernel(tg_ref, nu_ref, xs_ref, wg_ref, wu_ref, wd_ref, ys_ref):
    j = pl.program_id(0)

    @pl.when(j < nu_ref[0])
    def _():
        g = tg_ref[j]
        x = _tiles_to_rows(xs_ref, 0, X_TILES).astype(BF16)
        gate = xs_ref[:, X_TILES, :]
        lane = lax.broadcasted_iota(jnp.int32, gate.shape, 1)
        y = jnp.zeros((MOE_ROWS, D_MODEL), F32)
        for e in range(EXPERTS_PER_GROUP):
            gu = jnp.dot(x, wg_ref[e], preferred_element_type=F32)
            up = jnp.dot(x, wu_ref[e], preferred_element_type=F32)
            he = (gu * (1.0 / (1.0 + jnp.exp(-gu)))) * up
            out = jnp.dot(he.astype(BF16), wd_ref[e], preferred_element_type=F32)
            col = ROUTER_LANE0 + g * EXPERTS_PER_GROUP + e
            y = y + jnp.sum(jnp.where(lane == col, gate, 0.0), axis=-1, keepdims=True) * out
        _rows_to_tiles(ys_ref, 0, y)

    @pl.when(j >= nu_ref[0])
    def _():
        ys_ref[...] = jnp.zeros(ys_ref.shape, F32)


def _group_moe(tile_group, n_used, xs, wg16, wu16, wd16):
    n_slots = xs.shape[0]
    wspec = lambda shape: pl.BlockSpec((EXPERTS_PER_GROUP,) + shape, lambda j, tg, nu: (tg[j], 0, 0))
    return pl.pallas_call(
        _group_moe_kernel,
        grid_spec=pltpu.PrefetchScalarGridSpec(
            num_scalar_prefetch=2,
            grid=(n_slots // MOE_ROWS,),
            in_specs=[pl.BlockSpec((MOE_ROWS, PAY_SUB, LANES), lambda j, tg, nu: (j, 0, 0)),
                      wspec((D_MODEL, D_EXPERT)), wspec((D_MODEL, D_EXPERT)), wspec((D_EXPERT, D_MODEL))],
            out_specs=pl.BlockSpec((MOE_ROWS, X_TILES, LANES), lambda j, tg, nu: (j, 0, 0)),
        ),
        out_shape=jax.ShapeDtypeStruct((n_slots, X_TILES, LANES), F32),
        compiler_params=pltpu.CompilerParams(dimension_semantics=("arbitrary",), vmem_limit_bytes=VMEM_LIMIT),
        name="moe_group",
    )(tile_group, n_used, xs, wg16, wu16, wd16)


def _routed_moe(h, pay, gate, wg16, wu16, wd16):
    t = pay.shape[0]
    n_slots = t + N_EXPERT_GROUPS * MOE_ROWS
    n_tiles = n_slots // MOE_ROWS
    plan, counts = _plan(gate)
    rank = plan[:, 0].astype(jnp.int32)
    gid = plan[:, 1].astype(jnp.int32)
    cnt = counts[0, :N_EXPERT_GROUPS].astype(jnp.int32)
    padded = -(-cnt // MOE_ROWS) * MOE_ROWS
    ends = jnp.cumsum(padded)
    off = ends - padded
    slot = (off[gid] + rank).reshape(t // ROW_CHUNK, 1, ROW_CHUNK)
    tile_start = jnp.arange(n_tiles, dtype=jnp.int32) * MOE_ROWS
    tile_group = jnp.minimum(jnp.sum(tile_start[:, None] >= ends[None, :], axis=1), N_EXPERT_GROUPS - 1)
    n_used = (ends[-1] // MOE_ROWS).reshape(1)
    meta = jnp.concatenate([off + cnt, ends[-1:], ends, jnp.full((1,), n_slots, jnp.int32)]).astype(jnp.int32)
    xs = _scatter(meta, slot, pay, n_slots)
    ys = _group_moe(tile_group.astype(jnp.int32), n_used.astype(jnp.int32), xs, wg16, wu16, wd16)
    return _gather(slot, h, ys)


def _moe_kernel(n2_ref, gate_ref, h_ref, wg_ref, wu_ref, wd_ref, y_ref):
    e = pl.program_id(1)

    @pl.when(e == 0)
    def _():
        y_ref[...] = h_ref[...]

    n2 = n2_ref[...]
    gu = jnp.dot(n2, wg_ref[0], preferred_element_type=F32)
    up = jnp.dot(n2, wu_ref[0], preferred_element_type=F32)
    he = (gu * (1.0 / (1.0 + jnp.exp(-gu)))) * up
    out = jnp.dot(he.astype(BF16), wd_ref[0], preferred_element_type=F32)
    lane = lax.broadcasted_iota(jnp.int32, gate_ref.shape, 1)
    gcol = jnp.sum(jnp.where(lane == e + ROUTER_LANE0, gate_ref[...], 0.0), axis=-1, keepdims=True)
    y_ref[...] += gcol * out


def _moe(n2, gate, h, wg16, wu16, wd16, tm):
    t = n2.shape[0]
    row = lambda i, e: (i, 0)
    return pl.pallas_call(
        _moe_kernel,
        grid=(t // tm, N_EXPERTS),
        in_specs=[
            pl.BlockSpec((tm, D_MODEL), row),
            pl.BlockSpec((tm, LANES), row),
            pl.BlockSpec((tm, D_MODEL), row),
            pl.BlockSpec((1, D_MODEL, D_EXPERT), lambda i, e: (e, 0, 0)),
            pl.BlockSpec((1, D_MODEL, D_EXPERT), lambda i, e: (e, 0, 0)),
            pl.BlockSpec((1, D_EXPERT, D_MODEL), lambda i, e: (e, 0, 0)),
        ],
        out_specs=pl.BlockSpec((tm, D_MODEL), row),
        out_shape=jax.ShapeDtypeStruct((t, D_MODEL), F32),
        compiler_params=pltpu.CompilerParams(
            dimension_semantics=("arbitrary", "arbitrary"), vmem_limit_bytes=VMEM_LIMIT),
        name="moe",
    )(n2, gate, h, wg16, wu16, wd16)


PROJ_TILE = 512
MOE_TILE = 1024
PROMPT_NPP = LANES // 2
SAMPLE_NPP = LANES


def _round_up(n, mult):
    return -(-n // mult) * mult


def _feature_major(x5):
    b, t = x5.shape[:2]
    return jnp.transpose(x5, (0, 2, 3, 4, 1)).reshape(b, KV_COLS, t)


def _token_major(xt):
    b, _, t = xt.shape
    return jnp.transpose(xt.reshape(b, 2, N_KV_HEADS, HEAD_DIM, t), (0, 4, 1, 2, 3))


def _pad_pages(x, npp_from, npp_to):
    b = x.shape[0]
    x = x.reshape(b, N_KV_HEADS, npp_from, LANES)
    return jnp.concatenate([x, jnp.zeros((b, N_KV_HEADS, npp_to - npp_from, LANES), x.dtype)], axis=2)


def _expand_const(n_chunks, npp):
    col = np.arange(2 * npp)
    blk = 2 * (col % npp) + col // npp
    tok_blk = np.arange(n_chunks * K_CHUNK) // CMP_BLOCK
    e = (blk[None, :, None] == tok_blk.reshape(n_chunks, 1, K_CHUNK)).astype(np.float32)
    return jnp.asarray(e, dtype=BF16)


def _gate_place_const():
    gp = np.zeros((GATE_PAD, 3 * ATTN_WIDTH), np.float32)
    for h in range(N_HEADS):
        for c in range(3):
            gp[3 * h + c, c * ATTN_WIDTH + h * HEAD_DIM:c * ATTN_WIDTH + (h + 1) * HEAD_DIM] = 1.0
    return jnp.asarray(gp, dtype=BF16)


def kernel(x_prompt, x_sample, cache_cmp_kv, cache_slc_kv, state_win_kv, state_pool, page_table, norm1_g, w_in, q_norm_g, k_norm_cmp_g, k_norm_slc_g, k_norm_win_g, cmp_pos_emb, w_cmp_k, w_cmp_v, pool_w, pool_scale, w_out, norm2_g, w_router_group, w_router_expert, w_gate, w_up, w_down):
    b, s, _ = x_prompt.shape
    db, ds, _ = x_sample.shape
    n_pool, page_rows = cache_cmp_kv.shape[:2]
    assert page_rows == PAGE and s % PAGE == 0
    past = page_table.shape[1] * page_rows

    kv0 = ATTN_WIDTH
    kv1 = kv0 + KV_ROWS
    w_row = jnp.concatenate(
        [w_in[:, :kv0], w_in[:, kv1:kv1 + GATE_COLS], jnp.zeros((D_MODEL, GATE_PAD - GATE_COLS), w_in.dtype),
         w_in[:, kv1 + GATE_COLS:]], axis=1).astype(BF16)
    w_kv = w_in[:, kv0:kv1].T.astype(BF16)
    g1 = norm1_g[None, :]
    g2 = norm2_g[None, :]
    two = lambda g: jnp.tile(g, 2)[None, :]
    kg = jnp.stack([k_norm_slc_g, k_norm_win_g])[:, :, None]
    pos_t = jnp.tile(cmp_pos_emb.T, (1, 2))
    zb = jnp.zeros((HEAD_DIM, CMP_BLOCK, HEAD_DIM), F32)

    def blockdiag(w):
        wt = jnp.transpose(w, (1, 0, 2))
        return jnp.concatenate([jnp.concatenate([wt, zb], axis=2), jnp.concatenate([zb, wt], axis=2)], axis=1)

    bdk, bdv = blockdiag(w_cmp_k), blockdiag(w_cmp_v)
    zq = jnp.zeros_like(bdk)
    bd = jnp.concatenate([jnp.concatenate([bdk, zq], axis=2), jnp.concatenate([zq, bdv], axis=2)],
                         axis=1).astype(BF16)
    w_router = jnp.concatenate(
        [w_router_group, w_router_expert,
         jnp.zeros((D_MODEL, LANES - N_EXPERT_GROUPS - N_EXPERTS), w_router_group.dtype)], axis=1).T.astype(BF16)
    w_out16 = w_out.astype(BF16)
    pool_w16 = pool_w.astype(BF16)
    wg16, wu16, wd16 = w_gate.astype(BF16), w_up.astype(BF16), w_down.astype(BF16)
    ps = pool_scale[None, :]

    def ffn_moe(x2d, a2d, m2d, tm_ffn, tm_moe, routed):
        if routed:
            return _routed_moe(*_ffn(x2d, a2d, m2d, w_out16, g2, w_router, tm_ffn, True), wg16, wu16, wd16)
        h, n2, gate = _ffn(x2d, a2d, m2d, w_out16, g2, w_router, tm_ffn, False)
        return _moe(n2, gate, h, wg16, wu16, wd16, tm_moe)

    q, gates, u, kvc_t, kvs_t, kvw_t, ks16, vsx, kw16, vwx = _proj(
        x_prompt, g1, w_row, w_kv, two(q_norm_g), kg, PROJ_TILE)
    m_p = _pool(u, pool_w16, ps, s)
    pages_p = s // PAGE
    pt_p = jnp.zeros((b, pages_p), jnp.int32)
    dummy_tail = jnp.zeros((b, KV_COLS, LANES), F32)
    kc_p, vc_p = _compress(pt_p, kvc_t, dummy_tail, pos_t, bd, two(k_norm_cmp_g),
                           npp=pages_p, has_tail=False, paged=False)
    kc_p = _pad_pages(kc_p, pages_p, PROMPT_NPP)
    vc_p = _pad_pages(vc_p, pages_p, PROMPT_NPP)
    a_p = _attn_prompt(q, gates, kc_p, vc_p, ks16, vsx, kw16, vwx, _expand_const(s // K_CHUNK, PROMPT_NPP),
                       _gate_place_const(), pages_p)
    y_p = ffn_moe(x_prompt.reshape(b * s, D_MODEL), a_p.reshape(b * s, ATTN_WIDTH),
                  m_p.reshape(b * s, POOL_WIDTH), PROJ_TILE, MOE_TILE, True)

    ts = db * ds
    q_s, gates_s, u_s, kvc_st, kvs_st, kvw_st, _, _, _, _ = _proj(
        x_sample.reshape(1, ts, D_MODEL), g1, w_row, w_kv, two(q_norm_g), kg, ts)
    u_s = u_s.reshape(db, ds, POOL_WIDTH)
    pool_ext = jnp.concatenate([state_pool, u_s], axis=1)
    n_ext = pool_ext.shape[1]
    lead = _round_up(n_ext, 8) - n_ext
    pool_in = jnp.concatenate([jnp.zeros((db, lead, POOL_WIDTH), F32), pool_ext], axis=1)
    n_in = lead + n_ext
    m_s = _pool(pool_in.reshape(1, db * n_in, POOL_WIDTH), pool_w16, ps, n_in).reshape(db, n_in, POOL_WIDTH)[:, n_in - ds:]

    def new_rows_t(xt):
        x = jnp.transpose(xt.reshape(KV_COLS, db, ds), (1, 0, 2))
        return jnp.concatenate([x, jnp.zeros((db, KV_COLS, LANES - ds), F32)], axis=2)

    kvc_new, kvs_new, kvw_new = new_rows_t(kvc_st), new_rows_t(kvs_st), new_rows_t(kvw_st)
    n_pages = page_table.shape[1]
    npp_c = _round_up(n_pages + 1, 8)
    cmp_pages = _feature_major(cache_cmp_kv)
    slc_pages = _feature_major(cache_slc_kv)
    kc_s, vc_s = _compress(page_table, cmp_pages, kvc_new, pos_t, bd, two(k_norm_cmp_g),
                           npp=npp_c, has_tail=True, paged=True)
    kc_s = _pad_pages(kc_s, npp_c, SAMPLE_NPP)
    vc_s = _pad_pages(vc_s, npp_c, SAMPLE_NPP)
    win_t = _feature_major(state_win_kv)
    q_s3 = q_s.reshape(db, ds, ATTN_WIDTH)
    o_cmp_s, sel_s, need_s = _select_sample(q_s3, kc_s, vc_s, past=past)
    need = need_s[:, 0, :n_pages] > 0.5
    order = jnp.argsort(jnp.logical_not(need), axis=1, stable=True).astype(jnp.int32)
    n_need = jnp.sum(need, axis=1).astype(jnp.int32)
    a_s = _attn_sample(page_table, order, n_need, q_s3, gates_s.reshape(db, ds, GATE_PAD), o_cmp_s, sel_s,
                       kvs_new, kvw_new, win_t, slc_pages, past=past)
    y_s = ffn_moe(x_sample.reshape(ts, D_MODEL), a_s.reshape(ts, ATTN_WIDTH), m_s.reshape(ts, POOL_WIDTH), ts, ts, False)

    win_keep = min(WINDOW, s)
    sample5 = lambda xt: jnp.transpose(xt.reshape(2, N_KV_HEADS, HEAD_DIM, db, ds), (3, 4, 0, 1, 2))
    win_ctx_t = jnp.concatenate([win_t, kvw_new[:, :, :ds]], axis=2)
    return (y_p.reshape(b, s, D_MODEL), y_s.reshape(db, ds, D_MODEL),
            _token_major(kvc_t), sample5(kvc_st),
            _token_major(kvs_t), sample5(kvs_st),
            _token_major(kvw_t[:, :, s - win_keep:]), _token_major(win_ctx_t[:, :, ds:]),
            u[:, s - POOL_BUF:], pool_ext[:, ds:])
```

```python
import functools

import jax
import jax.numpy as jnp
import numpy as np
from jax import lax
from jax.experimental import pallas as pl
from jax.experimental.pallas import tpu as pltpu

F32 = jnp.float32
BF16 = jnp.bfloat16

D_MODEL = 1024
N_HEADS = 8
HEAD_DIM = 64
N_KV_HEADS = 2
Q_PER_KV = N_HEADS // N_KV_HEADS
ATTN_WIDTH = N_HEADS * HEAD_DIM
KV_COLS = 2 * N_KV_HEADS * HEAD_DIM
GATE_COLS = 3 * N_HEADS
POOL_WIDTH = D_MODEL - ATTN_WIDTH
POOL_WINDOWS = (2, 4, 8, 16)
POOL_GROUP_WIDTH = POOL_WIDTH // len(POOL_WINDOWS)
POOL_BUF = max(POOL_WINDOWS) - 1
CMP_BLOCK = 64
TOP_K_BLOCKS = 16
WINDOW = 512
FORCE_SCORE = 1.0e4
N_EXPERT_GROUPS = 4
EXPERTS_PER_GROUP = 4
N_EXPERTS = N_EXPERT_GROUPS * EXPERTS_PER_GROUP
D_EXPERT = 512
EPS = 1e-6
NEG = -1e30
SCALE = HEAD_DIM ** -0.5

LANES = 128
PAGE = 2 * CMP_BLOCK
GATE_PAD = LANES
ROW_COLS = ATTN_WIDTH + GATE_PAD + POOL_WIDTH
OFF_GATE = ATTN_WIDTH
OFF_U = OFF_GATE + GATE_PAD
KV_ROWS = 3 * KV_COLS
ROUTER_LANE0 = N_EXPERT_GROUPS
Q_TILE = 256
K_CHUNK = 256
VMEM_LIMIT = 56 * 1024 * 1024


def _slope(h):
    return float(2.0 ** (-8.0 * (h + 1) / N_HEADS))


def _half_group_norm(v, g):
    lane = lax.broadcasted_iota(jnp.int32, v.shape, 1)
    lo = lane < HEAD_DIM
    v2 = v * v
    s_lo = jnp.sum(jnp.where(lo, v2, 0.0), axis=-1, keepdims=True)
    s_hi = jnp.sum(jnp.where(lo, 0.0, v2), axis=-1, keepdims=True)
    r_lo = lax.rsqrt(s_lo * (1.0 / HEAD_DIM) + EPS)
    r_hi = lax.rsqrt(s_hi * (1.0 / HEAD_DIM) + EPS)
    return (v * jnp.where(lo, r_lo, r_hi)) * g


def _dot_t(a, b):
    return lax.dot_general(a, b, (((1,), (1,)), ((), ())), preferred_element_type=F32)


def _proj_kernel(x_ref, g1_ref, wr_ref, wkv_ref, qg_ref, kg_ref,
                 q_ref, gates_ref, u_ref, kvc_ref, kvs_ref, kvw_ref, ks16_ref, vsx_ref, kw16_ref, vwx_ref):
    x = x_ref[0]
    tm = x.shape[0]
    ms = jnp.mean(x * x, axis=-1, keepdims=True)
    n = ((x * lax.rsqrt(ms + EPS)) * g1_ref[...]).astype(BF16)
    p = jnp.dot(n, wr_ref[...], preferred_element_type=F32)
    pt = _dot_t(wkv_ref[...], n)
    for t in range(ATTN_WIDTH // LANES):
        sl = slice(t * LANES, (t + 1) * LANES)
        q_ref[0, :, sl] = _half_group_norm(p[:, sl], qg_ref[...]).astype(BF16)
    gates_ref[0] = 1.0 / (1.0 + jnp.exp(-p[:, OFF_GATE:OFF_GATE + GATE_PAD]))
    u_ref[0] = p[:, OFF_U:OFF_U + POOL_WIDTH]
    kvc_ref[0] = pt[:KV_COLS]
    half = KV_COLS // 2
    lane = lax.broadcasted_iota(jnp.int32, (tm, LANES), 1)
    for bi, (out32, k16, vx16) in enumerate(((kvs_ref, ks16_ref, vsx_ref), (kvw_ref, kw16_ref, vwx_ref))):
        off = (bi + 1) * KV_COLS
        heads = []
        for hh in range(N_KV_HEADS):
            kh = pt[off + hh * HEAD_DIM:off + (hh + 1) * HEAD_DIM]
            msk = jnp.mean(kh * kh, axis=0, keepdims=True)
            heads.append((kh * lax.rsqrt(msk + EPS)) * kg_ref[bi])
        kn = jnp.concatenate(heads, axis=0)
        v = pt[off + half:off + KV_COLS]
        out32[0, :half, :] = kn
        out32[0, half:, :] = v
        vt = v.T
        vx = (jnp.where(lane < HEAD_DIM, vt, 1.0), jnp.where(lane < HEAD_DIM, pltpu.roll(vt, HEAD_DIM, axis=1), 1.0))
        for c in range(tm // K_CHUNK):
            cs = slice(c * K_CHUNK, (c + 1) * K_CHUNK)
            k16[0, c] = kn[:, cs].astype(BF16)
            for hh in range(N_KV_HEADS):
                vx16[0, c, hh] = vx[hh][cs].astype(BF16)


def _proj(x3d, g1, w_row, w_kv, qg, kg, tm):
    b, s, _ = x3d.shape
    tok = lambda i, j: (i, j, 0)
    feat = lambda i, j: (i, 0, j)
    const2 = lambda i, j: (0, 0)
    nck = tm // K_CHUNK
    out_shape = (
        jax.ShapeDtypeStruct((b, s, ATTN_WIDTH), BF16),
        jax.ShapeDtypeStruct((b, s, GATE_PAD), F32),
        jax.ShapeDtypeStruct((b, s, POOL_WIDTH), F32),
        jax.ShapeDtypeStruct((b, KV_COLS, s), F32),
        jax.ShapeDtypeStruct((b, KV_COLS, s), F32),
        jax.ShapeDtypeStruct((b, KV_COLS, s), F32),
        jax.ShapeDtypeStruct((b, s // K_CHUNK, LANES, K_CHUNK), BF16),
        jax.ShapeDtypeStruct((b, s // K_CHUNK, N_KV_HEADS, K_CHUNK, LANES), BF16),
        jax.ShapeDtypeStruct((b, s // K_CHUNK, LANES, K_CHUNK), BF16),
        jax.ShapeDtypeStruct((b, s // K_CHUNK, N_KV_HEADS, K_CHUNK, LANES), BF16),
    )
    k_spec = pl.BlockSpec((1, nck, LANES, K_CHUNK), lambda i, j: (i, j, 0, 0))
    v_spec = pl.BlockSpec((1, nck, N_KV_HEADS, K_CHUNK, LANES), lambda i, j: (i, j, 0, 0, 0))
    out_specs = (
        pl.BlockSpec((1, tm, ATTN_WIDTH), tok),
        pl.BlockSpec((1, tm, GATE_PAD), tok),
        pl.BlockSpec((1, tm, POOL_WIDTH), tok),
        pl.BlockSpec((1, KV_COLS, tm), feat),
        pl.BlockSpec((1, KV_COLS, tm), feat),
        pl.BlockSpec((1, KV_COLS, tm), feat),
        k_spec, v_spec, k_spec, v_spec,
    )
    return pl.pallas_call(
        _proj_kernel,
        grid=(b, s // tm),
        in_specs=[
            pl.BlockSpec((1, tm, D_MODEL), tok),
            pl.BlockSpec((1, D_MODEL), const2),
            pl.BlockSpec((D_MODEL, ROW_COLS), const2),
            pl.BlockSpec((KV_ROWS, D_MODEL), const2),
            pl.BlockSpec((1, LANES), const2),
            pl.BlockSpec((2, HEAD_DIM, 1), lambda i, j: (0, 0, 0)),
        ],
        out_specs=out_specs,
        out_shape=out_shape,
        compiler_params=pltpu.CompilerParams(
            dimension_semantics=("arbitrary", "arbitrary"), vmem_limit_bytes=VMEM_LIMIT),
        name="proj",
    )(x3d, g1, w_row, w_kv, qg, kg)


def _pool_kernel(u_ref, pw_ref, ps_ref, m_ref, *, seg):
    u = u_ref[0]
    n = u.shape[0]

    def shift(v, k):
        rolled = pltpu.roll(v, k, axis=0)
        r = lax.broadcasted_iota(jnp.int32, v.shape, 0) % seg
        return jnp.where(r >= k, rolled, 0.0)

    sums = []
    s = u
    k = 1
    for gi in range(len(POOL_WINDOWS)):
        s = s + shift(s, k)
        k *= 2
        sums.append(s[:, :POOL_GROUP_WIDTH])
        s = s[:, POOL_GROUP_WIDTH:]
    row = lax.broadcasted_iota(jnp.int32, (n, 1), 0) % seg
    for gi, w in enumerate(POOL_WINDOWS):
        sl = slice(gi * POOL_GROUP_WIDTH, (gi + 1) * POOL_GROUP_WIDTH)
        cnt = jnp.minimum(row + 1, w).astype(F32)
        d = sums[gi] / cnt - u[:, sl]
        y = jnp.dot(d.astype(BF16), pw_ref[gi], preferred_element_type=F32)
        m_ref[0, :, sl] = (y * ps_ref[:, sl]).astype(BF16)


def _pool(u3d, pool_w16, pool_scale, seg):
    b, n, _ = u3d.shape
    assert n % seg == 0
    return pl.pallas_call(
        functools.partial(_pool_kernel, seg=seg),
        grid=(b,),
        in_specs=[
            pl.BlockSpec((1, n, POOL_WIDTH), lambda i: (i, 0, 0)),
            pl.BlockSpec((len(POOL_WINDOWS), POOL_GROUP_WIDTH, POOL_GROUP_WIDTH), lambda i: (0, 0, 0)),
            pl.BlockSpec((1, POOL_WIDTH), lambda i: (0, 0)),
        ],
        out_specs=pl.BlockSpec((1, n, POOL_WIDTH), lambda i: (i, 0, 0)),
        out_shape=jax.ShapeDtypeStruct((b, n, POOL_WIDTH), BF16),
        compiler_params=pltpu.CompilerParams(
            dimension_semantics=("arbitrary",), vmem_limit_bytes=VMEM_LIMIT),
        name="pool",
    )(u3d, pool_w16, pool_scale)


def _compress_kernel(pt_ref, src_ref, tail_ref, pos_ref, bd_ref, g_ref, kc_ref, vc_ref, buf, sem,
                     *, n_pages, npp, has_tail, paged):
    b = pl.program_id(0)
    nb = pl.num_programs(0)
    n_slabs = 2 * N_KV_HEADS

    def slab_copy(row, p, c, slot):
        kv, kvh = divmod(c, N_KV_HEADS)
        if paged:
            src = src_ref.at[pt_ref[row, p], pl.ds(c * HEAD_DIM, HEAD_DIM), :]
        else:
            src = src_ref.at[row, pl.ds(c * HEAD_DIM, HEAD_DIM), pl.ds(p * PAGE, PAGE)]
        return pltpu.make_async_copy(src, buf.at[slot, kv, :, kvh * npp + p, :], sem.at[slot])

    def tail_copy(c, slot):
        kv, kvh = divmod(c, N_KV_HEADS)
        return pltpu.make_async_copy(tail_ref.at[0, pl.ds(c * HEAD_DIM, HEAD_DIM), :],
                                     buf.at[slot, kv, :, kvh * npp + n_pages, :], sem.at[slot])

    def row_copies(row, slot, fn):
        def body(p, carry):
            for c in range(n_slabs):
                fn(slab_copy(row, p, c, slot))
            return carry
        lax.fori_loop(0, n_pages, body, 0)

    n_real = n_pages + (1 if has_tail else 0)

    @pl.when(b == 0)
    def _():
        if npp > n_real:
            for kvh in range(N_KV_HEADS):
                buf[:, :, :, kvh * npp + n_real:(kvh + 1) * npp, :] = jnp.zeros(
                    (2, 2, HEAD_DIM, npp - n_real, LANES), F32)
        row_copies(0, 0, lambda cp: cp.start())

    slot = b % 2

    if has_tail:
        for c in range(n_slabs):
            tail_copy(c, slot).start()
        for c in range(n_slabs):
            tail_copy(c, slot).wait()
    row_copies(b, slot, lambda cp: cp.wait())

    rows = N_KV_HEADS * npp
    d_per_it = 8
    n_it = HEAD_DIM // d_per_it
    pages_per_it = n_pages // n_it
    nxt = jnp.minimum(b + 1, nb - 1)

    def body(it, acc):
        for pp in range(pages_per_it):
            for c in range(n_slabs):
                slab_copy(nxt, it * pages_per_it + pp, c, 1 - slot).start()
        for dd in range(d_per_it):
            d = it * d_per_it + dd
            pos = pos_ref[pl.ds(d, 1), :]
            x = jnp.concatenate([buf[slot, 0, d] + pos, buf[slot, 1, d] + pos], axis=1).astype(BF16)
            acc = acc + jnp.dot(x, bd_ref[d], preferred_element_type=F32)
        return acc

    acc = lax.fori_loop(0, n_it, body, jnp.zeros((rows, 2 * LANES), F32))
    kc_ref[0] = _half_group_norm(acc[:, :LANES], g_ref[...])
    vc_ref[0] = acc[:, LANES:]

    @pl.when(b == nb - 1)
    def _():
        row_copies(nxt, 1 - slot, lambda cp: cp.wait())


def _compress(page_table, src, tail, pos_t, bd, g2, *, npp, has_tail, paged):
    b, n_pages = page_table.shape
    kern = functools.partial(_compress_kernel, n_pages=n_pages, npp=npp, has_tail=has_tail, paged=paged)
    rows = N_KV_HEADS * npp
    grid_spec = pltpu.PrefetchScalarGridSpec(
        num_scalar_prefetch=1,
        grid=(b,),
        in_specs=[
            pl.BlockSpec(memory_space=pl.ANY),
            pl.BlockSpec((1, KV_COLS, LANES), lambda i, pt: (i, 0, 0)),
            pl.BlockSpec((HEAD_DIM, LANES), lambda i, pt: (0, 0)),
            pl.BlockSpec((HEAD_DIM, 2 * LANES, 2 * LANES), lambda i, pt: (0, 0, 0)),
            pl.BlockSpec((1, LANES), lambda i, pt: (0, 0)),
        ],
        out_specs=(pl.BlockSpec((1, rows, LANES), lambda i, pt: (i, 0, 0)),
                   pl.BlockSpec((1, rows, LANES), lambda i, pt: (i, 0, 0))),
        scratch_shapes=[
            pltpu.VMEM((2, 2, HEAD_DIM, rows, LANES), F32),
            pltpu.SemaphoreType.DMA((2,)),
        ],
    )
    return pl.pallas_call(
        kern,
        grid_spec=grid_spec,
        out_shape=(jax.ShapeDtypeStruct((b, rows, LANES), F32), jax.ShapeDtypeStruct((b, rows, LANES), F32)),
        compiler_params=pltpu.CompilerParams(
            dimension_semantics=("arbitrary",), vmem_limit_bytes=VMEM_LIMIT),
        name="compress",
    )(page_table, src, tail, pos_t, bd, g2)


def _to_half(tile, src_half, dst_half):
    lane = lax.broadcasted_iota(jnp.int32, tile.shape, 1)
    src = tile if src_half == dst_half else pltpu.roll(tile, HEAD_DIM, axis=1)
    keep = (lane < HEAD_DIM) if dst_half == 0 else (lane >= HEAD_DIM)
    return jnp.where(keep, src, 0.0)


def _pair_tile(o_even, o_odd, half):
    lane = lax.broadcasted_iota(jnp.int32, o_even.shape, 1)
    if half == 0:
        return jnp.where(lane < HEAD_DIM, o_even, pltpu.roll(o_odd, HEAD_DIM, axis=1))
    return jnp.where(lane < HEAD_DIM, pltpu.roll(o_even, HEAD_DIM, axis=1), o_odd)


def _gate_tile(gates, pair, c, shape):
    lane = lax.broadcasted_iota(jnp.int32, shape, 1)
    he, ho = 2 * pair, 2 * pair + 1
    return jnp.where(lane < HEAD_DIM, gates[:, 3 * he + c:3 * he + c + 1], gates[:, 3 * ho + c:3 * ho + c + 1])


def _block_of_col(col, npp):
    return 2 * (col % npp) + col // npp


def _cmp_operand(x, npp):
    return jnp.concatenate([_to_half(x, 0, 0), _to_half(x, 1, 0)], axis=0).astype(BF16)


def _rank_select(score_t, blk_t, cols):
    rank = jnp.zeros(score_t.shape, jnp.int32)
    for r, n in cols:
        row = score_t[r:r + 1, :]
        ahead = (row > score_t) | ((row == score_t) & (blk_t > n))
        rank = rank + ahead.astype(jnp.int32)
    return ((rank < TOP_K_BLOCKS) & (score_t > -0.5)).astype(F32)


POS_HI, POS_LO = HEAD_DIM, HEAD_DIM + 1
MASK_BIG = 1e30


def _attn_prompt_kernel(q_ref, gates_ref, kc_ref, vc_ref, ks_ref, vsx_ref, kw_ref, vwx_ref, exp_ref, gp_ref, a_ref,
                        zbuf, m_sc, acc_sc, *, n_pages, npp):
    i = pl.program_id(1)
    q0 = i * Q_TILE
    rows = Q_PER_KV * Q_TILE
    ncols = 2 * npp
    qpos = q0 + lax.broadcasted_iota(jnp.int32, (Q_TILE, 1), 0)
    blk = _block_of_col(lax.broadcasted_iota(jnp.int32, (Q_TILE, ncols), 1), npp)
    real_cols = [(half * n_pages + p, 2 * p + half) for half in range(2) for p in range(n_pages)]

    g = gates_ref[0]
    g_hi = g.astype(BF16)
    g_lo = (g - g_hi.astype(F32)).astype(BF16)
    gexp = (jnp.dot(g_hi, gp_ref[...], preferred_element_type=F32)
            + jnp.dot(g_lo, gp_ref[...], preferred_element_type=F32))

    r_key = lax.broadcasted_iota(jnp.int32, (Q_TILE, K_CHUNK), 1)
    r_qry = lax.broadcasted_iota(jnp.int32, (Q_TILE, K_CHUNK), 0)
    keep = {"causal": r_key <= r_qry, "lower": r_key > r_qry}

    all_rows = N_HEADS * Q_TILE

    def branch(lhs, k_ref, vx_ref, slc, chunks, c_lo, c_hi):
        m_sc[...] = jnp.full((all_rows, LANES), NEG, F32)

        def scores(c, kind):
            rr = lax.broadcasted_iota(jnp.int32, (HEAD_DIM, K_CHUNK), 0)
            tt = lax.broadcasted_iota(jnp.int32, (HEAD_DIM, K_CHUNK), 1).astype(F32)
            hi = ((c - i) * K_CHUNK).astype(F32)
            pos = jnp.where(rr == 0, hi, jnp.where(rr == 1, tt, 0.0)).astype(BF16)
            for kvh in range(N_KV_HEADS):
                parts = [k_ref[0, c, kvh * HEAD_DIM:(kvh + 1) * HEAD_DIM, :], pos]
                if slc:
                    parts.append(exp_ref[c])
                kx = jnp.concatenate(parts, axis=0)
                for gq in range(Q_PER_KV):
                    sl = slice((kvh * Q_PER_KV + gq) * Q_TILE, (kvh * Q_PER_KV + gq + 1) * Q_TILE)
                    zg = jnp.dot(lhs[sl], kx, preferred_element_type=F32)
                    if kind is not None:
                        zg = jnp.where(keep[kind], zg, NEG)
                    zbuf[c, sl, :] = zg
                    m_sc[sl] = jnp.maximum(m_sc[sl], jnp.maximum(zg[:, :LANES], zg[:, LANES:]))

        def in_pairs(lo, hi, one):
            n = hi - lo

            def body(k, carry):
                one(lo + 2 * k)
                one(lo + 2 * k + 1)
                return carry
            lax.fori_loop(0, n // 2, body, 0)
            pl.when(n % 2 == 1)(lambda: one(hi - 1))

        for group in chunks:
            if not isinstance(group[1], list):
                in_pairs(group[0], group[1], lambda c: scores(c, None))
            else:
                cond, items = group[0], group[1]

                def run(items=items):
                    for c, kind in items:
                        scores(c, kind)
                run() if cond is None else pl.when(cond)(run)

        m = jnp.broadcast_to(jnp.max(m_sc[...], axis=-1, keepdims=True), (all_rows, LANES))
        acc_sc[...] = jnp.zeros((all_rows, LANES), F32)

        def weigh(c):
            for h in range(N_HEADS):
                sl = slice(h * Q_TILE, (h + 1) * Q_TILE)
                z = zbuf[c, sl, :]
                e = jnp.concatenate([jnp.exp(z[:, :LANES] - m[sl]), jnp.exp(z[:, LANES:] - m[sl])], axis=1)
                acc_sc[sl] += jnp.dot(e.astype(BF16), vx_ref[0, c, h // Q_PER_KV], preferred_element_type=F32)

        in_pairs(c_lo, c_hi, weigh)
        return acc_sc[...]

    def pair_norm(acc_e, acc_o):
        lane = lax.broadcasted_iota(jnp.int32, acc_e.shape, 1)
        safe = lambda d: jnp.where(d > 0.0, d, 1.0)
        sw_e = pltpu.roll(acc_e, HEAD_DIM, axis=1)
        sw_o = pltpu.roll(acc_o, HEAD_DIM, axis=1)
        return jnp.where(lane < HEAD_DIM, acc_e / safe(sw_e), sw_o / safe(acc_o))

    o_cmp_all, lhs_slc_all, lhs_win_all = [], [], []
    for kvh in range(N_KV_HEADS):
        slopes = [_slope(kvh * Q_PER_KV + g) for g in range(Q_PER_KV)]
        lane = lax.broadcasted_iota(jnp.int32, (Q_TILE, LANES), 1)
        pieces, pieces_x = [], []
        for gq in range(Q_PER_KV):
            h = kvh * Q_PER_KV + gq
            tile = q_ref[0, :, (h // 2) * LANES:(h // 2 + 1) * LANES].astype(F32) * SCALE
            low = _to_half(tile, h % 2, 0)
            pieces.append(low)
            pieces_x.append(jnp.where((lane == POS_HI) | (lane == POS_LO), slopes[gq], low))
        qk = jnp.concatenate(pieces, axis=0).astype(BF16)
        qk_x = jnp.concatenate(pieces_x, axis=0).astype(BF16)

        kc = _cmp_operand(kc_ref[0, kvh], npp)
        vc = _cmp_operand(vc_ref[0, kvh], npp)
        s = _dot_t(qk, kc)
        dist_c = qpos - (blk * CMP_BLOCK + (CMP_BLOCK - 1))
        mask_c = dist_c >= 0
        dist_cf = dist_c.astype(F32)
        p_list = []
        p_kv = jnp.zeros((Q_TILE, ncols), F32)
        for g in range(Q_PER_KV):
            z = jnp.where(mask_c, s[g * Q_TILE:(g + 1) * Q_TILE] - slopes[g] * dist_cf, NEG)
            mx = jnp.max(z, axis=-1, keepdims=True)
            e = jnp.where(mask_c, jnp.exp(z - mx), 0.0)
            den = jnp.sum(e, axis=-1, keepdims=True)
            p = e / jnp.where(den > 0.0, den, 1.0)
            p_kv = p_kv + p
            p_list.append(p.astype(BF16))
        o_cmp = jnp.dot(jnp.concatenate(p_list, axis=0), vc, preferred_element_type=F32)

        cand = blk * CMP_BLOCK <= qpos
        forced = (blk == qpos // CMP_BLOCK) | (blk == 0)
        score = jnp.where(forced, FORCE_SCORE, jnp.where(cand, p_kv, -1.0))
        score_t = score.T
        st = jnp.concatenate([score_t[:n_pages], score_t[npp:npp + n_pages]], axis=0)
        rr = lax.broadcasted_iota(jnp.int32, (2 * n_pages, Q_TILE), 0)
        bt = jnp.where(rr < n_pages, 2 * rr, 2 * (rr - n_pages) + 1)
        sel_s = _rank_select(st, bt, [(r, n) for r, (_, n) in enumerate(real_cols)])
        pad = jnp.zeros((npp - n_pages, Q_TILE), F32)
        sel_t = jnp.concatenate([sel_s[:n_pages], pad, sel_s[n_pages:], pad], axis=0)
        sel = sel_t.T
        sel_bias = jnp.where(sel > 0.5, 0.0, -MASK_BIG).astype(BF16)
        lhs_slc_all.append(jnp.concatenate([qk_x, jnp.concatenate([sel_bias] * Q_PER_KV, axis=0)], axis=1))
        lhs_win_all.append(qk_x)
        o_cmp_all.append(o_cmp)

    o_slc = branch(jnp.concatenate(lhs_slc_all, axis=0), ks_ref, vsx_ref, True,
                   [(0, i), (None, [(i, "causal")])], 0, i + 1)
    o_win = branch(jnp.concatenate(lhs_win_all, axis=0), kw_ref, vwx_ref, False,
                   [(i >= 2, [(i - 2, "lower"), (i - 1, None), (i, "causal")]),
                    (i == 1, [(0, None), (1, "causal")]),
                    (i == 0, [(0, "causal")])],
                   jnp.maximum(i - 2, 0), i + 1)
    o_cmp = jnp.concatenate(o_cmp_all, axis=0)

    for pair in range(N_HEADS // 2):
        e_sl = slice(2 * pair * Q_TILE, (2 * pair + 1) * Q_TILE)
        o_sl = slice((2 * pair + 1) * Q_TILE, (2 * pair + 2) * Q_TILE)
        col = pair * LANES
        tile = (gexp[:, col:col + LANES] * _pair_tile(o_cmp[e_sl], o_cmp[o_sl], 0)
                + gexp[:, ATTN_WIDTH + col:ATTN_WIDTH + col + LANES] * pair_norm(o_slc[e_sl], o_slc[o_sl])
                + gexp[:, 2 * ATTN_WIDTH + col:2 * ATTN_WIDTH + col + LANES] * pair_norm(o_win[e_sl], o_win[o_sl]))
        a_ref[0, :, pair * LANES:(pair + 1) * LANES] = tile.astype(BF16)


def _attn_prompt(q, gates, kc, vc, ks16, vsx, kw16, vwx, expand, gate_place, n_pages):
    b, s, _ = q.shape
    npp = kc.shape[2]
    n_chunks = s // K_CHUNK
    rows = Q_PER_KV * Q_TILE
    assert WINDOW == 2 * K_CHUNK and Q_TILE == K_CHUNK
    kern = functools.partial(_attn_prompt_kernel, n_pages=n_pages, npp=npp)
    per_b4 = lambda bi, i: (bi, 0, 0, 0)
    per_b5 = lambda bi, i: (bi, 0, 0, 0, 0)
    return pl.pallas_call(
        kern,
        grid=(b, s // Q_TILE),
        in_specs=[
            pl.BlockSpec((1, Q_TILE, ATTN_WIDTH), lambda bi, i: (bi, i, 0)),
            pl.BlockSpec((1, Q_TILE, GATE_PAD), lambda bi, i: (bi, i, 0)),
            pl.BlockSpec((1, N_KV_HEADS, npp, LANES), per_b4),
            pl.BlockSpec((1, N_KV_HEADS, npp, LANES), per_b4),
            pl.BlockSpec((1, n_chunks, LANES, K_CHUNK), per_b4),
            pl.BlockSpec((1, n_chunks, N_KV_HEADS, K_CHUNK, LANES), per_b5),
            pl.BlockSpec((1, n_chunks, LANES, K_CHUNK), per_b4),
            pl.BlockSpec((1, n_chunks, N_KV_HEADS, K_CHUNK, LANES), per_b5),
            pl.BlockSpec((n_chunks, 2 * npp, K_CHUNK), lambda bi, i: (0, 0, 0)),
            pl.BlockSpec((GATE_PAD, 3 * ATTN_WIDTH), lambda bi, i: (0, 0)),
        ],
        out_specs=pl.BlockSpec((1, Q_TILE, ATTN_WIDTH), lambda bi, i: (bi, i, 0)),
        out_shape=jax.ShapeDtypeStruct((b, s, ATTN_WIDTH), BF16),
        scratch_shapes=[
            pltpu.VMEM((n_chunks, N_KV_HEADS * rows, K_CHUNK), F32),
            pltpu.VMEM((N_KV_HEADS * rows, LANES), F32),
            pltpu.VMEM((N_KV_HEADS * rows, LANES), F32),
        ],
        compiler_params=pltpu.CompilerParams(
            dimension_semantics=("arbitrary", "arbitrary"), vmem_limit_bytes=VMEM_LIMIT),
        name="attn_prompt",
    )(q, gates, kc, vc, ks16, vsx, kw16, vwx, expand, gate_place)


S_ROWS = LANES
S_CHUNK = 1024


def _sample_rows(q_ref, ds, past):
    n_real = N_HEADS * ds
    by_kvh, low = [], []
    for h in range(N_HEADS):
        tile = q_ref[0, :, (h // 2) * LANES:(h // 2 + 1) * LANES].astype(F32) * SCALE
        by_kvh.append(_to_half(tile, h % 2, h // Q_PER_KV))
        low.append(_to_half(tile, h % 2, 0))
    zpad = jnp.zeros((S_ROWS - n_real, LANES), F32)
    qrows = jnp.concatenate(by_kvh + [zpad], axis=0).astype(BF16)
    qlow = jnp.concatenate(low + [zpad], axis=0).astype(BF16)
    r_col = lax.broadcasted_iota(jnp.int32, (S_ROWS, 1), 0)
    q_of_r = r_col % ds
    h_of_r = r_col // ds
    sl_r = jnp.zeros((S_ROWS, 1), F32)
    for h in range(N_HEADS):
        sl_r = jnp.where(h_of_r == h, _slope(h), sl_r)
    return qrows, qlow, q_of_r, past + q_of_r, sl_r, h_of_r // Q_PER_KV


def _select_sample_kernel(q_ref, kc_ref, vc_ref, ocmp_ref, sel_ref, need_ref, *, past, ds, npp):
    n_real = N_HEADS * ds
    ncols = 2 * npp
    n_blocks = (past + ds + CMP_BLOCK - 1) // CMP_BLOCK
    _, qlow, _, qp_r, sl_r, kvh_r = _sample_rows(q_ref, ds, past)

    blk = _block_of_col(lax.broadcasted_iota(jnp.int32, (S_ROWS, ncols), 1), npp)
    s = jnp.where(kvh_r == 0, _dot_t(qlow, _cmp_operand(kc_ref[0, 0], npp)),
                  _dot_t(qlow, _cmp_operand(kc_ref[0, 1], npp)))
    dist_c = qp_r - (blk * CMP_BLOCK + (CMP_BLOCK - 1))
    mask_c = dist_c >= 0
    z = jnp.where(mask_c, s - sl_r * dist_c.astype(F32), NEG)
    mx = jnp.max(z, axis=-1, keepdims=True)
    e = jnp.where(mask_c, jnp.exp(z - mx), 0.0)
    den = jnp.sum(e, axis=-1, keepdims=True)
    p = e / jnp.where(den > 0.0, den, 1.0)
    pb = p.astype(BF16)
    o_cmp = jnp.where(kvh_r == 0, jnp.dot(pb, _cmp_operand(vc_ref[0, 0], npp), preferred_element_type=F32),
                      jnp.dot(pb, _cmp_operand(vc_ref[0, 1], npp), preferred_element_type=F32))

    kq = N_KV_HEADS * ds
    p_kv = []
    for kvh in range(N_KV_HEADS):
        acc = jnp.zeros((ds, ncols), F32)
        for g in range(Q_PER_KV):
            r0 = (kvh * Q_PER_KV + g) * ds
            acc = acc + p[r0:r0 + ds]
        p_kv.append(acc)
    p_kv = jnp.concatenate(p_kv, axis=0)
    blk2 = _block_of_col(lax.broadcasted_iota(jnp.int32, (kq, ncols), 1), npp)
    qp2 = past + lax.broadcasted_iota(jnp.int32, (kq, 1), 0) % ds
    cand = blk2 * CMP_BLOCK <= qp2
    forced = (blk2 == qp2 // CMP_BLOCK) | (blk2 == 0)
    score = jnp.where(forced, FORCE_SCORE, jnp.where(cand, p_kv, -1.0))
    rank = jnp.zeros((kq, ncols), jnp.int32)
    for n in range(n_blocks):
        c = (n % 2) * npp + n // 2
        col = score[:, c:c + 1]
        ahead = (col > score) | ((col == score) & (blk2 > n))
        rank = rank + ahead.astype(jnp.int32)
    sel2 = ((rank < TOP_K_BLOCKS) & (score > -0.5)).astype(F32)
    sel_rows = []
    for h in range(N_HEADS):
        kvh = h // Q_PER_KV
        sel_rows.append(sel2[kvh * ds:(kvh + 1) * ds])
    sel_rows.append(jnp.zeros((S_ROWS - n_real, ncols), F32))
    sel_rows = jnp.concatenate(sel_rows, axis=0)
    ocmp_ref[0] = o_cmp
    sel_ref[0] = sel_rows
    any_row = jnp.max(sel_rows, axis=0, keepdims=True)
    page_need = jnp.maximum(any_row[:, :npp], any_row[:, npp:])
    need_ref[0] = jnp.broadcast_to(page_need, (8, npp))


def _select_sample(q, kc, vc, *, past):
    b, ds, _ = q.shape
    npp = kc.shape[2]
    kern = functools.partial(_select_sample_kernel, past=past, ds=ds, npp=npp)
    per_b3 = lambda i: (i, 0, 0)
    per_b4 = lambda i: (i, 0, 0, 0)
    return pl.pallas_call(
        kern,
        grid=(b,),
        in_specs=[pl.BlockSpec((1, ds, ATTN_WIDTH), per_b3),
                  pl.BlockSpec((1, N_KV_HEADS, npp, LANES), per_b4),
                  pl.BlockSpec((1, N_KV_HEADS, npp, LANES), per_b4)],
        out_specs=(pl.BlockSpec((1, S_ROWS, LANES), per_b3), pl.BlockSpec((1, S_ROWS, 2 * npp), per_b3),
                   pl.BlockSpec((1, 8, npp), per_b3)),
        out_shape=(jax.ShapeDtypeStruct((b, S_ROWS, LANES), F32), jax.ShapeDtypeStruct((b, S_ROWS, 2 * npp), F32),
                   jax.ShapeDtypeStruct((b, 8, npp), F32)),
        compiler_params=pltpu.CompilerParams(dimension_semantics=("arbitrary",), vmem_limit_bytes=VMEM_LIMIT),
        name="select_sample",
    )(q, kc, vc)


def _attn_sample_kernel(pt_ref, order_ref, nn_ref, q_ref, gates_ref, ocmp_ref, sel_ref, kvs_new_ref, kvw_new_ref,
                        win_ref, cache_ref, a_ref, kbuf, zbuf, sem, *, n_pages, past, ds, npp):
    b = pl.program_id(0)
    nb = pl.num_programs(0)
    ncols = 2 * npp
    half_cols = KV_COLS // 2
    pages_per_chunk = S_CHUNK // PAGE

    def page_copy(row, j, slot):
        return pltpu.make_async_copy(
            cache_ref.at[pt_ref[row, order_ref[row, j]]], kbuf.at[slot, :, pl.ds(j * PAGE, PAGE)], sem.at[slot])

    def row_copies(row, slot, fn):
        def body(j, carry):
            fn(page_copy(row, j, slot))
            return carry
        lax.fori_loop(0, nn_ref[row], body, 0)

    @pl.when(b == 0)
    def _():
        kbuf[...] = jnp.zeros(kbuf.shape, F32)
        row_copies(0, 0, lambda cp: cp.start())

    slot = b % 2

    @pl.when(b + 1 < nb)
    def _():
        row_copies(b + 1, 1 - slot, lambda cp: cp.start())

    qrows, _, q_of_r, qp_r, sl_r, _ = _sample_rows(q_ref, ds, past)
    o_cmp = ocmp_ref[0]
    sel_rows = sel_ref[0]
    sel16 = sel_rows.astype(BF16)

    new_lane = lax.broadcasted_iota(jnp.int32, (1, LANES), 1)

    def new_scores(ref, extra_mask):
        k_new = ref[0, :half_cols, :].astype(BF16)
        s_new = jnp.dot(qrows, k_new, preferred_element_type=F32)
        dist = q_of_r - new_lane
        mask = (new_lane < ds) & (dist >= 0) & extra_mask
        return jnp.where(mask, s_new - sl_r * dist.astype(F32), NEG)

    def weighted_new(ref, e_new):
        return _dot_t(e_new.astype(BF16), ref[0, half_cols:, :].astype(BF16))

    row_copies(b, slot, lambda cp: cp.wait())
    n_need = nn_ref[b]
    n_chunks = (n_need + pages_per_chunk - 1) // pages_per_chunk

    def pass1(c, mrow):
        k0 = pl.multiple_of(c * S_CHUNK, S_CHUNK)
        kt = kbuf[slot, :half_cols, pl.ds(k0, S_CHUNK)].astype(BF16)
        st = jnp.dot(qrows, kt, preferred_element_type=F32)
        lane = lax.broadcasted_iota(jnp.int32, (1, S_CHUNK), 1)
        tpos = jnp.full((1, S_CHUNK), -PAGE * n_pages, jnp.int32)
        for jj in range(pages_per_chunk):
            j = c * pages_per_chunk + jj
            page = order_ref[b, jnp.minimum(j, n_pages - 1)]
            base = jnp.where(j < n_need, page * PAGE, -PAGE * n_pages) - jj * PAGE
            tpos = jnp.where(lane // PAGE == jj, base + lane, tpos)
        blk_c = _block_of_col(lax.broadcasted_iota(jnp.int32, (ncols, S_CHUNK), 0), npp)
        expand = (blk_c == jnp.broadcast_to(tpos, (ncols, S_CHUNK)) // CMP_BLOCK).astype(BF16)
        chosen = jnp.dot(sel16, expand, preferred_element_type=F32) > 0.5
        dist = qp_r - tpos
        zc = jnp.where(chosen & (dist >= 0), st - sl_r * dist.astype(F32), NEG)
        zbuf[c] = zc
        return jnp.maximum(mrow, jnp.max(zc, axis=-1, keepdims=True))

    mrow = lax.fori_loop(0, n_chunks, pass1, jnp.full((S_ROWS, 1), NEG, F32))
    c_last = ((past // CMP_BLOCK) % 2) * npp + (past // CMP_BLOCK) // 2
    z_new = new_scores(kvs_new_ref, sel_rows[:, c_last:c_last + 1] > 0.5)
    m_r = jnp.maximum(mrow, jnp.max(z_new, axis=-1, keepdims=True))

    def pass2(c, carry):
        acc, den_r = carry
        k0 = pl.multiple_of(c * S_CHUNK, S_CHUNK)
        zc = zbuf[c]
        ec = jnp.where(zc > 0.5 * NEG, jnp.exp(zc - m_r), 0.0)
        vt = kbuf[slot, half_cols:, pl.ds(k0, S_CHUNK)].astype(BF16)
        return acc + _dot_t(ec.astype(BF16), vt), den_r + jnp.sum(ec, axis=-1, keepdims=True)

    e_new = jnp.where(z_new > 0.5 * NEG, jnp.exp(z_new - m_r), 0.0)
    acc, den_r = lax.fori_loop(0, n_chunks, pass2,
                               (weighted_new(kvs_new_ref, e_new), jnp.sum(e_new, axis=-1, keepdims=True)))
    o_slc = acc / jnp.where(den_r > 0.0, den_r, 1.0)

    win_buf = win_ref.shape[2]
    kw = win_ref[0, :half_cols, :].astype(BF16)
    st = jnp.dot(qrows, kw, preferred_element_type=F32)
    kpos = past - win_buf + lax.broadcasted_iota(jnp.int32, (1, win_buf), 1)
    dist = qp_r - kpos
    mask = (dist >= 0) & (dist < WINDOW) & (kpos >= 0)
    z_w = jnp.where(mask, st - sl_r * dist.astype(F32), NEG)
    zw_new = new_scores(kvw_new_ref, True)
    m_w = jnp.maximum(jnp.max(z_w, axis=-1, keepdims=True), jnp.max(zw_new, axis=-1, keepdims=True))
    e_w = jnp.where(z_w > 0.5 * NEG, jnp.exp(z_w - m_w), 0.0)
    ew_new = jnp.where(zw_new > 0.5 * NEG, jnp.exp(zw_new - m_w), 0.0)
    acc_w = _dot_t(e_w.astype(BF16), win_ref[0, half_cols:, :].astype(BF16)) + weighted_new(kvw_new_ref, ew_new)
    den_w = jnp.sum(e_w, axis=-1, keepdims=True) + jnp.sum(ew_new, axis=-1, keepdims=True)
    o_win = acc_w / jnp.where(den_w > 0.0, den_w, 1.0)

    gates = gates_ref[0]
    for pair in range(N_HEADS // 2):
        kvh = (2 * pair) // Q_PER_KV
        e_sl = slice(2 * pair * ds, (2 * pair + 1) * ds)
        o_sl = slice((2 * pair + 1) * ds, (2 * pair + 2) * ds)
        shape = (ds, LANES)
        tile = (_gate_tile(gates, pair, 0, shape) * _pair_tile(o_cmp[e_sl], o_cmp[o_sl], 0)
                + _gate_tile(gates, pair, 1, shape) * _pair_tile(o_slc[e_sl], o_slc[o_sl], kvh)
                + _gate_tile(gates, pair, 2, shape) * _pair_tile(o_win[e_sl], o_win[o_sl], kvh))
        a_ref[0, :, pair * LANES:(pair + 1) * LANES] = tile.astype(BF16)


def _attn_sample(page_table, order, n_need, q, gates, o_cmp, sel_rows, kvs_new, kvw_new, state_win_t, cache_pages,
                 *, past):
    b, n_pages = page_table.shape
    ds = q.shape[1]
    npp = sel_rows.shape[2] // 2
    win_buf = state_win_t.shape[2]
    n_keys = n_pages * PAGE
    kern = functools.partial(_attn_sample_kernel, n_pages=n_pages, past=past, ds=ds, npp=npp)
    per_b = lambda i, pt, od, nn: (i, 0, 0)
    grid_spec = pltpu.PrefetchScalarGridSpec(
        num_scalar_prefetch=3,
        grid=(b,),
        in_specs=[
            pl.BlockSpec((1, ds, ATTN_WIDTH), per_b),
            pl.BlockSpec((1, ds, GATE_PAD), per_b),
            pl.BlockSpec((1, S_ROWS, LANES), per_b),
            pl.BlockSpec((1, S_ROWS, 2 * npp), per_b),
            pl.BlockSpec((1, KV_COLS, LANES), per_b),
            pl.BlockSpec((1, KV_COLS, LANES), per_b),
            pl.BlockSpec((1, KV_COLS, win_buf), per_b),
            pl.BlockSpec(memory_space=pl.ANY),
        ],
        out_specs=pl.BlockSpec((1, ds, ATTN_WIDTH), per_b),
        scratch_shapes=[
            pltpu.VMEM((2, KV_COLS, n_keys), F32),
            pltpu.VMEM((n_keys // S_CHUNK, S_ROWS, S_CHUNK), F32),
            pltpu.SemaphoreType.DMA((2,)),
        ],
    )
    return pl.pallas_call(
        kern,
        grid_spec=grid_spec,
        out_shape=jax.ShapeDtypeStruct((b, ds, ATTN_WIDTH), BF16),
        compiler_params=pltpu.CompilerParams(
            dimension_semantics=("arbitrary",), vmem_limit_bytes=VMEM_LIMIT),
        name="attn_sample",
    )(page_table, order, n_need, q, gates, o_cmp, sel_rows, kvs_new, kvw_new, state_win_t, cache_pages)


X_TILES = D_MODEL // LANES
PAY_SUB = X_TILES + 1
GROUP_LANE = 0
ROUTER_ROWS = 32


def _rows_to_tiles(ref, lo, val):
    for k in range(val.shape[1] // LANES):
        ref[:, lo + k, :] = val[:, k * LANES:(k + 1) * LANES]


def _tiles_to_rows(ref, lo, n):
    return jnp.concatenate([ref[:, lo + k, :] for k in range(n)], axis=1)


def _ffn_kernel(x_ref, a_ref, m_ref, wo_ref, g2_ref, wr_ref, *out_refs, packed):
    h = (x_ref[...]
         + jnp.dot(a_ref[...], wo_ref[:ATTN_WIDTH, :], preferred_element_type=F32)
         + jnp.dot(m_ref[...], wo_ref[ATTN_WIDTH:, :], preferred_element_type=F32))
    ms = jnp.mean(h * h, axis=-1, keepdims=True)
    n2f = (h * lax.rsqrt(ms + EPS)) * g2_ref[...]
    n2 = n2f.astype(BF16)
    tm = n2.shape[0]
    logits = _dot_t(wr_ref[...], n2)[:ROUTER_ROWS]
    row = lax.broadcasted_iota(jnp.int32, (ROUTER_ROWS, tm), 0)
    big = jnp.int32(LANES)

    def masked_softmax(mask):
        zz = jnp.where(mask, logits, NEG)
        mx = jnp.max(zz, axis=0, keepdims=True)
        ee = jnp.where(mask, jnp.exp(zz - mx), 0.0)
        return ee / jnp.sum(ee, axis=0, keepdims=True)

    def first_argmax(vals, mask):
        v = jnp.max(jnp.where(mask, vals, -1.0), axis=0, keepdims=True)
        idx = jnp.min(jnp.where(mask & (vals == v), row, big), axis=0, keepdims=True)
        return v, idx

    is_g = row < N_EXPERT_GROUPS
    pg = masked_softmax(is_g)
    g_val, g_idx = first_argmax(pg, is_g)
    e_row = row - ROUTER_LANE0
    in_grp = (e_row >= 0) & (e_row < N_EXPERTS) & (e_row // EXPERTS_PER_GROUP == g_idx)
    pe = masked_softmax(in_grp)
    v1, i1 = first_argmax(pe, in_grp)
    rest = in_grp & (row != i1)
    v2, i2 = first_argmax(pe, rest)
    scale = g_val / (v1 + v2)
    gate_t = jnp.where(row == i1, v1 * scale, jnp.where(row == i2, v2 * scale, 0.0))
    gate_t = jnp.where(row == GROUP_LANE, g_idx.astype(F32), gate_t)
    gate = jnp.concatenate([gate_t, jnp.zeros((LANES - ROUTER_ROWS, tm), F32)], axis=0).T
    if packed:
        h_ref, pay_ref, gate_ref = out_refs
        h_ref[...] = h
        gate_ref[...] = gate
        _rows_to_tiles(pay_ref, 0, n2f)
        pay_ref[:, X_TILES, :] = gate
    else:
        h_ref, n2_ref, gate_ref = out_refs
        h_ref[...] = h
        n2_ref[...] = n2
        gate_ref[...] = gate


def _ffn(x2d, a2d, m2d, w_out16, g2, w_router16, tm, packed):
    t = x2d.shape[0]
    row = lambda i: (i, 0)
    const = lambda i: (0, 0)
    if packed:
        out_specs = (pl.BlockSpec((tm, D_MODEL), row), pl.BlockSpec((tm, PAY_SUB, LANES), lambda i: (i, 0, 0)),
                     pl.BlockSpec((tm, LANES), row))
        out_shape = (jax.ShapeDtypeStruct((t, D_MODEL), F32), jax.ShapeDtypeStruct((t, PAY_SUB, LANES), F32),
                     jax.ShapeDtypeStruct((t, LANES), F32))
    else:
        out_specs = (pl.BlockSpec((tm, D_MODEL), row), pl.BlockSpec((tm, D_MODEL), row),
                     pl.BlockSpec((tm, LANES), row))
        out_shape = (jax.ShapeDtypeStruct((t, D_MODEL), F32), jax.ShapeDtypeStruct((t, D_MODEL), BF16),
                     jax.ShapeDtypeStruct((t, LANES), F32))
    return pl.pallas_call(
        functools.partial(_ffn_kernel, packed=packed),
        grid=(t // tm,),
        in_specs=[
            pl.BlockSpec((tm, D_MODEL), row),
            pl.BlockSpec((tm, ATTN_WIDTH), row),
            pl.BlockSpec((tm, POOL_WIDTH), row),
            pl.BlockSpec((D_MODEL, D_MODEL), const),
            pl.BlockSpec((1, D_MODEL), const),
            pl.BlockSpec((LANES, D_MODEL), const),
        ],
        out_specs=out_specs,
        out_shape=out_shape,
        compiler_params=pltpu.CompilerParams(
            dimension_semantics=("arbitrary",), vmem_limit_bytes=VMEM_LIMIT),
        name="ffn",
    )(x2d, a2d, m2d, w_out16, g2, w_router16)


MOE_ROWS = 512
ROW_CHUNK = 1024
GATHER_CHUNK = 512
PLAN_TILE = 1024
PLAN_COLS = 8


def _plan_kernel(gate_ref, tri_ref, out_ref, cnt_ref, carry):
    @pl.when(pl.program_id(0) == 0)
    def _():
        carry[...] = jnp.zeros(carry.shape, F32)

    gl = gate_ref[...]
    lane = lax.broadcasted_iota(jnp.int32, gl.shape, 1)
    gid = gl[:, GROUP_LANE:GROUP_LANE + 1]
    onehot = jnp.where((lane < N_EXPERT_GROUPS) & (lane.astype(F32) == gid), 1.0, 0.0)
    before = jnp.dot(tri_ref[...], onehot.astype(BF16), preferred_element_type=F32) + carry[0:1, :]
    rank = jnp.sum(onehot * before, axis=-1, keepdims=True)
    out_ref[...] = jnp.where(lane == 0, rank, jnp.where(lane == 1, gid, 0.0))[:, :PLAN_COLS]
    carry[0:1, :] = carry[0:1, :] + jnp.sum(onehot, axis=0, keepdims=True)
    cnt_ref[...] = carry[...]


def _plan(gate):
    t = gate.shape[0]
    tri = jnp.asarray(np.tril(np.ones((PLAN_TILE, PLAN_TILE), np.float32), -1), dtype=BF16)
    return pl.pallas_call(
        _plan_kernel,
        grid=(t // PLAN_TILE,),
        in_specs=[pl.BlockSpec((PLAN_TILE, LANES), lambda i: (i, 0)),
                  pl.BlockSpec((PLAN_TILE, PLAN_TILE), lambda i: (0, 0))],
        out_specs=(pl.BlockSpec((PLAN_TILE, PLAN_COLS), lambda i: (i, 0)), pl.BlockSpec((8, LANES), lambda i: (0, 0))),
        out_shape=(jax.ShapeDtypeStruct((t, PLAN_COLS), F32), jax.ShapeDtypeStruct((8, LANES), F32)),
        scratch_shapes=[pltpu.VMEM((8, LANES), F32)],
        compiler_params=pltpu.CompilerParams(
            dimension_semantics=("arbitrary",), vmem_limit_bytes=VMEM_LIMIT),
        name="moe_plan",
    )(gate, tri)


def _scatter_kernel(meta_ref, slot_ref, pay_ref, xs_ref, zrow, sem, zsem):
    c = pl.program_id(0)

    @pl.when(c == 0)
    def _():
        zrow[...] = jnp.zeros(zrow.shape, F32)
        for g in range(N_EXPERT_GROUPS + 1):
            lo = meta_ref[g]
            hi = meta_ref[N_EXPERT_GROUPS + 1 + g]

            def zero_start(r, carry):
                pltpu.make_async_copy(zrow.at[pl.ds(0, 1)], xs_ref.at[pl.ds(r, 1)], zsem.at[0]).start()
                return carry

            def zero_wait(r, carry):
                pltpu.make_async_copy(zrow.at[pl.ds(0, 1)], xs_ref.at[pl.ds(0, 1)], zsem.at[0]).wait()
                return carry

            lax.fori_loop(lo, hi, zero_start, 0)
            lax.fori_loop(lo, hi, zero_wait, 0)

    def issue(u, carry):
        pltpu.make_async_copy(pay_ref.at[pl.ds(u, 1)], xs_ref.at[pl.ds(slot_ref[0, 0, u], 1)], sem.at[0]).start()
        return carry

    lax.fori_loop(0, ROW_CHUNK, issue, 0, unroll=8)
    pltpu.make_async_copy(pay_ref, xs_ref.at[pl.ds(0, ROW_CHUNK)], sem.at[0]).wait()


def _scatter(meta, slot3, pay, n_slots):
    t = pay.shape[0]
    return pl.pallas_call(
        _scatter_kernel,
        grid_spec=pltpu.PrefetchScalarGridSpec(
            num_scalar_prefetch=1,
            grid=(t // ROW_CHUNK,),
            in_specs=[pl.BlockSpec((1, 1, ROW_CHUNK), lambda c, meta: (c, 0, 0), memory_space=pltpu.SMEM),
                      pl.BlockSpec((ROW_CHUNK, PAY_SUB, LANES), lambda c, meta: (c, 0, 0))],
            out_specs=pl.BlockSpec(memory_space=pl.ANY),
            scratch_shapes=[pltpu.VMEM((1, PAY_SUB, LANES), F32), pltpu.SemaphoreType.DMA((1,)),
                            pltpu.SemaphoreType.DMA((1,))],
        ),
        out_shape=jax.ShapeDtypeStruct((n_slots, PAY_SUB, LANES), F32),
        compiler_params=pltpu.CompilerParams(dimension_semantics=("arbitrary",), vmem_limit_bytes=VMEM_LIMIT),
        name="moe_scatter",
    )(meta, slot3, pay)


def _gather_kernel(slot_ref, nslot_ref, h_ref, ys_ref, y_ref, buf, sem):
    c = pl.program_id(0)
    nc = pl.num_programs(0)

    def fetch(ref, slot):
        def issue(u, carry):
            pltpu.make_async_copy(ys_ref.at[pl.ds(ref[0, 0, u], 1)], buf.at[slot, pl.ds(u, 1)], sem.at[slot]).start()
            return carry
        lax.fori_loop(0, GATHER_CHUNK, issue, 0, unroll=8)

    @pl.when(c == 0)
    def _():
        fetch(slot_ref, 0)

    @pl.when(c + 1 < nc)
    def _():
        fetch(nslot_ref, (c + 1) % 2)

    cur = c % 2
    pltpu.make_async_copy(ys_ref.at[pl.ds(0, GATHER_CHUNK)], buf.at[cur], sem.at[cur]).wait()
    y_ref[...] = h_ref[...] + jnp.concatenate([buf[cur, :, k, :] for k in range(X_TILES)], axis=1)


def _gather(slot, h, ys):
    t = h.shape[0]
    nc = t // GATHER_CHUNK
    slot3 = slot.reshape(nc, 1, GATHER_CHUNK)
    return pl.pallas_call(
        _gather_kernel,
        grid=(nc,),
        in_specs=[pl.BlockSpec((1, 1, GATHER_CHUNK), lambda c: (c, 0, 0), memory_space=pltpu.SMEM),
                  pl.BlockSpec((1, 1, GATHER_CHUNK), lambda c: (jnp.minimum(c + 1, nc - 1), 0, 0),
                               memory_space=pltpu.SMEM),
                  pl.BlockSpec((GATHER_CHUNK, D_MODEL), lambda c: (c, 0)),
                  pl.BlockSpec(memory_space=pl.ANY)],
        out_specs=pl.BlockSpec((GATHER_CHUNK, D_MODEL), lambda c: (c, 0)),
        out_shape=jax.ShapeDtypeStruct((t, D_MODEL), F32),
        scratch_shapes=[pltpu.VMEM((2, GATHER_CHUNK, X_TILES, LANES), F32), pltpu.SemaphoreType.DMA((2,))],
        compiler_params=pltpu.CompilerParams(dimension_semantics=("arbitrary",), vmem_limit_bytes=VMEM_LIMIT),
        name="moe_gather",
    )(slot3, slot3, h, ys)


def _group_moe_kernel(tg_ref, nu_ref, xs_ref, wg_ref, wu_ref, wd_ref, ys_ref):
    j = pl.program_id(0)

    @pl.when(j < nu_ref[0])
    def _():
        g = tg_ref[j]
        x = _tiles_to_rows(xs_ref, 0, X_TILES).astype(BF16)
        gate = xs_ref[:, X_TILES, :]
        lane = lax.broadcasted_iota(jnp.int32, gate.shape, 1)
        y = jnp.zeros((MOE_ROWS, D_MODEL), F32)
        for e in range(EXPERTS_PER_GROUP):
            gu = jnp.dot(x, wg_ref[e], preferred_element_type=F32)
            up = jnp.dot(x, wu_ref[e], preferred_element_type=F32)
            he = (gu * (1.0 / (1.0 + jnp.exp(-gu)))) * up
            out = jnp.dot(he.astype(BF16), wd_ref[e], preferred_element_type=F32)
            col = ROUTER_LANE0 + g * EXPERTS_PER_GROUP + e
            y = y + jnp.sum(jnp.where(lane == col, gate, 0.0), axis=-1, keepdims=True) * out
        _rows_to_tiles(ys_ref, 0, y)

    @pl.when(j >= nu_ref[0])
    def _():
        ys_ref[...] = jnp.zeros(ys_ref.shape, F32)


def _group_moe(tile_group, n_used, xs, wg16, wu16, wd16):
    n_slots = xs.shape[0]
    wspec = lambda shape: pl.BlockSpec((EXPERTS_PER_GROUP,) + shape, lambda j, tg, nu: (tg[j], 0, 0))
    return pl.pallas_call(
        _group_moe_kernel,
        grid_spec=pltpu.PrefetchScalarGridSpec(
            num_scalar_prefetch=2,
            grid=(n_slots // MOE_ROWS,),
            in_specs=[pl.BlockSpec((MOE_ROWS, PAY_SUB, LANES), lambda j, tg, nu: (j, 0, 0)),
                      wspec((D_MODEL, D_EXPERT)), wspec((D_MODEL, D_EXPERT)), wspec((D_EXPERT, D_MODEL))],
            out_specs=pl.BlockSpec((MOE_ROWS, X_TILES, LANES), lambda j, tg, nu: (j, 0, 0)),
        ),
        out_shape=jax.ShapeDtypeStruct((n_slots, X_TILES, LANES), F32),
        compiler_params=pltpu.CompilerParams(dimension_semantics=("arbitrary",), vmem_limit_bytes=VMEM_LIMIT),
        name="moe_group",
    )(tile_group, n_used, xs, wg16, wu16, wd16)


def _routed_moe(h, pay, gate, wg16, wu16, wd16):
    t = pay.shape[0]
    n_slots = t + N_EXPERT_GROUPS * MOE_ROWS
    n_tiles = n_slots // MOE_ROWS
    plan, counts = _plan(gate)
    rank = plan[:, 0].astype(jnp.int32)
    gid = plan[:, 1].astype(jnp.int32)
    cnt = counts[0, :N_EXPERT_GROUPS].astype(jnp.int32)
    padded = -(-cnt // MOE_ROWS) * MOE_ROWS
    ends = jnp.cumsum(padded)
    off = ends - padded
    slot = (off[gid] + rank).reshape(t // ROW_CHUNK, 1, ROW_CHUNK)
    tile_start = jnp.arange(n_tiles, dtype=jnp.int32) * MOE_ROWS
    tile_group = jnp.minimum(jnp.sum(tile_start[:, None] >= ends[None, :], axis=1), N_EXPERT_GROUPS - 1)
    n_used = (ends[-1] // MOE_ROWS).reshape(1)
    meta = jnp.concatenate([off + cnt, ends[-1:], ends, jnp.full((1,), n_slots, jnp.int32)]).astype(jnp.int32)
    xs = _scatter(meta, slot, pay, n_slots)
    ys = _group_moe(tile_group.astype(jnp.int32), n_used.astype(jnp.int32), xs, wg16, wu16, wd16)
    return _gather(slot, h, ys)


def _moe_kernel(n2_ref, gate_ref, h_ref, wg_ref, wu_ref, wd_ref, y_ref):
    e = pl.program_id(1)

    @pl.when(e == 0)
    def _():
        y_ref[...] = h_ref[...]

    n2 = n2_ref[...]
    gu = jnp.dot(n2, wg_ref[0], preferred_element_type=F32)
    up = jnp.dot(n2, wu_ref[0], preferred_element_type=F32)
    he = (gu * (1.0 / (1.0 + jnp.exp(-gu)))) * up
    out = jnp.dot(he.astype(BF16), wd_ref[0], preferred_element_type=F32)
    lane = lax.broadcasted_iota(jnp.int32, gate_ref.shape, 1)
    gcol = jnp.sum(jnp.where(lane == e + ROUTER_LANE0, gate_ref[...], 0.0), axis=-1, keepdims=True)
    y_ref[...] += gcol * out


def _moe(n2, gate, h, wg16, wu16, wd16, tm):
    t = n2.shape[0]
    row = lambda i, e: (i, 0)
    return pl.pallas_call(
        _moe_kernel,
        grid=(t // tm, N_EXPERTS),
        in_specs=[
            pl.BlockSpec((tm, D_MODEL), row),
            pl.BlockSpec((tm, LANES), row),
            pl.BlockSpec((tm, D_MODEL), row),
            pl.BlockSpec((1, D_MODEL, D_EXPERT), lambda i, e: (e, 0, 0)),
            pl.BlockSpec((1, D_MODEL, D_EXPERT), lambda i, e: (e, 0, 0)),
            pl.BlockSpec((1, D_EXPERT, D_MODEL), lambda i, e: (e, 0, 0)),
        ],
        out_specs=pl.BlockSpec((tm, D_MODEL), row),
        out_shape=jax.ShapeDtypeStruct((t, D_MODEL), F32),
        compiler_params=pltpu.CompilerParams(
            dimension_semantics=("arbitrary", "arbitrary"), vmem_limit_bytes=VMEM_LIMIT),
        name="moe",
    )(n2, gate, h, wg16, wu16, wd16)


PROJ_TILE = 512
MOE_TILE = 1024
PROMPT_NPP = LANES // 2
SAMPLE_NPP = LANES


def _round_up(n, mult):
    return -(-n // mult) * mult


def _feature_major(x5):
    b, t = x5.shape[:2]
    return jnp.transpose(x5, (0, 2, 3, 4, 1)).reshape(b, KV_COLS, t)


def _token_major(xt):
    b, _, t = xt.shape
    return jnp.transpose(xt.reshape(b, 2, N_KV_HEADS, HEAD_DIM, t), (0, 4, 1, 2, 3))


def _pad_pages(x, npp_from, npp_to):
    b = x.shape[0]
    x = x.reshape(b, N_KV_HEADS, npp_from, LANES)
    return jnp.concatenate([x, jnp.zeros((b, N_KV_HEADS, npp_to - npp_from, LANES), x.dtype)], axis=2)


def _expand_const(n_chunks, npp):
    col = np.arange(2 * npp)
    blk = 2 * (col % npp) + col // npp
    tok_blk = np.arange(n_chunks * K_CHUNK) // CMP_BLOCK
    e = (blk[None, :, None] == tok_blk.reshape(n_chunks, 1, K_CHUNK)).astype(np.float32)
    return jnp.asarray(e, dtype=BF16)


def _gate_place_const():
    gp = np.zeros((GATE_PAD, 3 * ATTN_WIDTH), np.float32)
    for h in range(N_HEADS):
        for c in range(3):
            gp[3 * h + c, c * ATTN_WIDTH + h * HEAD_DIM:c * ATTN_WIDTH + (h + 1) * HEAD_DIM] = 1.0
    return jnp.asarray(gp, dtype=BF16)


def kernel(x_prompt, x_sample, cache_cmp_kv, cache_slc_kv, state_win_kv, state_pool, page_table, norm1_g, w_in, q_norm_g, k_norm_cmp_g, k_norm_slc_g, k_norm_win_g, cmp_pos_emb, w_cmp_k, w_cmp_v, pool_w, pool_scale, w_out, norm2_g, w_router_group, w_router_expert, w_gate, w_up, w_down):
    b, s, _ = x_prompt.shape
    db, ds, _ = x_sample.shape
    n_pool, page_rows = cache_cmp_kv.shape[:2]
    assert page_rows == PAGE and s % PAGE == 0
    past = page_table.shape[1] * page_rows

    kv0 = ATTN_WIDTH
    kv1 = kv0 + KV_ROWS
    w_row = jnp.concatenate(
        [w_in[:, :kv0], w_in[:, kv1:kv1 + GATE_COLS], jnp.zeros((D_MODEL, GATE_PAD - GATE_COLS), w_in.dtype),
         w_in[:, kv1 + GATE_COLS:]], axis=1).astype(BF16)
    w_kv = w_in[:, kv0:kv1].T.astype(BF16)
    g1 = norm1_g[None, :]
    g2 = norm2_g[None, :]
    two = lambda g: jnp.tile(g, 2)[None, :]
    kg = jnp.stack([k_norm_slc_g, k_norm_win_g])[:, :, None]
    pos_t = jnp.tile(cmp_pos_emb.T, (1, 2))
    zb = jnp.zeros((HEAD_DIM, CMP_BLOCK, HEAD_DIM), F32)

    def blockdiag(w):
        wt = jnp.transpose(w, (1, 0, 2))
        return jnp.concatenate([jnp.concatenate([wt, zb], axis=2), jnp.concatenate([zb, wt], axis=2)], axis=1)

    bdk, bdv = blockdiag(w_cmp_k), blockdiag(w_cmp_v)
    zq = jnp.zeros_like(bdk)
    bd = jnp.concatenate([jnp.concatenate([bdk, zq], axis=2), jnp.concatenate([zq, bdv], axis=2)],
                         axis=1).astype(BF16)
    w_router = jnp.concatenate(
        [w_router_group, w_router_expert,
         jnp.zeros((D_MODEL, LANES - N_EXPERT_GROUPS - N_EXPERTS), w_router_group.dtype)], axis=1).T.astype(BF16)
    w_out16 = w_out.astype(BF16)
    pool_w16 = pool_w.astype(BF16)
    wg16, wu16, wd16 = w_gate.astype(BF16), w_up.astype(BF16), w_down.astype(BF16)
    ps = pool_scale[None, :]

    def ffn_moe(x2d, a2d, m2d, tm_ffn, tm_moe, routed):
        if routed:
            return _routed_moe(*_ffn(x2d, a2d, m2d, w_out16, g2, w_router, tm_ffn, True), wg16, wu16, wd16)
        h, n2, gate = _ffn(x2d, a2d, m2d, w_out16, g2, w_router, tm_ffn, False)
        return _moe(n2, gate, h, wg16, wu16, wd16, tm_moe)

    q, gates, u, kvc_t, kvs_t, kvw_t, ks16, vsx, kw16, vwx = _proj(
        x_prompt, g1, w_row, w_kv, two(q_norm_g), kg, PROJ_TILE)
    m_p = _pool(u, pool_w16, ps, s)
    pages_p = s // PAGE
    pt_p = jnp.zeros((b, pages_p), jnp.int32)
    dummy_tail = jnp.zeros((b, KV_COLS, LANES), F32)
    kc_p, vc_p = _compress(pt_p, kvc_t, dummy_tail, pos_t, bd, two(k_norm_cmp_g),
                           npp=pages_p, has_tail=False, paged=False)
    kc_p = _pad_pages(kc_p, pages_p, PROMPT_NPP)
    vc_p = _pad_pages(vc_p, pages_p, PROMPT_NPP)
    a_p = _attn_prompt(q, gates, kc_p, vc_p, ks16, vsx, kw16, vwx, _expand_const(s // K_CHUNK, PROMPT_NPP),
                       _gate_place_const(), pages_p)
    y_p = ffn_moe(x_prompt.reshape(b * s, D_MODEL), a_p.reshape(b * s, ATTN_WIDTH),
                  m_p.reshape(b * s, POOL_WIDTH), PROJ_TILE, MOE_TILE, True)

    ts = db * ds
    q_s, gates_s, u_s, kvc_st, kvs_st, kvw_st, _, _, _, _ = _proj(
        x_sample.reshape(1, ts, D_MODEL), g1, w_row, w_kv, two(q_norm_g), kg, ts)
    u_s = u_s.reshape(db, ds, POOL_WIDTH)
    pool_ext = jnp.concatenate([state_pool, u_s], axis=1)
    n_ext = pool_ext.shape[1]
    lead = _round_up(n_ext, 8) - n_ext
    pool_in = jnp.concatenate([jnp.zeros((db, lead, POOL_WIDTH), F32), pool_ext], axis=1)
    n_in = lead + n_ext
    m_s = _pool(pool_in.reshape(1, db * n_in, POOL_WIDTH), pool_w16, ps, n_in).reshape(db, n_in, POOL_WIDTH)[:, n_in - ds:]

    def new_rows_t(xt):
        x = jnp.transpose(xt.reshape(KV_COLS, db, ds), (1, 0, 2))
        return jnp.concatenate([x, jnp.zeros((db, KV_COLS, LANES - ds), F32)], axis=2)

    kvc_new, kvs_new, kvw_new = new_rows_t(kvc_st), new_rows_t(kvs_st), new_rows_t(kvw_st)
    n_pages = page_table.shape[1]
    npp_c = _round_up(n_pages + 1, 8)
    cmp_pages = _feature_major(cache_cmp_kv)
    slc_pages = _feature_major(cache_slc_kv)
    kc_s, vc_s = _compress(page_table, cmp_pages, kvc_new, pos_t, bd, two(k_norm_cmp_g),
                           npp=npp_c, has_tail=True, paged=True)
    kc_s = _pad_pages(kc_s, npp_c, SAMPLE_NPP)
    vc_s = _pad_pages(vc_s, npp_c, SAMPLE_NPP)
    win_t = _feature_major(state_win_kv)
    q_s3 = q_s.reshape(db, ds, ATTN_WIDTH)
    o_cmp_s, sel_s, need_s = _select_sample(q_s3, kc_s, vc_s, past=past)
    need = need_s[:, 0, :n_pages] > 0.5
    order = jnp.argsort(jnp.logical_not(need), axis=1, stable=True).astype(jnp.int32)
    n_need = jnp.sum(need, axis=1).astype(jnp.int32)
    a_s = _attn_sample(page_table, order, n_need, q_s3, gates_s.reshape(db, ds, GATE_PAD), o_cmp_s, sel_s,
                       kvs_new, kvw_new, win_t, slc_pages, past=past)
    y_s = ffn_moe(x_sample.reshape(ts, D_MODEL), a_s.reshape(ts, ATTN_WIDTH), m_s.reshape(ts, POOL_WIDTH), ts, ts, False)

    win_keep = min(WINDOW, s)
    sample5 = lambda xt: jnp.transpose(xt.reshape(2, N_KV_HEADS, HEAD_DIM, db, ds), (3, 4, 0, 1, 2))
    win_ctx_t = jnp.concatenate([win_t, kvw_new[:, :, :ds]], axis=2)
    return (y_p.reshape(b, s, D_MODEL), y_s.reshape(db, ds, D_MODEL),
            _token_major(kvc_t), sample5(kvc_st),
            _token_major(kvs_t), sample5(kvs_st),
            _token_major(kvw_t[:, :, s - win_keep:]), _token_major(win_ctx_t[:, :, ds:]),
            u[:, s - POOL_BUF:], pool_ext[:, ds:])
```

```python
import functools

import jax
import jax.numpy as jnp
import numpy as np
from jax import lax
from jax.experimental import pallas as pl
from jax.experimental.pallas import tpu as pltpu

F32 = jnp.float32
BF16 = jnp.bfloat16

D_MODEL = 1024
N_HEADS = 8
HEAD_DIM = 64
N_KV_HEADS = 2
Q_PER_KV = N_HEADS // N_KV_HEADS
ATTN_WIDTH = N_HEADS * HEAD_DIM
KV_COLS = 2 * N_KV_HEADS * HEAD_DIM
GATE_COLS = 3 * N_HEADS
POOL_WIDTH = D_MODEL - ATTN_WIDTH
POOL_WINDOWS = (2, 4, 8, 16)
POOL_GROUP_WIDTH = POOL_WIDTH // len(POOL_WINDOWS)
POOL_BUF = max(POOL_WINDOWS) - 1
CMP_BLOCK = 64
TOP_K_BLOCKS = 16
WINDOW = 512
FORCE_SCORE = 1.0e4
N_EXPERT_GROUPS = 4
EXPERTS_PER_GROUP = 4
N_EXPERTS = N_EXPERT_GROUPS * EXPERTS_PER_GROUP
D_EXPERT = 512
EPS = 1e-6
NEG = -1e30
SCALE = HEAD_DIM ** -0.5

LANES = 128
PAGE = 2 * CMP_BLOCK
GATE_PAD = LANES
ROW_COLS = ATTN_WIDTH + GATE_PAD + POOL_WIDTH
OFF_GATE = ATTN_WIDTH
OFF_U = OFF_GATE + GATE_PAD
KV_ROWS = 3 * KV_COLS
ROUTER_LANE0 = N_EXPERT_GROUPS
Q_TILE = 256
K_CHUNK = 256
VMEM_LIMIT = 56 * 1024 * 1024


def _slope(h):
    return float(2.0 ** (-8.0 * (h + 1) / N_HEADS))


def _half_group_norm(v, g):
    lane = lax.broadcasted_iota(jnp.int32, v.shape, 1)
    lo = lane < HEAD_DIM
    v2 = v * v
    s_lo = jnp.sum(jnp.where(lo, v2, 0.0), axis=-1, keepdims=True)
    s_hi = jnp.sum(jnp.where(lo, 0.0, v2), axis=-1, keepdims=True)
    r_lo = lax.rsqrt(s_lo * (1.0 / HEAD_DIM) + EPS)
    r_hi = lax.rsqrt(s_hi * (1.0 / HEAD_DIM) + EPS)
    return (v * jnp.where(lo, r_lo, r_hi)) * g


def _dot_t(a, b):
    return lax.dot_general(a, b, (((1,), (1,)), ((), ())), preferred_element_type=F32)


def _proj_kernel(x_ref, g1_ref, wr_ref, wkv_ref, qg_ref, kg_ref,
                 q_ref, gates_ref, u_ref, kvc_ref, kvs_ref, kvw_ref, ks16_ref, vsx_ref, kw16_ref, vwx_ref):
    x = x_ref[0]
    tm = x.shape[0]
    ms = jnp.mean(x * x, axis=-1, keepdims=True)
    n = ((x * lax.rsqrt(ms + EPS)) * g1_ref[...]).astype(BF16)
    p = jnp.dot(n, wr_ref[...], preferred_element_type=F32)
    pt = _dot_t(wkv_ref[...], n)
    for t in range(ATTN_WIDTH // LANES):
        sl = slice(t * LANES, (t + 1) * LANES)
        q_ref[0, :, sl] = _half_group_norm(p[:, sl], qg_ref[...]).astype(BF16)
    gates_ref[0] = 1.0 / (1.0 + jnp.exp(-p[:, OFF_GATE:OFF_GATE + GATE_PAD]))
    u_ref[0] = p[:, OFF_U:OFF_U + POOL_WIDTH]
    kvc_ref[0] = pt[:KV_COLS]
    half = KV_COLS // 2
    lane = lax.broadcasted_iota(jnp.int32, (tm, LANES), 1)
    for bi, (out32, k16, vx16) in enumerate(((kvs_ref, ks16_ref, vsx_ref), (kvw_ref, kw16_ref, vwx_ref))):
        off = (bi + 1) * KV_COLS
        heads = []
        for hh in range(N_KV_HEADS):
            kh = pt[off + hh * HEAD_DIM:off + (hh + 1) * HEAD_DIM]
            msk = jnp.mean(kh * kh, axis=0, keepdims=True)
            heads.append((kh * lax.rsqrt(msk + EPS)) * kg_ref[bi])
        kn = jnp.concatenate(heads, axis=0)
        v = pt[off + half:off + KV_COLS]
        out32[0, :half, :] = kn
        out32[0, half:, :] = v
        vt = v.T
        vx = (jnp.where(lane < HEAD_DIM, vt, 1.0), jnp.where(lane < HEAD_DIM, pltpu.roll(vt, HEAD_DIM, axis=1), 1.0))
        for c in range(tm // K_CHUNK):
            cs = slice(c * K_CHUNK, (c + 1) * K_CHUNK)
            k16[0, c] = kn[:, cs].astype(BF16)
            for hh in range(N_KV_HEADS):
                vx16[0, c, hh] = vx[hh][cs].astype(BF16)


def _proj(x3d, g1, w_row, w_kv, qg, kg, tm):
    b, s, _ = x3d.shape
    tok = lambda i, j: (i, j, 0)
    feat = lambda i, j: (i, 0, j)
    const2 = lambda i, j: (0, 0)
    nck = tm // K_CHUNK
    out_shape = (
        jax.ShapeDtypeStruct((b, s, ATTN_WIDTH), BF16),
        jax.ShapeDtypeStruct((b, s, GATE_PAD), F32),
        jax.ShapeDtypeStruct((b, s, POOL_WIDTH), F32),
        jax.ShapeDtypeStruct((b, KV_COLS, s), F32),
        jax.ShapeDtypeStruct((b, KV_COLS, s), F32),
        jax.ShapeDtypeStruct((b, KV_COLS, s), F32),
        jax.ShapeDtypeStruct((b, s // K_CHUNK, LANES, K_CHUNK), BF16),
        jax.ShapeDtypeStruct((b, s // K_CHUNK, N_KV_HEADS, K_CHUNK, LANES), BF16),
        jax.ShapeDtypeStruct((b, s // K_CHUNK, LANES, K_CHUNK), BF16),
        jax.ShapeDtypeStruct((b, s // K_CHUNK, N_KV_HEADS, K_CHUNK, LANES), BF16),
    )
    k_spec = pl.BlockSpec((1, nck, LANES, K_CHUNK), lambda i, j: (i, j, 0, 0))
    v_spec = pl.BlockSpec((1, nck, N_KV_HEADS, K_CHUNK, LANES), lambda i, j: (i, j, 0, 0, 0))
    out_specs = (
        pl.BlockSpec((1, tm, ATTN_WIDTH), tok),
        pl.BlockSpec((1, tm, GATE_PAD), tok),
        pl.BlockSpec((1, tm, POOL_WIDTH), tok),
        pl.BlockSpec((1, KV_COLS, tm), feat),
        pl.BlockSpec((1, KV_COLS, tm), feat),
        pl.BlockSpec((1, KV_COLS, tm), feat),
        k_spec, v_spec, k_spec, v_spec,
    )
    return pl.pallas_call(
        _proj_kernel,
        grid=(b, s // tm),
        in_specs=[
            pl.BlockSpec((1, tm, D_MODEL), tok),
            pl.BlockSpec((1, D_MODEL), const2),
            pl.BlockSpec((D_MODEL, ROW_COLS), const2),
            pl.BlockSpec((KV_ROWS, D_MODEL), const2),
            pl.BlockSpec((1, LANES), const2),
            pl.BlockSpec((2, HEAD_DIM, 1), lambda i, j: (0, 0, 0)),
        ],
        out_specs=out_specs,
        out_shape=out_shape,
        compiler_params=pltpu.CompilerParams(
            dimension_semantics=("arbitrary", "arbitrary"), vmem_limit_bytes=VMEM_LIMIT),
        name="proj",
    )(x3d, g1, w_row, w_kv, qg, kg)


def _pool_kernel(u_ref, pw_ref, ps_ref, m_ref, *, seg):
    u = u_ref[0]
    n = u.shape[0]

    def shift(v, k):
        rolled = pltpu.roll(v, k, axis=0)
        r = lax.broadcasted_iota(jnp.int32, v.shape, 0) % seg
        return jnp.where(r >= k, rolled, 0.0)

    sums = []
    s = u
    k = 1
    for gi in range(len(POOL_WINDOWS)):
        s = s + shift(s, k)
        k *= 2
        sums.append(s[:, :POOL_GROUP_WIDTH])
        s = s[:, POOL_GROUP_WIDTH:]
    row = lax.broadcasted_iota(jnp.int32, (n, 1), 0) % seg
    for gi, w in enumerate(POOL_WINDOWS):
        sl = slice(gi * POOL_GROUP_WIDTH, (gi + 1) * POOL_GROUP_WIDTH)
        cnt = jnp.minimum(row + 1, w).astype(F32)
        d = sums[gi] / cnt - u[:, sl]
        y = jnp.dot(d.astype(BF16), pw_ref[gi], preferred_element_type=F32)
        m_ref[0, :, sl] = (y * ps_ref[:, sl]).astype(BF16)


def _pool(u3d, pool_w16, pool_scale, seg):
    b, n, _ = u3d.shape
    assert n % seg == 0
    return pl.pallas_call(
        functools.partial(_pool_kernel, seg=seg),
        grid=(b,),
        in_specs=[
            pl.BlockSpec((1, n, POOL_WIDTH), lambda i: (i, 0, 0)),
            pl.BlockSpec((len(POOL_WINDOWS), POOL_GROUP_WIDTH, POOL_GROUP_WIDTH), lambda i: (0, 0, 0)),
            pl.BlockSpec((1, POOL_WIDTH), lambda i: (0, 0)),
        ],
        out_specs=pl.BlockSpec((1, n, POOL_WIDTH), lambda i: (i, 0, 0)),
        out_shape=jax.ShapeDtypeStruct((b, n, POOL_WIDTH), BF16),
        compiler_params=pltpu.CompilerParams(
            dimension_semantics=("arbitrary",), vmem_limit_bytes=VMEM_LIMIT),
        name="pool",
    )(u3d, pool_w16, pool_scale)


def _compress_kernel(pt_ref, src_ref, tail_ref, pos_ref, bd_ref, g_ref, kc_ref, vc_ref, buf, sem,
                     *, n_pages, npp, has_tail, paged):
    b = pl.program_id(0)
    nb = pl.num_programs(0)
    n_slabs = 2 * N_KV_HEADS

    def slab_copy(row, p, c, slot):
        kv, kvh = divmod(c, N_KV_HEADS)
        if paged:
            src = src_ref.at[pt_ref[row, p], pl.ds(c * HEAD_DIM, HEAD_DIM), :]
        else:
            src = src_ref.at[row, pl.ds(c * HEAD_DIM, HEAD_DIM), pl.ds(p * PAGE, PAGE)]
        return pltpu.make_async_copy(src, buf.at[slot, kv, :, kvh * npp + p, :], sem.at[slot])

    def tail_copy(c, slot):
        kv, kvh = divmod(c, N_KV_HEADS)
        return pltpu.make_async_copy(tail_ref.at[0, pl.ds(c * HEAD_DIM, HEAD_DIM), :],
                                     buf.at[slot, kv, :, kvh * npp + n_pages, :], sem.at[slot])

    def row_copies(row, slot, fn):
        def body(p, carry):
            for c in range(n_slabs):
                fn(slab_copy(row, p, c, slot))
            return carry
        lax.fori_loop(0, n_pages, body, 0)

    n_real = n_pages + (1 if has_tail else 0)

    @pl.when(b == 0)
    def _():
        if npp > n_real:
            for kvh in range(N_KV_HEADS):
                buf[:, :, :, kvh * npp + n_real:(kvh + 1) * npp, :] = jnp.zeros(
                    (2, 2, HEAD_DIM, npp - n_real, LANES), F32)
        row_copies(0, 0, lambda cp: cp.start())

    slot = b % 2

    if has_tail:
        for c in range(n_slabs):
            tail_copy(c, slot).start()
        for c in range(n_slabs):
            tail_copy(c, slot).wait()
    row_copies(b, slot, lambda cp: cp.wait())

    rows = N_KV_HEADS * npp
    d_per_it = 8
    n_it = HEAD_DIM // d_per_it
    pages_per_it = n_pages // n_it
    nxt = jnp.minimum(b + 1, nb - 1)

    def body(it, acc):
        for pp in range(pages_per_it):
            for c in range(n_slabs):
                slab_copy(nxt, it * pages_per_it + pp, c, 1 - slot).start()
        for dd in range(d_per_it):
            d = it * d_per_it + dd
            pos = pos_ref[pl.ds(d, 1), :]
            x = jnp.concatenate([buf[slot, 0, d] + pos, buf[slot, 1, d] + pos], axis=1).astype(BF16)
            acc = acc + jnp.dot(x, bd_ref[d], preferred_element_type=F32)
        return acc

    acc = lax.fori_loop(0, n_it, body, jnp.zeros((rows, 2 * LANES), F32))
    kc_ref[0] = _half_group_norm(acc[:, :LANES], g_ref[...])
    vc_ref[0] = acc[:, LANES:]

    @pl.when(b == nb - 1)
    def _():
        row_copies(nxt, 1 - slot, lambda cp: cp.wait())


def _compress(page_table, src, tail, pos_t, bd, g2, *, npp, has_tail, paged):
    b, n_pages = page_table.shape
    kern = functools.partial(_compress_kernel, n_pages=n_pages, npp=npp, has_tail=has_tail, paged=paged)
    rows = N_KV_HEADS * npp
    grid_spec = pltpu.PrefetchScalarGridSpec(
        num_scalar_prefetch=1,
        grid=(b,),
        in_specs=[
            pl.BlockSpec(memory_space=pl.ANY),
            pl.BlockSpec((1, KV_COLS, LANES), lambda i, pt: (i, 0, 0)),
            pl.BlockSpec((HEAD_DIM, LANES), lambda i, pt: (0, 0)),
            pl.BlockSpec((HEAD_DIM, 2 * LANES, 2 * LANES), lambda i, pt: (0, 0, 0)),
            pl.BlockSpec((1, LANES), lambda i, pt: (0, 0)),
        ],
        out_specs=(pl.BlockSpec((1, rows, LANES), lambda i, pt: (i, 0, 0)),
                   pl.BlockSpec((1, rows, LANES), lambda i, pt: (i, 0, 0))),
        scratch_shapes=[
            pltpu.VMEM((2, 2, HEAD_DIM, rows, LANES), F32),
            pltpu.SemaphoreType.DMA((2,)),
        ],
    )
    return pl.pallas_call(
        kern,
        grid_spec=grid_spec,
        out_shape=(jax.ShapeDtypeStruct((b, rows, LANES), F32), jax.ShapeDtypeStruct((b, rows, LANES), F32)),
        compiler_params=pltpu.CompilerParams(
            dimension_semantics=("arbitrary",), vmem_limit_bytes=VMEM_LIMIT),
        name="compress",
    )(page_table, src, tail, pos_t, bd, g2)


def _to_half(tile, src_half, dst_half):
    lane = lax.broadcasted_iota(jnp.int32, tile.shape, 1)
    src = tile if src_half == dst_half else pltpu.roll(tile, HEAD_DIM, axis=1)
    keep = (lane < HEAD_DIM) if dst_half == 0 else (lane >= HEAD_DIM)
    return jnp.where(keep, src, 0.0)


def _pair_tile(o_even, o_odd, half):
    lane = lax.broadcasted_iota(jnp.int32, o_even.shape, 1)
    if half == 0:
        return jnp.where(lane < HEAD_DIM, o_even, pltpu.roll(o_odd, HEAD_DIM, axis=1))
    return jnp.where(lane < HEAD_DIM, pltpu.roll(o_even, HEAD_DIM, axis=1), o_odd)


def _gate_tile(gates, pair, c, shape):
    lane = lax.broadcasted_iota(jnp.int32, shape, 1)
    he, ho = 2 * pair, 2 * pair + 1
    return jnp.where(lane < HEAD_DIM, gates[:, 3 * he + c:3 * he + c + 1], gates[:, 3 * ho + c:3 * ho + c + 1])


def _block_of_col(col, npp):
    return 2 * (col % npp) + col // npp


def _cmp_operand(x, npp):
    return jnp.concatenate([_to_half(x, 0, 0), _to_half(x, 1, 0)], axis=0).astype(BF16)


def _rank_select(score_t, blk_t, cols):
    rank = jnp.zeros(score_t.shape, jnp.int32)
    for r, n in cols:
        row = score_t[r:r + 1, :]
        ahead = (row > score_t) | ((row == score_t) & (blk_t > n))
        rank = rank + ahead.astype(jnp.int32)
    return ((rank < TOP_K_BLOCKS) & (score_t > -0.5)).astype(F32)


POS_HI, POS_LO = HEAD_DIM, HEAD_DIM + 1
MASK_BIG = 1e30


def _attn_prompt_kernel(q_ref, gates_ref, kc_ref, vc_ref, ks_ref, vsx_ref, kw_ref, vwx_ref, exp_ref, gp_ref, a_ref,
                        zbuf, m_sc, acc_sc, *, n_pages, npp):
    i = pl.program_id(1)
    q0 = i * Q_TILE
    rows = Q_PER_KV * Q_TILE
    ncols = 2 * npp
    qpos = q0 + lax.broadcasted_iota(jnp.int32, (Q_TILE, 1), 0)
    blk = _block_of_col(lax.broadcasted_iota(jnp.int32, (Q_TILE, ncols), 1), npp)
    real_cols = [(half * n_pages + p, 2 * p + half) for half in range(2) for p in range(n_pages)]

    g = gates_ref[0]
    g_hi = g.astype(BF16)
    g_lo = (g - g_hi.astype(F32)).astype(BF16)
    gexp = (jnp.dot(g_hi, gp_ref[...], preferred_element_type=F32)
            + jnp.dot(g_lo, gp_ref[...], preferred_element_type=F32))

    r_key = lax.broadcasted_iota(jnp.int32, (Q_TILE, K_CHUNK), 1)
    r_qry = lax.broadcasted_iota(jnp.int32, (Q_TILE, K_CHUNK), 0)
    keep = {"causal": r_key <= r_qry, "lower": r_key > r_qry}

    all_rows = N_HEADS * Q_TILE

    def branch(lhs, k_ref, vx_ref, slc, chunks, c_lo, c_hi):
        m_sc[...] = jnp.full((all_rows, LANES), NEG, F32)

        def scores(c, kind):
            rr = lax.broadcasted_iota(jnp.int32, (HEAD_DIM, K_CHUNK), 0)
            tt = lax.broadcasted_iota(jnp.int32, (HEAD_DIM, K_CHUNK), 1).astype(F32)
            hi = ((c - i) * K_CHUNK).astype(F32)
            pos = jnp.where(rr == 0, hi, jnp.where(rr == 1, tt, 0.0)).astype(BF16)
            for kvh in range(N_KV_HEADS):
                parts = [k_ref[0, c, kvh * HEAD_DIM:(kvh + 1) * HEAD_DIM, :], pos]
                if slc:
                    parts.append(exp_ref[c])
                kx = jnp.concatenate(parts, axis=0)
                for gq in range(Q_PER_KV):
                    sl = slice((kvh * Q_PER_KV + gq) * Q_TILE, (kvh * Q_PER_KV + gq + 1) * Q_TILE)
                    zg = jnp.dot(lhs[sl], kx, preferred_element_type=F32)
                    if kind is not None:
                        zg = jnp.where(keep[kind], zg, NEG)
                    zbuf[c, sl, :] = zg
                    m_sc[sl] = jnp.maximum(m_sc[sl], jnp.maximum(zg[:, :LANES], zg[:, LANES:]))

        def in_pairs(lo, hi, one):
            n = hi - lo

            def body(k, carry):
                one(lo + 2 * k)
                one(lo + 2 * k + 1)
                return carry
            lax.fori_loop(0, n // 2, body, 0)
            pl.when(n % 2 == 1)(lambda: one(hi - 1))

        for group in chunks:
            if not isinstance(group[1], list):
                in_pairs(group[0], group[1], lambda c: scores(c, None))
            else:
                cond, items = group[0], group[1]

                def run(items=items):
                    for c, kind in items:
                        scores(c, kind)
                run() if cond is None else pl.when(cond)(run)

        m = jnp.broadcast_to(jnp.max(m_sc[...], axis=-1, keepdims=True), (all_rows, LANES))
        acc_sc[...] = jnp.zeros((all_rows, LANES), F32)

        def weigh(c):
            for h in range(N_HEADS):
                sl = slice(h * Q_TILE, (h + 1) * Q_TILE)
                z = zbuf[c, sl, :]
                e = jnp.concatenate([jnp.exp(z[:, :LANES] - m[sl]), jnp.exp(z[:, LANES:] - m[sl])], axis=1)
                acc_sc[sl] += jnp.dot(e.astype(BF16), vx_ref[0, c, h // Q_PER_KV], preferred_element_type=F32)

        in_pairs(c_lo, c_hi, weigh)
        return acc_sc[...]

    def pair_norm(acc_e, acc_o):
        lane = lax.broadcasted_iota(jnp.int32, acc_e.shape, 1)
        safe = lambda d: jnp.where(d > 0.0, d, 1.0)
        sw_e = pltpu.roll(acc_e, HEAD_DIM, axis=1)
        sw_o = pltpu.roll(acc_o, HEAD_DIM, axis=1)
        return jnp.where(lane < HEAD_DIM, acc_e / safe(sw_e), sw_o / safe(acc_o))

    o_cmp_all, lhs_slc_all, lhs_win_all = [], [], []
    for kvh in range(N_KV_HEADS):
        slopes = [_slope(kvh * Q_PER_KV + g) for g in range(Q_PER_KV)]
        lane = lax.broadcasted_iota(jnp.int32, (Q_TILE, LANES), 1)
        pieces, pieces_x = [], []
        for gq in range(Q_PER_KV):
            h = kvh * Q_PER_KV + gq
            tile = q_ref[0, :, (h // 2) * LANES:(h // 2 + 1) * LANES].astype(F32) * SCALE
            low = _to_half(tile, h % 2, 0)
            pieces.append(low)
            pieces_x.append(jnp.where((lane == POS_HI) | (lane == POS_LO), slopes[gq], low))
        qk = jnp.concatenate(pieces, axis=0).astype(BF16)
        qk_x = jnp.concatenate(pieces_x, axis=0).astype(BF16)

        kc = _cmp_operand(kc_ref[0, kvh], npp)
        vc = _cmp_operand(vc_ref[0, kvh], npp)
        s = _dot_t(qk, kc)
        dist_c = qpos - (blk * CMP_BLOCK + (CMP_BLOCK - 1))
        mask_c = dist_c >= 0
        dist_cf = dist_c.astype(F32)
        p_list = []
        p_kv = jnp.zeros((Q_TILE, ncols), F32)
        for g in range(Q_PER_KV):
            z = jnp.where(mask_c, s[g * Q_TILE:(g + 1) * Q_TILE] - slopes[g] * dist_cf, NEG)
            mx = jnp.max(z, axis=-1, keepdims=True)
            e = jnp.where(mask_c, jnp.exp(z - mx), 0.0)
            den = jnp.sum(e, axis=-1, keepdims=True)
            p = e / jnp.where(den > 0.0, den, 1.0)
            p_kv = p_kv + p
            p_list.append(p.astype(BF16))
        o_cmp = jnp.dot(jnp.concatenate(p_list, axis=0), vc, preferred_element_type=F32)

        cand = blk * CMP_BLOCK <= qpos
        forced = (blk == qpos // CMP_BLOCK) | (blk == 0)
        score = jnp.where(forced, FORCE_SCORE, jnp.where(cand, p_kv, -1.0))
        score_t = score.T
        st = jnp.concatenate([score_t[:n_pages], score_t[npp:npp + n_pages]], axis=0)
        rr = lax.broadcasted_iota(jnp.int32, (2 * n_pages, Q_TILE), 0)
        bt = jnp.where(rr < n_pages, 2 * rr, 2 * (rr - n_pages) + 1)
        sel_s = _rank_select(st, bt, [(r, n) for r, (_, n) in enumerate(real_cols)])
        pad = jnp.zeros((npp - n_pages, Q_TILE), F32)
        sel_t = jnp.concatenate([sel_s[:n_pages], pad, sel_s[n_pages:], pad], axis=0)
        sel = sel_t.T
        sel_bias = jnp.where(sel > 0.5, 0.0, -MASK_BIG).astype(BF16)
        lhs_slc_all.append(jnp.concatenate([qk_x, jnp.concatenate([sel_bias] * Q_PER_KV, axis=0)], axis=1))
        lhs_win_all.append(qk_x)
        o_cmp_all.append(o_cmp)

    o_slc = branch(jnp.concatenate(lhs_slc_all, axis=0), ks_ref, vsx_ref, True,
                   [(0, i), (None, [(i, "causal")])], 0, i + 1)
    o_win = branch(jnp.concatenate(lhs_win_all, axis=0), kw_ref, vwx_ref, False,
                   [(i >= 2, [(i - 2, "lower"), (i - 1, None), (i, "causal")]),
                    (i == 1, [(0, None), (1, "causal")]),
                    (i == 0, [(0, "causal")])],
                   jnp.maximum(i - 2, 0), i + 1)
    o_cmp = jnp.concatenate(o_cmp_all, axis=0)

    for pair in range(N_HEADS // 2):
        e_sl = slice(2 * pair * Q_TILE, (2 * pair + 1) * Q_TILE)
        o_sl = slice((2 * pair + 1) * Q_TILE, (2 * pair + 2) * Q_TILE)
        col = pair * LANES
        tile = (gexp[:, col:col + LANES] * _pair_tile(o_cmp[e_sl], o_cmp[o_sl], 0)
                + gexp[:, ATTN_WIDTH + col:ATTN_WIDTH + col + LANES] * pair_norm(o_slc[e_sl], o_slc[o_sl])
                + gexp[:, 2 * ATTN_WIDTH + col:2 * ATTN_WIDTH + col + LANES] * pair_norm(o_win[e_sl], o_win[o_sl]))
        a_ref[0, :, pair * LANES:(pair + 1) * LANES] = tile.astype(BF16)


def _attn_prompt(q, gates, kc, vc, ks16, vsx, kw16, vwx, expand, gate_place, n_pages):
    b, s, _ = q.shape
    npp = kc.shape[2]
    n_chunks = s // K_CHUNK
    rows = Q_PER_KV * Q_TILE
    assert WINDOW == 2 * K_CHUNK and Q_TILE == K_CHUNK
    kern = functools.partial(_attn_prompt_kernel, n_pages=n_pages, npp=npp)
    per_b4 = lambda bi, i: (bi, 0, 0, 0)
    per_b5 = lambda bi, i: (bi, 0, 0, 0, 0)
    return pl.pallas_call(
        kern,
        grid=(b, s // Q_TILE),
        in_specs=[
            pl.BlockSpec((1, Q_TILE, ATTN_WIDTH), lambda bi, i: (bi, i, 0)),
            pl.BlockSpec((1, Q_TILE, GATE_PAD), lambda bi, i: (bi, i, 0)),
            pl.BlockSpec((1, N_KV_HEADS, npp, LANES), per_b4),
            pl.BlockSpec((1, N_KV_HEADS, npp, LANES), per_b4),
            pl.BlockSpec((1, n_chunks, LANES, K_CHUNK), per_b4),
            pl.BlockSpec((1, n_chunks, N_KV_HEADS, K_CHUNK, LANES), per_b5),
            pl.BlockSpec((1, n_chunks, LANES, K_CHUNK), per_b4),
            pl.BlockSpec((1, n_chunks, N_KV_HEADS, K_CHUNK, LANES), per_b5),
            pl.BlockSpec((n_chunks, 2 * npp, K_CHUNK), lambda bi, i: (0, 0, 0)),
            pl.BlockSpec((GATE_PAD, 3 * ATTN_WIDTH), lambda bi, i: (0, 0)),
        ],
        out_specs=pl.BlockSpec((1, Q_TILE, ATTN_WIDTH), lambda bi, i: (bi, i, 0)),
        out_shape=jax.ShapeDtypeStruct((b, s, ATTN_WIDTH), BF16),
        scratch_shapes=[
            pltpu.VMEM((n_chunks, N_KV_HEADS * rows, K_CHUNK), F32),
            pltpu.VMEM((N_KV_HEADS * rows, LANES), F32),
            pltpu.VMEM((N_KV_HEADS * rows, LANES), F32),
        ],
        compiler_params=pltpu.CompilerParams(
            dimension_semantics=("arbitrary", "arbitrary"), vmem_limit_bytes=VMEM_LIMIT),
        name="attn_prompt",
    )(q, gates, kc, vc, ks16, vsx, kw16, vwx, expand, gate_place)


S_ROWS = LANES
S_CHUNK = 1024


def _sample_rows(q_ref, ds, past):
    n_real = N_HEADS * ds
    by_kvh, low = [], []
    for h in range(N_HEADS):
        tile = q_ref[0, :, (h // 2) * LANES:(h // 2 + 1) * LANES].astype(F32) * SCALE
        by_kvh.append(_to_half(tile, h % 2, h // Q_PER_KV))
        low.append(_to_half(tile, h % 2, 0))
    zpad = jnp.zeros((S_ROWS - n_real, LANES), F32)
    qrows = jnp.concatenate(by_kvh + [zpad], axis=0).astype(BF16)
    qlow = jnp.concatenate(low + [zpad], axis=0).astype(BF16)
    r_col = lax.broadcasted_iota(jnp.int32, (S_ROWS, 1), 0)
    q_of_r = r_col % ds
    h_of_r = r_col // ds
    sl_r = jnp.zeros((S_ROWS, 1), F32)
    for h in range(N_HEADS):
        sl_r = jnp.where(h_of_r == h, _slope(h), sl_r)
    return qrows, qlow, q_of_r, past + q_of_r, sl_r, h_of_r // Q_PER_KV


def _select_sample_kernel(q_ref, kc_ref, vc_ref, ocmp_ref, sel_ref, need_ref, *, past, ds, npp):
    n_real = N_HEADS * ds
    ncols = 2 * npp
    n_blocks = (past + ds + CMP_BLOCK - 1) // CMP_BLOCK
    _, qlow, _, qp_r, sl_r, kvh_r = _sample_rows(q_ref, ds, past)

    blk = _block_of_col(lax.broadcasted_iota(jnp.int32, (S_ROWS, ncols), 1), npp)
    s = jnp.where(kvh_r == 0, _dot_t(qlow, _cmp_operand(kc_ref[0, 0], npp)),
                  _dot_t(qlow, _cmp_operand(kc_ref[0, 1], npp)))
    dist_c = qp_r - (blk * CMP_BLOCK + (CMP_BLOCK - 1))
    mask_c = dist_c >= 0
    z = jnp.where(mask_c, s - sl_r * dist_c.astype(F32), NEG)
    mx = jnp.max(z, axis=-1, keepdims=True)
    e = jnp.where(mask_c, jnp.exp(z - mx), 0.0)
    den = jnp.sum(e, axis=-1, keepdims=True)
    p = e / jnp.where(den > 0.0, den, 1.0)
    pb = p.astype(BF16)
    o_cmp = jnp.where(kvh_r == 0, jnp.dot(pb, _cmp_operand(vc_ref[0, 0], npp), preferred_element_type=F32),
                      jnp.dot(pb, _cmp_operand(vc_ref[0, 1], npp), preferred_element_type=F32))

    kq = N_KV_HEADS * ds
    p_kv = []
    for kvh in range(N_KV_HEADS):
        acc = jnp.zeros((ds, ncols), F32)
        for g in range(Q_PER_KV):
            r0 = (kvh * Q_PER_KV + g) * ds
            acc = acc + p[r0:r0 + ds]
        p_kv.append(acc)
    p_kv = jnp.concatenate(p_kv, axis=0)
    blk2 = _block_of_col(lax.broadcasted_iota(jnp.int32, (kq, ncols), 1), npp)
    qp2 = past + lax.broadcasted_iota(jnp.int32, (kq, 1), 0) % ds
    cand = blk2 * CMP_BLOCK <= qp2
    forced = (blk2 == qp2 // CMP_BLOCK) | (blk2 == 0)
    score = jnp.where(forced, FORCE_SCORE, jnp.where(cand, p_kv, -1.0))
    rank = jnp.zeros((kq, ncols), jnp.int32)
    for n in range(n_blocks):
        c = (n % 2) * npp + n // 2
        col = score[:, c:c + 1]
        ahead = (col > score) | ((col == score) & (blk2 > n))
        rank = rank + ahead.astype(jnp.int32)
    sel2 = ((rank < TOP_K_BLOCKS) & (score > -0.5)).astype(F32)
    sel_rows = []
    for h in range(N_HEADS):
        kvh = h // Q_PER_KV
        sel_rows.append(sel2[kvh * ds:(kvh + 1) * ds])
    sel_rows.append(jnp.zeros((S_ROWS - n_real, ncols), F32))
    sel_rows = jnp.concatenate(sel_rows, axis=0)
    ocmp_ref[0] = o_cmp
    sel_ref[0] = sel_rows
    any_row = jnp.max(sel_rows, axis=0, keepdims=True)
    page_need = jnp.maximum(any_row[:, :npp], any_row[:, npp:])
    need_ref[0] = jnp.broadcast_to(page_need, (8, npp))


def _select_sample(q, kc, vc, *, past):
    b, ds, _ = q.shape
    npp = kc.shape[2]
    kern = functools.partial(_select_sample_kernel, past=past, ds=ds, npp=npp)
    per_b3 = lambda i: (i, 0, 0)
    per_b4 = lambda i: (i, 0, 0, 0)
    return pl.pallas_call(
        kern,
        grid=(b,),
        in_specs=[pl.BlockSpec((1, ds, ATTN_WIDTH), per_b3),
                  pl.BlockSpec((1, N_KV_HEADS, npp, LANES), per_b4),
                  pl.BlockSpec((1, N_KV_HEADS, npp, LANES), per_b4)],
        out_specs=(pl.BlockSpec((1, S_ROWS, LANES), per_b3), pl.BlockSpec((1, S_ROWS, 2 * npp), per_b3),
                   pl.BlockSpec((1, 8, npp), per_b3)),
        out_shape=(jax.ShapeDtypeStruct((b, S_ROWS, LANES), F32), jax.ShapeDtypeStruct((b, S_ROWS, 2 * npp), F32),
                   jax.ShapeDtypeStruct((b, 8, npp), F32)),
        compiler_params=pltpu.CompilerParams(dimension_semantics=("arbitrary",), vmem_limit_bytes=VMEM_LIMIT),
        name="select_sample",
    )(q, kc, vc)


def _attn_sample_kernel(pt_ref, order_ref, nn_ref, q_ref, gates_ref, ocmp_ref, sel_ref, kvs_new_ref, kvw_new_ref,
                        win_ref, cache_ref, a_ref, kbuf, zbuf, sem, *, n_pages, past, ds, npp):
    b = pl.program_id(0)
    nb = pl.num_programs(0)
    ncols = 2 * npp
    half_cols = KV_COLS // 2
    pages_per_chunk = S_CHUNK // PAGE

    def page_copy(row, j, slot):
        return pltpu.make_async_copy(
            cache_ref.at[pt_ref[row, order_ref[row, j]]], kbuf.at[slot, :, pl.ds(j * PAGE, PAGE)], sem.at[slot])

    def row_copies(row, slot, fn):
        def body(j, carry):
            fn(page_copy(row, j, slot))
            return carry
        lax.fori_loop(0, nn_ref[row], body, 0)

    @pl.when(b == 0)
    def _():
        kbuf[...] = jnp.zeros(kbuf.shape, F32)
        row_copies(0, 0, lambda cp: cp.start())

    slot = b % 2

    @pl.when(b + 1 < nb)
    def _():
        row_copies(b + 1, 1 - slot, lambda cp: cp.start())

    qrows, _, q_of_r, qp_r, sl_r, _ = _sample_rows(q_ref, ds, past)
    o_cmp = ocmp_ref[0]
    sel_rows = sel_ref[0]
    sel16 = sel_rows.astype(BF16)

    new_lane = lax.broadcasted_iota(jnp.int32, (1, LANES), 1)

    def new_scores(ref, extra_mask):
        k_new = ref[0, :half_cols, :].astype(BF16)
        s_new = jnp.dot(qrows, k_new, preferred_element_type=F32)
        dist = q_of_r - new_lane
        mask = (new_lane < ds) & (dist >= 0) & extra_mask
        return jnp.where(mask, s_new - sl_r * dist.astype(F32), NEG)

    def weighted_new(ref, e_new):
        return _dot_t(e_new.astype(BF16), ref[0, half_cols:, :].astype(BF16))

    row_copies(b, slot, lambda cp: cp.wait())
    n_need = nn_ref[b]
    n_chunks = (n_need + pages_per_chunk - 1) // pages_per_chunk

    def pass1(c, mrow):
        k0 = pl.multiple_of(c * S_CHUNK, S_CHUNK)
        kt = kbuf[slot, :half_cols, pl.ds(k0, S_CHUNK)].astype(BF16)
        st = jnp.dot(qrows, kt, preferred_element_type=F32)
        lane = lax.broadcasted_iota(jnp.int32, (1, S_CHUNK), 1)
        tpos = jnp.full((1, S_CHUNK), -PAGE * n_pages, jnp.int32)
        for jj in range(pages_per_chunk):
            j = c * pages_per_chunk + jj
            page = order_ref[b, jnp.minimum(j, n_pages - 1)]
            base = jnp.where(j < n_need, page * PAGE, -PAGE * n_pages) - jj * PAGE
            tpos = jnp.where(lane // PAGE == jj, base + lane, tpos)
        blk_c = _block_of_col(lax.broadcasted_iota(jnp.int32, (ncols, S_CHUNK), 0), npp)
        expand = (blk_c == jnp.broadcast_to(tpos, (ncols, S_CHUNK)) // CMP_BLOCK).astype(BF16)
        chosen = jnp.dot(sel16, expand, preferred_element_type=F32) > 0.5
        dist = qp_r - tpos
        zc = jnp.where(chosen & (dist >= 0), st - sl_r * dist.astype(F32), NEG)
        zbuf[c] = zc
        return jnp.maximum(mrow, jnp.max(zc, axis=-1, keepdims=True))

    mrow = lax.fori_loop(0, n_chunks, pass1, jnp.full((S_ROWS, 1), NEG, F32))
    c_last = ((past // CMP_BLOCK) % 2) * npp + (past // CMP_BLOCK) // 2
    z_new = new_scores(kvs_new_ref, sel_rows[:, c_last:c_last + 1] > 0.5)
    m_r = jnp.maximum(mrow, jnp.max(z_new, axis=-1, keepdims=True))

    def pass2(c, carry):
        acc, den_r = carry
        k0 = pl.multiple_of(c * S_CHUNK, S_CHUNK)
        zc = zbuf[c]
        ec = jnp.where(zc > 0.5 * NEG, jnp.exp(zc - m_r), 0.0)
        vt = kbuf[slot, half_cols:, pl.ds(k0, S_CHUNK)].astype(BF16)
        return acc + _dot_t(ec.astype(BF16), vt), den_r + jnp.sum(ec, axis=-1, keepdims=True)

    e_new = jnp.where(z_new > 0.5 * NEG, jnp.exp(z_new - m_r), 0.0)
    acc, den_r = lax.fori_loop(0, n_chunks, pass2,
                               (weighted_new(kvs_new_ref, e_new), jnp.sum(e_new, axis=-1, keepdims=True)))
    o_slc = acc / jnp.where(den_r > 0.0, den_r, 1.0)

    win_buf = win_ref.shape[2]
    kw = win_ref[0, :half_cols, :].astype(BF16)
    st = jnp.dot(qrows, kw, preferred_element_type=F32)
    kpos = past - win_buf + lax.broadcasted_iota(jnp.int32, (1, win_buf), 1)
    dist = qp_r - kpos
    mask = (dist >= 0) & (dist < WINDOW) & (kpos >= 0)
    z_w = jnp.where(mask, st - sl_r * dist.astype(F32), NEG)
    zw_new = new_scores(kvw_new_ref, True)
    m_w = jnp.maximum(jnp.max(z_w, axis=-1, keepdims=True), jnp.max(zw_new, axis=-1, keepdims=True))
    e_w = jnp.where(z_w > 0.5 * NEG, jnp.exp(z_w - m_w), 0.0)
    ew_new = jnp.where(zw_new > 0.5 * NEG, jnp.exp(zw_new - m_w), 0.0)
    acc_w = _dot_t(e_w.astype(BF16), win_ref[0, half_cols:, :].astype(BF16)) + weighted_new(kvw_new_ref, ew_new)
    den_w = jnp.sum(e_w, axis=-1, keepdims=True) + jnp.sum(ew_new, axis=-1, keepdims=True)
    o_win = acc_w / jnp.where(den_w > 0.0, den_w, 1.0)

    gates = gates_ref[0]
    for pair in range(N_HEADS // 2):
        kvh = (2 * pair) // Q_PER_KV
        e_sl = slice(2 * pair * ds, (2 * pair + 1) * ds)
        o_sl = slice((2 * pair + 1) * ds, (2 * pair + 2) * ds)
        shape = (ds, LANES)
        tile = (_gate_tile(gates, pair, 0, shape) * _pair_tile(o_cmp[e_sl], o_cmp[o_sl], 0)
                + _gate_tile(gates, pair, 1, shape) * _pair_tile(o_slc[e_sl], o_slc[o_sl], kvh)
                + _gate_tile(gates, pair, 2, shape) * _pair_tile(o_win[e_sl], o_win[o_sl], kvh))
        a_ref[0, :, pair * LANES:(pair + 1) * LANES] = tile.astype(BF16)


def _attn_sample(page_table, order, n_need, q, gates, o_cmp, sel_rows, kvs_new, kvw_new, state_win_t, cache_pages,
                 *, past):
    b, n_pages = page_table.shape
    ds = q.shape[1]
    npp = sel_rows.shape[2] // 2
    win_buf = state_win_t.shape[2]
    n_keys = n_pages * PAGE
    kern = functools.partial(_attn_sample_kernel, n_pages=n_pages, past=past, ds=ds, npp=npp)
    per_b = lambda i, pt, od, nn: (i, 0, 0)
    grid_spec = pltpu.PrefetchScalarGridSpec(
        num_scalar_prefetch=3,
        grid=(b,),
        in_specs=[
            pl.BlockSpec((1, ds, ATTN_WIDTH), per_b),
            pl.BlockSpec((1, ds, GATE_PAD), per_b),
            pl.BlockSpec((1, S_ROWS, LANES), per_b),
            pl.BlockSpec((1, S_ROWS, 2 * npp), per_b),
            pl.BlockSpec((1, KV_COLS, LANES), per_b),
            pl.BlockSpec((1, KV_COLS, LANES), per_b),
            pl.BlockSpec((1, KV_COLS, win_buf), per_b),
            pl.BlockSpec(memory_space=pl.ANY),
        ],
        out_specs=pl.BlockSpec((1, ds, ATTN_WIDTH), per_b),
        scratch_shapes=[
            pltpu.VMEM((2, KV_COLS, n_keys), F32),
            pltpu.VMEM((n_keys // S_CHUNK, S_ROWS, S_CHUNK), F32),
            pltpu.SemaphoreType.DMA((2,)),
        ],
    )
    return pl.pallas_call(
        kern,
        grid_spec=grid_spec,
        out_shape=jax.ShapeDtypeStruct((b, ds, ATTN_WIDTH), BF16),
        compiler_params=pltpu.CompilerParams(
            dimension_semantics=("arbitrary",), vmem_limit_bytes=VMEM_LIMIT),
        name="attn_sample",
    )(page_table, order, n_need, q, gates, o_cmp, sel_rows, kvs_new, kvw_new, state_win_t, cache_pages)


X_TILES = D_MODEL // LANES
PAY_SUB = X_TILES + 1
GROUP_LANE = 0
ROUTER_ROWS = 32
RANK_LANE = 1
PLAN_COLS = 8


def _rows_to_tiles(ref, lo, val):
    for k in range(val.shape[1] // LANES):
        ref[:, lo + k, :] = val[:, k * LANES:(k + 1) * LANES]


def _tiles_to_rows(ref, lo, n):
    return jnp.concatenate([ref[:, lo + k, :] for k in range(n)], axis=1)


def _ffn_kernel(x_ref, a_ref, m_ref, wo_ref, g2_ref, wr_ref, tri_ref, *refs, packed):
    out_refs, carry = refs[:-1], refs[-1]
    h = (x_ref[...]
         + jnp.dot(a_ref[...], wo_ref[:ATTN_WIDTH, :], preferred_element_type=F32)
         + jnp.dot(m_ref[...], wo_ref[ATTN_WIDTH:, :], preferred_element_type=F32))
    ms = jnp.mean(h * h, axis=-1, keepdims=True)
    n2f = (h * lax.rsqrt(ms + EPS)) * g2_ref[...]
    n2 = n2f.astype(BF16)
    tm = n2.shape[0]
    logits = _dot_t(wr_ref[...], n2)[:ROUTER_ROWS]
    row = lax.broadcasted_iota(jnp.int32, (ROUTER_ROWS, tm), 0)
    big = jnp.int32(LANES)

    def masked_softmax(mask):
        zz = jnp.where(mask, logits, NEG)
        mx = jnp.max(zz, axis=0, keepdims=True)
        ee = jnp.where(mask, jnp.exp(zz - mx), 0.0)
        return ee / jnp.sum(ee, axis=0, keepdims=True)

    def first_argmax(vals, mask):
        v = jnp.max(jnp.where(mask, vals, -1.0), axis=0, keepdims=True)
        idx = jnp.min(jnp.where(mask & (vals == v), row, big), axis=0, keepdims=True)
        return v, idx

    is_g = row < N_EXPERT_GROUPS
    pg = masked_softmax(is_g)
    g_val, g_idx = first_argmax(pg, is_g)
    e_row = row - ROUTER_LANE0
    in_grp = (e_row >= 0) & (e_row < N_EXPERTS) & (e_row // EXPERTS_PER_GROUP == g_idx)
    pe = masked_softmax(in_grp)
    v1, i1 = first_argmax(pe, in_grp)
    rest = in_grp & (row != i1)
    v2, i2 = first_argmax(pe, rest)
    scale = g_val / (v1 + v2)
    gate_t = jnp.where(row == i1, v1 * scale, jnp.where(row == i2, v2 * scale, 0.0))
    gate_t = jnp.where(row == GROUP_LANE, g_idx.astype(F32), gate_t)
    if packed:
        @pl.when(pl.program_id(0) == 0)
        def _():
            carry[...] = jnp.zeros(carry.shape, F32)

        onehot_t = jnp.where(is_g & (row == g_idx), 1.0, 0.0)
        before = jnp.dot(onehot_t.astype(BF16), tri_ref[...], preferred_element_type=F32) + carry[:, :1]
        rank = jnp.sum(onehot_t * before, axis=0, keepdims=True)
        gate_t = jnp.where(row == RANK_LANE, rank, gate_t)
        carry[...] = carry[...] + jnp.sum(onehot_t, axis=1, keepdims=True)
    gate = jnp.concatenate([gate_t, jnp.zeros((LANES - ROUTER_ROWS, tm), F32)], axis=0).T
    if packed:
        h_ref, pay_ref, plan_ref, cnt_ref = out_refs
        h_ref[...] = h
        plan_ref[...] = gate[:, :PLAN_COLS]
        cnt_ref[...] = carry[:8, :]
        _rows_to_tiles(pay_ref, 0, n2f)
        pay_ref[:, X_TILES, :] = gate
    else:
        h_ref, n2_ref, gate_ref = out_refs
        h_ref[...] = h
        n2_ref[...] = n2
        gate_ref[...] = gate


def _ffn(x2d, a2d, m2d, w_out16, g2, w_router16, tm, packed):
    t = x2d.shape[0]
    row = lambda i: (i, 0)
    const = lambda i: (0, 0)
    if packed:
        out_specs = (pl.BlockSpec((tm, D_MODEL), row), pl.BlockSpec((tm, PAY_SUB, LANES), lambda i: (i, 0, 0)),
                     pl.BlockSpec((tm, PLAN_COLS), row), pl.BlockSpec((8, LANES), const))
        out_shape = (jax.ShapeDtypeStruct((t, D_MODEL), F32), jax.ShapeDtypeStruct((t, PAY_SUB, LANES), F32),
                     jax.ShapeDtypeStruct((t, PLAN_COLS), F32), jax.ShapeDtypeStruct((8, LANES), F32))
    else:
        out_specs = (pl.BlockSpec((tm, D_MODEL), row), pl.BlockSpec((tm, D_MODEL), row),
                     pl.BlockSpec((tm, LANES), row))
        out_shape = (jax.ShapeDtypeStruct((t, D_MODEL), F32), jax.ShapeDtypeStruct((t, D_MODEL), BF16),
                     jax.ShapeDtypeStruct((t, LANES), F32))
    return pl.pallas_call(
        functools.partial(_ffn_kernel, packed=packed),
        grid=(t // tm,),
        in_specs=[
            pl.BlockSpec((tm, D_MODEL), row),
            pl.BlockSpec((tm, ATTN_WIDTH), row),
            pl.BlockSpec((tm, POOL_WIDTH), row),
            pl.BlockSpec((D_MODEL, D_MODEL), const),
            pl.BlockSpec((1, D_MODEL), const),
            pl.BlockSpec((LANES, D_MODEL), const),
            pl.BlockSpec((tm, tm), const),
        ],
        out_specs=out_specs,
        out_shape=out_shape,
        scratch_shapes=[pltpu.VMEM((ROUTER_ROWS, LANES), F32)],
        compiler_params=pltpu.CompilerParams(
            dimension_semantics=("arbitrary",), vmem_limit_bytes=VMEM_LIMIT),
        name="ffn",
    )(x2d, a2d, m2d, w_out16, g2, w_router16, jnp.asarray(np.triu(np.ones((tm, tm), np.float32), 1), dtype=BF16))


MOE_ROWS = 512
ROW_CHUNK = 1024
GATHER_CHUNK = 512

def _scatter_kernel(meta_ref, slot_ref, pay_ref, xs_ref, zrow, sem, zsem):
    c = pl.program_id(0)

    @pl.when(c == 0)
    def _():
        zrow[...] = jnp.zeros(zrow.shape, F32)
        for g in range(N_EXPERT_GROUPS + 1):
            lo = meta_ref[g]
            hi = meta_ref[N_EXPERT_GROUPS + 1 + g]

            def zero_start(r, carry):
                pltpu.make_async_copy(zrow.at[pl.ds(0, 1)], xs_ref.at[pl.ds(r, 1)], zsem.at[0]).start()
                return carry

            def zero_wait(r, carry):
                pltpu.make_async_copy(zrow.at[pl.ds(0, 1)], xs_ref.at[pl.ds(0, 1)], zsem.at[0]).wait()
                return carry

            lax.fori_loop(lo, hi, zero_start, 0)
            lax.fori_loop(lo, hi, zero_wait, 0)

    def issue(u, carry):
        pltpu.make_async_copy(pay_ref.at[pl.ds(u, 1)], xs_ref.at[pl.ds(slot_ref[0, 0, u], 1)], sem.at[0]).start()
        return carry

    lax.fori_loop(0, ROW_CHUNK, issue, 0, unroll=8)
    pltpu.make_async_copy(pay_ref, xs_ref.at[pl.ds(0, ROW_CHUNK)], sem.at[0]).wait()


def _scatter(meta, slot3, pay, n_slots):
    t = pay.shape[0]
    return pl.pallas_call(
        _scatter_kernel,
        grid_spec=pltpu.PrefetchScalarGridSpec(
            num_scalar_prefetch=1,
            grid=(t // ROW_CHUNK,),
            in_specs=[pl.BlockSpec((1, 1, ROW_CHUNK), lambda c, meta: (c, 0, 0), memory_space=pltpu.SMEM),
                      pl.BlockSpec((ROW_CHUNK, PAY_SUB, LANES), lambda c, meta: (c, 0, 0))],
            out_specs=pl.BlockSpec(memory_space=pl.ANY),
            scratch_shapes=[pltpu.VMEM((1, PAY_SUB, LANES), F32), pltpu.SemaphoreType.DMA((1,)),
                            pltpu.SemaphoreType.DMA((1,))],
        ),
        out_shape=jax.ShapeDtypeStruct((n_slots, PAY_SUB, LANES), F32),
        compiler_params=pltpu.CompilerParams(dimension_semantics=("arbitrary",), vmem_limit_bytes=VMEM_LIMIT),
        name="moe_scatter",
    )(meta, slot3, pay)


def _gather_kernel(slot_ref, nslot_ref, h_ref, ys_ref, y_ref, buf, sem):
    c = pl.program_id(0)
    nc = pl.num_programs(0)

    def fetch(ref, slot):
        def issue(u, carry):
            pltpu.make_async_copy(ys_ref.at[pl.ds(ref[0, 0, u], 1)], buf.at[slot, pl.ds(u, 1)], sem.at[slot]).start()
            return carry
        lax.fori_loop(0, GATHER_CHUNK, issue, 0, unroll=8)

    @pl.when(c == 0)
    def _():
        fetch(slot_ref, 0)

    @pl.when(c + 1 < nc)
    def _():
        fetch(nslot_ref, (c + 1) % 2)

    cur = c % 2
    pltpu.make_async_copy(ys_ref.at[pl.ds(0, GATHER_CHUNK)], buf.at[cur], sem.at[cur]).wait()
    y_ref[...] = h_ref[...] + jnp.concatenate([buf[cur, :, k, :] for k in range(X_TILES)], axis=1)


def _gather(slot, h, ys):
    t = h.shape[0]
    nc = t // GATHER_CHUNK
    slot3 = slot.reshape(nc, 1, GATHER_CHUNK)
    return pl.pallas_call(
        _gather_kernel,
        grid=(nc,),
        in_specs=[pl.BlockSpec((1, 1, GATHER_CHUNK), lambda c: (c, 0, 0), memory_space=pltpu.SMEM),
                  pl.BlockSpec((1, 1, GATHER_CHUNK), lambda c: (jnp.minimum(c + 1, nc - 1), 0, 0),
                               memory_space=pltpu.SMEM),
                  pl.BlockSpec((GATHER_CHUNK, D_MODEL), lambda c: (c, 0)),
                  pl.BlockSpec(memory_space=pl.ANY)],
        out_specs=pl.BlockSpec((GATHER_CHUNK, D_MODEL), lambda c: (c, 0)),
        out_shape=jax.ShapeDtypeStruct((t, D_MODEL), F32),
        scratch_shapes=[pltpu.VMEM((2, GATHER_CHUNK, X_TILES, LANES), F32), pltpu.SemaphoreType.DMA((2,))],
        compiler_params=pltpu.CompilerParams(dimension_semantics=("arbitrary",), vmem_limit_bytes=VMEM_LIMIT),
        name="moe_gather",
    )(slot3, slot3, h, ys)


def _group_moe_kernel(tg_ref, nu_ref, xs_ref, wg_ref, wu_ref, wd_ref, ys_ref):
    j = pl.program_id(0)

    @pl.when(j < nu_ref[0])
    def _():
        g = tg_ref[j]
        x = _tiles_to_rows(xs_ref, 0, X_TILES).astype(BF16)
        gate = xs_ref[:, X_TILES, :]
        lane = lax.broadcasted_iota(jnp.int32, gate.shape, 1)
        y = jnp.zeros((MOE_ROWS, D_MODEL), F32)
        for e in range(EXPERTS_PER_GROUP):
            gu = jnp.dot(x, wg_ref[e], preferred_element_type=F32)
            up = jnp.dot(x, wu_ref[e], preferred_element_type=F32)
            he = (gu * (1.0 / (1.0 + jnp.exp(-gu)))) * up
            out = jnp.dot(he.astype(BF16), wd_ref[e], preferred_element_type=F32)
            col = ROUTER_LANE0 + g * EXPERTS_PER_GROUP + e
            y = y + jnp.sum(jnp.where(lane == col, gate, 0.0), axis=-1, keepdims=True) * out
        _rows_to_tiles(ys_ref, 0, y)

    @pl.when(j >= nu_ref[0])
    def _():
        ys_ref[...] = jnp.zeros(ys_ref.shape, F32)


def _group_moe(tile_group, n_used, xs, wg16, wu16, wd16):
    n_slots = xs.shape[0]
    wspec = lambda shape: pl.BlockSpec((EXPERTS_PER_GROUP,) + shape, lambda j, tg, nu: (tg[j], 0, 0))
    return pl.pallas_call(
        _group_moe_kernel,
        grid_spec=pltpu.PrefetchScalarGridSpec(
            num_scalar_prefetch=2,
            grid=(n_slots // MOE_ROWS,),
            in_specs=[pl.BlockSpec((MOE_ROWS, PAY_SUB, LANES), lambda j, tg, nu: (j, 0, 0)),
                      wspec((D_MODEL, D_EXPERT)), wspec((D_MODEL, D_EXPERT)), wspec((D_EXPERT, D_MODEL))],
            out_specs=pl.BlockSpec((MOE_ROWS, X_TILES, LANES), lambda j, tg, nu: (j, 0, 0)),
        ),
        out_shape=jax.ShapeDtypeStruct((n_slots, X_TILES, LANES), F32),
        compiler_params=pltpu.CompilerParams(dimension_semantics=("arbitrary",), vmem_limit_bytes=VMEM_LIMIT),
        name="moe_group",
    )(tile_group, n_used, xs, wg16, wu16, wd16)


def _routed_moe(h, pay, plan, counts, wg16, wu16, wd16):
    t = pay.shape[0]
    n_slots = t + N_EXPERT_GROUPS * MOE_ROWS
    n_tiles = n_slots // MOE_ROWS
    gid = plan[:, GROUP_LANE].astype(jnp.int32)
    rank = plan[:, RANK_LANE].astype(jnp.int32)
    cnt = counts[:N_EXPERT_GROUPS, 0].astype(jnp.int32)
    padded = -(-cnt // MOE_ROWS) * MOE_ROWS
    ends = jnp.cumsum(padded)
    off = ends - padded
    slot = (off[gid] + rank).reshape(t // ROW_CHUNK, 1, ROW_CHUNK)
    tile_start = jnp.arange(n_tiles, dtype=jnp.int32) * MOE_ROWS
    tile_group = jnp.minimum(jnp.sum(tile_start[:, None] >= ends[None, :], axis=1), N_EXPERT_GROUPS - 1)
    n_used = (ends[-1] // MOE_ROWS).reshape(1)
    meta = jnp.concatenate([off + cnt, ends[-1:], ends, jnp.full((1,), n_slots, jnp.int32)]).astype(jnp.int32)
    xs = _scatter(meta, slot, pay, n_slots)
    ys = _group_moe(tile_group.astype(jnp.int32), n_used.astype(jnp.int32), xs, wg16, wu16, wd16)
    return _gather(slot, h, ys)


def _moe_kernel(n2_ref, gate_ref, h_ref, wg_ref, wu_ref, wd_ref, y_ref):
    e = pl.program_id(1)

    @pl.when(e == 0)
    def _():
        y_ref[...] = h_ref[...]

    n2 = n2_ref[...]
    gu = jnp.dot(n2, wg_ref[0], preferred_element_type=F32)
    up = jnp.dot(n2, wu_ref[0], preferred_element_type=F32)
    he = (gu * (1.0 / (1.0 + jnp.exp(-gu)))) * up
    out = jnp.dot(he.astype(BF16), wd_ref[0], preferred_element_type=F32)
    lane = lax.broadcasted_iota(jnp.int32, gate_ref.shape, 1)
    gcol = jnp.sum(jnp.where(lane == e + ROUTER_LANE0, gate_ref[...], 0.0), axis=-1, keepdims=True)
    y_ref[...] += gcol * out


def _moe(n2, gate, h, wg16, wu16, wd16, tm):
    t = n2.shape[0]
    row = lambda i, e: (i, 0)
    return pl.pallas_call(
        _moe_kernel,
        grid=(t // tm, N_EXPERTS),
        in_specs=[
            pl.BlockSpec((tm, D_MODEL), row),
            pl.BlockSpec((tm, LANES), row),
            pl.BlockSpec((tm, D_MODEL), row),
            pl.BlockSpec((1, D_MODEL, D_EXPERT), lambda i, e: (e, 0, 0)),
            pl.BlockSpec((1, D_MODEL, D_EXPERT), lambda i, e: (e, 0, 0)),
            pl.BlockSpec((1, D_EXPERT, D_MODEL), lambda i, e: (e, 0, 0)),
        ],
        out_specs=pl.BlockSpec((tm, D_MODEL), row),
        out_shape=jax.ShapeDtypeStruct((t, D_MODEL), F32),
        compiler_params=pltpu.CompilerParams(
            dimension_semantics=("arbitrary", "arbitrary"), vmem_limit_bytes=VMEM_LIMIT),
        name="moe",
    )(n2, gate, h, wg16, wu16, wd16)


PROJ_TILE = 512
MOE_TILE = 1024
PROMPT_NPP = LANES // 2
SAMPLE_NPP = LANES


def _round_up(n, mult):
    return -(-n // mult) * mult


def _feature_major(x5):
    b, t = x5.shape[:2]
    return jnp.transpose(x5, (0, 2, 3, 4, 1)).reshape(b, KV_COLS, t)


def _token_major(xt):
    b, _, t = xt.shape
    return jnp.transpose(xt.reshape(b, 2, N_KV_HEADS, HEAD_DIM, t), (0, 4, 1, 2, 3))


def _pad_pages(x, npp_from, npp_to):
    b = x.shape[0]
    x = x.reshape(b, N_KV_HEADS, npp_from, LANES)
    return jnp.concatenate([x, jnp.zeros((b, N_KV_HEADS, npp_to - npp_from, LANES), x.dtype)], axis=2)


def _expand_const(n_chunks, npp):
    col = np.arange(2 * npp)
    blk = 2 * (col % npp) + col // npp
    tok_blk = np.arange(n_chunks * K_CHUNK) // CMP_BLOCK
    e = (blk[None, :, None] == tok_blk.reshape(n_chunks, 1, K_CHUNK)).astype(np.float32)
    return jnp.asarray(e, dtype=BF16)


def _gate_place_const():
    gp = np.zeros((GATE_PAD, 3 * ATTN_WIDTH), np.float32)
    for h in range(N_HEADS):
        for c in range(3):
            gp[3 * h + c, c * ATTN_WIDTH + h * HEAD_DIM:c * ATTN_WIDTH + (h + 1) * HEAD_DIM] = 1.0
    return jnp.asarray(gp, dtype=BF16)


def kernel(x_prompt, x_sample, cache_cmp_kv, cache_slc_kv, state_win_kv, state_pool, page_table, norm1_g, w_in, q_norm_g, k_norm_cmp_g, k_norm_slc_g, k_norm_win_g, cmp_pos_emb, w_cmp_k, w_cmp_v, pool_w, pool_scale, w_out, norm2_g, w_router_group, w_router_expert, w_gate, w_up, w_down):
    b, s, _ = x_prompt.shape
    db, ds, _ = x_sample.shape
    n_pool, page_rows = cache_cmp_kv.shape[:2]
    assert page_rows == PAGE and s % PAGE == 0
    past = page_table.shape[1] * page_rows

    kv0 = ATTN_WIDTH
    kv1 = kv0 + KV_ROWS
    w_row = jnp.concatenate(
        [w_in[:, :kv0], w_in[:, kv1:kv1 + GATE_COLS], jnp.zeros((D_MODEL, GATE_PAD - GATE_COLS), w_in.dtype),
         w_in[:, kv1 + GATE_COLS:]], axis=1).astype(BF16)
    w_kv = w_in[:, kv0:kv1].T.astype(BF16)
    g1 = norm1_g[None, :]
    g2 = norm2_g[None, :]
    two = lambda g: jnp.tile(g, 2)[None, :]
    kg = jnp.stack([k_norm_slc_g, k_norm_win_g])[:, :, None]
    pos_t = jnp.tile(cmp_pos_emb.T, (1, 2))
    zb = jnp.zeros((HEAD_DIM, CMP_BLOCK, HEAD_DIM), F32)

    def blockdiag(w):
        wt = jnp.transpose(w, (1, 0, 2))
        return jnp.concatenate([jnp.concatenate([wt, zb], axis=2), jnp.concatenate([zb, wt], axis=2)], axis=1)

    bdk, bdv = blockdiag(w_cmp_k), blockdiag(w_cmp_v)
    zq = jnp.zeros_like(bdk)
    bd = jnp.concatenate([jnp.concatenate([bdk, zq], axis=2), jnp.concatenate([zq, bdv], axis=2)],
                         axis=1).astype(BF16)
    w_router = jnp.concatenate(
        [w_router_group, w_router_expert,
         jnp.zeros((D_MODEL, LANES - N_EXPERT_GROUPS - N_EXPERTS), w_router_group.dtype)], axis=1).T.astype(BF16)
    w_out16 = w_out.astype(BF16)
    pool_w16 = pool_w.astype(BF16)
    wg16, wu16, wd16 = w_gate.astype(BF16), w_up.astype(BF16), w_down.astype(BF16)
    ps = pool_scale[None, :]

    def ffn_moe(x2d, a2d, m2d, tm_ffn, tm_moe, routed):
        if routed:
            return _routed_moe(*_ffn(x2d, a2d, m2d, w_out16, g2, w_router, tm_ffn, True), wg16, wu16, wd16)
        h, n2, gate = _ffn(x2d, a2d, m2d, w_out16, g2, w_router, tm_ffn, False)
        return _moe(n2, gate, h, wg16, wu16, wd16, tm_moe)

    q, gates, u, kvc_t, kvs_t, kvw_t, ks16, vsx, kw16, vwx = _proj(
        x_prompt, g1, w_row, w_kv, two(q_norm_g), kg, PROJ_TILE)
    m_p = _pool(u, pool_w16, ps, s)
    pages_p = s // PAGE
    pt_p = jnp.zeros((b, pages_p), jnp.int32)
    dummy_tail = jnp.zeros((b, KV_COLS, LANES), F32)
    kc_p, vc_p = _compress(pt_p, kvc_t, dummy_tail, pos_t, bd, two(k_norm_cmp_g),
                           npp=pages_p, has_tail=False, paged=False)
    kc_p = _pad_pages(kc_p, pages_p, PROMPT_NPP)
    vc_p = _pad_pages(vc_p, pages_p, PROMPT_NPP)
    a_p = _attn_prompt(q, gates, kc_p, vc_p, ks16, vsx, kw16, vwx, _expand_const(s // K_CHUNK, PROMPT_NPP),
                       _gate_place_const(), pages_p)
    y_p = ffn_moe(x_prompt.reshape(b * s, D_MODEL), a_p.reshape(b * s, ATTN_WIDTH),
                  m_p.reshape(b * s, POOL_WIDTH), PROJ_TILE, MOE_TILE, True)

    ts = db * ds
    q_s, gates_s, u_s, kvc_st, kvs_st, kvw_st, _, _, _, _ = _proj(
        x_sample.reshape(1, ts, D_MODEL), g1, w_row, w_kv, two(q_norm_g), kg, ts)
    u_s = u_s.reshape(db, ds, POOL_WIDTH)
    pool_ext = jnp.concatenate([state_pool, u_s], axis=1)
    n_ext = pool_ext.shape[1]
    lead = _round_up(n_ext, 8) - n_ext
    pool_in = jnp.concatenate([jnp.zeros((db, lead, POOL_WIDTH), F32), pool_ext], axis=1)
    n_in = lead + n_ext
    m_s = _pool(pool_in.reshape(1, db * n_in, POOL_WIDTH), pool_w16, ps, n_in).reshape(db, n_in, POOL_WIDTH)[:, n_in - ds:]

    def new_rows_t(xt):
        x = jnp.transpose(xt.reshape(KV_COLS, db, ds), (1, 0, 2))
        return jnp.concatenate([x, jnp.zeros((db, KV_COLS, LANES - ds), F32)], axis=2)

    kvc_new, kvs_new, kvw_new = new_rows_t(kvc_st), new_rows_t(kvs_st), new_rows_t(kvw_st)
    n_pages = page_table.shape[1]
    npp_c = _round_up(n_pages + 1, 8)
    cmp_pages = _feature_major(cache_cmp_kv)
    slc_pages = _feature_major(cache_slc_kv)
    kc_s, vc_s = _compress(page_table, cmp_pages, kvc_new, pos_t, bd, two(k_norm_cmp_g),
                           npp=npp_c, has_tail=True, paged=True)
    kc_s = _pad_pages(kc_s, npp_c, SAMPLE_NPP)
    vc_s = _pad_pages(vc_s, npp_c, SAMPLE_NPP)
    win_t = _feature_major(state_win_kv)
    q_s3 = q_s.reshape(db, ds, ATTN_WIDTH)
    o_cmp_s, sel_s, need_s = _select_sample(q_s3, kc_s, vc_s, past=past)
    need = need_s[:, 0, :n_pages] > 0.5
    order = jnp.argsort(jnp.logical_not(need), axis=1, stable=True).astype(jnp.int32)
    n_need = jnp.sum(need, axis=1).astype(jnp.int32)
    a_s = _attn_sample(page_table, order, n_need, q_s3, gates_s.reshape(db, ds, GATE_PAD), o_cmp_s, sel_s,
                       kvs_new, kvw_new, win_t, slc_pages, past=past)
    y_s = ffn_moe(x_sample.reshape(ts, D_MODEL), a_s.reshape(ts, ATTN_WIDTH), m_s.reshape(ts, POOL_WIDTH), ts, ts, False)

    win_keep = min(WINDOW, s)
    sample5 = lambda xt: jnp.transpose(xt.reshape(2, N_KV_HEADS, HEAD_DIM, db, ds), (3, 4, 0, 1, 2))
    win_ctx_t = jnp.concatenate([win_t, kvw_new[:, :, :ds]], axis=2)
    return (y_p.reshape(b, s, D_MODEL), y_s.reshape(db, ds, D_MODEL),
            _token_major(kvc_t), sample5(kvc_st),
            _token_major(kvs_t), sample5(kvs_st),
            _token_major(kvw_t[:, :, s - win_keep:]), _token_major(win_ctx_t[:, :, ds:]),
            u[:, s - POOL_BUF:], pool_ext[:, ds:])
```

```python
import functools

import jax
import jax.numpy as jnp
import numpy as np
from jax import lax
from jax.experimental import pallas as pl
from jax.experimental.pallas import tpu as pltpu

F32 = jnp.float32
BF16 = jnp.bfloat16

D_MODEL = 1024
N_HEADS = 8
HEAD_DIM = 64
N_KV_HEADS = 2
Q_PER_KV = N_HEADS // N_KV_HEADS
ATTN_WIDTH = N_HEADS * HEAD_DIM
KV_COLS = 2 * N_KV_HEADS * HEAD_DIM
GATE_COLS = 3 * N_HEADS
POOL_WIDTH = D_MODEL - ATTN_WIDTH
POOL_WINDOWS = (2, 4, 8, 16)
POOL_GROUP_WIDTH = POOL_WIDTH // len(POOL_WINDOWS)
POOL_BUF = max(POOL_WINDOWS) - 1
CMP_BLOCK = 64
TOP_K_BLOCKS = 16
WINDOW = 512
FORCE_SCORE = 1.0e4
N_EXPERT_GROUPS = 4
EXPERTS_PER_GROUP = 4
N_EXPERTS = N_EXPERT_GROUPS * EXPERTS_PER_GROUP
D_EXPERT = 512
EPS = 1e-6
NEG = -1e30
SCALE = HEAD_DIM ** -0.5

LANES = 128
PAGE = 2 * CMP_BLOCK
GATE_PAD = LANES
ROW_COLS = ATTN_WIDTH + GATE_PAD + POOL_WIDTH
OFF_GATE = ATTN_WIDTH
OFF_U = OFF_GATE + GATE_PAD
KV_ROWS = 3 * KV_COLS
ROUTER_LANE0 = N_EXPERT_GROUPS
Q_TILE = 256
K_CHUNK = 256
VMEM_LIMIT = 56 * 1024 * 1024


def _slope(h):
    return float(2.0 ** (-8.0 * (h + 1) / N_HEADS))


def _half_group_norm(v, g):
    lane = lax.broadcasted_iota(jnp.int32, v.shape, 1)
    lo = lane < HEAD_DIM
    v2 = v * v
    s_lo = jnp.sum(jnp.where(lo, v2, 0.0), axis=-1, keepdims=True)
    s_hi = jnp.sum(jnp.where(lo, 0.0, v2), axis=-1, keepdims=True)
    r_lo = lax.rsqrt(s_lo * (1.0 / HEAD_DIM) + EPS)
    r_hi = lax.rsqrt(s_hi * (1.0 / HEAD_DIM) + EPS)
    return (v * jnp.where(lo, r_lo, r_hi)) * g


def _dot_t(a, b):
    return lax.dot_general(a, b, (((1,), (1,)), ((), ())), preferred_element_type=F32)


def _proj_kernel(x_ref, g1_ref, wr_ref, wkv_ref, qg_ref, kg_ref,
                 q_ref, gates_ref, u_ref, kvc_ref, kvs_ref, kvw_ref, ks16_ref, vsx_ref, kw16_ref, vwx_ref):
    x = x_ref[0]
    tm = x.shape[0]
    ms = jnp.mean(x * x, axis=-1, keepdims=True)
    n = ((x * lax.rsqrt(ms + EPS)) * g1_ref[...]).astype(BF16)
    p = jnp.dot(n, wr_ref[...], preferred_element_type=F32)
    pt = _dot_t(wkv_ref[...], n)
    for t in range(ATTN_WIDTH // LANES):
        sl = slice(t * LANES, (t + 1) * LANES)
        q_ref[0, :, sl] = _half_group_norm(p[:, sl], qg_ref[...]).astype(BF16)
    gates_ref[0] = 1.0 / (1.0 + jnp.exp(-p[:, OFF_GATE:OFF_GATE + GATE_PAD]))
    u_ref[0] = p[:, OFF_U:OFF_U + POOL_WIDTH]
    kvc_ref[0] = pt[:KV_COLS]
    half = KV_COLS // 2
    lane = lax.broadcasted_iota(jnp.int32, (tm, LANES), 1)
    for bi, (out32, k16, vx16) in enumerate(((kvs_ref, ks16_ref, vsx_ref), (kvw_ref, kw16_ref, vwx_ref))):
        off = (bi + 1) * KV_COLS
        heads = []
        for hh in range(N_KV_HEADS):
            kh = pt[off + hh * HEAD_DIM:off + (hh + 1) * HEAD_DIM]
            msk = jnp.mean(kh * kh, axis=0, keepdims=True)
            heads.append((kh * lax.rsqrt(msk + EPS)) * kg_ref[bi])
        kn = jnp.concatenate(heads, axis=0)
        v = pt[off + half:off + KV_COLS]
        out32[0, :half, :] = kn
        out32[0, half:, :] = v
        vt = v.T
        vx = (jnp.where(lane < HEAD_DIM, vt, 1.0), jnp.where(lane < HEAD_DIM, pltpu.roll(vt, HEAD_DIM, axis=1), 1.0))
        for c in range(tm // K_CHUNK):
            cs = slice(c * K_CHUNK, (c + 1) * K_CHUNK)
            k16[0, c] = kn[:, cs].astype(BF16)
            for hh in range(N_KV_HEADS):
                vx16[0, c, hh] = vx[hh][cs].astype(BF16)


def _proj(x3d, g1, w_row, w_kv, qg, kg, tm):
    b, s, _ = x3d.shape
    tok = lambda i, j: (i, j, 0)
    feat = lambda i, j: (i, 0, j)
    const2 = lambda i, j: (0, 0)
    nck = tm // K_CHUNK
    out_shape = (
        jax.ShapeDtypeStruct((b, s, ATTN_WIDTH), BF16),
        jax.ShapeDtypeStruct((b, s, GATE_PAD), F32),
        jax.ShapeDtypeStruct((b, s, POOL_WIDTH), F32),
        jax.ShapeDtypeStruct((b, KV_COLS, s), F32),
        jax.ShapeDtypeStruct((b, KV_COLS, s), F32),
        jax.ShapeDtypeStruct((b, KV_COLS, s), F32),
        jax.ShapeDtypeStruct((b, s // K_CHUNK, LANES, K_CHUNK), BF16),
        jax.ShapeDtypeStruct((b, s // K_CHUNK, N_KV_HEADS, K_CHUNK, LANES), BF16),
        jax.ShapeDtypeStruct((b, s // K_CHUNK, LANES, K_CHUNK), BF16),
        jax.ShapeDtypeStruct((b, s // K_CHUNK, N_KV_HEADS, K_CHUNK, LANES), BF16),
    )
    k_spec = pl.BlockSpec((1, nck, LANES, K_CHUNK), lambda i, j: (i, j, 0, 0))
    v_spec = pl.BlockSpec((1, nck, N_KV_HEADS, K_CHUNK, LANES), lambda i, j: (i, j, 0, 0, 0))
    out_specs = (
        pl.BlockSpec((1, tm, ATTN_WIDTH), tok),
        pl.BlockSpec((1, tm, GATE_PAD), tok),
        pl.BlockSpec((1, tm, POOL_WIDTH), tok),
        pl.BlockSpec((1, KV_COLS, tm), feat),
        pl.BlockSpec((1, KV_COLS, tm), feat),
        pl.BlockSpec((1, KV_COLS, tm), feat),
        k_spec, v_spec, k_spec, v_spec,
    )
    return pl.pallas_call(
        _proj_kernel,
        grid=(b, s // tm),
        in_specs=[
            pl.BlockSpec((1, tm, D_MODEL), tok),
            pl.BlockSpec((1, D_MODEL), const2),
            pl.BlockSpec((D_MODEL, ROW_COLS), const2),
            pl.BlockSpec((KV_ROWS, D_MODEL), const2),
            pl.BlockSpec((1, LANES), const2),
            pl.BlockSpec((2, HEAD_DIM, 1), lambda i, j: (0, 0, 0)),
        ],
        out_specs=out_specs,
        out_shape=out_shape,
        compiler_params=pltpu.CompilerParams(
            dimension_semantics=("arbitrary", "arbitrary"), vmem_limit_bytes=VMEM_LIMIT),
        name="proj",
    )(x3d, g1, w_row, w_kv, qg, kg)


def _pool_kernel(u_ref, pw_ref, ps_ref, m_ref, *, seg):
    u = u_ref[0]
    n = u.shape[0]

    def shift(v, k):
        rolled = pltpu.roll(v, k, axis=0)
        r = lax.broadcasted_iota(jnp.int32, v.shape, 0) % seg
        return jnp.where(r >= k, rolled, 0.0)

    sums = []
    s = u
    k = 1
    for gi in range(len(POOL_WINDOWS)):
        s = s + shift(s, k)
        k *= 2
        sums.append(s[:, :POOL_GROUP_WIDTH])
        s = s[:, POOL_GROUP_WIDTH:]
    row = lax.broadcasted_iota(jnp.int32, (n, 1), 0) % seg
    for gi, w in enumerate(POOL_WINDOWS):
        sl = slice(gi * POOL_GROUP_WIDTH, (gi + 1) * POOL_GROUP_WIDTH)
        cnt = jnp.minimum(row + 1, w).astype(F32)
        d = sums[gi] / cnt - u[:, sl]
        y = jnp.dot(d.astype(BF16), pw_ref[gi], preferred_element_type=F32)
        m_ref[0, :, sl] = (y * ps_ref[:, sl]).astype(BF16)


def _pool(u3d, pool_w16, pool_scale, seg):
    b, n, _ = u3d.shape
    assert n % seg == 0
    return pl.pallas_call(
        functools.partial(_pool_kernel, seg=seg),
        grid=(b,),
        in_specs=[
            pl.BlockSpec((1, n, POOL_WIDTH), lambda i: (i, 0, 0)),
            pl.BlockSpec((len(POOL_WINDOWS), POOL_GROUP_WIDTH, POOL_GROUP_WIDTH), lambda i: (0, 0, 0)),
            pl.BlockSpec((1, POOL_WIDTH), lambda i: (0, 0)),
        ],
        out_specs=pl.BlockSpec((1, n, POOL_WIDTH), lambda i: (i, 0, 0)),
        out_shape=jax.ShapeDtypeStruct((b, n, POOL_WIDTH), BF16),
        compiler_params=pltpu.CompilerParams(
            dimension_semantics=("arbitrary",), vmem_limit_bytes=VMEM_LIMIT),
        name="pool",
    )(u3d, pool_w16, pool_scale)


def _compress_kernel(pt_ref, src_ref, tail_ref, pos_ref, bd_ref, g_ref, kc_ref, vc_ref, buf, sem,
                     *, n_pages, npp, has_tail, paged):
    b = pl.program_id(0)
    nb = pl.num_programs(0)
    n_slabs = 2 * N_KV_HEADS

    def slab_copy(row, p, c, slot):
        kv, kvh = divmod(c, N_KV_HEADS)
        if paged:
            src = src_ref.at[pt_ref[row, p], pl.ds(c * HEAD_DIM, HEAD_DIM), :]
        else:
            src = src_ref.at[row, pl.ds(c * HEAD_DIM, HEAD_DIM), pl.ds(p * PAGE, PAGE)]
        return pltpu.make_async_copy(src, buf.at[slot, kv, :, kvh * npp + p, :], sem.at[slot])

    def tail_copy(c, slot):
        kv, kvh = divmod(c, N_KV_HEADS)
        return pltpu.make_async_copy(tail_ref.at[0, pl.ds(c * HEAD_DIM, HEAD_DIM), :],
                                     buf.at[slot, kv, :, kvh * npp + n_pages, :], sem.at[slot])

    def row_copies(row, slot, fn):
        def body(p, carry):
            for c in range(n_slabs):
                fn(slab_copy(row, p, c, slot))
            return carry
        lax.fori_loop(0, n_pages, body, 0)

    n_real = n_pages + (1 if has_tail else 0)

    @pl.when(b == 0)
    def _():
        if npp > n_real:
            for kvh in range(N_KV_HEADS):
                buf[:, :, :, kvh * npp + n_real:(kvh + 1) * npp, :] = jnp.zeros(
                    (2, 2, HEAD_DIM, npp - n_real, LANES), F32)
        row_copies(0, 0, lambda cp: cp.start())

    slot = b % 2

    if has_tail:
        for c in range(n_slabs):
            tail_copy(c, slot).start()
        for c in range(n_slabs):
            tail_copy(c, slot).wait()
    row_copies(b, slot, lambda cp: cp.wait())

    rows = N_KV_HEADS * npp
    d_per_it = 8
    n_it = HEAD_DIM // d_per_it
    pages_per_it = n_pages // n_it
    nxt = jnp.minimum(b + 1, nb - 1)

    def body(it, acc):
        for pp in range(pages_per_it):
            for c in range(n_slabs):
                slab_copy(nxt, it * pages_per_it + pp, c, 1 - slot).start()
        for dd in range(d_per_it):
            d = it * d_per_it + dd
            pos = pos_ref[pl.ds(d, 1), :]
            x = jnp.concatenate([buf[slot, 0, d] + pos, buf[slot, 1, d] + pos], axis=1).astype(BF16)
            acc = acc + jnp.dot(x, bd_ref[d], preferred_element_type=F32)
        return acc

    acc = lax.fori_loop(0, n_it, body, jnp.zeros((rows, 2 * LANES), F32))
    kc_ref[0] = _half_group_norm(acc[:, :LANES], g_ref[...])
    vc_ref[0] = acc[:, LANES:]

    @pl.when(b == nb - 1)
    def _():
        row_copies(nxt, 1 - slot, lambda cp: cp.wait())


def _compress(page_table, src, tail, pos_t, bd, g2, *, npp, has_tail, paged):
    b, n_pages = page_table.shape
    kern = functools.partial(_compress_kernel, n_pages=n_pages, npp=npp, has_tail=has_tail, paged=paged)
    rows = N_KV_HEADS * npp
    grid_spec = pltpu.PrefetchScalarGridSpec(
        num_scalar_prefetch=1,
        grid=(b,),
        in_specs=[
            pl.BlockSpec(memory_space=pl.ANY),
            pl.BlockSpec((1, KV_COLS, LANES), lambda i, pt: (i, 0, 0)),
            pl.BlockSpec((HEAD_DIM, LANES), lambda i, pt: (0, 0)),
            pl.BlockSpec((HEAD_DIM, 2 * LANES, 2 * LANES), lambda i, pt: (0, 0, 0)),
            pl.BlockSpec((1, LANES), lambda i, pt: (0, 0)),
        ],
        out_specs=(pl.BlockSpec((1, rows, LANES), lambda i, pt: (i, 0, 0)),
                   pl.BlockSpec((1, rows, LANES), lambda i, pt: (i, 0, 0))),
        scratch_shapes=[
            pltpu.VMEM((2, 2, HEAD_DIM, rows, LANES), F32),
            pltpu.SemaphoreType.DMA((2,)),
        ],
    )
    return pl.pallas_call(
        kern,
        grid_spec=grid_spec,
        out_shape=(jax.ShapeDtypeStruct((b, rows, LANES), F32), jax.ShapeDtypeStruct((b, rows, LANES), F32)),
        compiler_params=pltpu.CompilerParams(
            dimension_semantics=("arbitrary",), vmem_limit_bytes=VMEM_LIMIT),
        name="compress",
    )(page_table, src, tail, pos_t, bd, g2)


def _to_half(tile, src_half, dst_half):
    lane = lax.broadcasted_iota(jnp.int32, tile.shape, 1)
    src = tile if src_half == dst_half else pltpu.roll(tile, HEAD_DIM, axis=1)
    keep = (lane < HEAD_DIM) if dst_half == 0 else (lane >= HEAD_DIM)
    return jnp.where(keep, src, 0.0)


def _pair_tile(o_even, o_odd, half):
    lane = lax.broadcasted_iota(jnp.int32, o_even.shape, 1)
    if half == 0:
        return jnp.where(lane < HEAD_DIM, o_even, pltpu.roll(o_odd, HEAD_DIM, axis=1))
    return jnp.where(lane < HEAD_DIM, pltpu.roll(o_even, HEAD_DIM, axis=1), o_odd)


def _gate_tile(gates, pair, c, shape):
    lane = lax.broadcasted_iota(jnp.int32, shape, 1)
    he, ho = 2 * pair, 2 * pair + 1
    return jnp.where(lane < HEAD_DIM, gates[:, 3 * he + c:3 * he + c + 1], gates[:, 3 * ho + c:3 * ho + c + 1])


def _block_of_col(col, npp):
    return 2 * (col % npp) + col // npp


def _cmp_operand(x, npp):
    return jnp.concatenate([_to_half(x, 0, 0), _to_half(x, 1, 0)], axis=0).astype(BF16)


def _rank_select(score_t, blk_t, cols):
    rank = jnp.zeros(score_t.shape, jnp.int32)
    for r, n in cols:
        row = score_t[r:r + 1, :]
        ahead = (row > score_t) | ((row == score_t) & (blk_t > n))
        rank = rank + ahead.astype(jnp.int32)
    return ((rank < TOP_K_BLOCKS) & (score_t > -0.5)).astype(F32)


POS_HI, POS_LO = HEAD_DIM, HEAD_DIM + 1
MASK_BIG = 1e30


def _attn_prompt_kernel(q_ref, gates_ref, kc_ref, vc_ref, ks_ref, vsx_ref, kw_ref, vwx_ref, exp_ref, gp_ref, a_ref,
                        zbuf, m_sc, acc_sc, *, n_pages, npp):
    i = pl.program_id(1)
    q0 = i * Q_TILE
    rows = Q_PER_KV * Q_TILE
    ncols = 2 * npp
    qpos = q0 + lax.broadcasted_iota(jnp.int32, (Q_TILE, 1), 0)
    blk = _block_of_col(lax.broadcasted_iota(jnp.int32, (Q_TILE, ncols), 1), npp)
    real_cols = [(half * n_pages + p, 2 * p + half) for half in range(2) for p in range(n_pages)]

    g = gates_ref[0]
    g_hi = g.astype(BF16)
    g_lo = (g - g_hi.astype(F32)).astype(BF16)
    gexp = (jnp.dot(g_hi, gp_ref[...], preferred_element_type=F32)
            + jnp.dot(g_lo, gp_ref[...], preferred_element_type=F32))

    r_key = lax.broadcasted_iota(jnp.int32, (Q_TILE, K_CHUNK), 1)
    r_qry = lax.broadcasted_iota(jnp.int32, (Q_TILE, K_CHUNK), 0)
    keep = {"causal": r_key <= r_qry, "lower": r_key > r_qry}

    all_rows = N_HEADS * Q_TILE

    def branch(lhs, k_ref, vx_ref, slc, chunks, c_lo, c_hi):
        m_sc[...] = jnp.full((all_rows, LANES), NEG, F32)

        def scores(c, kind):
            rr = lax.broadcasted_iota(jnp.int32, (HEAD_DIM, K_CHUNK), 0)
            tt = lax.broadcasted_iota(jnp.int32, (HEAD_DIM, K_CHUNK), 1).astype(F32)
            hi = ((c - i) * K_CHUNK).astype(F32)
            pos = jnp.where(rr == 0, hi, jnp.where(rr == 1, tt, 0.0)).astype(BF16)
            for kvh in range(N_KV_HEADS):
                parts = [k_ref[0, c, kvh * HEAD_DIM:(kvh + 1) * HEAD_DIM, :], pos]
                if slc:
                    parts.append(exp_ref[c])
                kx = jnp.concatenate(parts, axis=0)
                for gq in range(Q_PER_KV):
                    sl = slice((kvh * Q_PER_KV + gq) * Q_TILE, (kvh * Q_PER_KV + gq + 1) * Q_TILE)
                    zg = jnp.dot(lhs[sl], kx, preferred_element_type=F32)
                    if kind is not None:
                        zg = jnp.where(keep[kind], zg, NEG)
                    zbuf[c, sl, :] = zg
                    m_sc[sl] = jnp.maximum(m_sc[sl], jnp.maximum(zg[:, :LANES], zg[:, LANES:]))

        def in_pairs(lo, hi, one):
            n = hi - lo

            def body(k, carry):
                one(lo + 2 * k)
                one(lo + 2 * k + 1)
                return carry
            lax.fori_loop(0, n // 2, body, 0)
            pl.when(n % 2 == 1)(lambda: one(hi - 1))

        for group in chunks:
            if not isinstance(group[1], list):
                in_pairs(group[0], group[1], lambda c: scores(c, None))
            else:
                cond, items = group[0], group[1]

                def run(items=items):
                    for c, kind in items:
                        scores(c, kind)
                run() if cond is None else pl.when(cond)(run)

        m = jnp.broadcast_to(jnp.max(m_sc[...], axis=-1, keepdims=True), (all_rows, LANES))
        acc_sc[...] = jnp.zeros((all_rows, LANES), F32)

        def weigh(c):
            for h in range(N_HEADS):
                sl = slice(h * Q_TILE, (h + 1) * Q_TILE)
                z = zbuf[c, sl, :]
                e = jnp.concatenate([jnp.exp(z[:, :LANES] - m[sl]), jnp.exp(z[:, LANES:] - m[sl])], axis=1)
                acc_sc[sl] += jnp.dot(e.astype(BF16), vx_ref[0, c, h // Q_PER_KV], preferred_element_type=F32)

        in_pairs(c_lo, c_hi, weigh)
        return acc_sc[...]

    def pair_norm(acc_e, acc_o):
        lane = lax.broadcasted_iota(jnp.int32, acc_e.shape, 1)
        safe = lambda d: jnp.where(d > 0.0, d, 1.0)
        sw_e = pltpu.roll(acc_e, HEAD_DIM, axis=1)
        sw_o = pltpu.roll(acc_o, HEAD_DIM, axis=1)
        return jnp.where(lane < HEAD_DIM, acc_e / safe(sw_e), sw_o / safe(acc_o))

    o_cmp_all, lhs_slc_all, lhs_win_all = [], [], []
    for kvh in range(N_KV_HEADS):
        slopes = [_slope(kvh * Q_PER_KV + g) for g in range(Q_PER_KV)]
        lane = lax.broadcasted_iota(jnp.int32, (Q_TILE, LANES), 1)
        pieces, pieces_x = [], []
        for gq in range(Q_PER_KV):
            h = kvh * Q_PER_KV + gq
            tile = q_ref[0, :, (h // 2) * LANES:(h // 2 + 1) * LANES].astype(F32) * SCALE
            low = _to_half(tile, h % 2, 0)
            pieces.append(low)
            pieces_x.append(jnp.where((lane == POS_HI) | (lane == POS_LO), slopes[gq], low))
        qk = jnp.concatenate(pieces, axis=0).astype(BF16)
        qk_x = jnp.concatenate(pieces_x, axis=0).astype(BF16)

        kc = _cmp_operand(kc_ref[0, kvh], npp)
        vc = _cmp_operand(vc_ref[0, kvh], npp)
        s = _dot_t(qk, kc)
        dist_c = qpos - (blk * CMP_BLOCK + (CMP_BLOCK - 1))
        mask_c = dist_c >= 0
        dist_cf = dist_c.astype(F32)
        p_list = []
        p_kv = jnp.zeros((Q_TILE, ncols), F32)
        for g in range(Q_PER_KV):
            z = jnp.where(mask_c, s[g * Q_TILE:(g + 1) * Q_TILE] - slopes[g] * dist_cf, NEG)
            mx = jnp.max(z, axis=-1, keepdims=True)
            e = jnp.where(mask_c, jnp.exp(z - mx), 0.0)
            den = jnp.sum(e, axis=-1, keepdims=True)
            p = e / jnp.where(den > 0.0, den, 1.0)
            p_kv = p_kv + p
            p_list.append(p.astype(BF16))
        o_cmp = jnp.dot(jnp.concatenate(p_list, axis=0), vc, preferred_element_type=F32)

        cand = blk * CMP_BLOCK <= qpos
        forced = (blk == qpos // CMP_BLOCK) | (blk == 0)
        score = jnp.where(forced, FORCE_SCORE, jnp.where(cand, p_kv, -1.0))
        score_t = score.T
        st = jnp.concatenate([score_t[:n_pages], score_t[npp:npp + n_pages]], axis=0)
        rr = lax.broadcasted_iota(jnp.int32, (2 * n_pages, Q_TILE), 0)
        bt = jnp.where(rr < n_pages, 2 * rr, 2 * (rr - n_pages) + 1)
        sel_s = _rank_select(st, bt, [(r, n) for r, (_, n) in enumerate(real_cols)])
        pad = jnp.zeros((npp - n_pages, Q_TILE), F32)
        sel_t = jnp.concatenate([sel_s[:n_pages], pad, sel_s[n_pages:], pad], axis=0)
        sel = sel_t.T
        sel_bias = jnp.where(sel > 0.5, 0.0, -MASK_BIG).astype(BF16)
        lhs_slc_all.append(jnp.concatenate([qk_x, jnp.concatenate([sel_bias] * Q_PER_KV, axis=0)], axis=1))
        lhs_win_all.append(qk_x)
        o_cmp_all.append(o_cmp)

    o_slc = branch(jnp.concatenate(lhs_slc_all, axis=0), ks_ref, vsx_ref, True,
                   [(0, 2 * (i // 2)),
                    (i % 2 == 1, [(i - 1, None), (i, "causal")]), (i % 2 == 0, [(i, "causal")])], 0, i + 1)
    o_win = branch(jnp.concatenate(lhs_win_all, axis=0), kw_ref, vwx_ref, False,
                   [(i >= 2, [(i - 2, "lower"), (i - 1, None), (i, "causal")]),
                    (i == 1, [(0, None), (1, "causal")]),
                    (i == 0, [(0, "causal")])],
                   jnp.maximum(i - 2, 0), i + 1)
    o_cmp = jnp.concatenate(o_cmp_all, axis=0)

    for pair in range(N_HEADS // 2):
        e_sl = slice(2 * pair * Q_TILE, (2 * pair + 1) * Q_TILE)
        o_sl = slice((2 * pair + 1) * Q_TILE, (2 * pair + 2) * Q_TILE)
        col = pair * LANES
        tile = (gexp[:, col:col + LANES] * _pair_tile(o_cmp[e_sl], o_cmp[o_sl], 0)
                + gexp[:, ATTN_WIDTH + col:ATTN_WIDTH + col + LANES] * pair_norm(o_slc[e_sl], o_slc[o_sl])
                + gexp[:, 2 * ATTN_WIDTH + col:2 * ATTN_WIDTH + col + LANES] * pair_norm(o_win[e_sl], o_win[o_sl]))
        a_ref[0, :, pair * LANES:(pair + 1) * LANES] = tile.astype(BF16)


def _attn_prompt(q, gates, kc, vc, ks16, vsx, kw16, vwx, expand, gate_place, n_pages):
    b, s, _ = q.shape
    npp = kc.shape[2]
    n_chunks = s // K_CHUNK
    rows = Q_PER_KV * Q_TILE
    assert WINDOW == 2 * K_CHUNK and Q_TILE == K_CHUNK
    kern = functools.partial(_attn_prompt_kernel, n_pages=n_pages, npp=npp)
    per_b4 = lambda bi, i: (bi, 0, 0, 0)
    per_b5 = lambda bi, i: (bi, 0, 0, 0, 0)
    return pl.pallas_call(
        kern,
        grid=(b, s // Q_TILE),
        in_specs=[
            pl.BlockSpec((1, Q_TILE, ATTN_WIDTH), lambda bi, i: (bi, i, 0)),
            pl.BlockSpec((1, Q_TILE, GATE_PAD), lambda bi, i: (bi, i, 0)),
            pl.BlockSpec((1, N_KV_HEADS, npp, LANES), per_b4),
            pl.BlockSpec((1, N_KV_HEADS, npp, LANES), per_b4),
            pl.BlockSpec((1, n_chunks, LANES, K_CHUNK), per_b4),
            pl.BlockSpec((1, n_chunks, N_KV_HEADS, K_CHUNK, LANES), per_b5),
            pl.BlockSpec((1, n_chunks, LANES, K_CHUNK), per_b4),
            pl.BlockSpec((1, n_chunks, N_KV_HEADS, K_CHUNK, LANES), per_b5),
            pl.BlockSpec((n_chunks, 2 * npp, K_CHUNK), lambda bi, i: (0, 0, 0)),
            pl.BlockSpec((GATE_PAD, 3 * ATTN_WIDTH), lambda bi, i: (0, 0)),
        ],
        out_specs=pl.BlockSpec((1, Q_TILE, ATTN_WIDTH), lambda bi, i: (bi, i, 0)),
        out_shape=jax.ShapeDtypeStruct((b, s, ATTN_WIDTH), BF16),
        scratch_shapes=[
            pltpu.VMEM((n_chunks, N_KV_HEADS * rows, K_CHUNK), F32),
            pltpu.VMEM((N_KV_HEADS * rows, LANES), F32),
            pltpu.VMEM((N_KV_HEADS * rows, LANES), F32),
        ],
        compiler_params=pltpu.CompilerParams(
            dimension_semantics=("arbitrary", "arbitrary"), vmem_limit_bytes=VMEM_LIMIT),
        name="attn_prompt",
    )(q, gates, kc, vc, ks16, vsx, kw16, vwx, expand, gate_place)


S_ROWS = LANES
S_CHUNK = 1024


def _sample_rows(q_ref, ds, past):
    n_real = N_HEADS * ds
    by_kvh, low = [], []
    for h in range(N_HEADS):
        tile = q_ref[0, :, (h // 2) * LANES:(h // 2 + 1) * LANES].astype(F32) * SCALE
        by_kvh.append(_to_half(tile, h % 2, h // Q_PER_KV))
        low.append(_to_half(tile, h % 2, 0))
    zpad = jnp.zeros((S_ROWS - n_real, LANES), F32)
    qrows = jnp.concatenate(by_kvh + [zpad], axis=0).astype(BF16)
    qlow = jnp.concatenate(low + [zpad], axis=0).astype(BF16)
    r_col = lax.broadcasted_iota(jnp.int32, (S_ROWS, 1), 0)
    q_of_r = r_col % ds
    h_of_r = r_col // ds
    sl_r = jnp.zeros((S_ROWS, 1), F32)
    for h in range(N_HEADS):
        sl_r = jnp.where(h_of_r == h, _slope(h), sl_r)
    return qrows, qlow, q_of_r, past + q_of_r, sl_r, h_of_r // Q_PER_KV


def _select_sample_kernel(q_ref, kc_ref, vc_ref, ocmp_ref, sel_ref, need_ref, *, past, ds, npp):
    n_real = N_HEADS * ds
    ncols = 2 * npp
    n_blocks = (past + ds + CMP_BLOCK - 1) // CMP_BLOCK
    _, qlow, _, qp_r, sl_r, kvh_r = _sample_rows(q_ref, ds, past)

    blk = _block_of_col(lax.broadcasted_iota(jnp.int32, (S_ROWS, ncols), 1), npp)
    s = jnp.where(kvh_r == 0, _dot_t(qlow, _cmp_operand(kc_ref[0, 0], npp)),
                  _dot_t(qlow, _cmp_operand(kc_ref[0, 1], npp)))
    dist_c = qp_r - (blk * CMP_BLOCK + (CMP_BLOCK - 1))
    mask_c = dist_c >= 0
    z = jnp.where(mask_c, s - sl_r * dist_c.astype(F32), NEG)
    mx = jnp.max(z, axis=-1, keepdims=True)
    e = jnp.where(mask_c, jnp.exp(z - mx), 0.0)
    den = jnp.sum(e, axis=-1, keepdims=True)
    p = e / jnp.where(den > 0.0, den, 1.0)
    pb = p.astype(BF16)
    o_cmp = jnp.where(kvh_r == 0, jnp.dot(pb, _cmp_operand(vc_ref[0, 0], npp), preferred_element_type=F32),
                      jnp.dot(pb, _cmp_operand(vc_ref[0, 1], npp), preferred_element_type=F32))

    kq = N_KV_HEADS * ds
    p_kv = []
    for kvh in range(N_KV_HEADS):
        acc = jnp.zeros((ds, ncols), F32)
        for g in range(Q_PER_KV):
            r0 = (kvh * Q_PER_KV + g) * ds
            acc = acc + p[r0:r0 + ds]
        p_kv.append(acc)
    p_kv = jnp.concatenate(p_kv, axis=0)
    blk2 = _block_of_col(lax.broadcasted_iota(jnp.int32, (kq, ncols), 1), npp)
    qp2 = past + lax.broadcasted_iota(jnp.int32, (kq, 1), 0) % ds
    cand = blk2 * CMP_BLOCK <= qp2
    forced = (blk2 == qp2 // CMP_BLOCK) | (blk2 == 0)
    score = jnp.where(forced, FORCE_SCORE, jnp.where(cand, p_kv, -1.0))
    rank = jnp.zeros((kq, ncols), jnp.int32)
    for n in range(n_blocks):
        c = (n % 2) * npp + n // 2
        col = score[:, c:c + 1]
        ahead = (col > score) | ((col == score) & (blk2 > n))
        rank = rank + ahead.astype(jnp.int32)
    sel2 = ((rank < TOP_K_BLOCKS) & (score > -0.5)).astype(F32)
    sel_rows = []
    for h in range(N_HEADS):
        kvh = h // Q_PER_KV
        sel_rows.append(sel2[kvh * ds:(kvh + 1) * ds])
    sel_rows.append(jnp.zeros((S_ROWS - n_real, ncols), F32))
    sel_rows = jnp.concatenate(sel_rows, axis=0)
    ocmp_ref[0] = o_cmp
    sel_ref[0] = sel_rows
    any_row = jnp.max(sel_rows, axis=0, keepdims=True)
    page_need = jnp.maximum(any_row[:, :npp], any_row[:, npp:])
    need_ref[0] = jnp.broadcast_to(page_need, (8, npp))


def _select_sample(q, kc, vc, *, past):
    b, ds, _ = q.shape
    npp = kc.shape[2]
    kern = functools.partial(_select_sample_kernel, past=past, ds=ds, npp=npp)
    per_b3 = lambda i: (i, 0, 0)
    per_b4 = lambda i: (i, 0, 0, 0)
    return pl.pallas_call(
        kern,
        grid=(b,),
        in_specs=[pl.BlockSpec((1, ds, ATTN_WIDTH), per_b3),
                  pl.BlockSpec((1, N_KV_HEADS, npp, LANES), per_b4),
                  pl.BlockSpec((1, N_KV_HEADS, npp, LANES), per_b4)],
        out_specs=(pl.BlockSpec((1, S_ROWS, LANES), per_b3), pl.BlockSpec((1, S_ROWS, 2 * npp), per_b3),
                   pl.BlockSpec((1, 8, npp), per_b3)),
        out_shape=(jax.ShapeDtypeStruct((b, S_ROWS, LANES), F32), jax.ShapeDtypeStruct((b, S_ROWS, 2 * npp), F32),
                   jax.ShapeDtypeStruct((b, 8, npp), F32)),
        compiler_params=pltpu.CompilerParams(dimension_semantics=("arbitrary",), vmem_limit_bytes=VMEM_LIMIT),
        name="select_sample",
    )(q, kc, vc)


def _attn_sample_kernel(pt_ref, order_ref, nn_ref, q_ref, gates_ref, ocmp_ref, sel_ref, kvs_new_ref, kvw_new_ref,
                        win_ref, cache_ref, a_ref, kbuf, zbuf, sem, *, n_pages, past, ds, npp):
    b = pl.program_id(0)
    nb = pl.num_programs(0)
    ncols = 2 * npp
    half_cols = KV_COLS // 2
    pages_per_chunk = S_CHUNK // PAGE

    def page_copy(row, j, slot):
        return pltpu.make_async_copy(
            cache_ref.at[pt_ref[row, order_ref[row, j]]], kbuf.at[slot, :, pl.ds(j * PAGE, PAGE)], sem.at[slot])

    def row_copies(row, slot, fn):
        def body(j, carry):
            fn(page_copy(row, j, slot))
            return carry
        lax.fori_loop(0, nn_ref[row], body, 0)

    @pl.when(b == 0)
    def _():
        kbuf[...] = jnp.zeros(kbuf.shape, F32)
        row_copies(0, 0, lambda cp: cp.start())

    slot = b % 2

    @pl.when(b + 1 < nb)
    def _():
        row_copies(b + 1, 1 - slot, lambda cp: cp.start())

    qrows, _, q_of_r, qp_r, sl_r, _ = _sample_rows(q_ref, ds, past)
    o_cmp = ocmp_ref[0]
    sel_rows = sel_ref[0]
    sel16 = sel_rows.astype(BF16)

    new_lane = lax.broadcasted_iota(jnp.int32, (1, LANES), 1)

    def new_scores(ref, extra_mask):
        k_new = ref[0, :half_cols, :].astype(BF16)
        s_new = jnp.dot(qrows, k_new, preferred_element_type=F32)
        dist = q_of_r - new_lane
        mask = (new_lane < ds) & (dist >= 0) & extra_mask
        return jnp.where(mask, s_new - sl_r * dist.astype(F32), NEG)

    def weighted_new(ref, e_new):
        return _dot_t(e_new.astype(BF16), ref[0, half_cols:, :].astype(BF16))

    row_copies(b, slot, lambda cp: cp.wait())
    n_need = nn_ref[b]
    n_chunks = (n_need + pages_per_chunk - 1) // pages_per_chunk

    def pass1(c, mrow):
        k0 = pl.multiple_of(c * S_CHUNK, S_CHUNK)
        kt = kbuf[slot, :half_cols, pl.ds(k0, S_CHUNK)].astype(BF16)
        st = jnp.dot(qrows, kt, preferred_element_type=F32)
        lane = lax.broadcasted_iota(jnp.int32, (1, S_CHUNK), 1)
        tpos = jnp.full((1, S_CHUNK), -PAGE * n_pages, jnp.int32)
        for jj in range(pages_per_chunk):
            j = c * pages_per_chunk + jj
            page = order_ref[b, jnp.minimum(j, n_pages - 1)]
            base = jnp.where(j < n_need, page * PAGE, -PAGE * n_pages) - jj * PAGE
            tpos = jnp.where(lane // PAGE == jj, base + lane, tpos)
        blk_c = _block_of_col(lax.broadcasted_iota(jnp.int32, (ncols, S_CHUNK), 0), npp)
        expand = (blk_c == jnp.broadcast_to(tpos, (ncols, S_CHUNK)) // CMP_BLOCK).astype(BF16)
        chosen = jnp.dot(sel16, expand, preferred_element_type=F32) > 0.5
        dist = qp_r - tpos
        zc = jnp.where(chosen & (dist >= 0), st - sl_r * dist.astype(F32), NEG)
        zbuf[c] = zc
        return jnp.maximum(mrow, jnp.max(zc, axis=-1, keepdims=True))

    mrow = lax.fori_loop(0, n_chunks, pass1, jnp.full((S_ROWS, 1), NEG, F32))
    c_last = ((past // CMP_BLOCK) % 2) * npp + (past // CMP_BLOCK) // 2
    z_new = new_scores(kvs_new_ref, sel_rows[:, c_last:c_last + 1] > 0.5)
    m_r = jnp.maximum(mrow, jnp.max(z_new, axis=-1, keepdims=True))

    def pass2(c, carry):
        acc, den_r = carry
        k0 = pl.multiple_of(c * S_CHUNK, S_CHUNK)
        zc = zbuf[c]
        ec = jnp.where(zc > 0.5 * NEG, jnp.exp(zc - m_r), 0.0)
        vt = kbuf[slot, half_cols:, pl.ds(k0, S_CHUNK)].astype(BF16)
        return acc + _dot_t(ec.astype(BF16), vt), den_r + jnp.sum(ec, axis=-1, keepdims=True)

    e_new = jnp.where(z_new > 0.5 * NEG, jnp.exp(z_new - m_r), 0.0)
    acc, den_r = lax.fori_loop(0, n_chunks, pass2,
                               (weighted_new(kvs_new_ref, e_new), jnp.sum(e_new, axis=-1, keepdims=True)))
    o_slc = acc / jnp.where(den_r > 0.0, den_r, 1.0)

    win_buf = win_ref.shape[2]
    kw = win_ref[0, :half_cols, :].astype(BF16)
    st = jnp.dot(qrows, kw, preferred_element_type=F32)
    kpos = past - win_buf + lax.broadcasted_iota(jnp.int32, (1, win_buf), 1)
    dist = qp_r - kpos
    mask = (dist >= 0) & (dist < WINDOW) & (kpos >= 0)
    z_w = jnp.where(mask, st - sl_r * dist.astype(F32), NEG)
    zw_new = new_scores(kvw_new_ref, True)
    m_w = jnp.maximum(jnp.max(z_w, axis=-1, keepdims=True), jnp.max(zw_new, axis=-1, keepdims=True))
    e_w = jnp.where(z_w > 0.5 * NEG, jnp.exp(z_w - m_w), 0.0)
    ew_new = jnp.where(zw_new > 0.5 * NEG, jnp.exp(zw_new - m_w), 0.0)
    acc_w = _dot_t(e_w.astype(BF16), win_ref[0, half_cols:, :].astype(BF16)) + weighted_new(kvw_new_ref, ew_new)
    den_w = jnp.sum(e_w, axis=-1, keepdims=True) + jnp.sum(ew_new, axis=-1, keepdims=True)
    o_win = acc_w / jnp.where(den_w > 0.0, den_w, 1.0)

    gates = gates_ref[0]
    for pair in range(N_HEADS // 2):
        kvh = (2 * pair) // Q_PER_KV
        e_sl = slice(2 * pair * ds, (2 * pair + 1) * ds)
        o_sl = slice((2 * pair + 1) * ds, (2 * pair + 2) * ds)
        shape = (ds, LANES)
        tile = (_gate_tile(gates, pair, 0, shape) * _pair_tile(o_cmp[e_sl], o_cmp[o_sl], 0)
                + _gate_tile(gates, pair, 1, shape) * _pair_tile(o_slc[e_sl], o_slc[o_sl], kvh)
                + _gate_tile(gates, pair, 2, shape) * _pair_tile(o_win[e_sl], o_win[o_sl], kvh))
        a_ref[0, :, pair * LANES:(pair + 1) * LANES] = tile.astype(BF16)


def _attn_sample(page_table, order, n_need, q, gates, o_cmp, sel_rows, kvs_new, kvw_new, state_win_t, cache_pages,
                 *, past):
    b, n_pages = page_table.shape
    ds = q.shape[1]
    npp = sel_rows.shape[2] // 2
    win_buf = state_win_t.shape[2]
    n_keys = n_pages * PAGE
    kern = functools.partial(_attn_sample_kernel, n_pages=n_pages, past=past, ds=ds, npp=npp)
    per_b = lambda i, pt, od, nn: (i, 0, 0)
    grid_spec = pltpu.PrefetchScalarGridSpec(
        num_scalar_prefetch=3,
        grid=(b,),
        in_specs=[
            pl.BlockSpec((1, ds, ATTN_WIDTH), per_b),
            pl.BlockSpec((1, ds, GATE_PAD), per_b),
            pl.BlockSpec((1, S_ROWS, LANES), per_b),
            pl.BlockSpec((1, S_ROWS, 2 * npp), per_b),
            pl.BlockSpec((1, KV_COLS, LANES), per_b),
            pl.BlockSpec((1, KV_COLS, LANES), per_b),
            pl.BlockSpec((1, KV_COLS, win_buf), per_b),
            pl.BlockSpec(memory_space=pl.ANY),
        ],
        out_specs=pl.BlockSpec((1, ds, ATTN_WIDTH), per_b),
        scratch_shapes=[
            pltpu.VMEM((2, KV_COLS, n_keys), F32),
            pltpu.VMEM((n_keys // S_CHUNK, S_ROWS, S_CHUNK), F32),
            pltpu.SemaphoreType.DMA((2,)),
        ],
    )
    return pl.pallas_call(
        kern,
        grid_spec=grid_spec,
        out_shape=jax.ShapeDtypeStruct((b, ds, ATTN_WIDTH), BF16),
        compiler_params=pltpu.CompilerParams(
            dimension_semantics=("arbitrary",), vmem_limit_bytes=VMEM_LIMIT),
        name="attn_sample",
    )(page_table, order, n_need, q, gates, o_cmp, sel_rows, kvs_new, kvw_new, state_win_t, cache_pages)


X_TILES = D_MODEL // LANES
PAY_SUB = X_TILES + 1
GROUP_LANE = 0
ROUTER_ROWS = 32
RANK_LANE = 1
PLAN_COLS = 8


def _rows_to_tiles(ref, lo, val):
    for k in range(val.shape[1] // LANES):
        ref[:, lo + k, :] = val[:, k * LANES:(k + 1) * LANES]


def _tiles_to_rows(ref, lo, n):
    return jnp.concatenate([ref[:, lo + k, :] for k in range(n)], axis=1)


def _ffn_kernel(x_ref, a_ref, m_ref, wo_ref, g2_ref, wr_ref, tri_ref, *refs, packed):
    out_refs, carry = refs[:-1], refs[-1]
    h = (x_ref[...]
         + jnp.dot(a_ref[...], wo_ref[:ATTN_WIDTH, :], preferred_element_type=F32)
         + jnp.dot(m_ref[...], wo_ref[ATTN_WIDTH:, :], preferred_element_type=F32))
    ms = jnp.mean(h * h, axis=-1, keepdims=True)
    n2f = (h * lax.rsqrt(ms + EPS)) * g2_ref[...]
    n2 = n2f.astype(BF16)
    tm = n2.shape[0]
    logits = _dot_t(wr_ref[...], n2)[:ROUTER_ROWS]
    row = lax.broadcasted_iota(jnp.int32, (ROUTER_ROWS, tm), 0)
    big = jnp.int32(LANES)

    def masked_softmax(mask):
        zz = jnp.where(mask, logits, NEG)
        mx = jnp.max(zz, axis=0, keepdims=True)
        ee = jnp.where(mask, jnp.exp(zz - mx), 0.0)
        return ee / jnp.sum(ee, axis=0, keepdims=True)

    def first_argmax(vals, mask):
        v = jnp.max(jnp.where(mask, vals, -1.0), axis=0, keepdims=True)
        idx = jnp.min(jnp.where(mask & (vals == v), row, big), axis=0, keepdims=True)
        return v, idx

    is_g = row < N_EXPERT_GROUPS
    pg = masked_softmax(is_g)
    g_val, g_idx = first_argmax(pg, is_g)
    e_row = row - ROUTER_LANE0
    in_grp = (e_row >= 0) & (e_row < N_EXPERTS) & (e_row // EXPERTS_PER_GROUP == g_idx)
    pe = masked_softmax(in_grp)
    v1, i1 = first_argmax(pe, in_grp)
    rest = in_grp & (row != i1)
    v2, i2 = first_argmax(pe, rest)
    scale = g_val / (v1 + v2)
    gate_t = jnp.where(row == i1, v1 * scale, jnp.where(row == i2, v2 * scale, 0.0))
    gate_t = jnp.where(row == GROUP_LANE, g_idx.astype(F32), gate_t)
    if packed:
        @pl.when(pl.program_id(0) == 0)
        def _():
            carry[...] = jnp.zeros(carry.shape, F32)

        onehot_t = jnp.where(is_g & (row == g_idx), 1.0, 0.0)
        before = jnp.dot(onehot_t.astype(BF16), tri_ref[...], preferred_element_type=F32) + carry[:, :1]
        rank = jnp.sum(onehot_t * before, axis=0, keepdims=True)
        gate_t = jnp.where(row == RANK_LANE, rank, gate_t)
        carry[...] = carry[...] + jnp.sum(onehot_t, axis=1, keepdims=True)
    gate = jnp.concatenate([gate_t, jnp.zeros((LANES - ROUTER_ROWS, tm), F32)], axis=0).T
    if packed:
        h_ref, pay_ref, plan_ref, cnt_ref = out_refs
        h_ref[...] = h
        plan_ref[...] = gate[:, :PLAN_COLS]
        cnt_ref[...] = carry[:8, :]
        _rows_to_tiles(pay_ref, 0, n2f)
        pay_ref[:, X_TILES, :] = gate
    else:
        h_ref, n2_ref, gate_ref = out_refs
        h_ref[...] = h
        n2_ref[...] = n2
        gate_ref[...] = gate


def _ffn(x2d, a2d, m2d, w_out16, g2, w_router16, tm, packed):
    t = x2d.shape[0]
    row = lambda i: (i, 0)
    const = lambda i: (0, 0)
    if packed:
        out_specs = (pl.BlockSpec((tm, D_MODEL), row), pl.BlockSpec((tm, PAY_SUB, LANES), lambda i: (i, 0, 0)),
                     pl.BlockSpec((tm, PLAN_COLS), row), pl.BlockSpec((8, LANES), const))
        out_shape = (jax.ShapeDtypeStruct((t, D_MODEL), F32), jax.ShapeDtypeStruct((t, PAY_SUB, LANES), F32),
                     jax.ShapeDtypeStruct((t, PLAN_COLS), F32), jax.ShapeDtypeStruct((8, LANES), F32))
    else:
        out_specs = (pl.BlockSpec((tm, D_MODEL), row), pl.BlockSpec((tm, D_MODEL), row),
                     pl.BlockSpec((tm, LANES), row))
        out_shape = (jax.ShapeDtypeStruct((t, D_MODEL), F32), jax.ShapeDtypeStruct((t, D_MODEL), BF16),
                     jax.ShapeDtypeStruct((t, LANES), F32))
    return pl.pallas_call(
        functools.partial(_ffn_kernel, packed=packed),
        grid=(t // tm,),
        in_specs=[
            pl.BlockSpec((tm, D_MODEL), row),
            pl.BlockSpec((tm, ATTN_WIDTH), row),
            pl.BlockSpec((tm, POOL_WIDTH), row),
            pl.BlockSpec((D_MODEL, D_MODEL), const),
            pl.BlockSpec((1, D_MODEL), const),
            pl.BlockSpec((LANES, D_MODEL), const),
            pl.BlockSpec((tm, tm), const),
        ],
        out_specs=out_specs,
        out_shape=out_shape,
        scratch_shapes=[pltpu.VMEM((ROUTER_ROWS, LANES), F32)],
        compiler_params=pltpu.CompilerParams(
            dimension_semantics=("arbitrary",), vmem_limit_bytes=VMEM_LIMIT),
        name="ffn",
    )(x2d, a2d, m2d, w_out16, g2, w_router16, jnp.asarray(np.triu(np.ones((tm, tm), np.float32), 1), dtype=BF16))


MOE_ROWS = 512
ROW_CHUNK = 1024
GATHER_CHUNK = 512

def _scatter_kernel(meta_ref, slot_ref, pay_ref, xs_ref, zrow, sem, zsem):
    c = pl.program_id(0)

    @pl.when(c == 0)
    def _():
        zrow[...] = jnp.zeros(zrow.shape, F32)
        for g in range(N_EXPERT_GROUPS + 1):
            lo = meta_ref[g]
            hi = meta_ref[N_EXPERT_GROUPS + 1 + g]

            def zero_start(r, carry):
                pltpu.make_async_copy(zrow.at[pl.ds(0, 1)], xs_ref.at[pl.ds(r, 1)], zsem.at[0]).start()
                return carry

            def zero_wait(r, carry):
                pltpu.make_async_copy(zrow.at[pl.ds(0, 1)], xs_ref.at[pl.ds(0, 1)], zsem.at[0]).wait()
                return carry

            lax.fori_loop(lo, hi, zero_start, 0)
            lax.fori_loop(lo, hi, zero_wait, 0)

    def issue(u, carry):
        pltpu.make_async_copy(pay_ref.at[pl.ds(u, 1)], xs_ref.at[pl.ds(slot_ref[0, 0, u], 1)], sem.at[0]).start()
        return carry

    lax.fori_loop(0, ROW_CHUNK, issue, 0, unroll=8)
    pltpu.make_async_copy(pay_ref, xs_ref.at[pl.ds(0, ROW_CHUNK)], sem.at[0]).wait()


def _scatter(meta, slot3, pay, n_slots):
    t = pay.shape[0]
    return pl.pallas_call(
        _scatter_kernel,
        grid_spec=pltpu.PrefetchScalarGridSpec(
            num_scalar_prefetch=1,
            grid=(t // ROW_CHUNK,),
            in_specs=[pl.BlockSpec((1, 1, ROW_CHUNK), lambda c, meta: (c, 0, 0), memory_space=pltpu.SMEM),
                      pl.BlockSpec((ROW_CHUNK, PAY_SUB, LANES), lambda c, meta: (c, 0, 0))],
            out_specs=pl.BlockSpec(memory_space=pl.ANY),
            scratch_shapes=[pltpu.VMEM((1, PAY_SUB, LANES), F32), pltpu.SemaphoreType.DMA((1,)),
                            pltpu.SemaphoreType.DMA((1,))],
        ),
        out_shape=jax.ShapeDtypeStruct((n_slots, PAY_SUB, LANES), F32),
        compiler_params=pltpu.CompilerParams(dimension_semantics=("arbitrary",), vmem_limit_bytes=VMEM_LIMIT),
        name="moe_scatter",
    )(meta, slot3, pay)


def _gather_kernel(slot_ref, nslot_ref, h_ref, ys_ref, y_ref, buf, sem):
    c = pl.program_id(0)
    nc = pl.num_programs(0)

    def fetch(ref, slot):
        def issue(u, carry):
            pltpu.make_async_copy(ys_ref.at[pl.ds(ref[0, 0, u], 1)], buf.at[slot, pl.ds(u, 1)], sem.at[slot]).start()
            return carry
        lax.fori_loop(0, GATHER_CHUNK, issue, 0, unroll=8)

    @pl.when(c == 0)
    def _():
        fetch(slot_ref, 0)

    @pl.when(c + 1 < nc)
    def _():
        fetch(nslot_ref, (c + 1) % 2)

    cur = c % 2
    pltpu.make_async_copy(ys_ref.at[pl.ds(0, GATHER_CHUNK)], buf.at[cur], sem.at[cur]).wait()
    y_ref[...] = h_ref[...] + jnp.concatenate([buf[cur, :, k, :] for k in range(X_TILES)], axis=1)


def _gather(slot, h, ys):
    t = h.shape[0]
    nc = t // GATHER_CHUNK
    slot3 = slot.reshape(nc, 1, GATHER_CHUNK)
    return pl.pallas_call(
        _gather_kernel,
        grid=(nc,),
        in_specs=[pl.BlockSpec((1, 1, GATHER_CHUNK), lambda c: (c, 0, 0), memory_space=pltpu.SMEM),
                  pl.BlockSpec((1, 1, GATHER_CHUNK), lambda c: (jnp.minimum(c + 1, nc - 1), 0, 0),
                               memory_space=pltpu.SMEM),
                  pl.BlockSpec((GATHER_CHUNK, D_MODEL), lambda c: (c, 0)),
                  pl.BlockSpec(memory_space=pl.ANY)],
        out_specs=pl.BlockSpec((GATHER_CHUNK, D_MODEL), lambda c: (c, 0)),
        out_shape=jax.ShapeDtypeStruct((t, D_MODEL), F32),
        scratch_shapes=[pltpu.VMEM((2, GATHER_CHUNK, X_TILES, LANES), F32), pltpu.SemaphoreType.DMA((2,))],
        compiler_params=pltpu.CompilerParams(dimension_semantics=("arbitrary",), vmem_limit_bytes=VMEM_LIMIT),
        name="moe_gather",
    )(slot3, slot3, h, ys)


def _group_moe_kernel(tg_ref, nu_ref, xs_ref, wg_ref, wu_ref, wd_ref, ys_ref):
    j = pl.program_id(0)

    @pl.when(j < nu_ref[0])
    def _():
        g = tg_ref[j]
        x = _tiles_to_rows(xs_ref, 0, X_TILES).astype(BF16)
        gate = xs_ref[:, X_TILES, :]
        lane = lax.broadcasted_iota(jnp.int32, gate.shape, 1)
        y = jnp.zeros((MOE_ROWS, D_MODEL), F32)
        for e in range(EXPERTS_PER_GROUP):
            gu = jnp.dot(x, wg_ref[e], preferred_element_type=F32)
            up = jnp.dot(x, wu_ref[e], preferred_element_type=F32)
            he = (gu * (1.0 / (1.0 + jnp.exp(-gu)))) * up
            out = jnp.dot(he.astype(BF16), wd_ref[e], preferred_element_type=F32)
            col = ROUTER_LANE0 + g * EXPERTS_PER_GROUP + e
            y = y + jnp.sum(jnp.where(lane == col, gate, 0.0), axis=-1, keepdims=True) * out
        _rows_to_tiles(ys_ref, 0, y)

    @pl.when(j >= nu_ref[0])
    def _():
        ys_ref[...] = jnp.zeros(ys_ref.shape, F32)


def _group_moe(tile_group, n_used, xs, wg16, wu16, wd16):
    n_slots = xs.shape[0]
    wspec = lambda shape: pl.BlockSpec((EXPERTS_PER_GROUP,) + shape, lambda j, tg, nu: (tg[j], 0, 0))
    return pl.pallas_call(
        _group_moe_kernel,
        grid_spec=pltpu.PrefetchScalarGridSpec(
            num_scalar_prefetch=2,
            grid=(n_slots // MOE_ROWS,),
            in_specs=[pl.BlockSpec((MOE_ROWS, PAY_SUB, LANES), lambda j, tg, nu: (j, 0, 0)),
                      wspec((D_MODEL, D_EXPERT)), wspec((D_MODEL, D_EXPERT)), wspec((D_EXPERT, D_MODEL))],
            out_specs=pl.BlockSpec((MOE_ROWS, X_TILES, LANES), lambda j, tg, nu: (j, 0, 0)),
        ),
        out_shape=jax.ShapeDtypeStruct((n_slots, X_TILES, LANES), F32),
        compiler_params=pltpu.CompilerParams(dimension_semantics=("arbitrary",), vmem_limit_bytes=VMEM_LIMIT),
        name="moe_group",
    )(tile_group, n_used, xs, wg16, wu16, wd16)


def _routed_moe(h, pay, plan, counts, wg16, wu16, wd16):
    t = pay.shape[0]
    n_slots = t + N_EXPERT_GROUPS * MOE_ROWS
    n_tiles = n_slots // MOE_ROWS
    gid = plan[:, GROUP_LANE].astype(jnp.int32)
    rank = plan[:, RANK_LANE].astype(jnp.int32)
    cnt = counts[:N_EXPERT_GROUPS, 0].astype(jnp.int32)
    padded = -(-cnt // MOE_ROWS) * MOE_ROWS
    ends = jnp.cumsum(padded)
    off = ends - padded
    slot = (off[gid] + rank).reshape(t // ROW_CHUNK, 1, ROW_CHUNK)
    tile_start = jnp.arange(n_tiles, dtype=jnp.int32) * MOE_ROWS
    tile_group = jnp.minimum(jnp.sum(tile_start[:, None] >= ends[None, :], axis=1), N_EXPERT_GROUPS - 1)
    n_used = (ends[-1] // MOE_ROWS).reshape(1)
    meta = jnp.concatenate([off + cnt, ends[-1:], ends, jnp.full((1,), n_slots, jnp.int32)]).astype(jnp.int32)
    xs = _scatter(meta, slot, pay, n_slots)
    ys = _group_moe(tile_group.astype(jnp.int32), n_used.astype(jnp.int32), xs, wg16, wu16, wd16)
    return _gather(slot, h, ys)


def _moe_kernel(n2_ref, gate_ref, h_ref, wg_ref, wu_ref, wd_ref, y_ref):
    e = pl.program_id(1)

    @pl.when(e == 0)
    def _():
        y_ref[...] = h_ref[...]

    n2 = n2_ref[...]
    gu = jnp.dot(n2, wg_ref[0], preferred_element_type=F32)
    up = jnp.dot(n2, wu_ref[0], preferred_element_type=F32)
    he = (gu * (1.0 / (1.0 + jnp.exp(-gu)))) * up
    out = jnp.dot(he.astype(BF16), wd_ref[0], preferred_element_type=F32)
    lane = lax.broadcasted_iota(jnp.int32, gate_ref.shape, 1)
    gcol = jnp.sum(jnp.where(lane == e + ROUTER_LANE0, gate_ref[...], 0.0), axis=-1, keepdims=True)
    y_ref[...] += gcol * out


def _moe(n2, gate, h, wg16, wu16, wd16, tm):
    t = n2.shape[0]
    row = lambda i, e: (i, 0)
    return pl.pallas_call(
        _moe_kernel,
        grid=(t // tm, N_EXPERTS),
        in_specs=[
            pl.BlockSpec((tm, D_MODEL), row),
            pl.BlockSpec((tm, LANES), row),
            pl.BlockSpec((tm, D_MODEL), row),
            pl.BlockSpec((1, D_MODEL, D_EXPERT), lambda i, e: (e, 0, 0)),
            pl.BlockSpec((1, D_MODEL, D_EXPERT), lambda i, e: (e, 0, 0)),
            pl.BlockSpec((1, D_EXPERT, D_MODEL), lambda i, e: (e, 0, 0)),
        ],
        out_specs=pl.BlockSpec((tm, D_MODEL), row),
        out_shape=jax.ShapeDtypeStruct((t, D_MODEL), F32),
        compiler_params=pltpu.CompilerParams(
            dimension_semantics=("arbitrary", "arbitrary"), vmem_limit_bytes=VMEM_LIMIT),
        name="moe",
    )(n2, gate, h, wg16, wu16, wd16)


PROJ_TILE = 512
MOE_TILE = 1024
PROMPT_NPP = LANES // 2
SAMPLE_NPP = LANES


def _round_up(n, mult):
    return -(-n // mult) * mult


def _feature_major(x5):
    b, t = x5.shape[:2]
    return jnp.transpose(x5, (0, 2, 3, 4, 1)).reshape(b, KV_COLS, t)


def _token_major(xt):
    b, _, t = xt.shape
    return jnp.transpose(xt.reshape(b, 2, N_KV_HEADS, HEAD_DIM, t), (0, 4, 1, 2, 3))


def _pad_pages(x, npp_from, npp_to):
    b = x.shape[0]
    x = x.reshape(b, N_KV_HEADS, npp_from, LANES)
    return jnp.concatenate([x, jnp.zeros((b, N_KV_HEADS, npp_to - npp_from, LANES), x.dtype)], axis=2)


def _expand_const(n_chunks, npp):
    col = np.arange(2 * npp)
    blk = 2 * (col % npp) + col // npp
    tok_blk = np.arange(n_chunks * K_CHUNK) // CMP_BLOCK
    e = (blk[None, :, None] == tok_blk.reshape(n_chunks, 1, K_CHUNK)).astype(np.float32)
    return jnp.asarray(e, dtype=BF16)


def _gate_place_const():
    gp = np.zeros((GATE_PAD, 3 * ATTN_WIDTH), np.float32)
    for h in range(N_HEADS):
        for c in range(3):
            gp[3 * h + c, c * ATTN_WIDTH + h * HEAD_DIM:c * ATTN_WIDTH + (h + 1) * HEAD_DIM] = 1.0
    return jnp.asarray(gp, dtype=BF16)


def kernel(x_prompt, x_sample, cache_cmp_kv, cache_slc_kv, state_win_kv, state_pool, page_table, norm1_g, w_in, q_norm_g, k_norm_cmp_g, k_norm_slc_g, k_norm_win_g, cmp_pos_emb, w_cmp_k, w_cmp_v, pool_w, pool_scale, w_out, norm2_g, w_router_group, w_router_expert, w_gate, w_up, w_down):
    b, s, _ = x_prompt.shape
    db, ds, _ = x_sample.shape
    n_pool, page_rows = cache_cmp_kv.shape[:2]
    assert page_rows == PAGE and s % PAGE == 0
    past = page_table.shape[1] * page_rows

    kv0 = ATTN_WIDTH
    kv1 = kv0 + KV_ROWS
    w_row = jnp.concatenate(
        [w_in[:, :kv0], w_in[:, kv1:kv1 + GATE_COLS], jnp.zeros((D_MODEL, GATE_PAD - GATE_COLS), w_in.dtype),
         w_in[:, kv1 + GATE_COLS:]], axis=1).astype(BF16)
    w_kv = w_in[:, kv0:kv1].T.astype(BF16)
    g1 = norm1_g[None, :]
    g2 = norm2_g[None, :]
    two = lambda g: jnp.tile(g, 2)[None, :]
    kg = jnp.stack([k_norm_slc_g, k_norm_win_g])[:, :, None]
    pos_t = jnp.tile(cmp_pos_emb.T, (1, 2))
    zb = jnp.zeros((HEAD_DIM, CMP_BLOCK, HEAD_DIM), F32)

    def blockdiag(w):
        wt = jnp.transpose(w, (1, 0, 2))
        return jnp.concatenate([jnp.concatenate([wt, zb], axis=2), jnp.concatenate([zb, wt], axis=2)], axis=1)

    bdk, bdv = blockdiag(w_cmp_k), blockdiag(w_cmp_v)
    zq = jnp.zeros_like(bdk)
    bd = jnp.concatenate([jnp.concatenate([bdk, zq], axis=2), jnp.concatenate([zq, bdv], axis=2)],
                         axis=1).astype(BF16)
    w_router = jnp.concatenate(
        [w_router_group, w_router_expert,
         jnp.zeros((D_MODEL, LANES - N_EXPERT_GROUPS - N_EXPERTS), w_router_group.dtype)], axis=1).T.astype(BF16)
    w_out16 = w_out.astype(BF16)
    pool_w16 = pool_w.astype(BF16)
    wg16, wu16, wd16 = w_gate.astype(BF16), w_up.astype(BF16), w_down.astype(BF16)
    ps = pool_scale[None, :]

    def ffn_moe(x2d, a2d, m2d, tm_ffn, tm_moe, routed):
        if routed:
            return _routed_moe(*_ffn(x2d, a2d, m2d, w_out16, g2, w_router, tm_ffn, True), wg16, wu16, wd16)
        h, n2, gate = _ffn(x2d, a2d, m2d, w_out16, g2, w_router, tm_ffn, False)
        return _moe(n2, gate, h, wg16, wu16, wd16, tm_moe)

    q, gates, u, kvc_t, kvs_t, kvw_t, ks16, vsx, kw16, vwx = _proj(
        x_prompt, g1, w_row, w_kv, two(q_norm_g), kg, PROJ_TILE)
    m_p = _pool(u, pool_w16, ps, s)
    pages_p = s // PAGE
    pt_p = jnp.zeros((b, pages_p), jnp.int32)
    dummy_tail = jnp.zeros((b, KV_COLS, LANES), F32)
    kc_p, vc_p = _compress(pt_p, kvc_t, dummy_tail, pos_t, bd, two(k_norm_cmp_g),
                           npp=pages_p, has_tail=False, paged=False)
    kc_p = _pad_pages(kc_p, pages_p, PROMPT_NPP)
    vc_p = _pad_pages(vc_p, pages_p, PROMPT_NPP)
    a_p = _attn_prompt(q, gates, kc_p, vc_p, ks16, vsx, kw16, vwx, _expand_const(s // K_CHUNK, PROMPT_NPP),
                       _gate_place_const(), pages_p)
    y_p = ffn_moe(x_prompt.reshape(b * s, D_MODEL), a_p.reshape(b * s, ATTN_WIDTH),
                  m_p.reshape(b * s, POOL_WIDTH), PROJ_TILE, MOE_TILE, True)

    ts = db * ds
    q_s, gates_s, u_s, kvc_st, kvs_st, kvw_st, _, _, _, _ = _proj(
        x_sample.reshape(1, ts, D_MODEL), g1, w_row, w_kv, two(q_norm_g), kg, ts)
    u_s = u_s.reshape(db, ds, POOL_WIDTH)
    pool_ext = jnp.concatenate([state_pool, u_s], axis=1)
    n_ext = pool_ext.shape[1]
    lead = _round_up(n_ext, 8) - n_ext
    pool_in = jnp.concatenate([jnp.zeros((db, lead, POOL_WIDTH), F32), pool_ext], axis=1)
    n_in = lead + n_ext
    m_s = _pool(pool_in.reshape(1, db * n_in, POOL_WIDTH), pool_w16, ps, n_in).reshape(db, n_in, POOL_WIDTH)[:, n_in - ds:]

    def new_rows_t(xt):
        x = jnp.transpose(xt.reshape(KV_COLS, db, ds), (1, 0, 2))
        return jnp.concatenate([x, jnp.zeros((db, KV_COLS, LANES - ds), F32)], axis=2)

    kvc_new, kvs_new, kvw_new = new_rows_t(kvc_st), new_rows_t(kvs_st), new_rows_t(kvw_st)
    n_pages = page_table.shape[1]
    npp_c = _round_up(n_pages + 1, 8)
    cmp_pages = _feature_major(cache_cmp_kv)
    slc_pages = _feature_major(cache_slc_kv)
    kc_s, vc_s = _compress(page_table, cmp_pages, kvc_new, pos_t, bd, two(k_norm_cmp_g),
                           npp=npp_c, has_tail=True, paged=True)
    kc_s = _pad_pages(kc_s, npp_c, SAMPLE_NPP)
    vc_s = _pad_pages(vc_s, npp_c, SAMPLE_NPP)
    win_t = _feature_major(state_win_kv)
    q_s3 = q_s.reshape(db, ds, ATTN_WIDTH)
    o_cmp_s, sel_s, need_s = _select_sample(q_s3, kc_s, vc_s, past=past)
    need = need_s[:, 0, :n_pages] > 0.5
    order = jnp.argsort(jnp.logical_not(need), axis=1, stable=True).astype(jnp.int32)
    n_need = jnp.sum(need, axis=1).astype(jnp.int32)
    a_s = _attn_sample(page_table, order, n_need, q_s3, gates_s.reshape(db, ds, GATE_PAD), o_cmp_s, sel_s,
                       kvs_new, kvw_new, win_t, slc_pages, past=past)
    y_s = ffn_moe(x_sample.reshape(ts, D_MODEL), a_s.reshape(ts, ATTN_WIDTH), m_s.reshape(ts, POOL_WIDTH), ts, ts, False)

    win_keep = min(WINDOW, s)
    sample5 = lambda xt: jnp.transpose(xt.reshape(2, N_KV_HEADS, HEAD_DIM, db, ds), (3, 4, 0, 1, 2))
    win_ctx_t = jnp.concatenate([win_t, kvw_new[:, :, :ds]], axis=2)
    return (y_p.reshape(b, s, D_MODEL), y_s.reshape(db, ds, D_MODEL),
            _token_major(kvc_t), sample5(kvc_st),
            _token_major(kvs_t), sample5(kvs_st),
            _token_major(kvw_t[:, :, s - win_keep:]), _token_major(win_ctx_t[:, :, ds:]),
            u[:, s - POOL_BUF:], pool_ext[:, ds:])
```
